```python
import math
import jax, jax.numpy as jnp
from jax import lax
import numpy as np

D_MODEL = 1024
BATCH = 8
SEQ = 8192
DEPTH = 4

N_MIXERS = 4
DN_ALPHA = (2.0 * DEPTH) ** 0.25
DN_BETA = (8.0 * DEPTH) ** -0.25
LN_EPS = 1e-5

GDN_HEADS = 8
GDN_DK = D_MODEL // GDN_HEADS
GDN_DV = D_MODEL // GDN_HEADS
GDN_CONV = 4
GDN_CHUNK = 64
RET_HEADS = 4
RET_DK = D_MODEL // RET_HEADS
RET_DV = 2 * D_MODEL // RET_HEADS
RET_CHUNK = 128
RET_ROPE_BASE = 10000.0
GMLP_CHUNK = 128
GMLP_WIDTH = 2 * D_MODEL
GMLP_GROUPS = 8
SB_HEADS = 16
SB_DH = D_MODEL // SB_HEADS
SB_BLOCK = 128
FFN_HIDDEN = ((8 * D_MODEL // 3 + 255) // 256) * 256
FFN_CONV = 3

kernel_name = 'hybrid_interleaved_gdn_ret_gmlp_sb_trunk'

F32 = jnp.float32


def _standardize(x, eps):
    xf = x.astype(F32)
    mu = jnp.mean(xf, axis=-1, keepdims=True)
    xc = xf - mu
    var = jnp.mean(xc * xc, axis=-1, keepdims=True)
    return xc * lax.rsqrt(var + eps)


def layer_norm(x, g, b):
    return (_standardize(x, LN_EPS) * g.astype(F32) + b.astype(F32)).astype(x.dtype)


def _l2norm(x, eps=1e-6):
    return x * lax.rsqrt(jnp.sum(x * x, axis=-1, keepdims=True) + eps)


def causal_dwconv(x, w):
    k_w = w.shape[0]
    s = x.shape[1]
    xp = jnp.pad(x, ((0, 0), (k_w - 1, 0), (0, 0)))
    y = xp[:, k_w - 1:k_w - 1 + s] * w[k_w - 1]
    for j in range(k_w - 1):
        y = y + xp[:, j:j + s] * w[j]
    return y


def _chunk_heads(t, n_heads, chunk):
    b, s, hd = t.shape
    return t.astype(F32).reshape(b, s // chunk, chunk, n_heads, hd // n_heads).transpose(0, 3, 1, 2, 4)


def _unchunk_heads(t):
    b, h, n, c, d = t.shape
    return t.transpose(0, 2, 3, 1, 4).reshape(b, n * c, h, d)


def gated_deltanet(h, w_in, conv_w, a_log, dt_bias, norm_w, w_out):
    H, dk, dv, C = GDN_HEADS, GDN_DK, GDN_DV, GDN_CHUNK
    b_, s, _ = h.shape
    n_qkv = 2 * H * dk + H * dv
    proj = h @ w_in
    qkv, z, a, bt = jnp.split(proj, [n_qkv, n_qkv + H * dv, n_qkv + H * dv + H], axis=-1)
    qkv = jax.nn.silu(causal_dwconv(qkv, conv_w))
    q, k, v = jnp.split(qkv, [H * dk, 2 * H * dk], axis=-1)
    q = _l2norm(_chunk_heads(q, H, C)) * (dk ** -0.5)
    k = _l2norm(_chunk_heads(k, H, C))
    v = _chunk_heads(v, H, C)
    beta = jax.nn.sigmoid(_chunk_heads(bt, H, C)[..., 0])
    g = -jnp.exp(a_log.astype(F32))[:, None, None] * jax.nn.softplus(
        _chunk_heads(a, H, C)[..., 0] + dt_bias.astype(F32)[:, None, None])
    gc = jnp.cumsum(g, axis=-1)
    idx = jnp.arange(C)
    causal = idx[:, None] >= idx[None, :]
    strict = idx[:, None] > idx[None, :]
    diff = gc[..., :, None] - gc[..., None, :]
    decay = jnp.where(causal, jnp.exp(jnp.where(causal, diff, 0.0)), 0.0)
    kb = k * beta[..., None]
    kk = jnp.where(strict, jnp.einsum('bhncd,bhnmd->bhncm', kb, k) * decay, 0.0)
    eye = jnp.eye(C, dtype=F32)
    rhs = jnp.concatenate([v * beta[..., None], kb * jnp.exp(gc)[..., None]], axis=-1)
    sol = lax.linalg.triangular_solve(kk + eye, rhs, left_side=True, lower=True, unit_diagonal=True)
    u, w = sol[..., :dv], sol[..., dv:]
    qk = jnp.where(causal, jnp.einsum('bhncd,bhnmd->bhncm', q, k) * decay, 0.0)

    def step(state, inp):
        q_n, k_n, u_n, w_n, qk_n, g_n = inp
        v_new = u_n - jnp.einsum('bhck,bhkv->bhcv', w_n, state)
        o = (jnp.einsum('bhck,bhkv->bhcv', q_n * jnp.exp(g_n)[..., None], state)
             + jnp.einsum('bhcm,bhmv->bhcv', qk_n, v_new))
        g_last = g_n[..., -1:]
        state = (state * jnp.exp(g_last)[..., None]
                 + jnp.einsum('bhck,bhcv->bhkv', k_n * jnp.exp(g_last - g_n)[..., None], v_new))
        return state, o

    xs = tuple(jnp.moveaxis(t, 2, 0) for t in (q, k, u, w, qk, gc))
    state0 = jnp.zeros((b_, H, dk, dv), F32)
    _, o = lax.scan(step, state0, xs)
    o = _unchunk_heads(jnp.moveaxis(o, 0, 2))
    o = o * lax.rsqrt(jnp.mean(o * o, axis=-1, keepdims=True) + 1e-6) * norm_w.astype(F32)
    o = o * jax.nn.silu(z.astype(F32).reshape(b_, s, H, dv))
    return o.reshape(b_, s, H * dv).astype(h.dtype) @ w_out


def retention(h, w_in, w_out):
    H, dk, dv, C = RET_HEADS, RET_DK, RET_DV, RET_CHUNK
    b_, s, _ = h.shape
    q, k, v, gate = jnp.split(h @ w_in, [H * dk, 2 * H * dk, 2 * H * dk + H * dv], axis=-1)
    pos = jnp.arange(s, dtype=F32)
    inv_freq = RET_ROPE_BASE ** (-jnp.linspace(0.0, 1.0, dk // 2, dtype=F32))
    ang = pos[:, None] * inv_freq[None, :]
    cos_a, sin_a = jnp.cos(ang)[:, None, :], jnp.sin(ang)[:, None, :]

    def rot(t):
        t = t.astype(F32).reshape(b_, s, H, dk)
        t1, t2 = t[..., :dk // 2], t[..., dk // 2:]
        return jnp.concatenate([t1 * cos_a - t2 * sin_a, t1 * sin_a + t2 * cos_a], axis=-1).reshape(b_, s, H * dk)

    q = _chunk_heads(rot(q), H, C)
    k = _chunk_heads(rot(k), H, C) * (dk ** -0.5)
    v = _chunk_heads(v, H, C)
    log_gamma = jnp.log(1.0 - jnp.power(2.0, -5.0 - jnp.arange(H, dtype=F32)))
    idx = jnp.arange(C, dtype=F32)
    rel = idx[:, None] - idx[None, :]
    dmask = jnp.where(rel >= 0, jnp.exp(jnp.maximum(rel, 0.0) * log_gamma[:, None, None]), 0.0)
    scores = jnp.einsum('bhncd,bhnmd->bhncm', q, k) * dmask[None, :, None]
    intra = jnp.einsum('bhncm,bhnmv->bhncv', scores, v)
    zeta = jnp.exp((C - 1.0 - idx)[None, :] * log_gamma[:, None])
    xi = jnp.exp((idx + 1.0)[None, :] * log_gamma[:, None])
    gamma_c = jnp.exp(C * log_gamma)

    def step(state, inp):
        q_n, k_n, v_n = inp
        o = jnp.einsum('bhck,bhkv->bhcv', q_n, state) * xi[None, :, :, None]
        state = (state * gamma_c[None, :, None, None]
                 + jnp.einsum('bhck,bhcv->bhkv', k_n * zeta[None, :, :, None], v_n))
        return state, o

    xs = tuple(jnp.moveaxis(t, 2, 0) for t in (q, k, v))
    _, inter = lax.scan(step, jnp.zeros((b_, H, dk, dv), F32), xs)
    o = _unchunk_heads(intra + jnp.moveaxis(inter, 0, 2))
    o = _standardize(o, 1e-6).reshape(b_, s, H * dv)
    o = o * jax.nn.silu(gate.astype(F32))
    return o.astype(h.dtype) @ w_out


def chunked_gmlp(h, w_in, ln_g, ln_b, w_s, b_s, w_out):
    C, G, W = GMLP_CHUNK, GMLP_GROUPS, GMLP_WIDTH
    b_, s, _ = h.shape
    u, v = jnp.split(jax.nn.gelu(h @ w_in, approximate=False), 2, axis=-1)
    v = layer_norm(v, ln_g, ln_b).reshape(b_, s // C, C, G, W // G)
    causal = jnp.tril(jnp.ones((C, C), dtype=bool))
    ws = jnp.where(causal, w_s, 0.0).astype(v.dtype)
    vs = jnp.einsum('gts,bnsgd->bntgd', ws, v) + b_s.T.astype(v.dtype)[None, None, :, :, None]
    return (u * vs.reshape(b_, s, W)) @ w_out


def stick_breaking(h, w_in, w_out):
    H, dh, T = SB_HEADS, SB_DH, SB_BLOCK
    b_, s, _ = h.shape
    nb = s // T
    q, k, v = jnp.split(h @ w_in, 3, axis=-1)
    q, k, v = (t.reshape(b_, s, H, dh).transpose(0, 2, 1, 3) for t in (q, k, v))
    qb = q.reshape(b_, H, nb, T, dh).transpose(2, 0, 1, 3, 4)
    key_pos = jnp.arange(s)
    scale = dh ** -0.5

    def block(args):
        q_blk, blk = args
        z = jnp.einsum('bhtd,bhsd->bhts', q_blk, k).astype(F32) * scale
        q_pos = blk * T + jnp.arange(T)
        strict = key_pos[None, :] < q_pos[:, None]
        log_1mb = jnp.where(strict, jax.nn.log_sigmoid(-z), 0.0)
        after = lax.cumsum(log_1mb, axis=3, reverse=True) - log_1mb
        a = jnp.where(strict, jnp.exp(jax.nn.log_sigmoid(z) + after), 0.0)
        return jnp.einsum('bhts,bhsd->bhtd', a.astype(v.dtype), v)

    o = lax.map(block, (qb, jnp.arange(nb)))
    o = o.transpose(1, 0, 3, 2, 4).reshape(b_, s, H * dh)
    return o @ w_out


def conv_ffn(h, w_up, conv_w, conv_b, w_down):
    gate, up = jnp.split(h @ w_up, 2, axis=-1)
    gate = causal_dwconv(gate, conv_w) + conv_b
    return (jax.nn.silu(gate) * up) @ w_down


def _fwd_setup_inputs(seed: int = 0) -> dict:
    key = jax.random.key(seed)
    ks = jax.random.split(key, 32)
    D, F = D_MODEL, FFN_HIDDEN

    def nrm(i, shape, scale):
        return jax.random.normal(ks[i], shape, F32) * scale

    gdn_qkv = 2 * GDN_HEADS * GDN_DK + GDN_HEADS * GDN_DV
    gdn_in = gdn_qkv + GDN_HEADS * GDN_DV + 2 * GDN_HEADS
    ret_in = 2 * RET_HEADS * RET_DK + 2 * RET_HEADS * RET_DV
    a_vals = jax.random.uniform(ks[14], (GDN_HEADS,), F32, 1.0, 16.0)
    dt = jnp.exp(jax.random.uniform(ks[15], (GDN_HEADS,), F32, math.log(1e-3), math.log(1e-1)))
    return {
        'x': nrm(0, (BATCH, SEQ, D), 1.0),
        'c': nrm(1, (BATCH, D), 1.0),
        'cond_w': nrm(2, (D, D), D ** -0.5),
        'cond_b': nrm(3, (D,), 0.01),
        'ada_w': nrm(4, (DEPTH, D, 6 * D), 0.1 * D ** -0.5),
        'ada_b': nrm(5, (DEPTH, 6 * D), 0.01),
        'ln_g': 1.0 + nrm(6, (DEPTH, 2, D), 0.02),
        'ln_b': nrm(7, (DEPTH, 2, D), 0.02),
        'ffn_up': nrm(8, (DEPTH, D, 2 * F), D ** -0.5),
        'ffn_conv_w': nrm(9, (DEPTH, FFN_CONV, F), FFN_CONV ** -0.5),
        'ffn_conv_b': nrm(10, (DEPTH, F), 0.01),
        'ffn_down': nrm(11, (DEPTH, F, D), DN_BETA * F ** -0.5),
        'gdn_w_in': nrm(12, (D, gdn_in), D ** -0.5),
        'gdn_conv_w': nrm(13, (GDN_CONV, gdn_qkv), GDN_CONV ** -0.5),
        'gdn_a_log': jnp.log(a_vals),
        'gdn_dt_bias': dt + jnp.log(-jnp.expm1(-dt)),
        'gdn_norm_w': 1.0 + nrm(16, (GDN_DV,), 0.02),
        'gdn_w_out': nrm(17, (GDN_HEADS * GDN_DV, D), DN_BETA * (GDN_HEADS * GDN_DV) ** -0.5),
        'ret_w_in': nrm(18, (D, ret_in), D ** -0.5),
        'ret_w_out': nrm(19, (RET_HEADS * RET_DV, D), DN_BETA * (RET_HEADS * RET_DV) ** -0.5),
        'gmlp_w_in': nrm(20, (D, 2 * GMLP_WIDTH), D ** -0.5),
        'gmlp_ln_g': 1.0 + nrm(21, (GMLP_WIDTH,), 0.02),
        'gmlp_ln_b': nrm(22, (GMLP_WIDTH,), 0.02),
        'gmlp_w_s': nrm(23, (GMLP_GROUPS, GMLP_CHUNK, GMLP_CHUNK), GMLP_CHUNK ** -0.5),
        'gmlp_b_s': 1.0 + nrm(24, (GMLP_GROUPS, GMLP_CHUNK), 0.01),
        'gmlp_w_out': nrm(25, (GMLP_WIDTH, D), DN_BETA * GMLP_WIDTH ** -0.5),
        'sb_w_in': nrm(26, (D, 3 * D), D ** -0.5),
        'sb_w_out': nrm(27, (D, D), DN_BETA * D ** -0.5),
    }


def _fwd_reference(x, c, cond_w, cond_b, ada_w, ada_b, ln_g, ln_b, ffn_up, ffn_conv_w, ffn_conv_b, ffn_down,
              gdn_w_in, gdn_conv_w, gdn_a_log, gdn_dt_bias, gdn_norm_w, gdn_w_out,
              ret_w_in, ret_w_out,
              gmlp_w_in, gmlp_ln_g, gmlp_ln_b, gmlp_w_s, gmlp_b_s, gmlp_w_out,
              sb_w_in, sb_w_out):
    mixers = (
        lambda t: gated_deltanet(t, gdn_w_in, gdn_conv_w, gdn_a_log, gdn_dt_bias, gdn_norm_w, gdn_w_out),
        lambda t: retention(t, ret_w_in, ret_w_out),
        lambda t: chunked_gmlp(t, gmlp_w_in, gmlp_ln_g, gmlp_ln_b, gmlp_w_s, gmlp_b_s, gmlp_w_out),
        lambda t: stick_breaking(t, sb_w_in, sb_w_out),
    )
    e = jax.nn.silu(c @ cond_w + cond_b)
    for i in range(DEPTH):
        mod = (e @ ada_w[i] + ada_b[i])[:, None, :]
        sh1, sc1, g1, sh2, sc2, g2 = jnp.split(mod, 6, axis=-1)
        y = mixers[i % N_MIXERS](x * (1.0 + sc1) + sh1)
        x = layer_norm(DN_ALPHA * x + (1.0 + g1) * y, ln_g[i, 0], ln_b[i, 0])
        y = conv_ffn(x * (1.0 + sc2) + sh2, ffn_up[i], ffn_conv_w[i], ffn_conv_b[i], ffn_down[i])
        x = layer_norm(DN_ALPHA * x + (1.0 + g2) * y, ln_g[i, 1], ln_b[i, 1])
    return x


import jax as _jax
import jax.numpy as _jnp

TWIN_FORMAT = 'train_step'
FWD_PARAMS = ['x', 'c', 'cond_w', 'cond_b', 'ada_w', 'ada_b', 'ln_g', 'ln_b', 'ffn_up', 'ffn_conv_w', 'ffn_conv_b', 'ffn_down', 'gdn_w_in', 'gdn_conv_w', 'gdn_a_log', 'gdn_dt_bias', 'gdn_norm_w', 'gdn_w_out', 'ret_w_in', 'ret_w_out', 'gmlp_w_in', 'gmlp_ln_g', 'gmlp_ln_b', 'gmlp_w_s', 'gmlp_b_s', 'gmlp_w_out', 'sb_w_in', 'sb_w_out']
TWIN_WEIGHTS = ['cond_w', 'cond_b', 'ada_w', 'ada_b', 'ln_g', 'ln_b', 'ffn_up', 'ffn_conv_w', 'ffn_conv_b', 'ffn_down', 'gdn_w_in', 'gdn_conv_w', 'gdn_a_log', 'gdn_dt_bias', 'gdn_norm_w', 'gdn_w_out', 'ret_w_in', 'ret_w_out', 'gmlp_w_in', 'gmlp_ln_g', 'gmlp_ln_b', 'gmlp_w_s', 'gmlp_b_s', 'gmlp_w_out', 'sb_w_in', 'sb_w_out']
TWIN_DIFF_INPUT = 'x'
TWIN_INPUTS = ['x', 'c', 'cond_w', 'cond_b', 'ada_w', 'ada_b', 'ln_g', 'ln_b', 'ffn_up', 'ffn_conv_w', 'ffn_conv_b', 'ffn_down', 'gdn_w_in', 'gdn_conv_w', 'gdn_a_log', 'gdn_dt_bias', 'gdn_norm_w', 'gdn_w_out', 'ret_w_in', 'ret_w_out', 'gmlp_w_in', 'gmlp_ln_g', 'gmlp_ln_b', 'gmlp_w_s', 'gmlp_b_s', 'gmlp_w_out', 'sb_w_in', 'sb_w_out', 'loss_target', 'm_cond_w', 'm_cond_b', 'm_ada_w', 'm_ada_b', 'm_ln_g', 'm_ln_b', 'm_ffn_up', 'm_ffn_conv_w', 'm_ffn_conv_b', 'm_ffn_down', 'm_gdn_w_in', 'm_gdn_conv_w', 'm_gdn_a_log', 'm_gdn_dt_bias', 'm_gdn_norm_w', 'm_gdn_w_out', 'm_ret_w_in', 'm_ret_w_out', 'm_gmlp_w_in', 'm_gmlp_ln_g', 'm_gmlp_ln_b', 'm_gmlp_w_s', 'm_gmlp_b_s', 'm_gmlp_w_out', 'm_sb_w_in', 'm_sb_w_out', 'v_cond_w', 'v_cond_b', 'v_ada_w', 'v_ada_b', 'v_ln_g', 'v_ln_b', 'v_ffn_up', 'v_ffn_conv_w', 'v_ffn_conv_b', 'v_ffn_down', 'v_gdn_w_in', 'v_gdn_conv_w', 'v_gdn_a_log', 'v_gdn_dt_bias', 'v_gdn_norm_w', 'v_gdn_w_out', 'v_ret_w_in', 'v_ret_w_out', 'v_gmlp_w_in', 'v_gmlp_ln_g', 'v_gmlp_ln_b', 'v_gmlp_w_s', 'v_gmlp_b_s', 'v_gmlp_w_out', 'v_sb_w_in', 'v_sb_w_out']
TWIN_OUTPUTS = ['loss', 'grad_x', 'grad_cond_w', 'grad_cond_b', 'grad_ada_w', 'grad_ada_b', 'grad_ln_g', 'grad_ln_b', 'grad_ffn_up', 'grad_ffn_conv_w', 'grad_ffn_conv_b', 'grad_ffn_down', 'grad_gdn_w_in', 'grad_gdn_conv_w', 'grad_gdn_a_log', 'grad_gdn_dt_bias', 'grad_gdn_norm_w', 'grad_gdn_w_out', 'grad_ret_w_in', 'grad_ret_w_out', 'grad_gmlp_w_in', 'grad_gmlp_ln_g', 'grad_gmlp_ln_b', 'grad_gmlp_w_s', 'grad_gmlp_b_s', 'grad_gmlp_w_out', 'grad_sb_w_in', 'grad_sb_w_out', 'delta_cond_w', 'delta_cond_b', 'delta_ada_w', 'delta_ada_b', 'delta_ln_g', 'delta_ln_b', 'delta_ffn_up', 'delta_ffn_conv_w', 'delta_ffn_conv_b', 'delta_ffn_down', 'delta_gdn_w_in', 'delta_gdn_conv_w', 'delta_gdn_a_log', 'delta_gdn_dt_bias', 'delta_gdn_norm_w', 'delta_gdn_w_out', 'delta_ret_w_in', 'delta_ret_w_out', 'delta_gmlp_w_in', 'delta_gmlp_ln_g', 'delta_gmlp_ln_b', 'delta_gmlp_w_s', 'delta_gmlp_b_s', 'delta_gmlp_w_out', 'delta_sb_w_in', 'delta_sb_w_out', 'new_m_cond_w', 'new_m_cond_b', 'new_m_ada_w', 'new_m_ada_b', 'new_m_ln_g', 'new_m_ln_b', 'new_m_ffn_up', 'new_m_ffn_conv_w', 'new_m_ffn_conv_b', 'new_m_ffn_down', 'new_m_gdn_w_in', 'new_m_gdn_conv_w', 'new_m_gdn_a_log', 'new_m_gdn_dt_bias', 'new_m_gdn_norm_w', 'new_m_gdn_w_out', 'new_m_ret_w_in', 'new_m_ret_w_out', 'new_m_gmlp_w_in', 'new_m_gmlp_ln_g', 'new_m_gmlp_ln_b', 'new_m_gmlp_w_s', 'new_m_gmlp_b_s', 'new_m_gmlp_w_out', 'new_m_sb_w_in', 'new_m_sb_w_out', 'new_v_cond_w', 'new_v_cond_b', 'new_v_ada_w', 'new_v_ada_b', 'new_v_ln_g', 'new_v_ln_b', 'new_v_ffn_up', 'new_v_ffn_conv_w', 'new_v_ffn_conv_b', 'new_v_ffn_down', 'new_v_gdn_w_in', 'new_v_gdn_conv_w', 'new_v_gdn_a_log', 'new_v_gdn_dt_bias', 'new_v_gdn_norm_w', 'new_v_gdn_w_out', 'new_v_ret_w_in', 'new_v_ret_w_out', 'new_v_gmlp_w_in', 'new_v_gmlp_ln_g', 'new_v_gmlp_ln_b', 'new_v_gmlp_w_s', 'new_v_gmlp_b_s', 'new_v_gmlp_w_out', 'new_v_sb_w_in', 'new_v_sb_w_out']
TWIN_LEAF_KINDS = {'loss': 'loss', 'grad_x': 'grad_x', 'grad_cond_w': 'grad_w', 'grad_cond_b': 'grad_w', 'grad_ada_w': 'grad_w', 'grad_ada_b': 'grad_w', 'grad_ln_g': 'grad_w', 'grad_ln_b': 'grad_w', 'grad_ffn_up': 'grad_w', 'grad_ffn_conv_w': 'grad_w', 'grad_ffn_conv_b': 'grad_w', 'grad_ffn_down': 'grad_w', 'grad_gdn_w_in': 'grad_w', 'grad_gdn_conv_w': 'grad_w', 'grad_gdn_a_log': 'grad_w', 'grad_gdn_dt_bias': 'grad_w', 'grad_gdn_norm_w': 'grad_w', 'grad_gdn_w_out': 'grad_w', 'grad_ret_w_in': 'grad_w', 'grad_ret_w_out': 'grad_w', 'grad_gmlp_w_in': 'grad_w', 'grad_gmlp_ln_g': 'grad_w', 'grad_gmlp_ln_b': 'grad_w', 'grad_gmlp_w_s': 'grad_w', 'grad_gmlp_b_s': 'grad_w', 'grad_gmlp_w_out': 'grad_w', 'grad_sb_w_in': 'grad_w', 'grad_sb_w_out': 'grad_w', 'delta_cond_w': 'delta_w', 'delta_cond_b': 'delta_w', 'delta_ada_w': 'delta_w', 'delta_ada_b': 'delta_w', 'delta_ln_g': 'delta_w', 'delta_ln_b': 'delta_w', 'delta_ffn_up': 'delta_w', 'delta_ffn_conv_w': 'delta_w', 'delta_ffn_conv_b': 'delta_w', 'delta_ffn_down': 'delta_w', 'delta_gdn_w_in': 'delta_w', 'delta_gdn_conv_w': 'delta_w', 'delta_gdn_a_log': 'delta_w', 'delta_gdn_dt_bias': 'delta_w', 'delta_gdn_norm_w': 'delta_w', 'delta_gdn_w_out': 'delta_w', 'delta_ret_w_in': 'delta_w', 'delta_ret_w_out': 'delta_w', 'delta_gmlp_w_in': 'delta_w', 'delta_gmlp_ln_g': 'delta_w', 'delta_gmlp_ln_b': 'delta_w', 'delta_gmlp_w_s': 'delta_w', 'delta_gmlp_b_s': 'delta_w', 'delta_gmlp_w_out': 'delta_w', 'delta_sb_w_in': 'delta_w', 'delta_sb_w_out': 'delta_w', 'new_m_cond_w': 'new_m', 'new_m_cond_b': 'new_m', 'new_m_ada_w': 'new_m', 'new_m_ada_b': 'new_m', 'new_m_ln_g': 'new_m', 'new_m_ln_b': 'new_m', 'new_m_ffn_up': 'new_m', 'new_m_ffn_conv_w': 'new_m', 'new_m_ffn_conv_b': 'new_m', 'new_m_ffn_down': 'new_m', 'new_m_gdn_w_in': 'new_m', 'new_m_gdn_conv_w': 'new_m', 'new_m_gdn_a_log': 'new_m', 'new_m_gdn_dt_bias': 'new_m', 'new_m_gdn_norm_w': 'new_m', 'new_m_gdn_w_out': 'new_m', 'new_m_ret_w_in': 'new_m', 'new_m_ret_w_out': 'new_m', 'new_m_gmlp_w_in': 'new_m', 'new_m_gmlp_ln_g': 'new_m', 'new_m_gmlp_ln_b': 'new_m', 'new_m_gmlp_w_s': 'new_m', 'new_m_gmlp_b_s': 'new_m', 'new_m_gmlp_w_out': 'new_m', 'new_m_sb_w_in': 'new_m', 'new_m_sb_w_out': 'new_m', 'new_v_cond_w': 'new_v', 'new_v_cond_b': 'new_v', 'new_v_ada_w': 'new_v', 'new_v_ada_b': 'new_v', 'new_v_ln_g': 'new_v', 'new_v_ln_b': 'new_v', 'new_v_ffn_up': 'new_v', 'new_v_ffn_conv_w': 'new_v', 'new_v_ffn_conv_b': 'new_v', 'new_v_ffn_down': 'new_v', 'new_v_gdn_w_in': 'new_v', 'new_v_gdn_conv_w': 'new_v', 'new_v_gdn_a_log': 'new_v', 'new_v_gdn_dt_bias': 'new_v', 'new_v_gdn_norm_w': 'new_v', 'new_v_gdn_w_out': 'new_v', 'new_v_ret_w_in': 'new_v', 'new_v_ret_w_out': 'new_v', 'new_v_gmlp_w_in': 'new_v', 'new_v_gmlp_ln_g': 'new_v', 'new_v_gmlp_ln_b': 'new_v', 'new_v_gmlp_w_s': 'new_v', 'new_v_gmlp_b_s': 'new_v', 'new_v_gmlp_w_out': 'new_v', 'new_v_sb_w_in': 'new_v', 'new_v_sb_w_out': 'new_v'}


def _forward(args):
    return _fwd_reference(*[args[k] for k in FWD_PARAMS])


def _output_shape():
    def fwd():
        inp = _fwd_setup_inputs(0)
        return _fwd_reference(*[inp[k] for k in FWD_PARAMS])
    out = _jax.eval_shape(fwd)
    return out.shape, out.dtype

N_MICROBATCH = 1
ADAM_LR = 0.001
ADAM_B1 = 0.9
ADAM_B2 = 0.999
ADAM_EPS = 1e-08
ADAM_WD = 0.01
ADAM_STEP = 10
PER_EXAMPLE_BATCH_AXIS = {'x': 0, 'c': 0, 'loss_target': 0}
SHARED_INPUTS = []
_WEIGHT_DTYPES = {'cond_w': _jnp.float32, 'cond_b': _jnp.float32, 'ada_w': _jnp.float32, 'ada_b': _jnp.float32, 'ln_g': _jnp.float32, 'ln_b': _jnp.float32, 'ffn_up': _jnp.float32, 'ffn_conv_w': _jnp.float32, 'ffn_conv_b': _jnp.float32, 'ffn_down': _jnp.float32, 'gdn_w_in': _jnp.float32, 'gdn_conv_w': _jnp.float32, 'gdn_a_log': _jnp.float32, 'gdn_dt_bias': _jnp.float32, 'gdn_norm_w': _jnp.float32, 'gdn_w_out': _jnp.float32, 'ret_w_in': _jnp.float32, 'ret_w_out': _jnp.float32, 'gmlp_w_in': _jnp.float32, 'gmlp_ln_g': _jnp.float32, 'gmlp_ln_b': _jnp.float32, 'gmlp_w_s': _jnp.float32, 'gmlp_b_s': _jnp.float32, 'gmlp_w_out': _jnp.float32, 'sb_w_in': _jnp.float32, 'sb_w_out': _jnp.float32}
MOMENT_SCALE = {'cond_w': 1.857811e-02, 'cond_b': 2.249367e-02, 'ada_w': 3.738400e-02, 'ada_b': 7.717937e-02, 'ln_g': 2.280042e+01, 'ln_b': 2.191828e+00, 'ffn_up': 2.437341e-02, 'ffn_conv_w': 2.497572e-02, 'ffn_conv_b': 2.413171e-02, 'ffn_down': 9.464648e-02, 'gdn_w_in': 3.321173e-02, 'gdn_conv_w': 3.110192e-02, 'gdn_a_log': 2.771616e-01, 'gdn_dt_bias': 2.740660e-01, 'gdn_norm_w': 1.185955e-01, 'gdn_w_out': 1.087035e-01, 'ret_w_in': 3.244415e-02, 'ret_w_out': 9.283561e-02, 'gmlp_w_in': 3.147152e-02, 'gmlp_ln_g': 2.041270e-02, 'gmlp_ln_b': 2.080057e-02, 'gmlp_w_s': 2.946290e-02, 'gmlp_b_s': 4.196623e-02, 'gmlp_w_out': 1.720156e-01, 'sb_w_in': 2.955751e-02, 'sb_w_out': 1.041056e-01}


def _to_microbatches(a, axis):
    t = _jnp.moveaxis(a, axis, 0)
    t = t.reshape((N_MICROBATCH, t.shape[0] // N_MICROBATCH) + t.shape[1:])
    return _jnp.moveaxis(t, 1, axis + 1)


def setup_inputs(seed: int = 0) -> dict:
    inp = _fwd_setup_inputs(seed)
    key = _jax.random.fold_in(_jax.random.key(seed), 7919)
    shape, _ = _output_shape()
    out = dict(inp)
    out["loss_target"] = _jax.random.normal(_jax.random.fold_in(key, 0), shape, _jnp.float32)
    for i, name in enumerate(TWIN_WEIGHTS):
        w = inp[name].astype(_jnp.float32)
        if MOMENT_SCALE is None:
            s = _jnp.sqrt(_jnp.mean(_jnp.square(w)) + 1e-30)
        else:
            s = MOMENT_SCALE[name]
        km, kv = _jax.random.split(_jax.random.fold_in(key, i + 1))
        out[name] = w
        out["m_" + name] = s * _jax.random.normal(km, w.shape, _jnp.float32)
        out["v_" + name] = (s * s) * _jax.random.uniform(kv, w.shape, _jnp.float32, 0.5, 1.5)
    if N_MICROBATCH > 1:
        for name, axis in PER_EXAMPLE_BATCH_AXIS.items():
            out[name] = _to_microbatches(out[name], axis)
    return {'x': out['x'], 'c': out['c'], 'cond_w': out['cond_w'], 'cond_b': out['cond_b'], 'ada_w': out['ada_w'], 'ada_b': out['ada_b'], 'ln_g': out['ln_g'], 'ln_b': out['ln_b'], 'ffn_up': out['ffn_up'], 'ffn_conv_w': out['ffn_conv_w'], 'ffn_conv_b': out['ffn_conv_b'], 'ffn_down': out['ffn_down'], 'gdn_w_in': out['gdn_w_in'], 'gdn_conv_w': out['gdn_conv_w'], 'gdn_a_log': out['gdn_a_log'], 'gdn_dt_bias': out['gdn_dt_bias'], 'gdn_norm_w': out['gdn_norm_w'], 'gdn_w_out': out['gdn_w_out'], 'ret_w_in': out['ret_w_in'], 'ret_w_out': out['ret_w_out'], 'gmlp_w_in': out['gmlp_w_in'], 'gmlp_ln_g': out['gmlp_ln_g'], 'gmlp_ln_b': out['gmlp_ln_b'], 'gmlp_w_s': out['gmlp_w_s'], 'gmlp_b_s': out['gmlp_b_s'], 'gmlp_w_out': out['gmlp_w_out'], 'sb_w_in': out['sb_w_in'], 'sb_w_out': out['sb_w_out'], 'loss_target': out['loss_target'], 'm_cond_w': out['m_cond_w'], 'm_cond_b': out['m_cond_b'], 'm_ada_w': out['m_ada_w'], 'm_ada_b': out['m_ada_b'], 'm_ln_g': out['m_ln_g'], 'm_ln_b': out['m_ln_b'], 'm_ffn_up': out['m_ffn_up'], 'm_ffn_conv_w': out['m_ffn_conv_w'], 'm_ffn_conv_b': out['m_ffn_conv_b'], 'm_ffn_down': out['m_ffn_down'], 'm_gdn_w_in': out['m_gdn_w_in'], 'm_gdn_conv_w': out['m_gdn_conv_w'], 'm_gdn_a_log': out['m_gdn_a_log'], 'm_gdn_dt_bias': out['m_gdn_dt_bias'], 'm_gdn_norm_w': out['m_gdn_norm_w'], 'm_gdn_w_out': out['m_gdn_w_out'], 'm_ret_w_in': out['m_ret_w_in'], 'm_ret_w_out': out['m_ret_w_out'], 'm_gmlp_w_in': out['m_gmlp_w_in'], 'm_gmlp_ln_g': out['m_gmlp_ln_g'], 'm_gmlp_ln_b': out['m_gmlp_ln_b'], 'm_gmlp_w_s': out['m_gmlp_w_s'], 'm_gmlp_b_s': out['m_gmlp_b_s'], 'm_gmlp_w_out': out['m_gmlp_w_out'], 'm_sb_w_in': out['m_sb_w_in'], 'm_sb_w_out': out['m_sb_w_out'], 'v_cond_w': out['v_cond_w'], 'v_cond_b': out['v_cond_b'], 'v_ada_w': out['v_ada_w'], 'v_ada_b': out['v_ada_b'], 'v_ln_g': out['v_ln_g'], 'v_ln_b': out['v_ln_b'], 'v_ffn_up': out['v_ffn_up'], 'v_ffn_conv_w': out['v_ffn_conv_w'], 'v_ffn_conv_b': out['v_ffn_conv_b'], 'v_ffn_down': out['v_ffn_down'], 'v_gdn_w_in': out['v_gdn_w_in'], 'v_gdn_conv_w': out['v_gdn_conv_w'], 'v_gdn_a_log': out['v_gdn_a_log'], 'v_gdn_dt_bias': out['v_gdn_dt_bias'], 'v_gdn_norm_w': out['v_gdn_norm_w'], 'v_gdn_w_out': out['v_gdn_w_out'], 'v_ret_w_in': out['v_ret_w_in'], 'v_ret_w_out': out['v_ret_w_out'], 'v_gmlp_w_in': out['v_gmlp_w_in'], 'v_gmlp_ln_g': out['v_gmlp_ln_g'], 'v_gmlp_ln_b': out['v_gmlp_ln_b'], 'v_gmlp_w_s': out['v_gmlp_w_s'], 'v_gmlp_b_s': out['v_gmlp_b_s'], 'v_gmlp_w_out': out['v_gmlp_w_out'], 'v_sb_w_in': out['v_sb_w_in'], 'v_sb_w_out': out['v_sb_w_out']}


def _loss(weights, diff, rest, loss_target):
    with _jax.named_scope("forward"):
        args = {**rest, TWIN_DIFF_INPUT: diff, **{k: w.astype(_WEIGHT_DTYPES[k]) for k, w in weights.items()}}
        y = _forward(args)
    with _jax.named_scope("loss_head"):
        err = _jnp.square(y.astype(_jnp.float32) - loss_target)
        return 0.5 * _jnp.sum(_jnp.mean(err, axis=-1)) if err.ndim else 0.5 * err


def _adamw(w, g, m, v):
    m = ADAM_B1 * m + (1.0 - ADAM_B1) * g
    v = ADAM_B2 * v + (1.0 - ADAM_B2) * _jnp.square(g)
    m_hat = m / (1.0 - ADAM_B1 ** ADAM_STEP)
    v_hat = v / (1.0 - ADAM_B2 ** ADAM_STEP)
    delta = -ADAM_LR * (m_hat / (_jnp.sqrt(v_hat) + ADAM_EPS) + ADAM_WD * w)
    return delta, m, v


def reference(x, c, cond_w, cond_b, ada_w, ada_b, ln_g, ln_b, ffn_up, ffn_conv_w, ffn_conv_b, ffn_down, gdn_w_in, gdn_conv_w, gdn_a_log, gdn_dt_bias, gdn_norm_w, gdn_w_out, ret_w_in, ret_w_out, gmlp_w_in, gmlp_ln_g, gmlp_ln_b, gmlp_w_s, gmlp_b_s, gmlp_w_out, sb_w_in, sb_w_out, loss_target, m_cond_w, m_cond_b, m_ada_w, m_ada_b, m_ln_g, m_ln_b, m_ffn_up, m_ffn_conv_w, m_ffn_conv_b, m_ffn_down, m_gdn_w_in, m_gdn_conv_w, m_gdn_a_log, m_gdn_dt_bias, m_gdn_norm_w, m_gdn_w_out, m_ret_w_in, m_ret_w_out, m_gmlp_w_in, m_gmlp_ln_g, m_gmlp_ln_b, m_gmlp_w_s, m_gmlp_b_s, m_gmlp_w_out, m_sb_w_in, m_sb_w_out, v_cond_w, v_cond_b, v_ada_w, v_ada_b, v_ln_g, v_ln_b, v_ffn_up, v_ffn_conv_w, v_ffn_conv_b, v_ffn_down, v_gdn_w_in, v_gdn_conv_w, v_gdn_a_log, v_gdn_dt_bias, v_gdn_norm_w, v_gdn_w_out, v_ret_w_in, v_ret_w_out, v_gmlp_w_in, v_gmlp_ln_g, v_gmlp_ln_b, v_gmlp_w_s, v_gmlp_b_s, v_gmlp_w_out, v_sb_w_in, v_sb_w_out):
    given = dict(x=x, c=c, cond_w=cond_w, cond_b=cond_b, ada_w=ada_w, ada_b=ada_b, ln_g=ln_g, ln_b=ln_b, ffn_up=ffn_up, ffn_conv_w=ffn_conv_w, ffn_conv_b=ffn_conv_b, ffn_down=ffn_down, gdn_w_in=gdn_w_in, gdn_conv_w=gdn_conv_w, gdn_a_log=gdn_a_log, gdn_dt_bias=gdn_dt_bias, gdn_norm_w=gdn_norm_w, gdn_w_out=gdn_w_out, ret_w_in=ret_w_in, ret_w_out=ret_w_out, gmlp_w_in=gmlp_w_in, gmlp_ln_g=gmlp_ln_g, gmlp_ln_b=gmlp_ln_b, gmlp_w_s=gmlp_w_s, gmlp_b_s=gmlp_b_s, gmlp_w_out=gmlp_w_out, sb_w_in=sb_w_in, sb_w_out=sb_w_out, loss_target=loss_target, m_cond_w=m_cond_w, m_cond_b=m_cond_b, m_ada_w=m_ada_w, m_ada_b=m_ada_b, m_ln_g=m_ln_g, m_ln_b=m_ln_b, m_ffn_up=m_ffn_up, m_ffn_conv_w=m_ffn_conv_w, m_ffn_conv_b=m_ffn_conv_b, m_ffn_down=m_ffn_down, m_gdn_w_in=m_gdn_w_in, m_gdn_conv_w=m_gdn_conv_w, m_gdn_a_log=m_gdn_a_log, m_gdn_dt_bias=m_gdn_dt_bias, m_gdn_norm_w=m_gdn_norm_w, m_gdn_w_out=m_gdn_w_out, m_ret_w_in=m_ret_w_in, m_ret_w_out=m_ret_w_out, m_gmlp_w_in=m_gmlp_w_in, m_gmlp_ln_g=m_gmlp_ln_g, m_gmlp_ln_b=m_gmlp_ln_b, m_gmlp_w_s=m_gmlp_w_s, m_gmlp_b_s=m_gmlp_b_s, m_gmlp_w_out=m_gmlp_w_out, m_sb_w_in=m_sb_w_in, m_sb_w_out=m_sb_w_out, v_cond_w=v_cond_w, v_cond_b=v_cond_b, v_ada_w=v_ada_w, v_ada_b=v_ada_b, v_ln_g=v_ln_g, v_ln_b=v_ln_b, v_ffn_up=v_ffn_up, v_ffn_conv_w=v_ffn_conv_w, v_ffn_conv_b=v_ffn_conv_b, v_ffn_down=v_ffn_down, v_gdn_w_in=v_gdn_w_in, v_gdn_conv_w=v_gdn_conv_w, v_gdn_a_log=v_gdn_a_log, v_gdn_dt_bias=v_gdn_dt_bias, v_gdn_norm_w=v_gdn_norm_w, v_gdn_w_out=v_gdn_w_out, v_ret_w_in=v_ret_w_in, v_ret_w_out=v_ret_w_out, v_gmlp_w_in=v_gmlp_w_in, v_gmlp_ln_g=v_gmlp_ln_g, v_gmlp_ln_b=v_gmlp_ln_b, v_gmlp_w_s=v_gmlp_w_s, v_gmlp_b_s=v_gmlp_b_s, v_gmlp_w_out=v_gmlp_w_out, v_sb_w_in=v_sb_w_in, v_sb_w_out=v_sb_w_out)
    weights = {n: given[n] for n in TWIN_WEIGHTS}
    shared = {n: given[n] for n in SHARED_INPUTS}
    per_example = {n: given[n] for n in ['x', 'c']}
    grad_fn = _jax.value_and_grad(_loss, argnums=(0, 1))

    def one_microbatch(ex, loss_target):
        ex = dict(ex)
        diff = ex.pop(TWIN_DIFF_INPUT)
        return grad_fn(weights, diff, {**shared, **ex}, loss_target)

    if N_MICROBATCH == 1:
        loss, (grad_w, grad_x) = one_microbatch(per_example, given["loss_target"])
    else:
        def body(carry, xs):
            loss_sum, grad_sum = carry
            l_k, (gw_k, gx_k) = one_microbatch(xs[0], xs[1])
            with _jax.named_scope("update"):
                return (loss_sum + l_k, _jax.tree.map(_jnp.add, grad_sum, gw_k)), gx_k

        init = (_jnp.zeros((), _jnp.float32), _jax.tree.map(_jnp.zeros_like, weights))
        (loss, grad_w), grad_x = _jax.lax.scan(body, init, (per_example, given["loss_target"]))
    with _jax.named_scope("update"):
        delta_w, new_m, new_v = {}, {}, {}
        for n in TWIN_WEIGHTS:
            delta_w[n], new_m[n], new_v[n] = _adamw(weights[n], grad_w[n], given["m_" + n], given["v_" + n])
    return (loss, grad_x, *[grad_w[n] for n in TWIN_WEIGHTS], *[delta_w[n] for n in TWIN_WEIGHTS],
            *[new_m[n] for n in TWIN_WEIGHTS], *[new_v[n] for n in TWIN_WEIGHTS])
```

```python
import functools
import math

import jax
import jax.numpy as jnp
from jax import lax
from jax.experimental import pallas as pl
from jax.experimental.pallas import tpu as pltpu

F32 = jnp.float32
BF16 = jnp.bfloat16
MXU_DTYPE = jnp.bfloat16
MESH = pl.DeviceIdType.MESH
N_DEV = 8
LANES = 128
VMEM_LIMIT = 48 * 1024 * 1024

DEPTH = 4
LN_EPS = 1e-5
DN_ALPHA = (2.0 * DEPTH) ** 0.25
GDN_HEADS, GDN_CONV, GDN_CHUNK = 8, 4, 64
RET_HEADS, RET_CHUNK, RET_ROPE_BASE = 4, 128, 10000.0
GMLP_CHUNK, GMLP_GROUPS = 128, 8
SB_HEADS = 16
ADAM_LR, ADAM_B1, ADAM_B2, ADAM_EPS, ADAM_WD, ADAM_STEP = 0.001, 0.9, 0.999, 1e-08, 0.01, 10

WEIGHTS = ['cond_w', 'cond_b', 'ada_w', 'ada_b', 'ln_g', 'ln_b', 'ffn_up', 'ffn_conv_w', 'ffn_conv_b', 'ffn_down',
           'gdn_w_in', 'gdn_conv_w', 'gdn_a_log', 'gdn_dt_bias', 'gdn_norm_w', 'gdn_w_out', 'ret_w_in', 'ret_w_out',
           'gmlp_w_in', 'gmlp_ln_g', 'gmlp_ln_b', 'gmlp_w_s', 'gmlp_b_s', 'gmlp_w_out', 'sb_w_in', 'sb_w_out']
BIG = {'ffn_up': 2, 'ffn_down': 1, 'gdn_w_in': 1, 'gdn_w_out': 0, 'ret_w_in': 1, 'ret_w_out': 0,
       'gmlp_w_in': 1, 'gmlp_w_out': 0, 'sb_w_in': 1, 'sb_w_out': 0}
SMALL_SHARDED = {'ln_g': 2, 'ln_b': 2, 'ffn_conv_w': 2, 'gdn_conv_w': 1}
SMALL_REPL = ['ffn_conv_b', 'gdn_a_log', 'gdn_dt_bias', 'gdn_norm_w', 'gmlp_ln_g', 'gmlp_ln_b', 'gmlp_w_s', 'gmlp_b_s']


def _pcall(body, **kw):
    return pl.pallas_call(body, **kw)


def _my_id():
    return 4 * lax.axis_index("x") + 2 * lax.axis_index("y") + lax.axis_index("c")


def _exchange(src, scatter, name):
    blk = src.shape[1:] if scatter else src.shape
    out_shape = jax.ShapeDtypeStruct((N_DEV,) + tuple(blk), src.dtype)

    def body(src_ref, out_ref, send_sems, recv_sems, local_sem):
        x, y, c = lax.axis_index("x"), lax.axis_index("y"), lax.axis_index("c")
        me = 4 * x + 2 * y + c
        mine = pltpu.make_async_copy(src_ref.at[me] if scatter else src_ref, out_ref.at[me], local_sem)
        mine.start()
        copies = []
        for k in range(1, N_DEV):
            px = 1 - x if (k >> 2) & 1 else x
            py = 1 - y if (k >> 1) & 1 else y
            pc = 1 - c if k & 1 else c
            peer = 4 * px + 2 * py + pc
            cp = pltpu.make_async_remote_copy(
                src_ref=src_ref.at[peer] if scatter else src_ref,
                dst_ref=out_ref.at[me],
                send_sem=send_sems.at[k - 1], recv_sem=recv_sems.at[k - 1],
                device_id=(px, py, pc), device_id_type=MESH)
            cp.start()
            copies.append(cp)
        for cp in copies:
            cp.wait()
        mine.wait()

    return _pcall(
        body, name=name, out_shape=out_shape,
        in_specs=[pl.BlockSpec(memory_space=pl.ANY)],
        out_specs=pl.BlockSpec(memory_space=pl.ANY),
        scratch_shapes=[pltpu.SemaphoreType.DMA((N_DEV - 1,)), pltpu.SemaphoreType.DMA((N_DEV - 1,)),
                        pltpu.SemaphoreType.DMA(())],
    )(src)


ROW_ALIGN = 16
BIG_ROW_ALIGN = 512


def _pack_rows(parts, dtype, row_align=ROW_ALIGN):
    flat = jnp.concatenate([p.reshape(-1).astype(dtype) for p in parts])
    n = flat.shape[0]
    pad = (-n) % (row_align * LANES)
    if pad:
        flat = jnp.concatenate([flat, jnp.zeros((pad,), dtype)])
    return flat.reshape(-1, LANES)


def _unpack_rows(packed, shapes):
    lead = packed.shape[:-2]
    flat = packed.reshape(lead + (-1,))
    out, off = [], 0
    for s in shapes:
        n = math.prod(s)
        out.append(flat[..., off:off + n].reshape(lead + tuple(s)))
        off += n
    return out


def _pick(dim, prefs):
    for p in prefs:
        if dim % p == 0:
            return p
    return dim


def _mm(a, b, dims, name, exact=False):
    if dims == 'nn':
        (m, k), n = a.shape, b.shape[1]
    elif dims == 'nt':
        (m, k), n = a.shape, b.shape[0]
    else:
        (k, m), n = a.shape, b.shape[1]
    tm = _pick(m, (512, 256, 128))
    tn = _pick(n, (512, 256, 128))
    tk = k if k <= 2816 else _pick(k, (1024, 512, 256, 128))
    nk = k // tk
    if dims == 'nn':
        a_spec = pl.BlockSpec((tm, tk), lambda i, j, kk: (i, kk))
        b_spec = pl.BlockSpec((tk, tn), lambda i, j, kk: (kk, j))
        dn = (((1,), (0,)), ((), ()))
    elif dims == 'nt':
        a_spec = pl.BlockSpec((tm, tk), lambda i, j, kk: (i, kk))
        b_spec = pl.BlockSpec((tn, tk), lambda i, j, kk: (j, kk))
        dn = (((1,), (1,)), ((), ()))
    else:
        a_spec = pl.BlockSpec((tk, tm), lambda i, j, kk: (kk, i))
        b_spec = pl.BlockSpec((tk, tn), lambda i, j, kk: (kk, j))
        dn = (((0,), (0,)), ((), ()))

    def body(a_ref, b_ref, o_ref, acc_ref):
        kk = pl.program_id(2)

        @pl.when(kk == 0)
        def _():
            acc_ref[...] = jnp.zeros_like(acc_ref)

        if exact:
            acc_ref[...] += lax.dot_general(a_ref[...], b_ref[...], dn, precision=lax.Precision.HIGHEST,
                                            preferred_element_type=F32)
        else:
            acc_ref[...] += lax.dot_general(a_ref[...].astype(MXU_DTYPE), b_ref[...].astype(MXU_DTYPE), dn,
                                            preferred_element_type=F32)

        @pl.when(kk == nk - 1)
        def _():
            o_ref[...] = acc_ref[...]

    return _pcall(
        body, name=name, out_shape=jax.ShapeDtypeStruct((m, n), F32),
        grid=(m // tm, n // tn, nk),
        in_specs=[a_spec, b_spec],
        out_specs=pl.BlockSpec((tm, tn), lambda i, j, kk: (i, j)),
        scratch_shapes=[pltpu.VMEM((tm, tn), F32)],
        compiler_params=pltpu.CompilerParams(dimension_semantics=("parallel", "parallel", "arbitrary"),
                                             vmem_limit_bytes=VMEM_LIMIT),
    )(a, b)


def _mm_outer(a, b, name):
    pad = LANES - a.shape[0]
    return _mm(jnp.pad(a.T, ((0, 0), (0, pad))), jnp.pad(b, ((0, pad), (0, 0))), 'nn', name, exact=True)


@functools.partial(jax.custom_vjp, nondiff_argnums=(2,))
def _linear(a, w, name):
    return _mm(a, w, 'nn', name + "_fwd")


def _linear_fwd(a, w, name):
    return _mm(a, w, 'nn', name + "_fwd"), (a, w)


def _linear_bwd(name, res, dy):
    a, w = res
    return _mm(dy, w, 'nt', name + "_dx"), _mm(a, dy, 'tn', name + "_dw")


_linear.defvjp(_linear_fwd, _linear_bwd)


SB_BK = 128


def _sb_tq(s):
    return _pick(s, (512, 256, 128))


def _sb_tile(qb, kb, scale, q0, k0, masked, u_excl, carry_l):
    tq, bk = qb.shape[0], kb.shape[0]
    z = lax.dot_general(qb, kb, (((1,), (1,)), ((), ())), preferred_element_type=F32) * scale
    sp = jnp.log(1.0 + jnp.exp(-jnp.abs(z)))
    ls = jnp.minimum(z, 0.0) - sp
    lm = ls - z
    if masked:
        row = lax.broadcasted_iota(jnp.int32, (tq, bk), 0) + q0
        col = lax.broadcasted_iota(jnp.int32, (tq, bk), 1) + k0
        valid = col < row
        lm = jnp.where(valid, lm, 0.0)
    else:
        valid = None
    hi = lm.astype(BF16)
    lo = (lm - hi.astype(F32)).astype(BF16)
    f = (jnp.dot(hi, u_excl, preferred_element_type=F32) + jnp.dot(lo, u_excl, preferred_element_type=F32)
         + carry_l)
    a = jnp.exp(ls + f)
    if masked:
        a = jnp.where(valid, a, 0.0)
    return valid, ls, lm, a


def _sb_fwd_call(q, k, v):
    h, s, dh = q.shape
    tq, bk = _sb_tq(s), SB_BK
    scale = dh ** -0.5
    nd = tq // bk

    def body(q_ref, k_ref, v_ref, o_ref):
        i = pl.program_id(1)
        q0 = i * tq
        qb = q_ref[...]
        r = lax.broadcasted_iota(jnp.int32, (bk, bk), 0)
        cc = lax.broadcasted_iota(jnp.int32, (bk, bk), 1)
        u_excl = (r > cc).astype(BF16)

        def step(j, carry, masked):
            cl, acc = carry
            k0 = pl.multiple_of(j * bk, bk)
            kb = k_ref[pl.ds(k0, bk), :]
            vb = v_ref[pl.ds(k0, bk), :]
            _, _, lm, a = _sb_tile(qb, kb, scale, q0, k0, masked, u_excl, cl)
            acc = acc + jnp.dot(a.astype(MXU_DTYPE), vb, preferred_element_type=F32)
            cl = cl + jnp.sum(lm, axis=1, keepdims=True)
            return cl, acc

        carry = (jnp.zeros((tq, 1), F32), jnp.zeros((tq, dh), F32))
        nkb = (i + 1) * nd
        carry = lax.fori_loop(0, nd, lambda jj, cr: step(nkb - 1 - jj, cr, True), carry)
        carry = lax.fori_loop(0, i * nd, lambda jj, cr: step(i * nd - 1 - jj, cr, False), carry)
        o_ref[...] = carry[1]

    return _pcall(
        body, name="sb_fwd", out_shape=jax.ShapeDtypeStruct((h, s, dh), F32),
        grid=(h, s // tq),
        in_specs=[pl.BlockSpec((None, tq, dh), lambda hh, i: (hh, i, 0)),
                  pl.BlockSpec((None, s, dh), lambda hh, i: (hh, 0, 0)),
                  pl.BlockSpec((None, s, dh), lambda hh, i: (hh, 0, 0))],
        out_specs=pl.BlockSpec((None, tq, dh), lambda hh, i: (hh, i, 0)),
        compiler_params=pltpu.CompilerParams(dimension_semantics=("parallel", "arbitrary"),
                                             vmem_limit_bytes=VMEM_LIMIT),
    )(q, k, v)


def _sb_bwd_call(q, k, v, o, do):
    h, s, dh = q.shape
    tq, bk = _sb_tq(s), SB_BK
    scale = dh ** -0.5
    nd = tq // bk

    def body(q_ref, k_ref, v_ref, o_ref, do_ref, dq_ref, dk_ref, dv_ref):
        i = pl.program_id(1)
        q0 = i * tq

        @pl.when(i == 0)
        def _():
            dk_ref[...] = jnp.zeros_like(dk_ref)
            dv_ref[...] = jnp.zeros_like(dv_ref)

        qb = q_ref[...]
        dof = do_ref[...]
        dob = dof.astype(MXU_DTYPE)
        delta = jnp.sum(dob.astype(F32) * o_ref[...], axis=1, keepdims=True)
        r = lax.broadcasted_iota(jnp.int32, (bk, bk), 0)
        cc = lax.broadcasted_iota(jnp.int32, (bk, bk), 1)
        u_excl = (r > cc).astype(BF16)
        u_incl = (r >= cc).astype(BF16)

        def step(j, carry, masked):
            cl, cg, dq = carry
            k0 = pl.multiple_of(j * bk, bk)
            kb = k_ref[pl.ds(k0, bk), :]
            vb = v_ref[pl.ds(k0, bk), :]
            valid, ls, lm, a = _sb_tile(qb, kb, scale, q0, k0, masked, u_excl, cl)
            da = lax.dot_general(dob, vb, (((1,), (1,)), ((), ())), preferred_element_type=F32)
            ab = a.astype(MXU_DTYPE)
            g = da * ab.astype(F32)
            ghi = g.astype(BF16)
            glo = (g - ghi.astype(F32)).astype(BF16)
            sg = (jnp.dot(ghi, u_incl, preferred_element_type=F32) + jnp.dot(glo, u_incl, preferred_element_type=F32)
                  + cg)
            p = delta - sg
            sig = jnp.exp(ls)
            dz = g * (1.0 - sig) - p * sig
            if masked:
                dz = jnp.where(valid, dz, 0.0)
            dzb = (dz * scale).astype(MXU_DTYPE)
            dq = dq + jnp.dot(dzb, kb, preferred_element_type=F32)
            dk_ref[pl.ds(k0, bk), :] += lax.dot_general(dzb, qb, (((0,), (0,)), ((), ())),
                                                        preferred_element_type=F32)
            dv_ref[pl.ds(k0, bk), :] += lax.dot_general(ab, dob, (((0,), (0,)), ((), ())),
                                                        preferred_element_type=F32)
            cl = cl + jnp.sum(lm, axis=1, keepdims=True)
            cg = cg + jnp.sum(g, axis=1, keepdims=True)
            return cl, cg, dq

        carry = (jnp.zeros((tq, 1), F32), jnp.zeros((tq, 1), F32), jnp.zeros((tq, dh), F32))
        nkb = (i + 1) * nd
        carry = lax.fori_loop(0, nd, lambda jj, cr: step(nkb - 1 - jj, cr, True), carry)
        carry = lax.fori_loop(0, i * nd, lambda jj, cr: step(i * nd - 1 - jj, cr, False), carry)
        dq_ref[...] = carry[2]

    blk_q = pl.BlockSpec((None, tq, dh), lambda hh, i: (hh, i, 0))
    blk_s = pl.BlockSpec((None, s, dh), lambda hh, i: (hh, 0, 0))
    sds = jax.ShapeDtypeStruct((h, s, dh), F32)
    return _pcall(
        body, name="sb_bwd", out_shape=(sds, sds, sds),
        grid=(h, s // tq),
        in_specs=[blk_q, blk_s, blk_s, blk_q, blk_q],
        out_specs=(blk_q, blk_s, blk_s),
        compiler_params=pltpu.CompilerParams(dimension_semantics=("parallel", "arbitrary"),
                                             vmem_limit_bytes=VMEM_LIMIT),
    )(q, k, v, o, do)


@jax.custom_vjp
def _sb_core(q, k, v):
    return _sb_fwd_call(q.astype(MXU_DTYPE), k.astype(MXU_DTYPE), v.astype(MXU_DTYPE))


def _sb_core_fwd(q, k, v):
    qb, kb, vb = q.astype(MXU_DTYPE), k.astype(MXU_DTYPE), v.astype(MXU_DTYPE)
    o = _sb_fwd_call(qb, kb, vb)
    return o, (qb, kb, vb, o)


def _sb_core_bwd(res, do):
    qb, kb, vb, o = res
    return _sb_bwd_call(qb, kb, vb, o, do)


_sb_core.defvjp(_sb_core_fwd, _sb_core_bwd)


def _adamw(gslots, w, m, v, name):
    n, r, _ = gslots.shape
    tr = _pick(r, (1024, 512, 256, 128, 64, 32, 16, 8))
    bc1 = 1.0 / (1.0 - ADAM_B1 ** ADAM_STEP)
    bc2 = 1.0 / (1.0 - ADAM_B2 ** ADAM_STEP)

    def body(g_ref, w_ref, m_ref, v_ref, go_ref, d_ref, mo_ref, vo_ref):
        g = g_ref[0]
        for t in range(1, n):
            g = g + g_ref[t]
        mn = ADAM_B1 * m_ref[...] + (1.0 - ADAM_B1) * g
        vn = ADAM_B2 * v_ref[...] + (1.0 - ADAM_B2) * (g * g)
        m_hat = mn * bc1
        v_hat = vn * bc2
        go_ref[...] = g
        d_ref[...] = -ADAM_LR * (m_hat / (jnp.sqrt(v_hat) + ADAM_EPS) + ADAM_WD * w_ref[...])
        mo_ref[...] = mn
        vo_ref[...] = vn

    row = pl.BlockSpec((tr, LANES), lambda i: (i, 0))
    sds = jax.ShapeDtypeStruct((r, LANES), F32)
    return _pcall(
        body, name=name, out_shape=(sds, sds, sds, sds),
        grid=(r // tr,),
        in_specs=[pl.BlockSpec((n, tr, LANES), lambda i: (0, i, 0)), row, row, row],
        out_specs=(row, row, row, row),
        compiler_params=pltpu.CompilerParams(dimension_semantics=("parallel",), vmem_limit_bytes=VMEM_LIMIT),
    )(gslots, w, m, v)


def _sum_slots(gslots, name):
    n, r, _ = gslots.shape
    tr = _pick(r, (1024, 512, 256, 128, 64, 32, 16, 8))

    def body(g_ref, o_ref):
        g = g_ref[0]
        for t in range(1, n):
            g = g + g_ref[t]
        o_ref[...] = g

    return _pcall(
        body, name=name, out_shape=jax.ShapeDtypeStruct((r, LANES), F32),
        grid=(r // tr,),
        in_specs=[pl.BlockSpec((n, tr, LANES), lambda i: (0, i, 0))],
        out_specs=pl.BlockSpec((tr, LANES), lambda i: (i, 0)),
        compiler_params=pltpu.CompilerParams(dimension_semantics=("parallel",), vmem_limit_bytes=VMEM_LIMIT),
    )(gslots)


def _standardize(x, eps):
    mu = jnp.mean(x, axis=-1, keepdims=True)
    xc = x - mu
    var = jnp.mean(xc * xc, axis=-1, keepdims=True)
    return xc * lax.rsqrt(var + eps)


def _layer_norm(x, g, b):
    return _standardize(x, LN_EPS) * g + b


def _l2norm(x, eps=1e-6):
    return x * lax.rsqrt(jnp.sum(x * x, axis=-1, keepdims=True) + eps)


def _causal_dwconv(x, w):
    k_w, s = w.shape[0], x.shape[0]
    xp = jnp.pad(x, ((k_w - 1, 0), (0, 0)))
    y = xp[k_w - 1:k_w - 1 + s] * w[k_w - 1]
    for j in range(k_w - 1):
        y = y + xp[j:j + s] * w[j]
    return y


def _chunk_heads(t, n_heads, chunk):
    s, hd = t.shape
    return t.reshape(s // chunk, chunk, n_heads, hd // n_heads).transpose(2, 0, 1, 3)


def _unchunk_heads(t):
    h, n, c, d = t.shape
    return t.transpose(1, 2, 0, 3).reshape(n * c, h, d)


def _gated_deltanet(hx, p):
    H, C = GDN_HEADS, GDN_CHUNK
    s, d = hx.shape
    dk = dv = d // H
    qkvz = _linear(hx, p['gdn_w_qkvz'], "gdn_in")
    ab = _linear(hx, p['gdn_w_ab'], "gdn_ab")
    qkv, z = qkvz[:, :3 * d], qkvz[:, 3 * d:]
    a, bt = ab[:, :H], ab[:, H:2 * H]
    qkv = jax.nn.silu(_causal_dwconv(qkv, p['gdn_conv_w']))
    q, k, v = qkv[:, :d], qkv[:, d:2 * d], qkv[:, 2 * d:]
    q = _l2norm(_chunk_heads(q, H, C)) * (dk ** -0.5)
    k = _l2norm(_chunk_heads(k, H, C))
    v = _chunk_heads(v, H, C)
    beta = jax.nn.sigmoid(_chunk_heads(bt, H, C)[..., 0])
    g = -jnp.exp(p['gdn_a_log'])[:, None, None] * jax.nn.softplus(
        _chunk_heads(a, H, C)[..., 0] + p['gdn_dt_bias'][:, None, None])
    gc = jnp.cumsum(g, axis=-1)
    idx = jnp.arange(C)
    causal = idx[:, None] >= idx[None, :]
    strict = idx[:, None] > idx[None, :]
    diff = gc[..., :, None] - gc[..., None, :]
    decay = jnp.where(causal, jnp.exp(jnp.where(causal, diff, 0.0)), 0.0)
    kb = k * beta[..., None]
    kk = jnp.where(strict, jnp.einsum('hncd,hnmd->hncm', kb, k) * decay, 0.0)
    eye = jnp.eye(C, dtype=F32)
    rhs = jnp.concatenate([v * beta[..., None], kb * jnp.exp(gc)[..., None]], axis=-1)
    sol = lax.linalg.triangular_solve(kk + eye, rhs, left_side=True, lower=True, unit_diagonal=True)
    u, w = sol[..., :dv], sol[..., dv:]
    qk = jnp.where(causal, jnp.einsum('hncd,hnmd->hncm', q, k) * decay, 0.0)

    def step(state, inp):
        q_n, k_n, u_n, w_n, qk_n, g_n = inp
        v_new = u_n - jnp.einsum('hck,hkv->hcv', w_n, state)
        o = (jnp.einsum('hck,hkv->hcv', q_n * jnp.exp(g_n)[..., None], state)
             + jnp.einsum('hcm,hmv->hcv', qk_n, v_new))
        g_last = g_n[..., -1:]
        state = (state * jnp.exp(g_last)[..., None]
                 + jnp.einsum('hck,hcv->hkv', k_n * jnp.exp(g_last - g_n)[..., None], v_new))
        return state, o

    xs = tuple(jnp.moveaxis(t, 1, 0) for t in (q, k, u, w, qk, gc))
    _, o = lax.scan(step, jnp.zeros((H, dk, dv), F32), xs)
    o = _unchunk_heads(jnp.moveaxis(o, 0, 1))
    o = o * lax.rsqrt(jnp.mean(o * o, axis=-1, keepdims=True) + 1e-6) * p['gdn_norm_w']
    o = o * jax.nn.silu(z.reshape(s, H, dv))
    return _linear(o.reshape(s, H * dv), p['gdn_w_out'], "gdn_out")


def _retention(hx, p):
    H, C = RET_HEADS, RET_CHUNK
    s, d = hx.shape
    dk, dv = d // H, 2 * d // H
    proj = _linear(hx, p['ret_w_in'], "ret_in")
    q, k, v, gate = proj[:, :d], proj[:, d:2 * d], proj[:, 2 * d:4 * d], proj[:, 4 * d:]
    pos = jnp.arange(s, dtype=F32)
    inv_freq = RET_ROPE_BASE ** (-jnp.linspace(0.0, 1.0, dk // 2, dtype=F32))
    ang = pos[:, None] * inv_freq[None, :]
    cos_a, sin_a = jnp.cos(ang)[:, None, :], jnp.sin(ang)[:, None, :]

    def rot(t):
        t = t.reshape(s, H, dk)
        t1, t2 = t[..., :dk // 2], t[..., dk // 2:]
        return jnp.concatenate([t1 * cos_a - t2 * sin_a, t1 * sin_a + t2 * cos_a], axis=-1).reshape(s, H * dk)

    q = _chunk_heads(rot(q), H, C)
    k = _chunk_heads(rot(k), H, C) * (dk ** -0.5)
    v = _chunk_heads(v, H, C)
    log_gamma = jnp.log(1.0 - jnp.power(2.0, -5.0 - jnp.arange(H, dtype=F32)))
    idx = jnp.arange(C, dtype=F32)
    rel = idx[:, None] - idx[None, :]
    dmask = jnp.where(rel >= 0, jnp.exp(jnp.maximum(rel, 0.0) * log_gamma[:, None, None]), 0.0)
    scores = jnp.einsum('hncd,hnmd->hncm', q, k) * dmask[:, None]
    intra = jnp.einsum('hncm,hnmv->hncv', scores, v)
    zeta = jnp.exp((C - 1.0 - idx)[None, :] * log_gamma[:, None])
    xi = jnp.exp((idx + 1.0)[None, :] * log_gamma[:, None])
    gamma_c = jnp.exp(C * log_gamma)

    def step(state, inp):
        q_n, k_n, v_n = inp
        o = jnp.einsum('hck,hkv->hcv', q_n, state) * xi[:, :, None]
        state = (state * gamma_c[:, None, None]
                 + jnp.einsum('hck,hcv->hkv', k_n * zeta[:, :, None], v_n))
        return state, o

    xs = tuple(jnp.moveaxis(t, 1, 0) for t in (q, k, v))
    _, inter = lax.scan(step, jnp.zeros((H, dk, dv), F32), xs)
    o = _unchunk_heads(intra + jnp.moveaxis(inter, 0, 1))
    o = _standardize(o, 1e-6).reshape(s, H * dv)
    o = o * jax.nn.silu(gate)
    return _linear(o, p['ret_w_out'], "ret_out")


def _chunked_gmlp(hx, p):
    C, G = GMLP_CHUNK, GMLP_GROUPS
    s, d = hx.shape
    W = 2 * d
    uv = jax.nn.gelu(_linear(hx, p['gmlp_w_in'], "gmlp_in"), approximate=False)
    u, v = uv[:, :W], uv[:, W:]
    v = _layer_norm(v, p['gmlp_ln_g'], p['gmlp_ln_b']).reshape(s // C, C, G, W // G)
    causal = jnp.tril(jnp.ones((C, C), dtype=bool))
    ws = jnp.where(causal, p['gmlp_w_s'], 0.0)
    vs = jnp.einsum('gts,nsgd->ntgd', ws, v) + p['gmlp_b_s'].T[None, :, :, None]
    return _linear(u * vs.reshape(s, W), p['gmlp_w_out'], "gmlp_out")


def _stick_breaking(hx, p):
    H = SB_HEADS
    s, d = hx.shape
    dh = d // H
    qkv = _linear(hx, p['sb_w_in'], "sb_in")
    q, k, v = (qkv[:, j * d:(j + 1) * d].reshape(s, H, dh).transpose(1, 0, 2) for j in range(3))
    o = _sb_core(q, k, v)
    return _linear(o.transpose(1, 0, 2).reshape(s, d), p['sb_w_out'], "sb_out")


def _conv_ffn(hx, w_up, conv_w, conv_b, w_down, i):
    f = w_down.shape[0]
    gu = _linear(hx, w_up, "ffn_up%d" % i)
    gate = _causal_dwconv(gu[:, :f], conv_w) + conv_b
    return _linear(jax.nn.silu(gate) * gu[:, f:], w_down, "ffn_down%d" % i)


def _local_loss(x, mods, p, target):
    d = x.shape[-1]
    mixers = (_gated_deltanet, _retention, _chunked_gmlp, _stick_breaking)
    for i in range(DEPTH):
        sh1, sc1, g1, sh2, sc2, g2 = (mods[i, j * d:(j + 1) * d] for j in range(6))
        y = mixers[i](x * (1.0 + sc1) + sh1, p)
        x = _layer_norm(DN_ALPHA * x + (1.0 + g1) * y, p['ln_g'][i, 0], p['ln_b'][i, 0])
        y = _conv_ffn(x * (1.0 + sc2) + sh2, p['ffn_up'][i], p['ffn_conv_w'][i], p['ffn_conv_b'][i],
                      p['ffn_down'][i], i)
        x = _layer_norm(DN_ALPHA * x + (1.0 + g2) * y, p['ln_g'][i, 1], p['ln_b'][i, 1])
    err = jnp.square(x - target)
    return 0.5 * jnp.sum(jnp.mean(err, axis=-1))


def _join(blocks, axis):
    return jnp.concatenate([blocks[d] for d in range(N_DEV)], axis=axis)


def _split(whole, axis):
    n = whole.shape[axis] // N_DEV
    return jnp.stack([lax.slice_in_dim(whole, d * n, (d + 1) * n, axis=axis) for d in range(N_DEV)])


def _pad8(a):
    pad = (-a.shape[0]) % 8
    return jnp.pad(a, ((0, pad), (0, 0))) if pad else a


def _pack_big_grads(full_grads, axes):
    per_dev = jnp.concatenate([_split(g, ax).reshape(N_DEV, -1) for g, ax in zip(full_grads, axes)], axis=1)
    pad = (-per_dev.shape[1]) % (BIG_ROW_ALIGN * LANES)
    if pad:
        per_dev = jnp.pad(per_dev, ((0, 0), (0, pad)))
    return per_dev.reshape(N_DEV, -1, LANES)


def kernel(x, c, cond_w, cond_b, ada_w, ada_b, ln_g, ln_b, ffn_up, ffn_conv_w, ffn_conv_b, ffn_down, gdn_w_in, gdn_conv_w, gdn_a_log, gdn_dt_bias, gdn_norm_w, gdn_w_out, ret_w_in, ret_w_out, gmlp_w_in, gmlp_ln_g, gmlp_ln_b, gmlp_w_s, gmlp_b_s, gmlp_w_out, sb_w_in, sb_w_out, loss_target, m_cond_w, m_cond_b, m_ada_w, m_ada_b, m_ln_g, m_ln_b, m_ffn_up, m_ffn_conv_w, m_ffn_conv_b, m_ffn_down, m_gdn_w_in, m_gdn_conv_w, m_gdn_a_log, m_gdn_dt_bias, m_gdn_norm_w, m_gdn_w_out, m_ret_w_in, m_ret_w_out, m_gmlp_w_in, m_gmlp_ln_g, m_gmlp_ln_b, m_gmlp_w_s, m_gmlp_b_s, m_gmlp_w_out, m_sb_w_in, m_sb_w_out, v_cond_w, v_cond_b, v_ada_w, v_ada_b, v_ln_g, v_ln_b, v_ffn_up, v_ffn_conv_w, v_ffn_conv_b, v_ffn_down, v_gdn_w_in, v_gdn_conv_w, v_gdn_a_log, v_gdn_dt_bias, v_gdn_norm_w, v_gdn_w_out, v_ret_w_in, v_ret_w_out, v_gmlp_w_in, v_gmlp_ln_g, v_gmlp_ln_b, v_gmlp_w_s, v_gmlp_b_s, v_gmlp_w_out, v_sb_w_in, v_sb_w_out):
    w = dict(cond_w=cond_w, cond_b=cond_b, ada_w=ada_w, ada_b=ada_b, ln_g=ln_g, ln_b=ln_b, ffn_up=ffn_up,
             ffn_conv_w=ffn_conv_w, ffn_conv_b=ffn_conv_b, ffn_down=ffn_down, gdn_w_in=gdn_w_in,
             gdn_conv_w=gdn_conv_w, gdn_a_log=gdn_a_log, gdn_dt_bias=gdn_dt_bias, gdn_norm_w=gdn_norm_w,
             gdn_w_out=gdn_w_out, ret_w_in=ret_w_in, ret_w_out=ret_w_out, gmlp_w_in=gmlp_w_in,
             gmlp_ln_g=gmlp_ln_g, gmlp_ln_b=gmlp_ln_b, gmlp_w_s=gmlp_w_s, gmlp_b_s=gmlp_b_s,
             gmlp_w_out=gmlp_w_out, sb_w_in=sb_w_in, sb_w_out=sb_w_out)
    mom = dict(cond_w=m_cond_w, cond_b=m_cond_b, ada_w=m_ada_w, ada_b=m_ada_b, ln_g=m_ln_g, ln_b=m_ln_b,
               ffn_up=m_ffn_up, ffn_conv_w=m_ffn_conv_w, ffn_conv_b=m_ffn_conv_b, ffn_down=m_ffn_down,
               gdn_w_in=m_gdn_w_in, gdn_conv_w=m_gdn_conv_w, gdn_a_log=m_gdn_a_log, gdn_dt_bias=m_gdn_dt_bias,
               gdn_norm_w=m_gdn_norm_w, gdn_w_out=m_gdn_w_out, ret_w_in=m_ret_w_in, ret_w_out=m_ret_w_out,
               gmlp_w_in=m_gmlp_w_in, gmlp_ln_g=m_gmlp_ln_g, gmlp_ln_b=m_gmlp_ln_b, gmlp_w_s=m_gmlp_w_s,
               gmlp_b_s=m_gmlp_b_s, gmlp_w_out=m_gmlp_w_out, sb_w_in=m_sb_w_in, sb_w_out=m_sb_w_out)
    var = dict(cond_w=v_cond_w, cond_b=v_cond_b, ada_w=v_ada_w, ada_b=v_ada_b, ln_g=v_ln_g, ln_b=v_ln_b,
               ffn_up=v_ffn_up, ffn_conv_w=v_ffn_conv_w, ffn_conv_b=v_ffn_conv_b, ffn_down=v_ffn_down,
               gdn_w_in=v_gdn_w_in, gdn_conv_w=v_gdn_conv_w, gdn_a_log=v_gdn_a_log, gdn_dt_bias=v_gdn_dt_bias,
               gdn_norm_w=v_gdn_norm_w, gdn_w_out=v_gdn_w_out, ret_w_in=v_ret_w_in, ret_w_out=v_ret_w_out,
               gmlp_w_in=v_gmlp_w_in, gmlp_ln_g=v_gmlp_ln_g, gmlp_ln_b=v_gmlp_ln_b, gmlp_w_s=v_gmlp_w_s,
               gmlp_b_s=v_gmlp_b_s, gmlp_w_out=v_gmlp_w_out, sb_w_in=v_sb_w_in, sb_w_out=v_sb_w_out)

    me = _my_id()
    x = x[0]
    target = loss_target[0]
    d = x.shape[-1]
    dsh = d // N_DEV
    msh = ada_w.shape[-1]

    c_all = _exchange(_pad8(c), False, "gather_c")[:, 0, :]
    c_mine = lax.dynamic_slice_in_dim(c_all, me * dsh, dsh, axis=1)
    pre_part = _mm(c_mine, cond_w, 'nn', "cond_fwd")
    pre = jnp.sum(_exchange(pre_part, False, "gather_pre"), axis=0) + cond_b
    e_all = jax.nn.silu(pre)
    mod_part = jnp.concatenate([_mm(e_all, ada_w[i], 'nn', "ada_fwd%d" % i) for i in range(DEPTH)], axis=0)
    mod_all = _exchange(mod_part, False, "gather_mod")
    mod_all = mod_all.reshape(N_DEV, DEPTH, N_DEV, msh)
    mods = lax.dynamic_index_in_dim(mod_all, me, axis=2, keepdims=False)
    mods = mods.transpose(1, 0, 2).reshape(DEPTH, N_DEV * msh) + ada_b

    big_names = list(BIG)
    packed = _pack_rows([w[n] for n in big_names], BF16, BIG_ROW_ALIGN)
    gathered = _exchange(packed, False, "gather_weights")
    blocks = _unpack_rows(gathered, [w[n].shape for n in big_names])
    p = {n: _join(b, BIG[n]).astype(F32) for n, b in zip(big_names, blocks)}
    sm_names = list(SMALL_SHARDED)
    sm_packed = _pack_rows([w[n] for n in sm_names], F32)
    sm_blocks = _unpack_rows(_exchange(sm_packed, False, "gather_small"), [w[n].shape for n in sm_names])
    for n, b in zip(sm_names, sm_blocks):
        p[n] = _join(b, SMALL_SHARDED[n])
    for n in SMALL_REPL:
        p[n] = w[n]
    n_qkvz = 4 * d
    p['gdn_w_qkvz'] = p['gdn_w_in'][:, :n_qkvz]
    p['gdn_w_ab'] = jnp.pad(p['gdn_w_in'][:, n_qkvz:], ((0, 0), (0, LANES - 2 * GDN_HEADS)))
    del p['gdn_w_in']

    loss_local, (dx, dmods, dp) = jax.value_and_grad(_local_loss, argnums=(0, 1, 2))(x, mods, p, target)
    dp['gdn_w_in'] = jnp.concatenate([dp.pop('gdn_w_qkvz'), dp.pop('gdn_w_ab')[:, :2 * GDN_HEADS]], axis=1)

    dmod_all = _exchange(dmods.reshape(-1, d), False, "gather_dmod").reshape(N_DEV, DEPTH, 6 * d)
    grads = {'ada_b': jnp.sum(dmod_all, axis=0)}
    dm_mine = lax.dynamic_slice_in_dim(dmod_all, me * msh, msh, axis=2)
    grads['ada_w'] = jnp.stack([_mm_outer(e_all, dm_mine[:, i], "ada_dw%d" % i) for i in range(DEPTH)])
    de_part = _mm(dm_mine[:, 0], ada_w[0], 'nt', "ada_de0")
    for i in range(1, DEPTH):
        de_part = de_part + _mm(dm_mine[:, i], ada_w[i], 'nt', "ada_de%d" % i)
    de_all = jnp.sum(_exchange(de_part, False, "gather_de"), axis=0)
    sig = jax.nn.sigmoid(pre)
    dpre = de_all * (sig * (1.0 + pre * (1.0 - sig)))
    grads['cond_b'] = jnp.sum(dpre, axis=0)
    grads['cond_w'] = _mm_outer(c_mine, dpre, "cond_dw")

    small_names = sm_names + SMALL_REPL
    small_packed = _pack_rows([loss_local.reshape(1)] + [dp[n] for n in small_names], F32)
    small_sum = _sum_slots(_exchange(small_packed, False, "gather_small_grads"), "sum_small_grads")
    small = _unpack_rows(small_sum, [(1,)] + [dp[n].shape for n in small_names])
    loss = small[0][0]
    for n, g in zip(small_names, small[1:]):
        if n in SMALL_SHARDED:
            ax = SMALL_SHARDED[n]
            g = lax.dynamic_slice_in_dim(g, me * w[n].shape[ax], w[n].shape[ax], axis=ax)
        grads[n] = g

    send = _pack_big_grads([dp[n] for n in big_names], [BIG[n] for n in big_names])
    recv = _exchange(send, True, "scatter_grads")
    shapes = [w[n].shape for n in big_names]
    outs = _adamw(recv, *[_pack_rows([t[n] for n in big_names], F32, BIG_ROW_ALIGN) for t in (w, mom, var)],
                  "adamw_big")
    g_b, d_b, m_b, v_b = (_unpack_rows(o, shapes) for o in outs)
    delta, new_m, new_v = {}, {}, {}
    for j, n in enumerate(big_names):
        grads[n], delta[n], new_m[n], new_v[n] = g_b[j], d_b[j], m_b[j], v_b[j]

    rest = [n for n in WEIGHTS if n not in BIG]
    shapes = [w[n].shape for n in rest]
    outs = _adamw(_pack_rows([grads[n] for n in rest], F32)[None], _pack_rows([w[n] for n in rest], F32),
                  _pack_rows([mom[n] for n in rest], F32), _pack_rows([var[n] for n in rest], F32), "adamw_rest")
    g_r, d_r, m_r, v_r = (_unpack_rows(o, shapes) for o in outs)
    for j, n in enumerate(rest):
        delta[n], new_m[n], new_v[n] = d_r[j], m_r[j], v_r[j]

    return (loss, dx[None], *[grads[n] for n in WEIGHTS], *[delta[n] for n in WEIGHTS],
            *[new_m[n] for n in WEIGHTS], *[new_v[n] for n in WEIGHTS])
```

```python
import functools
import math

import jax
import jax.numpy as jnp
from jax import lax
from jax.experimental import pallas as pl
from jax.experimental.pallas import tpu as pltpu

F32 = jnp.float32
BF16 = jnp.bfloat16
MXU_DTYPE = jnp.bfloat16
MESH = pl.DeviceIdType.MESH
N_DEV = 8
LANES = 128
SUBLANES = 8
VMEM_LIMIT = 48 * 1024 * 1024

DEPTH = 4
LN_EPS = 1e-5
DN_ALPHA = (2.0 * DEPTH) ** 0.25
GDN_HEADS, GDN_CONV, GDN_CHUNK = 8, 4, 64
RET_HEADS, RET_CHUNK, RET_ROPE_BASE = 4, 128, 10000.0
GMLP_CHUNK, GMLP_GROUPS = 128, 8
SB_HEADS = 16
ADAM_LR, ADAM_B1, ADAM_B2, ADAM_EPS, ADAM_WD, ADAM_STEP = 0.001, 0.9, 0.999, 1e-08, 0.01, 10

WEIGHTS = ['cond_w', 'cond_b', 'ada_w', 'ada_b', 'ln_g', 'ln_b', 'ffn_up', 'ffn_conv_w', 'ffn_conv_b', 'ffn_down',
           'gdn_w_in', 'gdn_conv_w', 'gdn_a_log', 'gdn_dt_bias', 'gdn_norm_w', 'gdn_w_out', 'ret_w_in', 'ret_w_out',
           'gmlp_w_in', 'gmlp_ln_g', 'gmlp_ln_b', 'gmlp_w_s', 'gmlp_b_s', 'gmlp_w_out', 'sb_w_in', 'sb_w_out']
BIG = {'ffn_up': 2, 'ffn_down': 1, 'gdn_w_in': 1, 'gdn_w_out': 0, 'ret_w_in': 1, 'ret_w_out': 0,
       'gmlp_w_in': 1, 'gmlp_w_out': 0, 'sb_w_in': 1, 'sb_w_out': 0}
SMALL_SHARDED = {'ln_g': 2, 'ln_b': 2, 'ffn_conv_w': 2, 'gdn_conv_w': 1}
SMALL_REPL = ['ffn_conv_b', 'gdn_a_log', 'gdn_dt_bias', 'gdn_norm_w', 'gmlp_ln_g', 'gmlp_ln_b', 'gmlp_w_s', 'gmlp_b_s']
MIXER_PARAMS = (('gdn_w_qkvz', 'gdn_w_ab', 'gdn_conv_w', 'gdn_a_log', 'gdn_dt_bias', 'gdn_norm_w', 'gdn_w_out'),
                ('ret_w_in', 'ret_w_out'),
                ('gmlp_w_in', 'gmlp_ln_g', 'gmlp_ln_b', 'gmlp_w_s', 'gmlp_b_s', 'gmlp_w_out'),
                ('sb_w_in', 'sb_w_out'))


def _pcall(body, **kw):
    return pl.pallas_call(body, **kw)


def _params(*semantics):
    return pltpu.CompilerParams(dimension_semantics=semantics, vmem_limit_bytes=VMEM_LIMIT)


def _my_id():
    return 4 * lax.axis_index("x") + 2 * lax.axis_index("y") + lax.axis_index("c")


def _pick(dim, prefs):
    for p in prefs:
        if dim % p == 0:
            return p
    return dim


def _exchange(src, scatter, name):
    blk = src.shape[1:] if scatter else src.shape
    out_shape = jax.ShapeDtypeStruct((N_DEV,) + tuple(blk), src.dtype)

    def body(src_ref, out_ref, send_sems, recv_sems, local_sem):
        x, y, c = lax.axis_index("x"), lax.axis_index("y"), lax.axis_index("c")
        me = 4 * x + 2 * y + c
        mine = pltpu.make_async_copy(src_ref.at[me] if scatter else src_ref, out_ref.at[me], local_sem)
        mine.start()
        copies = []
        for k in range(1, N_DEV):
            px = 1 - x if (k >> 2) & 1 else x
            py = 1 - y if (k >> 1) & 1 else y
            pc = 1 - c if k & 1 else c
            peer = 4 * px + 2 * py + pc
            cp = pltpu.make_async_remote_copy(
                src_ref=src_ref.at[peer] if scatter else src_ref,
                dst_ref=out_ref.at[me],
                send_sem=send_sems.at[k - 1], recv_sem=recv_sems.at[k - 1],
                device_id=(px, py, pc), device_id_type=MESH)
            cp.start()
            copies.append(cp)
        for cp in copies:
            cp.wait()
        mine.wait()

    return _pcall(
        body, name=name, out_shape=out_shape,
        in_specs=[pl.BlockSpec(memory_space=pl.ANY)],
        out_specs=pl.BlockSpec(memory_space=pl.ANY),
        scratch_shapes=[pltpu.SemaphoreType.DMA((N_DEV - 1,)), pltpu.SemaphoreType.DMA((N_DEV - 1,)),
                        pltpu.SemaphoreType.DMA(())],
    )(src)


ROW_ALIGN = 16
BIG_ROW_ALIGN = 512


def _pack_rows(parts, dtype, row_align=ROW_ALIGN):
    flat = jnp.concatenate([p.reshape(-1).astype(dtype) for p in parts])
    n = flat.shape[0]
    pad = (-n) % (row_align * LANES)
    if pad:
        flat = jnp.concatenate([flat, jnp.zeros((pad,), dtype)])
    return flat.reshape(-1, LANES)


def _unpack_rows(packed, shapes):
    lead = packed.shape[:-2]
    flat = packed.reshape(lead + (-1,))
    out, off = [], 0
    for s in shapes:
        n = math.prod(s)
        out.append(flat[..., off:off + n].reshape(lead + tuple(s)))
        off += n
    return out


def _mm(a, b, dims, name, exact=False):
    if dims == 'nn':
        (m, k), n = a.shape, b.shape[1]
    elif dims == 'nt':
        (m, k), n = a.shape, b.shape[0]
    else:
        (k, m), n = a.shape, b.shape[1]
    tm = _pick(m, (512, 256, 128))
    tn = _pick(n, (512, 256, 128))
    tk = k if k <= 2816 else _pick(k, (1024, 512, 256, 128))
    nk = k // tk
    if dims == 'nn':
        a_spec = pl.BlockSpec((tm, tk), lambda i, j, kk: (i, kk))
        b_spec = pl.BlockSpec((tk, tn), lambda i, j, kk: (kk, j))
        dn = (((1,), (0,)), ((), ()))
    elif dims == 'nt':
        a_spec = pl.BlockSpec((tm, tk), lambda i, j, kk: (i, kk))
        b_spec = pl.BlockSpec((tn, tk), lambda i, j, kk: (j, kk))
        dn = (((1,), (1,)), ((), ()))
    else:
        a_spec = pl.BlockSpec((tk, tm), lambda i, j, kk: (kk, i))
        b_spec = pl.BlockSpec((tk, tn), lambda i, j, kk: (kk, j))
        dn = (((0,), (0,)), ((), ()))

    def body(a_ref, b_ref, o_ref, acc_ref):
        kk = pl.program_id(2)

        @pl.when(kk == 0)
        def _():
            acc_ref[...] = jnp.zeros_like(acc_ref)

        if exact:
            acc_ref[...] += lax.dot_general(a_ref[...], b_ref[...], dn, precision=lax.Precision.HIGHEST,
                                            preferred_element_type=F32)
        else:
            acc_ref[...] += lax.dot_general(a_ref[...].astype(MXU_DTYPE), b_ref[...].astype(MXU_DTYPE), dn,
                                            preferred_element_type=F32)

        @pl.when(kk == nk - 1)
        def _():
            o_ref[...] = acc_ref[...]

    return _pcall(
        body, name=name, out_shape=jax.ShapeDtypeStruct((m, n), F32),
        grid=(m // tm, n // tn, nk),
        in_specs=[a_spec, b_spec],
        out_specs=pl.BlockSpec((tm, tn), lambda i, j, kk: (i, j)),
        scratch_shapes=[pltpu.VMEM((tm, tn), F32)],
        compiler_params=_params("parallel", "parallel", "arbitrary"),
    )(a, b)


def _mm_outer(a, b, name):
    pad = LANES - a.shape[0]
    return _mm(jnp.pad(a.T, ((0, 0), (0, pad))), jnp.pad(b, ((0, pad), (0, 0))), 'nn', name, exact=True)


@functools.partial(jax.custom_vjp, nondiff_argnums=(2,))
def _linear(a, w, name):
    return _mm(a, w, 'nn', name + "_fwd")


def _linear_fwd(a, w, name):
    return _mm(a, w, 'nn', name + "_fwd"), (a, w)


def _linear_bwd(name, res, dy):
    a, w = res
    return _mm(dy, w, 'nt', name + "_dx"), _mm(a, dy, 'tn', name + "_dw")


_linear.defvjp(_linear_fwd, _linear_bwd)


SB_BK = 128
SB_STRIP = 64


def _sb_tq(s):
    return _pick(s, (512, 256, 128))


def _sb_strip(t, sr):
    return slice(t * sr, (t + 1) * sr)


def _sb_valid(t, sr, bk, q0, k0):
    row = lax.broadcasted_iota(jnp.int32, (sr, bk), 0) + (q0 + t * sr)
    col = lax.broadcasted_iota(jnp.int32, (sr, bk), 1) + k0
    return col < row


def _sb_logits_phase(z_ref, ls_ref, hl_ref, rs_ref, tq, bk, sr, q0, k0, masked):
    for t in range(tq // sr):
        rows = _sb_strip(t, sr)
        z = z_ref[rows, :]
        ls = jnp.minimum(z, 0.0) - jnp.log(1.0 + jnp.exp(-jnp.abs(z)))
        lm = ls - z
        if masked:
            lm = jnp.where(_sb_valid(t, sr, bk, q0, k0), lm, 0.0)
        ls_ref[rows, :] = ls
        hi = lm.astype(BF16)
        hl_ref[rows, :] = hi
        hl_ref[tq + t * sr:tq + (t + 1) * sr, :] = (lm - hi.astype(F32)).astype(BF16)
        rs_ref[rows, :] = jnp.sum(lm, axis=1, keepdims=True)


def _sb_tri(bk, inclusive):
    r = lax.broadcasted_iota(jnp.int32, (bk, bk), 0)
    c = lax.broadcasted_iota(jnp.int32, (bk, bk), 1)
    return (r >= c).astype(BF16) if inclusive else (r > c).astype(BF16)


def _sb_fwd_call(q, kt, v):
    h, s, dh = q.shape
    tq, bk, sr = _sb_tq(s), SB_BK, SB_STRIP
    scale = dh ** -0.5
    nd = tq // bk

    def body(q_ref, kt_ref, v_ref, o_ref, z_ref, ls_ref, hl_ref, f_ref, a_ref, cl_ref, rs_ref, acc_ref):
        i = pl.program_id(1)
        q0 = i * tq
        cl_ref[...] = jnp.zeros_like(cl_ref)
        acc_ref[...] = jnp.zeros_like(acc_ref)
        u_excl = _sb_tri(bk, False)

        def step(j, masked):
            k0 = pl.multiple_of(j * bk, bk)
            z_ref[...] = jnp.dot(q_ref[...], kt_ref[:, pl.ds(k0, bk)], preferred_element_type=F32) * scale
            _sb_logits_phase(z_ref, ls_ref, hl_ref, rs_ref, tq, bk, sr, q0, k0, masked)
            f_ref[...] = jnp.dot(hl_ref[...], u_excl, preferred_element_type=F32)
            for t in range(tq // sr):
                rows = _sb_strip(t, sr)
                f = f_ref[rows, :] + f_ref[tq + t * sr:tq + (t + 1) * sr, :] + cl_ref[rows, :]
                a = jnp.exp(ls_ref[rows, :] + f)
                if masked:
                    a = jnp.where(_sb_valid(t, sr, bk, q0, k0), a, 0.0)
                a_ref[rows, :] = a.astype(a_ref.dtype)
            acc_ref[...] += jnp.dot(a_ref[...], v_ref[pl.ds(k0, bk), :], preferred_element_type=F32)
            cl_ref[...] += rs_ref[...]

        def diag(jj, c):
            step((i + 1) * nd - 1 - jj, True)
            return c

        def below(jj, c):
            step(i * nd - 1 - jj, False)
            return c

        lax.fori_loop(0, nd, diag, 0)
        lax.fori_loop(0, i * nd, below, 0)
        o_ref[...] = acc_ref[...]

    return _pcall(
        body, name="sb_fwd", out_shape=jax.ShapeDtypeStruct((h, s, dh), F32),
        grid=(h, s // tq),
        in_specs=[pl.BlockSpec((None, tq, dh), lambda hh, i: (hh, i, 0)),
                  pl.BlockSpec((None, dh, s), lambda hh, i: (hh, 0, 0)),
                  pl.BlockSpec((None, s, dh), lambda hh, i: (hh, 0, 0))],
        out_specs=pl.BlockSpec((None, tq, dh), lambda hh, i: (hh, i, 0)),
        scratch_shapes=[pltpu.VMEM((tq, bk), F32), pltpu.VMEM((tq, bk), F32), pltpu.VMEM((2 * tq, bk), BF16),
                        pltpu.VMEM((2 * tq, bk), F32), pltpu.VMEM((tq, bk), q.dtype),
                        pltpu.VMEM((tq, 1), F32), pltpu.VMEM((tq, 1), F32), pltpu.VMEM((tq, dh), F32)],
        compiler_params=_params("parallel", "arbitrary"),
    )(q, kt, v)


def _sb_bwd_call(q, qt, k, kt, vt, o, do, dot):
    h, s, dh = q.shape
    tq, bk, sr = _sb_tq(s), SB_BK, SB_STRIP
    scale = dh ** -0.5
    nd = tq // bk

    def body(q_ref, qt_ref, k_ref, kt_ref, vt_ref, o_ref, do_ref, dot_ref, dq_ref, dkt_ref, dvt_ref,
             z_ref, ls_ref, hl_ref, f_ref, a_ref, g_ref, dz_ref, da_ref, dob_ref,
             cl_ref, cg_ref, rs_ref, rg_ref, dl_ref, dqa_ref):
        i = pl.program_id(1)
        q0 = i * tq

        @pl.when(i == 0)
        def _():
            dkt_ref[...] = jnp.zeros_like(dkt_ref)
            dvt_ref[...] = jnp.zeros_like(dvt_ref)

        cl_ref[...] = jnp.zeros_like(cl_ref)
        cg_ref[...] = jnp.zeros_like(cg_ref)
        dqa_ref[...] = jnp.zeros_like(dqa_ref)
        dob = do_ref[...].astype(dob_ref.dtype)
        dob_ref[...] = dob
        dl_ref[...] = jnp.sum(dob.astype(F32) * o_ref[...], axis=1, keepdims=True)
        u_excl = _sb_tri(bk, False)
        u_incl = _sb_tri(bk, True)

        def step(j, masked):
            k0 = pl.multiple_of(j * bk, bk)
            z_ref[...] = jnp.dot(q_ref[...], kt_ref[:, pl.ds(k0, bk)], preferred_element_type=F32) * scale
            da_ref[...] = jnp.dot(dob_ref[...], vt_ref[:, pl.ds(k0, bk)], preferred_element_type=F32)
            _sb_logits_phase(z_ref, ls_ref, hl_ref, rs_ref, tq, bk, sr, q0, k0, masked)
            f_ref[...] = jnp.dot(hl_ref[...], u_excl, preferred_element_type=F32)
            for t in range(tq // sr):
                rows = _sb_strip(t, sr)
                hi_rows = slice(tq + t * sr, tq + (t + 1) * sr)
                f = f_ref[rows, :] + f_ref[hi_rows, :] + cl_ref[rows, :]
                a = jnp.exp(ls_ref[rows, :] + f)
                if masked:
                    a = jnp.where(_sb_valid(t, sr, bk, q0, k0), a, 0.0)
                ab = a.astype(a_ref.dtype)
                a_ref[rows, :] = ab
                g = da_ref[rows, :] * ab.astype(F32)
                g_ref[rows, :] = g
                ghi = g.astype(BF16)
                hl_ref[rows, :] = ghi
                hl_ref[hi_rows, :] = (g - ghi.astype(F32)).astype(BF16)
                rg_ref[rows, :] = jnp.sum(g, axis=1, keepdims=True)
            f_ref[...] = jnp.dot(hl_ref[...], u_incl, preferred_element_type=F32)
            for t in range(tq // sr):
                rows = _sb_strip(t, sr)
                sg = f_ref[rows, :] + f_ref[tq + t * sr:tq + (t + 1) * sr, :] + cg_ref[rows, :]
                p = dl_ref[rows, :] - sg
                sig = jnp.exp(ls_ref[rows, :])
                dz = g_ref[rows, :] * (1.0 - sig) - p * sig
                if masked:
                    dz = jnp.where(_sb_valid(t, sr, bk, q0, k0), dz, 0.0)
                dz_ref[rows, :] = (dz * scale).astype(dz_ref.dtype)
            dqa_ref[...] += jnp.dot(dz_ref[...], k_ref[pl.ds(k0, bk), :], preferred_element_type=F32)
            dkt_ref[:, pl.ds(k0, bk)] += jnp.dot(qt_ref[...], dz_ref[...], preferred_element_type=F32)
            dvt_ref[:, pl.ds(k0, bk)] += jnp.dot(dot_ref[...], a_ref[...], preferred_element_type=F32)
            cl_ref[...] += rs_ref[...]
            cg_ref[...] += rg_ref[...]

        def diag(jj, c):
            step((i + 1) * nd - 1 - jj, True)
            return c

        def below(jj, c):
            step(i * nd - 1 - jj, False)
            return c

        lax.fori_loop(0, nd, diag, 0)
        lax.fori_loop(0, i * nd, below, 0)
        dq_ref[...] = dqa_ref[...]

    blk_q = pl.BlockSpec((None, tq, dh), lambda hh, i: (hh, i, 0))
    blk_qt = pl.BlockSpec((None, dh, tq), lambda hh, i: (hh, 0, i))
    blk_s = pl.BlockSpec((None, s, dh), lambda hh, i: (hh, 0, 0))
    blk_st = pl.BlockSpec((None, dh, s), lambda hh, i: (hh, 0, 0))
    mx = q.dtype
    return _pcall(
        body, name="sb_bwd",
        out_shape=(jax.ShapeDtypeStruct((h, s, dh), F32), jax.ShapeDtypeStruct((h, dh, s), F32),
                   jax.ShapeDtypeStruct((h, dh, s), F32)),
        grid=(h, s // tq),
        in_specs=[blk_q, blk_qt, blk_s, blk_st, blk_st, blk_q, blk_q, blk_qt],
        out_specs=(blk_q, blk_st, blk_st),
        scratch_shapes=[pltpu.VMEM((tq, bk), F32), pltpu.VMEM((tq, bk), F32), pltpu.VMEM((2 * tq, bk), BF16),
                        pltpu.VMEM((2 * tq, bk), F32), pltpu.VMEM((tq, bk), mx), pltpu.VMEM((tq, bk), F32),
                        pltpu.VMEM((tq, bk), mx), pltpu.VMEM((tq, bk), F32), pltpu.VMEM((tq, dh), mx),
                        pltpu.VMEM((tq, 1), F32), pltpu.VMEM((tq, 1), F32), pltpu.VMEM((tq, 1), F32),
                        pltpu.VMEM((tq, 1), F32), pltpu.VMEM((tq, 1), F32), pltpu.VMEM((tq, dh), F32)],
        compiler_params=_params("parallel", "arbitrary"),
    )(q, qt, k, kt, vt, o, do, dot)


def _swap(t):
    return t.transpose(0, 2, 1)


@jax.custom_vjp
def _sb_core(q, k, v):
    return _sb_fwd_call(q.astype(MXU_DTYPE), _swap(k.astype(MXU_DTYPE)), v.astype(MXU_DTYPE))


def _sb_core_fwd(q, k, v):
    qb, kb, vb = q.astype(MXU_DTYPE), k.astype(MXU_DTYPE), v.astype(MXU_DTYPE)
    o = _sb_fwd_call(qb, _swap(kb), vb)
    return o, (qb, kb, vb, o)


def _sb_core_bwd(res, do):
    qb, kb, vb, o = res
    dq, dkt, dvt = _sb_bwd_call(qb, _swap(qb), kb, _swap(kb), _swap(vb), o, do, _swap(do.astype(MXU_DTYPE)))
    return dq, _swap(dkt), _swap(dvt)


_sb_core.defvjp(_sb_core_fwd, _sb_core_bwd)


def _row_block(s):
    return _pick(s, (512, 256, 128, 64, 32, 16, 8))


def _fold8(t):
    r, c = t.shape
    return jnp.sum(t.reshape(r // SUBLANES, SUBLANES, c), axis=0)


def _vec(a):
    return a.reshape(1, -1)


def _modulate(x, sc, sh, out_dtype, name):
    s, d = x.shape
    tr = _row_block(s)

    def body(x_ref, sc_ref, sh_ref, o_ref):
        o_ref[...] = (x_ref[...] * (1.0 + sc_ref[...]) + sh_ref[...]).astype(o_ref.dtype)

    row = pl.BlockSpec((tr, d), lambda i: (i, 0))
    vec = pl.BlockSpec((1, d), lambda i: (0, 0))
    return _pcall(body, name=name, out_shape=jax.ShapeDtypeStruct((s, d), out_dtype), grid=(s // tr,),
                  in_specs=[row, vec, vec], out_specs=row, compiler_params=_params("parallel"))(x, _vec(sc), _vec(sh))


def _modulate_bwd(dxa, dh, x, sc, name):
    s, d = x.shape
    tr = _row_block(s)

    def body(dxa_ref, dh_ref, x_ref, sc_ref, dx_ref, acc_ref):
        @pl.when(pl.program_id(0) == 0)
        def _():
            acc_ref[...] = jnp.zeros_like(acc_ref)

        dh = dh_ref[...]
        dx_ref[...] = dxa_ref[...] + dh * (1.0 + sc_ref[...])
        acc_ref[0] += _fold8(dh * x_ref[...])
        acc_ref[1] += _fold8(dh)

    row = pl.BlockSpec((tr, d), lambda i: (i, 0))
    vec = pl.BlockSpec((1, d), lambda i: (0, 0))
    dx, acc = _pcall(
        body, name=name,
        out_shape=(jax.ShapeDtypeStruct((s, d), F32), jax.ShapeDtypeStruct((2, SUBLANES, d), F32)),
        grid=(s // tr,), in_specs=[row, row, row, vec],
        out_specs=(row, pl.BlockSpec((2, SUBLANES, d), lambda i: (0, 0, 0))),
        compiler_params=_params("arbitrary"))(dxa, dh, x, _vec(sc))
    acc = jnp.sum(acc, axis=1)
    return dx, acc[0], acc[1]


def _resid_ln(x, y, g, gamma, beta, name):
    s, d = x.shape
    tr = _row_block(s)

    def body(x_ref, y_ref, g_ref, gam_ref, bet_ref, o_ref):
        u = DN_ALPHA * x_ref[...] + (1.0 + g_ref[...]) * y_ref[...]
        uc = u - jnp.mean(u, axis=-1, keepdims=True)
        var = jnp.mean(uc * uc, axis=-1, keepdims=True)
        o_ref[...] = uc * lax.rsqrt(var + LN_EPS) * gam_ref[...] + bet_ref[...]

    row = pl.BlockSpec((tr, d), lambda i: (i, 0))
    vec = pl.BlockSpec((1, d), lambda i: (0, 0))
    return _pcall(body, name=name, out_shape=jax.ShapeDtypeStruct((s, d), F32), grid=(s // tr,),
                  in_specs=[row, row, vec, vec, vec], out_specs=row,
                  compiler_params=_params("parallel"))(x, y, _vec(g), _vec(gamma), _vec(beta))


def _resid_ln_bwd(x, y, g, gamma, dout, name):
    s, d = x.shape
    tr = _row_block(s)

    def body(x_ref, y_ref, g_ref, gam_ref, do_ref, dxa_ref, dy_ref, acc_ref):
        @pl.when(pl.program_id(0) == 0)
        def _():
            acc_ref[...] = jnp.zeros_like(acc_ref)

        y = y_ref[...]
        gg = 1.0 + g_ref[...]
        u = DN_ALPHA * x_ref[...] + gg * y
        uc = u - jnp.mean(u, axis=-1, keepdims=True)
        rstd = lax.rsqrt(jnp.mean(uc * uc, axis=-1, keepdims=True) + LN_EPS)
        xhat = uc * rstd
        dout = do_ref[...]
        dxh = dout * gam_ref[...]
        du = rstd * (dxh - jnp.mean(dxh, axis=-1, keepdims=True)
                     - xhat * jnp.mean(dxh * xhat, axis=-1, keepdims=True))
        dxa_ref[...] = DN_ALPHA * du
        dy_ref[...] = gg * du
        acc_ref[0] += _fold8(dout * xhat)
        acc_ref[1] += _fold8(dout)
        acc_ref[2] += _fold8(du * y)

    row = pl.BlockSpec((tr, d), lambda i: (i, 0))
    vec = pl.BlockSpec((1, d), lambda i: (0, 0))
    dxa, dy, acc = _pcall(
        body, name=name,
        out_shape=(jax.ShapeDtypeStruct((s, d), F32), jax.ShapeDtypeStruct((s, d), F32),
                   jax.ShapeDtypeStruct((3, SUBLANES, d), F32)),
        grid=(s // tr,), in_specs=[row, row, vec, vec, row],
        out_specs=(row, row, pl.BlockSpec((3, SUBLANES, d), lambda i: (0, 0, 0))),
        compiler_params=_params("arbitrary"))(x, y, _vec(g), _vec(gamma), dout)
    acc = jnp.sum(acc, axis=1)
    return dxa, dy, acc[0], acc[1], acc[2]


def _loss_head(x, target, name):
    s, d = x.shape
    tr = _row_block(s)

    def body(x_ref, t_ref, dx_ref, acc_ref):
        @pl.when(pl.program_id(0) == 0)
        def _():
            acc_ref[...] = jnp.zeros_like(acc_ref)

        e = x_ref[...] - t_ref[...]
        dx_ref[...] = e * (1.0 / d)
        acc_ref[...] += _fold8(e * e)

    row = pl.BlockSpec((tr, d), lambda i: (i, 0))
    dx, acc = _pcall(
        body, name=name,
        out_shape=(jax.ShapeDtypeStruct((s, d), F32), jax.ShapeDtypeStruct((SUBLANES, d), F32)),
        grid=(s // tr,), in_specs=[row, row],
        out_specs=(row, pl.BlockSpec((SUBLANES, d), lambda i: (0, 0))),
        compiler_params=_params("arbitrary"))(x, target)
    return (0.5 / d) * jnp.sum(acc), dx


CONV_STRIPE = 128
CONV_ROWS = 256


def _shift_down(cur, halo, k):
    ext = jnp.concatenate([halo, cur], axis=0)
    return pltpu.roll(ext, k, 0)[SUBLANES:]


def _shift_up(cur, halo, k):
    ext = jnp.concatenate([cur, halo], axis=0)
    n = ext.shape[0]
    return pltpu.roll(ext, n - k, 0)[:n - SUBLANES]


def _gate_chunk(g_ref, r, rc):
    r0 = pl.multiple_of(r * rc, rc)
    cur = g_ref[pl.ds(r0, rc), :]
    hs = pl.multiple_of(jnp.maximum(r0 - SUBLANES, 0), SUBLANES)
    halo = jnp.where(r > 0, g_ref[pl.ds(hs, SUBLANES), :], 0.0)
    return r0, cur, _shift_down(cur, halo, 1), _shift_down(cur, halo, 2)


def _ffn_gate(gu, cw, cb, name):
    s, f2 = gu.shape
    f = f2 // 2
    tc = _pick(f, (CONV_STRIPE,))
    rc = _pick(s, (CONV_ROWS, 128, 64, 32, 16, 8))
    nj = f // tc

    def body(g_ref, u_ref, cw_ref, cb_ref, a_ref):
        w0, w1, w2, b = cw_ref[0:1, :], cw_ref[1:2, :], cw_ref[2:3, :], cb_ref[...]

        def chunk(r, c):
            r0, cur, x1, x2 = _gate_chunk(g_ref, r, rc)
            gc = w2 * cur + w1 * x1 + w0 * x2 + b
            a_ref[pl.ds(r0, rc), :] = (gc * jax.nn.sigmoid(gc) * u_ref[pl.ds(r0, rc), :]).astype(a_ref.dtype)
            return c

        lax.fori_loop(0, s // rc, chunk, 0)

    return _pcall(
        body, name=name, out_shape=jax.ShapeDtypeStruct((s, f), MXU_DTYPE), grid=(nj,),
        in_specs=[pl.BlockSpec((s, tc), lambda j: (0, j)), pl.BlockSpec((s, tc), lambda j: (0, j + nj)),
                  pl.BlockSpec((3, tc), lambda j: (0, j)), pl.BlockSpec((1, tc), lambda j: (0, j))],
        out_specs=pl.BlockSpec((s, tc), lambda j: (0, j)),
        compiler_params=_params("parallel"))(gu, gu, cw, _vec(cb))


def _ffn_gate_bwd(da, gu, cw, cb, name):
    s, f2 = gu.shape
    f = f2 // 2
    tc = _pick(f, (CONV_STRIPE,))
    rc = _pick(s, (CONV_ROWS, 128, 64, 32, 16, 8))
    nj = f // tc
    nr = s // rc

    def body(da_ref, g_ref, u_ref, cw_ref, cb_ref, dg_ref, du_ref, acc_ref, dgc_ref):
        w0, w1, w2, b = cw_ref[0:1, :], cw_ref[1:2, :], cw_ref[2:3, :], cb_ref[...]

        def chunk1(r, carry):
            a0, a1, a2, ab = carry
            r0, cur, x1, x2 = _gate_chunk(g_ref, r, rc)
            gc = w2 * cur + w1 * x1 + w0 * x2 + b
            sg = jax.nn.sigmoid(gc)
            da_c = da_ref[pl.ds(r0, rc), :]
            du_ref[pl.ds(r0, rc), :] = (da_c * (gc * sg)).astype(du_ref.dtype)
            dgc = da_c * u_ref[pl.ds(r0, rc), :] * (sg * (1.0 + gc * (1.0 - sg)))
            dgc_ref[pl.ds(r0, rc), :] = dgc
            return a0 + _fold8(dgc * x2), a1 + _fold8(dgc * x1), a2 + _fold8(dgc * cur), ab + _fold8(dgc)

        zero = jnp.zeros((SUBLANES, tc), F32)
        a0, a1, a2, ab = lax.fori_loop(0, nr, chunk1, (zero, zero, zero, zero))
        acc_ref[0], acc_ref[1], acc_ref[2], acc_ref[3] = a0, a1, a2, ab

        def chunk2(r, c):
            r0 = pl.multiple_of(r * rc, rc)
            cur = dgc_ref[pl.ds(r0, rc), :]
            hs = pl.multiple_of(jnp.minimum(r0 + rc, s - SUBLANES), SUBLANES)
            halo = jnp.where(r < nr - 1, dgc_ref[pl.ds(hs, SUBLANES), :], 0.0)
            dg = w2 * cur + w1 * _shift_up(cur, halo, 1) + w0 * _shift_up(cur, halo, 2)
            dg_ref[pl.ds(r0, rc), :] = dg.astype(dg_ref.dtype)
            return c

        lax.fori_loop(0, nr, chunk2, 0)

    stripe = pl.BlockSpec((s, tc), lambda j: (0, j))
    dg, du, acc = _pcall(
        body, name=name,
        out_shape=(jax.ShapeDtypeStruct((s, f), MXU_DTYPE), jax.ShapeDtypeStruct((s, f), MXU_DTYPE),
                   jax.ShapeDtypeStruct((4, SUBLANES, f), F32)),
        grid=(nj,),
        in_specs=[stripe, stripe, pl.BlockSpec((s, tc), lambda j: (0, j + nj)),
                  pl.BlockSpec((3, tc), lambda j: (0, j)), pl.BlockSpec((1, tc), lambda j: (0, j))],
        out_specs=(stripe, stripe, pl.BlockSpec((4, SUBLANES, tc), lambda j: (0, 0, j))),
        scratch_shapes=[pltpu.VMEM((s, tc), F32)],
        compiler_params=_params("parallel"))(da, gu, gu, cw, _vec(cb))
    acc = jnp.sum(acc, axis=1)
    return dg, du, acc[:3], acc[3]


def _adamw(gslots, w, m, v, name):
    n, r, _ = gslots.shape
    tr = _pick(r, (1024, 512, 256, 128, 64, 32, 16, 8))
    bc1 = 1.0 / (1.0 - ADAM_B1 ** ADAM_STEP)
    bc2 = 1.0 / (1.0 - ADAM_B2 ** ADAM_STEP)

    def body(g_ref, w_ref, m_ref, v_ref, go_ref, d_ref, mo_ref, vo_ref):
        g = g_ref[0]
        for t in range(1, n):
            g = g + g_ref[t]
        mn = ADAM_B1 * m_ref[...] + (1.0 - ADAM_B1) * g
        vn = ADAM_B2 * v_ref[...] + (1.0 - ADAM_B2) * (g * g)
        m_hat = mn * bc1
        v_hat = vn * bc2
        go_ref[...] = g
        d_ref[...] = -ADAM_LR * (m_hat / (jnp.sqrt(v_hat) + ADAM_EPS) + ADAM_WD * w_ref[...])
        mo_ref[...] = mn
        vo_ref[...] = vn

    row = pl.BlockSpec((tr, LANES), lambda i: (i, 0))
    sds = jax.ShapeDtypeStruct((r, LANES), F32)
    return _pcall(
        body, name=name, out_shape=(sds, sds, sds, sds),
        grid=(r // tr,),
        in_specs=[pl.BlockSpec((n, tr, LANES), lambda i: (0, i, 0)), row, row, row],
        out_specs=(row, row, row, row),
        compiler_params=_params("parallel"),
    )(gslots, w, m, v)


def _sum_slots(gslots, name):
    n, r, _ = gslots.shape
    tr = _pick(r, (1024, 512, 256, 128, 64, 32, 16, 8))

    def body(g_ref, o_ref):
        g = g_ref[0]
        for t in range(1, n):
            g = g + g_ref[t]
        o_ref[...] = g

    return _pcall(
        body, name=name, out_shape=jax.ShapeDtypeStruct((r, LANES), F32),
        grid=(r // tr,),
        in_specs=[pl.BlockSpec((n, tr, LANES), lambda i: (0, i, 0))],
        out_specs=pl.BlockSpec((tr, LANES), lambda i: (i, 0)),
        compiler_params=_params("parallel"),
    )(gslots)


def _standardize(x, eps):
    mu = jnp.mean(x, axis=-1, keepdims=True)
    xc = x - mu
    var = jnp.mean(xc * xc, axis=-1, keepdims=True)
    return xc * lax.rsqrt(var + eps)


def _layer_norm(x, g, b):
    return _standardize(x, LN_EPS) * g + b


def _l2norm(x, eps=1e-6):
    return x * lax.rsqrt(jnp.sum(x * x, axis=-1, keepdims=True) + eps)


def _causal_dwconv(x, w):
    k_w, s = w.shape[0], x.shape[0]
    xp = jnp.pad(x, ((k_w - 1, 0), (0, 0)))
    y = xp[k_w - 1:k_w - 1 + s] * w[k_w - 1]
    for j in range(k_w - 1):
        y = y + xp[j:j + s] * w[j]
    return y


def _chunk_heads(t, n_heads, chunk):
    s, hd = t.shape
    return t.reshape(s // chunk, chunk, n_heads, hd // n_heads).transpose(2, 0, 1, 3)


def _unchunk_heads(t):
    h, n, c, d = t.shape
    return t.transpose(1, 2, 0, 3).reshape(n * c, h, d)


def _gated_deltanet(hx, p):
    H, C = GDN_HEADS, GDN_CHUNK
    s, d = hx.shape
    dk = dv = d // H
    qkvz = _linear(hx, p['gdn_w_qkvz'], "gdn_in")
    ab = _linear(hx, p['gdn_w_ab'], "gdn_ab")
    qkv, z = qkvz[:, :3 * d], qkvz[:, 3 * d:]
    a, bt = ab[:, :H], ab[:, H:2 * H]
    qkv = jax.nn.silu(_causal_dwconv(qkv, p['gdn_conv_w']))
    q, k, v = qkv[:, :d], qkv[:, d:2 * d], qkv[:, 2 * d:]
    q = _l2norm(_chunk_heads(q, H, C)) * (dk ** -0.5)
    k = _l2norm(_chunk_heads(k, H, C))
    v = _chunk_heads(v, H, C)
    beta = jax.nn.sigmoid(_chunk_heads(bt, H, C)[..., 0])
    g = -jnp.exp(p['gdn_a_log'])[:, None, None] * jax.nn.softplus(
        _chunk_heads(a, H, C)[..., 0] + p['gdn_dt_bias'][:, None, None])
    gc = jnp.cumsum(g, axis=-1)
    idx = jnp.arange(C)
    causal = idx[:, None] >= idx[None, :]
    strict = idx[:, None] > idx[None, :]
    diff = gc[..., :, None] - gc[..., None, :]
    decay = jnp.where(causal, jnp.exp(jnp.where(causal, diff, 0.0)), 0.0)
    kb = k * beta[..., None]
    kk = jnp.where(strict, jnp.einsum('hncd,hnmd->hncm', kb, k) * decay, 0.0)
    eye = jnp.eye(C, dtype=F32)
    rhs = jnp.concatenate([v * beta[..., None], kb * jnp.exp(gc)[..., None]], axis=-1)
    sol = lax.linalg.triangular_solve(kk + eye, rhs, left_side=True, lower=True, unit_diagonal=True)
    u, w = sol[..., :dv], sol[..., dv:]
    qk = jnp.where(causal, jnp.einsum('hncd,hnmd->hncm', q, k) * decay, 0.0)

    def step(state, inp):
        q_n, k_n, u_n, w_n, qk_n, g_n = inp
        v_new = u_n - jnp.einsum('hck,hkv->hcv', w_n, state)
        o = (jnp.einsum('hck,hkv->hcv', q_n * jnp.exp(g_n)[..., None], state)
             + jnp.einsum('hcm,hmv->hcv', qk_n, v_new))
        g_last = g_n[..., -1:]
        state = (state * jnp.exp(g_last)[..., None]
                 + jnp.einsum('hck,hcv->hkv', k_n * jnp.exp(g_last - g_n)[..., None], v_new))
        return state, o

    xs = tuple(jnp.moveaxis(t, 1, 0) for t in (q, k, u, w, qk, gc))
    _, o = lax.scan(step, jnp.zeros((H, dk, dv), F32), xs)
    o = _unchunk_heads(jnp.moveaxis(o, 0, 1))
    o = o * lax.rsqrt(jnp.mean(o * o, axis=-1, keepdims=True) + 1e-6) * p['gdn_norm_w']
    o = o * jax.nn.silu(z.reshape(s, H, dv))
    return _linear(o.reshape(s, H * dv), p['gdn_w_out'], "gdn_out")


def _retention(hx, p):
    H, C = RET_HEADS, RET_CHUNK
    s, d = hx.shape
    dk, dv = d // H, 2 * d // H
    proj = _linear(hx, p['ret_w_in'], "ret_in")
    q, k, v, gate = proj[:, :d], proj[:, d:2 * d], proj[:, 2 * d:4 * d], proj[:, 4 * d:]
    pos = jnp.arange(s, dtype=F32)
    inv_freq = RET_ROPE_BASE ** (-jnp.linspace(0.0, 1.0, dk // 2, dtype=F32))
    ang = pos[:, None] * inv_freq[None, :]
    cos_a, sin_a = jnp.cos(ang)[:, None, :], jnp.sin(ang)[:, None, :]

    def rot(t):
        t = t.reshape(s, H, dk)
        t1, t2 = t[..., :dk // 2], t[..., dk // 2:]
        return jnp.concatenate([t1 * cos_a - t2 * sin_a, t1 * sin_a + t2 * cos_a], axis=-1).reshape(s, H * dk)

    q = _chunk_heads(rot(q), H, C)
    k = _chunk_heads(rot(k), H, C) * (dk ** -0.5)
    v = _chunk_heads(v, H, C)
    log_gamma = jnp.log(1.0 - jnp.power(2.0, -5.0 - jnp.arange(H, dtype=F32)))
    idx = jnp.arange(C, dtype=F32)
    rel = idx[:, None] - idx[None, :]
    dmask = jnp.where(rel >= 0, jnp.exp(jnp.maximum(rel, 0.0) * log_gamma[:, None, None]), 0.0)
    scores = jnp.einsum('hncd,hnmd->hncm', q, k) * dmask[:, None]
    intra = jnp.einsum('hncm,hnmv->hncv', scores, v)
    zeta = jnp.exp((C - 1.0 - idx)[None, :] * log_gamma[:, None])
    xi = jnp.exp((idx + 1.0)[None, :] * log_gamma[:, None])
    gamma_c = jnp.exp(C * log_gamma)

    def step(state, inp):
        q_n, k_n, v_n = inp
        o = jnp.einsum('hck,hkv->hcv', q_n, state) * xi[:, :, None]
        state = (state * gamma_c[:, None, None]
                 + jnp.einsum('hck,hcv->hkv', k_n * zeta[:, :, None], v_n))
        return state, o

    xs = tuple(jnp.moveaxis(t, 1, 0) for t in (q, k, v))
    _, inter = lax.scan(step, jnp.zeros((H, dk, dv), F32), xs)
    o = _unchunk_heads(intra + jnp.moveaxis(inter, 0, 1))
    o = _standardize(o, 1e-6).reshape(s, H * dv)
    o = o * jax.nn.silu(gate)
    return _linear(o, p['ret_w_out'], "ret_out")


def _chunked_gmlp(hx, p):
    C, G = GMLP_CHUNK, GMLP_GROUPS
    s, d = hx.shape
    W = 2 * d
    uv = jax.nn.gelu(_linear(hx, p['gmlp_w_in'], "gmlp_in"), approximate=False)
    u, v = uv[:, :W], uv[:, W:]
    v = _layer_norm(v, p['gmlp_ln_g'], p['gmlp_ln_b']).reshape(s // C, C, G, W // G)
    causal = jnp.tril(jnp.ones((C, C), dtype=bool))
    ws = jnp.where(causal, p['gmlp_w_s'], 0.0)
    vs = jnp.einsum('gts,nsgd->ntgd', ws, v) + p['gmlp_b_s'].T[None, :, :, None]
    return _linear(u * vs.reshape(s, W), p['gmlp_w_out'], "gmlp_out")


def _stick_breaking(hx, p):
    H = SB_HEADS
    s, d = hx.shape
    dh = d // H
    qkv = _linear(hx, p['sb_w_in'], "sb_in")
    q, k, v = (qkv[:, j * d:(j + 1) * d].reshape(s, H, dh).transpose(1, 0, 2) for j in range(3))
    o = _sb_core(q, k, v)
    return _linear(o.transpose(1, 0, 2).reshape(s, d), p['sb_w_out'], "sb_out")


MIXERS = (_gated_deltanet, _retention, _chunked_gmlp, _stick_breaking)


def _trunk_grad(x, mods, p, target):
    d = x.shape[-1]
    saved = []
    for i in range(DEPTH):
        sh1, sc1, g1, sh2, sc2, g2 = (mods[i, j * d:(j + 1) * d] for j in range(6))
        h1 = _modulate(x, sc1, sh1, F32, "mod_a%d" % i)
        y1, mixer_vjp = jax.vjp(MIXERS[i], h1, {n: p[n] for n in MIXER_PARAMS[i]})
        x1 = _resid_ln(x, y1, g1, p['ln_g'][i, 0], p['ln_b'][i, 0], "ln_a%d" % i)
        h2 = _modulate(x1, sc2, sh2, MXU_DTYPE, "mod_b%d" % i)
        gu = _mm(h2, p['ffn_up'][i], 'nn', "ffn_up%d_fwd" % i)
        act = _ffn_gate(gu, p['ffn_conv_w'][i], p['ffn_conv_b'][i], "ffn_gate%d" % i)
        y2 = _mm(act, p['ffn_down'][i], 'nn', "ffn_down%d_fwd" % i)
        x2 = _resid_ln(x1, y2, g2, p['ln_g'][i, 1], p['ln_b'][i, 1], "ln_b%d" % i)
        saved.append((x, y1, mixer_vjp, x1, h2, gu, act, y2))
        x = x2
    loss, dx = _loss_head(x, target, "loss_head")

    dp = {n: None for n in p}
    d_ln_g, d_ln_b, d_up, d_down, d_cw, d_cb, dmods = [], [], [], [], [], [], []
    for i in reversed(range(DEPTH)):
        x0, y1, mixer_vjp, x1, h2, gu, act, y2 = saved[i]
        sh1, sc1, g1, sh2, sc2, g2 = (mods[i, j * d:(j + 1) * d] for j in range(6))
        dxa, dy2, dgam2, dbet2, dg2 = _resid_ln_bwd(x1, y2, g2, p['ln_g'][i, 1], dx, "ln_b%d_bwd" % i)
        dact = _mm(dy2, p['ffn_down'][i], 'nt', "ffn_down%d_dx" % i)
        d_down.append(_mm(act, dy2, 'tn', "ffn_down%d_dw" % i))
        dgate, dupp, dcw, dcb = _ffn_gate_bwd(dact, gu, p['ffn_conv_w'][i], p['ffn_conv_b'][i],
                                              "ffn_gate%d_bwd" % i)
        dgu = jnp.concatenate([dgate, dupp], axis=1)
        dh2 = _mm(dgu, p['ffn_up'][i], 'nt', "ffn_up%d_dx" % i)
        d_up.append(_mm(h2, dgu, 'tn', "ffn_up%d_dw" % i))
        dx1, dsc2, dsh2 = _modulate_bwd(dxa, dh2, x1, sc2, "mod_b%d_bwd" % i)
        dxa, dy1, dgam1, dbet1, dg1 = _resid_ln_bwd(x0, y1, g1, p['ln_g'][i, 0], dx1, "ln_a%d_bwd" % i)
        dh1, dmix = mixer_vjp(dy1)
        dp.update(dmix)
        dx, dsc1, dsh1 = _modulate_bwd(dxa, dh1, x0, sc1, "mod_a%d_bwd" % i)
        d_ln_g.append(jnp.stack([dgam1, dgam2]))
        d_ln_b.append(jnp.stack([dbet1, dbet2]))
        d_cw.append(dcw)
        d_cb.append(dcb)
        dmods.append(jnp.concatenate([dsh1, dsc1, dg1, dsh2, dsc2, dg2]))
    for n, parts in (('ln_g', d_ln_g), ('ln_b', d_ln_b), ('ffn_up', d_up), ('ffn_down', d_down),
                     ('ffn_conv_w', d_cw), ('ffn_conv_b', d_cb)):
        dp[n] = jnp.stack(parts[::-1])
    return loss, dx, jnp.stack(dmods[::-1]), dp


def _join(blocks, axis):
    return jnp.concatenate([blocks[d] for d in range(N_DEV)], axis=axis)


def _split(whole, axis):
    n = whole.shape[axis] // N_DEV
    return jnp.stack([lax.slice_in_dim(whole, d * n, (d + 1) * n, axis=axis) for d in range(N_DEV)])


def _pad8(a):
    pad = (-a.shape[0]) % 8
    return jnp.pad(a, ((0, pad), (0, 0))) if pad else a


def _pack_big_grads(full_grads, axes):
    per_dev = jnp.concatenate([_split(g, ax).reshape(N_DEV, -1) for g, ax in zip(full_grads, axes)], axis=1)
    pad = (-per_dev.shape[1]) % (BIG_ROW_ALIGN * LANES)
    if pad:
        per_dev = jnp.pad(per_dev, ((0, 0), (0, pad)))
    return per_dev.reshape(N_DEV, -1, LANES)


def kernel(x, c, cond_w, cond_b, ada_w, ada_b, ln_g, ln_b, ffn_up, ffn_conv_w, ffn_conv_b, ffn_down, gdn_w_in, gdn_conv_w, gdn_a_log, gdn_dt_bias, gdn_norm_w, gdn_w_out, ret_w_in, ret_w_out, gmlp_w_in, gmlp_ln_g, gmlp_ln_b, gmlp_w_s, gmlp_b_s, gmlp_w_out, sb_w_in, sb_w_out, loss_target, m_cond_w, m_cond_b, m_ada_w, m_ada_b, m_ln_g, m_ln_b, m_ffn_up, m_ffn_conv_w, m_ffn_conv_b, m_ffn_down, m_gdn_w_in, m_gdn_conv_w, m_gdn_a_log, m_gdn_dt_bias, m_gdn_norm_w, m_gdn_w_out, m_ret_w_in, m_ret_w_out, m_gmlp_w_in, m_gmlp_ln_g, m_gmlp_ln_b, m_gmlp_w_s, m_gmlp_b_s, m_gmlp_w_out, m_sb_w_in, m_sb_w_out, v_cond_w, v_cond_b, v_ada_w, v_ada_b, v_ln_g, v_ln_b, v_ffn_up, v_ffn_conv_w, v_ffn_conv_b, v_ffn_down, v_gdn_w_in, v_gdn_conv_w, v_gdn_a_log, v_gdn_dt_bias, v_gdn_norm_w, v_gdn_w_out, v_ret_w_in, v_ret_w_out, v_gmlp_w_in, v_gmlp_ln_g, v_gmlp_ln_b, v_gmlp_w_s, v_gmlp_b_s, v_gmlp_w_out, v_sb_w_in, v_sb_w_out):
    w = dict(cond_w=cond_w, cond_b=cond_b, ada_w=ada_w, ada_b=ada_b, ln_g=ln_g, ln_b=ln_b, ffn_up=ffn_up,
             ffn_conv_w=ffn_conv_w, ffn_conv_b=ffn_conv_b, ffn_down=ffn_down, gdn_w_in=gdn_w_in,
             gdn_conv_w=gdn_conv_w, gdn_a_log=gdn_a_log, gdn_dt_bias=gdn_dt_bias, gdn_norm_w=gdn_norm_w,
             gdn_w_out=gdn_w_out, ret_w_in=ret_w_in, ret_w_out=ret_w_out, gmlp_w_in=gmlp_w_in,
             gmlp_ln_g=gmlp_ln_g, gmlp_ln_b=gmlp_ln_b, gmlp_w_s=gmlp_w_s, gmlp_b_s=gmlp_b_s,
             gmlp_w_out=gmlp_w_out, sb_w_in=sb_w_in, sb_w_out=sb_w_out)
    mom = dict(cond_w=m_cond_w, cond_b=m_cond_b, ada_w=m_ada_w, ada_b=m_ada_b, ln_g=m_ln_g, ln_b=m_ln_b,
               ffn_up=m_ffn_up, ffn_conv_w=m_ffn_conv_w, ffn_conv_b=m_ffn_conv_b, ffn_down=m_ffn_down,
               gdn_w_in=m_gdn_w_in, gdn_conv_w=m_gdn_conv_w, gdn_a_log=m_gdn_a_log, gdn_dt_bias=m_gdn_dt_bias,
               gdn_norm_w=m_gdn_norm_w, gdn_w_out=m_gdn_w_out, ret_w_in=m_ret_w_in, ret_w_out=m_ret_w_out,
               gmlp_w_in=m_gmlp_w_in, gmlp_ln_g=m_gmlp_ln_g, gmlp_ln_b=m_gmlp_ln_b, gmlp_w_s=m_gmlp_w_s,
               gmlp_b_s=m_gmlp_b_s, gmlp_w_out=m_gmlp_w_out, sb_w_in=m_sb_w_in, sb_w_out=m_sb_w_out)
    var = dict(cond_w=v_cond_w, cond_b=v_cond_b, ada_w=v_ada_w, ada_b=v_ada_b, ln_g=v_ln_g, ln_b=v_ln_b,
               ffn_up=v_ffn_up, ffn_conv_w=v_ffn_conv_w, ffn_conv_b=v_ffn_conv_b, ffn_down=v_ffn_down,
               gdn_w_in=v_gdn_w_in, gdn_conv_w=v_gdn_conv_w, gdn_a_log=v_gdn_a_log, gdn_dt_bias=v_gdn_dt_bias,
               gdn_norm_w=v_gdn_norm_w, gdn_w_out=v_gdn_w_out, ret_w_in=v_ret_w_in, ret_w_out=v_ret_w_out,
               gmlp_w_in=v_gmlp_w_in, gmlp_ln_g=v_gmlp_ln_g, gmlp_ln_b=v_gmlp_ln_b, gmlp_w_s=v_gmlp_w_s,
               gmlp_b_s=v_gmlp_b_s, gmlp_w_out=v_gmlp_w_out, sb_w_in=v_sb_w_in, sb_w_out=v_sb_w_out)

    me = _my_id()
    x = x[0]
    target = loss_target[0]
    d = x.shape[-1]
    dsh = d // N_DEV
    msh = ada_w.shape[-1]

    c_all = _exchange(_pad8(c), False, "gather_c")[:, 0, :]
    c_mine = lax.dynamic_slice_in_dim(c_all, me * dsh, dsh, axis=1)
    pre_part = _mm(c_mine, cond_w, 'nn', "cond_fwd")
    pre = jnp.sum(_exchange(pre_part, False, "gather_pre"), axis=0) + cond_b
    e_all = jax.nn.silu(pre)
    mod_part = jnp.concatenate([_mm(e_all, ada_w[i], 'nn', "ada_fwd%d" % i) for i in range(DEPTH)], axis=0)
    mod_all = _exchange(mod_part, False, "gather_mod")
    mod_all = mod_all.reshape(N_DEV, DEPTH, N_DEV, msh)
    mods = lax.dynamic_index_in_dim(mod_all, me, axis=2, keepdims=False)
    mods = mods.transpose(1, 0, 2).reshape(DEPTH, N_DEV * msh) + ada_b

    big_names = list(BIG)
    packed = _pack_rows([w[n] for n in big_names], BF16, BIG_ROW_ALIGN)
    gathered = _exchange(packed, False, "gather_weights")
    blocks = _unpack_rows(gathered, [w[n].shape for n in big_names])
    p = {n: _join(b, BIG[n]) for n, b in zip(big_names, blocks)}
    for n in big_names:
        if not n.startswith('ffn_'):
            p[n] = p[n].astype(F32)
    sm_names = list(SMALL_SHARDED)
    sm_packed = _pack_rows([w[n] for n in sm_names], F32)
    sm_blocks = _unpack_rows(_exchange(sm_packed, False, "gather_small"), [w[n].shape for n in sm_names])
    for n, b in zip(sm_names, sm_blocks):
        p[n] = _join(b, SMALL_SHARDED[n])
    for n in SMALL_REPL:
        p[n] = w[n]
    n_qkvz = 4 * d
    p['gdn_w_qkvz'] = p['gdn_w_in'][:, :n_qkvz]
    p['gdn_w_ab'] = jnp.pad(p['gdn_w_in'][:, n_qkvz:], ((0, 0), (0, LANES - 2 * GDN_HEADS)))
    del p['gdn_w_in']

    loss_local, dx, dmods, dp = _trunk_grad(x, mods, p, target)
    dp['gdn_w_in'] = jnp.concatenate([dp.pop('gdn_w_qkvz'), dp.pop('gdn_w_ab')[:, :2 * GDN_HEADS]], axis=1)

    dmod_all = _exchange(dmods.reshape(-1, d), False, "gather_dmod").reshape(N_DEV, DEPTH, 6 * d)
    grads = {'ada_b': jnp.sum(dmod_all, axis=0)}
    dm_mine = lax.dynamic_slice_in_dim(dmod_all, me * msh, msh, axis=2)
    grads['ada_w'] = jnp.stack([_mm_outer(e_all, dm_mine[:, i], "ada_dw%d" % i) for i in range(DEPTH)])
    de_part = _mm(dm_mine[:, 0], ada_w[0], 'nt', "ada_de0")
    for i in range(1, DEPTH):
        de_part = de_part + _mm(dm_mine[:, i], ada_w[i], 'nt', "ada_de%d" % i)
    de_all = jnp.sum(_exchange(de_part, False, "gather_de"), axis=0)
    sig = jax.nn.sigmoid(pre)
    dpre = de_all * (sig * (1.0 + pre * (1.0 - sig)))
    grads['cond_b'] = jnp.sum(dpre, axis=0)
    grads['cond_w'] = _mm_outer(c_mine, dpre, "cond_dw")

    small_names = sm_names + SMALL_REPL
    small_packed = _pack_rows([loss_local.reshape(1)] + [dp[n] for n in small_names], F32)
    small_sum = _sum_slots(_exchange(small_packed, False, "gather_small_grads"), "sum_small_grads")
    small = _unpack_rows(small_sum, [(1,)] + [dp[n].shape for n in small_names])
    loss = small[0][0]
    for n, g in zip(small_names, small[1:]):
        if n in SMALL_SHARDED:
            ax = SMALL_SHARDED[n]
            g = lax.dynamic_slice_in_dim(g, me * w[n].shape[ax], w[n].shape[ax], axis=ax)
        grads[n] = g

    send = _pack_big_grads([dp[n] for n in big_names], [BIG[n] for n in big_names])
    recv = _exchange(send, True, "scatter_grads")
    shapes = [w[n].shape for n in big_names]
    outs = _adamw(recv, *[_pack_rows([t[n] for n in big_names], F32, BIG_ROW_ALIGN) for t in (w, mom, var)],
                  "adamw_big")
    g_b, d_b, m_b, v_b = (_unpack_rows(o, shapes) for o in outs)
    delta, new_m, new_v = {}, {}, {}
    for j, n in enumerate(big_names):
        grads[n], delta[n], new_m[n], new_v[n] = g_b[j], d_b[j], m_b[j], v_b[j]

    rest = [n for n in WEIGHTS if n not in BIG]
    shapes = [w[n].shape for n in rest]
    outs = _adamw(_pack_rows([grads[n] for n in rest], F32, BIG_ROW_ALIGN)[None],
                  *[_pack_rows([t[n] for n in rest], F32, BIG_ROW_ALIGN) for t in (w, mom, var)], "adamw_rest")
    _, d_r, m_r, v_r = (_unpack_rows(o, shapes) for o in outs)
    for j, n in enumerate(rest):
        delta[n], new_m[n], new_v[n] = d_r[j], m_r[j], v_r[j]

    return (loss, dx[None], *[grads[n] for n in WEIGHTS], *[delta[n] for n in WEIGHTS],
            *[new_m[n] for n in WEIGHTS], *[new_v[n] for n in WEIGHTS])
```

```python
import functools
import math

import jax
import jax.numpy as jnp
from jax import lax
from jax.experimental import pallas as pl
from jax.experimental.pallas import tpu as pltpu

F32 = jnp.float32
BF16 = jnp.bfloat16
MXU_DTYPE = jnp.bfloat16
MESH = pl.DeviceIdType.MESH
N_DEV = 8
LANES = 128
SUBLANES = 8
VMEM_LIMIT = 48 * 1024 * 1024

DEPTH = 4
LN_EPS = 1e-5
DN_ALPHA = (2.0 * DEPTH) ** 0.25
GDN_HEADS, GDN_CONV, GDN_CHUNK = 8, 4, 64
RET_HEADS, RET_CHUNK, RET_ROPE_BASE = 4, 128, 10000.0
GMLP_CHUNK, GMLP_GROUPS = 128, 8
SB_HEADS = 16
ADAM_LR, ADAM_B1, ADAM_B2, ADAM_EPS, ADAM_WD, ADAM_STEP = 0.001, 0.9, 0.999, 1e-08, 0.01, 10

WEIGHTS = ['cond_w', 'cond_b', 'ada_w', 'ada_b', 'ln_g', 'ln_b', 'ffn_up', 'ffn_conv_w', 'ffn_conv_b', 'ffn_down',
           'gdn_w_in', 'gdn_conv_w', 'gdn_a_log', 'gdn_dt_bias', 'gdn_norm_w', 'gdn_w_out', 'ret_w_in', 'ret_w_out',
           'gmlp_w_in', 'gmlp_ln_g', 'gmlp_ln_b', 'gmlp_w_s', 'gmlp_b_s', 'gmlp_w_out', 'sb_w_in', 'sb_w_out']
BIG = {'ffn_up': 2, 'ffn_down': 1, 'gdn_w_in': 1, 'gdn_w_out': 0, 'ret_w_in': 1, 'ret_w_out': 0,
       'gmlp_w_in': 1, 'gmlp_w_out': 0, 'sb_w_in': 1, 'sb_w_out': 0}
SMALL_SHARDED = {'ln_g': 2, 'ln_b': 2, 'ffn_conv_w': 2, 'gdn_conv_w': 1}
SMALL_REPL = ['ffn_conv_b', 'gdn_a_log', 'gdn_dt_bias', 'gdn_norm_w', 'gmlp_ln_g', 'gmlp_ln_b', 'gmlp_w_s', 'gmlp_b_s']
MIXER_PARAMS = (('gdn_w_qkvz', 'gdn_w_ab', 'gdn_conv_w', 'gdn_a_log', 'gdn_dt_bias', 'gdn_norm_w', 'gdn_w_out'),
                ('ret_w_in', 'ret_w_out'),
                ('gmlp_w_in', 'gmlp_ln_g', 'gmlp_ln_b', 'gmlp_w_s', 'gmlp_b_s', 'gmlp_w_out'),
                ('sb_w_in', 'sb_w_out'))


def _pcall(body, **kw):
    return pl.pallas_call(body, **kw)


def _params(*semantics):
    return pltpu.CompilerParams(dimension_semantics=semantics, vmem_limit_bytes=VMEM_LIMIT)


def _my_id():
    return 4 * lax.axis_index("x") + 2 * lax.axis_index("y") + lax.axis_index("c")


def _pick(dim, prefs):
    for p in prefs:
        if dim % p == 0:
            return p
    return dim


def _exchange(src, scatter, name):
    blk = src.shape[1:] if scatter else src.shape
    out_shape = jax.ShapeDtypeStruct((N_DEV,) + tuple(blk), src.dtype)

    def body(src_ref, out_ref, send_sems, recv_sems, local_sem):
        x, y, c = lax.axis_index("x"), lax.axis_index("y"), lax.axis_index("c")
        me = 4 * x + 2 * y + c
        mine = pltpu.make_async_copy(src_ref.at[me] if scatter else src_ref, out_ref.at[me], local_sem)
        mine.start()
        copies = []
        for k in range(1, N_DEV):
            px = 1 - x if (k >> 2) & 1 else x
            py = 1 - y if (k >> 1) & 1 else y
            pc = 1 - c if k & 1 else c
            peer = 4 * px + 2 * py + pc
            cp = pltpu.make_async_remote_copy(
                src_ref=src_ref.at[peer] if scatter else src_ref,
                dst_ref=out_ref.at[me],
                send_sem=send_sems.at[k - 1], recv_sem=recv_sems.at[k - 1],
                device_id=(px, py, pc), device_id_type=MESH)
            cp.start()
            copies.append(cp)
        for cp in copies:
            cp.wait()
        mine.wait()

    return _pcall(
        body, name=name, out_shape=out_shape,
        in_specs=[pl.BlockSpec(memory_space=pl.ANY)],
        out_specs=pl.BlockSpec(memory_space=pl.ANY),
        scratch_shapes=[pltpu.SemaphoreType.DMA((N_DEV - 1,)), pltpu.SemaphoreType.DMA((N_DEV - 1,)),
                        pltpu.SemaphoreType.DMA(())],
    )(src)


ROW_ALIGN = 16
BIG_ROW_ALIGN = 512


def _pack_rows(parts, dtype, row_align=ROW_ALIGN):
    flat = jnp.concatenate([p.reshape(-1).astype(dtype) for p in parts])
    n = flat.shape[0]
    pad = (-n) % (row_align * LANES)
    if pad:
        flat = jnp.concatenate([flat, jnp.zeros((pad,), dtype)])
    return flat.reshape(-1, LANES)


def _unpack_rows(packed, shapes):
    lead = packed.shape[:-2]
    flat = packed.reshape(lead + (-1,))
    out, off = [], 0
    for s in shapes:
        n = math.prod(s)
        out.append(flat[..., off:off + n].reshape(lead + tuple(s)))
        off += n
    return out


def _mm(a, b, dims, name, exact=False):
    if dims == 'nn':
        (m, k), n = a.shape, b.shape[1]
    elif dims == 'nt':
        (m, k), n = a.shape, b.shape[0]
    else:
        (k, m), n = a.shape, b.shape[1]
    tm = _pick(m, (1024, 512, 256, 128))
    tn = _pick(n, (512, 256, 128))
    tk = k if k <= 2816 else _pick(k, (1024, 512, 256, 128))
    nk = k // tk
    if dims == 'nn':
        a_spec = pl.BlockSpec((tm, tk), lambda i, j, kk: (i, kk))
        b_spec = pl.BlockSpec((tk, tn), lambda i, j, kk: (kk, j))
        dn = (((1,), (0,)), ((), ()))
    elif dims == 'nt':
        a_spec = pl.BlockSpec((tm, tk), lambda i, j, kk: (i, kk))
        b_spec = pl.BlockSpec((tn, tk), lambda i, j, kk: (j, kk))
        dn = (((1,), (1,)), ((), ()))
    else:
        a_spec = pl.BlockSpec((tk, tm), lambda i, j, kk: (kk, i))
        b_spec = pl.BlockSpec((tk, tn), lambda i, j, kk: (kk, j))
        dn = (((0,), (0,)), ((), ()))

    def product(a_ref, b_ref):
        if exact:
            return lax.dot_general(a_ref[...], b_ref[...], dn, precision=lax.Precision.HIGHEST,
                                   preferred_element_type=F32)
        return lax.dot_general(a_ref[...].astype(MXU_DTYPE), b_ref[...].astype(MXU_DTYPE), dn,
                               preferred_element_type=F32)

    def body(a_ref, b_ref, o_ref, *acc):
        if nk == 1:
            o_ref[...] = product(a_ref, b_ref)
            return
        acc_ref, = acc
        kk = pl.program_id(2)

        @pl.when(kk == 0)
        def _():
            acc_ref[...] = jnp.zeros_like(acc_ref)

        acc_ref[...] += product(a_ref, b_ref)

        @pl.when(kk == nk - 1)
        def _():
            o_ref[...] = acc_ref[...]

    return _pcall(
        body, name=name, out_shape=jax.ShapeDtypeStruct((m, n), F32),
        grid=(m // tm, n // tn, nk),
        in_specs=[a_spec, b_spec],
        out_specs=pl.BlockSpec((tm, tn), lambda i, j, kk: (i, j)),
        scratch_shapes=[pltpu.VMEM((tm, tn), F32)] if nk > 1 else [],
        compiler_params=_params("parallel", "parallel", "arbitrary"),
    )(a, b)


def _mm_outer(a, b, name):
    pad = LANES - a.shape[0]
    return _mm(jnp.pad(a.T, ((0, 0), (0, pad))), jnp.pad(b, ((0, pad), (0, 0))), 'nn', name, exact=True)


@functools.partial(jax.custom_vjp, nondiff_argnums=(2,))
def _linear(a, w, name):
    return _mm(a, w, 'nn', name + "_fwd")


def _linear_fwd(a, w, name):
    return _mm(a, w, 'nn', name + "_fwd"), (a, w)


def _linear_bwd(name, res, dy):
    a, w = res
    return _mm(dy, w, 'nt', name + "_dx"), _mm(a, dy, 'tn', name + "_dw")


_linear.defvjp(_linear_fwd, _linear_bwd)


SB_BK = 256
SB_STRIP = 32


def _sb_tiles(s):
    tq = _pick(s, (512, 256, 128))
    bk = min(SB_BK, tq)
    return tq, bk, tq // bk


def _sb_valid(t, sr, bk, q0, k0):
    row = lax.broadcasted_iota(jnp.int32, (sr, bk), 0) + (q0 + t * sr)
    col = lax.broadcasted_iota(jnp.int32, (sr, bk), 1) + k0
    return col < row


def _sb_tri(bk, inclusive):
    r = lax.broadcasted_iota(jnp.int32, (bk, bk), 0)
    c = lax.broadcasted_iota(jnp.int32, (bk, bk), 1)
    return (r >= c).astype(BF16) if inclusive else (r > c).astype(BF16)


def _sb_split(ref, n, rows, hi_rows, val):
    hi = val.astype(BF16)
    ref[n, rows, :] = hi
    ref[n, hi_rows, :] = (val - hi.astype(F32)).astype(BF16)


def _sb_logits_phase(z_ref, ls_ref, hl_ref, l0_ref, n, tq, bk, sr, scale, q0, k0, masked):
    for t in range(tq // sr):
        rows = slice(t * sr, (t + 1) * sr)
        z = z_ref[n, rows, :] * scale
        ls = jnp.minimum(z, 0.0) - jnp.log(1.0 + jnp.exp(-jnp.abs(z)))
        lm = ls - z
        if masked:
            lm = jnp.where(_sb_valid(t, sr, bk, q0, k0), lm, 0.0)
        ls_ref[n, rows, :] = ls
        _sb_split(hl_ref, n, rows, slice(tq + t * sr, tq + (t + 1) * sr), lm)
        l0_ref[n, rows, :] = lm[:, 0:1]


def _sb_fwd_call(q, kt, v):
    h, s, dh = q.shape
    tq, bk, nt = _sb_tiles(s)
    sr = SB_STRIP
    scale = dh ** -0.5

    def body(q_ref, kt_ref, v_ref, o_ref, z_ref, ls_ref, hl_ref, f_ref, a_ref, l0_ref, cl_ref, acc_ref):
        i = pl.program_id(1)
        q0 = i * tq
        cl_ref[...] = jnp.zeros_like(cl_ref)
        acc_ref[...] = jnp.zeros_like(acc_ref)
        u_excl = _sb_tri(bk, False)

        def iteration(kb0, masked):
            k0s = [pl.multiple_of(kb0 + (nt - 1 - n) * bk, bk) for n in range(nt)]
            for n in range(nt):
                z_ref[n] = jnp.dot(q_ref[...], kt_ref[:, pl.ds(k0s[n], bk)], preferred_element_type=F32)
            for n in range(nt):
                _sb_logits_phase(z_ref, ls_ref, hl_ref, l0_ref, n, tq, bk, sr, scale, q0, k0s[n], masked)
            for n in range(nt):
                f_ref[n] = jnp.dot(hl_ref[n], u_excl, preferred_element_type=F32)
            for t in range(tq // sr):
                rows = slice(t * sr, (t + 1) * sr)
                hi_rows = slice(tq + t * sr, tq + (t + 1) * sr)
                c = cl_ref[rows, :]
                for n in range(nt):
                    f = f_ref[n, rows, :] + f_ref[n, hi_rows, :]
                    a = jnp.exp(ls_ref[n, rows, :] + f + c)
                    if masked:
                        a = jnp.where(_sb_valid(t, sr, bk, q0, k0s[n]), a, 0.0)
                    a_ref[n, rows, :] = a.astype(a_ref.dtype)
                    c = c + f[:, 0:1] + l0_ref[n, rows, :]
                cl_ref[rows, :] = c
            for n in range(nt):
                acc_ref[...] += jnp.dot(a_ref[n], v_ref[pl.ds(k0s[n], bk), :], preferred_element_type=F32)

        def below(jj, c):
            iteration((i - 1 - jj) * tq, False)
            return c

        iteration(q0, True)
        lax.fori_loop(0, i, below, 0)
        o_ref[...] = acc_ref[...]

    return _pcall(
        body, name="sb_fwd", out_shape=jax.ShapeDtypeStruct((h, s, dh), F32),
        grid=(h, s // tq),
        in_specs=[pl.BlockSpec((None, tq, dh), lambda hh, i: (hh, i, 0)),
                  pl.BlockSpec((None, dh, s), lambda hh, i: (hh, 0, 0)),
                  pl.BlockSpec((None, s, dh), lambda hh, i: (hh, 0, 0))],
        out_specs=pl.BlockSpec((None, tq, dh), lambda hh, i: (hh, i, 0)),
        scratch_shapes=[pltpu.VMEM((nt, tq, bk), F32), pltpu.VMEM((nt, tq, bk), F32),
                        pltpu.VMEM((nt, 2 * tq, bk), BF16), pltpu.VMEM((nt, 2 * tq, bk), F32),
                        pltpu.VMEM((nt, tq, bk), q.dtype), pltpu.VMEM((nt, tq, 1), F32),
                        pltpu.VMEM((tq, 1), F32), pltpu.VMEM((tq, dh), F32)],
        compiler_params=_params("parallel", "arbitrary"),
    )(q, kt, v)


def _sb_bwd_call(q, qt, k, kt, vt, o, do, dot):
    h, s, dh = q.shape
    tq, bk, nt = _sb_tiles(s)
    sr = SB_STRIP
    scale = dh ** -0.5

    def body(q_ref, qt_ref, k_ref, kt_ref, vt_ref, o_ref, do_ref, dot_ref, dq_ref, dkt_ref, dvt_ref,
             z_ref, ls_ref, hl_ref, f_ref, a_ref, g_ref, dz_ref, da_ref, l0_ref, dob_ref,
             cl_ref, cg_ref, dl_ref, dqa_ref):
        i = pl.program_id(1)
        q0 = i * tq

        @pl.when(i == 0)
        def _():
            dkt_ref[...] = jnp.zeros_like(dkt_ref)
            dvt_ref[...] = jnp.zeros_like(dvt_ref)

        cl_ref[...] = jnp.zeros_like(cl_ref)
        cg_ref[...] = jnp.zeros_like(cg_ref)
        dqa_ref[...] = jnp.zeros_like(dqa_ref)
        dob = do_ref[...].astype(dob_ref.dtype)
        dob_ref[...] = dob
        dl_ref[...] = jnp.sum(dob.astype(F32) * o_ref[...], axis=1, keepdims=True)
        u_excl = _sb_tri(bk, False)
        u_incl = _sb_tri(bk, True)

        def iteration(kb0, masked):
            k0s = [pl.multiple_of(kb0 + (nt - 1 - n) * bk, bk) for n in range(nt)]
            for n in range(nt):
                z_ref[n] = jnp.dot(q_ref[...], kt_ref[:, pl.ds(k0s[n], bk)], preferred_element_type=F32)
                da_ref[n] = jnp.dot(dob_ref[...], vt_ref[:, pl.ds(k0s[n], bk)], preferred_element_type=F32)
            for n in range(nt):
                _sb_logits_phase(z_ref, ls_ref, hl_ref, l0_ref, n, tq, bk, sr, scale, q0, k0s[n], masked)
            for n in range(nt):
                f_ref[n] = jnp.dot(hl_ref[n], u_excl, preferred_element_type=F32)
            for t in range(tq // sr):
                rows = slice(t * sr, (t + 1) * sr)
                hi_rows = slice(tq + t * sr, tq + (t + 1) * sr)
                c = cl_ref[rows, :]
                for n in range(nt):
                    f = f_ref[n, rows, :] + f_ref[n, hi_rows, :]
                    a = jnp.exp(ls_ref[n, rows, :] + f + c)
                    if masked:
                        a = jnp.where(_sb_valid(t, sr, bk, q0, k0s[n]), a, 0.0)
                    ab = a.astype(a_ref.dtype)
                    a_ref[n, rows, :] = ab
                    g = da_ref[n, rows, :] * ab.astype(F32)
                    g_ref[n, rows, :] = g
                    _sb_split(hl_ref, n, rows, hi_rows, g)
                    c = c + f[:, 0:1] + l0_ref[n, rows, :]
                cl_ref[rows, :] = c
            for n in range(nt):
                f_ref[n] = jnp.dot(hl_ref[n], u_incl, preferred_element_type=F32)
            for t in range(tq // sr):
                rows = slice(t * sr, (t + 1) * sr)
                hi_rows = slice(tq + t * sr, tq + (t + 1) * sr)
                cg = cg_ref[rows, :]
                for n in range(nt):
                    sg_tile = f_ref[n, rows, :] + f_ref[n, hi_rows, :]
                    p = dl_ref[rows, :] - (sg_tile + cg)
                    sig = jnp.exp(ls_ref[n, rows, :])
                    dz = g_ref[n, rows, :] * (1.0 - sig) - p * sig
                    if masked:
                        dz = jnp.where(_sb_valid(t, sr, bk, q0, k0s[n]), dz, 0.0)
                    dz_ref[n, rows, :] = (dz * scale).astype(dz_ref.dtype)
                    cg = cg + sg_tile[:, 0:1]
                cg_ref[rows, :] = cg
            for n in range(nt):
                cols = pl.ds(k0s[n], bk)
                dqa_ref[...] += jnp.dot(dz_ref[n], k_ref[cols, :], preferred_element_type=F32)
                dkt_ref[:, cols] += jnp.dot(qt_ref[...], dz_ref[n], preferred_element_type=F32)
                dvt_ref[:, cols] += jnp.dot(dot_ref[...], a_ref[n], preferred_element_type=F32)

        def below(jj, c):
            iteration((i - 1 - jj) * tq, False)
            return c

        iteration(q0, True)
        lax.fori_loop(0, i, below, 0)
        dq_ref[...] = dqa_ref[...]

    blk_q = pl.BlockSpec((None, tq, dh), lambda hh, i: (hh, i, 0))
    blk_qt = pl.BlockSpec((None, dh, tq), lambda hh, i: (hh, 0, i))
    blk_s = pl.BlockSpec((None, s, dh), lambda hh, i: (hh, 0, 0))
    blk_st = pl.BlockSpec((None, dh, s), lambda hh, i: (hh, 0, 0))
    mx = q.dtype
    return _pcall(
        body, name="sb_bwd",
        out_shape=(jax.ShapeDtypeStruct((h, s, dh), F32), jax.ShapeDtypeStruct((h, dh, s), F32),
                   jax.ShapeDtypeStruct((h, dh, s), F32)),
        grid=(h, s // tq),
        in_specs=[blk_q, blk_qt, blk_s, blk_st, blk_st, blk_q, blk_q, blk_qt],
        out_specs=(blk_q, blk_st, blk_st),
        scratch_shapes=[pltpu.VMEM((nt, tq, bk), F32), pltpu.VMEM((nt, tq, bk), F32),
                        pltpu.VMEM((nt, 2 * tq, bk), BF16), pltpu.VMEM((nt, 2 * tq, bk), F32),
                        pltpu.VMEM((nt, tq, bk), mx), pltpu.VMEM((nt, tq, bk), F32),
                        pltpu.VMEM((nt, tq, bk), mx), pltpu.VMEM((nt, tq, bk), F32),
                        pltpu.VMEM((nt, tq, 1), F32), pltpu.VMEM((tq, dh), mx),
                        pltpu.VMEM((tq, 1), F32), pltpu.VMEM((tq, 1), F32), pltpu.VMEM((tq, 1), F32),
                        pltpu.VMEM((tq, dh), F32)],
        compiler_params=_params("parallel", "arbitrary"),
    )(q, qt, k, kt, vt, o, do, dot)


def _swap(t):
    return t.transpose(0, 2, 1)


@jax.custom_vjp
def _sb_core(q, k, v):
    return _sb_fwd_call(q.astype(MXU_DTYPE), _swap(k.astype(MXU_DTYPE)), v.astype(MXU_DTYPE))


def _sb_core_fwd(q, k, v):
    qb, kb, vb = q.astype(MXU_DTYPE), k.astype(MXU_DTYPE), v.astype(MXU_DTYPE)
    o = _sb_fwd_call(qb, _swap(kb), vb)
    return o, (qb, kb, vb, o)


def _sb_core_bwd(res, do):
    qb, kb, vb, o = res
    dq, dkt, dvt = _sb_bwd_call(qb, _swap(qb), kb, _swap(kb), _swap(vb), o, do, _swap(do.astype(MXU_DTYPE)))
    return dq, _swap(dkt), _swap(dvt)


_sb_core.defvjp(_sb_core_fwd, _sb_core_bwd)


def _row_block(s):
    return _pick(s, (512, 256, 128, 64, 32, 16, 8))


def _fold8(t):
    r, c = t.shape
    return jnp.sum(t.reshape(r // SUBLANES, SUBLANES, c), axis=0)


def _vec(a):
    return a.reshape(1, -1)


def _modulate(x, sc, sh, out_dtype, name):
    s, d = x.shape
    tr = _row_block(s)

    def body(x_ref, sc_ref, sh_ref, o_ref):
        o_ref[...] = (x_ref[...] * (1.0 + sc_ref[...]) + sh_ref[...]).astype(o_ref.dtype)

    row = pl.BlockSpec((tr, d), lambda i: (i, 0))
    vec = pl.BlockSpec((1, d), lambda i: (0, 0))
    return _pcall(body, name=name, out_shape=jax.ShapeDtypeStruct((s, d), out_dtype), grid=(s // tr,),
                  in_specs=[row, vec, vec], out_specs=row, compiler_params=_params("parallel"))(x, _vec(sc), _vec(sh))


def _modulate_bwd(dxa, dh, x, sc, name):
    s, d = x.shape
    tr = _row_block(s)

    def body(dxa_ref, dh_ref, x_ref, sc_ref, dx_ref, acc_ref):
        @pl.when(pl.program_id(0) == 0)
        def _():
            acc_ref[...] = jnp.zeros_like(acc_ref)

        dh = dh_ref[...]
        dx_ref[...] = dxa_ref[...] + dh * (1.0 + sc_ref[...])
        acc_ref[0] += _fold8(dh * x_ref[...])
        acc_ref[1] += _fold8(dh)

    row = pl.BlockSpec((tr, d), lambda i: (i, 0))
    vec = pl.BlockSpec((1, d), lambda i: (0, 0))
    dx, acc = _pcall(
        body, name=name,
        out_shape=(jax.ShapeDtypeStruct((s, d), F32), jax.ShapeDtypeStruct((2, SUBLANES, d), F32)),
        grid=(s // tr,), in_specs=[row, row, row, vec],
        out_specs=(row, pl.BlockSpec((2, SUBLANES, d), lambda i: (0, 0, 0))),
        compiler_params=_params("arbitrary"))(dxa, dh, x, _vec(sc))
    acc = jnp.sum(acc, axis=1)
    return dx, acc[0], acc[1]


def _resid_ln(x, y, g, gamma, beta, name):
    s, d = x.shape
    tr = _row_block(s)

    def body(x_ref, y_ref, g_ref, gam_ref, bet_ref, o_ref):
        u = DN_ALPHA * x_ref[...] + (1.0 + g_ref[...]) * y_ref[...]
        uc = u - jnp.mean(u, axis=-1, keepdims=True)
        var = jnp.mean(uc * uc, axis=-1, keepdims=True)
        o_ref[...] = uc * lax.rsqrt(var + LN_EPS) * gam_ref[...] + bet_ref[...]

    row = pl.BlockSpec((tr, d), lambda i: (i, 0))
    vec = pl.BlockSpec((1, d), lambda i: (0, 0))
    return _pcall(body, name=name, out_shape=jax.ShapeDtypeStruct((s, d), F32), grid=(s // tr,),
                  in_specs=[row, row, vec, vec, vec], out_specs=row,
                  compiler_params=_params("parallel"))(x, y, _vec(g), _vec(gamma), _vec(beta))


def _resid_ln_bwd(x, y, g, gamma, dout, name):
    s, d = x.shape
    tr = _row_block(s)

    def body(x_ref, y_ref, g_ref, gam_ref, do_ref, dxa_ref, dy_ref, acc_ref):
        @pl.when(pl.program_id(0) == 0)
        def _():
            acc_ref[...] = jnp.zeros_like(acc_ref)

        y = y_ref[...]
        gg = 1.0 + g_ref[...]
        u = DN_ALPHA * x_ref[...] + gg * y
        uc = u - jnp.mean(u, axis=-1, keepdims=True)
        rstd = lax.rsqrt(jnp.mean(uc * uc, axis=-1, keepdims=True) + LN_EPS)
        xhat = uc * rstd
        dout = do_ref[...]
        dxh = dout * gam_ref[...]
        du = rstd * (dxh - jnp.mean(dxh, axis=-1, keepdims=True)
                     - xhat * jnp.mean(dxh * xhat, axis=-1, keepdims=True))
        dxa_ref[...] = DN_ALPHA * du
        dy_ref[...] = gg * du
        acc_ref[0] += _fold8(dout * xhat)
        acc_ref[1] += _fold8(dout)
        acc_ref[2] += _fold8(du * y)

    row = pl.BlockSpec((tr, d), lambda i: (i, 0))
    vec = pl.BlockSpec((1, d), lambda i: (0, 0))
    dxa, dy, acc = _pcall(
        body, name=name,
        out_shape=(jax.ShapeDtypeStruct((s, d), F32), jax.ShapeDtypeStruct((s, d), F32),
                   jax.ShapeDtypeStruct((3, SUBLANES, d), F32)),
        grid=(s // tr,), in_specs=[row, row, vec, vec, row],
        out_specs=(row, row, pl.BlockSpec((3, SUBLANES, d), lambda i: (0, 0, 0))),
        compiler_params=_params("arbitrary"))(x, y, _vec(g), _vec(gamma), dout)
    acc = jnp.sum(acc, axis=1)
    return dxa, dy, acc[0], acc[1], acc[2]


def _loss_head(x, target, name):
    s, d = x.shape
    tr = _row_block(s)

    def body(x_ref, t_ref, dx_ref, acc_ref):
        @pl.when(pl.program_id(0) == 0)
        def _():
            acc_ref[...] = jnp.zeros_like(acc_ref)

        e = x_ref[...] - t_ref[...]
        dx_ref[...] = e * (1.0 / d)
        acc_ref[...] += _fold8(e * e)

    row = pl.BlockSpec((tr, d), lambda i: (i, 0))
    dx, acc = _pcall(
        body, name=name,
        out_shape=(jax.ShapeDtypeStruct((s, d), F32), jax.ShapeDtypeStruct((SUBLANES, d), F32)),
        grid=(s // tr,), in_specs=[row, row],
        out_specs=(row, pl.BlockSpec((SUBLANES, d), lambda i: (0, 0))),
        compiler_params=_params("arbitrary"))(x, target)
    return (0.5 / d) * jnp.sum(acc), dx


CONV_STRIPE = 128
CONV_ROWS = 256


def _shift_down(cur, halo, k):
    ext = jnp.concatenate([halo, cur], axis=0)
    return pltpu.roll(ext, k, 0)[SUBLANES:]


def _shift_up(cur, halo, k):
    ext = jnp.concatenate([cur, halo], axis=0)
    n = ext.shape[0]
    return pltpu.roll(ext, n - k, 0)[:n - SUBLANES]


def _gate_chunk(g_ref, r, rc):
    r0 = pl.multiple_of(r * rc, rc)
    cur = g_ref[pl.ds(r0, rc), :]
    hs = pl.multiple_of(jnp.maximum(r0 - SUBLANES, 0), SUBLANES)
    halo = jnp.where(r > 0, g_ref[pl.ds(hs, SUBLANES), :], 0.0)
    return r0, cur, _shift_down(cur, halo, 1), _shift_down(cur, halo, 2)


def _ffn_gate(gu, cw, cb, name):
    s, f2 = gu.shape
    f = f2 // 2
    tc = _pick(f, (CONV_STRIPE,))
    rc = _pick(s, (CONV_ROWS, 128, 64, 32, 16, 8))
    nj = f // tc

    def body(g_ref, u_ref, cw_ref, cb_ref, a_ref):
        w0, w1, w2, b = cw_ref[0:1, :], cw_ref[1:2, :], cw_ref[2:3, :], cb_ref[...]

        def chunk(r, c):
            r0, cur, x1, x2 = _gate_chunk(g_ref, r, rc)
            gc = w2 * cur + w1 * x1 + w0 * x2 + b
            a_ref[pl.ds(r0, rc), :] = (gc * jax.nn.sigmoid(gc) * u_ref[pl.ds(r0, rc), :]).astype(a_ref.dtype)
            return c

        lax.fori_loop(0, s // rc, chunk, 0)

    return _pcall(
        body, name=name, out_shape=jax.ShapeDtypeStruct((s, f), MXU_DTYPE), grid=(nj,),
        in_specs=[pl.BlockSpec((s, tc), lambda j: (0, j)), pl.BlockSpec((s, tc), lambda j: (0, j + nj)),
                  pl.BlockSpec((3, tc), lambda j: (0, j)), pl.BlockSpec((1, tc), lambda j: (0, j))],
        out_specs=pl.BlockSpec((s, tc), lambda j: (0, j)),
        compiler_params=_params("parallel"))(gu, gu, cw, _vec(cb))


def _ffn_gate_bwd(da, gu, cw, cb, name):
    s, f2 = gu.shape
    f = f2 // 2
    tc = _pick(f, (CONV_STRIPE,))
    rc = _pick(s, (CONV_ROWS, 128, 64, 32, 16, 8))
    nj = f // tc
    nr = s // rc

    def body(da_ref, g_ref, u_ref, cw_ref, cb_ref, dg_ref, du_ref, acc_ref, dgc_ref):
        w0, w1, w2, b = cw_ref[0:1, :], cw_ref[1:2, :], cw_ref[2:3, :], cb_ref[...]

        def chunk1(r, carry):
            a0, a1, a2, ab = carry
            r0, cur, x1, x2 = _gate_chunk(g_ref, r, rc)
            gc = w2 * cur + w1 * x1 + w0 * x2 + b
            sg = jax.nn.sigmoid(gc)
            da_c = da_ref[pl.ds(r0, rc), :]
            du_ref[pl.ds(r0, rc), :] = (da_c * (gc * sg)).astype(du_ref.dtype)
            dgc = da_c * u_ref[pl.ds(r0, rc), :] * (sg * (1.0 + gc * (1.0 - sg)))
            dgc_ref[pl.ds(r0, rc), :] = dgc
            return a0 + _fold8(dgc * x2), a1 + _fold8(dgc * x1), a2 + _fold8(dgc * cur), ab + _fold8(dgc)

        zero = jnp.zeros((SUBLANES, tc), F32)
        a0, a1, a2, ab = lax.fori_loop(0, nr, chunk1, (zero, zero, zero, zero))
        acc_ref[0], acc_ref[1], acc_ref[2], acc_ref[3] = a0, a1, a2, ab

        def chunk2(r, c):
            r0 = pl.multiple_of(r * rc, rc)
            cur = dgc_ref[pl.ds(r0, rc), :]
            hs = pl.multiple_of(jnp.minimum(r0 + rc, s - SUBLANES), SUBLANES)
            halo = jnp.where(r < nr - 1, dgc_ref[pl.ds(hs, SUBLANES), :], 0.0)
            dg = w2 * cur + w1 * _shift_up(cur, halo, 1) + w0 * _shift_up(cur, halo, 2)
            dg_ref[pl.ds(r0, rc), :] = dg.astype(dg_ref.dtype)
            return c

        lax.fori_loop(0, nr, chunk2, 0)

    stripe = pl.BlockSpec((s, tc), lambda j: (0, j))
    dg, du, acc = _pcall(
        body, name=name,
        out_shape=(jax.ShapeDtypeStruct((s, f), MXU_DTYPE), jax.ShapeDtypeStruct((s, f), MXU_DTYPE),
                   jax.ShapeDtypeStruct((4, SUBLANES, f), F32)),
        grid=(nj,),
        in_specs=[stripe, stripe, pl.BlockSpec((s, tc), lambda j: (0, j + nj)),
                  pl.BlockSpec((3, tc), lambda j: (0, j)), pl.BlockSpec((1, tc), lambda j: (0, j))],
        out_specs=(stripe, stripe, pl.BlockSpec((4, SUBLANES, tc), lambda j: (0, 0, j))),
        scratch_shapes=[pltpu.VMEM((s, tc), F32)],
        compiler_params=_params("parallel"))(da, gu, gu, cw, _vec(cb))
    acc = jnp.sum(acc, axis=1)
    return dg, du, acc[:3], acc[3]


def _adamw(gslots, w, m, v, name):
    n, r, _ = gslots.shape
    tr = _pick(r, (1024, 512, 256, 128, 64, 32, 16, 8))
    bc1 = 1.0 / (1.0 - ADAM_B1 ** ADAM_STEP)
    bc2 = 1.0 / (1.0 - ADAM_B2 ** ADAM_STEP)

    def body(g_ref, w_ref, m_ref, v_ref, go_ref, d_ref, mo_ref, vo_ref):
        g = g_ref[0]
        for t in range(1, n):
            g = g + g_ref[t]
        mn = ADAM_B1 * m_ref[...] + (1.0 - ADAM_B1) * g
        vn = ADAM_B2 * v_ref[...] + (1.0 - ADAM_B2) * (g * g)
        m_hat = mn * bc1
        v_hat = vn * bc2
        go_ref[...] = g
        d_ref[...] = -ADAM_LR * (m_hat / (jnp.sqrt(v_hat) + ADAM_EPS) + ADAM_WD * w_ref[...])
        mo_ref[...] = mn
        vo_ref[...] = vn

    row = pl.BlockSpec((tr, LANES), lambda i: (i, 0))
    sds = jax.ShapeDtypeStruct((r, LANES), F32)
    return _pcall(
        body, name=name, out_shape=(sds, sds, sds, sds),
        grid=(r // tr,),
        in_specs=[pl.BlockSpec((n, tr, LANES), lambda i: (0, i, 0)), row, row, row],
        out_specs=(row, row, row, row),
        compiler_params=_params("parallel"),
    )(gslots, w, m, v)


def _sum_slots(gslots, name):
    n, r, _ = gslots.shape
    tr = _pick(r, (1024, 512, 256, 128, 64, 32, 16, 8))

    def body(g_ref, o_ref):
        g = g_ref[0]
        for t in range(1, n):
            g = g + g_ref[t]
        o_ref[...] = g

    return _pcall(
        body, name=name, out_shape=jax.ShapeDtypeStruct((r, LANES), F32),
        grid=(r // tr,),
        in_specs=[pl.BlockSpec((n, tr, LANES), lambda i: (0, i, 0))],
        out_specs=pl.BlockSpec((tr, LANES), lambda i: (i, 0)),
        compiler_params=_params("parallel"),
    )(gslots)


def _standardize(x, eps):
    mu = jnp.mean(x, axis=-1, keepdims=True)
    xc = x - mu
    var = jnp.mean(xc * xc, axis=-1, keepdims=True)
    return xc * lax.rsqrt(var + eps)


def _layer_norm(x, g, b):
    return _standardize(x, LN_EPS) * g + b


def _l2norm(x, eps=1e-6):
    return x * lax.rsqrt(jnp.sum(x * x, axis=-1, keepdims=True) + eps)


def _causal_dwconv(x, w):
    k_w, s = w.shape[0], x.shape[0]
    xp = jnp.pad(x, ((k_w - 1, 0), (0, 0)))
    y = xp[k_w - 1:k_w - 1 + s] * w[k_w - 1]
    for j in range(k_w - 1):
        y = y + xp[j:j + s] * w[j]
    return y


def _chunk_heads(t, n_heads, chunk):
    s, hd = t.shape
    return t.reshape(s // chunk, chunk, n_heads, hd // n_heads).transpose(2, 0, 1, 3)


def _unchunk_heads(t):
    h, n, c, d = t.shape
    return t.transpose(1, 2, 0, 3).reshape(n * c, h, d)


def _gated_deltanet(hx, p):
    H, C = GDN_HEADS, GDN_CHUNK
    s, d = hx.shape
    dk = dv = d // H
    qkvz = _linear(hx, p['gdn_w_qkvz'], "gdn_in")
    ab = _linear(hx, p['gdn_w_ab'], "gdn_ab")
    qkv, z = qkvz[:, :3 * d], qkvz[:, 3 * d:]
    a, bt = ab[:, :H], ab[:, H:2 * H]
    qkv = jax.nn.silu(_causal_dwconv(qkv, p['gdn_conv_w']))
    q, k, v = qkv[:, :d], qkv[:, d:2 * d], qkv[:, 2 * d:]
    q = _l2norm(_chunk_heads(q, H, C)) * (dk ** -0.5)
    k = _l2norm(_chunk_heads(k, H, C))
    v = _chunk_heads(v, H, C)
    beta = jax.nn.sigmoid(_chunk_heads(bt, H, C)[..., 0])
    g = -jnp.exp(p['gdn_a_log'])[:, None, None] * jax.nn.softplus(
        _chunk_heads(a, H, C)[..., 0] + p['gdn_dt_bias'][:, None, None])
    gc = jnp.cumsum(g, axis=-1)
    idx = jnp.arange(C)
    causal = idx[:, None] >= idx[None, :]
    strict = idx[:, None] > idx[None, :]
    diff = gc[..., :, None] - gc[..., None, :]
    decay = jnp.where(causal, jnp.exp(jnp.where(causal, diff, 0.0)), 0.0)
    kb = k * beta[..., None]
    kk = jnp.where(strict, jnp.einsum('hncd,hnmd->hncm', kb, k) * decay, 0.0)
    eye = jnp.eye(C, dtype=F32)
    rhs = jnp.concatenate([v * beta[..., None], kb * jnp.exp(gc)[..., None]], axis=-1)
    sol = lax.linalg.triangular_solve(kk + eye, rhs, left_side=True, lower=True, unit_diagonal=True)
    u, w = sol[..., :dv], sol[..., dv:]
    qk = jnp.where(causal, jnp.einsum('hncd,hnmd->hncm', q, k) * decay, 0.0)

    def step(state, inp):
        q_n, k_n, u_n, w_n, qk_n, g_n = inp
        v_new = u_n - jnp.einsum('hck,hkv->hcv', w_n, state)
        o = (jnp.einsum('hck,hkv->hcv', q_n * jnp.exp(g_n)[..., None], state)
             + jnp.einsum('hcm,hmv->hcv', qk_n, v_new))
        g_last = g_n[..., -1:]
        state = (state * jnp.exp(g_last)[..., None]
                 + jnp.einsum('hck,hcv->hkv', k_n * jnp.exp(g_last - g_n)[..., None], v_new))
        return state, o

    xs = tuple(jnp.moveaxis(t, 1, 0) for t in (q, k, u, w, qk, gc))
    _, o = lax.scan(step, jnp.zeros((H, dk, dv), F32), xs)
    o = _unchunk_heads(jnp.moveaxis(o, 0, 1))
    o = o * lax.rsqrt(jnp.mean(o * o, axis=-1, keepdims=True) + 1e-6) * p['gdn_norm_w']
    o = o * jax.nn.silu(z.reshape(s, H, dv))
    return _linear(o.reshape(s, H * dv), p['gdn_w_out'], "gdn_out")


def _ret_consts(c):
    log_gamma = jnp.log(1.0 - jnp.power(2.0, -5.0 - jnp.arange(RET_HEADS, dtype=F32)))
    idx = jnp.arange(c, dtype=F32)
    rel = idx[:, None] - idx[None, :]
    dmask = jnp.where(rel >= 0, jnp.exp(jnp.maximum(rel, 0.0) * log_gamma[:, None, None]), 0.0)
    zeta = jnp.exp((c - 1.0 - idx)[None, :] * log_gamma[:, None])[..., None]
    xi = jnp.exp((idx + 1.0)[None, :] * log_gamma[:, None])[..., None]
    gamma_c = jnp.exp(c * log_gamma)[:, None, None]
    return dmask, zeta, xi, gamma_c


def _ret_angles(s, dk):
    pos = jnp.arange(s, dtype=F32)
    inv_freq = RET_ROPE_BASE ** (-jnp.linspace(0.0, 1.0, dk // 2, dtype=F32))
    ang = pos[:, None] * inv_freq[None, :]
    return jnp.cos(ang), jnp.sin(ang)


def _rot(t, cs, sn):
    half = t.shape[1] // 2
    t1, t2 = t[:, :half], t[:, half:]
    return jnp.concatenate([t1 * cs - t2 * sn, t1 * sn + t2 * cs], axis=1)


def _rot_t(t, cs, sn):
    half = t.shape[1] // 2
    t1, t2 = t[:, :half], t[:, half:]
    return jnp.concatenate([t1 * cs + t2 * sn, t2 * cs - t1 * sn], axis=1)


_NT = (((1,), (1,)), ((), ()))
_TN = (((0,), (0,)), ((), ()))


def _ret_cols(d, dk, dv, hd):
    return (slice(hd * dk, (hd + 1) * dk), slice(d + hd * dk, d + (hd + 1) * dk),
            slice(2 * d + hd * dv, 2 * d + (hd + 1) * dv), slice(4 * d + hd * dv, 4 * d + (hd + 1) * dv))


def _ret_fwd_call(proj):
    s, d6 = proj.shape
    d = d6 // 6
    H, c = RET_HEADS, RET_CHUNK
    dk, dv = d // H, 2 * d // H
    n_chunks = s // c
    kscale = dk ** -0.5
    cos_a, sin_a = _ret_angles(s, dk)
    consts = _ret_consts(c)

    def body(p_ref, cos_ref, sin_ref, dm_ref, ze_ref, xi_ref, gc_ref, out_ref, oraw_ref, st_ref, state_ref):
        @pl.when(pl.program_id(0) == 0)
        def _():
            state_ref[...] = jnp.zeros_like(state_ref)

        cs, sn = cos_ref[...], sin_ref[...]
        for hd in range(H):
            qc, kc, vc, gcol = _ret_cols(d, dk, dv, hd)
            ocol = slice(hd * dv, (hd + 1) * dv)
            qb = _rot(p_ref[:, qc], cs, sn).astype(MXU_DTYPE)
            kr = _rot(p_ref[:, kc], cs, sn) * kscale
            kb = kr.astype(MXU_DTYPE)
            vb = p_ref[:, vc].astype(MXU_DTYPE)
            st = state_ref[hd]
            stb = st.astype(MXU_DTYPE)
            st_ref[hd] = stb
            sc = lax.dot_general(qb, kb, _NT, preferred_element_type=F32) * dm_ref[hd]
            o = (jnp.dot(sc.astype(MXU_DTYPE), vb, preferred_element_type=F32)
                 + jnp.dot(qb, stb, preferred_element_type=F32) * xi_ref[hd])
            state_ref[hd] = st * gc_ref[hd] + lax.dot_general((kr * ze_ref[hd]).astype(MXU_DTYPE), vb, _TN,
                                                              preferred_element_type=F32)
            oraw_ref[:, ocol] = o
            oc = o - jnp.mean(o, axis=-1, keepdims=True)
            on = oc * lax.rsqrt(jnp.mean(oc * oc, axis=-1, keepdims=True) + 1e-6)
            gate = p_ref[:, gcol]
            out_ref[:, ocol] = on * (gate * jax.nn.sigmoid(gate))

    row = lambda width: pl.BlockSpec((c, width), lambda n: (n, 0))
    whole = lambda a: pl.BlockSpec(a.shape, lambda n: (0,) * a.ndim)
    return _pcall(
        body, name="ret_fwd",
        out_shape=(jax.ShapeDtypeStruct((s, 2 * d), F32), jax.ShapeDtypeStruct((s, 2 * d), F32),
                   jax.ShapeDtypeStruct((n_chunks, H, dk, dv), MXU_DTYPE)),
        grid=(n_chunks,),
        in_specs=[row(d6), row(dk // 2), row(dk // 2)] + [whole(a) for a in consts],
        out_specs=(row(2 * d), row(2 * d), pl.BlockSpec((None, H, dk, dv), lambda n: (n, 0, 0, 0))),
        scratch_shapes=[pltpu.VMEM((H, dk, dv), F32)],
        compiler_params=_params("arbitrary"),
    )(proj, cos_a, sin_a, *consts)


def _ret_bwd_call(proj, oraw, states, dout):
    s, d6 = proj.shape
    d = d6 // 6
    H, c = RET_HEADS, RET_CHUNK
    dk, dv = d // H, 2 * d // H
    n_chunks = s // c
    kscale = dk ** -0.5
    cos_a, sin_a = _ret_angles(s, dk)
    consts = _ret_consts(c)

    def body(p_ref, cos_ref, sin_ref, dm_ref, ze_ref, xi_ref, gc_ref, oraw_ref, st_ref, do_ref, dp_ref, ds_ref):
        @pl.when(pl.program_id(0) == 0)
        def _():
            ds_ref[...] = jnp.zeros_like(ds_ref)

        cs, sn = cos_ref[...], sin_ref[...]
        for hd in range(H):
            qc, kc, vc, gcol = _ret_cols(d, dk, dv, hd)
            ocol = slice(hd * dv, (hd + 1) * dv)
            qb = _rot(p_ref[:, qc], cs, sn).astype(MXU_DTYPE)
            kr = _rot(p_ref[:, kc], cs, sn) * kscale
            kb = kr.astype(MXU_DTYPE)
            vb = p_ref[:, vc].astype(MXU_DTYPE)
            gate = p_ref[:, gcol]
            o = oraw_ref[:, ocol]
            oc = o - jnp.mean(o, axis=-1, keepdims=True)
            rstd = lax.rsqrt(jnp.mean(oc * oc, axis=-1, keepdims=True) + 1e-6)
            on = oc * rstd
            dout_h = do_ref[:, ocol]
            sg = jax.nn.sigmoid(gate)
            dp_ref[:, gcol] = dout_h * on * (sg * (1.0 + gate * (1.0 - sg)))
            don = dout_h * (gate * sg)
            do_raw = rstd * (don - jnp.mean(don, axis=-1, keepdims=True)
                             - on * jnp.mean(don * on, axis=-1, keepdims=True))
            dob = do_raw.astype(MXU_DTYPE)
            stb = st_ref[hd]
            ds = ds_ref[hd]
            dsb = ds.astype(MXU_DTYPE)
            dm = dm_ref[hd]
            scb = (lax.dot_general(qb, kb, _NT, preferred_element_type=F32) * dm).astype(MXU_DTYPE)
            dsc = (lax.dot_general(dob, vb, _NT, preferred_element_type=F32) * dm).astype(MXU_DTYPE)
            dqr = jnp.dot(dsc, kb, preferred_element_type=F32)
            dkr = lax.dot_general(dsc, qb, _TN, preferred_element_type=F32)
            dvv = lax.dot_general(scb, dob, _TN, preferred_element_type=F32)
            doi = (do_raw * xi_ref[hd]).astype(MXU_DTYPE)
            dqr = dqr + lax.dot_general(doi, stb, _NT, preferred_element_type=F32)
            ds_in = lax.dot_general(qb, doi, _TN, preferred_element_type=F32)
            ze = ze_ref[hd]
            dkr = dkr + lax.dot_general(vb, dsb, _NT, preferred_element_type=F32) * ze
            dvv = dvv + jnp.dot((kr * ze).astype(MXU_DTYPE), dsb, preferred_element_type=F32)
            ds_ref[hd] = ds * gc_ref[hd] + ds_in
            dp_ref[:, qc] = _rot_t(dqr, cs, sn)
            dp_ref[:, kc] = _rot_t(dkr * kscale, cs, sn)
            dp_ref[:, vc] = dvv

    last = n_chunks - 1
    row = lambda width: pl.BlockSpec((c, width), lambda n: (last - n, 0))
    whole = lambda a: pl.BlockSpec(a.shape, lambda n: (0,) * a.ndim)
    return _pcall(
        body, name="ret_bwd", out_shape=jax.ShapeDtypeStruct((s, d6), F32),
        grid=(n_chunks,),
        in_specs=[row(d6), row(dk // 2), row(dk // 2)] + [whole(a) for a in consts]
                 + [row(2 * d), pl.BlockSpec((None, H, dk, dv), lambda n: (last - n, 0, 0, 0)), row(2 * d)],
        out_specs=row(d6),
        scratch_shapes=[pltpu.VMEM((H, dk, dv), F32)],
        compiler_params=_params("arbitrary"),
    )(proj, cos_a, sin_a, *consts, oraw, states, dout)


@jax.custom_vjp
def _ret_core(proj):
    return _ret_fwd_call(proj)[0]


def _ret_core_fwd(proj):
    out, oraw, states = _ret_fwd_call(proj)
    return out, (proj, oraw, states)


def _ret_core_bwd(res, dout):
    return (_ret_bwd_call(*res, dout),)


_ret_core.defvjp(_ret_core_fwd, _ret_core_bwd)


def _retention(hx, p):
    return _linear(_ret_core(_linear(hx, p['ret_w_in'], "ret_in")), p['ret_w_out'], "ret_out")


SQRT_HALF = 2.0 ** -0.5
INV_SQRT_2PI = (2.0 * math.pi) ** -0.5


def _gmlp_front(p_ref, g_ref, b_ref, w):
    x = p_ref[...]
    cdf = 0.5 * (1.0 + lax.erf(x * SQRT_HALF))
    uv = x * cdf
    u, v = uv[:, :w], uv[:, w:]
    vc = v - jnp.mean(v, axis=-1, keepdims=True)
    rstd = lax.rsqrt(jnp.mean(vc * vc, axis=-1, keepdims=True) + LN_EPS)
    vhat = vc * rstd
    return x, cdf, u, vhat, rstd, vhat * g_ref[...] + b_ref[...]


def _gmlp_fwd_call(proj, ln_g, ln_b, ws, bs):
    s, w2 = proj.shape
    w = w2 // 2
    c, G = GMLP_CHUNK, GMLP_GROUPS
    gw = w // G

    def body(p_ref, g_ref, b_ref, ws_ref, bs_ref, o_ref):
        _, _, u, _, _, vn = _gmlp_front(p_ref, g_ref, b_ref, w)
        for gi in range(G):
            cols = slice(gi * gw, (gi + 1) * gw)
            vs = jnp.dot(ws_ref[gi].astype(MXU_DTYPE), vn[:, cols].astype(MXU_DTYPE),
                         preferred_element_type=F32) + bs_ref[gi]
            o_ref[:, cols] = u[:, cols] * vs

    whole = lambda a: pl.BlockSpec(a.shape, lambda n: (0,) * a.ndim)
    args = (_vec(ln_g), _vec(ln_b), ws, bs)
    return _pcall(
        body, name="gmlp_fwd", out_shape=jax.ShapeDtypeStruct((s, w), F32), grid=(s // c,),
        in_specs=[pl.BlockSpec((c, w2), lambda n: (n, 0))] + [whole(a) for a in args],
        out_specs=pl.BlockSpec((c, w), lambda n: (n, 0)),
        compiler_params=_params("parallel"),
    )(proj, *args)


def _gmlp_bwd_call(proj, ln_g, ln_b, ws, bs, dout):
    s, w2 = proj.shape
    w = w2 // 2
    c, G = GMLP_CHUNK, GMLP_GROUPS
    gw = w // G

    def body(p_ref, g_ref, b_ref, ws_ref, bs_ref, do_ref, dp_ref, dws_ref, dbs_ref, dgb_ref):
        @pl.when(pl.program_id(0) == 0)
        def _():
            dws_ref[...] = jnp.zeros_like(dws_ref)
            dbs_ref[...] = jnp.zeros_like(dbs_ref)
            dgb_ref[...] = jnp.zeros_like(dgb_ref)

        x, cdf, u, vhat, rstd, vn = _gmlp_front(p_ref, g_ref, b_ref, w)
        dout = do_ref[...]
        du_parts, dvn_parts = [], []
        for gi in range(G):
            cols = slice(gi * gw, (gi + 1) * gw)
            wsg = ws_ref[gi].astype(MXU_DTYPE)
            vng = vn[:, cols].astype(MXU_DTYPE)
            vs = jnp.dot(wsg, vng, preferred_element_type=F32) + bs_ref[gi]
            du_parts.append(dout[:, cols] * vs)
            dvs = dout[:, cols] * u[:, cols]
            dbs_ref[:, cols] += dvs
            dvsb = dvs.astype(MXU_DTYPE)
            dws_ref[gi] += lax.dot_general(dvsb, vng, _NT, preferred_element_type=F32)
            dvn_parts.append(lax.dot_general(wsg, dvsb, _TN, preferred_element_type=F32))
        dvn = jnp.concatenate(dvn_parts, axis=1)
        dgb_ref[0] += _fold8(dvn * vhat)
        dgb_ref[1] += _fold8(dvn)
        dvh = dvn * g_ref[...]
        dv = rstd * (dvh - jnp.mean(dvh, axis=-1, keepdims=True)
                     - vhat * jnp.mean(dvh * vhat, axis=-1, keepdims=True))
        duv = jnp.concatenate(du_parts + [dv], axis=1)
        dp_ref[...] = duv * (cdf + x * (jnp.exp(-0.5 * x * x) * INV_SQRT_2PI))

    whole = lambda a: pl.BlockSpec(a.shape, lambda n: (0,) * a.ndim)
    args = (_vec(ln_g), _vec(ln_b), ws, bs)
    acc = lambda *shape: pl.BlockSpec(shape, lambda n: (0,) * len(shape))
    return _pcall(
        body, name="gmlp_bwd",
        out_shape=(jax.ShapeDtypeStruct((s, w2), F32), jax.ShapeDtypeStruct((G, c, c), F32),
                   jax.ShapeDtypeStruct((c, w), F32), jax.ShapeDtypeStruct((2, SUBLANES, w), F32)),
        grid=(s // c,),
        in_specs=[pl.BlockSpec((c, w2), lambda n: (n, 0))] + [whole(a) for a in args]
                 + [pl.BlockSpec((c, w), lambda n: (n, 0))],
        out_specs=(pl.BlockSpec((c, w2), lambda n: (n, 0)), acc(G, c, c), acc(c, w), acc(2, SUBLANES, w)),
        compiler_params=_params("arbitrary"),
    )(proj, *args, dout)


def _gmlp_mask(c):
    return jnp.tril(jnp.ones((c, c), dtype=bool))


@jax.custom_vjp
def _gmlp_core(proj, ln_g, ln_b, w_s, b_s):
    ws = jnp.where(_gmlp_mask(GMLP_CHUNK), w_s, 0.0)
    return _gmlp_fwd_call(proj, ln_g, ln_b, ws, b_s[..., None])


def _gmlp_core_fwd(proj, ln_g, ln_b, w_s, b_s):
    return _gmlp_core(proj, ln_g, ln_b, w_s, b_s), (proj, ln_g, ln_b, w_s, b_s)


def _gmlp_core_bwd(res, dout):
    proj, ln_g, ln_b, w_s, b_s = res
    mask = _gmlp_mask(GMLP_CHUNK)
    dproj, dws, dbs, dgb = _gmlp_bwd_call(proj, ln_g, ln_b, jnp.where(mask, w_s, 0.0), b_s[..., None], dout)
    dgb = jnp.sum(dgb, axis=1)
    c = GMLP_CHUNK
    db_s = jnp.sum(dbs.reshape(c, GMLP_GROUPS, -1), axis=-1).T
    return dproj, dgb[0], dgb[1], jnp.where(mask, dws, 0.0), db_s


_gmlp_core.defvjp(_gmlp_core_fwd, _gmlp_core_bwd)


def _chunked_gmlp(hx, p):
    core = _gmlp_core(_linear(hx, p['gmlp_w_in'], "gmlp_in"), p['gmlp_ln_g'], p['gmlp_ln_b'],
                      p['gmlp_w_s'], p['gmlp_b_s'])
    return _linear(core, p['gmlp_w_out'], "gmlp_out")


def _stick_breaking(hx, p):
    H = SB_HEADS
    s, d = hx.shape
    dh = d // H
    qkv = _linear(hx, p['sb_w_in'], "sb_in")
    q, k, v = (qkv[:, j * d:(j + 1) * d].reshape(s, H, dh).transpose(1, 0, 2) for j in range(3))
    o = _sb_core(q, k, v)
    return _linear(o.transpose(1, 0, 2).reshape(s, d), p['sb_w_out'], "sb_out")


MIXERS = (_gated_deltanet, _retention, _chunked_gmlp, _stick_breaking)


def _trunk_grad(x, mods, p, target):
    d = x.shape[-1]
    saved = []
    for i in range(DEPTH):
        sh1, sc1, g1, sh2, sc2, g2 = (mods[i, j * d:(j + 1) * d] for j in range(6))
        h1 = _modulate(x, sc1, sh1, F32, "mod_a%d" % i)
        y1, mixer_vjp = jax.vjp(MIXERS[i], h1, {n: p[n] for n in MIXER_PARAMS[i]})
        x1 = _resid_ln(x, y1, g1, p['ln_g'][i, 0], p['ln_b'][i, 0], "ln_a%d" % i)
        h2 = _modulate(x1, sc2, sh2, MXU_DTYPE, "mod_b%d" % i)
        gu = _mm(h2, p['ffn_up'][i], 'nn', "ffn_up%d_fwd" % i)
        act = _ffn_gate(gu, p['ffn_conv_w'][i], p['ffn_conv_b'][i], "ffn_gate%d" % i)
        y2 = _mm(act, p['ffn_down'][i], 'nn', "ffn_down%d_fwd" % i)
        x2 = _resid_ln(x1, y2, g2, p['ln_g'][i, 1], p['ln_b'][i, 1], "ln_b%d" % i)
        saved.append((x, y1, mixer_vjp, x1, h2, gu, act, y2))
        x = x2
    loss, dx = _loss_head(x, target, "loss_head")

    dp = {n: None for n in p}
    d_ln_g, d_ln_b, d_up, d_down, d_cw, d_cb, dmods = [], [], [], [], [], [], []
    for i in reversed(range(DEPTH)):
        x0, y1, mixer_vjp, x1, h2, gu, act, y2 = saved[i]
        sh1, sc1, g1, sh2, sc2, g2 = (mods[i, j * d:(j + 1) * d] for j in range(6))
        dxa, dy2, dgam2, dbet2, dg2 = _resid_ln_bwd(x1, y2, g2, p['ln_g'][i, 1], dx, "ln_b%d_bwd" % i)
        dact = _mm(dy2, p['ffn_down'][i], 'nt', "ffn_down%d_dx" % i)
        d_down.append(_mm(act, dy2, 'tn', "ffn_down%d_dw" % i))
        dgate, dupp, dcw, dcb = _ffn_gate_bwd(dact, gu, p['ffn_conv_w'][i], p['ffn_conv_b'][i],
                                              "ffn_gate%d_bwd" % i)
        dgu = jnp.concatenate([dgate, dupp], axis=1)
        dh2 = _mm(dgu, p['ffn_up'][i], 'nt', "ffn_up%d_dx" % i)
        d_up.append(_mm(h2, dgu, 'tn', "ffn_up%d_dw" % i))
        dx1, dsc2, dsh2 = _modulate_bwd(dxa, dh2, x1, sc2, "mod_b%d_bwd" % i)
        dxa, dy1, dgam1, dbet1, dg1 = _resid_ln_bwd(x0, y1, g1, p['ln_g'][i, 0], dx1, "ln_a%d_bwd" % i)
        dh1, dmix = mixer_vjp(dy1)
        dp.update(dmix)
        dx, dsc1, dsh1 = _modulate_bwd(dxa, dh1, x0, sc1, "mod_a%d_bwd" % i)
        d_ln_g.append(jnp.stack([dgam1, dgam2]))
        d_ln_b.append(jnp.stack([dbet1, dbet2]))
        d_cw.append(dcw)
        d_cb.append(dcb)
        dmods.append(jnp.concatenate([dsh1, dsc1, dg1, dsh2, dsc2, dg2]))
    for n, parts in (('ln_g', d_ln_g), ('ln_b', d_ln_b), ('ffn_up', d_up), ('ffn_down', d_down),
                     ('ffn_conv_w', d_cw), ('ffn_conv_b', d_cb)):
        dp[n] = jnp.stack(parts[::-1])
    return loss, dx, jnp.stack(dmods[::-1]), dp


def _join(blocks, axis):
    return jnp.concatenate([blocks[d] for d in range(N_DEV)], axis=axis)


def _split(whole, axis):
    n = whole.shape[axis] // N_DEV
    return jnp.stack([lax.slice_in_dim(whole, d * n, (d + 1) * n, axis=axis) for d in range(N_DEV)])


def _pad8(a):
    pad = (-a.shape[0]) % 8
    return jnp.pad(a, ((0, pad), (0, 0))) if pad else a


def _pack_big_grads(full_grads, axes):
    per_dev = jnp.concatenate([_split(g, ax).reshape(N_DEV, -1) for g, ax in zip(full_grads, axes)], axis=1)
    pad = (-per_dev.shape[1]) % (BIG_ROW_ALIGN * LANES)
    if pad:
        per_dev = jnp.pad(per_dev, ((0, 0), (0, pad)))
    return per_dev.reshape(N_DEV, -1, LANES)


def kernel(x, c, cond_w, cond_b, ada_w, ada_b, ln_g, ln_b, ffn_up, ffn_conv_w, ffn_conv_b, ffn_down, gdn_w_in, gdn_conv_w, gdn_a_log, gdn_dt_bias, gdn_norm_w, gdn_w_out, ret_w_in, ret_w_out, gmlp_w_in, gmlp_ln_g, gmlp_ln_b, gmlp_w_s, gmlp_b_s, gmlp_w_out, sb_w_in, sb_w_out, loss_target, m_cond_w, m_cond_b, m_ada_w, m_ada_b, m_ln_g, m_ln_b, m_ffn_up, m_ffn_conv_w, m_ffn_conv_b, m_ffn_down, m_gdn_w_in, m_gdn_conv_w, m_gdn_a_log, m_gdn_dt_bias, m_gdn_norm_w, m_gdn_w_out, m_ret_w_in, m_ret_w_out, m_gmlp_w_in, m_gmlp_ln_g, m_gmlp_ln_b, m_gmlp_w_s, m_gmlp_b_s, m_gmlp_w_out, m_sb_w_in, m_sb_w_out, v_cond_w, v_cond_b, v_ada_w, v_ada_b, v_ln_g, v_ln_b, v_ffn_up, v_ffn_conv_w, v_ffn_conv_b, v_ffn_down, v_gdn_w_in, v_gdn_conv_w, v_gdn_a_log, v_gdn_dt_bias, v_gdn_norm_w, v_gdn_w_out, v_ret_w_in, v_ret_w_out, v_gmlp_w_in, v_gmlp_ln_g, v_gmlp_ln_b, v_gmlp_w_s, v_gmlp_b_s, v_gmlp_w_out, v_sb_w_in, v_sb_w_out):
    w = dict(cond_w=cond_w, cond_b=cond_b, ada_w=ada_w, ada_b=ada_b, ln_g=ln_g, ln_b=ln_b, ffn_up=ffn_up,
             ffn_conv_w=ffn_conv_w, ffn_conv_b=ffn_conv_b, ffn_down=ffn_down, gdn_w_in=gdn_w_in,
             gdn_conv_w=gdn_conv_w, gdn_a_log=gdn_a_log, gdn_dt_bias=gdn_dt_bias, gdn_norm_w=gdn_norm_w,
             gdn_w_out=gdn_w_out, ret_w_in=ret_w_in, ret_w_out=ret_w_out, gmlp_w_in=gmlp_w_in,
             gmlp_ln_g=gmlp_ln_g, gmlp_ln_b=gmlp_ln_b, gmlp_w_s=gmlp_w_s, gmlp_b_s=gmlp_b_s,
             gmlp_w_out=gmlp_w_out, sb_w_in=sb_w_in, sb_w_out=sb_w_out)
    mom = dict(cond_w=m_cond_w, cond_b=m_cond_b, ada_w=m_ada_w, ada_b=m_ada_b, ln_g=m_ln_g, ln_b=m_ln_b,
               ffn_up=m_ffn_up, ffn_conv_w=m_ffn_conv_w, ffn_conv_b=m_ffn_conv_b, ffn_down=m_ffn_down,
               gdn_w_in=m_gdn_w_in, gdn_conv_w=m_gdn_conv_w, gdn_a_log=m_gdn_a_log, gdn_dt_bias=m_gdn_dt_bias,
               gdn_norm_w=m_gdn_norm_w, gdn_w_out=m_gdn_w_out, ret_w_in=m_ret_w_in, ret_w_out=m_ret_w_out,
               gmlp_w_in=m_gmlp_w_in, gmlp_ln_g=m_gmlp_ln_g, gmlp_ln_b=m_gmlp_ln_b, gmlp_w_s=m_gmlp_w_s,
               gmlp_b_s=m_gmlp_b_s, gmlp_w_out=m_gmlp_w_out, sb_w_in=m_sb_w_in, sb_w_out=m_sb_w_out)
    var = dict(cond_w=v_cond_w, cond_b=v_cond_b, ada_w=v_ada_w, ada_b=v_ada_b, ln_g=v_ln_g, ln_b=v_ln_b,
               ffn_up=v_ffn_up, ffn_conv_w=v_ffn_conv_w, ffn_conv_b=v_ffn_conv_b, ffn_down=v_ffn_down,
               gdn_w_in=v_gdn_w_in, gdn_conv_w=v_gdn_conv_w, gdn_a_log=v_gdn_a_log, gdn_dt_bias=v_gdn_dt_bias,
               gdn_norm_w=v_gdn_norm_w, gdn_w_out=v_gdn_w_out, ret_w_in=v_ret_w_in, ret_w_out=v_ret_w_out,
               gmlp_w_in=v_gmlp_w_in, gmlp_ln_g=v_gmlp_ln_g, gmlp_ln_b=v_gmlp_ln_b, gmlp_w_s=v_gmlp_w_s,
               gmlp_b_s=v_gmlp_b_s, gmlp_w_out=v_gmlp_w_out, sb_w_in=v_sb_w_in, sb_w_out=v_sb_w_out)

    me = _my_id()
    x = x[0]
    target = loss_target[0]
    d = x.shape[-1]
    dsh = d // N_DEV
    msh = ada_w.shape[-1]

    c_all = _exchange(_pad8(c), False, "gather_c")[:, 0, :]
    c_mine = lax.dynamic_slice_in_dim(c_all, me * dsh, dsh, axis=1)
    pre_part = _mm(c_mine, cond_w, 'nn', "cond_fwd")
    pre = jnp.sum(_exchange(pre_part, False, "gather_pre"), axis=0) + cond_b
    e_all = jax.nn.silu(pre)
    mod_part = jnp.concatenate([_mm(e_all, ada_w[i], 'nn', "ada_fwd%d" % i) for i in range(DEPTH)], axis=0)
    mod_all = _exchange(mod_part, False, "gather_mod")
    mod_all = mod_all.reshape(N_DEV, DEPTH, N_DEV, msh)
    mods = lax.dynamic_index_in_dim(mod_all, me, axis=2, keepdims=False)
    mods = mods.transpose(1, 0, 2).reshape(DEPTH, N_DEV * msh) + ada_b

    big_names = list(BIG)
    packed = _pack_rows([w[n] for n in big_names], BF16, BIG_ROW_ALIGN)
    gathered = _exchange(packed, False, "gather_weights")
    blocks = _unpack_rows(gathered, [w[n].shape for n in big_names])
    p = {n: _join(b, BIG[n]) for n, b in zip(big_names, blocks)}
    for n in big_names:
        if not n.startswith('ffn_'):
            p[n] = p[n].astype(F32)
    sm_names = list(SMALL_SHARDED)
    sm_packed = _pack_rows([w[n] for n in sm_names], F32)
    sm_blocks = _unpack_rows(_exchange(sm_packed, False, "gather_small"), [w[n].shape for n in sm_names])
    for n, b in zip(sm_names, sm_blocks):
        p[n] = _join(b, SMALL_SHARDED[n])
    for n in SMALL_REPL:
        p[n] = w[n]
    n_qkvz = 4 * d
    p['gdn_w_qkvz'] = p['gdn_w_in'][:, :n_qkvz]
    p['gdn_w_ab'] = jnp.pad(p['gdn_w_in'][:, n_qkvz:], ((0, 0), (0, LANES - 2 * GDN_HEADS)))
    del p['gdn_w_in']

    loss_local, dx, dmods, dp = _trunk_grad(x, mods, p, target)
    dp['gdn_w_in'] = jnp.concatenate([dp.pop('gdn_w_qkvz'), dp.pop('gdn_w_ab')[:, :2 * GDN_HEADS]], axis=1)

    dmod_all = _exchange(dmods.reshape(-1, d), False, "gather_dmod").reshape(N_DEV, DEPTH, 6 * d)
    grads = {'ada_b': jnp.sum(dmod_all, axis=0)}
    dm_mine = lax.dynamic_slice_in_dim(dmod_all, me * msh, msh, axis=2)
    grads['ada_w'] = jnp.stack([_mm_outer(e_all, dm_mine[:, i], "ada_dw%d" % i) for i in range(DEPTH)])
    de_part = _mm(dm_mine[:, 0], ada_w[0], 'nt', "ada_de0")
    for i in range(1, DEPTH):
        de_part = de_part + _mm(dm_mine[:, i], ada_w[i], 'nt', "ada_de%d" % i)
    de_all = jnp.sum(_exchange(de_part, False, "gather_de"), axis=0)
    sig = jax.nn.sigmoid(pre)
    dpre = de_all * (sig * (1.0 + pre * (1.0 - sig)))
    grads['cond_b'] = jnp.sum(dpre, axis=0)
    grads['cond_w'] = _mm_outer(c_mine, dpre, "cond_dw")

    small_names = sm_names + SMALL_REPL
    small_packed = _pack_rows([loss_local.reshape(1)] + [dp[n] for n in small_names], F32)
    small_sum = _sum_slots(_exchange(small_packed, False, "gather_small_grads"), "sum_small_grads")
    small = _unpack_rows(small_sum, [(1,)] + [dp[n].shape for n in small_names])
    loss = small[0][0]
    for n, g in zip(small_names, small[1:]):
        if n in SMALL_SHARDED:
            ax = SMALL_SHARDED[n]
            g = lax.dynamic_slice_in_dim(g, me * w[n].shape[ax], w[n].shape[ax], axis=ax)
        grads[n] = g

    send = _pack_big_grads([dp[n] for n in big_names], [BIG[n] for n in big_names])
    recv = _exchange(send, True, "scatter_grads")
    shapes = [w[n].shape for n in big_names]
    outs = _adamw(recv, *[_pack_rows([t[n] for n in big_names], F32, BIG_ROW_ALIGN) for t in (w, mom, var)],
                  "adamw_big")
    g_b, d_b, m_b, v_b = (_unpack_rows(o, shapes) for o in outs)
    delta, new_m, new_v = {}, {}, {}
    for j, n in enumerate(big_names):
        grads[n], delta[n], new_m[n], new_v[n] = g_b[j], d_b[j], m_b[j], v_b[j]

    rest = [n for n in WEIGHTS if n not in BIG]
    shapes = [w[n].shape for n in rest]
    outs = _adamw(_pack_rows([grads[n] for n in rest], F32, BIG_ROW_ALIGN)[None],
                  *[_pack_rows([t[n] for n in rest], F32, BIG_ROW_ALIGN) for t in (w, mom, var)], "adamw_rest")
    _, d_r, m_r, v_r = (_unpack_rows(o, shapes) for o in outs)
    for j, n in enumerate(rest):
        delta[n], new_m[n], new_v[n] = d_r[j], m_r[j], v_r[j]

    return (loss, dx[None], *[grads[n] for n in WEIGHTS], *[delta[n] for n in WEIGHTS],
            *[new_m[n] for n in WEIGHTS], *[new_v[n] for n in WEIGHTS])
```

```python
import functools
import math

import jax
import jax.numpy as jnp
from jax import lax
from jax.experimental import pallas as pl
from jax.experimental.pallas import tpu as pltpu

F32 = jnp.float32
BF16 = jnp.bfloat16
MXU_DTYPE = jnp.bfloat16
MESH = pl.DeviceIdType.MESH
N_DEV = 8
LANES = 128
SUBLANES = 8
VMEM_LIMIT = 48 * 1024 * 1024

DEPTH = 4
LN_EPS = 1e-5
DN_ALPHA = (2.0 * DEPTH) ** 0.25
GDN_HEADS, GDN_CONV, GDN_CHUNK = 8, 4, 64
RET_HEADS, RET_CHUNK, RET_ROPE_BASE = 4, 128, 10000.0
GMLP_CHUNK, GMLP_GROUPS = 128, 8
SB_HEADS = 16
ADAM_LR, ADAM_B1, ADAM_B2, ADAM_EPS, ADAM_WD, ADAM_STEP = 0.001, 0.9, 0.999, 1e-08, 0.01, 10

WEIGHTS = ['cond_w', 'cond_b', 'ada_w', 'ada_b', 'ln_g', 'ln_b', 'ffn_up', 'ffn_conv_w', 'ffn_conv_b', 'ffn_down',
           'gdn_w_in', 'gdn_conv_w', 'gdn_a_log', 'gdn_dt_bias', 'gdn_norm_w', 'gdn_w_out', 'ret_w_in', 'ret_w_out',
           'gmlp_w_in', 'gmlp_ln_g', 'gmlp_ln_b', 'gmlp_w_s', 'gmlp_b_s', 'gmlp_w_out', 'sb_w_in', 'sb_w_out']
BIG = {'ffn_up': 2, 'ffn_down': 1, 'gdn_w_in': 1, 'gdn_w_out': 0, 'ret_w_in': 1, 'ret_w_out': 0,
       'gmlp_w_in': 1, 'gmlp_w_out': 0, 'sb_w_in': 1, 'sb_w_out': 0}
SMALL_SHARDED = {'ln_g': 2, 'ln_b': 2, 'ffn_conv_w': 2, 'gdn_conv_w': 1}
SMALL_REPL = ['ffn_conv_b', 'gdn_a_log', 'gdn_dt_bias', 'gdn_norm_w', 'gmlp_ln_g', 'gmlp_ln_b', 'gmlp_w_s', 'gmlp_b_s']
MIXER_PARAMS = (('gdn_w_qkvz', 'gdn_w_ab', 'gdn_conv_w', 'gdn_a_log', 'gdn_dt_bias', 'gdn_norm_w', 'gdn_w_out'),
                ('ret_w_in', 'ret_w_out'),
                ('gmlp_w_in', 'gmlp_ln_g', 'gmlp_ln_b', 'gmlp_w_s', 'gmlp_b_s', 'gmlp_w_out'),
                ('sb_w_in', 'sb_w_out'))


def _pcall(body, **kw):
    return pl.pallas_call(body, **kw)


def _params(*semantics):
    return pltpu.CompilerParams(dimension_semantics=semantics, vmem_limit_bytes=VMEM_LIMIT)


def _my_id():
    return 4 * lax.axis_index("x") + 2 * lax.axis_index("y") + lax.axis_index("c")


def _pick(dim, prefs):
    for p in prefs:
        if dim % p == 0:
            return p
    return dim


def _exchange(src, scatter, name):
    blk = src.shape[1:] if scatter else src.shape
    out_shape = jax.ShapeDtypeStruct((N_DEV,) + tuple(blk), src.dtype)

    def body(src_ref, out_ref, send_sems, recv_sems, local_sem):
        x, y, c = lax.axis_index("x"), lax.axis_index("y"), lax.axis_index("c")
        me = 4 * x + 2 * y + c
        mine = pltpu.make_async_copy(src_ref.at[me] if scatter else src_ref, out_ref.at[me], local_sem)
        mine.start()
        copies = []
        for k in range(1, N_DEV):
            px = 1 - x if (k >> 2) & 1 else x
            py = 1 - y if (k >> 1) & 1 else y
            pc = 1 - c if k & 1 else c
            peer = 4 * px + 2 * py + pc
            cp = pltpu.make_async_remote_copy(
                src_ref=src_ref.at[peer] if scatter else src_ref,
                dst_ref=out_ref.at[me],
                send_sem=send_sems.at[k - 1], recv_sem=recv_sems.at[k - 1],
                device_id=(px, py, pc), device_id_type=MESH)
            cp.start()
            copies.append(cp)
        for cp in copies:
            cp.wait()
        mine.wait()

    return _pcall(
        body, name=name, out_shape=out_shape,
        in_specs=[pl.BlockSpec(memory_space=pl.ANY)],
        out_specs=pl.BlockSpec(memory_space=pl.ANY),
        scratch_shapes=[pltpu.SemaphoreType.DMA((N_DEV - 1,)), pltpu.SemaphoreType.DMA((N_DEV - 1,)),
                        pltpu.SemaphoreType.DMA(())],
    )(src)


def _flip(x, y, c, k):
    return (1 - x if (k >> 2) & 1 else x, 1 - y if (k >> 1) & 1 else y, 1 - c if k & 1 else c)


def _dev_id(p):
    return 4 * p[0] + 2 * p[1] + p[2]


OTHER_CHIPS = (4, 2, 6)


def _gather_two_level(src, name):
    out_shape = jax.ShapeDtypeStruct((N_DEV,) + tuple(src.shape), src.dtype)

    def body(x_ref, out_ref, send_sems, recv_sems, local_sem):
        x, y, c = lax.axis_index("x"), lax.axis_index("y"), lax.axis_index("c")
        me, sibling = (x, y, c), (x, y, 1 - c)
        chips = [_flip(x, y, c, k) for k in OTHER_CHIPS]

        def copy(k, block, to, from_src=False):
            slot = out_ref.at[_dev_id(block)]
            return pltpu.make_async_remote_copy(
                src_ref=x_ref if from_src else slot, dst_ref=slot,
                send_sem=send_sems.at[k], recv_sem=recv_sems.at[k], device_id=to, device_id_type=MESH)

        mine = pltpu.make_async_copy(x_ref, out_ref.at[_dev_id(me)], local_sem)
        mine.start()
        first = [copy(0, me, sibling, True)] + [copy(1 + j, me, chip, True) for j, chip in enumerate(chips)]
        for cp in first:
            cp.start()
        passed = [copy(4 + j, chip, sibling) for j, chip in enumerate(chips)]
        for j, chip in enumerate(chips):
            copy(1 + j, chip, me).wait_recv()
            passed[j].start()
        copy(0, sibling, me).wait_recv()
        for j, chip in enumerate(chips):
            copy(4 + j, (chip[0], chip[1], 1 - c), me).wait_recv()
        for cp in first + passed:
            cp.wait_send()
        mine.wait()

    return _pcall(
        body, name=name, out_shape=out_shape,
        in_specs=[pl.BlockSpec(memory_space=pl.ANY)],
        out_specs=pl.BlockSpec(memory_space=pl.ANY),
        scratch_shapes=[pltpu.SemaphoreType.DMA((N_DEV - 1,)), pltpu.SemaphoreType.DMA((N_DEV - 1,)),
                        pltpu.SemaphoreType.DMA(())],
    )(src)


def _send_slots(src, plan, n_out, name):
    out_shape = jax.ShapeDtypeStruct((n_out,) + tuple(src.shape[1:]), src.dtype)

    def body(src_ref, out_ref, send_sems, recv_sems):
        x, y, c = lax.axis_index("x"), lax.axis_index("y"), lax.axis_index("c")
        copies = []
        for e, (k, src_slot, dst_slot) in enumerate(plan):
            cp = pltpu.make_async_remote_copy(
                src_ref=src_ref.at[src_slot(x, y, c)], dst_ref=out_ref.at[dst_slot],
                send_sem=send_sems.at[e], recv_sem=recv_sems.at[e],
                device_id=_flip(x, y, c, k), device_id_type=MESH)
            cp.start()
            copies.append(cp)
        for cp in copies:
            cp.wait()

    return _pcall(
        body, name=name, out_shape=out_shape,
        in_specs=[pl.BlockSpec(memory_space=pl.ANY)],
        out_specs=pl.BlockSpec(memory_space=pl.ANY),
        scratch_shapes=[pltpu.SemaphoreType.DMA((len(plan),)), pltpu.SemaphoreType.DMA((len(plan),))],
    )(src)


def _reduce_to_owner(send):
    x, y, c = lax.axis_index("x"), lax.axis_index("y"), lax.axis_index("c")
    plan_a = [(1, lambda x, y, c: _dev_id((x, y, 1 - c)), 0)]
    plan_a += [(1, functools.partial(lambda k, x, y, c: _dev_id(_flip(x, y, c, k | 1)), k), 1 + j)
               for j, k in enumerate(OTHER_CHIPS)]
    from_sibling = _send_slots(send, plan_a, 1 + len(OTHER_CHIPS), "reduce_d2d")
    mine = jnp.stack([lax.dynamic_index_in_dim(send, _dev_id(_flip(x, y, c, k)), 0, keepdims=False)
                      for k in OTHER_CHIPS])
    rows = mine.shape[1]
    pair = _add_rows(mine.reshape(-1, LANES), from_sibling[1:].reshape(-1, LANES), "reduce_pair_sum")
    plan_c = [(k, functools.partial(lambda j, x, y, c: j, j), j) for j, k in enumerate(OTHER_CHIPS)]
    from_chips = _send_slots(pair.reshape(len(OTHER_CHIPS), rows, LANES), plan_c, len(OTHER_CHIPS), "reduce_ici")
    own = lax.dynamic_index_in_dim(send, _dev_id((x, y, c)), 0, keepdims=True)
    return [(own, 0), (from_sibling, 0)] + [(from_chips, j) for j in range(len(OTHER_CHIPS))]


ROW_ALIGN = 16
BIG_ROW_ALIGN = 512


def _pack_rows(parts, dtype, row_align=ROW_ALIGN):
    flat = jnp.concatenate([p.reshape(-1).astype(dtype) for p in parts])
    n = flat.shape[0]
    pad = (-n) % (row_align * LANES)
    if pad:
        flat = jnp.concatenate([flat, jnp.zeros((pad,), dtype)])
    return flat.reshape(-1, LANES)


def _unpack_rows(packed, shapes):
    lead = packed.shape[:-2]
    flat = packed.reshape(lead + (-1,))
    out, off = [], 0
    for s in shapes:
        n = math.prod(s)
        out.append(flat[..., off:off + n].reshape(lead + tuple(s)))
        off += n
    return out


def _mm(a, b, dims, name, exact=False):
    if dims == 'nn':
        (m, k), n = a.shape, b.shape[1]
    elif dims == 'nt':
        (m, k), n = a.shape, b.shape[0]
    else:
        (k, m), n = a.shape, b.shape[1]
    tm = _pick(m, (1024, 512, 256, 128))
    tn = _pick(n, (512, 256, 128))
    tk = k if k <= 2816 else _pick(k, (1024, 512, 256, 128))
    nk = k // tk
    if dims == 'nn':
        a_spec = pl.BlockSpec((tm, tk), lambda i, j, kk: (i, kk))
        b_spec = pl.BlockSpec((tk, tn), lambda i, j, kk: (kk, j))
        dn = (((1,), (0,)), ((), ()))
    elif dims == 'nt':
        a_spec = pl.BlockSpec((tm, tk), lambda i, j, kk: (i, kk))
        b_spec = pl.BlockSpec((tn, tk), lambda i, j, kk: (j, kk))
        dn = (((1,), (1,)), ((), ()))
    else:
        a_spec = pl.BlockSpec((tk, tm), lambda i, j, kk: (kk, i))
        b_spec = pl.BlockSpec((tk, tn), lambda i, j, kk: (kk, j))
        dn = (((0,), (0,)), ((), ()))

    def product(a_ref, b_ref):
        if exact:
            return lax.dot_general(a_ref[...], b_ref[...], dn, precision=lax.Precision.HIGHEST,
                                   preferred_element_type=F32)
        return lax.dot_general(a_ref[...].astype(MXU_DTYPE), b_ref[...].astype(MXU_DTYPE), dn,
                               preferred_element_type=F32)

    def body(a_ref, b_ref, o_ref, *acc):
        if nk == 1:
            o_ref[...] = product(a_ref, b_ref)
            return
        acc_ref, = acc
        kk = pl.program_id(2)

        @pl.when(kk == 0)
        def _():
            acc_ref[...] = jnp.zeros_like(acc_ref)

        acc_ref[...] += product(a_ref, b_ref)

        @pl.when(kk == nk - 1)
        def _():
            o_ref[...] = acc_ref[...]

    return _pcall(
        body, name=name, out_shape=jax.ShapeDtypeStruct((m, n), F32),
        grid=(m // tm, n // tn, nk),
        in_specs=[a_spec, b_spec],
        out_specs=pl.BlockSpec((tm, tn), lambda i, j, kk: (i, j)),
        scratch_shapes=[pltpu.VMEM((tm, tn), F32)] if nk > 1 else [],
        compiler_params=_params("parallel", "parallel", "arbitrary"),
    )(a, b)


def _mm_outer(a, b, name):
    pad = LANES - a.shape[0]
    return _mm(jnp.pad(a.T, ((0, 0), (0, pad))), jnp.pad(b, ((0, pad), (0, 0))), 'nn', name, exact=True)


@functools.partial(jax.custom_vjp, nondiff_argnums=(2,))
def _linear(a, w, name):
    return _mm(a, w, 'nn', name + "_fwd")


def _linear_fwd(a, w, name):
    return _mm(a, w, 'nn', name + "_fwd"), (a, w)


def _linear_bwd(name, res, dy):
    a, w = res
    return _mm(dy, w, 'nt', name + "_dx"), _mm(a, dy, 'tn', name + "_dw")


_linear.defvjp(_linear_fwd, _linear_bwd)


SB_BK = 256
SB_STRIP = 32


def _sb_tiles(s):
    tq = _pick(s, (512, 256, 128))
    bk = min(SB_BK, tq)
    return tq, bk, tq // bk


def _sb_valid(t, sr, bk, q0, k0):
    row = lax.broadcasted_iota(jnp.int32, (sr, bk), 0) + (q0 + t * sr)
    col = lax.broadcasted_iota(jnp.int32, (sr, bk), 1) + k0
    return col < row


def _sb_tri(bk, inclusive):
    r = lax.broadcasted_iota(jnp.int32, (bk, bk), 0)
    c = lax.broadcasted_iota(jnp.int32, (bk, bk), 1)
    return (r >= c).astype(BF16) if inclusive else (r > c).astype(BF16)


def _sb_split(ref, n, rows, hi_rows, val):
    hi = val.astype(BF16)
    ref[n, rows, :] = hi
    ref[n, hi_rows, :] = (val - hi.astype(F32)).astype(BF16)


def _sb_logits_phase(z_ref, ls_ref, hl_ref, l0_ref, n, tq, bk, sr, scale, q0, k0, masked):
    for t in range(tq // sr):
        rows = slice(t * sr, (t + 1) * sr)
        z = z_ref[n, rows, :] * scale
        ls = jnp.minimum(z, 0.0) - jnp.log(1.0 + jnp.exp(-jnp.abs(z)))
        lm = ls - z
        if masked:
            lm = jnp.where(_sb_valid(t, sr, bk, q0, k0), lm, 0.0)
        ls_ref[n, rows, :] = ls
        _sb_split(hl_ref, n, rows, slice(tq + t * sr, tq + (t + 1) * sr), lm)
        l0_ref[n, rows, :] = lm[:, 0:1]


def _sb_fwd_call(q, kt, v):
    h, s, dh = q.shape
    tq, bk, nt = _sb_tiles(s)
    sr = SB_STRIP
    scale = dh ** -0.5

    def body(q_ref, kt_ref, v_ref, o_ref, z_ref, ls_ref, hl_ref, f_ref, a_ref, l0_ref, cl_ref, acc_ref):
        i = pl.program_id(1)
        q0 = i * tq
        cl_ref[...] = jnp.zeros_like(cl_ref)
        acc_ref[...] = jnp.zeros_like(acc_ref)
        u_excl = _sb_tri(bk, False)

        def iteration(kb0, masked):
            k0s = [pl.multiple_of(kb0 + (nt - 1 - n) * bk, bk) for n in range(nt)]
            for n in range(nt):
                z_ref[n] = jnp.dot(q_ref[...], kt_ref[:, pl.ds(k0s[n], bk)], preferred_element_type=F32)
            for n in range(nt):
                _sb_logits_phase(z_ref, ls_ref, hl_ref, l0_ref, n, tq, bk, sr, scale, q0, k0s[n], masked)
            for n in range(nt):
                f_ref[n] = jnp.dot(hl_ref[n], u_excl, preferred_element_type=F32)
            for t in range(tq // sr):
                rows = slice(t * sr, (t + 1) * sr)
                hi_rows = slice(tq + t * sr, tq + (t + 1) * sr)
                c = cl_ref[rows, :]
                for n in range(nt):
                    f = f_ref[n, rows, :] + f_ref[n, hi_rows, :]
                    a = jnp.exp(ls_ref[n, rows, :] + f + c)
                    if masked:
                        a = jnp.where(_sb_valid(t, sr, bk, q0, k0s[n]), a, 0.0)
                    a_ref[n, rows, :] = a.astype(a_ref.dtype)
                    c = c + f[:, 0:1] + l0_ref[n, rows, :]
                cl_ref[rows, :] = c
            for n in range(nt):
                acc_ref[...] += jnp.dot(a_ref[n], v_ref[pl.ds(k0s[n], bk), :], preferred_element_type=F32)

        def below(jj, c):
            iteration((i - 1 - jj) * tq, False)
            return c

        iteration(q0, True)
        lax.fori_loop(0, i, below, 0)
        o_ref[...] = acc_ref[...]

    return _pcall(
        body, name="sb_fwd", out_shape=jax.ShapeDtypeStruct((h, s, dh), F32),
        grid=(h, s // tq),
        in_specs=[pl.BlockSpec((None, tq, dh), lambda hh, i: (hh, i, 0)),
                  pl.BlockSpec((None, dh, s), lambda hh, i: (hh, 0, 0)),
                  pl.BlockSpec((None, s, dh), lambda hh, i: (hh, 0, 0))],
        out_specs=pl.BlockSpec((None, tq, dh), lambda hh, i: (hh, i, 0)),
        scratch_shapes=[pltpu.VMEM((nt, tq, bk), F32), pltpu.VMEM((nt, tq, bk), F32),
                        pltpu.VMEM((nt, 2 * tq, bk), BF16), pltpu.VMEM((nt, 2 * tq, bk), F32),
                        pltpu.VMEM((nt, tq, bk), q.dtype), pltpu.VMEM((nt, tq, 1), F32),
                        pltpu.VMEM((tq, 1), F32), pltpu.VMEM((tq, dh), F32)],
        compiler_params=_params("parallel", "arbitrary"),
    )(q, kt, v)


def _sb_bwd_call(q, qt, k, kt, vt, o, do, dot):
    h, s, dh = q.shape
    tq, bk, nt = _sb_tiles(s)
    sr = SB_STRIP
    scale = dh ** -0.5

    def body(q_ref, qt_ref, k_ref, kt_ref, vt_ref, o_ref, do_ref, dot_ref, dq_ref, dkt_ref, dvt_ref,
             z_ref, ls_ref, hl_ref, f_ref, a_ref, g_ref, dz_ref, da_ref, l0_ref, dob_ref,
             cl_ref, cg_ref, dl_ref, dqa_ref):
        i = pl.program_id(1)
        q0 = i * tq

        @pl.when(i == 0)
        def _():
            dkt_ref[...] = jnp.zeros_like(dkt_ref)
            dvt_ref[...] = jnp.zeros_like(dvt_ref)

        cl_ref[...] = jnp.zeros_like(cl_ref)
        cg_ref[...] = jnp.zeros_like(cg_ref)
        dqa_ref[...] = jnp.zeros_like(dqa_ref)
        dob = do_ref[...].astype(dob_ref.dtype)
        dob_ref[...] = dob
        dl_ref[...] = jnp.sum(dob.astype(F32) * o_ref[...], axis=1, keepdims=True)
        u_excl = _sb_tri(bk, False)
        u_incl = _sb_tri(bk, True)

        def iteration(kb0, masked):
            k0s = [pl.multiple_of(kb0 + (nt - 1 - n) * bk, bk) for n in range(nt)]
            for n in range(nt):
                z_ref[n] = jnp.dot(q_ref[...], kt_ref[:, pl.ds(k0s[n], bk)], preferred_element_type=F32)
                da_ref[n] = jnp.dot(dob_ref[...], vt_ref[:, pl.ds(k0s[n], bk)], preferred_element_type=F32)
            for n in range(nt):
                _sb_logits_phase(z_ref, ls_ref, hl_ref, l0_ref, n, tq, bk, sr, scale, q0, k0s[n], masked)
            for n in range(nt):
                f_ref[n] = jnp.dot(hl_ref[n], u_excl, preferred_element_type=F32)
            for t in range(tq // sr):
                rows = slice(t * sr, (t + 1) * sr)
                hi_rows = slice(tq + t * sr, tq + (t + 1) * sr)
                c = cl_ref[rows, :]
                for n in range(nt):
                    f = f_ref[n, rows, :] + f_ref[n, hi_rows, :]
                    a = jnp.exp(ls_ref[n, rows, :] + f + c)
                    if masked:
                        a = jnp.where(_sb_valid(t, sr, bk, q0, k0s[n]), a, 0.0)
                    ab = a.astype(a_ref.dtype)
                    a_ref[n, rows, :] = ab
                    g = da_ref[n, rows, :] * ab.astype(F32)
                    g_ref[n, rows, :] = g
                    _sb_split(hl_ref, n, rows, hi_rows, g)
                    c = c + f[:, 0:1] + l0_ref[n, rows, :]
                cl_ref[rows, :] = c
            for n in range(nt):
                f_ref[n] = jnp.dot(hl_ref[n], u_incl, preferred_element_type=F32)
            for t in range(tq // sr):
                rows = slice(t * sr, (t + 1) * sr)
                hi_rows = slice(tq + t * sr, tq + (t + 1) * sr)
                cg = cg_ref[rows, :]
                for n in range(nt):
                    sg_tile = f_ref[n, rows, :] + f_ref[n, hi_rows, :]
                    p = dl_ref[rows, :] - (sg_tile + cg)
                    sig = jnp.exp(ls_ref[n, rows, :])
                    dz = g_ref[n, rows, :] * (1.0 - sig) - p * sig
                    if masked:
                        dz = jnp.where(_sb_valid(t, sr, bk, q0, k0s[n]), dz, 0.0)
                    dz_ref[n, rows, :] = (dz * scale).astype(dz_ref.dtype)
                    cg = cg + sg_tile[:, 0:1]
                cg_ref[rows, :] = cg
            for n in range(nt):
                cols = pl.ds(k0s[n], bk)
                dqa_ref[...] += jnp.dot(dz_ref[n], k_ref[cols, :], preferred_element_type=F32)
                dkt_ref[:, cols] += jnp.dot(qt_ref[...], dz_ref[n], preferred_element_type=F32)
                dvt_ref[:, cols] += jnp.dot(dot_ref[...], a_ref[n], preferred_element_type=F32)

        def below(jj, c):
            iteration((i - 1 - jj) * tq, False)
            return c

        iteration(q0, True)
        lax.fori_loop(0, i, below, 0)
        dq_ref[...] = dqa_ref[...]

    blk_q = pl.BlockSpec((None, tq, dh), lambda hh, i: (hh, i, 0))
    blk_qt = pl.BlockSpec((None, dh, tq), lambda hh, i: (hh, 0, i))
    blk_s = pl.BlockSpec((None, s, dh), lambda hh, i: (hh, 0, 0))
    blk_st = pl.BlockSpec((None, dh, s), lambda hh, i: (hh, 0, 0))
    mx = q.dtype
    return _pcall(
        body, name="sb_bwd",
        out_shape=(jax.ShapeDtypeStruct((h, s, dh), F32), jax.ShapeDtypeStruct((h, dh, s), F32),
                   jax.ShapeDtypeStruct((h, dh, s), F32)),
        grid=(h, s // tq),
        in_specs=[blk_q, blk_qt, blk_s, blk_st, blk_st, blk_q, blk_q, blk_qt],
        out_specs=(blk_q, blk_st, blk_st),
        scratch_shapes=[pltpu.VMEM((nt, tq, bk), F32), pltpu.VMEM((nt, tq, bk), F32),
                        pltpu.VMEM((nt, 2 * tq, bk), BF16), pltpu.VMEM((nt, 2 * tq, bk), F32),
                        pltpu.VMEM((nt, tq, bk), mx), pltpu.VMEM((nt, tq, bk), F32),
                        pltpu.VMEM((nt, tq, bk), mx), pltpu.VMEM((nt, tq, bk), F32),
                        pltpu.VMEM((nt, tq, 1), F32), pltpu.VMEM((tq, dh), mx),
                        pltpu.VMEM((tq, 1), F32), pltpu.VMEM((tq, 1), F32), pltpu.VMEM((tq, 1), F32),
                        pltpu.VMEM((tq, dh), F32)],
        compiler_params=_params("parallel", "arbitrary"),
    )(q, qt, k, kt, vt, o, do, dot)


def _swap(t):
    return t.transpose(0, 2, 1)


@jax.custom_vjp
def _sb_core(q, k, v):
    return _sb_fwd_call(q.astype(MXU_DTYPE), _swap(k.astype(MXU_DTYPE)), v.astype(MXU_DTYPE))


def _sb_core_fwd(q, k, v):
    qb, kb, vb = q.astype(MXU_DTYPE), k.astype(MXU_DTYPE), v.astype(MXU_DTYPE)
    o = _sb_fwd_call(qb, _swap(kb), vb)
    return o, (qb, kb, vb, o)


def _sb_core_bwd(res, do):
    qb, kb, vb, o = res
    dq, dkt, dvt = _sb_bwd_call(qb, _swap(qb), kb, _swap(kb), _swap(vb), o, do, _swap(do.astype(MXU_DTYPE)))
    return dq, _swap(dkt), _swap(dvt)


_sb_core.defvjp(_sb_core_fwd, _sb_core_bwd)


def _row_block(s):
    return _pick(s, (512, 256, 128, 64, 32, 16, 8))


def _fold8(t):
    r, c = t.shape
    return jnp.sum(t.reshape(r // SUBLANES, SUBLANES, c), axis=0)


def _vec(a):
    return a.reshape(1, -1)


def _modulate(x, sc, sh, out_dtype, name):
    s, d = x.shape
    tr = _row_block(s)

    def body(x_ref, sc_ref, sh_ref, o_ref):
        o_ref[...] = (x_ref[...] * (1.0 + sc_ref[...]) + sh_ref[...]).astype(o_ref.dtype)

    row = pl.BlockSpec((tr, d), lambda i: (i, 0))
    vec = pl.BlockSpec((1, d), lambda i: (0, 0))
    return _pcall(body, name=name, out_shape=jax.ShapeDtypeStruct((s, d), out_dtype), grid=(s // tr,),
                  in_specs=[row, vec, vec], out_specs=row, compiler_params=_params("parallel"))(x, _vec(sc), _vec(sh))


def _modulate_bwd(dxa, dh, x, sc, name):
    s, d = x.shape
    tr = _row_block(s)

    def body(dxa_ref, dh_ref, x_ref, sc_ref, dx_ref, acc_ref):
        @pl.when(pl.program_id(0) == 0)
        def _():
            acc_ref[...] = jnp.zeros_like(acc_ref)

        dh = dh_ref[...]
        dx_ref[...] = dxa_ref[...] + dh * (1.0 + sc_ref[...])
        acc_ref[0] += _fold8(dh * x_ref[...])
        acc_ref[1] += _fold8(dh)

    row = pl.BlockSpec((tr, d), lambda i: (i, 0))
    vec = pl.BlockSpec((1, d), lambda i: (0, 0))
    dx, acc = _pcall(
        body, name=name,
        out_shape=(jax.ShapeDtypeStruct((s, d), F32), jax.ShapeDtypeStruct((2, SUBLANES, d), F32)),
        grid=(s // tr,), in_specs=[row, row, row, vec],
        out_specs=(row, pl.BlockSpec((2, SUBLANES, d), lambda i: (0, 0, 0))),
        compiler_params=_params("arbitrary"))(dxa, dh, x, _vec(sc))
    acc = jnp.sum(acc, axis=1)
    return dx, acc[0], acc[1]


def _resid_ln(x, y, g, gamma, beta, name):
    s, d = x.shape
    tr = _row_block(s)

    def body(x_ref, y_ref, g_ref, gam_ref, bet_ref, o_ref):
        u = DN_ALPHA * x_ref[...] + (1.0 + g_ref[...]) * y_ref[...]
        uc = u - jnp.mean(u, axis=-1, keepdims=True)
        var = jnp.mean(uc * uc, axis=-1, keepdims=True)
        o_ref[...] = uc * lax.rsqrt(var + LN_EPS) * gam_ref[...] + bet_ref[...]

    row = pl.BlockSpec((tr, d), lambda i: (i, 0))
    vec = pl.BlockSpec((1, d), lambda i: (0, 0))
    return _pcall(body, name=name, out_shape=jax.ShapeDtypeStruct((s, d), F32), grid=(s // tr,),
                  in_specs=[row, row, vec, vec, vec], out_specs=row,
                  compiler_params=_params("parallel"))(x, y, _vec(g), _vec(gamma), _vec(beta))


def _resid_ln_bwd(x, y, g, gamma, dout, name):
    s, d = x.shape
    tr = _row_block(s)

    def body(x_ref, y_ref, g_ref, gam_ref, do_ref, dxa_ref, dy_ref, acc_ref):
        @pl.when(pl.program_id(0) == 0)
        def _():
            acc_ref[...] = jnp.zeros_like(acc_ref)

        y = y_ref[...]
        gg = 1.0 + g_ref[...]
        u = DN_ALPHA * x_ref[...] + gg * y
        uc = u - jnp.mean(u, axis=-1, keepdims=True)
        rstd = lax.rsqrt(jnp.mean(uc * uc, axis=-1, keepdims=True) + LN_EPS)
        xhat = uc * rstd
        dout = do_ref[...]
        dxh = dout * gam_ref[...]
        du = rstd * (dxh - jnp.mean(dxh, axis=-1, keepdims=True)
                     - xhat * jnp.mean(dxh * xhat, axis=-1, keepdims=True))
        dxa_ref[...] = DN_ALPHA * du
        dy_ref[...] = gg * du
        acc_ref[0] += _fold8(dout * xhat)
        acc_ref[1] += _fold8(dout)
        acc_ref[2] += _fold8(du * y)

    row = pl.BlockSpec((tr, d), lambda i: (i, 0))
    vec = pl.BlockSpec((1, d), lambda i: (0, 0))
    dxa, dy, acc = _pcall(
        body, name=name,
        out_shape=(jax.ShapeDtypeStruct((s, d), F32), jax.ShapeDtypeStruct((s, d), F32),
                   jax.ShapeDtypeStruct((3, SUBLANES, d), F32)),
        grid=(s // tr,), in_specs=[row, row, vec, vec, row],
        out_specs=(row, row, pl.BlockSpec((3, SUBLANES, d), lambda i: (0, 0, 0))),
        compiler_params=_params("arbitrary"))(x, y, _vec(g), _vec(gamma), dout)
    acc = jnp.sum(acc, axis=1)
    return dxa, dy, acc[0], acc[1], acc[2]


def _loss_head(x, target, name):
    s, d = x.shape
    tr = _row_block(s)

    def body(x_ref, t_ref, dx_ref, acc_ref):
        @pl.when(pl.program_id(0) == 0)
        def _():
            acc_ref[...] = jnp.zeros_like(acc_ref)

        e = x_ref[...] - t_ref[...]
        dx_ref[...] = e * (1.0 / d)
        acc_ref[...] += _fold8(e * e)

    row = pl.BlockSpec((tr, d), lambda i: (i, 0))
    dx, acc = _pcall(
        body, name=name,
        out_shape=(jax.ShapeDtypeStruct((s, d), F32), jax.ShapeDtypeStruct((SUBLANES, d), F32)),
        grid=(s // tr,), in_specs=[row, row],
        out_specs=(row, pl.BlockSpec((SUBLANES, d), lambda i: (0, 0))),
        compiler_params=_params("arbitrary"))(x, target)
    return (0.5 / d) * jnp.sum(acc), dx


CONV_STRIPE = 128
CONV_ROWS = 256


def _shift_down(cur, halo, k):
    ext = jnp.concatenate([halo, cur], axis=0)
    return pltpu.roll(ext, k, 0)[SUBLANES:]


def _shift_up(cur, halo, k):
    ext = jnp.concatenate([cur, halo], axis=0)
    n = ext.shape[0]
    return pltpu.roll(ext, n - k, 0)[:n - SUBLANES]


def _gate_chunk(g_ref, r, rc):
    r0 = pl.multiple_of(r * rc, rc)
    cur = g_ref[pl.ds(r0, rc), :]
    hs = pl.multiple_of(jnp.maximum(r0 - SUBLANES, 0), SUBLANES)
    halo = jnp.where(r > 0, g_ref[pl.ds(hs, SUBLANES), :], 0.0)
    return r0, cur, _shift_down(cur, halo, 1), _shift_down(cur, halo, 2)


def _ffn_gate(gu, cw, cb, name):
    s, f2 = gu.shape
    f = f2 // 2
    tc = _pick(f, (CONV_STRIPE,))
    rc = _pick(s, (CONV_ROWS, 128, 64, 32, 16, 8))
    nj = f // tc

    def body(g_ref, u_ref, cw_ref, cb_ref, a_ref):
        w0, w1, w2, b = cw_ref[0:1, :], cw_ref[1:2, :], cw_ref[2:3, :], cb_ref[...]

        def chunk(r, c):
            r0, cur, x1, x2 = _gate_chunk(g_ref, r, rc)
            gc = w2 * cur + w1 * x1 + w0 * x2 + b
            a_ref[pl.ds(r0, rc), :] = (gc * jax.nn.sigmoid(gc) * u_ref[pl.ds(r0, rc), :]).astype(a_ref.dtype)
            return c

        lax.fori_loop(0, s // rc, chunk, 0)

    return _pcall(
        body, name=name, out_shape=jax.ShapeDtypeStruct((s, f), MXU_DTYPE), grid=(nj,),
        in_specs=[pl.BlockSpec((s, tc), lambda j: (0, j)), pl.BlockSpec((s, tc), lambda j: (0, j + nj)),
                  pl.BlockSpec((3, tc), lambda j: (0, j)), pl.BlockSpec((1, tc), lambda j: (0, j))],
        out_specs=pl.BlockSpec((s, tc), lambda j: (0, j)),
        compiler_params=_params("parallel"))(gu, gu, cw, _vec(cb))


def _ffn_gate_bwd(da, gu, cw, cb, name):
    s, f2 = gu.shape
    f = f2 // 2
    tc = _pick(f, (CONV_STRIPE,))
    rc = _pick(s, (CONV_ROWS, 128, 64, 32, 16, 8))
    nj = f // tc
    nr = s // rc

    def body(da_ref, g_ref, u_ref, cw_ref, cb_ref, dg_ref, du_ref, acc_ref, dgc_ref):
        w0, w1, w2, b = cw_ref[0:1, :], cw_ref[1:2, :], cw_ref[2:3, :], cb_ref[...]

        def chunk1(r, carry):
            a0, a1, a2, ab = carry
            r0, cur, x1, x2 = _gate_chunk(g_ref, r, rc)
            gc = w2 * cur + w1 * x1 + w0 * x2 + b
            sg = jax.nn.sigmoid(gc)
            da_c = da_ref[pl.ds(r0, rc), :]
            du_ref[pl.ds(r0, rc), :] = (da_c * (gc * sg)).astype(du_ref.dtype)
            dgc = da_c * u_ref[pl.ds(r0, rc), :] * (sg * (1.0 + gc * (1.0 - sg)))
            dgc_ref[pl.ds(r0, rc), :] = dgc
            return a0 + _fold8(dgc * x2), a1 + _fold8(dgc * x1), a2 + _fold8(dgc * cur), ab + _fold8(dgc)

        zero = jnp.zeros((SUBLANES, tc), F32)
        a0, a1, a2, ab = lax.fori_loop(0, nr, chunk1, (zero, zero, zero, zero))
        acc_ref[0], acc_ref[1], acc_ref[2], acc_ref[3] = a0, a1, a2, ab

        def chunk2(r, c):
            r0 = pl.multiple_of(r * rc, rc)
            cur = dgc_ref[pl.ds(r0, rc), :]
            hs = pl.multiple_of(jnp.minimum(r0 + rc, s - SUBLANES), SUBLANES)
            halo = jnp.where(r < nr - 1, dgc_ref[pl.ds(hs, SUBLANES), :], 0.0)
            dg = w2 * cur + w1 * _shift_up(cur, halo, 1) + w0 * _shift_up(cur, halo, 2)
            dg_ref[pl.ds(r0, rc), :] = dg.astype(dg_ref.dtype)
            return c

        lax.fori_loop(0, nr, chunk2, 0)

    stripe = pl.BlockSpec((s, tc), lambda j: (0, j))
    dg, du, acc = _pcall(
        body, name=name,
        out_shape=(jax.ShapeDtypeStruct((s, f), MXU_DTYPE), jax.ShapeDtypeStruct((s, f), MXU_DTYPE),
                   jax.ShapeDtypeStruct((4, SUBLANES, f), F32)),
        grid=(nj,),
        in_specs=[stripe, stripe, pl.BlockSpec((s, tc), lambda j: (0, j + nj)),
                  pl.BlockSpec((3, tc), lambda j: (0, j)), pl.BlockSpec((1, tc), lambda j: (0, j))],
        out_specs=(stripe, stripe, pl.BlockSpec((4, SUBLANES, tc), lambda j: (0, 0, j))),
        scratch_shapes=[pltpu.VMEM((s, tc), F32)],
        compiler_params=_params("parallel"))(da, gu, gu, cw, _vec(cb))
    acc = jnp.sum(acc, axis=1)
    return dg, du, acc[:3], acc[3]


def _add_rows(a, b, name):
    r = a.shape[0]
    tr = _pick(r, (1024, 512, 256, 128, 64, 32, 16, 8))

    def body(a_ref, b_ref, o_ref):
        o_ref[...] = a_ref[...] + b_ref[...]

    row = pl.BlockSpec((tr, LANES), lambda i: (i, 0))
    return _pcall(body, name=name, out_shape=jax.ShapeDtypeStruct(a.shape, a.dtype), grid=(r // tr,),
                  in_specs=[row, row], out_specs=row, compiler_params=_params("parallel"))(a, b)


def _adamw(gparts, w, m, v, name):
    r = w.shape[0]
    tr = _pick(r, (1024, 512, 256, 128, 64, 32, 16, 8))
    bc1 = 1.0 / (1.0 - ADAM_B1 ** ADAM_STEP)
    bc2 = 1.0 / (1.0 - ADAM_B2 ** ADAM_STEP)
    n = len(gparts)

    def body(*refs):
        w_ref, m_ref, v_ref, go_ref, d_ref, mo_ref, vo_ref = refs[n:]
        g = refs[0][...]
        for t in range(1, n):
            g = g + refs[t][...]
        mn = ADAM_B1 * m_ref[...] + (1.0 - ADAM_B1) * g
        vn = ADAM_B2 * v_ref[...] + (1.0 - ADAM_B2) * (g * g)
        m_hat = mn * bc1
        v_hat = vn * bc2
        go_ref[...] = g
        d_ref[...] = -ADAM_LR * (m_hat / (jnp.sqrt(v_hat) + ADAM_EPS) + ADAM_WD * w_ref[...])
        mo_ref[...] = mn
        vo_ref[...] = vn

    row = pl.BlockSpec((tr, LANES), lambda i: (i, 0))
    sds = jax.ShapeDtypeStruct((r, LANES), F32)
    return _pcall(
        body, name=name, out_shape=(sds, sds, sds, sds),
        grid=(r // tr,),
        in_specs=[pl.BlockSpec((None, tr, LANES), functools.partial(lambda slot, i: (slot, i, 0), slot))
                  for _, slot in gparts] + [row, row, row],
        out_specs=(row, row, row, row),
        compiler_params=_params("parallel"),
    )(*[a for a, _ in gparts], w, m, v)


def _sum_slots(gslots, name):
    n, r, _ = gslots.shape
    tr = _pick(r, (1024, 512, 256, 128, 64, 32, 16, 8))

    def body(g_ref, o_ref):
        g = g_ref[0]
        for t in range(1, n):
            g = g + g_ref[t]
        o_ref[...] = g

    return _pcall(
        body, name=name, out_shape=jax.ShapeDtypeStruct((r, LANES), F32),
        grid=(r // tr,),
        in_specs=[pl.BlockSpec((n, tr, LANES), lambda i: (0, i, 0))],
        out_specs=pl.BlockSpec((tr, LANES), lambda i: (i, 0)),
        compiler_params=_params("parallel"),
    )(gslots)


def _standardize(x, eps):
    mu = jnp.mean(x, axis=-1, keepdims=True)
    xc = x - mu
    var = jnp.mean(xc * xc, axis=-1, keepdims=True)
    return xc * lax.rsqrt(var + eps)


def _layer_norm(x, g, b):
    return _standardize(x, LN_EPS) * g + b


def _l2norm(x, eps=1e-6):
    return x * lax.rsqrt(jnp.sum(x * x, axis=-1, keepdims=True) + eps)


def _causal_dwconv(x, w):
    k_w, s = w.shape[0], x.shape[0]
    xp = jnp.pad(x, ((k_w - 1, 0), (0, 0)))
    y = xp[k_w - 1:k_w - 1 + s] * w[k_w - 1]
    for j in range(k_w - 1):
        y = y + xp[j:j + s] * w[j]
    return y


def _chunk_heads(t, n_heads, chunk):
    s, hd = t.shape
    return t.reshape(s // chunk, chunk, n_heads, hd // n_heads).transpose(2, 0, 1, 3)


def _unchunk_heads(t):
    h, n, c, d = t.shape
    return t.transpose(1, 2, 0, 3).reshape(n * c, h, d)


_NT = (((1,), (1,)), ((), ()))
_TN = (((0,), (0,)), ((), ()))


def _gdn_blocks(a, rev_from=None):
    h, _, r, c = a.shape
    if rev_from is None:
        return pl.BlockSpec((h, None, r, c), lambda n: (0, n, 0, 0))
    return pl.BlockSpec((h, None, r, c), lambda n: (0, rev_from - n, 0, 0))


def _gdn_scan_fwd_call(qg, w, u, qk, kd, e):
    H, n_chunks, c, dk = qg.shape
    dv = u.shape[-1]

    def body(qg_ref, w_ref, u_ref, qk_ref, kd_ref, e_ref, o_ref, sin_ref, vn_ref, state_ref):
        @pl.when(pl.program_id(0) == 0)
        def _():
            state_ref[...] = jnp.zeros_like(state_ref)

        for hd in range(H):
            st = state_ref[hd]
            stb = st.astype(MXU_DTYPE)
            sin_ref[hd] = st
            v_new = u_ref[hd] - jnp.dot(w_ref[hd].astype(MXU_DTYPE), stb, preferred_element_type=F32)
            vn_ref[hd] = v_new
            vnb = v_new.astype(MXU_DTYPE)
            o_ref[hd] = (jnp.dot(qg_ref[hd].astype(MXU_DTYPE), stb, preferred_element_type=F32)
                         + jnp.dot(qk_ref[hd].astype(MXU_DTYPE), vnb, preferred_element_type=F32))
            state_ref[hd] = st * e_ref[hd] + lax.dot_general(kd_ref[hd].astype(MXU_DTYPE), vnb, _TN,
                                                             preferred_element_type=F32)

    ins = (qg, w, u, qk, kd, e)
    outs = (jax.ShapeDtypeStruct((H, n_chunks, c, dv), F32), jax.ShapeDtypeStruct((H, n_chunks, dk, dv), F32),
            jax.ShapeDtypeStruct((H, n_chunks, c, dv), F32))
    return _pcall(
        body, name="gdn_scan_fwd", out_shape=outs, grid=(n_chunks,),
        in_specs=[_gdn_blocks(a) for a in ins], out_specs=tuple(_gdn_blocks(a) for a in outs),
        scratch_shapes=[pltpu.VMEM((H, dk, dv), F32)],
        compiler_params=_params("arbitrary"),
    )(*ins)


def _gdn_scan_bwd_call(qg, w, qk, kd, e, s_in, v_new, do):
    H, n_chunks, c, dk = qg.shape
    dv = v_new.shape[-1]

    def body(qg_ref, w_ref, qk_ref, kd_ref, e_ref, sin_ref, vn_ref, do_ref,
             dqg_ref, dw_ref, du_ref, dqk_ref, dkd_ref, de_ref, ds_ref):
        @pl.when(pl.program_id(0) == 0)
        def _():
            ds_ref[...] = jnp.zeros_like(ds_ref)

        for hd in range(H):
            st = sin_ref[hd]
            stb = st.astype(MXU_DTYPE)
            vnb = vn_ref[hd].astype(MXU_DTYPE)
            dob = do_ref[hd].astype(MXU_DTYPE)
            ds = ds_ref[hd]
            dsb = ds.astype(MXU_DTYPE)
            dvn = lax.dot_general(qk_ref[hd].astype(MXU_DTYPE), dob, _TN, preferred_element_type=F32)
            dqk_ref[hd] = lax.dot_general(dob, vnb, _NT, preferred_element_type=F32)
            dqg_ref[hd] = lax.dot_general(dob, stb, _NT, preferred_element_type=F32)
            ds_in = lax.dot_general(qg_ref[hd].astype(MXU_DTYPE), dob, _TN, preferred_element_type=F32)
            dvn = dvn + jnp.dot(kd_ref[hd].astype(MXU_DTYPE), dsb, preferred_element_type=F32)
            dkd_ref[hd] = lax.dot_general(vnb, dsb, _NT, preferred_element_type=F32)
            de_ref[hd] = _fold8(st * ds)
            ds_in = ds_in + ds * e_ref[hd]
            du_ref[hd] = dvn
            dvnb = dvn.astype(MXU_DTYPE)
            dw_ref[hd] = -lax.dot_general(dvnb, stb, _NT, preferred_element_type=F32)
            ds_ref[hd] = ds_in - lax.dot_general(w_ref[hd].astype(MXU_DTYPE), dvnb, _TN,
                                                 preferred_element_type=F32)

    last = n_chunks - 1
    ins = (qg, w, qk, kd, e, s_in, v_new, do)
    outs = (jax.ShapeDtypeStruct(qg.shape, F32), jax.ShapeDtypeStruct(w.shape, F32),
            jax.ShapeDtypeStruct(v_new.shape, F32), jax.ShapeDtypeStruct(qk.shape, F32),
            jax.ShapeDtypeStruct(kd.shape, F32), jax.ShapeDtypeStruct((H, n_chunks, SUBLANES, dv), F32))
    return _pcall(
        body, name="gdn_scan_bwd", out_shape=outs, grid=(n_chunks,),
        in_specs=[_gdn_blocks(a, last) for a in ins], out_specs=tuple(_gdn_blocks(a, last) for a in outs),
        scratch_shapes=[pltpu.VMEM((H, dk, dv), F32)],
        compiler_params=_params("arbitrary"),
    )(*ins)


@jax.custom_vjp
def _gdn_scan(qg, w, u, qk, kd, e):
    return _gdn_scan_fwd_call(qg, w, u, qk, kd, e)[0]


def _gdn_scan_fwd(qg, w, u, qk, kd, e):
    o, s_in, v_new = _gdn_scan_fwd_call(qg, w, u, qk, kd, e)
    return o, (qg, w, qk, kd, e, s_in, v_new)


def _gdn_scan_bwd(res, do):
    dqg, dw, du, dqk, dkd, de = _gdn_scan_bwd_call(*res, do)
    return dqg, dw, du, dqk, dkd, jnp.sum(de, axis=2, keepdims=True)


_gdn_scan.defvjp(_gdn_scan_fwd, _gdn_scan_bwd)


def _gated_deltanet(hx, p):
    H, C = GDN_HEADS, GDN_CHUNK
    s, d = hx.shape
    dk = dv = d // H
    qkvz = _linear(hx, p['gdn_w_qkvz'], "gdn_in")
    ab = _linear(hx, p['gdn_w_ab'], "gdn_ab")
    qkv, z = qkvz[:, :3 * d], qkvz[:, 3 * d:]
    a, bt = ab[:, :H], ab[:, H:2 * H]
    qkv = jax.nn.silu(_causal_dwconv(qkv, p['gdn_conv_w']))
    q, k, v = qkv[:, :d], qkv[:, d:2 * d], qkv[:, 2 * d:]
    q = _l2norm(_chunk_heads(q, H, C)) * (dk ** -0.5)
    k = _l2norm(_chunk_heads(k, H, C))
    v = _chunk_heads(v, H, C)
    beta = jax.nn.sigmoid(_chunk_heads(bt, H, C)[..., 0])
    g = -jnp.exp(p['gdn_a_log'])[:, None, None] * jax.nn.softplus(
        _chunk_heads(a, H, C)[..., 0] + p['gdn_dt_bias'][:, None, None])
    gc = jnp.cumsum(g, axis=-1)
    idx = jnp.arange(C)
    causal = idx[:, None] >= idx[None, :]
    strict = idx[:, None] > idx[None, :]
    diff = gc[..., :, None] - gc[..., None, :]
    decay = jnp.where(causal, jnp.exp(jnp.where(causal, diff, 0.0)), 0.0)
    kb = k * beta[..., None]
    kk = jnp.where(strict, jnp.einsum('hncd,hnmd->hncm', kb, k) * decay, 0.0)
    eye = jnp.eye(C, dtype=F32)
    rhs = jnp.concatenate([v * beta[..., None], kb * jnp.exp(gc)[..., None]], axis=-1)
    sol = lax.linalg.triangular_solve(kk + eye, rhs, left_side=True, lower=True, unit_diagonal=True)
    u, w = sol[..., :dv], sol[..., dv:]
    qk = jnp.where(causal, jnp.einsum('hncd,hnmd->hncm', q, k) * decay, 0.0)

    g_last = gc[..., -1:]
    e = jnp.broadcast_to(jnp.exp(g_last)[..., None], gc.shape[:2] + (1, dv))
    o = _gdn_scan(q * jnp.exp(gc)[..., None], w, u, qk, k * jnp.exp(g_last - gc)[..., None], e)
    o = _unchunk_heads(o)
    o = o * lax.rsqrt(jnp.mean(o * o, axis=-1, keepdims=True) + 1e-6) * p['gdn_norm_w']
    o = o * jax.nn.silu(z.reshape(s, H, dv))
    return _linear(o.reshape(s, H * dv), p['gdn_w_out'], "gdn_out")


def _ret_consts(c):
    log_gamma = jnp.log(1.0 - jnp.power(2.0, -5.0 - jnp.arange(RET_HEADS, dtype=F32)))
    idx = jnp.arange(c, dtype=F32)
    rel = idx[:, None] - idx[None, :]
    dmask = jnp.where(rel >= 0, jnp.exp(jnp.maximum(rel, 0.0) * log_gamma[:, None, None]), 0.0)
    zeta = jnp.exp((c - 1.0 - idx)[None, :] * log_gamma[:, None])[..., None]
    xi = jnp.exp((idx + 1.0)[None, :] * log_gamma[:, None])[..., None]
    gamma_c = jnp.exp(c * log_gamma)[:, None, None]
    return dmask, zeta, xi, gamma_c


def _ret_angles(s, dk):
    pos = jnp.arange(s, dtype=F32)
    inv_freq = RET_ROPE_BASE ** (-jnp.linspace(0.0, 1.0, dk // 2, dtype=F32))
    ang = pos[:, None] * inv_freq[None, :]
    return jnp.cos(ang), jnp.sin(ang)


def _rot(t, cs, sn):
    half = t.shape[1] // 2
    t1, t2 = t[:, :half], t[:, half:]
    return jnp.concatenate([t1 * cs - t2 * sn, t1 * sn + t2 * cs], axis=1)


def _rot_t(t, cs, sn):
    half = t.shape[1] // 2
    t1, t2 = t[:, :half], t[:, half:]
    return jnp.concatenate([t1 * cs + t2 * sn, t2 * cs - t1 * sn], axis=1)


def _ret_cols(d, dk, dv, hd):
    return (slice(hd * dk, (hd + 1) * dk), slice(d + hd * dk, d + (hd + 1) * dk),
            slice(2 * d + hd * dv, 2 * d + (hd + 1) * dv), slice(4 * d + hd * dv, 4 * d + (hd + 1) * dv))


def _ret_fwd_call(proj):
    s, d6 = proj.shape
    d = d6 // 6
    H, c = RET_HEADS, RET_CHUNK
    dk, dv = d // H, 2 * d // H
    n_chunks = s // c
    kscale = dk ** -0.5
    cos_a, sin_a = _ret_angles(s, dk)
    consts = _ret_consts(c)

    def body(p_ref, cos_ref, sin_ref, dm_ref, ze_ref, xi_ref, gc_ref, out_ref, oraw_ref, st_ref, state_ref):
        @pl.when(pl.program_id(0) == 0)
        def _():
            state_ref[...] = jnp.zeros_like(state_ref)

        cs, sn = cos_ref[...], sin_ref[...]
        for hd in range(H):
            qc, kc, vc, gcol = _ret_cols(d, dk, dv, hd)
            ocol = slice(hd * dv, (hd + 1) * dv)
            qb = _rot(p_ref[:, qc], cs, sn).astype(MXU_DTYPE)
            kr = _rot(p_ref[:, kc], cs, sn) * kscale
            kb = kr.astype(MXU_DTYPE)
            vb = p_ref[:, vc].astype(MXU_DTYPE)
            st = state_ref[hd]
            stb = st.astype(MXU_DTYPE)
            st_ref[hd] = stb
            sc = lax.dot_general(qb, kb, _NT, preferred_element_type=F32) * dm_ref[hd]
            o = (jnp.dot(sc.astype(MXU_DTYPE), vb, preferred_element_type=F32)
                 + jnp.dot(qb, stb, preferred_element_type=F32) * xi_ref[hd])
            state_ref[hd] = st * gc_ref[hd] + lax.dot_general((kr * ze_ref[hd]).astype(MXU_DTYPE), vb, _TN,
                                                              preferred_element_type=F32)
            oraw_ref[:, ocol] = o
            oc = o - jnp.mean(o, axis=-1, keepdims=True)
            on = oc * lax.rsqrt(jnp.mean(oc * oc, axis=-1, keepdims=True) + 1e-6)
            gate = p_ref[:, gcol]
            out_ref[:, ocol] = on * (gate * jax.nn.sigmoid(gate))

    row = lambda width: pl.BlockSpec((c, width), lambda n: (n, 0))
    whole = lambda a: pl.BlockSpec(a.shape, lambda n: (0,) * a.ndim)
    return _pcall(
        body, name="ret_fwd",
        out_shape=(jax.ShapeDtypeStruct((s, 2 * d), F32), jax.ShapeDtypeStruct((s, 2 * d), F32),
                   jax.ShapeDtypeStruct((n_chunks, H, dk, dv), MXU_DTYPE)),
        grid=(n_chunks,),
        in_specs=[row(d6), row(dk // 2), row(dk // 2)] + [whole(a) for a in consts],
        out_specs=(row(2 * d), row(2 * d), pl.BlockSpec((None, H, dk, dv), lambda n: (n, 0, 0, 0))),
        scratch_shapes=[pltpu.VMEM((H, dk, dv), F32)],
        compiler_params=_params("arbitrary"),
    )(proj, cos_a, sin_a, *consts)


def _ret_bwd_call(proj, oraw, states, dout):
    s, d6 = proj.shape
    d = d6 // 6
    H, c = RET_HEADS, RET_CHUNK
    dk, dv = d // H, 2 * d // H
    n_chunks = s // c
    kscale = dk ** -0.5
    cos_a, sin_a = _ret_angles(s, dk)
    consts = _ret_consts(c)

    def body(p_ref, cos_ref, sin_ref, dm_ref, ze_ref, xi_ref, gc_ref, oraw_ref, st_ref, do_ref, dp_ref, ds_ref):
        @pl.when(pl.program_id(0) == 0)
        def _():
            ds_ref[...] = jnp.zeros_like(ds_ref)

        cs, sn = cos_ref[...], sin_ref[...]
        for hd in range(H):
            qc, kc, vc, gcol = _ret_cols(d, dk, dv, hd)
            ocol = slice(hd * dv, (hd + 1) * dv)
            qb = _rot(p_ref[:, qc], cs, sn).astype(MXU_DTYPE)
            kr = _rot(p_ref[:, kc], cs, sn) * kscale
            kb = kr.astype(MXU_DTYPE)
            vb = p_ref[:, vc].astype(MXU_DTYPE)
            gate = p_ref[:, gcol]
            o = oraw_ref[:, ocol]
            oc = o - jnp.mean(o, axis=-1, keepdims=True)
            rstd = lax.rsqrt(jnp.mean(oc * oc, axis=-1, keepdims=True) + 1e-6)
            on = oc * rstd
            dout_h = do_ref[:, ocol]
            sg = jax.nn.sigmoid(gate)
            dp_ref[:, gcol] = dout_h * on * (sg * (1.0 + gate * (1.0 - sg)))
            don = dout_h * (gate * sg)
            do_raw = rstd * (don - jnp.mean(don, axis=-1, keepdims=True)
                             - on * jnp.mean(don * on, axis=-1, keepdims=True))
            dob = do_raw.astype(MXU_DTYPE)
            stb = st_ref[hd]
            ds = ds_ref[hd]
            dsb = ds.astype(MXU_DTYPE)
            dm = dm_ref[hd]
            scb = (lax.dot_general(qb, kb, _NT, preferred_element_type=F32) * dm).astype(MXU_DTYPE)
            dsc = (lax.dot_general(dob, vb, _NT, preferred_element_type=F32) * dm).astype(MXU_DTYPE)
            dqr = jnp.dot(dsc, kb, preferred_element_type=F32)
            dkr = lax.dot_general(dsc, qb, _TN, preferred_element_type=F32)
            dvv = lax.dot_general(scb, dob, _TN, preferred_element_type=F32)
            doi = (do_raw * xi_ref[hd]).astype(MXU_DTYPE)
            dqr = dqr + lax.dot_general(doi, stb, _NT, preferred_element_type=F32)
            ds_in = lax.dot_general(qb, doi, _TN, preferred_element_type=F32)
            ze = ze_ref[hd]
            dkr = dkr + lax.dot_general(vb, dsb, _NT, preferred_element_type=F32) * ze
            dvv = dvv + jnp.dot((kr * ze).astype(MXU_DTYPE), dsb, preferred_element_type=F32)
            ds_ref[hd] = ds * gc_ref[hd] + ds_in
            dp_ref[:, qc] = _rot_t(dqr, cs, sn)
            dp_ref[:, kc] = _rot_t(dkr * kscale, cs, sn)
            dp_ref[:, vc] = dvv

    last = n_chunks - 1
    row = lambda width: pl.BlockSpec((c, width), lambda n: (last - n, 0))
    whole = lambda a: pl.BlockSpec(a.shape, lambda n: (0,) * a.ndim)
    return _pcall(
        body, name="ret_bwd", out_shape=jax.ShapeDtypeStruct((s, d6), F32),
        grid=(n_chunks,),
        in_specs=[row(d6), row(dk // 2), row(dk // 2)] + [whole(a) for a in consts]
                 + [row(2 * d), pl.BlockSpec((None, H, dk, dv), lambda n: (last - n, 0, 0, 0)), row(2 * d)],
        out_specs=row(d6),
        scratch_shapes=[pltpu.VMEM((H, dk, dv), F32)],
        compiler_params=_params("arbitrary"),
    )(proj, cos_a, sin_a, *consts, oraw, states, dout)


@jax.custom_vjp
def _ret_core(proj):
    return _ret_fwd_call(proj)[0]


def _ret_core_fwd(proj):
    out, oraw, states = _ret_fwd_call(proj)
    return out, (proj, oraw, states)


def _ret_core_bwd(res, dout):
    return (_ret_bwd_call(*res, dout),)


_ret_core.defvjp(_ret_core_fwd, _ret_core_bwd)


def _retention(hx, p):
    return _linear(_ret_core(_linear(hx, p['ret_w_in'], "ret_in")), p['ret_w_out'], "ret_out")


SQRT_HALF = 2.0 ** -0.5
INV_SQRT_2PI = (2.0 * math.pi) ** -0.5


def _gmlp_front(p_ref, g_ref, b_ref, w):
    x = p_ref[...]
    cdf = 0.5 * (1.0 + lax.erf(x * SQRT_HALF))
    uv = x * cdf
    u, v = uv[:, :w], uv[:, w:]
    vc = v - jnp.mean(v, axis=-1, keepdims=True)
    rstd = lax.rsqrt(jnp.mean(vc * vc, axis=-1, keepdims=True) + LN_EPS)
    vhat = vc * rstd
    return x, cdf, u, vhat, rstd, vhat * g_ref[...] + b_ref[...]


def _gmlp_fwd_call(proj, ln_g, ln_b, ws, bs):
    s, w2 = proj.shape
    w = w2 // 2
    c, G = GMLP_CHUNK, GMLP_GROUPS
    gw = w // G

    def body(p_ref, g_ref, b_ref, ws_ref, bs_ref, o_ref):
        _, _, u, _, _, vn = _gmlp_front(p_ref, g_ref, b_ref, w)
        for gi in range(G):
            cols = slice(gi * gw, (gi + 1) * gw)
            vs = jnp.dot(ws_ref[gi].astype(MXU_DTYPE), vn[:, cols].astype(MXU_DTYPE),
                         preferred_element_type=F32) + bs_ref[gi]
            o_ref[:, cols] = u[:, cols] * vs

    whole = lambda a: pl.BlockSpec(a.shape, lambda n: (0,) * a.ndim)
    args = (_vec(ln_g), _vec(ln_b), ws, bs)
    return _pcall(
        body, name="gmlp_fwd", out_shape=jax.ShapeDtypeStruct((s, w), F32), grid=(s // c,),
        in_specs=[pl.BlockSpec((c, w2), lambda n: (n, 0))] + [whole(a) for a in args],
        out_specs=pl.BlockSpec((c, w), lambda n: (n, 0)),
        compiler_params=_params("parallel"),
    )(proj, *args)


def _gmlp_bwd_call(proj, ln_g, ln_b, ws, bs, dout):
    s, w2 = proj.shape
    w = w2 // 2
    c, G = GMLP_CHUNK, GMLP_GROUPS
    gw = w // G

    def body(p_ref, g_ref, b_ref, ws_ref, bs_ref, do_ref, dp_ref, dws_ref, dbs_ref, dgb_ref):
        @pl.when(pl.program_id(0) == 0)
        def _():
            dws_ref[...] = jnp.zeros_like(dws_ref)
            dbs_ref[...] = jnp.zeros_like(dbs_ref)
            dgb_ref[...] = jnp.zeros_like(dgb_ref)

        x, cdf, u, vhat, rstd, vn = _gmlp_front(p_ref, g_ref, b_ref, w)
        dout = do_ref[...]
        du_parts, dvn_parts = [], []
        for gi in range(G):
            cols = slice(gi * gw, (gi + 1) * gw)
            wsg = ws_ref[gi].astype(MXU_DTYPE)
            vng = vn[:, cols].astype(MXU_DTYPE)
            vs = jnp.dot(wsg, vng, preferred_element_type=F32) + bs_ref[gi]
            du_parts.append(dout[:, cols] * vs)
            dvs = dout[:, cols] * u[:, cols]
            dbs_ref[:, cols] += dvs
            dvsb = dvs.astype(MXU_DTYPE)
            dws_ref[gi] += lax.dot_general(dvsb, vng, _NT, preferred_element_type=F32)
            dvn_parts.append(lax.dot_general(wsg, dvsb, _TN, preferred_element_type=F32))
        dvn = jnp.concatenate(dvn_parts, axis=1)
        dgb_ref[0] += _fold8(dvn * vhat)
        dgb_ref[1] += _fold8(dvn)
        dvh = dvn * g_ref[...]
        dv = rstd * (dvh - jnp.mean(dvh, axis=-1, keepdims=True)
                     - vhat * jnp.mean(dvh * vhat, axis=-1, keepdims=True))
        duv = jnp.concatenate(du_parts + [dv], axis=1)
        dp_ref[...] = duv * (cdf + x * (jnp.exp(-0.5 * x * x) * INV_SQRT_2PI))

    whole = lambda a: pl.BlockSpec(a.shape, lambda n: (0,) * a.ndim)
    args = (_vec(ln_g), _vec(ln_b), ws, bs)
    acc = lambda *shape: pl.BlockSpec(shape, lambda n: (0,) * len(shape))
    return _pcall(
        body, name="gmlp_bwd",
        out_shape=(jax.ShapeDtypeStruct((s, w2), F32), jax.ShapeDtypeStruct((G, c, c), F32),
                   jax.ShapeDtypeStruct((c, w), F32), jax.ShapeDtypeStruct((2, SUBLANES, w), F32)),
        grid=(s // c,),
        in_specs=[pl.BlockSpec((c, w2), lambda n: (n, 0))] + [whole(a) for a in args]
                 + [pl.BlockSpec((c, w), lambda n: (n, 0))],
        out_specs=(pl.BlockSpec((c, w2), lambda n: (n, 0)), acc(G, c, c), acc(c, w), acc(2, SUBLANES, w)),
        compiler_params=_params("arbitrary"),
    )(proj, *args, dout)


def _gmlp_mask(c):
    return jnp.tril(jnp.ones((c, c), dtype=bool))


@jax.custom_vjp
def _gmlp_core(proj, ln_g, ln_b, w_s, b_s):
    ws = jnp.where(_gmlp_mask(GMLP_CHUNK), w_s, 0.0)
    return _gmlp_fwd_call(proj, ln_g, ln_b, ws, b_s[..., None])


def _gmlp_core_fwd(proj, ln_g, ln_b, w_s, b_s):
    return _gmlp_core(proj, ln_g, ln_b, w_s, b_s), (proj, ln_g, ln_b, w_s, b_s)


def _gmlp_core_bwd(res, dout):
    proj, ln_g, ln_b, w_s, b_s = res
    mask = _gmlp_mask(GMLP_CHUNK)
    dproj, dws, dbs, dgb = _gmlp_bwd_call(proj, ln_g, ln_b, jnp.where(mask, w_s, 0.0), b_s[..., None], dout)
    dgb = jnp.sum(dgb, axis=1)
    c = GMLP_CHUNK
    db_s = jnp.sum(dbs.reshape(c, GMLP_GROUPS, -1), axis=-1).T
    return dproj, dgb[0], dgb[1], jnp.where(mask, dws, 0.0), db_s


_gmlp_core.defvjp(_gmlp_core_fwd, _gmlp_core_bwd)


def _chunked_gmlp(hx, p):
    core = _gmlp_core(_linear(hx, p['gmlp_w_in'], "gmlp_in"), p['gmlp_ln_g'], p['gmlp_ln_b'],
                      p['gmlp_w_s'], p['gmlp_b_s'])
    return _linear(core, p['gmlp_w_out'], "gmlp_out")


def _stick_breaking(hx, p):
    H = SB_HEADS
    s, d = hx.shape
    dh = d // H
    qkv = _linear(hx, p['sb_w_in'], "sb_in")
    q, k, v = (qkv[:, j * d:(j + 1) * d].reshape(s, H, dh).transpose(1, 0, 2) for j in range(3))
    o = _sb_core(q, k, v)
    return _linear(o.transpose(1, 0, 2).reshape(s, d), p['sb_w_out'], "sb_out")


MIXERS = (_gated_deltanet, _retention, _chunked_gmlp, _stick_breaking)


def _trunk_grad(x, mods, p, target):
    d = x.shape[-1]
    saved = []
    for i in range(DEPTH):
        sh1, sc1, g1, sh2, sc2, g2 = (mods[i, j * d:(j + 1) * d] for j in range(6))
        h1 = _modulate(x, sc1, sh1, F32, "mod_a%d" % i)
        y1, mixer_vjp = jax.vjp(MIXERS[i], h1, {n: p[n] for n in MIXER_PARAMS[i]})
        x1 = _resid_ln(x, y1, g1, p['ln_g'][i, 0], p['ln_b'][i, 0], "ln_a%d" % i)
        h2 = _modulate(x1, sc2, sh2, MXU_DTYPE, "mod_b%d" % i)
        gu = _mm(h2, p['ffn_up'][i], 'nn', "ffn_up%d_fwd" % i)
        act = _ffn_gate(gu, p['ffn_conv_w'][i], p['ffn_conv_b'][i], "ffn_gate%d" % i)
        y2 = _mm(act, p['ffn_down'][i], 'nn', "ffn_down%d_fwd" % i)
        x2 = _resid_ln(x1, y2, g2, p['ln_g'][i, 1], p['ln_b'][i, 1], "ln_b%d" % i)
        saved.append((x, y1, mixer_vjp, x1, h2, gu, act, y2))
        x = x2
    loss, dx = _loss_head(x, target, "loss_head")

    dp = {n: None for n in p}
    d_ln_g, d_ln_b, d_up, d_down, d_cw, d_cb, dmods = [], [], [], [], [], [], []
    for i in reversed(range(DEPTH)):
        x0, y1, mixer_vjp, x1, h2, gu, act, y2 = saved[i]
        sh1, sc1, g1, sh2, sc2, g2 = (mods[i, j * d:(j + 1) * d] for j in range(6))
        dxa, dy2, dgam2, dbet2, dg2 = _resid_ln_bwd(x1, y2, g2, p['ln_g'][i, 1], dx, "ln_b%d_bwd" % i)
        dact = _mm(dy2, p['ffn_down'][i], 'nt', "ffn_down%d_dx" % i)
        d_down.append(_mm(act, dy2, 'tn', "ffn_down%d_dw" % i))
        dgate, dupp, dcw, dcb = _ffn_gate_bwd(dact, gu, p['ffn_conv_w'][i], p['ffn_conv_b'][i],
                                              "ffn_gate%d_bwd" % i)
        dgu = jnp.concatenate([dgate, dupp], axis=1)
        dh2 = _mm(dgu, p['ffn_up'][i], 'nt', "ffn_up%d_dx" % i)
        d_up.append(_mm(h2, dgu, 'tn', "ffn_up%d_dw" % i))
        dx1, dsc2, dsh2 = _modulate_bwd(dxa, dh2, x1, sc2, "mod_b%d_bwd" % i)
        dxa, dy1, dgam1, dbet1, dg1 = _resid_ln_bwd(x0, y1, g1, p['ln_g'][i, 0], dx1, "ln_a%d_bwd" % i)
        dh1, dmix = mixer_vjp(dy1)
        dp.update(dmix)
        dx, dsc1, dsh1 = _modulate_bwd(dxa, dh1, x0, sc1, "mod_a%d_bwd" % i)
        d_ln_g.append(jnp.stack([dgam1, dgam2]))
        d_ln_b.append(jnp.stack([dbet1, dbet2]))
        d_cw.append(dcw)
        d_cb.append(dcb)
        dmods.append(jnp.concatenate([dsh1, dsc1, dg1, dsh2, dsc2, dg2]))
    for n, parts in (('ln_g', d_ln_g), ('ln_b', d_ln_b), ('ffn_up', d_up), ('ffn_down', d_down),
                     ('ffn_conv_w', d_cw), ('ffn_conv_b', d_cb)):
        dp[n] = jnp.stack(parts[::-1])
    return loss, dx, jnp.stack(dmods[::-1]), dp


def _join(blocks, axis):
    return jnp.concatenate([blocks[d] for d in range(N_DEV)], axis=axis)


def _split(whole, axis):
    n = whole.shape[axis] // N_DEV
    return jnp.stack([lax.slice_in_dim(whole, d * n, (d + 1) * n, axis=axis) for d in range(N_DEV)])


def _pad8(a):
    pad = (-a.shape[0]) % 8
    return jnp.pad(a, ((0, pad), (0, 0))) if pad else a


def _pack_big_grads(full_grads, axes):
    per_dev = jnp.concatenate([_split(g, ax).reshape(N_DEV, -1) for g, ax in zip(full_grads, axes)], axis=1)
    pad = (-per_dev.shape[1]) % (BIG_ROW_ALIGN * LANES)
    if pad:
        per_dev = jnp.pad(per_dev, ((0, 0), (0, pad)))
    return per_dev.reshape(N_DEV, -1, LANES)


def kernel(x, c, cond_w, cond_b, ada_w, ada_b, ln_g, ln_b, ffn_up, ffn_conv_w, ffn_conv_b, ffn_down, gdn_w_in, gdn_conv_w, gdn_a_log, gdn_dt_bias, gdn_norm_w, gdn_w_out, ret_w_in, ret_w_out, gmlp_w_in, gmlp_ln_g, gmlp_ln_b, gmlp_w_s, gmlp_b_s, gmlp_w_out, sb_w_in, sb_w_out, loss_target, m_cond_w, m_cond_b, m_ada_w, m_ada_b, m_ln_g, m_ln_b, m_ffn_up, m_ffn_conv_w, m_ffn_conv_b, m_ffn_down, m_gdn_w_in, m_gdn_conv_w, m_gdn_a_log, m_gdn_dt_bias, m_gdn_norm_w, m_gdn_w_out, m_ret_w_in, m_ret_w_out, m_gmlp_w_in, m_gmlp_ln_g, m_gmlp_ln_b, m_gmlp_w_s, m_gmlp_b_s, m_gmlp_w_out, m_sb_w_in, m_sb_w_out, v_cond_w, v_cond_b, v_ada_w, v_ada_b, v_ln_g, v_ln_b, v_ffn_up, v_ffn_conv_w, v_ffn_conv_b, v_ffn_down, v_gdn_w_in, v_gdn_conv_w, v_gdn_a_log, v_gdn_dt_bias, v_gdn_norm_w, v_gdn_w_out, v_ret_w_in, v_ret_w_out, v_gmlp_w_in, v_gmlp_ln_g, v_gmlp_ln_b, v_gmlp_w_s, v_gmlp_b_s, v_gmlp_w_out, v_sb_w_in, v_sb_w_out):
    w = dict(cond_w=cond_w, cond_b=cond_b, ada_w=ada_w, ada_b=ada_b, ln_g=ln_g, ln_b=ln_b, ffn_up=ffn_up,
             ffn_conv_w=ffn_conv_w, ffn_conv_b=ffn_conv_b, ffn_down=ffn_down, gdn_w_in=gdn_w_in,
             gdn_conv_w=gdn_conv_w, gdn_a_log=gdn_a_log, gdn_dt_bias=gdn_dt_bias, gdn_norm_w=gdn_norm_w,
             gdn_w_out=gdn_w_out, ret_w_in=ret_w_in, ret_w_out=ret_w_out, gmlp_w_in=gmlp_w_in,
             gmlp_ln_g=gmlp_ln_g, gmlp_ln_b=gmlp_ln_b, gmlp_w_s=gmlp_w_s, gmlp_b_s=gmlp_b_s,
             gmlp_w_out=gmlp_w_out, sb_w_in=sb_w_in, sb_w_out=sb_w_out)
    mom = dict(cond_w=m_cond_w, cond_b=m_cond_b, ada_w=m_ada_w, ada_b=m_ada_b, ln_g=m_ln_g, ln_b=m_ln_b,
               ffn_up=m_ffn_up, ffn_conv_w=m_ffn_conv_w, ffn_conv_b=m_ffn_conv_b, ffn_down=m_ffn_down,
               gdn_w_in=m_gdn_w_in, gdn_conv_w=m_gdn_conv_w, gdn_a_log=m_gdn_a_log, gdn_dt_bias=m_gdn_dt_bias,
               gdn_norm_w=m_gdn_norm_w, gdn_w_out=m_gdn_w_out, ret_w_in=m_ret_w_in, ret_w_out=m_ret_w_out,
               gmlp_w_in=m_gmlp_w_in, gmlp_ln_g=m_gmlp_ln_g, gmlp_ln_b=m_gmlp_ln_b, gmlp_w_s=m_gmlp_w_s,
               gmlp_b_s=m_gmlp_b_s, gmlp_w_out=m_gmlp_w_out, sb_w_in=m_sb_w_in, sb_w_out=m_sb_w_out)
    var = dict(cond_w=v_cond_w, cond_b=v_cond_b, ada_w=v_ada_w, ada_b=v_ada_b, ln_g=v_ln_g, ln_b=v_ln_b,
               ffn_up=v_ffn_up, ffn_conv_w=v_ffn_conv_w, ffn_conv_b=v_ffn_conv_b, ffn_down=v_ffn_down,
               gdn_w_in=v_gdn_w_in, gdn_conv_w=v_gdn_conv_w, gdn_a_log=v_gdn_a_log, gdn_dt_bias=v_gdn_dt_bias,
               gdn_norm_w=v_gdn_norm_w, gdn_w_out=v_gdn_w_out, ret_w_in=v_ret_w_in, ret_w_out=v_ret_w_out,
               gmlp_w_in=v_gmlp_w_in, gmlp_ln_g=v_gmlp_ln_g, gmlp_ln_b=v_gmlp_ln_b, gmlp_w_s=v_gmlp_w_s,
               gmlp_b_s=v_gmlp_b_s, gmlp_w_out=v_gmlp_w_out, sb_w_in=v_sb_w_in, sb_w_out=v_sb_w_out)

    me = _my_id()
    x = x[0]
    target = loss_target[0]
    d = x.shape[-1]
    dsh = d // N_DEV
    msh = ada_w.shape[-1]

    c_all = _exchange(_pad8(c), False, "gather_c")[:, 0, :]
    c_mine = lax.dynamic_slice_in_dim(c_all, me * dsh, dsh, axis=1)
    pre_part = _mm(c_mine, cond_w, 'nn', "cond_fwd")
    pre = jnp.sum(_exchange(pre_part, False, "gather_pre"), axis=0) + cond_b
    e_all = jax.nn.silu(pre)
    mod_part = jnp.concatenate([_mm(e_all, ada_w[i], 'nn', "ada_fwd%d" % i) for i in range(DEPTH)], axis=0)
    mod_all = _exchange(mod_part, False, "gather_mod")
    mod_all = mod_all.reshape(N_DEV, DEPTH, N_DEV, msh)
    mods = lax.dynamic_index_in_dim(mod_all, me, axis=2, keepdims=False)
    mods = mods.transpose(1, 0, 2).reshape(DEPTH, N_DEV * msh) + ada_b

    big_names = list(BIG)
    packed = _pack_rows([w[n] for n in big_names], BF16, BIG_ROW_ALIGN)
    gathered = _gather_two_level(packed, "gather_weights")
    blocks = _unpack_rows(gathered, [w[n].shape for n in big_names])
    p = {n: _join(b, BIG[n]) for n, b in zip(big_names, blocks)}
    for n in big_names:
        if not n.startswith('ffn_'):
            p[n] = p[n].astype(F32)
    sm_names = list(SMALL_SHARDED)
    sm_packed = _pack_rows([w[n] for n in sm_names], F32)
    sm_blocks = _unpack_rows(_exchange(sm_packed, False, "gather_small"), [w[n].shape for n in sm_names])
    for n, b in zip(sm_names, sm_blocks):
        p[n] = _join(b, SMALL_SHARDED[n])
    for n in SMALL_REPL:
        p[n] = w[n]
    n_qkvz = 4 * d
    p['gdn_w_qkvz'] = p['gdn_w_in'][:, :n_qkvz]
    p['gdn_w_ab'] = jnp.pad(p['gdn_w_in'][:, n_qkvz:], ((0, 0), (0, LANES - 2 * GDN_HEADS)))
    del p['gdn_w_in']

    loss_local, dx, dmods, dp = _trunk_grad(x, mods, p, target)
    dp['gdn_w_in'] = jnp.concatenate([dp.pop('gdn_w_qkvz'), dp.pop('gdn_w_ab')[:, :2 * GDN_HEADS]], axis=1)

    dmod_all = _exchange(dmods.reshape(-1, d), False, "gather_dmod").reshape(N_DEV, DEPTH, 6 * d)
    grads = {'ada_b': jnp.sum(dmod_all, axis=0)}
    dm_mine = lax.dynamic_slice_in_dim(dmod_all, me * msh, msh, axis=2)
    grads['ada_w'] = jnp.stack([_mm_outer(e_all, dm_mine[:, i], "ada_dw%d" % i) for i in range(DEPTH)])
    de_part = _mm(dm_mine[:, 0], ada_w[0], 'nt', "ada_de0")
    for i in range(1, DEPTH):
        de_part = de_part + _mm(dm_mine[:, i], ada_w[i], 'nt', "ada_de%d" % i)
    de_all = jnp.sum(_exchange(de_part, False, "gather_de"), axis=0)
    sig = jax.nn.sigmoid(pre)
    dpre = de_all * (sig * (1.0 + pre * (1.0 - sig)))
    grads['cond_b'] = jnp.sum(dpre, axis=0)
    grads['cond_w'] = _mm_outer(c_mine, dpre, "cond_dw")

    small_names = sm_names + SMALL_REPL
    small_packed = _pack_rows([loss_local.reshape(1)] + [dp[n] for n in small_names], F32)
    small_sum = _sum_slots(_exchange(small_packed, False, "gather_small_grads"), "sum_small_grads")
    small = _unpack_rows(small_sum, [(1,)] + [dp[n].shape for n in small_names])
    loss = small[0][0]
    for n, g in zip(small_names, small[1:]):
        if n in SMALL_SHARDED:
            ax = SMALL_SHARDED[n]
            g = lax.dynamic_slice_in_dim(g, me * w[n].shape[ax], w[n].shape[ax], axis=ax)
        grads[n] = g

    send = _pack_big_grads([dp[n] for n in big_names], [BIG[n] for n in big_names])
    shapes = [w[n].shape for n in big_names]
    outs = _adamw(_reduce_to_owner(send), *[_pack_rows([t[n] for n in big_names], F32, BIG_ROW_ALIGN) for t in (w, mom, var)],
                  "adamw_big")
    g_b, d_b, m_b, v_b = (_unpack_rows(o, shapes) for o in outs)
    delta, new_m, new_v = {}, {}, {}
    for j, n in enumerate(big_names):
        grads[n], delta[n], new_m[n], new_v[n] = g_b[j], d_b[j], m_b[j], v_b[j]

    rest = [n for n in WEIGHTS if n not in BIG]
    shapes = [w[n].shape for n in rest]
    outs = _adamw([(_pack_rows([grads[n] for n in rest], F32, BIG_ROW_ALIGN)[None], 0)],
                  *[_pack_rows([t[n] for n in rest], F32, BIG_ROW_ALIGN) for t in (w, mom, var)], "adamw_rest")
    _, d_r, m_r, v_r = (_unpack_rows(o, shapes) for o in outs)
    for j, n in enumerate(rest):
        delta[n], new_m[n], new_v[n] = d_r[j], m_r[j], v_r[j]

    return (loss, dx[None], *[grads[n] for n in WEIGHTS], *[delta[n] for n in WEIGHTS],
            *[new_m[n] for n in WEIGHTS], *[new_v[n] for n in WEIGHTS])
```

```python
import functools
import math

import jax
import jax.numpy as jnp
from jax import lax
from jax.experimental import pallas as pl
from jax.experimental.pallas import tpu as pltpu

F32 = jnp.float32
BF16 = jnp.bfloat16
MXU_DTYPE = jnp.bfloat16
MESH = pl.DeviceIdType.MESH
N_DEV = 8
LANES = 128
SUBLANES = 8
VMEM_LIMIT = 48 * 1024 * 1024

DEPTH = 4
LN_EPS = 1e-5
DN_ALPHA = (2.0 * DEPTH) ** 0.25
GDN_HEADS, GDN_CONV, GDN_CHUNK = 8, 4, 64
RET_HEADS, RET_CHUNK, RET_ROPE_BASE = 4, 128, 10000.0
GMLP_CHUNK, GMLP_GROUPS = 128, 8
SB_HEADS = 16
ADAM_LR, ADAM_B1, ADAM_B2, ADAM_EPS, ADAM_WD, ADAM_STEP = 0.001, 0.9, 0.999, 1e-08, 0.01, 10

WEIGHTS = ['cond_w', 'cond_b', 'ada_w', 'ada_b', 'ln_g', 'ln_b', 'ffn_up', 'ffn_conv_w', 'ffn_conv_b', 'ffn_down',
           'gdn_w_in', 'gdn_conv_w', 'gdn_a_log', 'gdn_dt_bias', 'gdn_norm_w', 'gdn_w_out', 'ret_w_in', 'ret_w_out',
           'gmlp_w_in', 'gmlp_ln_g', 'gmlp_ln_b', 'gmlp_w_s', 'gmlp_b_s', 'gmlp_w_out', 'sb_w_in', 'sb_w_out']
BIG = {'ffn_up': 2, 'ffn_down': 1, 'gdn_w_in': 1, 'gdn_w_out': 0, 'ret_w_in': 1, 'ret_w_out': 0,
       'gmlp_w_in': 1, 'gmlp_w_out': 0, 'sb_w_in': 1, 'sb_w_out': 0}
SMALL_SHARDED = {'ln_g': 2, 'ln_b': 2, 'ffn_conv_w': 2, 'gdn_conv_w': 1}
SMALL_REPL = ['ffn_conv_b', 'gdn_a_log', 'gdn_dt_bias', 'gdn_norm_w', 'gmlp_ln_g', 'gmlp_ln_b', 'gmlp_w_s', 'gmlp_b_s']
MIXER_PARAMS = (('gdn_w_qkvz', 'gdn_w_ab', 'gdn_conv_w', 'gdn_a_log', 'gdn_dt_bias', 'gdn_norm_w', 'gdn_w_out'),
                ('ret_w_in', 'ret_w_out'),
                ('gmlp_w_in', 'gmlp_ln_g', 'gmlp_ln_b', 'gmlp_w_s', 'gmlp_b_s', 'gmlp_w_out'),
                ('sb_w_in', 'sb_w_out'))


def _pcall(body, **kw):
    return pl.pallas_call(body, **kw)


def _params(*semantics):
    return pltpu.CompilerParams(dimension_semantics=semantics, vmem_limit_bytes=VMEM_LIMIT)


def _my_id():
    return 4 * lax.axis_index("x") + 2 * lax.axis_index("y") + lax.axis_index("c")


def _pick(dim, prefs):
    for p in prefs:
        if dim % p == 0:
            return p
    return dim


def _exchange(src, scatter, name):
    blk = src.shape[1:] if scatter else src.shape
    out_shape = jax.ShapeDtypeStruct((N_DEV,) + tuple(blk), src.dtype)

    def body(src_ref, out_ref, send_sems, recv_sems, local_sem):
        x, y, c = lax.axis_index("x"), lax.axis_index("y"), lax.axis_index("c")
        me = 4 * x + 2 * y + c
        mine = pltpu.make_async_copy(src_ref.at[me] if scatter else src_ref, out_ref.at[me], local_sem)
        mine.start()
        copies = []
        for k in range(1, N_DEV):
            px = 1 - x if (k >> 2) & 1 else x
            py = 1 - y if (k >> 1) & 1 else y
            pc = 1 - c if k & 1 else c
            peer = 4 * px + 2 * py + pc
            cp = pltpu.make_async_remote_copy(
                src_ref=src_ref.at[peer] if scatter else src_ref,
                dst_ref=out_ref.at[me],
                send_sem=send_sems.at[k - 1], recv_sem=recv_sems.at[k - 1],
                device_id=(px, py, pc), device_id_type=MESH)
            cp.start()
            copies.append(cp)
        for cp in copies:
            cp.wait()
        mine.wait()

    return _pcall(
        body, name=name, out_shape=out_shape,
        in_specs=[pl.BlockSpec(memory_space=pl.ANY)],
        out_specs=pl.BlockSpec(memory_space=pl.ANY),
        scratch_shapes=[pltpu.SemaphoreType.DMA((N_DEV - 1,)), pltpu.SemaphoreType.DMA((N_DEV - 1,)),
                        pltpu.SemaphoreType.DMA(())],
    )(src)


def _flip(x, y, c, k):
    return (1 - x if (k >> 2) & 1 else x, 1 - y if (k >> 1) & 1 else y, 1 - c if k & 1 else c)


def _dev_id(p):
    return 4 * p[0] + 2 * p[1] + p[2]


OTHER_CHIPS = (4, 2, 6)


def _gather_two_level(src, name):
    out_shape = jax.ShapeDtypeStruct((N_DEV,) + tuple(src.shape), src.dtype)

    def body(x_ref, out_ref, send_sems, recv_sems, local_sem):
        x, y, c = lax.axis_index("x"), lax.axis_index("y"), lax.axis_index("c")
        me, sibling = (x, y, c), (x, y, 1 - c)
        chips = [_flip(x, y, c, k) for k in OTHER_CHIPS]

        def copy(k, block, to, from_src=False):
            slot = out_ref.at[_dev_id(block)]
            return pltpu.make_async_remote_copy(
                src_ref=x_ref if from_src else slot, dst_ref=slot,
                send_sem=send_sems.at[k], recv_sem=recv_sems.at[k], device_id=to, device_id_type=MESH)

        mine = pltpu.make_async_copy(x_ref, out_ref.at[_dev_id(me)], local_sem)
        mine.start()
        first = [copy(0, me, sibling, True)] + [copy(1 + j, me, chip, True) for j, chip in enumerate(chips)]
        for cp in first:
            cp.start()
        passed = [copy(4 + j, chip, sibling) for j, chip in enumerate(chips)]
        for j, chip in enumerate(chips):
            copy(1 + j, chip, me).wait_recv()
            passed[j].start()
        copy(0, sibling, me).wait_recv()
        for j, chip in enumerate(chips):
            copy(4 + j, (chip[0], chip[1], 1 - c), me).wait_recv()
        for cp in first + passed:
            cp.wait_send()
        mine.wait()

    return _pcall(
        body, name=name, out_shape=out_shape,
        in_specs=[pl.BlockSpec(memory_space=pl.ANY)],
        out_specs=pl.BlockSpec(memory_space=pl.ANY),
        scratch_shapes=[pltpu.SemaphoreType.DMA((N_DEV - 1,)), pltpu.SemaphoreType.DMA((N_DEV - 1,)),
                        pltpu.SemaphoreType.DMA(())],
    )(src)


def _send_slots(src, plan, n_out, name):
    out_shape = jax.ShapeDtypeStruct((n_out,) + tuple(src.shape[1:]), src.dtype)

    def body(src_ref, out_ref, send_sems, recv_sems):
        x, y, c = lax.axis_index("x"), lax.axis_index("y"), lax.axis_index("c")
        copies = []
        for e, (k, src_slot, dst_slot) in enumerate(plan):
            cp = pltpu.make_async_remote_copy(
                src_ref=src_ref.at[src_slot(x, y, c)], dst_ref=out_ref.at[dst_slot],
                send_sem=send_sems.at[e], recv_sem=recv_sems.at[e],
                device_id=_flip(x, y, c, k), device_id_type=MESH)
            cp.start()
            copies.append(cp)
        for cp in copies:
            cp.wait()

    return _pcall(
        body, name=name, out_shape=out_shape,
        in_specs=[pl.BlockSpec(memory_space=pl.ANY)],
        out_specs=pl.BlockSpec(memory_space=pl.ANY),
        scratch_shapes=[pltpu.SemaphoreType.DMA((len(plan),)), pltpu.SemaphoreType.DMA((len(plan),))],
    )(src)


def _reduce_to_owner(send):
    x, y, c = lax.axis_index("x"), lax.axis_index("y"), lax.axis_index("c")
    plan_a = [(1, lambda x, y, c: _dev_id((x, y, 1 - c)), 0)]
    plan_a += [(1, functools.partial(lambda k, x, y, c: _dev_id(_flip(x, y, c, k | 1)), k), 1 + j)
               for j, k in enumerate(OTHER_CHIPS)]
    from_sibling = _send_slots(send, plan_a, 1 + len(OTHER_CHIPS), "reduce_d2d")
    mine = jnp.stack([lax.dynamic_index_in_dim(send, _dev_id(_flip(x, y, c, k)), 0, keepdims=False)
                      for k in OTHER_CHIPS])
    rows = mine.shape[1]
    pair = _add_rows(mine.reshape(-1, LANES), from_sibling[1:].reshape(-1, LANES), "reduce_pair_sum")
    plan_c = [(k, functools.partial(lambda j, x, y, c: j, j), j) for j, k in enumerate(OTHER_CHIPS)]
    from_chips = _send_slots(pair.reshape(len(OTHER_CHIPS), rows, LANES), plan_c, len(OTHER_CHIPS), "reduce_ici")
    own = lax.dynamic_index_in_dim(send, _dev_id((x, y, c)), 0, keepdims=True)
    return [(own, 0), (from_sibling, 0)] + [(from_chips, j) for j in range(len(OTHER_CHIPS))]


ROW_ALIGN = 16
BIG_ROW_ALIGN = 512


def _pack_rows(parts, dtype, row_align=ROW_ALIGN):
    flat = jnp.concatenate([p.reshape(-1).astype(dtype) for p in parts])
    n = flat.shape[0]
    pad = (-n) % (row_align * LANES)
    if pad:
        flat = jnp.concatenate([flat, jnp.zeros((pad,), dtype)])
    return flat.reshape(-1, LANES)


def _unpack_rows(packed, shapes):
    lead = packed.shape[:-2]
    flat = packed.reshape(lead + (-1,))
    out, off = [], 0
    for s in shapes:
        n = math.prod(s)
        out.append(flat[..., off:off + n].reshape(lead + tuple(s)))
        off += n
    return out


def _mm(a, b, dims, name, exact=False):
    if dims == 'nn':
        (m, k), n = a.shape, b.shape[1]
    elif dims == 'nt':
        (m, k), n = a.shape, b.shape[0]
    else:
        (k, m), n = a.shape, b.shape[1]
    tm = _pick(m, (1024, 512, 256, 128))
    tn = _pick(n, (512, 256, 128))
    tk = k if k <= 2816 else _pick(k, (1024, 512, 256, 128))
    nk = k // tk
    if dims == 'nn':
        a_spec = pl.BlockSpec((tm, tk), lambda i, j, kk: (i, kk))
        b_spec = pl.BlockSpec((tk, tn), lambda i, j, kk: (kk, j))
        dn = (((1,), (0,)), ((), ()))
    elif dims == 'nt':
        a_spec = pl.BlockSpec((tm, tk), lambda i, j, kk: (i, kk))
        b_spec = pl.BlockSpec((tn, tk), lambda i, j, kk: (j, kk))
        dn = (((1,), (1,)), ((), ()))
    else:
        a_spec = pl.BlockSpec((tk, tm), lambda i, j, kk: (kk, i))
        b_spec = pl.BlockSpec((tk, tn), lambda i, j, kk: (kk, j))
        dn = (((0,), (0,)), ((), ()))

    def product(a_ref, b_ref):
        if exact:
            return lax.dot_general(a_ref[...], b_ref[...], dn, precision=lax.Precision.HIGHEST,
                                   preferred_element_type=F32)
        return lax.dot_general(a_ref[...].astype(MXU_DTYPE), b_ref[...].astype(MXU_DTYPE), dn,
                               preferred_element_type=F32)

    def body(a_ref, b_ref, o_ref, *acc):
        if nk == 1:
            o_ref[...] = product(a_ref, b_ref)
            return
        acc_ref, = acc
        kk = pl.program_id(2)

        @pl.when(kk == 0)
        def _():
            acc_ref[...] = jnp.zeros_like(acc_ref)

        acc_ref[...] += product(a_ref, b_ref)

        @pl.when(kk == nk - 1)
        def _():
            o_ref[...] = acc_ref[...]

    return _pcall(
        body, name=name, out_shape=jax.ShapeDtypeStruct((m, n), F32),
        grid=(m // tm, n // tn, nk),
        in_specs=[a_spec, b_spec],
        out_specs=pl.BlockSpec((tm, tn), lambda i, j, kk: (i, j)),
        scratch_shapes=[pltpu.VMEM((tm, tn), F32)] if nk > 1 else [],
        compiler_params=_params("parallel", "parallel", "arbitrary"),
    )(a, b)


def _mm_outer(a, b, name):
    pad = LANES - a.shape[0]
    return _mm(jnp.pad(a.T, ((0, 0), (0, pad))), jnp.pad(b, ((0, pad), (0, 0))), 'nn', name, exact=True)


@functools.partial(jax.custom_vjp, nondiff_argnums=(2,))
def _linear(a, w, name):
    return _mm(a, w, 'nn', name + "_fwd")


def _linear_fwd(a, w, name):
    return _mm(a, w, 'nn', name + "_fwd"), (a, w)


def _linear_bwd(name, res, dy):
    a, w = res
    return _mm(dy, w, 'nt', name + "_dx"), _mm(a, dy, 'tn', name + "_dw")


_linear.defvjp(_linear_fwd, _linear_bwd)


SB_BK = 256
SB_STRIP = 32


def _sb_tiles(s):
    tq = _pick(s, (512, 256, 128))
    bk = min(SB_BK, tq)
    return tq, bk, tq // bk


def _sb_valid(t, sr, bk, q0, k0):
    row = lax.broadcasted_iota(jnp.int32, (sr, bk), 0) + (q0 + t * sr)
    col = lax.broadcasted_iota(jnp.int32, (sr, bk), 1) + k0
    return col < row


def _sb_tri(bk, inclusive):
    r = jnp.bitwise_and(lax.broadcasted_iota(jnp.int32, (2 * bk, bk), 0), bk - 1)
    c = lax.broadcasted_iota(jnp.int32, (2 * bk, bk), 1)
    return (r >= c).astype(BF16) if inclusive else (r > c).astype(BF16)


def _sb_split(ref, n, rows, bk, val):
    hi = val.astype(BF16)
    ref[n, rows, 0:bk] = hi
    ref[n, rows, bk:2 * bk] = (val - hi.astype(F32)).astype(BF16)


LOG2E = 1.0 / math.log(2.0)


def _sb_logits_phase(z_ref, ls_ref, hl_ref, l0_ref, n, tq, bk, sr, q0, k0, masked):
    for t in range(tq // sr):
        rows = slice(t * sr, (t + 1) * sr)
        z = z_ref[n, rows, :] * LOG2E
        ls = jnp.minimum(z, 0.0) - jnp.log(1.0 + jnp.exp2(-jnp.abs(z))) * LOG2E
        lm = ls - z
        if masked:
            lm = jnp.where(_sb_valid(t, sr, bk, q0, k0), lm, 0.0)
        ls_ref[n, rows, :] = ls
        _sb_split(hl_ref, n, rows, bk, lm)
        l0_ref[n, rows, :] = lm[:, 0:1]


def _sb_fwd_call(q, kt, v):
    h, s, dh = q.shape
    tq, bk, nt = _sb_tiles(s)
    sr = SB_STRIP

    def body(q_ref, kt_ref, v_ref, o_ref, z_ref, ls_ref, hl_ref, f_ref, a_ref, l0_ref, cl_ref, acc_ref):
        i = pl.program_id(1)
        q0 = i * tq
        cl_ref[...] = jnp.zeros_like(cl_ref)
        acc_ref[...] = jnp.zeros_like(acc_ref)
        u_excl = _sb_tri(bk, False)

        def iteration(kb0, masked):
            k0s = [pl.multiple_of(kb0 + (nt - 1 - n) * bk, bk) for n in range(nt)]
            for n in range(nt):
                z_ref[n] = jnp.dot(q_ref[...], kt_ref[:, pl.ds(k0s[n], bk)], preferred_element_type=F32)
            for n in range(nt):
                _sb_logits_phase(z_ref, ls_ref, hl_ref, l0_ref, n, tq, bk, sr, q0, k0s[n], masked)
            for n in range(nt):
                f_ref[n] = jnp.dot(hl_ref[n], u_excl, preferred_element_type=F32)
            for t in range(tq // sr):
                rows = slice(t * sr, (t + 1) * sr)
                c = cl_ref[rows, :]
                for n in range(nt):
                    f = f_ref[n, rows, :]
                    a = jnp.exp2(ls_ref[n, rows, :] + f + c)
                    if masked:
                        a = jnp.where(_sb_valid(t, sr, bk, q0, k0s[n]), a, 0.0)
                    a_ref[n, rows, :] = a.astype(a_ref.dtype)
                    c = c + f[:, 0:1] + l0_ref[n, rows, :]
                cl_ref[rows, :] = c
            for n in range(nt):
                acc_ref[...] += jnp.dot(a_ref[n], v_ref[pl.ds(k0s[n], bk), :], preferred_element_type=F32)

        def below(jj, c):
            iteration((i - 1 - jj) * tq, False)
            return c

        iteration(q0, True)
        lax.fori_loop(0, i, below, 0)
        o_ref[...] = acc_ref[...]

    return _pcall(
        body, name="sb_fwd", out_shape=jax.ShapeDtypeStruct((h, s, dh), F32),
        grid=(h, s // tq),
        in_specs=[pl.BlockSpec((None, tq, dh), lambda hh, i: (hh, i, 0)),
                  pl.BlockSpec((None, dh, s), lambda hh, i: (hh, 0, 0)),
                  pl.BlockSpec((None, s, dh), lambda hh, i: (hh, 0, 0))],
        out_specs=pl.BlockSpec((None, tq, dh), lambda hh, i: (hh, i, 0)),
        scratch_shapes=[pltpu.VMEM((nt, tq, bk), F32), pltpu.VMEM((nt, tq, bk), F32),
                        pltpu.VMEM((nt, tq, 2 * bk), BF16), pltpu.VMEM((nt, tq, bk), F32),
                        pltpu.VMEM((nt, tq, bk), q.dtype), pltpu.VMEM((nt, tq, 1), F32),
                        pltpu.VMEM((tq, 1), F32), pltpu.VMEM((tq, dh), F32)],
        compiler_params=_params("parallel", "arbitrary"),
    )(q, kt, v)


def _sb_bwd_call(q, qt, k, kt, vt, o, do, dot):
    h, s, dh = q.shape
    tq, bk, nt = _sb_tiles(s)
    sr = SB_STRIP

    def body(q_ref, qt_ref, k_ref, kt_ref, vt_ref, o_ref, do_ref, dot_ref, dq_ref, dkt_ref, dvt_ref,
             z_ref, ls_ref, hl_ref, f_ref, a_ref, g_ref, dz_ref, da_ref, l0_ref, dob_ref,
             cl_ref, cg_ref, dl_ref, dqa_ref):
        i = pl.program_id(1)
        q0 = i * tq

        @pl.when(i == 0)
        def _():
            dkt_ref[...] = jnp.zeros_like(dkt_ref)
            dvt_ref[...] = jnp.zeros_like(dvt_ref)

        cl_ref[...] = jnp.zeros_like(cl_ref)
        cg_ref[...] = jnp.zeros_like(cg_ref)
        dqa_ref[...] = jnp.zeros_like(dqa_ref)
        dob = do_ref[...].astype(dob_ref.dtype)
        dob_ref[...] = dob
        dl_ref[...] = jnp.sum(dob.astype(F32) * o_ref[...], axis=1, keepdims=True)
        u_excl = _sb_tri(bk, False)
        u_incl = _sb_tri(bk, True)

        def iteration(kb0, masked):
            k0s = [pl.multiple_of(kb0 + (nt - 1 - n) * bk, bk) for n in range(nt)]
            for n in range(nt):
                z_ref[n] = jnp.dot(q_ref[...], kt_ref[:, pl.ds(k0s[n], bk)], preferred_element_type=F32)
                da_ref[n] = jnp.dot(dob_ref[...], vt_ref[:, pl.ds(k0s[n], bk)], preferred_element_type=F32)
            for n in range(nt):
                _sb_logits_phase(z_ref, ls_ref, hl_ref, l0_ref, n, tq, bk, sr, q0, k0s[n], masked)
            for n in range(nt):
                f_ref[n] = jnp.dot(hl_ref[n], u_excl, preferred_element_type=F32)
            for t in range(tq // sr):
                rows = slice(t * sr, (t + 1) * sr)
                c = cl_ref[rows, :]
                for n in range(nt):
                    f = f_ref[n, rows, :]
                    a = jnp.exp2(ls_ref[n, rows, :] + f + c)
                    if masked:
                        a = jnp.where(_sb_valid(t, sr, bk, q0, k0s[n]), a, 0.0)
                    ab = a.astype(a_ref.dtype)
                    a_ref[n, rows, :] = ab
                    g = da_ref[n, rows, :] * ab.astype(F32)
                    g_ref[n, rows, :] = g
                    _sb_split(hl_ref, n, rows, bk, g)
                    c = c + f[:, 0:1] + l0_ref[n, rows, :]
                cl_ref[rows, :] = c
            for n in range(nt):
                f_ref[n] = jnp.dot(hl_ref[n], u_incl, preferred_element_type=F32)
            for t in range(tq // sr):
                rows = slice(t * sr, (t + 1) * sr)
                cg = cg_ref[rows, :]
                for n in range(nt):
                    sg_tile = f_ref[n, rows, :]
                    p = dl_ref[rows, :] - (sg_tile + cg)
                    g = g_ref[n, rows, :]
                    dz = g - (g + p) * jnp.exp2(ls_ref[n, rows, :])
                    if masked:
                        dz = jnp.where(_sb_valid(t, sr, bk, q0, k0s[n]), dz, 0.0)
                    dz_ref[n, rows, :] = dz.astype(dz_ref.dtype)
                    cg = cg + sg_tile[:, 0:1]
                cg_ref[rows, :] = cg
            for n in range(nt):
                cols = pl.ds(k0s[n], bk)
                dqa_ref[...] += jnp.dot(dz_ref[n], k_ref[cols, :], preferred_element_type=F32)
                dkt_ref[:, cols] += jnp.dot(qt_ref[...], dz_ref[n], preferred_element_type=F32)
                dvt_ref[:, cols] += jnp.dot(dot_ref[...], a_ref[n], preferred_element_type=F32)

        def below(jj, c):
            iteration((i - 1 - jj) * tq, False)
            return c

        iteration(q0, True)
        lax.fori_loop(0, i, below, 0)
        dq_ref[...] = dqa_ref[...]

    blk_q = pl.BlockSpec((None, tq, dh), lambda hh, i: (hh, i, 0))
    blk_qt = pl.BlockSpec((None, dh, tq), lambda hh, i: (hh, 0, i))
    blk_s = pl.BlockSpec((None, s, dh), lambda hh, i: (hh, 0, 0))
    blk_st = pl.BlockSpec((None, dh, s), lambda hh, i: (hh, 0, 0))
    mx = q.dtype
    return _pcall(
        body, name="sb_bwd",
        out_shape=(jax.ShapeDtypeStruct((h, s, dh), F32), jax.ShapeDtypeStruct((h, dh, s), F32),
                   jax.ShapeDtypeStruct((h, dh, s), F32)),
        grid=(h, s // tq),
        in_specs=[blk_q, blk_qt, blk_s, blk_st, blk_st, blk_q, blk_q, blk_qt],
        out_specs=(blk_q, blk_st, blk_st),
        scratch_shapes=[pltpu.VMEM((nt, tq, bk), F32), pltpu.VMEM((nt, tq, bk), F32),
                        pltpu.VMEM((nt, tq, 2 * bk), BF16), pltpu.VMEM((nt, tq, bk), F32),
                        pltpu.VMEM((nt, tq, bk), mx), pltpu.VMEM((nt, tq, bk), F32),
                        pltpu.VMEM((nt, tq, bk), mx), pltpu.VMEM((nt, tq, bk), F32),
                        pltpu.VMEM((nt, tq, 1), F32), pltpu.VMEM((tq, dh), mx),
                        pltpu.VMEM((tq, 1), F32), pltpu.VMEM((tq, 1), F32), pltpu.VMEM((tq, 1), F32),
                        pltpu.VMEM((tq, dh), F32)],
        compiler_params=_params("parallel", "arbitrary"),
    )(q, qt, k, kt, vt, o, do, dot)


def _swap(t):
    return t.transpose(0, 2, 1)


def _sb_scale(dh):
    assert math.log2(dh) % 2 == 0, dh
    return dh ** -0.5


@jax.custom_vjp
def _sb_core(q, k, v):
    return _sb_core_fwd(q, k, v)[0]


def _sb_core_fwd(q, k, v):
    scale = _sb_scale(q.shape[-1])
    qs, kb, vb = (q.astype(MXU_DTYPE) * scale).astype(MXU_DTYPE), k.astype(MXU_DTYPE), v.astype(MXU_DTYPE)
    o = _sb_fwd_call(qs, _swap(kb), vb)
    return o, (qs, kb, vb, o)


def _sb_core_bwd(res, do):
    qs, kb, vb, o = res
    ks = (kb * _sb_scale(kb.shape[-1])).astype(MXU_DTYPE)
    dq, dkt, dvt = _sb_bwd_call(qs, _swap(qs), ks, _swap(kb), _swap(vb), o, do, _swap(do.astype(MXU_DTYPE)))
    return dq, _swap(dkt), _swap(dvt)


_sb_core.defvjp(_sb_core_fwd, _sb_core_bwd)


def _row_block(s):
    return _pick(s, (512, 256, 128, 64, 32, 16, 8))


def _fold8(t):
    r, c = t.shape
    return jnp.sum(t.reshape(r // SUBLANES, SUBLANES, c), axis=0)


def _vec(a):
    return a.reshape(1, -1)


def _modulate(x, sc, sh, out_dtype, name):
    s, d = x.shape
    tr = _row_block(s)

    def body(x_ref, sc_ref, sh_ref, o_ref):
        o_ref[...] = (x_ref[...] * (1.0 + sc_ref[...]) + sh_ref[...]).astype(o_ref.dtype)

    row = pl.BlockSpec((tr, d), lambda i: (i, 0))
    vec = pl.BlockSpec((1, d), lambda i: (0, 0))
    return _pcall(body, name=name, out_shape=jax.ShapeDtypeStruct((s, d), out_dtype), grid=(s // tr,),
                  in_specs=[row, vec, vec], out_specs=row, compiler_params=_params("parallel"))(x, _vec(sc), _vec(sh))


def _modulate_bwd(dxa, dh, x, sc, name):
    s, d = x.shape
    tr = _row_block(s)

    def body(dxa_ref, dh_ref, x_ref, sc_ref, dx_ref, acc_ref):
        @pl.when(pl.program_id(0) == 0)
        def _():
            acc_ref[...] = jnp.zeros_like(acc_ref)

        dh = dh_ref[...]
        dx_ref[...] = dxa_ref[...] + dh * (1.0 + sc_ref[...])
        acc_ref[0] += _fold8(dh * x_ref[...])
        acc_ref[1] += _fold8(dh)

    row = pl.BlockSpec((tr, d), lambda i: (i, 0))
    vec = pl.BlockSpec((1, d), lambda i: (0, 0))
    dx, acc = _pcall(
        body, name=name,
        out_shape=(jax.ShapeDtypeStruct((s, d), F32), jax.ShapeDtypeStruct((2, SUBLANES, d), F32)),
        grid=(s // tr,), in_specs=[row, row, row, vec],
        out_specs=(row, pl.BlockSpec((2, SUBLANES, d), lambda i: (0, 0, 0))),
        compiler_params=_params("arbitrary"))(dxa, dh, x, _vec(sc))
    acc = jnp.sum(acc, axis=1)
    return dx, acc[0], acc[1]


def _resid_ln(x, y, g, gamma, beta, name):
    s, d = x.shape
    tr = _row_block(s)

    def body(x_ref, y_ref, g_ref, gam_ref, bet_ref, o_ref):
        u = DN_ALPHA * x_ref[...] + (1.0 + g_ref[...]) * y_ref[...]
        uc = u - jnp.mean(u, axis=-1, keepdims=True)
        var = jnp.mean(uc * uc, axis=-1, keepdims=True)
        o_ref[...] = uc * lax.rsqrt(var + LN_EPS) * gam_ref[...] + bet_ref[...]

    row = pl.BlockSpec((tr, d), lambda i: (i, 0))
    vec = pl.BlockSpec((1, d), lambda i: (0, 0))
    return _pcall(body, name=name, out_shape=jax.ShapeDtypeStruct((s, d), F32), grid=(s // tr,),
                  in_specs=[row, row, vec, vec, vec], out_specs=row,
                  compiler_params=_params("parallel"))(x, y, _vec(g), _vec(gamma), _vec(beta))


def _resid_ln_bwd(x, y, g, gamma, dout, name):
    s, d = x.shape
    tr = _row_block(s)

    def body(x_ref, y_ref, g_ref, gam_ref, do_ref, dxa_ref, dy_ref, acc_ref):
        @pl.when(pl.program_id(0) == 0)
        def _():
            acc_ref[...] = jnp.zeros_like(acc_ref)

        y = y_ref[...]
        gg = 1.0 + g_ref[...]
        u = DN_ALPHA * x_ref[...] + gg * y
        uc = u - jnp.mean(u, axis=-1, keepdims=True)
        rstd = lax.rsqrt(jnp.mean(uc * uc, axis=-1, keepdims=True) + LN_EPS)
        xhat = uc * rstd
        dout = do_ref[...]
        dxh = dout * gam_ref[...]
        du = rstd * (dxh - jnp.mean(dxh, axis=-1, keepdims=True)
                     - xhat * jnp.mean(dxh * xhat, axis=-1, keepdims=True))
        dxa_ref[...] = DN_ALPHA * du
        dy_ref[...] = gg * du
        acc_ref[0] += _fold8(dout * xhat)
        acc_ref[1] += _fold8(dout)
        acc_ref[2] += _fold8(du * y)

    row = pl.BlockSpec((tr, d), lambda i: (i, 0))
    vec = pl.BlockSpec((1, d), lambda i: (0, 0))
    dxa, dy, acc = _pcall(
        body, name=name,
        out_shape=(jax.ShapeDtypeStruct((s, d), F32), jax.ShapeDtypeStruct((s, d), F32),
                   jax.ShapeDtypeStruct((3, SUBLANES, d), F32)),
        grid=(s // tr,), in_specs=[row, row, vec, vec, row],
        out_specs=(row, row, pl.BlockSpec((3, SUBLANES, d), lambda i: (0, 0, 0))),
        compiler_params=_params("arbitrary"))(x, y, _vec(g), _vec(gamma), dout)
    acc = jnp.sum(acc, axis=1)
    return dxa, dy, acc[0], acc[1], acc[2]


def _loss_head(x, target, name):
    s, d = x.shape
    tr = _row_block(s)

    def body(x_ref, t_ref, dx_ref, acc_ref):
        @pl.when(pl.program_id(0) == 0)
        def _():
            acc_ref[...] = jnp.zeros_like(acc_ref)

        e = x_ref[...] - t_ref[...]
        dx_ref[...] = e * (1.0 / d)
        acc_ref[...] += _fold8(e * e)

    row = pl.BlockSpec((tr, d), lambda i: (i, 0))
    dx, acc = _pcall(
        body, name=name,
        out_shape=(jax.ShapeDtypeStruct((s, d), F32), jax.ShapeDtypeStruct((SUBLANES, d), F32)),
        grid=(s // tr,), in_specs=[row, row],
        out_specs=(row, pl.BlockSpec((SUBLANES, d), lambda i: (0, 0))),
        compiler_params=_params("arbitrary"))(x, target)
    return (0.5 / d) * jnp.sum(acc), dx


CONV_STRIPE = 128
CONV_ROWS = 256


def _shift_down(cur, halo, k):
    ext = jnp.concatenate([halo, cur], axis=0)
    return pltpu.roll(ext, k, 0)[SUBLANES:]


def _shift_up(cur, halo, k):
    ext = jnp.concatenate([cur, halo], axis=0)
    n = ext.shape[0]
    return pltpu.roll(ext, n - k, 0)[:n - SUBLANES]


def _gate_chunk(g_ref, r, rc):
    r0 = pl.multiple_of(r * rc, rc)
    cur = g_ref[pl.ds(r0, rc), :]
    hs = pl.multiple_of(jnp.maximum(r0 - SUBLANES, 0), SUBLANES)
    halo = jnp.where(r > 0, g_ref[pl.ds(hs, SUBLANES), :], 0.0)
    return r0, cur, _shift_down(cur, halo, 1), _shift_down(cur, halo, 2)


def _ffn_gate(gu, cw, cb, name):
    s, f2 = gu.shape
    f = f2 // 2
    tc = _pick(f, (CONV_STRIPE,))
    rc = _pick(s, (CONV_ROWS, 128, 64, 32, 16, 8))
    nj = f // tc

    def body(g_ref, u_ref, cw_ref, cb_ref, a_ref):
        w0, w1, w2, b = cw_ref[0:1, :], cw_ref[1:2, :], cw_ref[2:3, :], cb_ref[...]

        def chunk(r, c):
            r0, cur, x1, x2 = _gate_chunk(g_ref, r, rc)
            gc = w2 * cur + w1 * x1 + w0 * x2 + b
            a_ref[pl.ds(r0, rc), :] = (gc * jax.nn.sigmoid(gc) * u_ref[pl.ds(r0, rc), :]).astype(a_ref.dtype)
            return c

        lax.fori_loop(0, s // rc, chunk, 0)

    return _pcall(
        body, name=name, out_shape=jax.ShapeDtypeStruct((s, f), MXU_DTYPE), grid=(nj,),
        in_specs=[pl.BlockSpec((s, tc), lambda j: (0, j)), pl.BlockSpec((s, tc), lambda j: (0, j + nj)),
                  pl.BlockSpec((3, tc), lambda j: (0, j)), pl.BlockSpec((1, tc), lambda j: (0, j))],
        out_specs=pl.BlockSpec((s, tc), lambda j: (0, j)),
        compiler_params=_params("parallel"))(gu, gu, cw, _vec(cb))


def _ffn_gate_bwd(da, gu, cw, cb, name):
    s, f2 = gu.shape
    f = f2 // 2
    tc = _pick(f, (CONV_STRIPE,))
    rc = _pick(s, (CONV_ROWS, 128, 64, 32, 16, 8))
    nj = f // tc
    nr = s // rc

    def body(da_ref, g_ref, u_ref, cw_ref, cb_ref, dg_ref, du_ref, acc_ref, dgc_ref):
        w0, w1, w2, b = cw_ref[0:1, :], cw_ref[1:2, :], cw_ref[2:3, :], cb_ref[...]

        def chunk1(r, carry):
            a0, a1, a2, ab = carry
            r0, cur, x1, x2 = _gate_chunk(g_ref, r, rc)
            gc = w2 * cur + w1 * x1 + w0 * x2 + b
            sg = jax.nn.sigmoid(gc)
            da_c = da_ref[pl.ds(r0, rc), :]
            du_ref[pl.ds(r0, rc), :] = (da_c * (gc * sg)).astype(du_ref.dtype)
            dgc = da_c * u_ref[pl.ds(r0, rc), :] * (sg * (1.0 + gc * (1.0 - sg)))
            dgc_ref[pl.ds(r0, rc), :] = dgc
            return a0 + _fold8(dgc * x2), a1 + _fold8(dgc * x1), a2 + _fold8(dgc * cur), ab + _fold8(dgc)

        zero = jnp.zeros((SUBLANES, tc), F32)
        a0, a1, a2, ab = lax.fori_loop(0, nr, chunk1, (zero, zero, zero, zero))
        acc_ref[0], acc_ref[1], acc_ref[2], acc_ref[3] = a0, a1, a2, ab

        def chunk2(r, c):
            r0 = pl.multiple_of(r * rc, rc)
            cur = dgc_ref[pl.ds(r0, rc), :]
            hs = pl.multiple_of(jnp.minimum(r0 + rc, s - SUBLANES), SUBLANES)
            halo = jnp.where(r < nr - 1, dgc_ref[pl.ds(hs, SUBLANES), :], 0.0)
            dg = w2 * cur + w1 * _shift_up(cur, halo, 1) + w0 * _shift_up(cur, halo, 2)
            dg_ref[pl.ds(r0, rc), :] = dg.astype(dg_ref.dtype)
            return c

        lax.fori_loop(0, nr, chunk2, 0)

    stripe = pl.BlockSpec((s, tc), lambda j: (0, j))
    dg, du, acc = _pcall(
        body, name=name,
        out_shape=(jax.ShapeDtypeStruct((s, f), MXU_DTYPE), jax.ShapeDtypeStruct((s, f), MXU_DTYPE),
                   jax.ShapeDtypeStruct((4, SUBLANES, f), F32)),
        grid=(nj,),
        in_specs=[stripe, stripe, pl.BlockSpec((s, tc), lambda j: (0, j + nj)),
                  pl.BlockSpec((3, tc), lambda j: (0, j)), pl.BlockSpec((1, tc), lambda j: (0, j))],
        out_specs=(stripe, stripe, pl.BlockSpec((4, SUBLANES, tc), lambda j: (0, 0, j))),
        scratch_shapes=[pltpu.VMEM((s, tc), F32)],
        compiler_params=_params("parallel"))(da, gu, gu, cw, _vec(cb))
    acc = jnp.sum(acc, axis=1)
    return dg, du, acc[:3], acc[3]


def _add_rows(a, b, name):
    r = a.shape[0]
    tr = _pick(r, (1024, 512, 256, 128, 64, 32, 16, 8))

    def body(a_ref, b_ref, o_ref):
        o_ref[...] = a_ref[...] + b_ref[...]

    row = pl.BlockSpec((tr, LANES), lambda i: (i, 0))
    return _pcall(body, name=name, out_shape=jax.ShapeDtypeStruct(a.shape, a.dtype), grid=(r // tr,),
                  in_specs=[row, row], out_specs=row, compiler_params=_params("parallel"))(a, b)


def _adamw(gparts, w, m, v, name):
    r = w.shape[0]
    tr = _pick(r, (1024, 512, 256, 128, 64, 32, 16, 8))
    bc1 = 1.0 / (1.0 - ADAM_B1 ** ADAM_STEP)
    bc2 = 1.0 / (1.0 - ADAM_B2 ** ADAM_STEP)
    n = len(gparts)

    def body(*refs):
        w_ref, m_ref, v_ref, go_ref, d_ref, mo_ref, vo_ref = refs[n:]
        g = refs[0][...]
        for t in range(1, n):
            g = g + refs[t][...]
        mn = ADAM_B1 * m_ref[...] + (1.0 - ADAM_B1) * g
        vn = ADAM_B2 * v_ref[...] + (1.0 - ADAM_B2) * (g * g)
        m_hat = mn * bc1
        v_hat = vn * bc2
        go_ref[...] = g
        d_ref[...] = -ADAM_LR * (m_hat / (jnp.sqrt(v_hat) + ADAM_EPS) + ADAM_WD * w_ref[...])
        mo_ref[...] = mn
        vo_ref[...] = vn

    row = pl.BlockSpec((tr, LANES), lambda i: (i, 0))
    sds = jax.ShapeDtypeStruct((r, LANES), F32)
    return _pcall(
        body, name=name, out_shape=(sds, sds, sds, sds),
        grid=(r // tr,),
        in_specs=[pl.BlockSpec((None, tr, LANES), functools.partial(lambda slot, i: (slot, i, 0), slot))
                  for _, slot in gparts] + [row, row, row],
        out_specs=(row, row, row, row),
        compiler_params=_params("parallel"),
    )(*[a for a, _ in gparts], w, m, v)


def _sum_slots(gslots, name):
    n, r, _ = gslots.shape
    tr = _pick(r, (1024, 512, 256, 128, 64, 32, 16, 8))

    def body(g_ref, o_ref):
        g = g_ref[0]
        for t in range(1, n):
            g = g + g_ref[t]
        o_ref[...] = g

    return _pcall(
        body, name=name, out_shape=jax.ShapeDtypeStruct((r, LANES), F32),
        grid=(r // tr,),
        in_specs=[pl.BlockSpec((n, tr, LANES), lambda i: (0, i, 0))],
        out_specs=pl.BlockSpec((tr, LANES), lambda i: (i, 0)),
        compiler_params=_params("parallel"),
    )(gslots)


def _standardize(x, eps):
    mu = jnp.mean(x, axis=-1, keepdims=True)
    xc = x - mu
    var = jnp.mean(xc * xc, axis=-1, keepdims=True)
    return xc * lax.rsqrt(var + eps)


def _layer_norm(x, g, b):
    return _standardize(x, LN_EPS) * g + b


def _l2norm(x, eps=1e-6):
    return x * lax.rsqrt(jnp.sum(x * x, axis=-1, keepdims=True) + eps)


def _causal_dwconv(x, w):
    k_w, s = w.shape[0], x.shape[0]
    xp = jnp.pad(x, ((k_w - 1, 0), (0, 0)))
    y = xp[k_w - 1:k_w - 1 + s] * w[k_w - 1]
    for j in range(k_w - 1):
        y = y + xp[j:j + s] * w[j]
    return y


def _chunk_heads(t, n_heads, chunk):
    s, hd = t.shape
    return t.reshape(s // chunk, chunk, n_heads, hd // n_heads).transpose(2, 0, 1, 3)


def _unchunk_heads(t):
    h, n, c, d = t.shape
    return t.transpose(1, 2, 0, 3).reshape(n * c, h, d)


_NT = (((1,), (1,)), ((), ()))
_TN = (((0,), (0,)), ((), ()))


def _gdn_blocks(a, rev_from=None):
    h, _, r, c = a.shape
    if rev_from is None:
        return pl.BlockSpec((h, None, r, c), lambda n: (0, n, 0, 0))
    return pl.BlockSpec((h, None, r, c), lambda n: (0, rev_from - n, 0, 0))


def _gdn_scan_fwd_call(qg, w, u, qk, kd, e):
    H, n_chunks, c, dk = qg.shape
    dv = u.shape[-1]

    def body(qg_ref, w_ref, u_ref, qk_ref, kd_ref, e_ref, o_ref, sin_ref, vn_ref, state_ref):
        @pl.when(pl.program_id(0) == 0)
        def _():
            state_ref[...] = jnp.zeros_like(state_ref)

        for hd in range(H):
            st = state_ref[hd]
            stb = st.astype(MXU_DTYPE)
            sin_ref[hd] = st
            v_new = u_ref[hd] - jnp.dot(w_ref[hd].astype(MXU_DTYPE), stb, preferred_element_type=F32)
            vn_ref[hd] = v_new
            vnb = v_new.astype(MXU_DTYPE)
            o_ref[hd] = (jnp.dot(qg_ref[hd].astype(MXU_DTYPE), stb, preferred_element_type=F32)
                         + jnp.dot(qk_ref[hd].astype(MXU_DTYPE), vnb, preferred_element_type=F32))
            state_ref[hd] = st * e_ref[hd] + lax.dot_general(kd_ref[hd].astype(MXU_DTYPE), vnb, _TN,
                                                             preferred_element_type=F32)

    ins = (qg, w, u, qk, kd, e)
    outs = (jax.ShapeDtypeStruct((H, n_chunks, c, dv), F32), jax.ShapeDtypeStruct((H, n_chunks, dk, dv), F32),
            jax.ShapeDtypeStruct((H, n_chunks, c, dv), F32))
    return _pcall(
        body, name="gdn_scan_fwd", out_shape=outs, grid=(n_chunks,),
        in_specs=[_gdn_blocks(a) for a in ins], out_specs=tuple(_gdn_blocks(a) for a in outs),
        scratch_shapes=[pltpu.VMEM((H, dk, dv), F32)],
        compiler_params=_params("arbitrary"),
    )(*ins)


def _gdn_scan_bwd_call(qg, w, qk, kd, e, s_in, v_new, do):
    H, n_chunks, c, dk = qg.shape
    dv = v_new.shape[-1]

    def body(qg_ref, w_ref, qk_ref, kd_ref, e_ref, sin_ref, vn_ref, do_ref,
             dqg_ref, dw_ref, du_ref, dqk_ref, dkd_ref, de_ref, ds_ref):
        @pl.when(pl.program_id(0) == 0)
        def _():
            ds_ref[...] = jnp.zeros_like(ds_ref)

        for hd in range(H):
            st = sin_ref[hd]
            stb = st.astype(MXU_DTYPE)
            vnb = vn_ref[hd].astype(MXU_DTYPE)
            dob = do_ref[hd].astype(MXU_DTYPE)
            ds = ds_ref[hd]
            dsb = ds.astype(MXU_DTYPE)
            dvn = lax.dot_general(qk_ref[hd].astype(MXU_DTYPE), dob, _TN, preferred_element_type=F32)
            dqk_ref[hd] = lax.dot_general(dob, vnb, _NT, preferred_element_type=F32)
            dqg_ref[hd] = lax.dot_general(dob, stb, _NT, preferred_element_type=F32)
            ds_in = lax.dot_general(qg_ref[hd].astype(MXU_DTYPE), dob, _TN, preferred_element_type=F32)
            dvn = dvn + jnp.dot(kd_ref[hd].astype(MXU_DTYPE), dsb, preferred_element_type=F32)
            dkd_ref[hd] = lax.dot_general(vnb, dsb, _NT, preferred_element_type=F32)
            de_ref[hd] = _fold8(st * ds)
            ds_in = ds_in + ds * e_ref[hd]
            du_ref[hd] = dvn
            dvnb = dvn.astype(MXU_DTYPE)
            dw_ref[hd] = -lax.dot_general(dvnb, stb, _NT, preferred_element_type=F32)
            ds_ref[hd] = ds_in - lax.dot_general(w_ref[hd].astype(MXU_DTYPE), dvnb, _TN,
                                                 preferred_element_type=F32)

    last = n_chunks - 1
    ins = (qg, w, qk, kd, e, s_in, v_new, do)
    outs = (jax.ShapeDtypeStruct(qg.shape, F32), jax.ShapeDtypeStruct(w.shape, F32),
            jax.ShapeDtypeStruct(v_new.shape, F32), jax.ShapeDtypeStruct(qk.shape, F32),
            jax.ShapeDtypeStruct(kd.shape, F32), jax.ShapeDtypeStruct((H, n_chunks, SUBLANES, dv), F32))
    return _pcall(
        body, name="gdn_scan_bwd", out_shape=outs, grid=(n_chunks,),
        in_specs=[_gdn_blocks(a, last) for a in ins], out_specs=tuple(_gdn_blocks(a, last) for a in outs),
        scratch_shapes=[pltpu.VMEM((H, dk, dv), F32)],
        compiler_params=_params("arbitrary"),
    )(*ins)


@jax.custom_vjp
def _gdn_scan(qg, w, u, qk, kd, e):
    return _gdn_scan_fwd_call(qg, w, u, qk, kd, e)[0]


def _gdn_scan_fwd(qg, w, u, qk, kd, e):
    o, s_in, v_new = _gdn_scan_fwd_call(qg, w, u, qk, kd, e)
    return o, (qg, w, qk, kd, e, s_in, v_new)


def _gdn_scan_bwd(res, do):
    dqg, dw, du, dqk, dkd, de = _gdn_scan_bwd_call(*res, do)
    return dqg, dw, du, dqk, dkd, jnp.sum(de, axis=2, keepdims=True)


_gdn_scan.defvjp(_gdn_scan_fwd, _gdn_scan_bwd)


def _gated_deltanet(hx, p):
    H, C = GDN_HEADS, GDN_CHUNK
    s, d = hx.shape
    dk = dv = d // H
    qkvz = _linear(hx, p['gdn_w_qkvz'], "gdn_in")
    ab = _linear(hx, p['gdn_w_ab'], "gdn_ab")
    qkv, z = qkvz[:, :3 * d], qkvz[:, 3 * d:]
    a, bt = ab[:, :H], ab[:, H:2 * H]
    qkv = jax.nn.silu(_causal_dwconv(qkv, p['gdn_conv_w']))
    q, k, v = qkv[:, :d], qkv[:, d:2 * d], qkv[:, 2 * d:]
    q = _l2norm(_chunk_heads(q, H, C)) * (dk ** -0.5)
    k = _l2norm(_chunk_heads(k, H, C))
    v = _chunk_heads(v, H, C)
    beta = jax.nn.sigmoid(_chunk_heads(bt, H, C)[..., 0])
    g = -jnp.exp(p['gdn_a_log'])[:, None, None] * jax.nn.softplus(
        _chunk_heads(a, H, C)[..., 0] + p['gdn_dt_bias'][:, None, None])
    gc = jnp.cumsum(g, axis=-1)
    idx = jnp.arange(C)
    causal = idx[:, None] >= idx[None, :]
    strict = idx[:, None] > idx[None, :]
    diff = gc[..., :, None] - gc[..., None, :]
    decay = jnp.where(causal, jnp.exp(jnp.where(causal, diff, 0.0)), 0.0)
    kb = k * beta[..., None]
    kk = jnp.where(strict, jnp.einsum('hncd,hnmd->hncm', kb, k) * decay, 0.0)
    eye = jnp.eye(C, dtype=F32)
    rhs = jnp.concatenate([v * beta[..., None], kb * jnp.exp(gc)[..., None]], axis=-1)
    sol = lax.linalg.triangular_solve(kk + eye, rhs, left_side=True, lower=True, unit_diagonal=True)
    u, w = sol[..., :dv], sol[..., dv:]
    qk = jnp.where(causal, jnp.einsum('hncd,hnmd->hncm', q, k) * decay, 0.0)

    g_last = gc[..., -1:]
    e = jnp.broadcast_to(jnp.exp(g_last)[..., None], gc.shape[:2] + (1, dv))
    o = _gdn_scan(q * jnp.exp(gc)[..., None], w, u, qk, k * jnp.exp(g_last - gc)[..., None], e)
    o = _unchunk_heads(o)
    o = o * lax.rsqrt(jnp.mean(o * o, axis=-1, keepdims=True) + 1e-6) * p['gdn_norm_w']
    o = o * jax.nn.silu(z.reshape(s, H, dv))
    return _linear(o.reshape(s, H * dv), p['gdn_w_out'], "gdn_out")


def _ret_consts(c):
    log_gamma = jnp.log(1.0 - jnp.power(2.0, -5.0 - jnp.arange(RET_HEADS, dtype=F32)))
    idx = jnp.arange(c, dtype=F32)
    rel = idx[:, None] - idx[None, :]
    dmask = jnp.where(rel >= 0, jnp.exp(jnp.maximum(rel, 0.0) * log_gamma[:, None, None]), 0.0)
    zeta = jnp.exp((c - 1.0 - idx)[None, :] * log_gamma[:, None])[..., None]
    xi = jnp.exp((idx + 1.0)[None, :] * log_gamma[:, None])[..., None]
    gamma_c = jnp.exp(c * log_gamma)[:, None, None]
    return dmask, zeta, xi, gamma_c


def _ret_angles(s, dk):
    pos = jnp.arange(s, dtype=F32)
    inv_freq = RET_ROPE_BASE ** (-jnp.linspace(0.0, 1.0, dk // 2, dtype=F32))
    ang = pos[:, None] * inv_freq[None, :]
    return jnp.cos(ang), jnp.sin(ang)


def _rot(t, cs, sn):
    half = t.shape[1] // 2
    t1, t2 = t[:, :half], t[:, half:]
    return jnp.concatenate([t1 * cs - t2 * sn, t1 * sn + t2 * cs], axis=1)


def _rot_t(t, cs, sn):
    half = t.shape[1] // 2
    t1, t2 = t[:, :half], t[:, half:]
    return jnp.concatenate([t1 * cs + t2 * sn, t2 * cs - t1 * sn], axis=1)


def _ret_cols(d, dk, dv, hd):
    return (slice(hd * dk, (hd + 1) * dk), slice(d + hd * dk, d + (hd + 1) * dk),
            slice(2 * d + hd * dv, 2 * d + (hd + 1) * dv), slice(4 * d + hd * dv, 4 * d + (hd + 1) * dv))


def _ret_fwd_call(proj):
    s, d6 = proj.shape
    d = d6 // 6
    H, c = RET_HEADS, RET_CHUNK
    dk, dv = d // H, 2 * d // H
    n_chunks = s // c
    kscale = dk ** -0.5
    cos_a, sin_a = _ret_angles(s, dk)
    consts = _ret_consts(c)

    def body(p_ref, cos_ref, sin_ref, dm_ref, ze_ref, xi_ref, gc_ref, out_ref, oraw_ref, st_ref, state_ref):
        @pl.when(pl.program_id(0) == 0)
        def _():
            state_ref[...] = jnp.zeros_like(state_ref)

        cs, sn = cos_ref[...], sin_ref[...]
        for hd in range(H):
            qc, kc, vc, gcol = _ret_cols(d, dk, dv, hd)
            ocol = slice(hd * dv, (hd + 1) * dv)
            qb = _rot(p_ref[:, qc], cs, sn).astype(MXU_DTYPE)
            kr = _rot(p_ref[:, kc], cs, sn) * kscale
            kb = kr.astype(MXU_DTYPE)
            vb = p_ref[:, vc].astype(MXU_DTYPE)
            st = state_ref[hd]
            stb = st.astype(MXU_DTYPE)
            st_ref[hd] = stb
            sc = lax.dot_general(qb, kb, _NT, preferred_element_type=F32) * dm_ref[hd]
            o = (jnp.dot(sc.astype(MXU_DTYPE), vb, preferred_element_type=F32)
                 + jnp.dot(qb, stb, preferred_element_type=F32) * xi_ref[hd])
            state_ref[hd] = st * gc_ref[hd] + lax.dot_general((kr * ze_ref[hd]).astype(MXU_DTYPE), vb, _TN,
                                                              preferred_element_type=F32)
            oraw_ref[:, ocol] = o
            oc = o - jnp.mean(o, axis=-1, keepdims=True)
            on = oc * lax.rsqrt(jnp.mean(oc * oc, axis=-1, keepdims=True) + 1e-6)
            gate = p_ref[:, gcol]
            out_ref[:, ocol] = on * (gate * jax.nn.sigmoid(gate))

    row = lambda width: pl.BlockSpec((c, width), lambda n: (n, 0))
    whole = lambda a: pl.BlockSpec(a.shape, lambda n: (0,) * a.ndim)
    return _pcall(
        body, name="ret_fwd",
        out_shape=(jax.ShapeDtypeStruct((s, 2 * d), F32), jax.ShapeDtypeStruct((s, 2 * d), F32),
                   jax.ShapeDtypeStruct((n_chunks, H, dk, dv), MXU_DTYPE)),
        grid=(n_chunks,),
        in_specs=[row(d6), row(dk // 2), row(dk // 2)] + [whole(a) for a in consts],
        out_specs=(row(2 * d), row(2 * d), pl.BlockSpec((None, H, dk, dv), lambda n: (n, 0, 0, 0))),
        scratch_shapes=[pltpu.VMEM((H, dk, dv), F32)],
        compiler_params=_params("arbitrary"),
    )(proj, cos_a, sin_a, *consts)


def _ret_bwd_call(proj, oraw, states, dout):
    s, d6 = proj.shape
    d = d6 // 6
    H, c = RET_HEADS, RET_CHUNK
    dk, dv = d // H, 2 * d // H
    n_chunks = s // c
    kscale = dk ** -0.5
    cos_a, sin_a = _ret_angles(s, dk)
    consts = _ret_consts(c)

    def body(p_ref, cos_ref, sin_ref, dm_ref, ze_ref, xi_ref, gc_ref, oraw_ref, st_ref, do_ref, dp_ref, ds_ref):
        @pl.when(pl.program_id(0) == 0)
        def _():
            ds_ref[...] = jnp.zeros_like(ds_ref)

        cs, sn = cos_ref[...], sin_ref[...]
        for hd in range(H):
            qc, kc, vc, gcol = _ret_cols(d, dk, dv, hd)
            ocol = slice(hd * dv, (hd + 1) * dv)
            qb = _rot(p_ref[:, qc], cs, sn).astype(MXU_DTYPE)
            kr = _rot(p_ref[:, kc], cs, sn) * kscale
            kb = kr.astype(MXU_DTYPE)
            vb = p_ref[:, vc].astype(MXU_DTYPE)
            gate = p_ref[:, gcol]
            o = oraw_ref[:, ocol]
            oc = o - jnp.mean(o, axis=-1, keepdims=True)
            rstd = lax.rsqrt(jnp.mean(oc * oc, axis=-1, keepdims=True) + 1e-6)
            on = oc * rstd
            dout_h = do_ref[:, ocol]
            sg = jax.nn.sigmoid(gate)
            dp_ref[:, gcol] = dout_h * on * (sg * (1.0 + gate * (1.0 - sg)))
            don = dout_h * (gate * sg)
            do_raw = rstd * (don - jnp.mean(don, axis=-1, keepdims=True)
                             - on * jnp.mean(don * on, axis=-1, keepdims=True))
            dob = do_raw.astype(MXU_DTYPE)
            stb = st_ref[hd]
            ds = ds_ref[hd]
            dsb = ds.astype(MXU_DTYPE)
            dm = dm_ref[hd]
            scb = (lax.dot_general(qb, kb, _NT, preferred_element_type=F32) * dm).astype(MXU_DTYPE)
            dsc = (lax.dot_general(dob, vb, _NT, preferred_element_type=F32) * dm).astype(MXU_DTYPE)
            dqr = jnp.dot(dsc, kb, preferred_element_type=F32)
            dkr = lax.dot_general(dsc, qb, _TN, preferred_element_type=F32)
            dvv = lax.dot_general(scb, dob, _TN, preferred_element_type=F32)
            doi = (do_raw * xi_ref[hd]).astype(MXU_DTYPE)
            dqr = dqr + lax.dot_general(doi, stb, _NT, preferred_element_type=F32)
            ds_in = lax.dot_general(qb, doi, _TN, preferred_element_type=F32)
            ze = ze_ref[hd]
            dkr = dkr + lax.dot_general(vb, dsb, _NT, preferred_element_type=F32) * ze
            dvv = dvv + jnp.dot((kr * ze).astype(MXU_DTYPE), dsb, preferred_element_type=F32)
            ds_ref[hd] = ds * gc_ref[hd] + ds_in
            dp_ref[:, qc] = _rot_t(dqr, cs, sn)
            dp_ref[:, kc] = _rot_t(dkr * kscale, cs, sn)
            dp_ref[:, vc] = dvv

    last = n_chunks - 1
    row = lambda width: pl.BlockSpec((c, width), lambda n: (last - n, 0))
    whole = lambda a: pl.BlockSpec(a.shape, lambda n: (0,) * a.ndim)
    return _pcall(
        body, name="ret_bwd", out_shape=jax.ShapeDtypeStruct((s, d6), F32),
        grid=(n_chunks,),
        in_specs=[row(d6), row(dk // 2), row(dk // 2)] + [whole(a) for a in consts]
                 + [row(2 * d), pl.BlockSpec((None, H, dk, dv), lambda n: (last - n, 0, 0, 0)), row(2 * d)],
        out_specs=row(d6),
        scratch_shapes=[pltpu.VMEM((H, dk, dv), F32)],
        compiler_params=_params("arbitrary"),
    )(proj, cos_a, sin_a, *consts, oraw, states, dout)


@jax.custom_vjp
def _ret_core(proj):
    return _ret_fwd_call(proj)[0]


def _ret_core_fwd(proj):
    out, oraw, states = _ret_fwd_call(proj)
    return out, (proj, oraw, states)


def _ret_core_bwd(res, dout):
    return (_ret_bwd_call(*res, dout),)


_ret_core.defvjp(_ret_core_fwd, _ret_core_bwd)


def _retention(hx, p):
    return _linear(_ret_core(_linear(hx, p['ret_w_in'], "ret_in")), p['ret_w_out'], "ret_out")


SQRT_HALF = 2.0 ** -0.5
INV_SQRT_2PI = (2.0 * math.pi) ** -0.5


def _gmlp_front(p_ref, g_ref, b_ref, w):
    x = p_ref[...]
    cdf = 0.5 * (1.0 + lax.erf(x * SQRT_HALF))
    uv = x * cdf
    u, v = uv[:, :w], uv[:, w:]
    vc = v - jnp.mean(v, axis=-1, keepdims=True)
    rstd = lax.rsqrt(jnp.mean(vc * vc, axis=-1, keepdims=True) + LN_EPS)
    vhat = vc * rstd
    return x, cdf, u, vhat, rstd, vhat * g_ref[...] + b_ref[...]


def _gmlp_fwd_call(proj, ln_g, ln_b, ws, bs):
    s, w2 = proj.shape
    w = w2 // 2
    c, G = GMLP_CHUNK, GMLP_GROUPS
    gw = w // G

    def body(p_ref, g_ref, b_ref, ws_ref, bs_ref, o_ref):
        _, _, u, _, _, vn = _gmlp_front(p_ref, g_ref, b_ref, w)
        for gi in range(G):
            cols = slice(gi * gw, (gi + 1) * gw)
            vs = jnp.dot(ws_ref[gi].astype(MXU_DTYPE), vn[:, cols].astype(MXU_DTYPE),
                         preferred_element_type=F32) + bs_ref[gi]
            o_ref[:, cols] = u[:, cols] * vs

    whole = lambda a: pl.BlockSpec(a.shape, lambda n: (0,) * a.ndim)
    args = (_vec(ln_g), _vec(ln_b), ws, bs)
    return _pcall(
        body, name="gmlp_fwd", out_shape=jax.ShapeDtypeStruct((s, w), F32), grid=(s // c,),
        in_specs=[pl.BlockSpec((c, w2), lambda n: (n, 0))] + [whole(a) for a in args],
        out_specs=pl.BlockSpec((c, w), lambda n: (n, 0)),
        compiler_params=_params("parallel"),
    )(proj, *args)


def _gmlp_bwd_call(proj, ln_g, ln_b, ws, bs, dout):
    s, w2 = proj.shape
    w = w2 // 2
    c, G = GMLP_CHUNK, GMLP_GROUPS
    gw = w // G

    def body(p_ref, g_ref, b_ref, ws_ref, bs_ref, do_ref, dp_ref, dws_ref, dbs_ref, dgb_ref):
        @pl.when(pl.program_id(0) == 0)
        def _():
            dws_ref[...] = jnp.zeros_like(dws_ref)
            dbs_ref[...] = jnp.zeros_like(dbs_ref)
            dgb_ref[...] = jnp.zeros_like(dgb_ref)

        x, cdf, u, vhat, rstd, vn = _gmlp_front(p_ref, g_ref, b_ref, w)
        dout = do_ref[...]
        du_parts, dvn_parts = [], []
        for gi in range(G):
            cols = slice(gi * gw, (gi + 1) * gw)
            wsg = ws_ref[gi].astype(MXU_DTYPE)
            vng = vn[:, cols].astype(MXU_DTYPE)
            vs = jnp.dot(wsg, vng, preferred_element_type=F32) + bs_ref[gi]
            du_parts.append(dout[:, cols] * vs)
            dvs = dout[:, cols] * u[:, cols]
            dbs_ref[:, cols] += dvs
            dvsb = dvs.astype(MXU_DTYPE)
            dws_ref[gi] += lax.dot_general(dvsb, vng, _NT, preferred_element_type=F32)
            dvn_parts.append(lax.dot_general(wsg, dvsb, _TN, preferred_element_type=F32))
        dvn = jnp.concatenate(dvn_parts, axis=1)
        dgb_ref[0] += _fold8(dvn * vhat)
        dgb_ref[1] += _fold8(dvn)
        dvh = dvn * g_ref[...]
        dv = rstd * (dvh - jnp.mean(dvh, axis=-1, keepdims=True)
                     - vhat * jnp.mean(dvh * vhat, axis=-1, keepdims=True))
        duv = jnp.concatenate(du_parts + [dv], axis=1)
        dp_ref[...] = duv * (cdf + x * (jnp.exp(-0.5 * x * x) * INV_SQRT_2PI))

    whole = lambda a: pl.BlockSpec(a.shape, lambda n: (0,) * a.ndim)
    args = (_vec(ln_g), _vec(ln_b), ws, bs)
    acc = lambda *shape: pl.BlockSpec(shape, lambda n: (0,) * len(shape))
    return _pcall(
        body, name="gmlp_bwd",
        out_shape=(jax.ShapeDtypeStruct((s, w2), F32), jax.ShapeDtypeStruct((G, c, c), F32),
                   jax.ShapeDtypeStruct((c, w), F32), jax.ShapeDtypeStruct((2, SUBLANES, w), F32)),
        grid=(s // c,),
        in_specs=[pl.BlockSpec((c, w2), lambda n: (n, 0))] + [whole(a) for a in args]
                 + [pl.BlockSpec((c, w), lambda n: (n, 0))],
        out_specs=(pl.BlockSpec((c, w2), lambda n: (n, 0)), acc(G, c, c), acc(c, w), acc(2, SUBLANES, w)),
        compiler_params=_params("arbitrary"),
    )(proj, *args, dout)


def _gmlp_mask(c):
    return jnp.tril(jnp.ones((c, c), dtype=bool))


@jax.custom_vjp
def _gmlp_core(proj, ln_g, ln_b, w_s, b_s):
    ws = jnp.where(_gmlp_mask(GMLP_CHUNK), w_s, 0.0)
    return _gmlp_fwd_call(proj, ln_g, ln_b, ws, b_s[..., None])


def _gmlp_core_fwd(proj, ln_g, ln_b, w_s, b_s):
    return _gmlp_core(proj, ln_g, ln_b, w_s, b_s), (proj, ln_g, ln_b, w_s, b_s)


def _gmlp_core_bwd(res, dout):
    proj, ln_g, ln_b, w_s, b_s = res
    mask = _gmlp_mask(GMLP_CHUNK)
    dproj, dws, dbs, dgb = _gmlp_bwd_call(proj, ln_g, ln_b, jnp.where(mask, w_s, 0.0), b_s[..., None], dout)
    dgb = jnp.sum(dgb, axis=1)
    c = GMLP_CHUNK
    db_s = jnp.sum(dbs.reshape(c, GMLP_GROUPS, -1), axis=-1).T
    return dproj, dgb[0], dgb[1], jnp.where(mask, dws, 0.0), db_s


_gmlp_core.defvjp(_gmlp_core_fwd, _gmlp_core_bwd)


def _chunked_gmlp(hx, p):
    core = _gmlp_core(_linear(hx, p['gmlp_w_in'], "gmlp_in"), p['gmlp_ln_g'], p['gmlp_ln_b'],
                      p['gmlp_w_s'], p['gmlp_b_s'])
    return _linear(core, p['gmlp_w_out'], "gmlp_out")


def _stick_breaking(hx, p):
    H = SB_HEADS
    s, d = hx.shape
    dh = d // H
    qkv = _linear(hx, p['sb_w_in'], "sb_in")
    q, k, v = (qkv[:, j * d:(j + 1) * d].reshape(s, H, dh).transpose(1, 0, 2) for j in range(3))
    o = _sb_core(q, k, v)
    return _linear(o.transpose(1, 0, 2).reshape(s, d), p['sb_w_out'], "sb_out")


MIXERS = (_gated_deltanet, _retention, _chunked_gmlp, _stick_breaking)


def _trunk_grad(x, mods, p, target):
    d = x.shape[-1]
    saved = []
    for i in range(DEPTH):
        sh1, sc1, g1, sh2, sc2, g2 = (mods[i, j * d:(j + 1) * d] for j in range(6))
        h1 = _modulate(x, sc1, sh1, F32, "mod_a%d" % i)
        y1, mixer_vjp = jax.vjp(MIXERS[i], h1, {n: p[n] for n in MIXER_PARAMS[i]})
        x1 = _resid_ln(x, y1, g1, p['ln_g'][i, 0], p['ln_b'][i, 0], "ln_a%d" % i)
        h2 = _modulate(x1, sc2, sh2, MXU_DTYPE, "mod_b%d" % i)
        gu = _mm(h2, p['ffn_up'][i], 'nn', "ffn_up%d_fwd" % i)
        act = _ffn_gate(gu, p['ffn_conv_w'][i], p['ffn_conv_b'][i], "ffn_gate%d" % i)
        y2 = _mm(act, p['ffn_down'][i], 'nn', "ffn_down%d_fwd" % i)
        x2 = _resid_ln(x1, y2, g2, p['ln_g'][i, 1], p['ln_b'][i, 1], "ln_b%d" % i)
        saved.append((x, y1, mixer_vjp, x1, h2, gu, act, y2))
        x = x2
    loss, dx = _loss_head(x, target, "loss_head")

    dp = {n: None for n in p}
    d_ln_g, d_ln_b, d_up, d_down, d_cw, d_cb, dmods = [], [], [], [], [], [], []
    for i in reversed(range(DEPTH)):
        x0, y1, mixer_vjp, x1, h2, gu, act, y2 = saved[i]
        sh1, sc1, g1, sh2, sc2, g2 = (mods[i, j * d:(j + 1) * d] for j in range(6))
        dxa, dy2, dgam2, dbet2, dg2 = _resid_ln_bwd(x1, y2, g2, p['ln_g'][i, 1], dx, "ln_b%d_bwd" % i)
        dact = _mm(dy2, p['ffn_down'][i], 'nt', "ffn_down%d_dx" % i)
        d_down.append(_mm(act, dy2, 'tn', "ffn_down%d_dw" % i))
        dgate, dupp, dcw, dcb = _ffn_gate_bwd(dact, gu, p['ffn_conv_w'][i], p['ffn_conv_b'][i],
                                              "ffn_gate%d_bwd" % i)
        dgu = jnp.concatenate([dgate, dupp], axis=1)
        dh2 = _mm(dgu, p['ffn_up'][i], 'nt', "ffn_up%d_dx" % i)
        d_up.append(_mm(h2, dgu, 'tn', "ffn_up%d_dw" % i))
        dx1, dsc2, dsh2 = _modulate_bwd(dxa, dh2, x1, sc2, "mod_b%d_bwd" % i)
        dxa, dy1, dgam1, dbet1, dg1 = _resid_ln_bwd(x0, y1, g1, p['ln_g'][i, 0], dx1, "ln_a%d_bwd" % i)
        dh1, dmix = mixer_vjp(dy1)
        dp.update(dmix)
        dx, dsc1, dsh1 = _modulate_bwd(dxa, dh1, x0, sc1, "mod_a%d_bwd" % i)
        d_ln_g.append(jnp.stack([dgam1, dgam2]))
        d_ln_b.append(jnp.stack([dbet1, dbet2]))
        d_cw.append(dcw)
        d_cb.append(dcb)
        dmods.append(jnp.concatenate([dsh1, dsc1, dg1, dsh2, dsc2, dg2]))
    for n, parts in (('ln_g', d_ln_g), ('ln_b', d_ln_b), ('ffn_up', d_up), ('ffn_down', d_down),
                     ('ffn_conv_w', d_cw), ('ffn_conv_b', d_cb)):
        dp[n] = jnp.stack(parts[::-1])
    return loss, dx, jnp.stack(dmods[::-1]), dp


def _join(blocks, axis):
    return jnp.concatenate([blocks[d] for d in range(N_DEV)], axis=axis)


def _pad8(a):
    pad = (-a.shape[0]) % 8
    return jnp.pad(a, ((0, pad), (0, 0))) if pad else a


def _pack_big_grads(full_grads, axes):
    def by_device(g, ax):
        pre, post = math.prod(g.shape[:ax]), math.prod(g.shape[ax + 1:])
        return g.reshape(pre, N_DEV, g.shape[ax] // N_DEV, post).transpose(1, 0, 2, 3).reshape(N_DEV, -1)

    per_dev = jnp.concatenate([by_device(g, ax) for g, ax in zip(full_grads, axes)], axis=1)
    pad = (-per_dev.shape[1]) % (BIG_ROW_ALIGN * LANES)
    if pad:
        per_dev = jnp.pad(per_dev, ((0, 0), (0, pad)))
    return per_dev.reshape(N_DEV, -1, LANES)


def kernel(x, c, cond_w, cond_b, ada_w, ada_b, ln_g, ln_b, ffn_up, ffn_conv_w, ffn_conv_b, ffn_down, gdn_w_in, gdn_conv_w, gdn_a_log, gdn_dt_bias, gdn_norm_w, gdn_w_out, ret_w_in, ret_w_out, gmlp_w_in, gmlp_ln_g, gmlp_ln_b, gmlp_w_s, gmlp_b_s, gmlp_w_out, sb_w_in, sb_w_out, loss_target, m_cond_w, m_cond_b, m_ada_w, m_ada_b, m_ln_g, m_ln_b, m_ffn_up, m_ffn_conv_w, m_ffn_conv_b, m_ffn_down, m_gdn_w_in, m_gdn_conv_w, m_gdn_a_log, m_gdn_dt_bias, m_gdn_norm_w, m_gdn_w_out, m_ret_w_in, m_ret_w_out, m_gmlp_w_in, m_gmlp_ln_g, m_gmlp_ln_b, m_gmlp_w_s, m_gmlp_b_s, m_gmlp_w_out, m_sb_w_in, m_sb_w_out, v_cond_w, v_cond_b, v_ada_w, v_ada_b, v_ln_g, v_ln_b, v_ffn_up, v_ffn_conv_w, v_ffn_conv_b, v_ffn_down, v_gdn_w_in, v_gdn_conv_w, v_gdn_a_log, v_gdn_dt_bias, v_gdn_norm_w, v_gdn_w_out, v_ret_w_in, v_ret_w_out, v_gmlp_w_in, v_gmlp_ln_g, v_gmlp_ln_b, v_gmlp_w_s, v_gmlp_b_s, v_gmlp_w_out, v_sb_w_in, v_sb_w_out):
    w = dict(cond_w=cond_w, cond_b=cond_b, ada_w=ada_w, ada_b=ada_b, ln_g=ln_g, ln_b=ln_b, ffn_up=ffn_up,
             ffn_conv_w=ffn_conv_w, ffn_conv_b=ffn_conv_b, ffn_down=ffn_down, gdn_w_in=gdn_w_in,
             gdn_conv_w=gdn_conv_w, gdn_a_log=gdn_a_log, gdn_dt_bias=gdn_dt_bias, gdn_norm_w=gdn_norm_w,
             gdn_w_out=gdn_w_out, ret_w_in=ret_w_in, ret_w_out=ret_w_out, gmlp_w_in=gmlp_w_in,
             gmlp_ln_g=gmlp_ln_g, gmlp_ln_b=gmlp_ln_b, gmlp_w_s=gmlp_w_s, gmlp_b_s=gmlp_b_s,
             gmlp_w_out=gmlp_w_out, sb_w_in=sb_w_in, sb_w_out=sb_w_out)
    mom = dict(cond_w=m_cond_w, cond_b=m_cond_b, ada_w=m_ada_w, ada_b=m_ada_b, ln_g=m_ln_g, ln_b=m_ln_b,
               ffn_up=m_ffn_up, ffn_conv_w=m_ffn_conv_w, ffn_conv_b=m_ffn_conv_b, ffn_down=m_ffn_down,
               gdn_w_in=m_gdn_w_in, gdn_conv_w=m_gdn_conv_w, gdn_a_log=m_gdn_a_log, gdn_dt_bias=m_gdn_dt_bias,
               gdn_norm_w=m_gdn_norm_w, gdn_w_out=m_gdn_w_out, ret_w_in=m_ret_w_in, ret_w_out=m_ret_w_out,
               gmlp_w_in=m_gmlp_w_in, gmlp_ln_g=m_gmlp_ln_g, gmlp_ln_b=m_gmlp_ln_b, gmlp_w_s=m_gmlp_w_s,
               gmlp_b_s=m_gmlp_b_s, gmlp_w_out=m_gmlp_w_out, sb_w_in=m_sb_w_in, sb_w_out=m_sb_w_out)
    var = dict(cond_w=v_cond_w, cond_b=v_cond_b, ada_w=v_ada_w, ada_b=v_ada_b, ln_g=v_ln_g, ln_b=v_ln_b,
               ffn_up=v_ffn_up, ffn_conv_w=v_ffn_conv_w, ffn_conv_b=v_ffn_conv_b, ffn_down=v_ffn_down,
               gdn_w_in=v_gdn_w_in, gdn_conv_w=v_gdn_conv_w, gdn_a_log=v_gdn_a_log, gdn_dt_bias=v_gdn_dt_bias,
               gdn_norm_w=v_gdn_norm_w, gdn_w_out=v_gdn_w_out, ret_w_in=v_ret_w_in, ret_w_out=v_ret_w_out,
               gmlp_w_in=v_gmlp_w_in, gmlp_ln_g=v_gmlp_ln_g, gmlp_ln_b=v_gmlp_ln_b, gmlp_w_s=v_gmlp_w_s,
               gmlp_b_s=v_gmlp_b_s, gmlp_w_out=v_gmlp_w_out, sb_w_in=v_sb_w_in, sb_w_out=v_sb_w_out)

    me = _my_id()
    x = x[0]
    target = loss_target[0]
    d = x.shape[-1]
    dsh = d // N_DEV
    msh = ada_w.shape[-1]

    c_all = _exchange(_pad8(c), False, "gather_c")[:, 0, :]
    c_mine = lax.dynamic_slice_in_dim(c_all, me * dsh, dsh, axis=1)
    pre_part = _mm(c_mine, cond_w, 'nn', "cond_fwd")
    pre = jnp.sum(_exchange(pre_part, False, "gather_pre"), axis=0) + cond_b
    e_all = jax.nn.silu(pre)
    mod_part = jnp.concatenate([_mm(e_all, ada_w[i], 'nn', "ada_fwd%d" % i) for i in range(DEPTH)], axis=0)
    mod_all = _exchange(mod_part, False, "gather_mod")
    mod_all = mod_all.reshape(N_DEV, DEPTH, N_DEV, msh)
    mods = lax.dynamic_index_in_dim(mod_all, me, axis=2, keepdims=False)
    mods = mods.transpose(1, 0, 2).reshape(DEPTH, N_DEV * msh) + ada_b

    big_names = list(BIG)
    packed = _pack_rows([w[n] for n in big_names], BF16, BIG_ROW_ALIGN)
    gathered = _gather_two_level(packed, "gather_weights")
    blocks = _unpack_rows(gathered, [w[n].shape for n in big_names])
    p = {n: _join(b, BIG[n]) for n, b in zip(big_names, blocks)}
    for n in big_names:
        if not n.startswith('ffn_'):
            p[n] = p[n].astype(F32)
    sm_names = list(SMALL_SHARDED)
    sm_packed = _pack_rows([w[n] for n in sm_names], F32)
    sm_blocks = _unpack_rows(_exchange(sm_packed, False, "gather_small"), [w[n].shape for n in sm_names])
    for n, b in zip(sm_names, sm_blocks):
        p[n] = _join(b, SMALL_SHARDED[n])
    for n in SMALL_REPL:
        p[n] = w[n]
    n_qkvz = 4 * d
    p['gdn_w_qkvz'] = p['gdn_w_in'][:, :n_qkvz]
    p['gdn_w_ab'] = jnp.pad(p['gdn_w_in'][:, n_qkvz:], ((0, 0), (0, LANES - 2 * GDN_HEADS)))
    del p['gdn_w_in']

    loss_local, dx, dmods, dp = _trunk_grad(x, mods, p, target)
    dp['gdn_w_in'] = jnp.concatenate([dp.pop('gdn_w_qkvz'), dp.pop('gdn_w_ab')[:, :2 * GDN_HEADS]], axis=1)

    dmod_all = _exchange(dmods.reshape(-1, d), False, "gather_dmod").reshape(N_DEV, DEPTH, 6 * d)
    grads = {'ada_b': jnp.sum(dmod_all, axis=0)}
    dm_mine = lax.dynamic_slice_in_dim(dmod_all, me * msh, msh, axis=2)
    grads['ada_w'] = jnp.stack([_mm_outer(e_all, dm_mine[:, i], "ada_dw%d" % i) for i in range(DEPTH)])
    de_part = _mm(dm_mine[:, 0], ada_w[0], 'nt', "ada_de0")
    for i in range(1, DEPTH):
        de_part = de_part + _mm(dm_mine[:, i], ada_w[i], 'nt', "ada_de%d" % i)
    de_all = jnp.sum(_exchange(de_part, False, "gather_de"), axis=0)
    sig = jax.nn.sigmoid(pre)
    dpre = de_all * (sig * (1.0 + pre * (1.0 - sig)))
    grads['cond_b'] = jnp.sum(dpre, axis=0)
    grads['cond_w'] = _mm_outer(c_mine, dpre, "cond_dw")

    small_names = sm_names + SMALL_REPL
    small_packed = _pack_rows([loss_local.reshape(1)] + [dp[n] for n in small_names], F32)
    small_sum = _sum_slots(_exchange(small_packed, False, "gather_small_grads"), "sum_small_grads")
    small = _unpack_rows(small_sum, [(1,)] + [dp[n].shape for n in small_names])
    loss = small[0][0]
    for n, g in zip(small_names, small[1:]):
        if n in SMALL_SHARDED:
            ax = SMALL_SHARDED[n]
            g = lax.dynamic_slice_in_dim(g, me * w[n].shape[ax], w[n].shape[ax], axis=ax)
        grads[n] = g

    send = _pack_big_grads([dp[n] for n in big_names], [BIG[n] for n in big_names])
    shapes = [w[n].shape for n in big_names]
    outs = _adamw(_reduce_to_owner(send), *[_pack_rows([t[n] for n in big_names], F32, BIG_ROW_ALIGN) for t in (w, mom, var)],
                  "adamw_big")
    g_b, d_b, m_b, v_b = (_unpack_rows(o, shapes) for o in outs)
    delta, new_m, new_v = {}, {}, {}
    for j, n in enumerate(big_names):
        grads[n], delta[n], new_m[n], new_v[n] = g_b[j], d_b[j], m_b[j], v_b[j]

    rest = [n for n in WEIGHTS if n not in BIG]
    shapes = [w[n].shape for n in rest]
    outs = _adamw([(_pack_rows([grads[n] for n in rest], F32, BIG_ROW_ALIGN)[None], 0)],
                  *[_pack_rows([t[n] for n in rest], F32, BIG_ROW_ALIGN) for t in (w, mom, var)], "adamw_rest")
    _, d_r, m_r, v_r = (_unpack_rows(o, shapes) for o in outs)
    for j, n in enumerate(rest):
        delta[n], new_m[n], new_v[n] = d_r[j], m_r[j], v_r[j]

    return (loss, dx[None], *[grads[n] for n in WEIGHTS], *[delta[n] for n in WEIGHTS],
            *[new_m[n] for n in WEIGHTS], *[new_v[n] for n in WEIGHTS])
```

```python
import functools
import math

import jax
import jax.numpy as jnp
from jax import lax
from jax.experimental import pallas as pl
from jax.experimental.pallas import tpu as pltpu

F32 = jnp.float32
BF16 = jnp.bfloat16
MXU_DTYPE = jnp.bfloat16
MESH = pl.DeviceIdType.MESH
N_DEV = 8
LANES = 128
SUBLANES = 8
VMEM_LIMIT = 48 * 1024 * 1024

DEPTH = 4
LN_EPS = 1e-5
DN_ALPHA = (2.0 * DEPTH) ** 0.25
GDN_HEADS, GDN_CONV, GDN_CHUNK = 8, 4, 64
RET_HEADS, RET_CHUNK, RET_ROPE_BASE = 4, 128, 10000.0
GMLP_CHUNK, GMLP_GROUPS = 128, 8
SB_HEADS = 16
ADAM_LR, ADAM_B1, ADAM_B2, ADAM_EPS, ADAM_WD, ADAM_STEP = 0.001, 0.9, 0.999, 1e-08, 0.01, 10

WEIGHTS = ['cond_w', 'cond_b', 'ada_w', 'ada_b', 'ln_g', 'ln_b', 'ffn_up', 'ffn_conv_w', 'ffn_conv_b', 'ffn_down',
           'gdn_w_in', 'gdn_conv_w', 'gdn_a_log', 'gdn_dt_bias', 'gdn_norm_w', 'gdn_w_out', 'ret_w_in', 'ret_w_out',
           'gmlp_w_in', 'gmlp_ln_g', 'gmlp_ln_b', 'gmlp_w_s', 'gmlp_b_s', 'gmlp_w_out', 'sb_w_in', 'sb_w_out']
BIG = {'ffn_up': 2, 'ffn_down': 1, 'gdn_w_in': 1, 'gdn_w_out': 0, 'ret_w_in': 1, 'ret_w_out': 0,
       'gmlp_w_in': 1, 'gmlp_w_out': 0, 'sb_w_in': 1, 'sb_w_out': 0}
SMALL_SHARDED = {'ln_g': 2, 'ln_b': 2, 'ffn_conv_w': 2, 'gdn_conv_w': 1}
SMALL_REPL = ['ffn_conv_b', 'gdn_a_log', 'gdn_dt_bias', 'gdn_norm_w', 'gmlp_ln_g', 'gmlp_ln_b', 'gmlp_w_s', 'gmlp_b_s']
MIXER_PARAMS = (('gdn_w_qkvz', 'gdn_w_ab', 'gdn_conv_w', 'gdn_a_log', 'gdn_dt_bias', 'gdn_norm_w', 'gdn_w_out'),
                ('ret_w_in', 'ret_w_out'),
                ('gmlp_w_in', 'gmlp_ln_g', 'gmlp_ln_b', 'gmlp_w_s', 'gmlp_b_s', 'gmlp_w_out'),
                ('sb_w_in', 'sb_w_out'))


def _pcall(body, **kw):
    return pl.pallas_call(body, **kw)


def _params(*semantics):
    return pltpu.CompilerParams(dimension_semantics=semantics, vmem_limit_bytes=VMEM_LIMIT)


def _my_id():
    return 4 * lax.axis_index("x") + 2 * lax.axis_index("y") + lax.axis_index("c")


def _pick(dim, prefs):
    for p in prefs:
        if dim % p == 0:
            return p
    return dim


def _exchange(src, scatter, name):
    blk = src.shape[1:] if scatter else src.shape
    out_shape = jax.ShapeDtypeStruct((N_DEV,) + tuple(blk), src.dtype)

    def body(src_ref, out_ref, send_sems, recv_sems, local_sem):
        x, y, c = lax.axis_index("x"), lax.axis_index("y"), lax.axis_index("c")
        me = 4 * x + 2 * y + c
        mine = pltpu.make_async_copy(src_ref.at[me] if scatter else src_ref, out_ref.at[me], local_sem)
        mine.start()
        copies = []
        for k in range(1, N_DEV):
            px = 1 - x if (k >> 2) & 1 else x
            py = 1 - y if (k >> 1) & 1 else y
            pc = 1 - c if k & 1 else c
            peer = 4 * px + 2 * py + pc
            cp = pltpu.make_async_remote_copy(
                src_ref=src_ref.at[peer] if scatter else src_ref,
                dst_ref=out_ref.at[me],
                send_sem=send_sems.at[k - 1], recv_sem=recv_sems.at[k - 1],
                device_id=(px, py, pc), device_id_type=MESH)
            cp.start()
            copies.append(cp)
        for cp in copies:
            cp.wait()
        mine.wait()

    return _pcall(
        body, name=name, out_shape=out_shape,
        in_specs=[pl.BlockSpec(memory_space=pl.ANY)],
        out_specs=pl.BlockSpec(memory_space=pl.ANY),
        scratch_shapes=[pltpu.SemaphoreType.DMA((N_DEV - 1,)), pltpu.SemaphoreType.DMA((N_DEV - 1,)),
                        pltpu.SemaphoreType.DMA(())],
    )(src)


def _flip(x, y, c, k):
    return (1 - x if (k >> 2) & 1 else x, 1 - y if (k >> 1) & 1 else y, 1 - c if k & 1 else c)


def _dev_id(p):
    return 4 * p[0] + 2 * p[1] + p[2]


OTHER_CHIPS = (4, 2, 6)


def _gather_two_level(src, name):
    out_shape = jax.ShapeDtypeStruct((N_DEV,) + tuple(src.shape), src.dtype)

    def body(x_ref, out_ref, send_sems, recv_sems, local_sem):
        x, y, c = lax.axis_index("x"), lax.axis_index("y"), lax.axis_index("c")
        me, sibling = (x, y, c), (x, y, 1 - c)
        chips = [_flip(x, y, c, k) for k in OTHER_CHIPS]

        def copy(k, block, to, from_src=False):
            slot = out_ref.at[_dev_id(block)]
            return pltpu.make_async_remote_copy(
                src_ref=x_ref if from_src else slot, dst_ref=slot,
                send_sem=send_sems.at[k], recv_sem=recv_sems.at[k], device_id=to, device_id_type=MESH)

        mine = pltpu.make_async_copy(x_ref, out_ref.at[_dev_id(me)], local_sem)
        mine.start()
        first = [copy(0, me, sibling, True)] + [copy(1 + j, me, chip, True) for j, chip in enumerate(chips)]
        for cp in first:
            cp.start()
        passed = [copy(4 + j, chip, sibling) for j, chip in enumerate(chips)]
        for j, chip in enumerate(chips):
            copy(1 + j, chip, me).wait_recv()
            passed[j].start()
        copy(0, sibling, me).wait_recv()
        for j, chip in enumerate(chips):
            copy(4 + j, (chip[0], chip[1], 1 - c), me).wait_recv()
        for cp in first + passed:
            cp.wait_send()
        mine.wait()

    return _pcall(
        body, name=name, out_shape=out_shape,
        in_specs=[pl.BlockSpec(memory_space=pl.ANY)],
        out_specs=pl.BlockSpec(memory_space=pl.ANY),
        scratch_shapes=[pltpu.SemaphoreType.DMA((N_DEV - 1,)), pltpu.SemaphoreType.DMA((N_DEV - 1,)),
                        pltpu.SemaphoreType.DMA(())],
    )(src)


def _send_slots(src, plan, n_out, name):
    out_shape = jax.ShapeDtypeStruct((n_out,) + tuple(src.shape[1:]), src.dtype)

    def body(src_ref, out_ref, send_sems, recv_sems):
        x, y, c = lax.axis_index("x"), lax.axis_index("y"), lax.axis_index("c")
        copies = []
        for e, (k, src_slot, dst_slot) in enumerate(plan):
            cp = pltpu.make_async_remote_copy(
                src_ref=src_ref.at[src_slot(x, y, c)], dst_ref=out_ref.at[dst_slot],
                send_sem=send_sems.at[e], recv_sem=recv_sems.at[e],
                device_id=_flip(x, y, c, k), device_id_type=MESH)
            cp.start()
            copies.append(cp)
        for cp in copies:
            cp.wait()

    return _pcall(
        body, name=name, out_shape=out_shape,
        in_specs=[pl.BlockSpec(memory_space=pl.ANY)],
        out_specs=pl.BlockSpec(memory_space=pl.ANY),
        scratch_shapes=[pltpu.SemaphoreType.DMA((len(plan),)), pltpu.SemaphoreType.DMA((len(plan),))],
    )(src)


def _reduce_to_owner(send):
    x, y, c = lax.axis_index("x"), lax.axis_index("y"), lax.axis_index("c")
    plan_a = [(1, lambda x, y, c: _dev_id((x, y, 1 - c)), 0)]
    plan_a += [(1, functools.partial(lambda k, x, y, c: _dev_id(_flip(x, y, c, k | 1)), k), 1 + j)
               for j, k in enumerate(OTHER_CHIPS)]
    from_sibling = _send_slots(send, plan_a, 1 + len(OTHER_CHIPS), "reduce_d2d")
    mine = jnp.stack([lax.dynamic_index_in_dim(send, _dev_id(_flip(x, y, c, k)), 0, keepdims=False)
                      for k in OTHER_CHIPS])
    rows = mine.shape[1]
    pair = _add_rows(mine.reshape(-1, LANES), from_sibling[1:].reshape(-1, LANES), "reduce_pair_sum")
    plan_c = [(k, functools.partial(lambda j, x, y, c: j, j), j) for j, k in enumerate(OTHER_CHIPS)]
    from_chips = _send_slots(pair.reshape(len(OTHER_CHIPS), rows, LANES), plan_c, len(OTHER_CHIPS), "reduce_ici")
    own = lax.dynamic_index_in_dim(send, _dev_id((x, y, c)), 0, keepdims=True)
    return [(own, 0), (from_sibling, 0)] + [(from_chips, j) for j in range(len(OTHER_CHIPS))]


ROW_ALIGN = 16
BIG_ROW_ALIGN = 512


def _pack_rows(parts, dtype, row_align=ROW_ALIGN):
    flat = jnp.concatenate([p.reshape(-1).astype(dtype) for p in parts])
    n = flat.shape[0]
    pad = (-n) % (row_align * LANES)
    if pad:
        flat = jnp.concatenate([flat, jnp.zeros((pad,), dtype)])
    return flat.reshape(-1, LANES)


def _unpack_rows(packed, shapes):
    lead = packed.shape[:-2]
    flat = packed.reshape(lead + (-1,))
    out, off = [], 0
    for s in shapes:
        n = math.prod(s)
        out.append(flat[..., off:off + n].reshape(lead + tuple(s)))
        off += n
    return out


def _mm(a, b, dims, name, exact=False):
    if dims == 'nn':
        (m, k), n = a.shape, b.shape[1]
    elif dims == 'nt':
        (m, k), n = a.shape, b.shape[0]
    else:
        (k, m), n = a.shape, b.shape[1]
    tm = _pick(m, (1408, 1024, 512, 256, 128))
    tn = _pick(n, (512, 256, 128))
    tk = k if k <= 2816 else _pick(k, (2816, 2048, 1536, 1024, 512, 256, 128))
    nk = k // tk
    if dims == 'nn':
        a_spec = pl.BlockSpec((tm, tk), lambda i, j, kk: (i, kk))
        b_spec = pl.BlockSpec((tk, tn), lambda i, j, kk: (kk, j))
        dn = (((1,), (0,)), ((), ()))
    elif dims == 'nt':
        a_spec = pl.BlockSpec((tm, tk), lambda i, j, kk: (i, kk))
        b_spec = pl.BlockSpec((tn, tk), lambda i, j, kk: (j, kk))
        dn = (((1,), (1,)), ((), ()))
    else:
        a_spec = pl.BlockSpec((tk, tm), lambda i, j, kk: (kk, i))
        b_spec = pl.BlockSpec((tk, tn), lambda i, j, kk: (kk, j))
        dn = (((0,), (0,)), ((), ()))

    def product(a_ref, b_ref):
        if exact:
            return lax.dot_general(a_ref[...], b_ref[...], dn, precision=lax.Precision.HIGHEST,
                                   preferred_element_type=F32)
        return lax.dot_general(a_ref[...].astype(MXU_DTYPE), b_ref[...].astype(MXU_DTYPE), dn,
                               preferred_element_type=F32)

    def body(a_ref, b_ref, o_ref, *acc):
        if nk == 1:
            o_ref[...] = product(a_ref, b_ref)
            return
        acc_ref, = acc
        kk = pl.program_id(2)

        @pl.when(kk == 0)
        def _():
            acc_ref[...] = jnp.zeros_like(acc_ref)

        acc_ref[...] += product(a_ref, b_ref)

        @pl.when(kk == nk - 1)
        def _():
            o_ref[...] = acc_ref[...]

    return _pcall(
        body, name=name, out_shape=jax.ShapeDtypeStruct((m, n), F32),
        grid=(m // tm, n // tn, nk),
        in_specs=[a_spec, b_spec],
        out_specs=pl.BlockSpec((tm, tn), lambda i, j, kk: (i, j)),
        scratch_shapes=[pltpu.VMEM((tm, tn), F32)] if nk > 1 else [],
        compiler_params=_params("parallel", "parallel", "arbitrary"),
    )(a, b)


def _mm_outer(a, b, name):
    pad = LANES - a.shape[0]
    return _mm(jnp.pad(a.T, ((0, 0), (0, pad))), jnp.pad(b, ((0, pad), (0, 0))), 'nn', name, exact=True)


@functools.partial(jax.custom_vjp, nondiff_argnums=(2,))
def _linear(a, w, name):
    return _mm(a, w, 'nn', name + "_fwd")


def _linear_fwd(a, w, name):
    return _mm(a, w, 'nn', name + "_fwd"), (a, w)


def _linear_bwd(name, res, dy):
    a, w = res
    return _mm(dy, w, 'nt', name + "_dx"), _mm(a, dy, 'tn', name + "_dw")


_linear.defvjp(_linear_fwd, _linear_bwd)


SB_BK = 256
SB_STRIP = 32


def _sb_tiles(s):
    tq = _pick(s, (512, 256, 128))
    bk = min(SB_BK, tq)
    return tq, bk, tq // bk


def _sb_valid(t, sr, bk, q0, k0):
    row = lax.broadcasted_iota(jnp.int32, (sr, bk), 0) + (q0 + t * sr)
    col = lax.broadcasted_iota(jnp.int32, (sr, bk), 1) + k0
    return col < row


def _sb_tri(bk, inclusive):
    r = jnp.bitwise_and(lax.broadcasted_iota(jnp.int32, (2 * bk, bk), 0), bk - 1)
    c = lax.broadcasted_iota(jnp.int32, (2 * bk, bk), 1)
    return (r >= c).astype(BF16) if inclusive else (r > c).astype(BF16)


def _sb_split(ref, n, rows, bk, val):
    hi = val.astype(BF16)
    ref[n, rows, 0:bk] = hi
    ref[n, rows, bk:2 * bk] = (val - hi.astype(F32)).astype(BF16)


LOG2E = 1.0 / math.log(2.0)


def _sb_logits_phase(z_ref, ls_ref, hl_ref, l0_ref, n, tq, bk, sr, q0, k0, masked):
    for t in range(tq // sr):
        rows = slice(t * sr, (t + 1) * sr)
        z = z_ref[n, rows, :] * LOG2E
        ls = jnp.minimum(z, 0.0) - jnp.log(1.0 + jnp.exp2(-jnp.abs(z))) * LOG2E
        lm = ls - z
        if masked:
            lm = jnp.where(_sb_valid(t, sr, bk, q0, k0), lm, 0.0)
        ls_ref[n, rows, :] = ls
        _sb_split(hl_ref, n, rows, bk, lm)
        l0_ref[n, rows, :] = lm[:, 0:1]


def _sb_fwd_call(q, kt, v):
    h, s, dh = q.shape
    tq, bk, nt = _sb_tiles(s)
    sr = SB_STRIP

    def body(q_ref, kt_ref, v_ref, o_ref, z_ref, ls_ref, hl_ref, f_ref, a_ref, l0_ref, cl_ref, acc_ref):
        i = pl.program_id(1)
        q0 = i * tq
        cl_ref[...] = jnp.zeros_like(cl_ref)
        acc_ref[...] = jnp.zeros_like(acc_ref)
        u_excl = _sb_tri(bk, False)

        def iteration(kb0, masked):
            k0s = [pl.multiple_of(kb0 + (nt - 1 - n) * bk, bk) for n in range(nt)]
            for n in range(nt):
                z_ref[n] = jnp.dot(q_ref[...], kt_ref[:, pl.ds(k0s[n], bk)], preferred_element_type=F32)
            for n in range(nt):
                _sb_logits_phase(z_ref, ls_ref, hl_ref, l0_ref, n, tq, bk, sr, q0, k0s[n], masked)
            for n in range(nt):
                f_ref[n] = jnp.dot(hl_ref[n], u_excl, preferred_element_type=F32)
            for t in range(tq // sr):
                rows = slice(t * sr, (t + 1) * sr)
                c = cl_ref[rows, :]
                for n in range(nt):
                    f = f_ref[n, rows, :]
                    a = jnp.exp2(ls_ref[n, rows, :] + f + c)
                    if masked:
                        a = jnp.where(_sb_valid(t, sr, bk, q0, k0s[n]), a, 0.0)
                    a_ref[n, rows, :] = a.astype(a_ref.dtype)
                    c = c + f[:, 0:1] + l0_ref[n, rows, :]
                cl_ref[rows, :] = c
            for n in range(nt):
                acc_ref[...] += jnp.dot(a_ref[n], v_ref[pl.ds(k0s[n], bk), :], preferred_element_type=F32)

        def below(jj, c):
            iteration((i - 1 - jj) * tq, False)
            return c

        iteration(q0, True)
        lax.fori_loop(0, i, below, 0)
        o_ref[...] = acc_ref[...]

    return _pcall(
        body, name="sb_fwd", out_shape=jax.ShapeDtypeStruct((h, s, dh), F32),
        grid=(h, s // tq),
        in_specs=[pl.BlockSpec((None, tq, dh), lambda hh, i: (hh, i, 0)),
                  pl.BlockSpec((None, dh, s), lambda hh, i: (hh, 0, 0)),
                  pl.BlockSpec((None, s, dh), lambda hh, i: (hh, 0, 0))],
        out_specs=pl.BlockSpec((None, tq, dh), lambda hh, i: (hh, i, 0)),
        scratch_shapes=[pltpu.VMEM((nt, tq, bk), F32), pltpu.VMEM((nt, tq, bk), F32),
                        pltpu.VMEM((nt, tq, 2 * bk), BF16), pltpu.VMEM((nt, tq, bk), F32),
                        pltpu.VMEM((nt, tq, bk), q.dtype), pltpu.VMEM((nt, tq, 1), F32),
                        pltpu.VMEM((tq, 1), F32), pltpu.VMEM((tq, dh), F32)],
        compiler_params=_params("parallel", "arbitrary"),
    )(q, kt, v)


def _sb_bwd_call(q, qt, k, kt, vt, o, do, dot):
    h, s, dh = q.shape
    tq, bk, nt = _sb_tiles(s)
    sr = SB_STRIP

    def body(q_ref, qt_ref, k_ref, kt_ref, vt_ref, o_ref, do_ref, dot_ref, dq_ref, dkt_ref, dvt_ref,
             z_ref, ls_ref, hl_ref, f_ref, a_ref, g_ref, dz_ref, da_ref, l0_ref, dob_ref,
             cl_ref, cg_ref, dl_ref, dqa_ref):
        i = pl.program_id(1)
        q0 = i * tq

        @pl.when(i == 0)
        def _():
            dkt_ref[...] = jnp.zeros_like(dkt_ref)
            dvt_ref[...] = jnp.zeros_like(dvt_ref)

        cl_ref[...] = jnp.zeros_like(cl_ref)
        cg_ref[...] = jnp.zeros_like(cg_ref)
        dqa_ref[...] = jnp.zeros_like(dqa_ref)
        dob = do_ref[...].astype(dob_ref.dtype)
        dob_ref[...] = dob
        dl_ref[...] = jnp.sum(dob.astype(F32) * o_ref[...], axis=1, keepdims=True)
        u_excl = _sb_tri(bk, False)
        u_incl = _sb_tri(bk, True)

        def iteration(kb0, masked):
            k0s = [pl.multiple_of(kb0 + (nt - 1 - n) * bk, bk) for n in range(nt)]
            for n in range(nt):
                z_ref[n] = jnp.dot(q_ref[...], kt_ref[:, pl.ds(k0s[n], bk)], preferred_element_type=F32)
                da_ref[n] = jnp.dot(dob_ref[...], vt_ref[:, pl.ds(k0s[n], bk)], preferred_element_type=F32)
            for n in range(nt):
                _sb_logits_phase(z_ref, ls_ref, hl_ref, l0_ref, n, tq, bk, sr, q0, k0s[n], masked)
            for n in range(nt):
                f_ref[n] = jnp.dot(hl_ref[n], u_excl, preferred_element_type=F32)
            for t in range(tq // sr):
                rows = slice(t * sr, (t + 1) * sr)
                c = cl_ref[rows, :]
                for n in range(nt):
                    f = f_ref[n, rows, :]
                    a = jnp.exp2(ls_ref[n, rows, :] + f + c)
                    if masked:
                        a = jnp.where(_sb_valid(t, sr, bk, q0, k0s[n]), a, 0.0)
                    ab = a.astype(a_ref.dtype)
                    a_ref[n, rows, :] = ab
                    g = da_ref[n, rows, :] * ab.astype(F32)
                    g_ref[n, rows, :] = g
                    _sb_split(hl_ref, n, rows, bk, g)
                    c = c + f[:, 0:1] + l0_ref[n, rows, :]
                cl_ref[rows, :] = c
            for n in range(nt):
                f_ref[n] = jnp.dot(hl_ref[n], u_incl, preferred_element_type=F32)
            for t in range(tq // sr):
                rows = slice(t * sr, (t + 1) * sr)
                cg = cg_ref[rows, :]
                for n in range(nt):
                    sg_tile = f_ref[n, rows, :]
                    p = dl_ref[rows, :] - (sg_tile + cg)
                    g = g_ref[n, rows, :]
                    dz = g - (g + p) * jnp.exp2(ls_ref[n, rows, :])
                    if masked:
                        dz = jnp.where(_sb_valid(t, sr, bk, q0, k0s[n]), dz, 0.0)
                    dz_ref[n, rows, :] = dz.astype(dz_ref.dtype)
                    cg = cg + sg_tile[:, 0:1]
                cg_ref[rows, :] = cg
            for n in range(nt):
                cols = pl.ds(k0s[n], bk)
                dqa_ref[...] += jnp.dot(dz_ref[n], k_ref[cols, :], preferred_element_type=F32)
                dkt_ref[:, cols] += jnp.dot(qt_ref[...], dz_ref[n], preferred_element_type=F32)
                dvt_ref[:, cols] += jnp.dot(dot_ref[...], a_ref[n], preferred_element_type=F32)

        def below(jj, c):
            iteration((i - 1 - jj) * tq, False)
            return c

        iteration(q0, True)
        lax.fori_loop(0, i, below, 0)
        dq_ref[...] = dqa_ref[...]

    blk_q = pl.BlockSpec((None, tq, dh), lambda hh, i: (hh, i, 0))
    blk_qt = pl.BlockSpec((None, dh, tq), lambda hh, i: (hh, 0, i))
    blk_s = pl.BlockSpec((None, s, dh), lambda hh, i: (hh, 0, 0))
    blk_st = pl.BlockSpec((None, dh, s), lambda hh, i: (hh, 0, 0))
    mx = q.dtype
    return _pcall(
        body, name="sb_bwd",
        out_shape=(jax.ShapeDtypeStruct((h, s, dh), F32), jax.ShapeDtypeStruct((h, dh, s), F32),
                   jax.ShapeDtypeStruct((h, dh, s), F32)),
        grid=(h, s // tq),
        in_specs=[blk_q, blk_qt, blk_s, blk_st, blk_st, blk_q, blk_q, blk_qt],
        out_specs=(blk_q, blk_st, blk_st),
        scratch_shapes=[pltpu.VMEM((nt, tq, bk), F32), pltpu.VMEM((nt, tq, bk), F32),
                        pltpu.VMEM((nt, tq, 2 * bk), BF16), pltpu.VMEM((nt, tq, bk), F32),
                        pltpu.VMEM((nt, tq, bk), mx), pltpu.VMEM((nt, tq, bk), F32),
                        pltpu.VMEM((nt, tq, bk), mx), pltpu.VMEM((nt, tq, bk), F32),
                        pltpu.VMEM((nt, tq, 1), F32), pltpu.VMEM((tq, dh), mx),
                        pltpu.VMEM((tq, 1), F32), pltpu.VMEM((tq, 1), F32), pltpu.VMEM((tq, 1), F32),
                        pltpu.VMEM((tq, dh), F32)],
        compiler_params=_params("parallel", "arbitrary"),
    )(q, qt, k, kt, vt, o, do, dot)


def _swap(t):
    return t.transpose(0, 2, 1)


def _sb_scale(dh):
    assert math.log2(dh) % 2 == 0, dh
    return dh ** -0.5


@jax.custom_vjp
def _sb_core(q, k, v):
    return _sb_core_fwd(q, k, v)[0]


def _sb_core_fwd(q, k, v):
    scale = _sb_scale(q.shape[-1])
    qs, kb, vb = (q.astype(MXU_DTYPE) * scale).astype(MXU_DTYPE), k.astype(MXU_DTYPE), v.astype(MXU_DTYPE)
    o = _sb_fwd_call(qs, _swap(kb), vb)
    return o, (qs, kb, vb, o)


def _sb_core_bwd(res, do):
    qs, kb, vb, o = res
    ks = (kb * _sb_scale(kb.shape[-1])).astype(MXU_DTYPE)
    dq, dkt, dvt = _sb_bwd_call(qs, _swap(qs), ks, _swap(kb), _swap(vb), o, do, _swap(do.astype(MXU_DTYPE)))
    return dq, _swap(dkt), _swap(dvt)


_sb_core.defvjp(_sb_core_fwd, _sb_core_bwd)


def _row_block(s):
    return _pick(s, (512, 256, 128, 64, 32, 16, 8))


def _fold8(t):
    r, c = t.shape
    return jnp.sum(t.reshape(r // SUBLANES, SUBLANES, c), axis=0)


def _vec(a):
    return a.reshape(1, -1)


def _modulate(x, sc, sh, out_dtype, name):
    s, d = x.shape
    tr = _row_block(s)

    def body(x_ref, sc_ref, sh_ref, o_ref):
        o_ref[...] = (x_ref[...] * (1.0 + sc_ref[...]) + sh_ref[...]).astype(o_ref.dtype)

    row = pl.BlockSpec((tr, d), lambda i: (i, 0))
    vec = pl.BlockSpec((1, d), lambda i: (0, 0))
    return _pcall(body, name=name, out_shape=jax.ShapeDtypeStruct((s, d), out_dtype), grid=(s // tr,),
                  in_specs=[row, vec, vec], out_specs=row, compiler_params=_params("parallel"))(x, _vec(sc), _vec(sh))


def _modulate_bwd(dxa, dh, x, sc, name):
    s, d = x.shape
    tr = _row_block(s)

    def body(dxa_ref, dh_ref, x_ref, sc_ref, dx_ref, acc_ref):
        @pl.when(pl.program_id(0) == 0)
        def _():
            acc_ref[...] = jnp.zeros_like(acc_ref)

        dh = dh_ref[...]
        dx_ref[...] = dxa_ref[...] + dh * (1.0 + sc_ref[...])
        acc_ref[0] += _fold8(dh * x_ref[...])
        acc_ref[1] += _fold8(dh)

    row = pl.BlockSpec((tr, d), lambda i: (i, 0))
    vec = pl.BlockSpec((1, d), lambda i: (0, 0))
    dx, acc = _pcall(
        body, name=name,
        out_shape=(jax.ShapeDtypeStruct((s, d), F32), jax.ShapeDtypeStruct((2, SUBLANES, d), F32)),
        grid=(s // tr,), in_specs=[row, row, row, vec],
        out_specs=(row, pl.BlockSpec((2, SUBLANES, d), lambda i: (0, 0, 0))),
        compiler_params=_params("arbitrary"))(dxa, dh, x, _vec(sc))
    acc = jnp.sum(acc, axis=1)
    return dx, acc[0], acc[1]


def _resid_ln(x, y, g, gamma, beta, name):
    s, d = x.shape
    tr = _row_block(s)

    def body(x_ref, y_ref, g_ref, gam_ref, bet_ref, o_ref):
        u = DN_ALPHA * x_ref[...] + (1.0 + g_ref[...]) * y_ref[...]
        uc = u - jnp.mean(u, axis=-1, keepdims=True)
        var = jnp.mean(uc * uc, axis=-1, keepdims=True)
        o_ref[...] = uc * lax.rsqrt(var + LN_EPS) * gam_ref[...] + bet_ref[...]

    row = pl.BlockSpec((tr, d), lambda i: (i, 0))
    vec = pl.BlockSpec((1, d), lambda i: (0, 0))
    return _pcall(body, name=name, out_shape=jax.ShapeDtypeStruct((s, d), F32), grid=(s // tr,),
                  in_specs=[row, row, vec, vec, vec], out_specs=row,
                  compiler_params=_params("parallel"))(x, y, _vec(g), _vec(gamma), _vec(beta))


def _resid_ln_bwd(x, y, g, gamma, dout, name):
    s, d = x.shape
    tr = _row_block(s)

    def body(x_ref, y_ref, g_ref, gam_ref, do_ref, dxa_ref, dy_ref, acc_ref):
        @pl.when(pl.program_id(0) == 0)
        def _():
            acc_ref[...] = jnp.zeros_like(acc_ref)

        y = y_ref[...]
        gg = 1.0 + g_ref[...]
        u = DN_ALPHA * x_ref[...] + gg * y
        uc = u - jnp.mean(u, axis=-1, keepdims=True)
        rstd = lax.rsqrt(jnp.mean(uc * uc, axis=-1, keepdims=True) + LN_EPS)
        xhat = uc * rstd
        dout = do_ref[...]
        dxh = dout * gam_ref[...]
        du = rstd * (dxh - jnp.mean(dxh, axis=-1, keepdims=True)
                     - xhat * jnp.mean(dxh * xhat, axis=-1, keepdims=True))
        dxa_ref[...] = DN_ALPHA * du
        dy_ref[...] = gg * du
        acc_ref[0] += _fold8(dout * xhat)
        acc_ref[1] += _fold8(dout)
        acc_ref[2] += _fold8(du * y)

    row = pl.BlockSpec((tr, d), lambda i: (i, 0))
    vec = pl.BlockSpec((1, d), lambda i: (0, 0))
    dxa, dy, acc = _pcall(
        body, name=name,
        out_shape=(jax.ShapeDtypeStruct((s, d), F32), jax.ShapeDtypeStruct((s, d), F32),
                   jax.ShapeDtypeStruct((3, SUBLANES, d), F32)),
        grid=(s // tr,), in_specs=[row, row, vec, vec, row],
        out_specs=(row, row, pl.BlockSpec((3, SUBLANES, d), lambda i: (0, 0, 0))),
        compiler_params=_params("arbitrary"))(x, y, _vec(g), _vec(gamma), dout)
    acc = jnp.sum(acc, axis=1)
    return dxa, dy, acc[0], acc[1], acc[2]


def _loss_head(x, target, name):
    s, d = x.shape
    tr = _row_block(s)

    def body(x_ref, t_ref, dx_ref, acc_ref):
        @pl.when(pl.program_id(0) == 0)
        def _():
            acc_ref[...] = jnp.zeros_like(acc_ref)

        e = x_ref[...] - t_ref[...]
        dx_ref[...] = e * (1.0 / d)
        acc_ref[...] += _fold8(e * e)

    row = pl.BlockSpec((tr, d), lambda i: (i, 0))
    dx, acc = _pcall(
        body, name=name,
        out_shape=(jax.ShapeDtypeStruct((s, d), F32), jax.ShapeDtypeStruct((SUBLANES, d), F32)),
        grid=(s // tr,), in_specs=[row, row],
        out_specs=(row, pl.BlockSpec((SUBLANES, d), lambda i: (0, 0))),
        compiler_params=_params("arbitrary"))(x, target)
    return (0.5 / d) * jnp.sum(acc), dx


CONV_STRIPE = 128
CONV_ROWS = 256


def _shift_down(cur, halo, k):
    ext = jnp.concatenate([halo, cur], axis=0)
    return pltpu.roll(ext, k, 0)[SUBLANES:]


def _shift_up(cur, halo, k):
    ext = jnp.concatenate([cur, halo], axis=0)
    n = ext.shape[0]
    return pltpu.roll(ext, n - k, 0)[:n - SUBLANES]


def _gate_chunk(g_ref, r, rc):
    r0 = pl.multiple_of(r * rc, rc)
    cur = g_ref[pl.ds(r0, rc), :]
    hs = pl.multiple_of(jnp.maximum(r0 - SUBLANES, 0), SUBLANES)
    halo = jnp.where(r > 0, g_ref[pl.ds(hs, SUBLANES), :], 0.0)
    return r0, cur, _shift_down(cur, halo, 1), _shift_down(cur, halo, 2)


def _ffn_gate(gu, cw, cb, name):
    s, f2 = gu.shape
    f = f2 // 2
    tc = _pick(f, (CONV_STRIPE,))
    rc = _pick(s, (CONV_ROWS, 128, 64, 32, 16, 8))
    nj = f // tc

    def body(g_ref, u_ref, cw_ref, cb_ref, a_ref):
        w0, w1, w2, b = cw_ref[0:1, :], cw_ref[1:2, :], cw_ref[2:3, :], cb_ref[...]

        def chunk(r, c):
            r0, cur, x1, x2 = _gate_chunk(g_ref, r, rc)
            gc = w2 * cur + w1 * x1 + w0 * x2 + b
            a_ref[pl.ds(r0, rc), :] = (gc * jax.nn.sigmoid(gc) * u_ref[pl.ds(r0, rc), :]).astype(a_ref.dtype)
            return c

        lax.fori_loop(0, s // rc, chunk, 0)

    return _pcall(
        body, name=name, out_shape=jax.ShapeDtypeStruct((s, f), MXU_DTYPE), grid=(nj,),
        in_specs=[pl.BlockSpec((s, tc), lambda j: (0, j)), pl.BlockSpec((s, tc), lambda j: (0, j + nj)),
                  pl.BlockSpec((3, tc), lambda j: (0, j)), pl.BlockSpec((1, tc), lambda j: (0, j))],
        out_specs=pl.BlockSpec((s, tc), lambda j: (0, j)),
        compiler_params=_params("parallel"))(gu, gu, cw, _vec(cb))


def _ffn_gate_bwd(da, gu, cw, cb, name):
    s, f2 = gu.shape
    f = f2 // 2
    tc = _pick(f, (CONV_STRIPE,))
    rc = _pick(s, (CONV_ROWS, 128, 64, 32, 16, 8))
    nj = f // tc
    nr = s // rc

    def body(da_ref, g_ref, u_ref, cw_ref, cb_ref, dg_ref, du_ref, acc_ref, dgc_ref):
        w0, w1, w2, b = cw_ref[0:1, :], cw_ref[1:2, :], cw_ref[2:3, :], cb_ref[...]

        def chunk1(r, carry):
            a0, a1, a2, ab = carry
            r0, cur, x1, x2 = _gate_chunk(g_ref, r, rc)
            gc = w2 * cur + w1 * x1 + w0 * x2 + b
            sg = jax.nn.sigmoid(gc)
            da_c = da_ref[pl.ds(r0, rc), :]
            du_ref[pl.ds(r0, rc), :] = (da_c * (gc * sg)).astype(du_ref.dtype)
            dgc = da_c * u_ref[pl.ds(r0, rc), :] * (sg * (1.0 + gc * (1.0 - sg)))
            dgc_ref[pl.ds(r0, rc), :] = dgc
            return a0 + _fold8(dgc * x2), a1 + _fold8(dgc * x1), a2 + _fold8(dgc * cur), ab + _fold8(dgc)

        zero = jnp.zeros((SUBLANES, tc), F32)
        a0, a1, a2, ab = lax.fori_loop(0, nr, chunk1, (zero, zero, zero, zero))
        acc_ref[0], acc_ref[1], acc_ref[2], acc_ref[3] = a0, a1, a2, ab

        def chunk2(r, c):
            r0 = pl.multiple_of(r * rc, rc)
            cur = dgc_ref[pl.ds(r0, rc), :]
            hs = pl.multiple_of(jnp.minimum(r0 + rc, s - SUBLANES), SUBLANES)
            halo = jnp.where(r < nr - 1, dgc_ref[pl.ds(hs, SUBLANES), :], 0.0)
            dg = w2 * cur + w1 * _shift_up(cur, halo, 1) + w0 * _shift_up(cur, halo, 2)
            dg_ref[pl.ds(r0, rc), :] = dg.astype(dg_ref.dtype)
            return c

        lax.fori_loop(0, nr, chunk2, 0)

    stripe = pl.BlockSpec((s, tc), lambda j: (0, j))
    dg, du, acc = _pcall(
        body, name=name,
        out_shape=(jax.ShapeDtypeStruct((s, f), MXU_DTYPE), jax.ShapeDtypeStruct((s, f), MXU_DTYPE),
                   jax.ShapeDtypeStruct((4, SUBLANES, f), F32)),
        grid=(nj,),
        in_specs=[stripe, stripe, pl.BlockSpec((s, tc), lambda j: (0, j + nj)),
                  pl.BlockSpec((3, tc), lambda j: (0, j)), pl.BlockSpec((1, tc), lambda j: (0, j))],
        out_specs=(stripe, stripe, pl.BlockSpec((4, SUBLANES, tc), lambda j: (0, 0, j))),
        scratch_shapes=[pltpu.VMEM((s, tc), F32)],
        compiler_params=_params("parallel"))(da, gu, gu, cw, _vec(cb))
    acc = jnp.sum(acc, axis=1)
    return dg, du, acc[:3], acc[3]


def _add_rows(a, b, name):
    r = a.shape[0]
    tr = _pick(r, (1024, 512, 256, 128, 64, 32, 16, 8))

    def body(a_ref, b_ref, o_ref):
        o_ref[...] = a_ref[...] + b_ref[...]

    row = pl.BlockSpec((tr, LANES), lambda i: (i, 0))
    return _pcall(body, name=name, out_shape=jax.ShapeDtypeStruct(a.shape, a.dtype), grid=(r // tr,),
                  in_specs=[row, row], out_specs=row, compiler_params=_params("parallel"))(a, b)


def _adamw(gparts, w, m, v, name):
    r = w.shape[0]
    tr = _pick(r, (1024, 512, 256, 128, 64, 32, 16, 8))
    bc1 = 1.0 / (1.0 - ADAM_B1 ** ADAM_STEP)
    bc2 = 1.0 / (1.0 - ADAM_B2 ** ADAM_STEP)
    n = len(gparts)

    def body(*refs):
        w_ref, m_ref, v_ref, go_ref, d_ref, mo_ref, vo_ref = refs[n:]
        g = refs[0][...]
        for t in range(1, n):
            g = g + refs[t][...]
        mn = ADAM_B1 * m_ref[...] + (1.0 - ADAM_B1) * g
        vn = ADAM_B2 * v_ref[...] + (1.0 - ADAM_B2) * (g * g)
        m_hat = mn * bc1
        v_hat = vn * bc2
        go_ref[...] = g
        d_ref[...] = -ADAM_LR * (m_hat / (jnp.sqrt(v_hat) + ADAM_EPS) + ADAM_WD * w_ref[...])
        mo_ref[...] = mn
        vo_ref[...] = vn

    row = pl.BlockSpec((tr, LANES), lambda i: (i, 0))
    sds = jax.ShapeDtypeStruct((r, LANES), F32)
    return _pcall(
        body, name=name, out_shape=(sds, sds, sds, sds),
        grid=(r // tr,),
        in_specs=[pl.BlockSpec((None, tr, LANES), functools.partial(lambda slot, i: (slot, i, 0), slot))
                  for _, slot in gparts] + [row, row, row],
        out_specs=(row, row, row, row),
        compiler_params=_params("parallel"),
    )(*[a for a, _ in gparts], w, m, v)


def _sum_slots(gslots, name):
    n, r, _ = gslots.shape
    tr = _pick(r, (1024, 512, 256, 128, 64, 32, 16, 8))

    def body(g_ref, o_ref):
        g = g_ref[0]
        for t in range(1, n):
            g = g + g_ref[t]
        o_ref[...] = g

    return _pcall(
        body, name=name, out_shape=jax.ShapeDtypeStruct((r, LANES), F32),
        grid=(r // tr,),
        in_specs=[pl.BlockSpec((n, tr, LANES), lambda i: (0, i, 0))],
        out_specs=pl.BlockSpec((tr, LANES), lambda i: (i, 0)),
        compiler_params=_params("parallel"),
    )(gslots)


def _standardize(x, eps):
    mu = jnp.mean(x, axis=-1, keepdims=True)
    xc = x - mu
    var = jnp.mean(xc * xc, axis=-1, keepdims=True)
    return xc * lax.rsqrt(var + eps)


def _layer_norm(x, g, b):
    return _standardize(x, LN_EPS) * g + b


def _l2norm(x, eps=1e-6):
    return x * lax.rsqrt(jnp.sum(x * x, axis=-1, keepdims=True) + eps)


def _causal_dwconv(x, w):
    k_w, s = w.shape[0], x.shape[0]
    xp = jnp.pad(x, ((k_w - 1, 0), (0, 0)))
    y = xp[k_w - 1:k_w - 1 + s] * w[k_w - 1]
    for j in range(k_w - 1):
        y = y + xp[j:j + s] * w[j]
    return y


def _chunk_heads(t, n_heads, chunk):
    s, hd = t.shape
    return t.reshape(s // chunk, chunk, n_heads, hd // n_heads).transpose(2, 0, 1, 3)


def _unchunk_heads(t):
    h, n, c, d = t.shape
    return t.transpose(1, 2, 0, 3).reshape(n * c, h, d)


_NT = (((1,), (1,)), ((), ()))
_TN = (((0,), (0,)), ((), ()))


def _gdn_blocks(a, rev_from=None):
    h, _, r, c = a.shape
    if rev_from is None:
        return pl.BlockSpec((h, None, r, c), lambda n: (0, n, 0, 0))
    return pl.BlockSpec((h, None, r, c), lambda n: (0, rev_from - n, 0, 0))


def _gdn_scan_fwd_call(qg, w, u, qk, kd, e):
    H, n_chunks, c, dk = qg.shape
    dv = u.shape[-1]

    def body(qg_ref, w_ref, u_ref, qk_ref, kd_ref, e_ref, o_ref, sin_ref, vn_ref, state_ref):
        @pl.when(pl.program_id(0) == 0)
        def _():
            state_ref[...] = jnp.zeros_like(state_ref)

        for hd in range(H):
            st = state_ref[hd]
            stb = st.astype(MXU_DTYPE)
            sin_ref[hd] = st
            v_new = u_ref[hd] - jnp.dot(w_ref[hd].astype(MXU_DTYPE), stb, preferred_element_type=F32)
            vn_ref[hd] = v_new
            vnb = v_new.astype(MXU_DTYPE)
            o_ref[hd] = (jnp.dot(qg_ref[hd].astype(MXU_DTYPE), stb, preferred_element_type=F32)
                         + jnp.dot(qk_ref[hd].astype(MXU_DTYPE), vnb, preferred_element_type=F32))
            state_ref[hd] = st * e_ref[hd] + lax.dot_general(kd_ref[hd].astype(MXU_DTYPE), vnb, _TN,
                                                             preferred_element_type=F32)

    ins = (qg, w, u, qk, kd, e)
    outs = (jax.ShapeDtypeStruct((H, n_chunks, c, dv), F32), jax.ShapeDtypeStruct((H, n_chunks, dk, dv), F32),
            jax.ShapeDtypeStruct((H, n_chunks, c, dv), F32))
    return _pcall(
        body, name="gdn_scan_fwd", out_shape=outs, grid=(n_chunks,),
        in_specs=[_gdn_blocks(a) for a in ins], out_specs=tuple(_gdn_blocks(a) for a in outs),
        scratch_shapes=[pltpu.VMEM((H, dk, dv), F32)],
        compiler_params=_params("arbitrary"),
    )(*ins)


def _gdn_scan_bwd_call(qg, w, qk, kd, e, s_in, v_new, do):
    H, n_chunks, c, dk = qg.shape
    dv = v_new.shape[-1]

    def body(qg_ref, w_ref, qk_ref, kd_ref, e_ref, sin_ref, vn_ref, do_ref,
             dqg_ref, dw_ref, du_ref, dqk_ref, dkd_ref, de_ref, ds_ref):
        @pl.when(pl.program_id(0) == 0)
        def _():
            ds_ref[...] = jnp.zeros_like(ds_ref)

        for hd in range(H):
            st = sin_ref[hd]
            stb = st.astype(MXU_DTYPE)
            vnb = vn_ref[hd].astype(MXU_DTYPE)
            dob = do_ref[hd].astype(MXU_DTYPE)
            ds = ds_ref[hd]
            dsb = ds.astype(MXU_DTYPE)
            dvn = lax.dot_general(qk_ref[hd].astype(MXU_DTYPE), dob, _TN, preferred_element_type=F32)
            dqk_ref[hd] = lax.dot_general(dob, vnb, _NT, preferred_element_type=F32)
            dqg_ref[hd] = lax.dot_general(dob, stb, _NT, preferred_element_type=F32)
            ds_in = lax.dot_general(qg_ref[hd].astype(MXU_DTYPE), dob, _TN, preferred_element_type=F32)
            dvn = dvn + jnp.dot(kd_ref[hd].astype(MXU_DTYPE), dsb, preferred_element_type=F32)
            dkd_ref[hd] = lax.dot_general(vnb, dsb, _NT, preferred_element_type=F32)
            de_ref[hd] = _fold8(st * ds)
            ds_in = ds_in + ds * e_ref[hd]
            du_ref[hd] = dvn
            dvnb = dvn.astype(MXU_DTYPE)
            dw_ref[hd] = -lax.dot_general(dvnb, stb, _NT, preferred_element_type=F32)
            ds_ref[hd] = ds_in - lax.dot_general(w_ref[hd].astype(MXU_DTYPE), dvnb, _TN,
                                                 preferred_element_type=F32)

    last = n_chunks - 1
    ins = (qg, w, qk, kd, e, s_in, v_new, do)
    outs = (jax.ShapeDtypeStruct(qg.shape, F32), jax.ShapeDtypeStruct(w.shape, F32),
            jax.ShapeDtypeStruct(v_new.shape, F32), jax.ShapeDtypeStruct(qk.shape, F32),
            jax.ShapeDtypeStruct(kd.shape, F32), jax.ShapeDtypeStruct((H, n_chunks, SUBLANES, dv), F32))
    return _pcall(
        body, name="gdn_scan_bwd", out_shape=outs, grid=(n_chunks,),
        in_specs=[_gdn_blocks(a, last) for a in ins], out_specs=tuple(_gdn_blocks(a, last) for a in outs),
        scratch_shapes=[pltpu.VMEM((H, dk, dv), F32)],
        compiler_params=_params("arbitrary"),
    )(*ins)


@jax.custom_vjp
def _gdn_scan(qg, w, u, qk, kd, e):
    return _gdn_scan_fwd_call(qg, w, u, qk, kd, e)[0]


def _gdn_scan_fwd(qg, w, u, qk, kd, e):
    o, s_in, v_new = _gdn_scan_fwd_call(qg, w, u, qk, kd, e)
    return o, (qg, w, qk, kd, e, s_in, v_new)


def _gdn_scan_bwd(res, do):
    dqg, dw, du, dqk, dkd, de = _gdn_scan_bwd_call(*res, do)
    return dqg, dw, du, dqk, dkd, jnp.sum(de, axis=2, keepdims=True)


_gdn_scan.defvjp(_gdn_scan_fwd, _gdn_scan_bwd)


def _gated_deltanet(hx, p):
    H, C = GDN_HEADS, GDN_CHUNK
    s, d = hx.shape
    dk = dv = d // H
    qkvz = _linear(hx, p['gdn_w_qkvz'], "gdn_in")
    ab = _linear(hx, p['gdn_w_ab'], "gdn_ab")
    qkv, z = qkvz[:, :3 * d], qkvz[:, 3 * d:]
    a, bt = ab[:, :H], ab[:, H:2 * H]
    qkv = jax.nn.silu(_causal_dwconv(qkv, p['gdn_conv_w']))
    q, k, v = qkv[:, :d], qkv[:, d:2 * d], qkv[:, 2 * d:]
    q = _l2norm(_chunk_heads(q, H, C)) * (dk ** -0.5)
    k = _l2norm(_chunk_heads(k, H, C))
    v = _chunk_heads(v, H, C)
    beta = jax.nn.sigmoid(_chunk_heads(bt, H, C)[..., 0])
    g = -jnp.exp(p['gdn_a_log'])[:, None, None] * jax.nn.softplus(
        _chunk_heads(a, H, C)[..., 0] + p['gdn_dt_bias'][:, None, None])
    gc = jnp.cumsum(g, axis=-1)
    idx = jnp.arange(C)
    causal = idx[:, None] >= idx[None, :]
    strict = idx[:, None] > idx[None, :]
    diff = gc[..., :, None] - gc[..., None, :]
    decay = jnp.where(causal, jnp.exp(jnp.where(causal, diff, 0.0)), 0.0)
    kb = k * beta[..., None]
    kk = jnp.where(strict, jnp.einsum('hncd,hnmd->hncm', kb, k) * decay, 0.0)
    eye = jnp.eye(C, dtype=F32)
    rhs = jnp.concatenate([v * beta[..., None], kb * jnp.exp(gc)[..., None]], axis=-1)
    sol = lax.linalg.triangular_solve(kk + eye, rhs, left_side=True, lower=True, unit_diagonal=True)
    u, w = sol[..., :dv], sol[..., dv:]
    qk = jnp.where(causal, jnp.einsum('hncd,hnmd->hncm', q, k) * decay, 0.0)

    g_last = gc[..., -1:]
    e = jnp.broadcast_to(jnp.exp(g_last)[..., None], gc.shape[:2] + (1, dv))
    o = _gdn_scan(q * jnp.exp(gc)[..., None], w, u, qk, k * jnp.exp(g_last - gc)[..., None], e)
    o = _unchunk_heads(o)
    o = o * lax.rsqrt(jnp.mean(o * o, axis=-1, keepdims=True) + 1e-6) * p['gdn_norm_w']
    o = o * jax.nn.silu(z.reshape(s, H, dv))
    return _linear(o.reshape(s, H * dv), p['gdn_w_out'], "gdn_out")


def _ret_consts(c):
    log_gamma = jnp.log(1.0 - jnp.power(2.0, -5.0 - jnp.arange(RET_HEADS, dtype=F32)))
    idx = jnp.arange(c, dtype=F32)
    rel = idx[:, None] - idx[None, :]
    dmask = jnp.where(rel >= 0, jnp.exp(jnp.maximum(rel, 0.0) * log_gamma[:, None, None]), 0.0)
    zeta = jnp.exp((c - 1.0 - idx)[None, :] * log_gamma[:, None])[..., None]
    xi = jnp.exp((idx + 1.0)[None, :] * log_gamma[:, None])[..., None]
    gamma_c = jnp.exp(c * log_gamma)[:, None, None]
    return dmask, zeta, xi, gamma_c


def _ret_angles(s, dk):
    pos = jnp.arange(s, dtype=F32)
    inv_freq = RET_ROPE_BASE ** (-jnp.linspace(0.0, 1.0, dk // 2, dtype=F32))
    ang = pos[:, None] * inv_freq[None, :]
    return jnp.cos(ang), jnp.sin(ang)


def _rot(t, cs, sn):
    half = t.shape[1] // 2
    t1, t2 = t[:, :half], t[:, half:]
    return jnp.concatenate([t1 * cs - t2 * sn, t1 * sn + t2 * cs], axis=1)


def _rot_t(t, cs, sn):
    half = t.shape[1] // 2
    t1, t2 = t[:, :half], t[:, half:]
    return jnp.concatenate([t1 * cs + t2 * sn, t2 * cs - t1 * sn], axis=1)


def _ret_cols(d, dk, dv, hd):
    return (slice(hd * dk, (hd + 1) * dk), slice(d + hd * dk, d + (hd + 1) * dk),
            slice(2 * d + hd * dv, 2 * d + (hd + 1) * dv), slice(4 * d + hd * dv, 4 * d + (hd + 1) * dv))


def _ret_fwd_call(proj):
    s, d6 = proj.shape
    d = d6 // 6
    H, c = RET_HEADS, RET_CHUNK
    dk, dv = d // H, 2 * d // H
    n_chunks = s // c
    kscale = dk ** -0.5
    cos_a, sin_a = _ret_angles(s, dk)
    consts = _ret_consts(c)

    def body(p_ref, cos_ref, sin_ref, dm_ref, ze_ref, xi_ref, gc_ref, out_ref, oraw_ref, st_ref, state_ref):
        @pl.when(pl.program_id(0) == 0)
        def _():
            state_ref[...] = jnp.zeros_like(state_ref)

        cs, sn = cos_ref[...], sin_ref[...]
        for hd in range(H):
            qc, kc, vc, gcol = _ret_cols(d, dk, dv, hd)
            ocol = slice(hd * dv, (hd + 1) * dv)
            qb = _rot(p_ref[:, qc], cs, sn).astype(MXU_DTYPE)
            kr = _rot(p_ref[:, kc], cs, sn) * kscale
            kb = kr.astype(MXU_DTYPE)
            vb = p_ref[:, vc].astype(MXU_DTYPE)
            st = state_ref[hd]
            stb = st.astype(MXU_DTYPE)
            st_ref[hd] = stb
            sc = lax.dot_general(qb, kb, _NT, preferred_element_type=F32) * dm_ref[hd]
            o = (jnp.dot(sc.astype(MXU_DTYPE), vb, preferred_element_type=F32)
                 + jnp.dot(qb, stb, preferred_element_type=F32) * xi_ref[hd])
            state_ref[hd] = st * gc_ref[hd] + lax.dot_general((kr * ze_ref[hd]).astype(MXU_DTYPE), vb, _TN,
                                                              preferred_element_type=F32)
            oraw_ref[:, ocol] = o
            oc = o - jnp.mean(o, axis=-1, keepdims=True)
            on = oc * lax.rsqrt(jnp.mean(oc * oc, axis=-1, keepdims=True) + 1e-6)
            gate = p_ref[:, gcol]
            out_ref[:, ocol] = on * (gate * jax.nn.sigmoid(gate))

    row = lambda width: pl.BlockSpec((c, width), lambda n: (n, 0))
    whole = lambda a: pl.BlockSpec(a.shape, lambda n: (0,) * a.ndim)
    return _pcall(
        body, name="ret_fwd",
        out_shape=(jax.ShapeDtypeStruct((s, 2 * d), F32), jax.ShapeDtypeStruct((s, 2 * d), F32),
                   jax.ShapeDtypeStruct((n_chunks, H, dk, dv), MXU_DTYPE)),
        grid=(n_chunks,),
        in_specs=[row(d6), row(dk // 2), row(dk // 2)] + [whole(a) for a in consts],
        out_specs=(row(2 * d), row(2 * d), pl.BlockSpec((None, H, dk, dv), lambda n: (n, 0, 0, 0))),
        scratch_shapes=[pltpu.VMEM((H, dk, dv), F32)],
        compiler_params=_params("arbitrary"),
    )(proj, cos_a, sin_a, *consts)


def _ret_bwd_call(proj, oraw, states, dout):
    s, d6 = proj.shape
    d = d6 // 6
    H, c = RET_HEADS, RET_CHUNK
    dk, dv = d // H, 2 * d // H
    n_chunks = s // c
    kscale = dk ** -0.5
    cos_a, sin_a = _ret_angles(s, dk)
    consts = _ret_consts(c)

    def body(p_ref, cos_ref, sin_ref, dm_ref, ze_ref, xi_ref, gc_ref, oraw_ref, st_ref, do_ref, dp_ref, ds_ref):
        @pl.when(pl.program_id(0) == 0)
        def _():
            ds_ref[...] = jnp.zeros_like(ds_ref)

        cs, sn = cos_ref[...], sin_ref[...]
        for hd in range(H):
            qc, kc, vc, gcol = _ret_cols(d, dk, dv, hd)
            ocol = slice(hd * dv, (hd + 1) * dv)
            qb = _rot(p_ref[:, qc], cs, sn).astype(MXU_DTYPE)
            kr = _rot(p_ref[:, kc], cs, sn) * kscale
            kb = kr.astype(MXU_DTYPE)
            vb = p_ref[:, vc].astype(MXU_DTYPE)
            gate = p_ref[:, gcol]
            o = oraw_ref[:, ocol]
            oc = o - jnp.mean(o, axis=-1, keepdims=True)
            rstd = lax.rsqrt(jnp.mean(oc * oc, axis=-1, keepdims=True) + 1e-6)
            on = oc * rstd
            dout_h = do_ref[:, ocol]
            sg = jax.nn.sigmoid(gate)
            dp_ref[:, gcol] = dout_h * on * (sg * (1.0 + gate * (1.0 - sg)))
            don = dout_h * (gate * sg)
            do_raw = rstd * (don - jnp.mean(don, axis=-1, keepdims=True)
                             - on * jnp.mean(don * on, axis=-1, keepdims=True))
            dob = do_raw.astype(MXU_DTYPE)
            stb = st_ref[hd]
            ds = ds_ref[hd]
            dsb = ds.astype(MXU_DTYPE)
            dm = dm_ref[hd]
            scb = (lax.dot_general(qb, kb, _NT, preferred_element_type=F32) * dm).astype(MXU_DTYPE)
            dsc = (lax.dot_general(dob, vb, _NT, preferred_element_type=F32) * dm).astype(MXU_DTYPE)
            dqr = jnp.dot(dsc, kb, preferred_element_type=F32)
            dkr = lax.dot_general(dsc, qb, _TN, preferred_element_type=F32)
            dvv = lax.dot_general(scb, dob, _TN, preferred_element_type=F32)
            doi = (do_raw * xi_ref[hd]).astype(MXU_DTYPE)
            dqr = dqr + lax.dot_general(doi, stb, _NT, preferred_element_type=F32)
            ds_in = lax.dot_general(qb, doi, _TN, preferred_element_type=F32)
            ze = ze_ref[hd]
            dkr = dkr + lax.dot_general(vb, dsb, _NT, preferred_element_type=F32) * ze
            dvv = dvv + jnp.dot((kr * ze).astype(MXU_DTYPE), dsb, preferred_element_type=F32)
            ds_ref[hd] = ds * gc_ref[hd] + ds_in
            dp_ref[:, qc] = _rot_t(dqr, cs, sn)
            dp_ref[:, kc] = _rot_t(dkr * kscale, cs, sn)
            dp_ref[:, vc] = dvv

    last = n_chunks - 1
    row = lambda width: pl.BlockSpec((c, width), lambda n: (last - n, 0))
    whole = lambda a: pl.BlockSpec(a.shape, lambda n: (0,) * a.ndim)
    return _pcall(
        body, name="ret_bwd", out_shape=jax.ShapeDtypeStruct((s, d6), F32),
        grid=(n_chunks,),
        in_specs=[row(d6), row(dk // 2), row(dk // 2)] + [whole(a) for a in consts]
                 + [row(2 * d), pl.BlockSpec((None, H, dk, dv), lambda n: (last - n, 0, 0, 0)), row(2 * d)],
        out_specs=row(d6),
        scratch_shapes=[pltpu.VMEM((H, dk, dv), F32)],
        compiler_params=_params("arbitrary"),
    )(proj, cos_a, sin_a, *consts, oraw, states, dout)


@jax.custom_vjp
def _ret_core(proj):
    return _ret_fwd_call(proj)[0]


def _ret_core_fwd(proj):
    out, oraw, states = _ret_fwd_call(proj)
    return out, (proj, oraw, states)


def _ret_core_bwd(res, dout):
    return (_ret_bwd_call(*res, dout),)


_ret_core.defvjp(_ret_core_fwd, _ret_core_bwd)


def _retention(hx, p):
    return _linear(_ret_core(_linear(hx, p['ret_w_in'], "ret_in")), p['ret_w_out'], "ret_out")


SQRT_HALF = 2.0 ** -0.5
INV_SQRT_2PI = (2.0 * math.pi) ** -0.5


def _gmlp_front(p_ref, g_ref, b_ref, w):
    x = p_ref[...]
    cdf = 0.5 * (1.0 + lax.erf(x * SQRT_HALF))
    uv = x * cdf
    u, v = uv[:, :w], uv[:, w:]
    vc = v - jnp.mean(v, axis=-1, keepdims=True)
    rstd = lax.rsqrt(jnp.mean(vc * vc, axis=-1, keepdims=True) + LN_EPS)
    vhat = vc * rstd
    return x, cdf, u, vhat, rstd, vhat * g_ref[...] + b_ref[...]


def _gmlp_fwd_call(proj, ln_g, ln_b, ws, bs):
    s, w2 = proj.shape
    w = w2 // 2
    c, G = GMLP_CHUNK, GMLP_GROUPS
    gw = w // G

    def body(p_ref, g_ref, b_ref, ws_ref, bs_ref, o_ref):
        _, _, u, _, _, vn = _gmlp_front(p_ref, g_ref, b_ref, w)
        for gi in range(G):
            cols = slice(gi * gw, (gi + 1) * gw)
            vs = jnp.dot(ws_ref[gi].astype(MXU_DTYPE), vn[:, cols].astype(MXU_DTYPE),
                         preferred_element_type=F32) + bs_ref[gi]
            o_ref[:, cols] = u[:, cols] * vs

    whole = lambda a: pl.BlockSpec(a.shape, lambda n: (0,) * a.ndim)
    args = (_vec(ln_g), _vec(ln_b), ws, bs)
    return _pcall(
        body, name="gmlp_fwd", out_shape=jax.ShapeDtypeStruct((s, w), F32), grid=(s // c,),
        in_specs=[pl.BlockSpec((c, w2), lambda n: (n, 0))] + [whole(a) for a in args],
        out_specs=pl.BlockSpec((c, w), lambda n: (n, 0)),
        compiler_params=_params("parallel"),
    )(proj, *args)


def _gmlp_bwd_call(proj, ln_g, ln_b, ws, bs, dout):
    s, w2 = proj.shape
    w = w2 // 2
    c, G = GMLP_CHUNK, GMLP_GROUPS
    gw = w // G

    def body(p_ref, g_ref, b_ref, ws_ref, bs_ref, do_ref, dp_ref, dws_ref, dbs_ref, dgb_ref):
        @pl.when(pl.program_id(0) == 0)
        def _():
            dws_ref[...] = jnp.zeros_like(dws_ref)
            dbs_ref[...] = jnp.zeros_like(dbs_ref)
            dgb_ref[...] = jnp.zeros_like(dgb_ref)

        x, cdf, u, vhat, rstd, vn = _gmlp_front(p_ref, g_ref, b_ref, w)
        dout = do_ref[...]
        du_parts, dvn_parts = [], []
        for gi in range(G):
            cols = slice(gi * gw, (gi + 1) * gw)
            wsg = ws_ref[gi].astype(MXU_DTYPE)
            vng = vn[:, cols].astype(MXU_DTYPE)
            vs = jnp.dot(wsg, vng, preferred_element_type=F32) + bs_ref[gi]
            du_parts.append(dout[:, cols] * vs)
            dvs = dout[:, cols] * u[:, cols]
            dbs_ref[:, cols] += dvs
            dvsb = dvs.astype(MXU_DTYPE)
            dws_ref[gi] += lax.dot_general(dvsb, vng, _NT, preferred_element_type=F32)
            dvn_parts.append(lax.dot_general(wsg, dvsb, _TN, preferred_element_type=F32))
        dvn = jnp.concatenate(dvn_parts, axis=1)
        dgb_ref[0] += _fold8(dvn * vhat)
        dgb_ref[1] += _fold8(dvn)
        dvh = dvn * g_ref[...]
        dv = rstd * (dvh - jnp.mean(dvh, axis=-1, keepdims=True)
                     - vhat * jnp.mean(dvh * vhat, axis=-1, keepdims=True))
        duv = jnp.concatenate(du_parts + [dv], axis=1)
        dp_ref[...] = duv * (cdf + x * (jnp.exp(-0.5 * x * x) * INV_SQRT_2PI))

    whole = lambda a: pl.BlockSpec(a.shape, lambda n: (0,) * a.ndim)
    args = (_vec(ln_g), _vec(ln_b), ws, bs)
    acc = lambda *shape: pl.BlockSpec(shape, lambda n: (0,) * len(shape))
    return _pcall(
        body, name="gmlp_bwd",
        out_shape=(jax.ShapeDtypeStruct((s, w2), F32), jax.ShapeDtypeStruct((G, c, c), F32),
                   jax.ShapeDtypeStruct((c, w), F32), jax.ShapeDtypeStruct((2, SUBLANES, w), F32)),
        grid=(s // c,),
        in_specs=[pl.BlockSpec((c, w2), lambda n: (n, 0))] + [whole(a) for a in args]
                 + [pl.BlockSpec((c, w), lambda n: (n, 0))],
        out_specs=(pl.BlockSpec((c, w2), lambda n: (n, 0)), acc(G, c, c), acc(c, w), acc(2, SUBLANES, w)),
        compiler_params=_params("arbitrary"),
    )(proj, *args, dout)


def _gmlp_mask(c):
    return jnp.tril(jnp.ones((c, c), dtype=bool))


@jax.custom_vjp
def _gmlp_core(proj, ln_g, ln_b, w_s, b_s):
    ws = jnp.where(_gmlp_mask(GMLP_CHUNK), w_s, 0.0)
    return _gmlp_fwd_call(proj, ln_g, ln_b, ws, b_s[..., None])


def _gmlp_core_fwd(proj, ln_g, ln_b, w_s, b_s):
    return _gmlp_core(proj, ln_g, ln_b, w_s, b_s), (proj, ln_g, ln_b, w_s, b_s)


def _gmlp_core_bwd(res, dout):
    proj, ln_g, ln_b, w_s, b_s = res
    mask = _gmlp_mask(GMLP_CHUNK)
    dproj, dws, dbs, dgb = _gmlp_bwd_call(proj, ln_g, ln_b, jnp.where(mask, w_s, 0.0), b_s[..., None], dout)
    dgb = jnp.sum(dgb, axis=1)
    c = GMLP_CHUNK
    db_s = jnp.sum(dbs.reshape(c, GMLP_GROUPS, -1), axis=-1).T
    return dproj, dgb[0], dgb[1], jnp.where(mask, dws, 0.0), db_s


_gmlp_core.defvjp(_gmlp_core_fwd, _gmlp_core_bwd)


def _chunked_gmlp(hx, p):
    core = _gmlp_core(_linear(hx, p['gmlp_w_in'], "gmlp_in"), p['gmlp_ln_g'], p['gmlp_ln_b'],
                      p['gmlp_w_s'], p['gmlp_b_s'])
    return _linear(core, p['gmlp_w_out'], "gmlp_out")


def _stick_breaking(hx, p):
    H = SB_HEADS
    s, d = hx.shape
    dh = d // H
    qkv = _linear(hx, p['sb_w_in'], "sb_in")
    q, k, v = (qkv[:, j * d:(j + 1) * d].reshape(s, H, dh).transpose(1, 0, 2) for j in range(3))
    o = _sb_core(q, k, v)
    return _linear(o.transpose(1, 0, 2).reshape(s, d), p['sb_w_out'], "sb_out")


MIXERS = (_gated_deltanet, _retention, _chunked_gmlp, _stick_breaking)


def _trunk_grad(x, mods, p, target):
    d = x.shape[-1]
    saved = []
    for i in range(DEPTH):
        sh1, sc1, g1, sh2, sc2, g2 = (mods[i, j * d:(j + 1) * d] for j in range(6))
        h1 = _modulate(x, sc1, sh1, F32, "mod_a%d" % i)
        y1, mixer_vjp = jax.vjp(MIXERS[i], h1, {n: p[n] for n in MIXER_PARAMS[i]})
        x1 = _resid_ln(x, y1, g1, p['ln_g'][i, 0], p['ln_b'][i, 0], "ln_a%d" % i)
        h2 = _modulate(x1, sc2, sh2, MXU_DTYPE, "mod_b%d" % i)
        gu = _mm(h2, p['ffn_up'][i], 'nn', "ffn_up%d_fwd" % i)
        act = _ffn_gate(gu, p['ffn_conv_w'][i], p['ffn_conv_b'][i], "ffn_gate%d" % i)
        y2 = _mm(act, p['ffn_down'][i], 'nn', "ffn_down%d_fwd" % i)
        x2 = _resid_ln(x1, y2, g2, p['ln_g'][i, 1], p['ln_b'][i, 1], "ln_b%d" % i)
        saved.append((x, y1, mixer_vjp, x1, h2, gu, act, y2))
        x = x2
    loss, dx = _loss_head(x, target, "loss_head")

    dp = {n: None for n in p}
    d_ln_g, d_ln_b, d_up, d_down, d_cw, d_cb, dmods = [], [], [], [], [], [], []
    for i in reversed(range(DEPTH)):
        x0, y1, mixer_vjp, x1, h2, gu, act, y2 = saved[i]
        sh1, sc1, g1, sh2, sc2, g2 = (mods[i, j * d:(j + 1) * d] for j in range(6))
        dxa, dy2, dgam2, dbet2, dg2 = _resid_ln_bwd(x1, y2, g2, p['ln_g'][i, 1], dx, "ln_b%d_bwd" % i)
        dact = _mm(dy2, p['ffn_down'][i], 'nt', "ffn_down%d_dx" % i)
        d_down.append(_mm(act, dy2, 'tn', "ffn_down%d_dw" % i))
        dgate, dupp, dcw, dcb = _ffn_gate_bwd(dact, gu, p['ffn_conv_w'][i], p['ffn_conv_b'][i],
                                              "ffn_gate%d_bwd" % i)
        dgu = jnp.concatenate([dgate, dupp], axis=1)
        dh2 = _mm(dgu, p['ffn_up'][i], 'nt', "ffn_up%d_dx" % i)
        d_up.append(_mm(h2, dgu, 'tn', "ffn_up%d_dw" % i))
        dx1, dsc2, dsh2 = _modulate_bwd(dxa, dh2, x1, sc2, "mod_b%d_bwd" % i)
        dxa, dy1, dgam1, dbet1, dg1 = _resid_ln_bwd(x0, y1, g1, p['ln_g'][i, 0], dx1, "ln_a%d_bwd" % i)
        dh1, dmix = mixer_vjp(dy1)
        dp.update(dmix)
        dx, dsc1, dsh1 = _modulate_bwd(dxa, dh1, x0, sc1, "mod_a%d_bwd" % i)
        d_ln_g.append(jnp.stack([dgam1, dgam2]))
        d_ln_b.append(jnp.stack([dbet1, dbet2]))
        d_cw.append(dcw)
        d_cb.append(dcb)
        dmods.append(jnp.concatenate([dsh1, dsc1, dg1, dsh2, dsc2, dg2]))
    for n, parts in (('ln_g', d_ln_g), ('ln_b', d_ln_b), ('ffn_up', d_up), ('ffn_down', d_down),
                     ('ffn_conv_w', d_cw), ('ffn_conv_b', d_cb)):
        dp[n] = jnp.stack(parts[::-1])
    return loss, dx, jnp.stack(dmods[::-1]), dp


def _join(blocks, axis):
    return jnp.concatenate([blocks[d] for d in range(N_DEV)], axis=axis)


def _split(whole, axis):
    n = whole.shape[axis] // N_DEV
    return jnp.stack([lax.slice_in_dim(whole, d * n, (d + 1) * n, axis=axis) for d in range(N_DEV)])


def _pad8(a):
    pad = (-a.shape[0]) % 8
    return jnp.pad(a, ((0, pad), (0, 0))) if pad else a


def _pack_big_grads(full_grads, axes):
    per_dev = jnp.concatenate([_split(g, ax).reshape(N_DEV, -1) for g, ax in zip(full_grads, axes)], axis=1)
    pad = (-per_dev.shape[1]) % (BIG_ROW_ALIGN * LANES)
    if pad:
        per_dev = jnp.pad(per_dev, ((0, 0), (0, pad)))
    return per_dev.reshape(N_DEV, -1, LANES)


def kernel(x, c, cond_w, cond_b, ada_w, ada_b, ln_g, ln_b, ffn_up, ffn_conv_w, ffn_conv_b, ffn_down, gdn_w_in, gdn_conv_w, gdn_a_log, gdn_dt_bias, gdn_norm_w, gdn_w_out, ret_w_in, ret_w_out, gmlp_w_in, gmlp_ln_g, gmlp_ln_b, gmlp_w_s, gmlp_b_s, gmlp_w_out, sb_w_in, sb_w_out, loss_target, m_cond_w, m_cond_b, m_ada_w, m_ada_b, m_ln_g, m_ln_b, m_ffn_up, m_ffn_conv_w, m_ffn_conv_b, m_ffn_down, m_gdn_w_in, m_gdn_conv_w, m_gdn_a_log, m_gdn_dt_bias, m_gdn_norm_w, m_gdn_w_out, m_ret_w_in, m_ret_w_out, m_gmlp_w_in, m_gmlp_ln_g, m_gmlp_ln_b, m_gmlp_w_s, m_gmlp_b_s, m_gmlp_w_out, m_sb_w_in, m_sb_w_out, v_cond_w, v_cond_b, v_ada_w, v_ada_b, v_ln_g, v_ln_b, v_ffn_up, v_ffn_conv_w, v_ffn_conv_b, v_ffn_down, v_gdn_w_in, v_gdn_conv_w, v_gdn_a_log, v_gdn_dt_bias, v_gdn_norm_w, v_gdn_w_out, v_ret_w_in, v_ret_w_out, v_gmlp_w_in, v_gmlp_ln_g, v_gmlp_ln_b, v_gmlp_w_s, v_gmlp_b_s, v_gmlp_w_out, v_sb_w_in, v_sb_w_out):
    w = dict(cond_w=cond_w, cond_b=cond_b, ada_w=ada_w, ada_b=ada_b, ln_g=ln_g, ln_b=ln_b, ffn_up=ffn_up,
             ffn_conv_w=ffn_conv_w, ffn_conv_b=ffn_conv_b, ffn_down=ffn_down, gdn_w_in=gdn_w_in,
             gdn_conv_w=gdn_conv_w, gdn_a_log=gdn_a_log, gdn_dt_bias=gdn_dt_bias, gdn_norm_w=gdn_norm_w,
             gdn_w_out=gdn_w_out, ret_w_in=ret_w_in, ret_w_out=ret_w_out, gmlp_w_in=gmlp_w_in,
             gmlp_ln_g=gmlp_ln_g, gmlp_ln_b=gmlp_ln_b, gmlp_w_s=gmlp_w_s, gmlp_b_s=gmlp_b_s,
             gmlp_w_out=gmlp_w_out, sb_w_in=sb_w_in, sb_w_out=sb_w_out)
    mom = dict(cond_w=m_cond_w, cond_b=m_cond_b, ada_w=m_ada_w, ada_b=m_ada_b, ln_g=m_ln_g, ln_b=m_ln_b,
               ffn_up=m_ffn_up, ffn_conv_w=m_ffn_conv_w, ffn_conv_b=m_ffn_conv_b, ffn_down=m_ffn_down,
               gdn_w_in=m_gdn_w_in, gdn_conv_w=m_gdn_conv_w, gdn_a_log=m_gdn_a_log, gdn_dt_bias=m_gdn_dt_bias,
               gdn_norm_w=m_gdn_norm_w, gdn_w_out=m_gdn_w_out, ret_w_in=m_ret_w_in, ret_w_out=m_ret_w_out,
               gmlp_w_in=m_gmlp_w_in, gmlp_ln_g=m_gmlp_ln_g, gmlp_ln_b=m_gmlp_ln_b, gmlp_w_s=m_gmlp_w_s,
               gmlp_b_s=m_gmlp_b_s, gmlp_w_out=m_gmlp_w_out, sb_w_in=m_sb_w_in, sb_w_out=m_sb_w_out)
    var = dict(cond_w=v_cond_w, cond_b=v_cond_b, ada_w=v_ada_w, ada_b=v_ada_b, ln_g=v_ln_g, ln_b=v_ln_b,
               ffn_up=v_ffn_up, ffn_conv_w=v_ffn_conv_w, ffn_conv_b=v_ffn_conv_b, ffn_down=v_ffn_down,
               gdn_w_in=v_gdn_w_in, gdn_conv_w=v_gdn_conv_w, gdn_a_log=v_gdn_a_log, gdn_dt_bias=v_gdn_dt_bias,
               gdn_norm_w=v_gdn_norm_w, gdn_w_out=v_gdn_w_out, ret_w_in=v_ret_w_in, ret_w_out=v_ret_w_out,
               gmlp_w_in=v_gmlp_w_in, gmlp_ln_g=v_gmlp_ln_g, gmlp_ln_b=v_gmlp_ln_b, gmlp_w_s=v_gmlp_w_s,
               gmlp_b_s=v_gmlp_b_s, gmlp_w_out=v_gmlp_w_out, sb_w_in=v_sb_w_in, sb_w_out=v_sb_w_out)

    me = _my_id()
    x = x[0]
    target = loss_target[0]
    d = x.shape[-1]
    dsh = d // N_DEV
    msh = ada_w.shape[-1]

    c_all = _exchange(_pad8(c), False, "gather_c")[:, 0, :]
    c_mine = lax.dynamic_slice_in_dim(c_all, me * dsh, dsh, axis=1)
    pre_part = _mm(c_mine, cond_w, 'nn', "cond_fwd")
    pre = jnp.sum(_exchange(pre_part, False, "gather_pre"), axis=0) + cond_b
    e_all = jax.nn.silu(pre)
    mod_part = jnp.concatenate([_mm(e_all, ada_w[i], 'nn', "ada_fwd%d" % i) for i in range(DEPTH)], axis=0)
    mod_all = _exchange(mod_part, False, "gather_mod")
    mod_all = mod_all.reshape(N_DEV, DEPTH, N_DEV, msh)
    mods = lax.dynamic_index_in_dim(mod_all, me, axis=2, keepdims=False)
    mods = mods.transpose(1, 0, 2).reshape(DEPTH, N_DEV * msh) + ada_b

    big_names = list(BIG)
    packed = _pack_rows([w[n] for n in big_names], BF16, BIG_ROW_ALIGN)
    gathered = _gather_two_level(packed, "gather_weights")
    blocks = _unpack_rows(gathered, [w[n].shape for n in big_names])
    p = {n: _join(b, BIG[n]) for n, b in zip(big_names, blocks)}
    for n in big_names:
        if not n.startswith('ffn_'):
            p[n] = p[n].astype(F32)
    sm_names = list(SMALL_SHARDED)
    sm_packed = _pack_rows([w[n] for n in sm_names], F32)
    sm_blocks = _unpack_rows(_exchange(sm_packed, False, "gather_small"), [w[n].shape for n in sm_names])
    for n, b in zip(sm_names, sm_blocks):
        p[n] = _join(b, SMALL_SHARDED[n])
    for n in SMALL_REPL:
        p[n] = w[n]
    n_qkvz = 4 * d
    p['gdn_w_qkvz'] = p['gdn_w_in'][:, :n_qkvz]
    p['gdn_w_ab'] = jnp.pad(p['gdn_w_in'][:, n_qkvz:], ((0, 0), (0, LANES - 2 * GDN_HEADS)))
    del p['gdn_w_in']

    loss_local, dx, dmods, dp = _trunk_grad(x, mods, p, target)
    dp['gdn_w_in'] = jnp.concatenate([dp.pop('gdn_w_qkvz'), dp.pop('gdn_w_ab')[:, :2 * GDN_HEADS]], axis=1)

    dmod_all = _exchange(dmods.reshape(-1, d), False, "gather_dmod").reshape(N_DEV, DEPTH, 6 * d)
    grads = {'ada_b': jnp.sum(dmod_all, axis=0)}
    dm_mine = lax.dynamic_slice_in_dim(dmod_all, me * msh, msh, axis=2)
    grads['ada_w'] = jnp.stack([_mm_outer(e_all, dm_mine[:, i], "ada_dw%d" % i) for i in range(DEPTH)])
    de_part = _mm(dm_mine[:, 0], ada_w[0], 'nt', "ada_de0")
    for i in range(1, DEPTH):
        de_part = de_part + _mm(dm_mine[:, i], ada_w[i], 'nt', "ada_de%d" % i)
    de_all = jnp.sum(_exchange(de_part, False, "gather_de"), axis=0)
    sig = jax.nn.sigmoid(pre)
    dpre = de_all * (sig * (1.0 + pre * (1.0 - sig)))
    grads['cond_b'] = jnp.sum(dpre, axis=0)
    grads['cond_w'] = _mm_outer(c_mine, dpre, "cond_dw")

    small_names = sm_names + SMALL_REPL
    small_packed = _pack_rows([loss_local.reshape(1)] + [dp[n] for n in small_names], F32)
    small_sum = _sum_slots(_exchange(small_packed, False, "gather_small_grads"), "sum_small_grads")
    small = _unpack_rows(small_sum, [(1,)] + [dp[n].shape for n in small_names])
    loss = small[0][0]
    for n, g in zip(small_names, small[1:]):
        if n in SMALL_SHARDED:
            ax = SMALL_SHARDED[n]
            g = lax.dynamic_slice_in_dim(g, me * w[n].shape[ax], w[n].shape[ax], axis=ax)
        grads[n] = g

    send = _pack_big_grads([dp[n] for n in big_names], [BIG[n] for n in big_names])
    shapes = [w[n].shape for n in big_names]
    outs = _adamw(_reduce_to_owner(send), *[_pack_rows([t[n] for n in big_names], F32, BIG_ROW_ALIGN) for t in (w, mom, var)],
                  "adamw_big")
    g_b, d_b, m_b, v_b = (_unpack_rows(o, shapes) for o in outs)
    delta, new_m, new_v = {}, {}, {}
    for j, n in enumerate(big_names):
        grads[n], delta[n], new_m[n], new_v[n] = g_b[j], d_b[j], m_b[j], v_b[j]

    rest = [n for n in WEIGHTS if n not in BIG]
    shapes = [w[n].shape for n in rest]
    outs = _adamw([(_pack_rows([grads[n] for n in rest], F32, BIG_ROW_ALIGN)[None], 0)],
                  *[_pack_rows([t[n] for n in rest], F32, BIG_ROW_ALIGN) for t in (w, mom, var)], "adamw_rest")
    _, d_r, m_r, v_r = (_unpack_rows(o, shapes) for o in outs)
    for j, n in enumerate(rest):
        delta[n], new_m[n], new_v[n] = d_r[j], m_r[j], v_r[j]

    return (loss, dx[None], *[grads[n] for n in WEIGHTS], *[delta[n] for n in WEIGHTS],
            *[new_m[n] for n in WEIGHTS], *[new_v[n] for n in WEIGHTS])
```

```python
import functools
import math

import jax
import jax.numpy as jnp
from jax import lax
from jax.experimental import pallas as pl
from jax.experimental.pallas import tpu as pltpu

F32 = jnp.float32
BF16 = jnp.bfloat16
MXU_DTYPE = jnp.bfloat16
MESH = pl.DeviceIdType.MESH
N_DEV = 8
LANES = 128
SUBLANES = 8
VMEM_LIMIT = 48 * 1024 * 1024

DEPTH = 4
LN_EPS = 1e-5
DN_ALPHA = (2.0 * DEPTH) ** 0.25
GDN_HEADS, GDN_CONV, GDN_CHUNK = 8, 4, 64
RET_HEADS, RET_CHUNK, RET_ROPE_BASE = 4, 128, 10000.0
GMLP_CHUNK, GMLP_GROUPS = 128, 8
SB_HEADS = 16
ADAM_LR, ADAM_B1, ADAM_B2, ADAM_EPS, ADAM_WD, ADAM_STEP = 0.001, 0.9, 0.999, 1e-08, 0.01, 10

WEIGHTS = ['cond_w', 'cond_b', 'ada_w', 'ada_b', 'ln_g', 'ln_b', 'ffn_up', 'ffn_conv_w', 'ffn_conv_b', 'ffn_down',
           'gdn_w_in', 'gdn_conv_w', 'gdn_a_log', 'gdn_dt_bias', 'gdn_norm_w', 'gdn_w_out', 'ret_w_in', 'ret_w_out',
           'gmlp_w_in', 'gmlp_ln_g', 'gmlp_ln_b', 'gmlp_w_s', 'gmlp_b_s', 'gmlp_w_out', 'sb_w_in', 'sb_w_out']
BIG = {'ffn_up': 2, 'ffn_down': 1, 'gdn_w_in': 1, 'gdn_w_out': 0, 'ret_w_in': 1, 'ret_w_out': 0,
       'gmlp_w_in': 1, 'gmlp_w_out': 0, 'sb_w_in': 1, 'sb_w_out': 0}
SMALL_SHARDED = {'ln_g': 2, 'ln_b': 2, 'ffn_conv_w': 2, 'gdn_conv_w': 1}
SMALL_REPL = ['ffn_conv_b', 'gdn_a_log', 'gdn_dt_bias', 'gdn_norm_w', 'gmlp_ln_g', 'gmlp_ln_b', 'gmlp_w_s', 'gmlp_b_s']
MIXER_PARAMS = (('gdn_w_qkvz', 'gdn_w_ab', 'gdn_conv_w', 'gdn_a_log', 'gdn_dt_bias', 'gdn_norm_w', 'gdn_w_out'),
                ('ret_w_in', 'ret_w_out'),
                ('gmlp_w_in', 'gmlp_ln_g', 'gmlp_ln_b', 'gmlp_w_s', 'gmlp_b_s', 'gmlp_w_out'),
                ('sb_w_in', 'sb_w_out'))


def _pcall(body, **kw):
    return pl.pallas_call(body, **kw)


def _params(*semantics):
    return pltpu.CompilerParams(dimension_semantics=semantics, vmem_limit_bytes=VMEM_LIMIT)


def _my_id():
    return 4 * lax.axis_index("x") + 2 * lax.axis_index("y") + lax.axis_index("c")


def _pick(dim, prefs):
    for p in prefs:
        if dim % p == 0:
            return p
    return dim


def _exchange(src, scatter, name):
    blk = src.shape[1:] if scatter else src.shape
    out_shape = jax.ShapeDtypeStruct((N_DEV,) + tuple(blk), src.dtype)

    def body(src_ref, out_ref, send_sems, recv_sems, local_sem):
        x, y, c = lax.axis_index("x"), lax.axis_index("y"), lax.axis_index("c")
        me = 4 * x + 2 * y + c
        mine = pltpu.make_async_copy(src_ref.at[me] if scatter else src_ref, out_ref.at[me], local_sem)
        mine.start()
        copies = []
        for k in range(1, N_DEV):
            px = 1 - x if (k >> 2) & 1 else x
            py = 1 - y if (k >> 1) & 1 else y
            pc = 1 - c if k & 1 else c
            peer = 4 * px + 2 * py + pc
            cp = pltpu.make_async_remote_copy(
                src_ref=src_ref.at[peer] if scatter else src_ref,
                dst_ref=out_ref.at[me],
                send_sem=send_sems.at[k - 1], recv_sem=recv_sems.at[k - 1],
                device_id=(px, py, pc), device_id_type=MESH)
            cp.start()
            copies.append(cp)
        for cp in copies:
            cp.wait()
        mine.wait()

    return _pcall(
        body, name=name, out_shape=out_shape,
        in_specs=[pl.BlockSpec(memory_space=pl.ANY)],
        out_specs=pl.BlockSpec(memory_space=pl.ANY),
        scratch_shapes=[pltpu.SemaphoreType.DMA((N_DEV - 1,)), pltpu.SemaphoreType.DMA((N_DEV - 1,)),
                        pltpu.SemaphoreType.DMA(())],
    )(src)


def _flip(x, y, c, k):
    return (1 - x if (k >> 2) & 1 else x, 1 - y if (k >> 1) & 1 else y, 1 - c if k & 1 else c)


def _dev_id(p):
    return 4 * p[0] + 2 * p[1] + p[2]


OTHER_CHIPS = (4, 2, 6)


def _gather_two_level(src, name):
    out_shape = jax.ShapeDtypeStruct((N_DEV,) + tuple(src.shape), src.dtype)

    def body(x_ref, out_ref, send_sems, recv_sems, local_sem):
        x, y, c = lax.axis_index("x"), lax.axis_index("y"), lax.axis_index("c")
        me, sibling = (x, y, c), (x, y, 1 - c)
        chips = [_flip(x, y, c, k) for k in OTHER_CHIPS]

        def copy(k, block, to, from_src=False):
            slot = out_ref.at[_dev_id(block)]
            return pltpu.make_async_remote_copy(
                src_ref=x_ref if from_src else slot, dst_ref=slot,
                send_sem=send_sems.at[k], recv_sem=recv_sems.at[k], device_id=to, device_id_type=MESH)

        mine = pltpu.make_async_copy(x_ref, out_ref.at[_dev_id(me)], local_sem)
        mine.start()
        first = [copy(0, me, sibling, True)] + [copy(1 + j, me, chip, True) for j, chip in enumerate(chips)]
        for cp in first:
            cp.start()
        passed = [copy(4 + j, chip, sibling) for j, chip in enumerate(chips)]
        for j, chip in enumerate(chips):
            copy(1 + j, chip, me).wait_recv()
            passed[j].start()
        copy(0, sibling, me).wait_recv()
        for j, chip in enumerate(chips):
            copy(4 + j, (chip[0], chip[1], 1 - c), me).wait_recv()
        for cp in first + passed:
            cp.wait_send()
        mine.wait()

    return _pcall(
        body, name=name, out_shape=out_shape,
        in_specs=[pl.BlockSpec(memory_space=pl.ANY)],
        out_specs=pl.BlockSpec(memory_space=pl.ANY),
        scratch_shapes=[pltpu.SemaphoreType.DMA((N_DEV - 1,)), pltpu.SemaphoreType.DMA((N_DEV - 1,)),
                        pltpu.SemaphoreType.DMA(())],
    )(src)


def _send_slots(src, plan, n_out, name):
    out_shape = jax.ShapeDtypeStruct((n_out,) + tuple(src.shape[1:]), src.dtype)

    def body(src_ref, out_ref, send_sems, recv_sems):
        x, y, c = lax.axis_index("x"), lax.axis_index("y"), lax.axis_index("c")
        copies = []
        for e, (k, src_slot, dst_slot) in enumerate(plan):
            cp = pltpu.make_async_remote_copy(
                src_ref=src_ref.at[src_slot(x, y, c)], dst_ref=out_ref.at[dst_slot],
                send_sem=send_sems.at[e], recv_sem=recv_sems.at[e],
                device_id=_flip(x, y, c, k), device_id_type=MESH)
            cp.start()
            copies.append(cp)
        for cp in copies:
            cp.wait()

    return _pcall(
        body, name=name, out_shape=out_shape,
        in_specs=[pl.BlockSpec(memory_space=pl.ANY)],
        out_specs=pl.BlockSpec(memory_space=pl.ANY),
        scratch_shapes=[pltpu.SemaphoreType.DMA((len(plan),)), pltpu.SemaphoreType.DMA((len(plan),))],
    )(src)


def _reduce_to_owner(send):
    x, y, c = lax.axis_index("x"), lax.axis_index("y"), lax.axis_index("c")
    plan_a = [(1, lambda x, y, c: _dev_id((x, y, 1 - c)), 0)]
    plan_a += [(1, functools.partial(lambda k, x, y, c: _dev_id(_flip(x, y, c, k | 1)), k), 1 + j)
               for j, k in enumerate(OTHER_CHIPS)]
    from_sibling = _send_slots(send, plan_a, 1 + len(OTHER_CHIPS), "reduce_d2d")
    mine = jnp.stack([lax.dynamic_index_in_dim(send, _dev_id(_flip(x, y, c, k)), 0, keepdims=False)
                      for k in OTHER_CHIPS])
    rows = mine.shape[1]
    pair = _add_rows(mine.reshape(-1, LANES), from_sibling[1:].reshape(-1, LANES), "reduce_pair_sum")
    plan_c = [(k, functools.partial(lambda j, x, y, c: j, j), j) for j, k in enumerate(OTHER_CHIPS)]
    from_chips = _send_slots(pair.reshape(len(OTHER_CHIPS), rows, LANES), plan_c, len(OTHER_CHIPS), "reduce_ici")
    own = lax.dynamic_index_in_dim(send, _dev_id((x, y, c)), 0, keepdims=True)
    return [(own, 0), (from_sibling, 0)] + [(from_chips, j) for j in range(len(OTHER_CHIPS))]


ROW_ALIGN = 16
BIG_ROW_ALIGN = 512


def _pack_rows(parts, dtype, row_align=ROW_ALIGN):
    flat = jnp.concatenate([p.reshape(-1).astype(dtype) for p in parts])
    n = flat.shape[0]
    pad = (-n) % (row_align * LANES)
    if pad:
        flat = jnp.concatenate([flat, jnp.zeros((pad,), dtype)])
    return flat.reshape(-1, LANES)


def _unpack_rows(packed, shapes):
    lead = packed.shape[:-2]
    flat = packed.reshape(lead + (-1,))
    out, off = [], 0
    for s in shapes:
        n = math.prod(s)
        out.append(flat[..., off:off + n].reshape(lead + tuple(s)))
        off += n
    return out


def _mm(a, b, dims, name, exact=False):
    if dims == 'nn':
        (m, k), n = a.shape, b.shape[1]
    elif dims == 'nt':
        (m, k), n = a.shape, b.shape[0]
    else:
        (k, m), n = a.shape, b.shape[1]
    tm = _pick(m, (1408, 1024, 512, 256, 128))
    tn = _pick(n, (512, 256, 128))
    tk = k if k <= 2816 else _pick(k, (2816, 2048, 1536, 1024, 512, 256, 128))
    nk = k // tk
    if dims == 'nn':
        a_spec = pl.BlockSpec((tm, tk), lambda i, j, kk: (i, kk))
        b_spec = pl.BlockSpec((tk, tn), lambda i, j, kk: (kk, j))
        dn = (((1,), (0,)), ((), ()))
    elif dims == 'nt':
        a_spec = pl.BlockSpec((tm, tk), lambda i, j, kk: (i, kk))
        b_spec = pl.BlockSpec((tn, tk), lambda i, j, kk: (j, kk))
        dn = (((1,), (1,)), ((), ()))
    else:
        a_spec = pl.BlockSpec((tk, tm), lambda i, j, kk: (kk, i))
        b_spec = pl.BlockSpec((tk, tn), lambda i, j, kk: (kk, j))
        dn = (((0,), (0,)), ((), ()))

    def product(a_ref, b_ref):
        if exact:
            return lax.dot_general(a_ref[...], b_ref[...], dn, precision=lax.Precision.HIGHEST,
                                   preferred_element_type=F32)
        return lax.dot_general(a_ref[...].astype(MXU_DTYPE), b_ref[...].astype(MXU_DTYPE), dn,
                               preferred_element_type=F32)

    def body(a_ref, b_ref, o_ref, *acc):
        if nk == 1:
            o_ref[...] = product(a_ref, b_ref)
            return
        acc_ref, = acc
        kk = pl.program_id(2)

        @pl.when(kk == 0)
        def _():
            acc_ref[...] = jnp.zeros_like(acc_ref)

        acc_ref[...] += product(a_ref, b_ref)

        @pl.when(kk == nk - 1)
        def _():
            o_ref[...] = acc_ref[...]

    return _pcall(
        body, name=name, out_shape=jax.ShapeDtypeStruct((m, n), F32),
        grid=(m // tm, n // tn, nk),
        in_specs=[a_spec, b_spec],
        out_specs=pl.BlockSpec((tm, tn), lambda i, j, kk: (i, j)),
        scratch_shapes=[pltpu.VMEM((tm, tn), F32)] if nk > 1 else [],
        compiler_params=_params("parallel", "parallel", "arbitrary"),
    )(a, b)


def _mm_outer(a, b, name):
    pad = LANES - a.shape[0]
    return _mm(jnp.pad(a.T, ((0, 0), (0, pad))), jnp.pad(b, ((0, pad), (0, 0))), 'nn', name, exact=True)


@functools.partial(jax.custom_vjp, nondiff_argnums=(2,))
def _linear(a, w, name):
    return _mm(a, w, 'nn', name + "_fwd")


def _linear_fwd(a, w, name):
    return _mm(a, w, 'nn', name + "_fwd"), (a, w)


def _linear_bwd(name, res, dy):
    a, w = res
    return _mm(dy, w, 'nt', name + "_dx"), _mm(a, dy, 'tn', name + "_dw")


_linear.defvjp(_linear_fwd, _linear_bwd)


SB_BK = 256
SB_STRIP = 32


def _sb_tiles(s):
    tq = _pick(s, (512, 256, 128))
    bk = min(SB_BK, tq)
    return tq, bk, tq // bk


def _sb_valid(t, sr, bk, q0, k0):
    row = lax.broadcasted_iota(jnp.int32, (sr, bk), 0) + (q0 + t * sr)
    col = lax.broadcasted_iota(jnp.int32, (sr, bk), 1) + k0
    return col < row


def _sb_tri(bk, inclusive):
    r = jnp.bitwise_and(lax.broadcasted_iota(jnp.int32, (2 * bk, bk), 0), bk - 1)
    c = lax.broadcasted_iota(jnp.int32, (2 * bk, bk), 1)
    return (r >= c).astype(BF16) if inclusive else (r > c).astype(BF16)


def _sb_split(ref, n, rows, bk, val):
    hi = val.astype(BF16)
    ref[n, rows, 0:bk] = hi
    ref[n, rows, bk:2 * bk] = (val - hi.astype(F32)).astype(BF16)


LOG2E = 1.0 / math.log(2.0)


def _sb_logits_phase(z_ref, ls_ref, hl_ref, l0_ref, n, tq, bk, sr, q0, k0, masked):
    for t in range(tq // sr):
        rows = slice(t * sr, (t + 1) * sr)
        z = z_ref[n, rows, :] * LOG2E
        ls = jnp.minimum(z, 0.0) - jnp.log(1.0 + jnp.exp2(-jnp.abs(z))) * LOG2E
        lm = ls - z
        if masked:
            lm = jnp.where(_sb_valid(t, sr, bk, q0, k0), lm, 0.0)
        ls_ref[n, rows, :] = ls
        _sb_split(hl_ref, n, rows, bk, lm)
        l0_ref[n, rows, :] = lm[:, 0:1]


def _sb_fwd_call(q, kt, v):
    h, s, dh = q.shape
    tq, bk, nt = _sb_tiles(s)
    sr = SB_STRIP

    def body(q_ref, kt_ref, v_ref, o_ref, z_ref, ls_ref, hl_ref, f_ref, a_ref, l0_ref, cl_ref, acc_ref):
        i = pl.program_id(1)
        q0 = i * tq
        cl_ref[...] = jnp.zeros_like(cl_ref)
        acc_ref[...] = jnp.zeros_like(acc_ref)
        u_excl = _sb_tri(bk, False)

        def iteration(kb0, masked):
            k0s = [pl.multiple_of(kb0 + (nt - 1 - n) * bk, bk) for n in range(nt)]
            for n in range(nt):
                z_ref[n] = jnp.dot(q_ref[...], kt_ref[:, pl.ds(k0s[n], bk)], preferred_element_type=F32)
            for n in range(nt):
                _sb_logits_phase(z_ref, ls_ref, hl_ref, l0_ref, n, tq, bk, sr, q0, k0s[n], masked)
            for n in range(nt):
                f_ref[n] = jnp.dot(hl_ref[n], u_excl, preferred_element_type=F32)
            for t in range(tq // sr):
                rows = slice(t * sr, (t + 1) * sr)
                c = cl_ref[rows, :]
                for n in range(nt):
                    f = f_ref[n, rows, :]
                    a = jnp.exp2(ls_ref[n, rows, :] + f + c)
                    if masked:
                        a = jnp.where(_sb_valid(t, sr, bk, q0, k0s[n]), a, 0.0)
                    a_ref[n, rows, :] = a.astype(a_ref.dtype)
                    c = c + f[:, 0:1] + l0_ref[n, rows, :]
                cl_ref[rows, :] = c
            for n in range(nt):
                acc_ref[...] += jnp.dot(a_ref[n], v_ref[pl.ds(k0s[n], bk), :], preferred_element_type=F32)

        def below(jj, c):
            iteration((i - 1 - jj) * tq, False)
            return c

        iteration(q0, True)
        lax.fori_loop(0, i, below, 0)
        o_ref[...] = acc_ref[...]

    return _pcall(
        body, name="sb_fwd", out_shape=jax.ShapeDtypeStruct((h, s, dh), F32),
        grid=(h, s // tq),
        in_specs=[pl.BlockSpec((None, tq, dh), lambda hh, i: (hh, i, 0)),
                  pl.BlockSpec((None, dh, s), lambda hh, i: (hh, 0, 0)),
                  pl.BlockSpec((None, s, dh), lambda hh, i: (hh, 0, 0))],
        out_specs=pl.BlockSpec((None, tq, dh), lambda hh, i: (hh, i, 0)),
        scratch_shapes=[pltpu.VMEM((nt, tq, bk), F32), pltpu.VMEM((nt, tq, bk), F32),
                        pltpu.VMEM((nt, tq, 2 * bk), BF16), pltpu.VMEM((nt, tq, bk), F32),
                        pltpu.VMEM((nt, tq, bk), q.dtype), pltpu.VMEM((nt, tq, 1), F32),
                        pltpu.VMEM((tq, 1), F32), pltpu.VMEM((tq, dh), F32)],
        compiler_params=_params("parallel", "arbitrary"),
    )(q, kt, v)


def _sb_bwd_call(q, qt, k, kt, vt, o, do, dot):
    h, s, dh = q.shape
    tq, bk, nt = _sb_tiles(s)
    sr = SB_STRIP

    def body(q_ref, qt_ref, k_ref, kt_ref, vt_ref, o_ref, do_ref, dot_ref, dq_ref, dkt_ref, dvt_ref,
             z_ref, ls_ref, hl_ref, f_ref, a_ref, g_ref, dz_ref, da_ref, l0_ref, dob_ref,
             cl_ref, cg_ref, dl_ref, dqa_ref):
        i = pl.program_id(1)
        q0 = i * tq

        @pl.when(i == 0)
        def _():
            dkt_ref[...] = jnp.zeros_like(dkt_ref)
            dvt_ref[...] = jnp.zeros_like(dvt_ref)

        cl_ref[...] = jnp.zeros_like(cl_ref)
        cg_ref[...] = jnp.zeros_like(cg_ref)
        dqa_ref[...] = jnp.zeros_like(dqa_ref)
        dob = do_ref[...].astype(dob_ref.dtype)
        dob_ref[...] = dob
        dl_ref[...] = jnp.sum(dob.astype(F32) * o_ref[...], axis=1, keepdims=True)
        u_excl = _sb_tri(bk, False)
        u_incl = _sb_tri(bk, True)

        def iteration(kb0, masked):
            k0s = [pl.multiple_of(kb0 + (nt - 1 - n) * bk, bk) for n in range(nt)]
            for n in range(nt):
                z_ref[n] = jnp.dot(q_ref[...], kt_ref[:, pl.ds(k0s[n], bk)], preferred_element_type=F32)
                da_ref[n] = jnp.dot(dob_ref[...], vt_ref[:, pl.ds(k0s[n], bk)], preferred_element_type=F32)
            for n in range(nt):
                _sb_logits_phase(z_ref, ls_ref, hl_ref, l0_ref, n, tq, bk, sr, q0, k0s[n], masked)
            for n in range(nt):
                f_ref[n] = jnp.dot(hl_ref[n], u_excl, preferred_element_type=F32)
            for t in range(tq // sr):
                rows = slice(t * sr, (t + 1) * sr)
                c = cl_ref[rows, :]
                for n in range(nt):
                    f = f_ref[n, rows, :]
                    a = jnp.exp2(ls_ref[n, rows, :] + f + c)
                    if masked:
                        a = jnp.where(_sb_valid(t, sr, bk, q0, k0s[n]), a, 0.0)
                    ab = a.astype(a_ref.dtype)
                    a_ref[n, rows, :] = ab
                    g = da_ref[n, rows, :] * ab.astype(F32)
                    g_ref[n, rows, :] = g
                    _sb_split(hl_ref, n, rows, bk, g)
                    c = c + f[:, 0:1] + l0_ref[n, rows, :]
                cl_ref[rows, :] = c
            for n in range(nt):
                f_ref[n] = jnp.dot(hl_ref[n], u_incl, preferred_element_type=F32)
            for t in range(tq // sr):
                rows = slice(t * sr, (t + 1) * sr)
                cg = cg_ref[rows, :]
                for n in range(nt):
                    sg_tile = f_ref[n, rows, :]
                    p = dl_ref[rows, :] - (sg_tile + cg)
                    g = g_ref[n, rows, :]
                    dz = g - (g + p) * jnp.exp2(ls_ref[n, rows, :])
                    if masked:
                        dz = jnp.where(_sb_valid(t, sr, bk, q0, k0s[n]), dz, 0.0)
                    dz_ref[n, rows, :] = dz.astype(dz_ref.dtype)
                    cg = cg + sg_tile[:, 0:1]
                cg_ref[rows, :] = cg
            for n in range(nt):
                cols = pl.ds(k0s[n], bk)
                dqa_ref[...] += jnp.dot(dz_ref[n], k_ref[cols, :], preferred_element_type=F32)
                dkt_ref[:, cols] += jnp.dot(qt_ref[...], dz_ref[n], preferred_element_type=F32)
                dvt_ref[:, cols] += jnp.dot(dot_ref[...], a_ref[n], preferred_element_type=F32)

        def below(jj, c):
            iteration((i - 1 - jj) * tq, False)
            return c

        iteration(q0, True)
        lax.fori_loop(0, i, below, 0)
        dq_ref[...] = dqa_ref[...]

    blk_q = pl.BlockSpec((None, tq, dh), lambda hh, i: (hh, i, 0))
    blk_qt = pl.BlockSpec((None, dh, tq), lambda hh, i: (hh, 0, i))
    blk_s = pl.BlockSpec((None, s, dh), lambda hh, i: (hh, 0, 0))
    blk_st = pl.BlockSpec((None, dh, s), lambda hh, i: (hh, 0, 0))
    mx = q.dtype
    return _pcall(
        body, name="sb_bwd",
        out_shape=(jax.ShapeDtypeStruct((h, s, dh), F32), jax.ShapeDtypeStruct((h, dh, s), F32),
                   jax.ShapeDtypeStruct((h, dh, s), F32)),
        grid=(h, s // tq),
        in_specs=[blk_q, blk_qt, blk_s, blk_st, blk_st, blk_q, blk_q, blk_qt],
        out_specs=(blk_q, blk_st, blk_st),
        scratch_shapes=[pltpu.VMEM((nt, tq, bk), F32), pltpu.VMEM((nt, tq, bk), F32),
                        pltpu.VMEM((nt, tq, 2 * bk), BF16), pltpu.VMEM((nt, tq, bk), F32),
                        pltpu.VMEM((nt, tq, bk), mx), pltpu.VMEM((nt, tq, bk), F32),
                        pltpu.VMEM((nt, tq, bk), mx), pltpu.VMEM((nt, tq, bk), F32),
                        pltpu.VMEM((nt, tq, 1), F32), pltpu.VMEM((tq, dh), mx),
                        pltpu.VMEM((tq, 1), F32), pltpu.VMEM((tq, 1), F32), pltpu.VMEM((tq, 1), F32),
                        pltpu.VMEM((tq, dh), F32)],
        compiler_params=_params("parallel", "arbitrary"),
    )(q, qt, k, kt, vt, o, do, dot)


def _swap(t):
    return t.transpose(0, 2, 1)


def _sb_scale(dh):
    assert math.log2(dh) % 2 == 0, dh
    return dh ** -0.5


@jax.custom_vjp
def _sb_core(q, k, v):
    return _sb_core_fwd(q, k, v)[0]


def _sb_core_fwd(q, k, v):
    scale = _sb_scale(q.shape[-1])
    qs, kb, vb = (q.astype(MXU_DTYPE) * scale).astype(MXU_DTYPE), k.astype(MXU_DTYPE), v.astype(MXU_DTYPE)
    o = _sb_fwd_call(qs, _swap(kb), vb)
    return o, (qs, kb, vb, o)


def _sb_core_bwd(res, do):
    qs, kb, vb, o = res
    ks = (kb * _sb_scale(kb.shape[-1])).astype(MXU_DTYPE)
    dq, dkt, dvt = _sb_bwd_call(qs, _swap(qs), ks, _swap(kb), _swap(vb), o, do, _swap(do.astype(MXU_DTYPE)))
    return dq, _swap(dkt), _swap(dvt)


_sb_core.defvjp(_sb_core_fwd, _sb_core_bwd)


def _row_block(s):
    return _pick(s, (512, 256, 128, 64, 32, 16, 8))


def _fold8(t):
    r, c = t.shape
    return jnp.sum(t.reshape(r // SUBLANES, SUBLANES, c), axis=0)


def _vec(a):
    return a.reshape(1, -1)


def _modulate(x, sc, sh, out_dtype, name):
    s, d = x.shape
    tr = _row_block(s)

    def body(x_ref, sc_ref, sh_ref, o_ref):
        o_ref[...] = (x_ref[...] * (1.0 + sc_ref[...]) + sh_ref[...]).astype(o_ref.dtype)

    row = pl.BlockSpec((tr, d), lambda i: (i, 0))
    vec = pl.BlockSpec((1, d), lambda i: (0, 0))
    return _pcall(body, name=name, out_shape=jax.ShapeDtypeStruct((s, d), out_dtype), grid=(s // tr,),
                  in_specs=[row, vec, vec], out_specs=row, compiler_params=_params("parallel"))(x, _vec(sc), _vec(sh))


def _modulate_bwd(dxa, dh, x, sc, name):
    s, d = x.shape
    tr = _row_block(s)

    def body(dxa_ref, dh_ref, x_ref, sc_ref, dx_ref, acc_ref):
        @pl.when(pl.program_id(0) == 0)
        def _():
            acc_ref[...] = jnp.zeros_like(acc_ref)

        dh = dh_ref[...]
        dx_ref[...] = dxa_ref[...] + dh * (1.0 + sc_ref[...])
        acc_ref[0] += _fold8(dh * x_ref[...])
        acc_ref[1] += _fold8(dh)

    row = pl.BlockSpec((tr, d), lambda i: (i, 0))
    vec = pl.BlockSpec((1, d), lambda i: (0, 0))
    dx, acc = _pcall(
        body, name=name,
        out_shape=(jax.ShapeDtypeStruct((s, d), F32), jax.ShapeDtypeStruct((2, SUBLANES, d), F32)),
        grid=(s // tr,), in_specs=[row, row, row, vec],
        out_specs=(row, pl.BlockSpec((2, SUBLANES, d), lambda i: (0, 0, 0))),
        compiler_params=_params("arbitrary"))(dxa, dh, x, _vec(sc))
    acc = jnp.sum(acc, axis=1)
    return dx, acc[0], acc[1]


def _resid_ln(x, y, g, gamma, beta, name):
    s, d = x.shape
    tr = _row_block(s)

    def body(x_ref, y_ref, g_ref, gam_ref, bet_ref, o_ref):
        u = DN_ALPHA * x_ref[...] + (1.0 + g_ref[...]) * y_ref[...]
        uc = u - jnp.mean(u, axis=-1, keepdims=True)
        var = jnp.mean(uc * uc, axis=-1, keepdims=True)
        o_ref[...] = uc * lax.rsqrt(var + LN_EPS) * gam_ref[...] + bet_ref[...]

    row = pl.BlockSpec((tr, d), lambda i: (i, 0))
    vec = pl.BlockSpec((1, d), lambda i: (0, 0))
    return _pcall(body, name=name, out_shape=jax.ShapeDtypeStruct((s, d), F32), grid=(s // tr,),
                  in_specs=[row, row, vec, vec, vec], out_specs=row,
                  compiler_params=_params("parallel"))(x, y, _vec(g), _vec(gamma), _vec(beta))


def _resid_ln_bwd(x, y, g, gamma, dout, name):
    s, d = x.shape
    tr = _row_block(s)

    def body(x_ref, y_ref, g_ref, gam_ref, do_ref, dxa_ref, dy_ref, acc_ref):
        @pl.when(pl.program_id(0) == 0)
        def _():
            acc_ref[...] = jnp.zeros_like(acc_ref)

        y = y_ref[...]
        gg = 1.0 + g_ref[...]
        u = DN_ALPHA * x_ref[...] + gg * y
        uc = u - jnp.mean(u, axis=-1, keepdims=True)
        rstd = lax.rsqrt(jnp.mean(uc * uc, axis=-1, keepdims=True) + LN_EPS)
        xhat = uc * rstd
        dout = do_ref[...]
        dxh = dout * gam_ref[...]
        du = rstd * (dxh - jnp.mean(dxh, axis=-1, keepdims=True)
                     - xhat * jnp.mean(dxh * xhat, axis=-1, keepdims=True))
        dxa_ref[...] = DN_ALPHA * du
        dy_ref[...] = gg * du
        acc_ref[0] += _fold8(dout * xhat)
        acc_ref[1] += _fold8(dout)
        acc_ref[2] += _fold8(du * y)

    row = pl.BlockSpec((tr, d), lambda i: (i, 0))
    vec = pl.BlockSpec((1, d), lambda i: (0, 0))
    dxa, dy, acc = _pcall(
        body, name=name,
        out_shape=(jax.ShapeDtypeStruct((s, d), F32), jax.ShapeDtypeStruct((s, d), F32),
                   jax.ShapeDtypeStruct((3, SUBLANES, d), F32)),
        grid=(s // tr,), in_specs=[row, row, vec, vec, row],
        out_specs=(row, row, pl.BlockSpec((3, SUBLANES, d), lambda i: (0, 0, 0))),
        compiler_params=_params("arbitrary"))(x, y, _vec(g), _vec(gamma), dout)
    acc = jnp.sum(acc, axis=1)
    return dxa, dy, acc[0], acc[1], acc[2]


def _loss_head(x, target, name):
    s, d = x.shape
    tr = _row_block(s)

    def body(x_ref, t_ref, dx_ref, acc_ref):
        @pl.when(pl.program_id(0) == 0)
        def _():
            acc_ref[...] = jnp.zeros_like(acc_ref)

        e = x_ref[...] - t_ref[...]
        dx_ref[...] = e * (1.0 / d)
        acc_ref[...] += _fold8(e * e)

    row = pl.BlockSpec((tr, d), lambda i: (i, 0))
    dx, acc = _pcall(
        body, name=name,
        out_shape=(jax.ShapeDtypeStruct((s, d), F32), jax.ShapeDtypeStruct((SUBLANES, d), F32)),
        grid=(s // tr,), in_specs=[row, row],
        out_specs=(row, pl.BlockSpec((SUBLANES, d), lambda i: (0, 0))),
        compiler_params=_params("arbitrary"))(x, target)
    return (0.5 / d) * jnp.sum(acc), dx


CONV_STRIPE = 128
CONV_ROWS = 256


def _shift_down(cur, halo, k):
    ext = jnp.concatenate([halo, cur], axis=0)
    return pltpu.roll(ext, k, 0)[SUBLANES:]


def _shift_up(cur, halo, k):
    ext = jnp.concatenate([cur, halo], axis=0)
    n = ext.shape[0]
    return pltpu.roll(ext, n - k, 0)[:n - SUBLANES]


def _gate_chunk(g_ref, r, rc):
    r0 = pl.multiple_of(r * rc, rc)
    cur = g_ref[pl.ds(r0, rc), :]
    hs = pl.multiple_of(jnp.maximum(r0 - SUBLANES, 0), SUBLANES)
    halo = jnp.where(r > 0, g_ref[pl.ds(hs, SUBLANES), :], 0.0)
    return r0, cur, _shift_down(cur, halo, 1), _shift_down(cur, halo, 2)


def _ffn_gate(gu, cw, cb, name):
    s, f2 = gu.shape
    f = f2 // 2
    tc = _pick(f, (CONV_STRIPE,))
    rc = _pick(s, (CONV_ROWS, 128, 64, 32, 16, 8))
    nj = f // tc

    def body(g_ref, u_ref, cw_ref, cb_ref, a_ref):
        w0, w1, w2, b = cw_ref[0:1, :], cw_ref[1:2, :], cw_ref[2:3, :], cb_ref[...]

        def chunk(r, c):
            r0, cur, x1, x2 = _gate_chunk(g_ref, r, rc)
            gc = w2 * cur + w1 * x1 + w0 * x2 + b
            a_ref[pl.ds(r0, rc), :] = (gc * jax.nn.sigmoid(gc) * u_ref[pl.ds(r0, rc), :]).astype(a_ref.dtype)
            return c

        lax.fori_loop(0, s // rc, chunk, 0)

    return _pcall(
        body, name=name, out_shape=jax.ShapeDtypeStruct((s, f), MXU_DTYPE), grid=(nj,),
        in_specs=[pl.BlockSpec((s, tc), lambda j: (0, j)), pl.BlockSpec((s, tc), lambda j: (0, j + nj)),
                  pl.BlockSpec((3, tc), lambda j: (0, j)), pl.BlockSpec((1, tc), lambda j: (0, j))],
        out_specs=pl.BlockSpec((s, tc), lambda j: (0, j)),
        compiler_params=_params("parallel"))(gu, gu, cw, _vec(cb))


def _ffn_gate_bwd(da, gu, cw, cb, name):
    s, f2 = gu.shape
    f = f2 // 2
    tc = _pick(f, (CONV_STRIPE,))
    rc = _pick(s, (CONV_ROWS, 128, 64, 32, 16, 8))
    nj = f // tc
    nr = s // rc

    def body(da_ref, g_ref, u_ref, cw_ref, cb_ref, dg_ref, du_ref, acc_ref, dgc_ref):
        w0, w1, w2, b = cw_ref[0:1, :], cw_ref[1:2, :], cw_ref[2:3, :], cb_ref[...]

        def chunk1(r, carry):
            a0, a1, a2, ab = carry
            r0, cur, x1, x2 = _gate_chunk(g_ref, r, rc)
            gc = w2 * cur + w1 * x1 + w0 * x2 + b
            sg = jax.nn.sigmoid(gc)
            da_c = da_ref[pl.ds(r0, rc), :]
            du_ref[pl.ds(r0, rc), :] = (da_c * (gc * sg)).astype(du_ref.dtype)
            dgc = da_c * u_ref[pl.ds(r0, rc), :] * (sg * (1.0 + gc * (1.0 - sg)))
            dgc_ref[pl.ds(r0, rc), :] = dgc
            return a0 + _fold8(dgc * x2), a1 + _fold8(dgc * x1), a2 + _fold8(dgc * cur), ab + _fold8(dgc)

        zero = jnp.zeros((SUBLANES, tc), F32)
        a0, a1, a2, ab = lax.fori_loop(0, nr, chunk1, (zero, zero, zero, zero))
        acc_ref[0], acc_ref[1], acc_ref[2], acc_ref[3] = a0, a1, a2, ab

        def chunk2(r, c):
            r0 = pl.multiple_of(r * rc, rc)
            cur = dgc_ref[pl.ds(r0, rc), :]
            hs = pl.multiple_of(jnp.minimum(r0 + rc, s - SUBLANES), SUBLANES)
            halo = jnp.where(r < nr - 1, dgc_ref[pl.ds(hs, SUBLANES), :], 0.0)
            dg = w2 * cur + w1 * _shift_up(cur, halo, 1) + w0 * _shift_up(cur, halo, 2)
            dg_ref[pl.ds(r0, rc), :] = dg.astype(dg_ref.dtype)
            return c

        lax.fori_loop(0, nr, chunk2, 0)

    stripe = pl.BlockSpec((s, tc), lambda j: (0, j))
    dg, du, acc = _pcall(
        body, name=name,
        out_shape=(jax.ShapeDtypeStruct((s, f), MXU_DTYPE), jax.ShapeDtypeStruct((s, f), MXU_DTYPE),
                   jax.ShapeDtypeStruct((4, SUBLANES, f), F32)),
        grid=(nj,),
        in_specs=[stripe, stripe, pl.BlockSpec((s, tc), lambda j: (0, j + nj)),
                  pl.BlockSpec((3, tc), lambda j: (0, j)), pl.BlockSpec((1, tc), lambda j: (0, j))],
        out_specs=(stripe, stripe, pl.BlockSpec((4, SUBLANES, tc), lambda j: (0, 0, j))),
        scratch_shapes=[pltpu.VMEM((s, tc), F32)],
        compiler_params=_params("parallel"))(da, gu, gu, cw, _vec(cb))
    acc = jnp.sum(acc, axis=1)
    return dg, du, acc[:3], acc[3]


def _add_rows(a, b, name):
    r = a.shape[0]
    tr = _pick(r, (1024, 512, 256, 128, 64, 32, 16, 8))

    def body(a_ref, b_ref, o_ref):
        o_ref[...] = a_ref[...] + b_ref[...]

    row = pl.BlockSpec((tr, LANES), lambda i: (i, 0))
    return _pcall(body, name=name, out_shape=jax.ShapeDtypeStruct(a.shape, a.dtype), grid=(r // tr,),
                  in_specs=[row, row], out_specs=row, compiler_params=_params("parallel"))(a, b)


def _adamw(gparts, w, m, v, name):
    r = w.shape[0]
    tr = _pick(r, (1024, 512, 256, 128, 64, 32, 16, 8))
    bc1 = 1.0 / (1.0 - ADAM_B1 ** ADAM_STEP)
    bc2 = 1.0 / (1.0 - ADAM_B2 ** ADAM_STEP)
    n = len(gparts)

    def body(*refs):
        w_ref, m_ref, v_ref, go_ref, d_ref, mo_ref, vo_ref = refs[n:]
        g = refs[0][...]
        for t in range(1, n):
            g = g + refs[t][...]
        mn = ADAM_B1 * m_ref[...] + (1.0 - ADAM_B1) * g
        vn = ADAM_B2 * v_ref[...] + (1.0 - ADAM_B2) * (g * g)
        m_hat = mn * bc1
        v_hat = vn * bc2
        go_ref[...] = g
        d_ref[...] = -ADAM_LR * (m_hat / (jnp.sqrt(v_hat) + ADAM_EPS) + ADAM_WD * w_ref[...])
        mo_ref[...] = mn
        vo_ref[...] = vn

    row = pl.BlockSpec((tr, LANES), lambda i: (i, 0))
    sds = jax.ShapeDtypeStruct((r, LANES), F32)
    return _pcall(
        body, name=name, out_shape=(sds, sds, sds, sds),
        grid=(r // tr,),
        in_specs=[pl.BlockSpec((None, tr, LANES), functools.partial(lambda slot, i: (slot, i, 0), slot))
                  for _, slot in gparts] + [row, row, row],
        out_specs=(row, row, row, row),
        compiler_params=_params("parallel"),
    )(*[a for a, _ in gparts], w, m, v)


def _sum_slots(gslots, name):
    n, r, _ = gslots.shape
    tr = _pick(r, (1024, 512, 256, 128, 64, 32, 16, 8))

    def body(g_ref, o_ref):
        g = g_ref[0]
        for t in range(1, n):
            g = g + g_ref[t]
        o_ref[...] = g

    return _pcall(
        body, name=name, out_shape=jax.ShapeDtypeStruct((r, LANES), F32),
        grid=(r // tr,),
        in_specs=[pl.BlockSpec((n, tr, LANES), lambda i: (0, i, 0))],
        out_specs=pl.BlockSpec((tr, LANES), lambda i: (i, 0)),
        compiler_params=_params("parallel"),
    )(gslots)


def _l2norm(x, eps=1e-6):
    return x * lax.rsqrt(jnp.sum(x * x, axis=-1, keepdims=True) + eps)


def _chunk_heads(t, n_heads, chunk):
    s, hd = t.shape
    return t.reshape(s // chunk, chunk, n_heads, hd // n_heads).transpose(2, 0, 1, 3)


def _unchunk_heads(t):
    h, n, c, d = t.shape
    return t.transpose(1, 2, 0, 3).reshape(n * c, h, d)


_NT = (((1,), (1,)), ((), ()))
_TN = (((0,), (0,)), ((), ()))


def _gdn_blocks(a, rev_from=None):
    h, _, r, c = a.shape
    if rev_from is None:
        return pl.BlockSpec((h, None, r, c), lambda n: (0, n, 0, 0))
    return pl.BlockSpec((h, None, r, c), lambda n: (0, rev_from - n, 0, 0))


def _gdn_scan_fwd_call(qg, w, u, qk, kd, e):
    H, n_chunks, c, dk = qg.shape
    dv = u.shape[-1]

    def body(qg_ref, w_ref, u_ref, qk_ref, kd_ref, e_ref, o_ref, sin_ref, vn_ref, state_ref):
        @pl.when(pl.program_id(0) == 0)
        def _():
            state_ref[...] = jnp.zeros_like(state_ref)

        for hd in range(H):
            st = state_ref[hd]
            stb = st.astype(MXU_DTYPE)
            sin_ref[hd] = st
            v_new = u_ref[hd] - jnp.dot(w_ref[hd].astype(MXU_DTYPE), stb, preferred_element_type=F32)
            vn_ref[hd] = v_new
            vnb = v_new.astype(MXU_DTYPE)
            o_ref[hd] = (jnp.dot(qg_ref[hd].astype(MXU_DTYPE), stb, preferred_element_type=F32)
                         + jnp.dot(qk_ref[hd].astype(MXU_DTYPE), vnb, preferred_element_type=F32))
            state_ref[hd] = st * e_ref[hd] + lax.dot_general(kd_ref[hd].astype(MXU_DTYPE), vnb, _TN,
                                                             preferred_element_type=F32)

    ins = (qg, w, u, qk, kd, e)
    outs = (jax.ShapeDtypeStruct((H, n_chunks, c, dv), F32), jax.ShapeDtypeStruct((H, n_chunks, dk, dv), F32),
            jax.ShapeDtypeStruct((H, n_chunks, c, dv), F32))
    return _pcall(
        body, name="gdn_scan_fwd", out_shape=outs, grid=(n_chunks,),
        in_specs=[_gdn_blocks(a) for a in ins], out_specs=tuple(_gdn_blocks(a) for a in outs),
        scratch_shapes=[pltpu.VMEM((H, dk, dv), F32)],
        compiler_params=_params("arbitrary"),
    )(*ins)


def _gdn_scan_bwd_call(qg, w, qk, kd, e, s_in, v_new, do):
    H, n_chunks, c, dk = qg.shape
    dv = v_new.shape[-1]

    def body(qg_ref, w_ref, qk_ref, kd_ref, e_ref, sin_ref, vn_ref, do_ref,
             dqg_ref, dw_ref, du_ref, dqk_ref, dkd_ref, de_ref, ds_ref):
        @pl.when(pl.program_id(0) == 0)
        def _():
            ds_ref[...] = jnp.zeros_like(ds_ref)

        for hd in range(H):
            st = sin_ref[hd]
            stb = st.astype(MXU_DTYPE)
            vnb = vn_ref[hd].astype(MXU_DTYPE)
            dob = do_ref[hd].astype(MXU_DTYPE)
            ds = ds_ref[hd]
            dsb = ds.astype(MXU_DTYPE)
            dvn = lax.dot_general(qk_ref[hd].astype(MXU_DTYPE), dob, _TN, preferred_element_type=F32)
            dqk_ref[hd] = lax.dot_general(dob, vnb, _NT, preferred_element_type=F32)
            dqg_ref[hd] = lax.dot_general(dob, stb, _NT, preferred_element_type=F32)
            ds_in = lax.dot_general(qg_ref[hd].astype(MXU_DTYPE), dob, _TN, preferred_element_type=F32)
            dvn = dvn + jnp.dot(kd_ref[hd].astype(MXU_DTYPE), dsb, preferred_element_type=F32)
            dkd_ref[hd] = lax.dot_general(vnb, dsb, _NT, preferred_element_type=F32)
            de_ref[hd] = _fold8(st * ds)
            ds_in = ds_in + ds * e_ref[hd]
            du_ref[hd] = dvn
            dvnb = dvn.astype(MXU_DTYPE)
            dw_ref[hd] = -lax.dot_general(dvnb, stb, _NT, preferred_element_type=F32)
            ds_ref[hd] = ds_in - lax.dot_general(w_ref[hd].astype(MXU_DTYPE), dvnb, _TN,
                                                 preferred_element_type=F32)

    last = n_chunks - 1
    ins = (qg, w, qk, kd, e, s_in, v_new, do)
    outs = (jax.ShapeDtypeStruct(qg.shape, F32), jax.ShapeDtypeStruct(w.shape, F32),
            jax.ShapeDtypeStruct(v_new.shape, F32), jax.ShapeDtypeStruct(qk.shape, F32),
            jax.ShapeDtypeStruct(kd.shape, F32), jax.ShapeDtypeStruct((H, n_chunks, SUBLANES, dv), F32))
    return _pcall(
        body, name="gdn_scan_bwd", out_shape=outs, grid=(n_chunks,),
        in_specs=[_gdn_blocks(a, last) for a in ins], out_specs=tuple(_gdn_blocks(a, last) for a in outs),
        scratch_shapes=[pltpu.VMEM((H, dk, dv), F32)],
        compiler_params=_params("arbitrary"),
    )(*ins)


@jax.custom_vjp
def _gdn_scan(qg, w, u, qk, kd, e):
    return _gdn_scan_fwd_call(qg, w, u, qk, kd, e)[0]


def _gdn_scan_fwd(qg, w, u, qk, kd, e):
    o, s_in, v_new = _gdn_scan_fwd_call(qg, w, u, qk, kd, e)
    return o, (qg, w, qk, kd, e, s_in, v_new)


def _gdn_scan_bwd(res, do):
    dqg, dw, du, dqk, dkd, de = _gdn_scan_bwd_call(*res, do)
    return dqg, dw, du, dqk, dkd, jnp.sum(de, axis=2, keepdims=True)


_gdn_scan.defvjp(_gdn_scan_fwd, _gdn_scan_bwd)


def _conv_taps(x_ref, r, rc, taps):
    r0 = pl.multiple_of(r * rc, rc)
    cur = x_ref[pl.ds(r0, rc), :]
    hs = pl.multiple_of(jnp.maximum(r0 - SUBLANES, 0), SUBLANES)
    halo = jnp.where(r > 0, x_ref[pl.ds(hs, SUBLANES), :], 0.0)
    return r0, [cur] + [_shift_down(cur, halo, k) for k in range(1, taps)]


def _conv_pre(w_ref, xs):
    taps = len(xs)
    gc = w_ref[taps - 1:taps, :] * xs[0]
    for j in range(taps - 1):
        gc = gc + w_ref[j:j + 1, :] * xs[taps - 1 - j]
    return gc


def _dwconv_silu_fwd_call(x, w):
    s, c = x.shape
    taps = w.shape[0]
    tc = _pick(c, (CONV_STRIPE,))
    rc = _pick(s, (CONV_ROWS, 128, 64, 32, 16, 8))

    def body(x_ref, w_ref, o_ref):
        def chunk(r, carry):
            r0, xs = _conv_taps(x_ref, r, rc, taps)
            gc = _conv_pre(w_ref, xs)
            o_ref[pl.ds(r0, rc), :] = gc * jax.nn.sigmoid(gc)
            return carry

        lax.fori_loop(0, s // rc, chunk, 0)

    stripe = pl.BlockSpec((s, tc), lambda j: (0, j))
    return _pcall(body, name="gdn_conv_fwd", out_shape=jax.ShapeDtypeStruct((s, c), F32), grid=(c // tc,),
                  in_specs=[stripe, pl.BlockSpec((taps, tc), lambda j: (0, j))], out_specs=stripe,
                  compiler_params=_params("parallel"))(x, w)


def _dwconv_silu_bwd_call(x, w, dy):
    s, c = x.shape
    taps = w.shape[0]
    tc = _pick(c, (CONV_STRIPE,))
    rc = _pick(s, (CONV_ROWS, 128, 64, 32, 16, 8))
    nr = s // rc

    def body(x_ref, w_ref, dy_ref, dx_ref, acc_ref, dgc_ref):
        def chunk1(r, acc):
            r0, xs = _conv_taps(x_ref, r, rc, taps)
            gc = _conv_pre(w_ref, xs)
            sg = jax.nn.sigmoid(gc)
            dgc = dy_ref[pl.ds(r0, rc), :] * (sg * (1.0 + gc * (1.0 - sg)))
            dgc_ref[pl.ds(r0, rc), :] = dgc
            return tuple(acc[j] + _fold8(dgc * xs[taps - 1 - j]) for j in range(taps))

        zero = jnp.zeros((SUBLANES, tc), F32)
        acc = lax.fori_loop(0, nr, chunk1, (zero,) * taps)
        for j in range(taps):
            acc_ref[j] = acc[j]

        def chunk2(r, carry):
            r0 = pl.multiple_of(r * rc, rc)
            cur = dgc_ref[pl.ds(r0, rc), :]
            hs = pl.multiple_of(jnp.minimum(r0 + rc, s - SUBLANES), SUBLANES)
            halo = jnp.where(r < nr - 1, dgc_ref[pl.ds(hs, SUBLANES), :], 0.0)
            dx = w_ref[taps - 1:taps, :] * cur
            for j in range(taps - 1):
                dx = dx + w_ref[j:j + 1, :] * _shift_up(cur, halo, taps - 1 - j)
            dx_ref[pl.ds(r0, rc), :] = dx
            return carry

        lax.fori_loop(0, nr, chunk2, 0)

    stripe = pl.BlockSpec((s, tc), lambda j: (0, j))
    dx, acc = _pcall(
        body, name="gdn_conv_bwd",
        out_shape=(jax.ShapeDtypeStruct((s, c), F32), jax.ShapeDtypeStruct((taps, SUBLANES, c), F32)),
        grid=(c // tc,),
        in_specs=[stripe, pl.BlockSpec((taps, tc), lambda j: (0, j)), stripe],
        out_specs=(stripe, pl.BlockSpec((taps, SUBLANES, tc), lambda j: (0, 0, j))),
        scratch_shapes=[pltpu.VMEM((s, tc), F32)],
        compiler_params=_params("parallel"))(x, w, dy)
    return dx, jnp.sum(acc, axis=1)


@jax.custom_vjp
def _dwconv_silu(x, w):
    return _dwconv_silu_fwd_call(x, w)


def _dwconv_silu_fwd(x, w):
    return _dwconv_silu_fwd_call(x, w), (x, w)


def _dwconv_silu_bwd(res, dy):
    return _dwconv_silu_bwd_call(*res, dy)


_dwconv_silu.defvjp(_dwconv_silu_fwd, _dwconv_silu_bwd)


def _gated_deltanet(hx, p):
    H, C = GDN_HEADS, GDN_CHUNK
    s, d = hx.shape
    dk = dv = d // H
    qkvz = _linear(hx, p['gdn_w_qkvz'], "gdn_in")
    ab = _linear(hx, p['gdn_w_ab'], "gdn_ab")
    qkv, z = qkvz[:, :3 * d], qkvz[:, 3 * d:]
    a, bt = ab[:, :H], ab[:, H:2 * H]
    qkv = _dwconv_silu(qkv, p['gdn_conv_w'])
    q, k, v = qkv[:, :d], qkv[:, d:2 * d], qkv[:, 2 * d:]
    q = _l2norm(_chunk_heads(q, H, C)) * (dk ** -0.5)
    k = _l2norm(_chunk_heads(k, H, C))
    v = _chunk_heads(v, H, C)
    beta = jax.nn.sigmoid(_chunk_heads(bt, H, C)[..., 0])
    g = -jnp.exp(p['gdn_a_log'])[:, None, None] * jax.nn.softplus(
        _chunk_heads(a, H, C)[..., 0] + p['gdn_dt_bias'][:, None, None])
    gc = jnp.cumsum(g, axis=-1)
    idx = jnp.arange(C)
    causal = idx[:, None] >= idx[None, :]
    strict = idx[:, None] > idx[None, :]
    diff = gc[..., :, None] - gc[..., None, :]
    decay = jnp.where(causal, jnp.exp(jnp.where(causal, diff, 0.0)), 0.0)
    kb = k * beta[..., None]
    kk = jnp.where(strict, jnp.einsum('hncd,hnmd->hncm', kb, k) * decay, 0.0)
    eye = jnp.eye(C, dtype=F32)
    rhs = jnp.concatenate([v * beta[..., None], kb * jnp.exp(gc)[..., None]], axis=-1)
    sol = lax.linalg.triangular_solve(kk + eye, rhs, left_side=True, lower=True, unit_diagonal=True)
    u, w = sol[..., :dv], sol[..., dv:]
    qk = jnp.where(causal, jnp.einsum('hncd,hnmd->hncm', q, k) * decay, 0.0)

    g_last = gc[..., -1:]
    e = jnp.broadcast_to(jnp.exp(g_last)[..., None], gc.shape[:2] + (1, dv))
    o = _gdn_scan(q * jnp.exp(gc)[..., None], w, u, qk, k * jnp.exp(g_last - gc)[..., None], e)
    o = _unchunk_heads(o)
    o = o * lax.rsqrt(jnp.mean(o * o, axis=-1, keepdims=True) + 1e-6) * p['gdn_norm_w']
    o = o * jax.nn.silu(z.reshape(s, H, dv))
    return _linear(o.reshape(s, H * dv), p['gdn_w_out'], "gdn_out")


def _ret_consts(c):
    log_gamma = jnp.log(1.0 - jnp.power(2.0, -5.0 - jnp.arange(RET_HEADS, dtype=F32)))
    idx = jnp.arange(c, dtype=F32)
    rel = idx[:, None] - idx[None, :]
    dmask = jnp.where(rel >= 0, jnp.exp(jnp.maximum(rel, 0.0) * log_gamma[:, None, None]), 0.0)
    zeta = jnp.exp((c - 1.0 - idx)[None, :] * log_gamma[:, None])[..., None]
    xi = jnp.exp((idx + 1.0)[None, :] * log_gamma[:, None])[..., None]
    gamma_c = jnp.exp(c * log_gamma)[:, None, None]
    return dmask, zeta, xi, gamma_c


def _ret_angles(s, dk):
    pos = jnp.arange(s, dtype=F32)
    inv_freq = RET_ROPE_BASE ** (-jnp.linspace(0.0, 1.0, dk // 2, dtype=F32))
    ang = pos[:, None] * inv_freq[None, :]
    return jnp.cos(ang), jnp.sin(ang)


def _rot(t, cs, sn):
    half = t.shape[1] // 2
    t1, t2 = t[:, :half], t[:, half:]
    return jnp.concatenate([t1 * cs - t2 * sn, t1 * sn + t2 * cs], axis=1)


def _rot_t(t, cs, sn):
    half = t.shape[1] // 2
    t1, t2 = t[:, :half], t[:, half:]
    return jnp.concatenate([t1 * cs + t2 * sn, t2 * cs - t1 * sn], axis=1)


def _ret_cols(d, dk, dv, hd):
    return (slice(hd * dk, (hd + 1) * dk), slice(d + hd * dk, d + (hd + 1) * dk),
            slice(2 * d + hd * dv, 2 * d + (hd + 1) * dv), slice(4 * d + hd * dv, 4 * d + (hd + 1) * dv))


def _ret_fwd_call(proj):
    s, d6 = proj.shape
    d = d6 // 6
    H, c = RET_HEADS, RET_CHUNK
    dk, dv = d // H, 2 * d // H
    n_chunks = s // c
    kscale = dk ** -0.5
    cos_a, sin_a = _ret_angles(s, dk)
    consts = _ret_consts(c)

    def body(p_ref, cos_ref, sin_ref, dm_ref, ze_ref, xi_ref, gc_ref, out_ref, oraw_ref, st_ref, state_ref):
        @pl.when(pl.program_id(0) == 0)
        def _():
            state_ref[...] = jnp.zeros_like(state_ref)

        cs, sn = cos_ref[...], sin_ref[...]
        for hd in range(H):
            qc, kc, vc, gcol = _ret_cols(d, dk, dv, hd)
            ocol = slice(hd * dv, (hd + 1) * dv)
            qb = _rot(p_ref[:, qc], cs, sn).astype(MXU_DTYPE)
            kr = _rot(p_ref[:, kc], cs, sn) * kscale
            kb = kr.astype(MXU_DTYPE)
            vb = p_ref[:, vc].astype(MXU_DTYPE)
            st = state_ref[hd]
            stb = st.astype(MXU_DTYPE)
            st_ref[hd] = stb
            sc = lax.dot_general(qb, kb, _NT, preferred_element_type=F32) * dm_ref[hd]
            o = (jnp.dot(sc.astype(MXU_DTYPE), vb, preferred_element_type=F32)
                 + jnp.dot(qb, stb, preferred_element_type=F32) * xi_ref[hd])
            state_ref[hd] = st * gc_ref[hd] + lax.dot_general((kr * ze_ref[hd]).astype(MXU_DTYPE), vb, _TN,
                                                              preferred_element_type=F32)
            oraw_ref[:, ocol] = o
            oc = o - jnp.mean(o, axis=-1, keepdims=True)
            on = oc * lax.rsqrt(jnp.mean(oc * oc, axis=-1, keepdims=True) + 1e-6)
            gate = p_ref[:, gcol]
            out_ref[:, ocol] = on * (gate * jax.nn.sigmoid(gate))

    row = lambda width: pl.BlockSpec((c, width), lambda n: (n, 0))
    whole = lambda a: pl.BlockSpec(a.shape, lambda n: (0,) * a.ndim)
    return _pcall(
        body, name="ret_fwd",
        out_shape=(jax.ShapeDtypeStruct((s, 2 * d), F32), jax.ShapeDtypeStruct((s, 2 * d), F32),
                   jax.ShapeDtypeStruct((n_chunks, H, dk, dv), MXU_DTYPE)),
        grid=(n_chunks,),
        in_specs=[row(d6), row(dk // 2), row(dk // 2)] + [whole(a) for a in consts],
        out_specs=(row(2 * d), row(2 * d), pl.BlockSpec((None, H, dk, dv), lambda n: (n, 0, 0, 0))),
        scratch_shapes=[pltpu.VMEM((H, dk, dv), F32)],
        compiler_params=_params("arbitrary"),
    )(proj, cos_a, sin_a, *consts)


def _ret_bwd_call(proj, oraw, states, dout):
    s, d6 = proj.shape
    d = d6 // 6
    H, c = RET_HEADS, RET_CHUNK
    dk, dv = d // H, 2 * d // H
    n_chunks = s // c
    kscale = dk ** -0.5
    cos_a, sin_a = _ret_angles(s, dk)
    consts = _ret_consts(c)

    def body(p_ref, cos_ref, sin_ref, dm_ref, ze_ref, xi_ref, gc_ref, oraw_ref, st_ref, do_ref, dp_ref, ds_ref):
        @pl.when(pl.program_id(0) == 0)
        def _():
            ds_ref[...] = jnp.zeros_like(ds_ref)

        cs, sn = cos_ref[...], sin_ref[...]
        for hd in range(H):
            qc, kc, vc, gcol = _ret_cols(d, dk, dv, hd)
            ocol = slice(hd * dv, (hd + 1) * dv)
            qb = _rot(p_ref[:, qc], cs, sn).astype(MXU_DTYPE)
            kr = _rot(p_ref[:, kc], cs, sn) * kscale
            kb = kr.astype(MXU_DTYPE)
            vb = p_ref[:, vc].astype(MXU_DTYPE)
            gate = p_ref[:, gcol]
            o = oraw_ref[:, ocol]
            oc = o - jnp.mean(o, axis=-1, keepdims=True)
            rstd = lax.rsqrt(jnp.mean(oc * oc, axis=-1, keepdims=True) + 1e-6)
            on = oc * rstd
            dout_h = do_ref[:, ocol]
            sg = jax.nn.sigmoid(gate)
            dp_ref[:, gcol] = dout_h * on * (sg * (1.0 + gate * (1.0 - sg)))
            don = dout_h * (gate * sg)
            do_raw = rstd * (don - jnp.mean(don, axis=-1, keepdims=True)
                             - on * jnp.mean(don * on, axis=-1, keepdims=True))
            dob = do_raw.astype(MXU_DTYPE)
            stb = st_ref[hd]
            ds = ds_ref[hd]
            dsb = ds.astype(MXU_DTYPE)
            dm = dm_ref[hd]
            scb = (lax.dot_general(qb, kb, _NT, preferred_element_type=F32) * dm).astype(MXU_DTYPE)
            dsc = (lax.dot_general(dob, vb, _NT, preferred_element_type=F32) * dm).astype(MXU_DTYPE)
            dqr = jnp.dot(dsc, kb, preferred_element_type=F32)
            dkr = lax.dot_general(dsc, qb, _TN, preferred_element_type=F32)
            dvv = lax.dot_general(scb, dob, _TN, preferred_element_type=F32)
            doi = (do_raw * xi_ref[hd]).astype(MXU_DTYPE)
            dqr = dqr + lax.dot_general(doi, stb, _NT, preferred_element_type=F32)
            ds_in = lax.dot_general(qb, doi, _TN, preferred_element_type=F32)
            ze = ze_ref[hd]
            dkr = dkr + lax.dot_general(vb, dsb, _NT, preferred_element_type=F32) * ze
            dvv = dvv + jnp.dot((kr * ze).astype(MXU_DTYPE), dsb, preferred_element_type=F32)
            ds_ref[hd] = ds * gc_ref[hd] + ds_in
            dp_ref[:, qc] = _rot_t(dqr, cs, sn)
            dp_ref[:, kc] = _rot_t(dkr * kscale, cs, sn)
            dp_ref[:, vc] = dvv

    last = n_chunks - 1
    row = lambda width: pl.BlockSpec((c, width), lambda n: (last - n, 0))
    whole = lambda a: pl.BlockSpec(a.shape, lambda n: (0,) * a.ndim)
    return _pcall(
        body, name="ret_bwd", out_shape=jax.ShapeDtypeStruct((s, d6), F32),
        grid=(n_chunks,),
        in_specs=[row(d6), row(dk // 2), row(dk // 2)] + [whole(a) for a in consts]
                 + [row(2 * d), pl.BlockSpec((None, H, dk, dv), lambda n: (last - n, 0, 0, 0)), row(2 * d)],
        out_specs=row(d6),
        scratch_shapes=[pltpu.VMEM((H, dk, dv), F32)],
        compiler_params=_params("arbitrary"),
    )(proj, cos_a, sin_a, *consts, oraw, states, dout)


@jax.custom_vjp
def _ret_core(proj):
    return _ret_fwd_call(proj)[0]


def _ret_core_fwd(proj):
    out, oraw, states = _ret_fwd_call(proj)
    return out, (proj, oraw, states)


def _ret_core_bwd(res, dout):
    return (_ret_bwd_call(*res, dout),)


_ret_core.defvjp(_ret_core_fwd, _ret_core_bwd)


def _retention(hx, p):
    return _linear(_ret_core(_linear(hx, p['ret_w_in'], "ret_in")), p['ret_w_out'], "ret_out")


SQRT_HALF = 2.0 ** -0.5
INV_SQRT_2PI = (2.0 * math.pi) ** -0.5


def _gmlp_front(p_ref, g_ref, b_ref, w):
    x = p_ref[...]
    cdf = 0.5 * (1.0 + lax.erf(x * SQRT_HALF))
    uv = x * cdf
    u, v = uv[:, :w], uv[:, w:]
    vc = v - jnp.mean(v, axis=-1, keepdims=True)
    rstd = lax.rsqrt(jnp.mean(vc * vc, axis=-1, keepdims=True) + LN_EPS)
    vhat = vc * rstd
    return x, cdf, u, vhat, rstd, vhat * g_ref[...] + b_ref[...]


def _gmlp_fwd_call(proj, ln_g, ln_b, ws, bs):
    s, w2 = proj.shape
    w = w2 // 2
    c, G = GMLP_CHUNK, GMLP_GROUPS
    gw = w // G

    def body(p_ref, g_ref, b_ref, ws_ref, bs_ref, o_ref):
        _, _, u, _, _, vn = _gmlp_front(p_ref, g_ref, b_ref, w)
        for gi in range(G):
            cols = slice(gi * gw, (gi + 1) * gw)
            vs = jnp.dot(ws_ref[gi].astype(MXU_DTYPE), vn[:, cols].astype(MXU_DTYPE),
                         preferred_element_type=F32) + bs_ref[gi]
            o_ref[:, cols] = u[:, cols] * vs

    whole = lambda a: pl.BlockSpec(a.shape, lambda n: (0,) * a.ndim)
    args = (_vec(ln_g), _vec(ln_b), ws, bs)
    return _pcall(
        body, name="gmlp_fwd", out_shape=jax.ShapeDtypeStruct((s, w), F32), grid=(s // c,),
        in_specs=[pl.BlockSpec((c, w2), lambda n: (n, 0))] + [whole(a) for a in args],
        out_specs=pl.BlockSpec((c, w), lambda n: (n, 0)),
        compiler_params=_params("parallel"),
    )(proj, *args)


def _gmlp_bwd_call(proj, ln_g, ln_b, ws, bs, dout):
    s, w2 = proj.shape
    w = w2 // 2
    c, G = GMLP_CHUNK, GMLP_GROUPS
    gw = w // G

    def body(p_ref, g_ref, b_ref, ws_ref, bs_ref, do_ref, dp_ref, dws_ref, dbs_ref, dgb_ref):
        @pl.when(pl.program_id(0) == 0)
        def _():
            dws_ref[...] = jnp.zeros_like(dws_ref)
            dbs_ref[...] = jnp.zeros_like(dbs_ref)
            dgb_ref[...] = jnp.zeros_like(dgb_ref)

        x, cdf, u, vhat, rstd, vn = _gmlp_front(p_ref, g_ref, b_ref, w)
        dout = do_ref[...]
        du_parts, dvn_parts = [], []
        for gi in range(G):
            cols = slice(gi * gw, (gi + 1) * gw)
            wsg = ws_ref[gi].astype(MXU_DTYPE)
            vng = vn[:, cols].astype(MXU_DTYPE)
            vs = jnp.dot(wsg, vng, preferred_element_type=F32) + bs_ref[gi]
            du_parts.append(dout[:, cols] * vs)
            dvs = dout[:, cols] * u[:, cols]
            dbs_ref[:, cols] += dvs
            dvsb = dvs.astype(MXU_DTYPE)
            dws_ref[gi] += lax.dot_general(dvsb, vng, _NT, preferred_element_type=F32)
            dvn_parts.append(lax.dot_general(wsg, dvsb, _TN, preferred_element_type=F32))
        dvn = jnp.concatenate(dvn_parts, axis=1)
        dgb_ref[0] += _fold8(dvn * vhat)
        dgb_ref[1] += _fold8(dvn)
        dvh = dvn * g_ref[...]
        dv = rstd * (dvh - jnp.mean(dvh, axis=-1, keepdims=True)
                     - vhat * jnp.mean(dvh * vhat, axis=-1, keepdims=True))
        duv = jnp.concatenate(du_parts + [dv], axis=1)
        dp_ref[...] = duv * (cdf + x * (jnp.exp(-0.5 * x * x) * INV_SQRT_2PI))

    whole = lambda a: pl.BlockSpec(a.shape, lambda n: (0,) * a.ndim)
    args = (_vec(ln_g), _vec(ln_b), ws, bs)
    acc = lambda *shape: pl.BlockSpec(shape, lambda n: (0,) * len(shape))
    return _pcall(
        body, name="gmlp_bwd",
        out_shape=(jax.ShapeDtypeStruct((s, w2), F32), jax.ShapeDtypeStruct((G, c, c), F32),
                   jax.ShapeDtypeStruct((c, w), F32), jax.ShapeDtypeStruct((2, SUBLANES, w), F32)),
        grid=(s // c,),
        in_specs=[pl.BlockSpec((c, w2), lambda n: (n, 0))] + [whole(a) for a in args]
                 + [pl.BlockSpec((c, w), lambda n: (n, 0))],
        out_specs=(pl.BlockSpec((c, w2), lambda n: (n, 0)), acc(G, c, c), acc(c, w), acc(2, SUBLANES, w)),
        compiler_params=_params("arbitrary"),
    )(proj, *args, dout)


def _gmlp_mask(c):
    return jnp.tril(jnp.ones((c, c), dtype=bool))


@jax.custom_vjp
def _gmlp_core(proj, ln_g, ln_b, w_s, b_s):
    ws = jnp.where(_gmlp_mask(GMLP_CHUNK), w_s, 0.0)
    return _gmlp_fwd_call(proj, ln_g, ln_b, ws, b_s[..., None])


def _gmlp_core_fwd(proj, ln_g, ln_b, w_s, b_s):
    return _gmlp_core(proj, ln_g, ln_b, w_s, b_s), (proj, ln_g, ln_b, w_s, b_s)


def _gmlp_core_bwd(res, dout):
    proj, ln_g, ln_b, w_s, b_s = res
    mask = _gmlp_mask(GMLP_CHUNK)
    dproj, dws, dbs, dgb = _gmlp_bwd_call(proj, ln_g, ln_b, jnp.where(mask, w_s, 0.0), b_s[..., None], dout)
    dgb = jnp.sum(dgb, axis=1)
    c = GMLP_CHUNK
    db_s = jnp.sum(dbs.reshape(c, GMLP_GROUPS, -1), axis=-1).T
    return dproj, dgb[0], dgb[1], jnp.where(mask, dws, 0.0), db_s


_gmlp_core.defvjp(_gmlp_core_fwd, _gmlp_core_bwd)


def _chunked_gmlp(hx, p):
    core = _gmlp_core(_linear(hx, p['gmlp_w_in'], "gmlp_in"), p['gmlp_ln_g'], p['gmlp_ln_b'],
                      p['gmlp_w_s'], p['gmlp_b_s'])
    return _linear(core, p['gmlp_w_out'], "gmlp_out")


def _stick_breaking(hx, p):
    H = SB_HEADS
    s, d = hx.shape
    dh = d // H
    qkv = _linear(hx, p['sb_w_in'], "sb_in")
    q, k, v = (qkv[:, j * d:(j + 1) * d].reshape(s, H, dh).transpose(1, 0, 2) for j in range(3))
    o = _sb_core(q, k, v)
    return _linear(o.transpose(1, 0, 2).reshape(s, d), p['sb_w_out'], "sb_out")


MIXERS = (_gated_deltanet, _retention, _chunked_gmlp, _stick_breaking)


def _trunk_grad(x, mods, p, target):
    d = x.shape[-1]
    saved = []
    for i in range(DEPTH):
        sh1, sc1, g1, sh2, sc2, g2 = (mods[i, j * d:(j + 1) * d] for j in range(6))
        h1 = _modulate(x, sc1, sh1, F32, "mod_a%d" % i)
        y1, mixer_vjp = jax.vjp(MIXERS[i], h1, {n: p[n] for n in MIXER_PARAMS[i]})
        x1 = _resid_ln(x, y1, g1, p['ln_g'][i, 0], p['ln_b'][i, 0], "ln_a%d" % i)
        h2 = _modulate(x1, sc2, sh2, MXU_DTYPE, "mod_b%d" % i)
        gu = _mm(h2, p['ffn_up'][i], 'nn', "ffn_up%d_fwd" % i)
        act = _ffn_gate(gu, p['ffn_conv_w'][i], p['ffn_conv_b'][i], "ffn_gate%d" % i)
        y2 = _mm(act, p['ffn_down'][i], 'nn', "ffn_down%d_fwd" % i)
        x2 = _resid_ln(x1, y2, g2, p['ln_g'][i, 1], p['ln_b'][i, 1], "ln_b%d" % i)
        saved.append((x, y1, mixer_vjp, x1, h2, gu, act, y2))
        x = x2
    loss, dx = _loss_head(x, target, "loss_head")

    dp = {n: None for n in p}
    d_ln_g, d_ln_b, d_up, d_down, d_cw, d_cb, dmods = [], [], [], [], [], [], []
    for i in reversed(range(DEPTH)):
        x0, y1, mixer_vjp, x1, h2, gu, act, y2 = saved[i]
        sh1, sc1, g1, sh2, sc2, g2 = (mods[i, j * d:(j + 1) * d] for j in range(6))
        dxa, dy2, dgam2, dbet2, dg2 = _resid_ln_bwd(x1, y2, g2, p['ln_g'][i, 1], dx, "ln_b%d_bwd" % i)
        dact = _mm(dy2, p['ffn_down'][i], 'nt', "ffn_down%d_dx" % i)
        d_down.append(_mm(act, dy2, 'tn', "ffn_down%d_dw" % i))
        dgate, dupp, dcw, dcb = _ffn_gate_bwd(dact, gu, p['ffn_conv_w'][i], p['ffn_conv_b'][i],
                                              "ffn_gate%d_bwd" % i)
        dgu = jnp.concatenate([dgate, dupp], axis=1)
        dh2 = _mm(dgu, p['ffn_up'][i], 'nt', "ffn_up%d_dx" % i)
        d_up.append(_mm(h2, dgu, 'tn', "ffn_up%d_dw" % i))
        dx1, dsc2, dsh2 = _modulate_bwd(dxa, dh2, x1, sc2, "mod_b%d_bwd" % i)
        dxa, dy1, dgam1, dbet1, dg1 = _resid_ln_bwd(x0, y1, g1, p['ln_g'][i, 0], dx1, "ln_a%d_bwd" % i)
        dh1, dmix = mixer_vjp(dy1)
        dp.update(dmix)
        dx, dsc1, dsh1 = _modulate_bwd(dxa, dh1, x0, sc1, "mod_a%d_bwd" % i)
        d_ln_g.append(jnp.stack([dgam1, dgam2]))
        d_ln_b.append(jnp.stack([dbet1, dbet2]))
        d_cw.append(dcw)
        d_cb.append(dcb)
        dmods.append(jnp.concatenate([dsh1, dsc1, dg1, dsh2, dsc2, dg2]))
    for n, parts in (('ln_g', d_ln_g), ('ln_b', d_ln_b), ('ffn_up', d_up), ('ffn_down', d_down),
                     ('ffn_conv_w', d_cw), ('ffn_conv_b', d_cb)):
        dp[n] = jnp.stack(parts[::-1])
    return loss, dx, jnp.stack(dmods[::-1]), dp


def _join(blocks, axis):
    return jnp.concatenate([blocks[d] for d in range(N_DEV)], axis=axis)


def _split(whole, axis):
    n = whole.shape[axis] // N_DEV
    return jnp.stack([lax.slice_in_dim(whole, d * n, (d + 1) * n, axis=axis) for d in range(N_DEV)])


def _pad8(a):
    pad = (-a.shape[0]) % 8
    return jnp.pad(a, ((0, pad), (0, 0))) if pad else a


def _pack_big_grads(full_grads, axes):
    per_dev = jnp.concatenate([_split(g, ax).reshape(N_DEV, -1) for g, ax in zip(full_grads, axes)], axis=1)
    pad = (-per_dev.shape[1]) % (BIG_ROW_ALIGN * LANES)
    if pad:
        per_dev = jnp.pad(per_dev, ((0, 0), (0, pad)))
    return per_dev.reshape(N_DEV, -1, LANES)


def kernel(x, c, cond_w, cond_b, ada_w, ada_b, ln_g, ln_b, ffn_up, ffn_conv_w, ffn_conv_b, ffn_down, gdn_w_in, gdn_conv_w, gdn_a_log, gdn_dt_bias, gdn_norm_w, gdn_w_out, ret_w_in, ret_w_out, gmlp_w_in, gmlp_ln_g, gmlp_ln_b, gmlp_w_s, gmlp_b_s, gmlp_w_out, sb_w_in, sb_w_out, loss_target, m_cond_w, m_cond_b, m_ada_w, m_ada_b, m_ln_g, m_ln_b, m_ffn_up, m_ffn_conv_w, m_ffn_conv_b, m_ffn_down, m_gdn_w_in, m_gdn_conv_w, m_gdn_a_log, m_gdn_dt_bias, m_gdn_norm_w, m_gdn_w_out, m_ret_w_in, m_ret_w_out, m_gmlp_w_in, m_gmlp_ln_g, m_gmlp_ln_b, m_gmlp_w_s, m_gmlp_b_s, m_gmlp_w_out, m_sb_w_in, m_sb_w_out, v_cond_w, v_cond_b, v_ada_w, v_ada_b, v_ln_g, v_ln_b, v_ffn_up, v_ffn_conv_w, v_ffn_conv_b, v_ffn_down, v_gdn_w_in, v_gdn_conv_w, v_gdn_a_log, v_gdn_dt_bias, v_gdn_norm_w, v_gdn_w_out, v_ret_w_in, v_ret_w_out, v_gmlp_w_in, v_gmlp_ln_g, v_gmlp_ln_b, v_gmlp_w_s, v_gmlp_b_s, v_gmlp_w_out, v_sb_w_in, v_sb_w_out):
    w = dict(cond_w=cond_w, cond_b=cond_b, ada_w=ada_w, ada_b=ada_b, ln_g=ln_g, ln_b=ln_b, ffn_up=ffn_up,
             ffn_conv_w=ffn_conv_w, ffn_conv_b=ffn_conv_b, ffn_down=ffn_down, gdn_w_in=gdn_w_in,
             gdn_conv_w=gdn_conv_w, gdn_a_log=gdn_a_log, gdn_dt_bias=gdn_dt_bias, gdn_norm_w=gdn_norm_w,
             gdn_w_out=gdn_w_out, ret_w_in=ret_w_in, ret_w_out=ret_w_out, gmlp_w_in=gmlp_w_in,
             gmlp_ln_g=gmlp_ln_g, gmlp_ln_b=gmlp_ln_b, gmlp_w_s=gmlp_w_s, gmlp_b_s=gmlp_b_s,
             gmlp_w_out=gmlp_w_out, sb_w_in=sb_w_in, sb_w_out=sb_w_out)
    mom = dict(cond_w=m_cond_w, cond_b=m_cond_b, ada_w=m_ada_w, ada_b=m_ada_b, ln_g=m_ln_g, ln_b=m_ln_b,
               ffn_up=m_ffn_up, ffn_conv_w=m_ffn_conv_w, ffn_conv_b=m_ffn_conv_b, ffn_down=m_ffn_down,
               gdn_w_in=m_gdn_w_in, gdn_conv_w=m_gdn_conv_w, gdn_a_log=m_gdn_a_log, gdn_dt_bias=m_gdn_dt_bias,
               gdn_norm_w=m_gdn_norm_w, gdn_w_out=m_gdn_w_out, ret_w_in=m_ret_w_in, ret_w_out=m_ret_w_out,
               gmlp_w_in=m_gmlp_w_in, gmlp_ln_g=m_gmlp_ln_g, gmlp_ln_b=m_gmlp_ln_b, gmlp_w_s=m_gmlp_w_s,
               gmlp_b_s=m_gmlp_b_s, gmlp_w_out=m_gmlp_w_out, sb_w_in=m_sb_w_in, sb_w_out=m_sb_w_out)
    var = dict(cond_w=v_cond_w, cond_b=v_cond_b, ada_w=v_ada_w, ada_b=v_ada_b, ln_g=v_ln_g, ln_b=v_ln_b,
               ffn_up=v_ffn_up, ffn_conv_w=v_ffn_conv_w, ffn_conv_b=v_ffn_conv_b, ffn_down=v_ffn_down,
               gdn_w_in=v_gdn_w_in, gdn_conv_w=v_gdn_conv_w, gdn_a_log=v_gdn_a_log, gdn_dt_bias=v_gdn_dt_bias,
               gdn_norm_w=v_gdn_norm_w, gdn_w_out=v_gdn_w_out, ret_w_in=v_ret_w_in, ret_w_out=v_ret_w_out,
               gmlp_w_in=v_gmlp_w_in, gmlp_ln_g=v_gmlp_ln_g, gmlp_ln_b=v_gmlp_ln_b, gmlp_w_s=v_gmlp_w_s,
               gmlp_b_s=v_gmlp_b_s, gmlp_w_out=v_gmlp_w_out, sb_w_in=v_sb_w_in, sb_w_out=v_sb_w_out)

    me = _my_id()
    x = x[0]
    target = loss_target[0]
    d = x.shape[-1]
    dsh = d // N_DEV
    msh = ada_w.shape[-1]

    c_all = _exchange(_pad8(c), False, "gather_c")[:, 0, :]
    c_mine = lax.dynamic_slice_in_dim(c_all, me * dsh, dsh, axis=1)
    pre_part = _mm(c_mine, cond_w, 'nn', "cond_fwd")
    pre = jnp.sum(_exchange(pre_part, False, "gather_pre"), axis=0) + cond_b
    e_all = jax.nn.silu(pre)
    mod_part = jnp.concatenate([_mm(e_all, ada_w[i], 'nn', "ada_fwd%d" % i) for i in range(DEPTH)], axis=0)
    mod_all = _exchange(mod_part, False, "gather_mod")
    mod_all = mod_all.reshape(N_DEV, DEPTH, N_DEV, msh)
    mods = lax.dynamic_index_in_dim(mod_all, me, axis=2, keepdims=False)
    mods = mods.transpose(1, 0, 2).reshape(DEPTH, N_DEV * msh) + ada_b

    big_names = list(BIG)
    packed = _pack_rows([w[n] for n in big_names], BF16, BIG_ROW_ALIGN)
    gathered = _gather_two_level(packed, "gather_weights")
    blocks = _unpack_rows(gathered, [w[n].shape for n in big_names])
    p = {n: _join(b, BIG[n]) for n, b in zip(big_names, blocks)}
    for n in big_names:
        if not n.startswith('ffn_'):
            p[n] = p[n].astype(F32)
    sm_names = list(SMALL_SHARDED)
    sm_packed = _pack_rows([w[n] for n in sm_names], F32)
    sm_blocks = _unpack_rows(_exchange(sm_packed, False, "gather_small"), [w[n].shape for n in sm_names])
    for n, b in zip(sm_names, sm_blocks):
        p[n] = _join(b, SMALL_SHARDED[n])
    for n in SMALL_REPL:
        p[n] = w[n]
    n_qkvz = 4 * d
    p['gdn_w_qkvz'] = p['gdn_w_in'][:, :n_qkvz]
    p['gdn_w_ab'] = jnp.pad(p['gdn_w_in'][:, n_qkvz:], ((0, 0), (0, LANES - 2 * GDN_HEADS)))
    del p['gdn_w_in']

    loss_local, dx, dmods, dp = _trunk_grad(x, mods, p, target)
    dp['gdn_w_in'] = jnp.concatenate([dp.pop('gdn_w_qkvz'), dp.pop('gdn_w_ab')[:, :2 * GDN_HEADS]], axis=1)

    dmod_all = _exchange(dmods.reshape(-1, d), False, "gather_dmod").reshape(N_DEV, DEPTH, 6 * d)
    grads = {'ada_b': jnp.sum(dmod_all, axis=0)}
    dm_mine = lax.dynamic_slice_in_dim(dmod_all, me * msh, msh, axis=2)
    grads['ada_w'] = jnp.stack([_mm_outer(e_all, dm_mine[:, i], "ada_dw%d" % i) for i in range(DEPTH)])
    de_part = _mm(dm_mine[:, 0], ada_w[0], 'nt', "ada_de0")
    for i in range(1, DEPTH):
        de_part = de_part + _mm(dm_mine[:, i], ada_w[i], 'nt', "ada_de%d" % i)
    de_all = jnp.sum(_exchange(de_part, False, "gather_de"), axis=0)
    sig = jax.nn.sigmoid(pre)
    dpre = de_all * (sig * (1.0 + pre * (1.0 - sig)))
    grads['cond_b'] = jnp.sum(dpre, axis=0)
    grads['cond_w'] = _mm_outer(c_mine, dpre, "cond_dw")

    small_names = sm_names + SMALL_REPL
    small_packed = _pack_rows([loss_local.reshape(1)] + [dp[n] for n in small_names], F32)
    small_sum = _sum_slots(_exchange(small_packed, False, "gather_small_grads"), "sum_small_grads")
    small = _unpack_rows(small_sum, [(1,)] + [dp[n].shape for n in small_names])
    loss = small[0][0]
    for n, g in zip(small_names, small[1:]):
        if n in SMALL_SHARDED:
            ax = SMALL_SHARDED[n]
            g = lax.dynamic_slice_in_dim(g, me * w[n].shape[ax], w[n].shape[ax], axis=ax)
        grads[n] = g

    send = _pack_big_grads([dp[n] for n in big_names], [BIG[n] for n in big_names])
    shapes = [w[n].shape for n in big_names]
    outs = _adamw(_reduce_to_owner(send), *[_pack_rows([t[n] for n in big_names], F32, BIG_ROW_ALIGN) for t in (w, mom, var)],
                  "adamw_big")
    g_b, d_b, m_b, v_b = (_unpack_rows(o, shapes) for o in outs)
    delta, new_m, new_v = {}, {}, {}
    for j, n in enumerate(big_names):
        grads[n], delta[n], new_m[n], new_v[n] = g_b[j], d_b[j], m_b[j], v_b[j]

    rest = [n for n in WEIGHTS if n not in BIG]
    shapes = [w[n].shape for n in rest]
    outs = _adamw([(_pack_rows([grads[n] for n in rest], F32, BIG_ROW_ALIGN)[None], 0)],
                  *[_pack_rows([t[n] for n in rest], F32, BIG_ROW_ALIGN) for t in (w, mom, var)], "adamw_rest")
    _, d_r, m_r, v_r = (_unpack_rows(o, shapes) for o in outs)
    for j, n in enumerate(rest):
        delta[n], new_m[n], new_v[n] = d_r[j], m_r[j], v_r[j]

    return (loss, dx[None], *[grads[n] for n in WEIGHTS], *[delta[n] for n in WEIGHTS],
            *[new_m[n] for n in WEIGHTS], *[new_v[n] for n in WEIGHTS])
```

```python
import functools
import math

import jax
import jax.numpy as jnp
from jax import lax
from jax.experimental import pallas as pl
from jax.experimental.pallas import tpu as pltpu

F32 = jnp.float32
BF16 = jnp.bfloat16
MXU_DTYPE = jnp.bfloat16
MESH = pl.DeviceIdType.MESH
N_DEV = 8
LANES = 128
SUBLANES = 8
VMEM_LIMIT = 48 * 1024 * 1024

DEPTH = 4
LN_EPS = 1e-5
DN_ALPHA = (2.0 * DEPTH) ** 0.25
GDN_HEADS, GDN_CHUNK = 8, 64
RET_HEADS, RET_CHUNK, RET_ROPE_BASE = 4, 128, 10000.0
GMLP_CHUNK, GMLP_GROUPS = 128, 8
SB_HEADS = 16
ADAM_LR, ADAM_B1, ADAM_B2, ADAM_EPS, ADAM_WD, ADAM_STEP = 0.001, 0.9, 0.999, 1e-08, 0.01, 10

WEIGHTS = ['cond_w', 'cond_b', 'ada_w', 'ada_b', 'ln_g', 'ln_b', 'ffn_up', 'ffn_conv_w', 'ffn_conv_b', 'ffn_down',
           'gdn_w_in', 'gdn_conv_w', 'gdn_a_log', 'gdn_dt_bias', 'gdn_norm_w', 'gdn_w_out', 'ret_w_in', 'ret_w_out',
           'gmlp_w_in', 'gmlp_ln_g', 'gmlp_ln_b', 'gmlp_w_s', 'gmlp_b_s', 'gmlp_w_out', 'sb_w_in', 'sb_w_out']
BIG = {'ffn_up': 2, 'ffn_down': 1, 'gdn_w_in': 1, 'gdn_w_out': 0, 'ret_w_in': 1, 'ret_w_out': 0,
       'gmlp_w_in': 1, 'gmlp_w_out': 0, 'sb_w_in': 1, 'sb_w_out': 0}
SMALL_SHARDED = {'ln_g': 2, 'ln_b': 2, 'ffn_conv_w': 2, 'gdn_conv_w': 1}
SMALL_REPL = ['ffn_conv_b', 'gdn_a_log', 'gdn_dt_bias', 'gdn_norm_w', 'gmlp_ln_g', 'gmlp_ln_b', 'gmlp_w_s', 'gmlp_b_s']
MIXER_PARAMS = (('gdn_w_qkvz', 'gdn_w_ab', 'gdn_conv_w', 'gdn_a_log', 'gdn_dt_bias', 'gdn_norm_w', 'gdn_w_out'),
                ('ret_w_in', 'ret_w_out'),
                ('gmlp_w_in', 'gmlp_ln_g', 'gmlp_ln_b', 'gmlp_w_s', 'gmlp_b_s', 'gmlp_w_out'),
                ('sb_w_in', 'sb_w_out'))


def _pcall(body, **kw):
    return pl.pallas_call(body, **kw)


def _params(*semantics):
    return pltpu.CompilerParams(dimension_semantics=semantics, vmem_limit_bytes=VMEM_LIMIT)


def _my_id():
    return 4 * lax.axis_index("x") + 2 * lax.axis_index("y") + lax.axis_index("c")


def _pick(dim, prefs):
    for p in prefs:
        if dim % p == 0:
            return p
    return dim


def _exchange(src, scatter, name):
    blk = src.shape[1:] if scatter else src.shape
    out_shape = jax.ShapeDtypeStruct((N_DEV,) + tuple(blk), src.dtype)

    def body(src_ref, out_ref, send_sems, recv_sems, local_sem):
        x, y, c = lax.axis_index("x"), lax.axis_index("y"), lax.axis_index("c")
        me = 4 * x + 2 * y + c
        mine = pltpu.make_async_copy(src_ref.at[me] if scatter else src_ref, out_ref.at[me], local_sem)
        mine.start()
        copies = []
        for k in range(1, N_DEV):
            px = 1 - x if (k >> 2) & 1 else x
            py = 1 - y if (k >> 1) & 1 else y
            pc = 1 - c if k & 1 else c
            peer = 4 * px + 2 * py + pc
            cp = pltpu.make_async_remote_copy(
                src_ref=src_ref.at[peer] if scatter else src_ref,
                dst_ref=out_ref.at[me],
                send_sem=send_sems.at[k - 1], recv_sem=recv_sems.at[k - 1],
                device_id=(px, py, pc), device_id_type=MESH)
            cp.start()
            copies.append(cp)
        for cp in copies:
            cp.wait()
        mine.wait()

    return _pcall(
        body, name=name, out_shape=out_shape,
        in_specs=[pl.BlockSpec(memory_space=pl.ANY)],
        out_specs=pl.BlockSpec(memory_space=pl.ANY),
        scratch_shapes=[pltpu.SemaphoreType.DMA((N_DEV - 1,)), pltpu.SemaphoreType.DMA((N_DEV - 1,)),
                        pltpu.SemaphoreType.DMA(())],
    )(src)


def _flip(x, y, c, k):
    return (1 - x if (k >> 2) & 1 else x, 1 - y if (k >> 1) & 1 else y, 1 - c if k & 1 else c)


def _dev_id(p):
    return 4 * p[0] + 2 * p[1] + p[2]


OTHER_CHIPS = (4, 2, 6)


def _gather_two_level(src, name):
    out_shape = jax.ShapeDtypeStruct((N_DEV,) + tuple(src.shape), src.dtype)

    def body(x_ref, out_ref, send_sems, recv_sems, local_sem):
        x, y, c = lax.axis_index("x"), lax.axis_index("y"), lax.axis_index("c")
        me, sibling = (x, y, c), (x, y, 1 - c)
        chips = [_flip(x, y, c, k) for k in OTHER_CHIPS]

        def copy(k, block, to, from_src=False):
            slot = out_ref.at[_dev_id(block)]
            return pltpu.make_async_remote_copy(
                src_ref=x_ref if from_src else slot, dst_ref=slot,
                send_sem=send_sems.at[k], recv_sem=recv_sems.at[k], device_id=to, device_id_type=MESH)

        mine = pltpu.make_async_copy(x_ref, out_ref.at[_dev_id(me)], local_sem)
        mine.start()
        first = [copy(0, me, sibling, True)] + [copy(1 + j, me, chip, True) for j, chip in enumerate(chips)]
        for cp in first:
            cp.start()
        passed = [copy(4 + j, chip, sibling) for j, chip in enumerate(chips)]
        for j, chip in enumerate(chips):
            copy(1 + j, chip, me).wait_recv()
            passed[j].start()
        copy(0, sibling, me).wait_recv()
        for j, chip in enumerate(chips):
            copy(4 + j, (chip[0], chip[1], 1 - c), me).wait_recv()
        for cp in first + passed:
            cp.wait_send()
        mine.wait()

    return _pcall(
        body, name=name, out_shape=out_shape,
        in_specs=[pl.BlockSpec(memory_space=pl.ANY)],
        out_specs=pl.BlockSpec(memory_space=pl.ANY),
        scratch_shapes=[pltpu.SemaphoreType.DMA((N_DEV - 1,)), pltpu.SemaphoreType.DMA((N_DEV - 1,)),
                        pltpu.SemaphoreType.DMA(())],
    )(src)


def _send_slots(src, plan, n_out, name):
    out_shape = jax.ShapeDtypeStruct((n_out,) + tuple(src.shape[1:]), src.dtype)

    def body(src_ref, out_ref, send_sems, recv_sems):
        x, y, c = lax.axis_index("x"), lax.axis_index("y"), lax.axis_index("c")
        copies = []
        for e, (k, src_slot, dst_slot) in enumerate(plan):
            cp = pltpu.make_async_remote_copy(
                src_ref=src_ref.at[src_slot(x, y, c)], dst_ref=out_ref.at[dst_slot],
                send_sem=send_sems.at[e], recv_sem=recv_sems.at[e],
                device_id=_flip(x, y, c, k), device_id_type=MESH)
            cp.start()
            copies.append(cp)
        for cp in copies:
            cp.wait()

    return _pcall(
        body, name=name, out_shape=out_shape,
        in_specs=[pl.BlockSpec(memory_space=pl.ANY)],
        out_specs=pl.BlockSpec(memory_space=pl.ANY),
        scratch_shapes=[pltpu.SemaphoreType.DMA((len(plan),)), pltpu.SemaphoreType.DMA((len(plan),))],
    )(src)


def _reduce_to_owner(send):
    x, y, c = lax.axis_index("x"), lax.axis_index("y"), lax.axis_index("c")
    plan_a = [(1, lambda x, y, c: _dev_id((x, y, 1 - c)), 0)]
    plan_a += [(1, functools.partial(lambda k, x, y, c: _dev_id(_flip(x, y, c, k | 1)), k), 1 + j)
               for j, k in enumerate(OTHER_CHIPS)]
    from_sibling = _send_slots(send, plan_a, 1 + len(OTHER_CHIPS), "reduce_d2d")
    mine = jnp.stack([lax.dynamic_index_in_dim(send, _dev_id(_flip(x, y, c, k)), 0, keepdims=False)
                      for k in OTHER_CHIPS])
    rows = mine.shape[1]
    pair = _add_rows(mine.reshape(-1, LANES), from_sibling[1:].reshape(-1, LANES), "reduce_pair_sum")
    plan_c = [(k, functools.partial(lambda j, x, y, c: j, j), j) for j, k in enumerate(OTHER_CHIPS)]
    from_chips = _send_slots(pair.reshape(len(OTHER_CHIPS), rows, LANES), plan_c, len(OTHER_CHIPS), "reduce_ici")
    own = lax.dynamic_index_in_dim(send, _dev_id((x, y, c)), 0, keepdims=True)
    return [(own, 0), (from_sibling, 0)] + [(from_chips, j) for j in range(len(OTHER_CHIPS))]


ROW_ALIGN = 16
BIG_ROW_ALIGN = 512


def _pack_rows(parts, dtype, row_align=ROW_ALIGN):
    flat = jnp.concatenate([p.reshape(-1).astype(dtype) for p in parts])
    n = flat.shape[0]
    pad = (-n) % (row_align * LANES)
    if pad:
        flat = jnp.concatenate([flat, jnp.zeros((pad,), dtype)])
    return flat.reshape(-1, LANES)


def _unpack_rows(packed, shapes):
    lead = packed.shape[:-2]
    flat = packed.reshape(lead + (-1,))
    out, off = [], 0
    for s in shapes:
        n = math.prod(s)
        out.append(flat[..., off:off + n].reshape(lead + tuple(s)))
        off += n
    return out


def _mm(a, b, dims, name, exact=False):
    if dims == 'nn':
        (m, k), n = a.shape, b.shape[1]
    elif dims == 'nt':
        (m, k), n = a.shape, b.shape[0]
    else:
        (k, m), n = a.shape, b.shape[1]
    tm = _pick(m, (1408, 1024, 512, 256, 128))
    tn = _pick(n, (512, 256, 128))
    tk = k if k <= 2816 else _pick(k, (2816, 2048, 1536, 1024, 512, 256, 128))
    nk = k // tk
    if dims == 'nn':
        a_spec = pl.BlockSpec((tm, tk), lambda i, j, kk: (i, kk))
        b_spec = pl.BlockSpec((tk, tn), lambda i, j, kk: (kk, j))
        dn = (((1,), (0,)), ((), ()))
    elif dims == 'nt':
        a_spec = pl.BlockSpec((tm, tk), lambda i, j, kk: (i, kk))
        b_spec = pl.BlockSpec((tn, tk), lambda i, j, kk: (j, kk))
        dn = (((1,), (1,)), ((), ()))
    else:
        a_spec = pl.BlockSpec((tk, tm), lambda i, j, kk: (kk, i))
        b_spec = pl.BlockSpec((tk, tn), lambda i, j, kk: (kk, j))
        dn = (((0,), (0,)), ((), ()))

    def product(a_ref, b_ref):
        if exact:
            return lax.dot_general(a_ref[...], b_ref[...], dn, precision=lax.Precision.HIGHEST,
                                   preferred_element_type=F32)
        return lax.dot_general(a_ref[...].astype(MXU_DTYPE), b_ref[...].astype(MXU_DTYPE), dn,
                               preferred_element_type=F32)

    def body(a_ref, b_ref, o_ref, *acc):
        if nk == 1:
            o_ref[...] = product(a_ref, b_ref)
            return
        acc_ref, = acc
        kk = pl.program_id(2)

        @pl.when(kk == 0)
        def _():
            acc_ref[...] = jnp.zeros_like(acc_ref)

        acc_ref[...] += product(a_ref, b_ref)

        @pl.when(kk == nk - 1)
        def _():
            o_ref[...] = acc_ref[...]

    return _pcall(
        body, name=name, out_shape=jax.ShapeDtypeStruct((m, n), F32),
        grid=(m // tm, n // tn, nk),
        in_specs=[a_spec, b_spec],
        out_specs=pl.BlockSpec((tm, tn), lambda i, j, kk: (i, j)),
        scratch_shapes=[pltpu.VMEM((tm, tn), F32)] if nk > 1 else [],
        compiler_params=_params("parallel", "parallel", "arbitrary"),
    )(a, b)


def _mm_outer(a, b, name):
    pad = LANES - a.shape[0]
    return _mm(jnp.pad(a.T, ((0, 0), (0, pad))), jnp.pad(b, ((0, pad), (0, 0))), 'nn', name, exact=True)


@functools.partial(jax.custom_vjp, nondiff_argnums=(2,))
def _linear(a, w, name):
    return _mm(a, w, 'nn', name + "_fwd")


def _linear_fwd(a, w, name):
    return _mm(a, w, 'nn', name + "_fwd"), (a, w)


def _linear_bwd(name, res, dy):
    a, w = res
    return _mm(dy, w, 'nt', name + "_dx"), _mm(a, dy, 'tn', name + "_dw")


_linear.defvjp(_linear_fwd, _linear_bwd)


SB_BK = 256
SB_STRIP = 16


def _sb_tiles(s):
    tq = _pick(s, (512, 256, 128))
    bk = min(SB_BK, tq)
    return tq, bk, tq // bk


def _sb_valid(t, sr, bk, q0, k0):
    row = lax.broadcasted_iota(jnp.int32, (sr, bk), 0) + (q0 + t * sr)
    col = lax.broadcasted_iota(jnp.int32, (sr, bk), 1) + k0
    return col < row


def _sb_tri(bk, inclusive):
    r = jnp.bitwise_and(lax.broadcasted_iota(jnp.int32, (2 * bk, bk), 0), bk - 1)
    c = lax.broadcasted_iota(jnp.int32, (2 * bk, bk), 1)
    return (r >= c).astype(BF16) if inclusive else (r > c).astype(BF16)


def _sb_split(ref, n, rows, bk, val):
    hi = val.astype(BF16)
    ref[n, rows, 0:bk] = hi
    ref[n, rows, bk:2 * bk] = (val - hi.astype(F32)).astype(BF16)


LOG2E = 1.0 / math.log(2.0)


def _sb_logits_phase(z_ref, ls_ref, hl_ref, l0_ref, n, tq, bk, sr, q0, k0, masked):
    for t in range(tq // sr):
        rows = slice(t * sr, (t + 1) * sr)
        z = z_ref[n, rows, :] * LOG2E
        ls = jnp.minimum(z, 0.0) - jnp.log(1.0 + jnp.exp2(-jnp.abs(z))) * LOG2E
        lm = ls - z
        if masked:
            lm = jnp.where(_sb_valid(t, sr, bk, q0, k0), lm, 0.0)
        ls_ref[n, rows, :] = ls
        _sb_split(hl_ref, n, rows, bk, lm)
        l0_ref[n, rows, :] = lm[:, 0:1]


def _sb_fwd_call(q, kt, v):
    h, s, dh = q.shape
    tq, bk, nt = _sb_tiles(s)
    sr = SB_STRIP

    def body(q_ref, kt_ref, v_ref, o_ref, z_ref, ls_ref, hl_ref, f_ref, a_ref, l0_ref, cl_ref, acc_ref):
        i = pl.program_id(1)
        q0 = i * tq
        cl_ref[...] = jnp.zeros_like(cl_ref)
        acc_ref[...] = jnp.zeros_like(acc_ref)
        u_excl = _sb_tri(bk, False)

        def iteration(kb0, masked):
            k0s = [pl.multiple_of(kb0 + (nt - 1 - n) * bk, bk) for n in range(nt)]
            for n in range(nt):
                z_ref[n] = jnp.dot(q_ref[...], kt_ref[:, pl.ds(k0s[n], bk)], preferred_element_type=F32)
            for n in range(nt):
                _sb_logits_phase(z_ref, ls_ref, hl_ref, l0_ref, n, tq, bk, sr, q0, k0s[n], masked)
            for n in range(nt):
                f_ref[n] = jnp.dot(hl_ref[n], u_excl, preferred_element_type=F32)
            for t in range(tq // sr):
                rows = slice(t * sr, (t + 1) * sr)
                c = cl_ref[rows, :]
                for n in range(nt):
                    f = f_ref[n, rows, :]
                    a = jnp.exp2(ls_ref[n, rows, :] + f + c)
                    if masked:
                        a = jnp.where(_sb_valid(t, sr, bk, q0, k0s[n]), a, 0.0)
                    a_ref[n, rows, :] = a.astype(a_ref.dtype)
                    c = c + f[:, 0:1] + l0_ref[n, rows, :]
                cl_ref[rows, :] = c
            for n in range(nt):
                acc_ref[...] += jnp.dot(a_ref[n], v_ref[pl.ds(k0s[n], bk), :], preferred_element_type=F32)

        def below(jj, c):
            iteration((i - 1 - jj) * tq, False)
            return c

        iteration(q0, True)
        lax.fori_loop(0, i, below, 0)
        o_ref[...] = acc_ref[...]

    return _pcall(
        body, name="sb_fwd", out_shape=jax.ShapeDtypeStruct((h, s, dh), F32),
        grid=(h, s // tq),
        in_specs=[pl.BlockSpec((None, tq, dh), lambda hh, i: (hh, i, 0)),
                  pl.BlockSpec((None, dh, s), lambda hh, i: (hh, 0, 0)),
                  pl.BlockSpec((None, s, dh), lambda hh, i: (hh, 0, 0))],
        out_specs=pl.BlockSpec((None, tq, dh), lambda hh, i: (hh, i, 0)),
        scratch_shapes=[pltpu.VMEM((nt, tq, bk), F32), pltpu.VMEM((nt, tq, bk), F32),
                        pltpu.VMEM((nt, tq, 2 * bk), BF16), pltpu.VMEM((nt, tq, bk), F32),
                        pltpu.VMEM((nt, tq, bk), q.dtype), pltpu.VMEM((nt, tq, 1), F32),
                        pltpu.VMEM((tq, 1), F32), pltpu.VMEM((tq, dh), F32)],
        compiler_params=_params("parallel", "arbitrary"),
    )(q, kt, v)


def _sb_bwd_call(q, qt, k, kt, vt, o, do, dot):
    h, s, dh = q.shape
    tq, bk, nt = _sb_tiles(s)
    sr = SB_STRIP

    def body(q_ref, qt_ref, k_ref, kt_ref, vt_ref, o_ref, do_ref, dot_ref, dq_ref, dkt_ref, dvt_ref,
             z_ref, ls_ref, hl_ref, f_ref, a_ref, g_ref, dz_ref, da_ref, l0_ref, dob_ref,
             cl_ref, cg_ref, dl_ref, dqa_ref):
        i = pl.program_id(1)
        q0 = i * tq

        @pl.when(i == 0)
        def _():
            dkt_ref[...] = jnp.zeros_like(dkt_ref)
            dvt_ref[...] = jnp.zeros_like(dvt_ref)

        cl_ref[...] = jnp.zeros_like(cl_ref)
        cg_ref[...] = jnp.zeros_like(cg_ref)
        dqa_ref[...] = jnp.zeros_like(dqa_ref)
        dob = do_ref[...].astype(dob_ref.dtype)
        dob_ref[...] = dob
        dl_ref[...] = jnp.sum(dob.astype(F32) * o_ref[...], axis=1, keepdims=True)
        u_excl = _sb_tri(bk, False)
        u_incl = _sb_tri(bk, True)

        def iteration(kb0, masked):
            k0s = [pl.multiple_of(kb0 + (nt - 1 - n) * bk, bk) for n in range(nt)]
            for n in range(nt):
                z_ref[n] = jnp.dot(q_ref[...], kt_ref[:, pl.ds(k0s[n], bk)], preferred_element_type=F32)
                da_ref[n] = jnp.dot(dob_ref[...], vt_ref[:, pl.ds(k0s[n], bk)], preferred_element_type=F32)
            for n in range(nt):
                _sb_logits_phase(z_ref, ls_ref, hl_ref, l0_ref, n, tq, bk, sr, q0, k0s[n], masked)
            for n in range(nt):
                f_ref[n] = jnp.dot(hl_ref[n], u_excl, preferred_element_type=F32)
            for t in range(tq // sr):
                rows = slice(t * sr, (t + 1) * sr)
                c = cl_ref[rows, :]
                for n in range(nt):
                    f = f_ref[n, rows, :]
                    a = jnp.exp2(ls_ref[n, rows, :] + f + c)
                    if masked:
                        a = jnp.where(_sb_valid(t, sr, bk, q0, k0s[n]), a, 0.0)
                    ab = a.astype(a_ref.dtype)
                    a_ref[n, rows, :] = ab
                    g = da_ref[n, rows, :] * ab.astype(F32)
                    g_ref[n, rows, :] = g
                    _sb_split(hl_ref, n, rows, bk, g)
                    c = c + f[:, 0:1] + l0_ref[n, rows, :]
                cl_ref[rows, :] = c
            for n in range(nt):
                f_ref[n] = jnp.dot(hl_ref[n], u_incl, preferred_element_type=F32)
            for t in range(tq // sr):
                rows = slice(t * sr, (t + 1) * sr)
                cg = cg_ref[rows, :]
                for n in range(nt):
                    sg_tile = f_ref[n, rows, :]
                    p = dl_ref[rows, :] - (sg_tile + cg)
                    g = g_ref[n, rows, :]
                    dz = g - (g + p) * jnp.exp2(ls_ref[n, rows, :])
                    if masked:
                        dz = jnp.where(_sb_valid(t, sr, bk, q0, k0s[n]), dz, 0.0)
                    dz_ref[n, rows, :] = dz.astype(dz_ref.dtype)
                    cg = cg + sg_tile[:, 0:1]
                cg_ref[rows, :] = cg
            for n in range(nt):
                cols = pl.ds(k0s[n], bk)
                dqa_ref[...] += jnp.dot(dz_ref[n], k_ref[cols, :], preferred_element_type=F32)
                dkt_ref[:, cols] += jnp.dot(qt_ref[...], dz_ref[n], preferred_element_type=F32)
                dvt_ref[:, cols] += jnp.dot(dot_ref[...], a_ref[n], preferred_element_type=F32)

        def below(jj, c):
            iteration((i - 1 - jj) * tq, False)
            return c

        iteration(q0, True)
        lax.fori_loop(0, i, below, 0)
        dq_ref[...] = dqa_ref[...]

    blk_q = pl.BlockSpec((None, tq, dh), lambda hh, i: (hh, i, 0))
    blk_qt = pl.BlockSpec((None, dh, tq), lambda hh, i: (hh, 0, i))
    blk_s = pl.BlockSpec((None, s, dh), lambda hh, i: (hh, 0, 0))
    blk_st = pl.BlockSpec((None, dh, s), lambda hh, i: (hh, 0, 0))
    mx = q.dtype
    return _pcall(
        body, name="sb_bwd",
        out_shape=(jax.ShapeDtypeStruct((h, s, dh), F32), jax.ShapeDtypeStruct((h, dh, s), F32),
                   jax.ShapeDtypeStruct((h, dh, s), F32)),
        grid=(h, s // tq),
        in_specs=[blk_q, blk_qt, blk_s, blk_st, blk_st, blk_q, blk_q, blk_qt],
        out_specs=(blk_q, blk_st, blk_st),
        scratch_shapes=[pltpu.VMEM((nt, tq, bk), F32), pltpu.VMEM((nt, tq, bk), F32),
                        pltpu.VMEM((nt, tq, 2 * bk), BF16), pltpu.VMEM((nt, tq, bk), F32),
                        pltpu.VMEM((nt, tq, bk), mx), pltpu.VMEM((nt, tq, bk), F32),
                        pltpu.VMEM((nt, tq, bk), mx), pltpu.VMEM((nt, tq, bk), F32),
                        pltpu.VMEM((nt, tq, 1), F32), pltpu.VMEM((tq, dh), mx),
                        pltpu.VMEM((tq, 1), F32), pltpu.VMEM((tq, 1), F32), pltpu.VMEM((tq, 1), F32),
                        pltpu.VMEM((tq, dh), F32)],
        compiler_params=_params("parallel", "arbitrary"),
    )(q, qt, k, kt, vt, o, do, dot)


def _swap(t):
    return t.transpose(0, 2, 1)


def _sb_scale(dh):
    assert math.log2(dh) % 2 == 0, dh
    return dh ** -0.5


@jax.custom_vjp
def _sb_core(q, k, v):
    return _sb_core_fwd(q, k, v)[0]


def _sb_core_fwd(q, k, v):
    scale = _sb_scale(q.shape[-1])
    qs, kb, vb = (q.astype(MXU_DTYPE) * scale).astype(MXU_DTYPE), k.astype(MXU_DTYPE), v.astype(MXU_DTYPE)
    o = _sb_fwd_call(qs, _swap(kb), vb)
    return o, (qs, kb, vb, o)


def _sb_core_bwd(res, do):
    qs, kb, vb, o = res
    ks = (kb * _sb_scale(kb.shape[-1])).astype(MXU_DTYPE)
    dq, dkt, dvt = _sb_bwd_call(qs, _swap(qs), ks, _swap(kb), _swap(vb), o, do, _swap(do.astype(MXU_DTYPE)))
    return dq, _swap(dkt), _swap(dvt)


_sb_core.defvjp(_sb_core_fwd, _sb_core_bwd)


def _row_block(s):
    return _pick(s, (512, 256, 128, 64, 32, 16, 8))


def _fold8(t):
    r, c = t.shape
    return jnp.sum(t.reshape(r // SUBLANES, SUBLANES, c), axis=0)


def _vec(a):
    return a.reshape(1, -1)


def _modulate(x, sc, sh, out_dtype, name):
    s, d = x.shape
    tr = _row_block(s)

    def body(x_ref, sc_ref, sh_ref, o_ref):
        o_ref[...] = (x_ref[...] * (1.0 + sc_ref[...]) + sh_ref[...]).astype(o_ref.dtype)

    row = pl.BlockSpec((tr, d), lambda i: (i, 0))
    vec = pl.BlockSpec((1, d), lambda i: (0, 0))
    return _pcall(body, name=name, out_shape=jax.ShapeDtypeStruct((s, d), out_dtype), grid=(s // tr,),
                  in_specs=[row, vec, vec], out_specs=row, compiler_params=_params("parallel"))(x, _vec(sc), _vec(sh))


def _modulate_bwd(dxa, dh, x, sc, name):
    s, d = x.shape
    tr = _row_block(s)

    def body(dxa_ref, dh_ref, x_ref, sc_ref, dx_ref, acc_ref):
        @pl.when(pl.program_id(0) == 0)
        def _():
            acc_ref[...] = jnp.zeros_like(acc_ref)

        dh = dh_ref[...]
        dx_ref[...] = dxa_ref[...] + dh * (1.0 + sc_ref[...])
        acc_ref[0] += _fold8(dh * x_ref[...])
        acc_ref[1] += _fold8(dh)

    row = pl.BlockSpec((tr, d), lambda i: (i, 0))
    vec = pl.BlockSpec((1, d), lambda i: (0, 0))
    dx, acc = _pcall(
        body, name=name,
        out_shape=(jax.ShapeDtypeStruct((s, d), F32), jax.ShapeDtypeStruct((2, SUBLANES, d), F32)),
        grid=(s // tr,), in_specs=[row, row, row, vec],
        out_specs=(row, pl.BlockSpec((2, SUBLANES, d), lambda i: (0, 0, 0))),
        compiler_params=_params("arbitrary"))(dxa, dh, x, _vec(sc))
    acc = jnp.sum(acc, axis=1)
    return dx, acc[0], acc[1]


def _resid_ln(x, y, g, gamma, beta, name):
    s, d = x.shape
    tr = _row_block(s)

    def body(x_ref, y_ref, g_ref, gam_ref, bet_ref, o_ref):
        u = DN_ALPHA * x_ref[...] + (1.0 + g_ref[...]) * y_ref[...]
        uc = u - jnp.mean(u, axis=-1, keepdims=True)
        var = jnp.mean(uc * uc, axis=-1, keepdims=True)
        o_ref[...] = uc * lax.rsqrt(var + LN_EPS) * gam_ref[...] + bet_ref[...]

    row = pl.BlockSpec((tr, d), lambda i: (i, 0))
    vec = pl.BlockSpec((1, d), lambda i: (0, 0))
    return _pcall(body, name=name, out_shape=jax.ShapeDtypeStruct((s, d), F32), grid=(s // tr,),
                  in_specs=[row, row, vec, vec, vec], out_specs=row,
                  compiler_params=_params("parallel"))(x, y, _vec(g), _vec(gamma), _vec(beta))


def _resid_ln_bwd(x, y, g, gamma, dout, name):
    s, d = x.shape
    tr = _row_block(s)

    def body(x_ref, y_ref, g_ref, gam_ref, do_ref, dxa_ref, dy_ref, acc_ref):
        @pl.when(pl.program_id(0) == 0)
        def _():
            acc_ref[...] = jnp.zeros_like(acc_ref)

        y = y_ref[...]
        gg = 1.0 + g_ref[...]
        u = DN_ALPHA * x_ref[...] + gg * y
        uc = u - jnp.mean(u, axis=-1, keepdims=True)
        rstd = lax.rsqrt(jnp.mean(uc * uc, axis=-1, keepdims=True) + LN_EPS)
        xhat = uc * rstd
        dout = do_ref[...]
        dxh = dout * gam_ref[...]
        du = rstd * (dxh - jnp.mean(dxh, axis=-1, keepdims=True)
                     - xhat * jnp.mean(dxh * xhat, axis=-1, keepdims=True))
        dxa_ref[...] = DN_ALPHA * du
        dy_ref[...] = gg * du
        acc_ref[0] += _fold8(dout * xhat)
        acc_ref[1] += _fold8(dout)
        acc_ref[2] += _fold8(du * y)

    row = pl.BlockSpec((tr, d), lambda i: (i, 0))
    vec = pl.BlockSpec((1, d), lambda i: (0, 0))
    dxa, dy, acc = _pcall(
        body, name=name,
        out_shape=(jax.ShapeDtypeStruct((s, d), F32), jax.ShapeDtypeStruct((s, d), F32),
                   jax.ShapeDtypeStruct((3, SUBLANES, d), F32)),
        grid=(s // tr,), in_specs=[row, row, vec, vec, row],
        out_specs=(row, row, pl.BlockSpec((3, SUBLANES, d), lambda i: (0, 0, 0))),
        compiler_params=_params("arbitrary"))(x, y, _vec(g), _vec(gamma), dout)
    acc = jnp.sum(acc, axis=1)
    return dxa, dy, acc[0], acc[1], acc[2]


def _loss_head(x, target, name):
    s, d = x.shape
    tr = _row_block(s)

    def body(x_ref, t_ref, dx_ref, acc_ref):
        @pl.when(pl.program_id(0) == 0)
        def _():
            acc_ref[...] = jnp.zeros_like(acc_ref)

        e = x_ref[...] - t_ref[...]
        dx_ref[...] = e * (1.0 / d)
        acc_ref[...] += _fold8(e * e)

    row = pl.BlockSpec((tr, d), lambda i: (i, 0))
    dx, acc = _pcall(
        body, name=name,
        out_shape=(jax.ShapeDtypeStruct((s, d), F32), jax.ShapeDtypeStruct((SUBLANES, d), F32)),
        grid=(s // tr,), in_specs=[row, row],
        out_specs=(row, pl.BlockSpec((SUBLANES, d), lambda i: (0, 0))),
        compiler_params=_params("arbitrary"))(x, target)
    return (0.5 / d) * jnp.sum(acc), dx


CONV_STRIPE = 128
CONV_ROWS = 256


def _shift_down(cur, halo, k):
    ext = jnp.concatenate([halo, cur], axis=0)
    return pltpu.roll(ext, k, 0)[SUBLANES:]


def _shift_up(cur, halo, k):
    ext = jnp.concatenate([cur, halo], axis=0)
    n = ext.shape[0]
    return pltpu.roll(ext, n - k, 0)[:n - SUBLANES]


def _gate_chunk(g_ref, r, rc):
    r0 = pl.multiple_of(r * rc, rc)
    cur = g_ref[pl.ds(r0, rc), :]
    hs = pl.multiple_of(jnp.maximum(r0 - SUBLANES, 0), SUBLANES)
    halo = jnp.where(r > 0, g_ref[pl.ds(hs, SUBLANES), :], 0.0)
    return r0, cur, _shift_down(cur, halo, 1), _shift_down(cur, halo, 2)


def _ffn_gate(gu, cw, cb, name):
    s, f2 = gu.shape
    f = f2 // 2
    tc = _pick(f, (CONV_STRIPE,))
    rc = _pick(s, (CONV_ROWS, 128, 64, 32, 16, 8))
    nj = f // tc

    def body(g_ref, u_ref, cw_ref, cb_ref, a_ref):
        w0, w1, w2, b = cw_ref[0:1, :], cw_ref[1:2, :], cw_ref[2:3, :], cb_ref[...]

        def chunk(r, c):
            r0, cur, x1, x2 = _gate_chunk(g_ref, r, rc)
            gc = w2 * cur + w1 * x1 + w0 * x2 + b
            a_ref[pl.ds(r0, rc), :] = (gc * jax.nn.sigmoid(gc) * u_ref[pl.ds(r0, rc), :]).astype(a_ref.dtype)
            return c

        lax.fori_loop(0, s // rc, chunk, 0)

    return _pcall(
        body, name=name, out_shape=jax.ShapeDtypeStruct((s, f), MXU_DTYPE), grid=(nj,),
        in_specs=[pl.BlockSpec((s, tc), lambda j: (0, j)), pl.BlockSpec((s, tc), lambda j: (0, j + nj)),
                  pl.BlockSpec((3, tc), lambda j: (0, j)), pl.BlockSpec((1, tc), lambda j: (0, j))],
        out_specs=pl.BlockSpec((s, tc), lambda j: (0, j)),
        compiler_params=_params("parallel"))(gu, gu, cw, _vec(cb))


def _ffn_gate_bwd(da, gu, cw, cb, name):
    s, f2 = gu.shape
    f = f2 // 2
    tc = _pick(f, (CONV_STRIPE,))
    rc = _pick(s, (CONV_ROWS, 128, 64, 32, 16, 8))
    nj = f // tc
    nr = s // rc

    def body(da_ref, g_ref, u_ref, cw_ref, cb_ref, dg_ref, du_ref, acc_ref, dgc_ref):
        w0, w1, w2, b = cw_ref[0:1, :], cw_ref[1:2, :], cw_ref[2:3, :], cb_ref[...]

        def chunk1(r, carry):
            a0, a1, a2, ab = carry
            r0, cur, x1, x2 = _gate_chunk(g_ref, r, rc)
            gc = w2 * cur + w1 * x1 + w0 * x2 + b
            sg = jax.nn.sigmoid(gc)
            da_c = da_ref[pl.ds(r0, rc), :]
            du_ref[pl.ds(r0, rc), :] = (da_c * (gc * sg)).astype(du_ref.dtype)
            dgc = da_c * u_ref[pl.ds(r0, rc), :] * (sg * (1.0 + gc * (1.0 - sg)))
            dgc_ref[pl.ds(r0, rc), :] = dgc
            return a0 + _fold8(dgc * x2), a1 + _fold8(dgc * x1), a2 + _fold8(dgc * cur), ab + _fold8(dgc)

        zero = jnp.zeros((SUBLANES, tc), F32)
        a0, a1, a2, ab = lax.fori_loop(0, nr, chunk1, (zero, zero, zero, zero))
        acc_ref[0], acc_ref[1], acc_ref[2], acc_ref[3] = a0, a1, a2, ab

        def chunk2(r, c):
            r0 = pl.multiple_of(r * rc, rc)
            cur = dgc_ref[pl.ds(r0, rc), :]
            hs = pl.multiple_of(jnp.minimum(r0 + rc, s - SUBLANES), SUBLANES)
            halo = jnp.where(r < nr - 1, dgc_ref[pl.ds(hs, SUBLANES), :], 0.0)
            dg = w2 * cur + w1 * _shift_up(cur, halo, 1) + w0 * _shift_up(cur, halo, 2)
            dg_ref[pl.ds(r0, rc), :] = dg.astype(dg_ref.dtype)
            return c

        lax.fori_loop(0, nr, chunk2, 0)

    stripe = pl.BlockSpec((s, tc), lambda j: (0, j))
    dg, du, acc = _pcall(
        body, name=name,
        out_shape=(jax.ShapeDtypeStruct((s, f), MXU_DTYPE), jax.ShapeDtypeStruct((s, f), MXU_DTYPE),
                   jax.ShapeDtypeStruct((4, SUBLANES, f), F32)),
        grid=(nj,),
        in_specs=[stripe, stripe, pl.BlockSpec((s, tc), lambda j: (0, j + nj)),
                  pl.BlockSpec((3, tc), lambda j: (0, j)), pl.BlockSpec((1, tc), lambda j: (0, j))],
        out_specs=(stripe, stripe, pl.BlockSpec((4, SUBLANES, tc), lambda j: (0, 0, j))),
        scratch_shapes=[pltpu.VMEM((s, tc), F32)],
        compiler_params=_params("parallel"))(da, gu, gu, cw, _vec(cb))
    acc = jnp.sum(acc, axis=1)
    return dg, du, acc[:3], acc[3]


def _add_rows(a, b, name):
    r = a.shape[0]
    tr = _pick(r, (1024, 512, 256, 128, 64, 32, 16, 8))

    def body(a_ref, b_ref, o_ref):
        o_ref[...] = a_ref[...] + b_ref[...]

    row = pl.BlockSpec((tr, LANES), lambda i: (i, 0))
    return _pcall(body, name=name, out_shape=jax.ShapeDtypeStruct(a.shape, a.dtype), grid=(r // tr,),
                  in_specs=[row, row], out_specs=row, compiler_params=_params("parallel"))(a, b)


def _adamw(gparts, w, m, v, name):
    r = w.shape[0]
    tr = _pick(r, (1024, 512, 256, 128, 64, 32, 16, 8))
    bc1 = 1.0 / (1.0 - ADAM_B1 ** ADAM_STEP)
    bc2 = 1.0 / (1.0 - ADAM_B2 ** ADAM_STEP)
    n = len(gparts)

    def body(*refs):
        w_ref, m_ref, v_ref, go_ref, d_ref, mo_ref, vo_ref = refs[n:]
        g = refs[0][...]
        for t in range(1, n):
            g = g + refs[t][...]
        mn = ADAM_B1 * m_ref[...] + (1.0 - ADAM_B1) * g
        vn = ADAM_B2 * v_ref[...] + (1.0 - ADAM_B2) * (g * g)
        m_hat = mn * bc1
        v_hat = vn * bc2
        go_ref[...] = g
        d_ref[...] = -ADAM_LR * (m_hat / (jnp.sqrt(v_hat) + ADAM_EPS) + ADAM_WD * w_ref[...])
        mo_ref[...] = mn
        vo_ref[...] = vn

    row = pl.BlockSpec((tr, LANES), lambda i: (i, 0))
    sds = jax.ShapeDtypeStruct((r, LANES), F32)
    return _pcall(
        body, name=name, out_shape=(sds, sds, sds, sds),
        grid=(r // tr,),
        in_specs=[pl.BlockSpec((None, tr, LANES), functools.partial(lambda slot, i: (slot, i, 0), slot))
                  for _, slot in gparts] + [row, row, row],
        out_specs=(row, row, row, row),
        compiler_params=_params("parallel"),
    )(*[a for a, _ in gparts], w, m, v)


def _sum_slots(gslots, name):
    n, r, _ = gslots.shape
    tr = _pick(r, (1024, 512, 256, 128, 64, 32, 16, 8))

    def body(g_ref, o_ref):
        g = g_ref[0]
        for t in range(1, n):
            g = g + g_ref[t]
        o_ref[...] = g

    return _pcall(
        body, name=name, out_shape=jax.ShapeDtypeStruct((r, LANES), F32),
        grid=(r // tr,),
        in_specs=[pl.BlockSpec((n, tr, LANES), lambda i: (0, i, 0))],
        out_specs=pl.BlockSpec((tr, LANES), lambda i: (i, 0)),
        compiler_params=_params("parallel"),
    )(gslots)


def _l2norm(x, eps=1e-6):
    return x * lax.rsqrt(jnp.sum(x * x, axis=-1, keepdims=True) + eps)


def _chunk_heads(t, n_heads, chunk):
    s, hd = t.shape
    return t.reshape(s // chunk, chunk, n_heads, hd // n_heads).transpose(2, 0, 1, 3)


def _unchunk_heads(t):
    h, n, c, d = t.shape
    return t.transpose(1, 2, 0, 3).reshape(n * c, h, d)


_NT = (((1,), (1,)), ((), ()))
_TN = (((0,), (0,)), ((), ()))


def _gdn_blocks(a, rev_from=None):
    h, _, r, c = a.shape
    if rev_from is None:
        return pl.BlockSpec((h, None, r, c), lambda n: (0, n, 0, 0))
    return pl.BlockSpec((h, None, r, c), lambda n: (0, rev_from - n, 0, 0))


def _gdn_scan_fwd_call(qg, w, u, qk, kd, e):
    H, n_chunks, c, dk = qg.shape
    dv = u.shape[-1]

    def body(qg_ref, w_ref, u_ref, qk_ref, kd_ref, e_ref, o_ref, sin_ref, vn_ref, state_ref):
        @pl.when(pl.program_id(0) == 0)
        def _():
            state_ref[...] = jnp.zeros_like(state_ref)

        for hd in range(H):
            st = state_ref[hd]
            stb = st.astype(MXU_DTYPE)
            sin_ref[hd] = st
            v_new = u_ref[hd] - jnp.dot(w_ref[hd].astype(MXU_DTYPE), stb, preferred_element_type=F32)
            vn_ref[hd] = v_new
            vnb = v_new.astype(MXU_DTYPE)
            o_ref[hd] = (jnp.dot(qg_ref[hd].astype(MXU_DTYPE), stb, preferred_element_type=F32)
                         + jnp.dot(qk_ref[hd].astype(MXU_DTYPE), vnb, preferred_element_type=F32))
            state_ref[hd] = st * e_ref[hd] + lax.dot_general(kd_ref[hd].astype(MXU_DTYPE), vnb, _TN,
                                                             preferred_element_type=F32)

    ins = (qg, w, u, qk, kd, e)
    outs = (jax.ShapeDtypeStruct((H, n_chunks, c, dv), F32), jax.ShapeDtypeStruct((H, n_chunks, dk, dv), F32),
            jax.ShapeDtypeStruct((H, n_chunks, c, dv), F32))
    return _pcall(
        body, name="gdn_scan_fwd", out_shape=outs, grid=(n_chunks,),
        in_specs=[_gdn_blocks(a) for a in ins], out_specs=tuple(_gdn_blocks(a) for a in outs),
        scratch_shapes=[pltpu.VMEM((H, dk, dv), F32)],
        compiler_params=_params("arbitrary"),
    )(*ins)


def _gdn_scan_bwd_call(qg, w, qk, kd, e, s_in, v_new, do):
    H, n_chunks, c, dk = qg.shape
    dv = v_new.shape[-1]

    def body(qg_ref, w_ref, qk_ref, kd_ref, e_ref, sin_ref, vn_ref, do_ref,
             dqg_ref, dw_ref, du_ref, dqk_ref, dkd_ref, de_ref, ds_ref):
        @pl.when(pl.program_id(0) == 0)
        def _():
            ds_ref[...] = jnp.zeros_like(ds_ref)

        for hd in range(H):
            st = sin_ref[hd]
            stb = st.astype(MXU_DTYPE)
            vnb = vn_ref[hd].astype(MXU_DTYPE)
            dob = do_ref[hd].astype(MXU_DTYPE)
            ds = ds_ref[hd]
            dsb = ds.astype(MXU_DTYPE)
            dvn = lax.dot_general(qk_ref[hd].astype(MXU_DTYPE), dob, _TN, preferred_element_type=F32)
            dqk_ref[hd] = lax.dot_general(dob, vnb, _NT, preferred_element_type=F32)
            dqg_ref[hd] = lax.dot_general(dob, stb, _NT, preferred_element_type=F32)
            ds_in = lax.dot_general(qg_ref[hd].astype(MXU_DTYPE), dob, _TN, preferred_element_type=F32)
            dvn = dvn + jnp.dot(kd_ref[hd].astype(MXU_DTYPE), dsb, preferred_element_type=F32)
            dkd_ref[hd] = lax.dot_general(vnb, dsb, _NT, preferred_element_type=F32)
            de_ref[hd] = _fold8(st * ds)
            ds_in = ds_in + ds * e_ref[hd]
            du_ref[hd] = dvn
            dvnb = dvn.astype(MXU_DTYPE)
            dw_ref[hd] = -lax.dot_general(dvnb, stb, _NT, preferred_element_type=F32)
            ds_ref[hd] = ds_in - lax.dot_general(w_ref[hd].astype(MXU_DTYPE), dvnb, _TN,
                                                 preferred_element_type=F32)

    last = n_chunks - 1
    ins = (qg, w, qk, kd, e, s_in, v_new, do)
    outs = (jax.ShapeDtypeStruct(qg.shape, F32), jax.ShapeDtypeStruct(w.shape, F32),
            jax.ShapeDtypeStruct(v_new.shape, F32), jax.ShapeDtypeStruct(qk.shape, F32),
            jax.ShapeDtypeStruct(kd.shape, F32), jax.ShapeDtypeStruct((H, n_chunks, SUBLANES, dv), F32))
    return _pcall(
        body, name="gdn_scan_bwd", out_shape=outs, grid=(n_chunks,),
        in_specs=[_gdn_blocks(a, last) for a in ins], out_specs=tuple(_gdn_blocks(a, last) for a in outs),
        scratch_shapes=[pltpu.VMEM((H, dk, dv), F32)],
        compiler_params=_params("arbitrary"),
    )(*ins)


@jax.custom_vjp
def _gdn_scan(qg, w, u, qk, kd, e):
    return _gdn_scan_fwd_call(qg, w, u, qk, kd, e)[0]


def _gdn_scan_fwd(qg, w, u, qk, kd, e):
    o, s_in, v_new = _gdn_scan_fwd_call(qg, w, u, qk, kd, e)
    return o, (qg, w, qk, kd, e, s_in, v_new)


def _gdn_scan_bwd(res, do):
    dqg, dw, du, dqk, dkd, de = _gdn_scan_bwd_call(*res, do)
    return dqg, dw, du, dqk, dkd, jnp.sum(de, axis=2, keepdims=True)


_gdn_scan.defvjp(_gdn_scan_fwd, _gdn_scan_bwd)


def _conv_taps(x_ref, r, rc, taps):
    r0 = pl.multiple_of(r * rc, rc)
    cur = x_ref[pl.ds(r0, rc), :]
    hs = pl.multiple_of(jnp.maximum(r0 - SUBLANES, 0), SUBLANES)
    halo = jnp.where(r > 0, x_ref[pl.ds(hs, SUBLANES), :], 0.0)
    return r0, [cur] + [_shift_down(cur, halo, k) for k in range(1, taps)]


def _conv_pre(w_ref, xs):
    taps = len(xs)
    gc = w_ref[taps - 1:taps, :] * xs[0]
    for j in range(taps - 1):
        gc = gc + w_ref[j:j + 1, :] * xs[taps - 1 - j]
    return gc


def _dwconv_silu_fwd_call(x, w):
    s, c = x.shape
    taps = w.shape[0]
    tc = _pick(c, (CONV_STRIPE,))
    rc = _pick(s, (CONV_ROWS, 128, 64, 32, 16, 8))

    def body(x_ref, w_ref, o_ref):
        def chunk(r, carry):
            r0, xs = _conv_taps(x_ref, r, rc, taps)
            gc = _conv_pre(w_ref, xs)
            o_ref[pl.ds(r0, rc), :] = gc * jax.nn.sigmoid(gc)
            return carry

        lax.fori_loop(0, s // rc, chunk, 0)

    stripe = pl.BlockSpec((s, tc), lambda j: (0, j))
    return _pcall(body, name="gdn_conv_fwd", out_shape=jax.ShapeDtypeStruct((s, c), F32), grid=(c // tc,),
                  in_specs=[stripe, pl.BlockSpec((taps, tc), lambda j: (0, j))], out_specs=stripe,
                  compiler_params=_params("parallel"))(x, w)


def _dwconv_silu_bwd_call(x, w, dy):
    s, c = x.shape
    taps = w.shape[0]
    tc = _pick(c, (CONV_STRIPE,))
    rc = _pick(s, (CONV_ROWS, 128, 64, 32, 16, 8))
    nr = s // rc

    def body(x_ref, w_ref, dy_ref, dx_ref, acc_ref, dgc_ref):
        def chunk1(r, acc):
            r0, xs = _conv_taps(x_ref, r, rc, taps)
            gc = _conv_pre(w_ref, xs)
            sg = jax.nn.sigmoid(gc)
            dgc = dy_ref[pl.ds(r0, rc), :] * (sg * (1.0 + gc * (1.0 - sg)))
            dgc_ref[pl.ds(r0, rc), :] = dgc
            return tuple(acc[j] + _fold8(dgc * xs[taps - 1 - j]) for j in range(taps))

        zero = jnp.zeros((SUBLANES, tc), F32)
        acc = lax.fori_loop(0, nr, chunk1, (zero,) * taps)
        for j in range(taps):
            acc_ref[j] = acc[j]

        def chunk2(r, carry):
            r0 = pl.multiple_of(r * rc, rc)
            cur = dgc_ref[pl.ds(r0, rc), :]
            hs = pl.multiple_of(jnp.minimum(r0 + rc, s - SUBLANES), SUBLANES)
            halo = jnp.where(r < nr - 1, dgc_ref[pl.ds(hs, SUBLANES), :], 0.0)
            dx = w_ref[taps - 1:taps, :] * cur
            for j in range(taps - 1):
                dx = dx + w_ref[j:j + 1, :] * _shift_up(cur, halo, taps - 1 - j)
            dx_ref[pl.ds(r0, rc), :] = dx
            return carry

        lax.fori_loop(0, nr, chunk2, 0)

    stripe = pl.BlockSpec((s, tc), lambda j: (0, j))
    dx, acc = _pcall(
        body, name="gdn_conv_bwd",
        out_shape=(jax.ShapeDtypeStruct((s, c), F32), jax.ShapeDtypeStruct((taps, SUBLANES, c), F32)),
        grid=(c // tc,),
        in_specs=[stripe, pl.BlockSpec((taps, tc), lambda j: (0, j)), stripe],
        out_specs=(stripe, pl.BlockSpec((taps, SUBLANES, tc), lambda j: (0, 0, j))),
        scratch_shapes=[pltpu.VMEM((s, tc), F32)],
        compiler_params=_params("parallel"))(x, w, dy)
    return dx, jnp.sum(acc, axis=1)


@jax.custom_vjp
def _dwconv_silu(x, w):
    return _dwconv_silu_fwd_call(x, w)


def _dwconv_silu_fwd(x, w):
    return _dwconv_silu_fwd_call(x, w), (x, w)


def _dwconv_silu_bwd(res, dy):
    return _dwconv_silu_bwd_call(*res, dy)


_dwconv_silu.defvjp(_dwconv_silu_fwd, _dwconv_silu_bwd)


def _gated_deltanet(hx, p):
    H, C = GDN_HEADS, GDN_CHUNK
    s, d = hx.shape
    dk = dv = d // H
    qkvz = _linear(hx, p['gdn_w_qkvz'], "gdn_in")
    ab = _linear(hx, p['gdn_w_ab'], "gdn_ab")
    qkv, z = qkvz[:, :3 * d], qkvz[:, 3 * d:]
    a, bt = ab[:, :H], ab[:, H:2 * H]
    qkv = _dwconv_silu(qkv, p['gdn_conv_w'])
    q, k, v = qkv[:, :d], qkv[:, d:2 * d], qkv[:, 2 * d:]
    q = _l2norm(_chunk_heads(q, H, C)) * (dk ** -0.5)
    k = _l2norm(_chunk_heads(k, H, C))
    v = _chunk_heads(v, H, C)
    beta = jax.nn.sigmoid(_chunk_heads(bt, H, C)[..., 0])
    g = -jnp.exp(p['gdn_a_log'])[:, None, None] * jax.nn.softplus(
        _chunk_heads(a, H, C)[..., 0] + p['gdn_dt_bias'][:, None, None])
    gc = jnp.cumsum(g, axis=-1)
    idx = jnp.arange(C)
    causal = idx[:, None] >= idx[None, :]
    strict = idx[:, None] > idx[None, :]
    diff = gc[..., :, None] - gc[..., None, :]
    decay = jnp.where(causal, jnp.exp(jnp.where(causal, diff, 0.0)), 0.0)
    kb = k * beta[..., None]
    kk = jnp.where(strict, jnp.einsum('hncd,hnmd->hncm', kb, k) * decay, 0.0)
    eye = jnp.eye(C, dtype=F32)
    rhs = jnp.concatenate([v * beta[..., None], kb * jnp.exp(gc)[..., None]], axis=-1)
    sol = lax.linalg.triangular_solve(kk + eye, rhs, left_side=True, lower=True, unit_diagonal=True)
    u, w = sol[..., :dv], sol[..., dv:]
    qk = jnp.where(causal, jnp.einsum('hncd,hnmd->hncm', q, k) * decay, 0.0)

    g_last = gc[..., -1:]
    e = jnp.broadcast_to(jnp.exp(g_last)[..., None], gc.shape[:2] + (1, dv))
    o = _gdn_scan(q * jnp.exp(gc)[..., None], w, u, qk, k * jnp.exp(g_last - gc)[..., None], e)
    o = _unchunk_heads(o)
    o = o * lax.rsqrt(jnp.mean(o * o, axis=-1, keepdims=True) + 1e-6) * p['gdn_norm_w']
    o = o * jax.nn.silu(z.reshape(s, H, dv))
    return _linear(o.reshape(s, H * dv), p['gdn_w_out'], "gdn_out")


def _ret_consts(c):
    log_gamma = jnp.log(1.0 - jnp.power(2.0, -5.0 - jnp.arange(RET_HEADS, dtype=F32)))
    idx = jnp.arange(c, dtype=F32)
    rel = idx[:, None] - idx[None, :]
    dmask = jnp.where(rel >= 0, jnp.exp(jnp.maximum(rel, 0.0) * log_gamma[:, None, None]), 0.0)
    zeta = jnp.exp((c - 1.0 - idx)[None, :] * log_gamma[:, None])[..., None]
    xi = jnp.exp((idx + 1.0)[None, :] * log_gamma[:, None])[..., None]
    gamma_c = jnp.exp(c * log_gamma)[:, None, None]
    return dmask, zeta, xi, gamma_c


def _ret_angles(s, dk):
    pos = jnp.arange(s, dtype=F32)
    inv_freq = RET_ROPE_BASE ** (-jnp.linspace(0.0, 1.0, dk // 2, dtype=F32))
    ang = pos[:, None] * inv_freq[None, :]
    return jnp.cos(ang), jnp.sin(ang)


def _rot(t, cs, sn):
    half = t.shape[1] // 2
    t1, t2 = t[:, :half], t[:, half:]
    return jnp.concatenate([t1 * cs - t2 * sn, t1 * sn + t2 * cs], axis=1)


def _rot_t(t, cs, sn):
    half = t.shape[1] // 2
    t1, t2 = t[:, :half], t[:, half:]
    return jnp.concatenate([t1 * cs + t2 * sn, t2 * cs - t1 * sn], axis=1)


def _ret_cols(d, dk, dv, hd):
    return (slice(hd * dk, (hd + 1) * dk), slice(d + hd * dk, d + (hd + 1) * dk),
            slice(2 * d + hd * dv, 2 * d + (hd + 1) * dv), slice(4 * d + hd * dv, 4 * d + (hd + 1) * dv))


def _ret_fwd_call(proj):
    s, d6 = proj.shape
    d = d6 // 6
    H, c = RET_HEADS, RET_CHUNK
    dk, dv = d // H, 2 * d // H
    n_chunks = s // c
    kscale = dk ** -0.5
    cos_a, sin_a = _ret_angles(s, dk)
    consts = _ret_consts(c)

    def body(p_ref, cos_ref, sin_ref, dm_ref, ze_ref, xi_ref, gc_ref, out_ref, oraw_ref, st_ref, state_ref):
        @pl.when(pl.program_id(0) == 0)
        def _():
            state_ref[...] = jnp.zeros_like(state_ref)

        cs, sn = cos_ref[...], sin_ref[...]
        for hd in range(H):
            qc, kc, vc, gcol = _ret_cols(d, dk, dv, hd)
            ocol = slice(hd * dv, (hd + 1) * dv)
            qb = _rot(p_ref[:, qc], cs, sn).astype(MXU_DTYPE)
            kr = _rot(p_ref[:, kc], cs, sn) * kscale
            kb = kr.astype(MXU_DTYPE)
            vb = p_ref[:, vc].astype(MXU_DTYPE)
            st = state_ref[hd]
            stb = st.astype(MXU_DTYPE)
            st_ref[hd] = stb
            sc = lax.dot_general(qb, kb, _NT, preferred_element_type=F32) * dm_ref[hd]
            o = (jnp.dot(sc.astype(MXU_DTYPE), vb, preferred_element_type=F32)
                 + jnp.dot(qb, stb, preferred_element_type=F32) * xi_ref[hd])
            state_ref[hd] = st * gc_ref[hd] + lax.dot_general((kr * ze_ref[hd]).astype(MXU_DTYPE), vb, _TN,
                                                              preferred_element_type=F32)
            oraw_ref[:, ocol] = o
            oc = o - jnp.mean(o, axis=-1, keepdims=True)
            on = oc * lax.rsqrt(jnp.mean(oc * oc, axis=-1, keepdims=True) + 1e-6)
            gate = p_ref[:, gcol]
            out_ref[:, ocol] = on * (gate * jax.nn.sigmoid(gate))

    row = lambda width: pl.BlockSpec((c, width), lambda n: (n, 0))
    whole = lambda a: pl.BlockSpec(a.shape, lambda n: (0,) * a.ndim)
    return _pcall(
        body, name="ret_fwd",
        out_shape=(jax.ShapeDtypeStruct((s, 2 * d), F32), jax.ShapeDtypeStruct((s, 2 * d), F32),
                   jax.ShapeDtypeStruct((n_chunks, H, dk, dv), MXU_DTYPE)),
        grid=(n_chunks,),
        in_specs=[row(d6), row(dk // 2), row(dk // 2)] + [whole(a) for a in consts],
        out_specs=(row(2 * d), row(2 * d), pl.BlockSpec((None, H, dk, dv), lambda n: (n, 0, 0, 0))),
        scratch_shapes=[pltpu.VMEM((H, dk, dv), F32)],
        compiler_params=_params("arbitrary"),
    )(proj, cos_a, sin_a, *consts)


def _ret_bwd_call(proj, oraw, states, dout):
    s, d6 = proj.shape
    d = d6 // 6
    H, c = RET_HEADS, RET_CHUNK
    dk, dv = d // H, 2 * d // H
    n_chunks = s // c
    kscale = dk ** -0.5
    cos_a, sin_a = _ret_angles(s, dk)
    consts = _ret_consts(c)

    def body(p_ref, cos_ref, sin_ref, dm_ref, ze_ref, xi_ref, gc_ref, oraw_ref, st_ref, do_ref, dp_ref, ds_ref):
        @pl.when(pl.program_id(0) == 0)
        def _():
            ds_ref[...] = jnp.zeros_like(ds_ref)

        cs, sn = cos_ref[...], sin_ref[...]
        for hd in range(H):
            qc, kc, vc, gcol = _ret_cols(d, dk, dv, hd)
            ocol = slice(hd * dv, (hd + 1) * dv)
            qb = _rot(p_ref[:, qc], cs, sn).astype(MXU_DTYPE)
            kr = _rot(p_ref[:, kc], cs, sn) * kscale
            kb = kr.astype(MXU_DTYPE)
            vb = p_ref[:, vc].astype(MXU_DTYPE)
            gate = p_ref[:, gcol]
            o = oraw_ref[:, ocol]
            oc = o - jnp.mean(o, axis=-1, keepdims=True)
            rstd = lax.rsqrt(jnp.mean(oc * oc, axis=-1, keepdims=True) + 1e-6)
            on = oc * rstd
            dout_h = do_ref[:, ocol]
            sg = jax.nn.sigmoid(gate)
            dp_ref[:, gcol] = dout_h * on * (sg * (1.0 + gate * (1.0 - sg)))
            don = dout_h * (gate * sg)
            do_raw = rstd * (don - jnp.mean(don, axis=-1, keepdims=True)
                             - on * jnp.mean(don * on, axis=-1, keepdims=True))
            dob = do_raw.astype(MXU_DTYPE)
            stb = st_ref[hd]
            ds = ds_ref[hd]
            dsb = ds.astype(MXU_DTYPE)
            dm = dm_ref[hd]
            scb = (lax.dot_general(qb, kb, _NT, preferred_element_type=F32) * dm).astype(MXU_DTYPE)
            dsc = (lax.dot_general(dob, vb, _NT, preferred_element_type=F32) * dm).astype(MXU_DTYPE)
            dqr = jnp.dot(dsc, kb, preferred_element_type=F32)
            dkr = lax.dot_general(dsc, qb, _TN, preferred_element_type=F32)
            dvv = lax.dot_general(scb, dob, _TN, preferred_element_type=F32)
            doi = (do_raw * xi_ref[hd]).astype(MXU_DTYPE)
            dqr = dqr + lax.dot_general(doi, stb, _NT, preferred_element_type=F32)
            ds_in = lax.dot_general(qb, doi, _TN, preferred_element_type=F32)
            ze = ze_ref[hd]
            dkr = dkr + lax.dot_general(vb, dsb, _NT, preferred_element_type=F32) * ze
            dvv = dvv + jnp.dot((kr * ze).astype(MXU_DTYPE), dsb, preferred_element_type=F32)
            ds_ref[hd] = ds * gc_ref[hd] + ds_in
            dp_ref[:, qc] = _rot_t(dqr, cs, sn)
            dp_ref[:, kc] = _rot_t(dkr * kscale, cs, sn)
            dp_ref[:, vc] = dvv

    last = n_chunks - 1
    row = lambda width: pl.BlockSpec((c, width), lambda n: (last - n, 0))
    whole = lambda a: pl.BlockSpec(a.shape, lambda n: (0,) * a.ndim)
    return _pcall(
        body, name="ret_bwd", out_shape=jax.ShapeDtypeStruct((s, d6), F32),
        grid=(n_chunks,),
        in_specs=[row(d6), row(dk // 2), row(dk // 2)] + [whole(a) for a in consts]
                 + [row(2 * d), pl.BlockSpec((None, H, dk, dv), lambda n: (last - n, 0, 0, 0)), row(2 * d)],
        out_specs=row(d6),
        scratch_shapes=[pltpu.VMEM((H, dk, dv), F32)],
        compiler_params=_params("arbitrary"),
    )(proj, cos_a, sin_a, *consts, oraw, states, dout)


@jax.custom_vjp
def _ret_core(proj):
    return _ret_fwd_call(proj)[0]


def _ret_core_fwd(proj):
    out, oraw, states = _ret_fwd_call(proj)
    return out, (proj, oraw, states)


def _ret_core_bwd(res, dout):
    return (_ret_bwd_call(*res, dout),)


_ret_core.defvjp(_ret_core_fwd, _ret_core_bwd)


def _retention(hx, p):
    return _linear(_ret_core(_linear(hx, p['ret_w_in'], "ret_in")), p['ret_w_out'], "ret_out")


SQRT_HALF = 2.0 ** -0.5
INV_SQRT_2PI = (2.0 * math.pi) ** -0.5


def _gmlp_front(p_ref, g_ref, b_ref, w):
    x = p_ref[...]
    cdf = 0.5 * (1.0 + lax.erf(x * SQRT_HALF))
    uv = x * cdf
    u, v = uv[:, :w], uv[:, w:]
    vc = v - jnp.mean(v, axis=-1, keepdims=True)
    rstd = lax.rsqrt(jnp.mean(vc * vc, axis=-1, keepdims=True) + LN_EPS)
    vhat = vc * rstd
    return x, cdf, u, vhat, rstd, vhat * g_ref[...] + b_ref[...]


def _gmlp_fwd_call(proj, ln_g, ln_b, ws, bs):
    s, w2 = proj.shape
    w = w2 // 2
    c, G = GMLP_CHUNK, GMLP_GROUPS
    gw = w // G

    def body(p_ref, g_ref, b_ref, ws_ref, bs_ref, o_ref):
        _, _, u, _, _, vn = _gmlp_front(p_ref, g_ref, b_ref, w)
        for gi in range(G):
            cols = slice(gi * gw, (gi + 1) * gw)
            vs = jnp.dot(ws_ref[gi].astype(MXU_DTYPE), vn[:, cols].astype(MXU_DTYPE),
                         preferred_element_type=F32) + bs_ref[gi]
            o_ref[:, cols] = u[:, cols] * vs

    whole = lambda a: pl.BlockSpec(a.shape, lambda n: (0,) * a.ndim)
    args = (_vec(ln_g), _vec(ln_b), ws, bs)
    return _pcall(
        body, name="gmlp_fwd", out_shape=jax.ShapeDtypeStruct((s, w), F32), grid=(s // c,),
        in_specs=[pl.BlockSpec((c, w2), lambda n: (n, 0))] + [whole(a) for a in args],
        out_specs=pl.BlockSpec((c, w), lambda n: (n, 0)),
        compiler_params=_params("parallel"),
    )(proj, *args)


def _gmlp_bwd_call(proj, ln_g, ln_b, ws, bs, dout):
    s, w2 = proj.shape
    w = w2 // 2
    c, G = GMLP_CHUNK, GMLP_GROUPS
    gw = w // G

    def body(p_ref, g_ref, b_ref, ws_ref, bs_ref, do_ref, dp_ref, dws_ref, dbs_ref, dgb_ref):
        @pl.when(pl.program_id(0) == 0)
        def _():
            dws_ref[...] = jnp.zeros_like(dws_ref)
            dbs_ref[...] = jnp.zeros_like(dbs_ref)
            dgb_ref[...] = jnp.zeros_like(dgb_ref)

        x, cdf, u, vhat, rstd, vn = _gmlp_front(p_ref, g_ref, b_ref, w)
        dout = do_ref[...]
        du_parts, dvn_parts = [], []
        for gi in range(G):
            cols = slice(gi * gw, (gi + 1) * gw)
            wsg = ws_ref[gi].astype(MXU_DTYPE)
            vng = vn[:, cols].astype(MXU_DTYPE)
            vs = jnp.dot(wsg, vng, preferred_element_type=F32) + bs_ref[gi]
            du_parts.append(dout[:, cols] * vs)
            dvs = dout[:, cols] * u[:, cols]
            dbs_ref[:, cols] += dvs
            dvsb = dvs.astype(MXU_DTYPE)
            dws_ref[gi] += lax.dot_general(dvsb, vng, _NT, preferred_element_type=F32)
            dvn_parts.append(lax.dot_general(wsg, dvsb, _TN, preferred_element_type=F32))
        dvn = jnp.concatenate(dvn_parts, axis=1)
        dgb_ref[0] += _fold8(dvn * vhat)
        dgb_ref[1] += _fold8(dvn)
        dvh = dvn * g_ref[...]
        dv = rstd * (dvh - jnp.mean(dvh, axis=-1, keepdims=True)
                     - vhat * jnp.mean(dvh * vhat, axis=-1, keepdims=True))
        duv = jnp.concatenate(du_parts + [dv], axis=1)
        dp_ref[...] = duv * (cdf + x * (jnp.exp(-0.5 * x * x) * INV_SQRT_2PI))

    whole = lambda a: pl.BlockSpec(a.shape, lambda n: (0,) * a.ndim)
    args = (_vec(ln_g), _vec(ln_b), ws, bs)
    acc = lambda *shape: pl.BlockSpec(shape, lambda n: (0,) * len(shape))
    return _pcall(
        body, name="gmlp_bwd",
        out_shape=(jax.ShapeDtypeStruct((s, w2), F32), jax.ShapeDtypeStruct((G, c, c), F32),
                   jax.ShapeDtypeStruct((c, w), F32), jax.ShapeDtypeStruct((2, SUBLANES, w), F32)),
        grid=(s // c,),
        in_specs=[pl.BlockSpec((c, w2), lambda n: (n, 0))] + [whole(a) for a in args]
                 + [pl.BlockSpec((c, w), lambda n: (n, 0))],
        out_specs=(pl.BlockSpec((c, w2), lambda n: (n, 0)), acc(G, c, c), acc(c, w), acc(2, SUBLANES, w)),
        compiler_params=_params("arbitrary"),
    )(proj, *args, dout)


def _gmlp_mask(c):
    return jnp.tril(jnp.ones((c, c), dtype=bool))


@jax.custom_vjp
def _gmlp_core(proj, ln_g, ln_b, w_s, b_s):
    ws = jnp.where(_gmlp_mask(GMLP_CHUNK), w_s, 0.0)
    return _gmlp_fwd_call(proj, ln_g, ln_b, ws, b_s[..., None])


def _gmlp_core_fwd(proj, ln_g, ln_b, w_s, b_s):
    return _gmlp_core(proj, ln_g, ln_b, w_s, b_s), (proj, ln_g, ln_b, w_s, b_s)


def _gmlp_core_bwd(res, dout):
    proj, ln_g, ln_b, w_s, b_s = res
    mask = _gmlp_mask(GMLP_CHUNK)
    dproj, dws, dbs, dgb = _gmlp_bwd_call(proj, ln_g, ln_b, jnp.where(mask, w_s, 0.0), b_s[..., None], dout)
    dgb = jnp.sum(dgb, axis=1)
    c = GMLP_CHUNK
    db_s = jnp.sum(dbs.reshape(c, GMLP_GROUPS, -1), axis=-1).T
    return dproj, dgb[0], dgb[1], jnp.where(mask, dws, 0.0), db_s


_gmlp_core.defvjp(_gmlp_core_fwd, _gmlp_core_bwd)


def _chunked_gmlp(hx, p):
    core = _gmlp_core(_linear(hx, p['gmlp_w_in'], "gmlp_in"), p['gmlp_ln_g'], p['gmlp_ln_b'],
                      p['gmlp_w_s'], p['gmlp_b_s'])
    return _linear(core, p['gmlp_w_out'], "gmlp_out")


def _stick_breaking(hx, p):
    H = SB_HEADS
    s, d = hx.shape
    dh = d // H
    qkv = _linear(hx, p['sb_w_in'], "sb_in")
    q, k, v = (qkv[:, j * d:(j + 1) * d].reshape(s, H, dh).transpose(1, 0, 2) for j in range(3))
    o = _sb_core(q, k, v)
    return _linear(o.transpose(1, 0, 2).reshape(s, d), p['sb_w_out'], "sb_out")


MIXERS = (_gated_deltanet, _retention, _chunked_gmlp, _stick_breaking)


def _trunk_grad(x, mods, p, target):
    d = x.shape[-1]
    saved = []
    for i in range(DEPTH):
        sh1, sc1, g1, sh2, sc2, g2 = (mods[i, j * d:(j + 1) * d] for j in range(6))
        h1 = _modulate(x, sc1, sh1, F32, "mod_a%d" % i)
        y1, mixer_vjp = jax.vjp(MIXERS[i], h1, {n: p[n] for n in MIXER_PARAMS[i]})
        x1 = _resid_ln(x, y1, g1, p['ln_g'][i, 0], p['ln_b'][i, 0], "ln_a%d" % i)
        h2 = _modulate(x1, sc2, sh2, MXU_DTYPE, "mod_b%d" % i)
        gu = _mm(h2, p['ffn_up'][i], 'nn', "ffn_up%d_fwd" % i)
        act = _ffn_gate(gu, p['ffn_conv_w'][i], p['ffn_conv_b'][i], "ffn_gate%d" % i)
        y2 = _mm(act, p['ffn_down'][i], 'nn', "ffn_down%d_fwd" % i)
        x2 = _resid_ln(x1, y2, g2, p['ln_g'][i, 1], p['ln_b'][i, 1], "ln_b%d" % i)
        saved.append((x, y1, mixer_vjp, x1, h2, gu, act, y2))
        x = x2
    loss, dx = _loss_head(x, target, "loss_head")

    dp = {n: None for n in p}
    d_ln_g, d_ln_b, d_up, d_down, d_cw, d_cb, dmods = [], [], [], [], [], [], []
    for i in reversed(range(DEPTH)):
        x0, y1, mixer_vjp, x1, h2, gu, act, y2 = saved[i]
        sh1, sc1, g1, sh2, sc2, g2 = (mods[i, j * d:(j + 1) * d] for j in range(6))
        dxa, dy2, dgam2, dbet2, dg2 = _resid_ln_bwd(x1, y2, g2, p['ln_g'][i, 1], dx, "ln_b%d_bwd" % i)
        dact = _mm(dy2, p['ffn_down'][i], 'nt', "ffn_down%d_dx" % i)
        d_down.append(_mm(act, dy2, 'tn', "ffn_down%d_dw" % i))
        dgate, dupp, dcw, dcb = _ffn_gate_bwd(dact, gu, p['ffn_conv_w'][i], p['ffn_conv_b'][i],
                                              "ffn_gate%d_bwd" % i)
        dgu = jnp.concatenate([dgate, dupp], axis=1)
        dh2 = _mm(dgu, p['ffn_up'][i], 'nt', "ffn_up%d_dx" % i)
        d_up.append(_mm(h2, dgu, 'tn', "ffn_up%d_dw" % i))
        dx1, dsc2, dsh2 = _modulate_bwd(dxa, dh2, x1, sc2, "mod_b%d_bwd" % i)
        dxa, dy1, dgam1, dbet1, dg1 = _resid_ln_bwd(x0, y1, g1, p['ln_g'][i, 0], dx1, "ln_a%d_bwd" % i)
        dh1, dmix = mixer_vjp(dy1)
        dp.update(dmix)
        dx, dsc1, dsh1 = _modulate_bwd(dxa, dh1, x0, sc1, "mod_a%d_bwd" % i)
        d_ln_g.append(jnp.stack([dgam1, dgam2]))
        d_ln_b.append(jnp.stack([dbet1, dbet2]))
        d_cw.append(dcw)
        d_cb.append(dcb)
        dmods.append(jnp.concatenate([dsh1, dsc1, dg1, dsh2, dsc2, dg2]))
    for n, parts in (('ln_g', d_ln_g), ('ln_b', d_ln_b), ('ffn_up', d_up), ('ffn_down', d_down),
                     ('ffn_conv_w', d_cw), ('ffn_conv_b', d_cb)):
        dp[n] = jnp.stack(parts[::-1])
    return loss, dx, jnp.stack(dmods[::-1]), dp


def _join(blocks, axis):
    return jnp.concatenate([blocks[d] for d in range(N_DEV)], axis=axis)


def _split(whole, axis):
    n = whole.shape[axis] // N_DEV
    return jnp.stack([lax.slice_in_dim(whole, d * n, (d + 1) * n, axis=axis) for d in range(N_DEV)])


def _pad8(a):
    pad = (-a.shape[0]) % 8
    return jnp.pad(a, ((0, pad), (0, 0))) if pad else a


def _pack_big_grads(full_grads, axes):
    per_dev = jnp.concatenate([_split(g, ax).reshape(N_DEV, -1) for g, ax in zip(full_grads, axes)], axis=1)
    pad = (-per_dev.shape[1]) % (BIG_ROW_ALIGN * LANES)
    if pad:
        per_dev = jnp.pad(per_dev, ((0, 0), (0, pad)))
    return per_dev.reshape(N_DEV, -1, LANES)


def kernel(x, c, cond_w, cond_b, ada_w, ada_b, ln_g, ln_b, ffn_up, ffn_conv_w, ffn_conv_b, ffn_down, gdn_w_in, gdn_conv_w, gdn_a_log, gdn_dt_bias, gdn_norm_w, gdn_w_out, ret_w_in, ret_w_out, gmlp_w_in, gmlp_ln_g, gmlp_ln_b, gmlp_w_s, gmlp_b_s, gmlp_w_out, sb_w_in, sb_w_out, loss_target, m_cond_w, m_cond_b, m_ada_w, m_ada_b, m_ln_g, m_ln_b, m_ffn_up, m_ffn_conv_w, m_ffn_conv_b, m_ffn_down, m_gdn_w_in, m_gdn_conv_w, m_gdn_a_log, m_gdn_dt_bias, m_gdn_norm_w, m_gdn_w_out, m_ret_w_in, m_ret_w_out, m_gmlp_w_in, m_gmlp_ln_g, m_gmlp_ln_b, m_gmlp_w_s, m_gmlp_b_s, m_gmlp_w_out, m_sb_w_in, m_sb_w_out, v_cond_w, v_cond_b, v_ada_w, v_ada_b, v_ln_g, v_ln_b, v_ffn_up, v_ffn_conv_w, v_ffn_conv_b, v_ffn_down, v_gdn_w_in, v_gdn_conv_w, v_gdn_a_log, v_gdn_dt_bias, v_gdn_norm_w, v_gdn_w_out, v_ret_w_in, v_ret_w_out, v_gmlp_w_in, v_gmlp_ln_g, v_gmlp_ln_b, v_gmlp_w_s, v_gmlp_b_s, v_gmlp_w_out, v_sb_w_in, v_sb_w_out):
    w = dict(cond_w=cond_w, cond_b=cond_b, ada_w=ada_w, ada_b=ada_b, ln_g=ln_g, ln_b=ln_b, ffn_up=ffn_up,
             ffn_conv_w=ffn_conv_w, ffn_conv_b=ffn_conv_b, ffn_down=ffn_down, gdn_w_in=gdn_w_in,
             gdn_conv_w=gdn_conv_w, gdn_a_log=gdn_a_log, gdn_dt_bias=gdn_dt_bias, gdn_norm_w=gdn_norm_w,
             gdn_w_out=gdn_w_out, ret_w_in=ret_w_in, ret_w_out=ret_w_out, gmlp_w_in=gmlp_w_in,
             gmlp_ln_g=gmlp_ln_g, gmlp_ln_b=gmlp_ln_b, gmlp_w_s=gmlp_w_s, gmlp_b_s=gmlp_b_s,
             gmlp_w_out=gmlp_w_out, sb_w_in=sb_w_in, sb_w_out=sb_w_out)
    mom = dict(cond_w=m_cond_w, cond_b=m_cond_b, ada_w=m_ada_w, ada_b=m_ada_b, ln_g=m_ln_g, ln_b=m_ln_b,
               ffn_up=m_ffn_up, ffn_conv_w=m_ffn_conv_w, ffn_conv_b=m_ffn_conv_b, ffn_down=m_ffn_down,
               gdn_w_in=m_gdn_w_in, gdn_conv_w=m_gdn_conv_w, gdn_a_log=m_gdn_a_log, gdn_dt_bias=m_gdn_dt_bias,
               gdn_norm_w=m_gdn_norm_w, gdn_w_out=m_gdn_w_out, ret_w_in=m_ret_w_in, ret_w_out=m_ret_w_out,
               gmlp_w_in=m_gmlp_w_in, gmlp_ln_g=m_gmlp_ln_g, gmlp_ln_b=m_gmlp_ln_b, gmlp_w_s=m_gmlp_w_s,
               gmlp_b_s=m_gmlp_b_s, gmlp_w_out=m_gmlp_w_out, sb_w_in=m_sb_w_in, sb_w_out=m_sb_w_out)
    var = dict(cond_w=v_cond_w, cond_b=v_cond_b, ada_w=v_ada_w, ada_b=v_ada_b, ln_g=v_ln_g, ln_b=v_ln_b,
               ffn_up=v_ffn_up, ffn_conv_w=v_ffn_conv_w, ffn_conv_b=v_ffn_conv_b, ffn_down=v_ffn_down,
               gdn_w_in=v_gdn_w_in, gdn_conv_w=v_gdn_conv_w, gdn_a_log=v_gdn_a_log, gdn_dt_bias=v_gdn_dt_bias,
               gdn_norm_w=v_gdn_norm_w, gdn_w_out=v_gdn_w_out, ret_w_in=v_ret_w_in, ret_w_out=v_ret_w_out,
               gmlp_w_in=v_gmlp_w_in, gmlp_ln_g=v_gmlp_ln_g, gmlp_ln_b=v_gmlp_ln_b, gmlp_w_s=v_gmlp_w_s,
               gmlp_b_s=v_gmlp_b_s, gmlp_w_out=v_gmlp_w_out, sb_w_in=v_sb_w_in, sb_w_out=v_sb_w_out)

    me = _my_id()
    x = x[0]
    target = loss_target[0]
    d = x.shape[-1]
    dsh = d // N_DEV
    msh = ada_w.shape[-1]

    c_all = _exchange(_pad8(c), False, "gather_c")[:, 0, :]
    c_mine = lax.dynamic_slice_in_dim(c_all, me * dsh, dsh, axis=1)
    pre_part = _mm(c_mine, cond_w, 'nn', "cond_fwd")
    pre = jnp.sum(_exchange(pre_part, False, "gather_pre"), axis=0) + cond_b
    e_all = jax.nn.silu(pre)
    mod_part = jnp.concatenate([_mm(e_all, ada_w[i], 'nn', "ada_fwd%d" % i) for i in range(DEPTH)], axis=0)
    mod_all = _exchange(mod_part, False, "gather_mod")
    mod_all = mod_all.reshape(N_DEV, DEPTH, N_DEV, msh)
    mods = lax.dynamic_index_in_dim(mod_all, me, axis=2, keepdims=False)
    mods = mods.transpose(1, 0, 2).reshape(DEPTH, N_DEV * msh) + ada_b

    big_names = list(BIG)
    packed = _pack_rows([w[n] for n in big_names], BF16, BIG_ROW_ALIGN)
    gathered = _gather_two_level(packed, "gather_weights")
    blocks = _unpack_rows(gathered, [w[n].shape for n in big_names])
    p = {n: _join(b, BIG[n]) for n, b in zip(big_names, blocks)}
    for n in big_names:
        if not n.startswith('ffn_'):
            p[n] = p[n].astype(F32)
    sm_names = list(SMALL_SHARDED)
    sm_packed = _pack_rows([w[n] for n in sm_names], F32)
    sm_blocks = _unpack_rows(_exchange(sm_packed, False, "gather_small"), [w[n].shape for n in sm_names])
    for n, b in zip(sm_names, sm_blocks):
        p[n] = _join(b, SMALL_SHARDED[n])
    for n in SMALL_REPL:
        p[n] = w[n]
    n_qkvz = 4 * d
    p['gdn_w_qkvz'] = p['gdn_w_in'][:, :n_qkvz]
    p['gdn_w_ab'] = jnp.pad(p['gdn_w_in'][:, n_qkvz:], ((0, 0), (0, LANES - 2 * GDN_HEADS)))
    del p['gdn_w_in']

    loss_local, dx, dmods, dp = _trunk_grad(x, mods, p, target)
    dp['gdn_w_in'] = jnp.concatenate([dp.pop('gdn_w_qkvz'), dp.pop('gdn_w_ab')[:, :2 * GDN_HEADS]], axis=1)

    dmod_all = _exchange(dmods.reshape(-1, d), False, "gather_dmod").reshape(N_DEV, DEPTH, 6 * d)
    grads = {'ada_b': jnp.sum(dmod_all, axis=0)}
    dm_mine = lax.dynamic_slice_in_dim(dmod_all, me * msh, msh, axis=2)
    grads['ada_w'] = jnp.stack([_mm_outer(e_all, dm_mine[:, i], "ada_dw%d" % i) for i in range(DEPTH)])
    de_part = _mm(dm_mine[:, 0], ada_w[0], 'nt', "ada_de0")
    for i in range(1, DEPTH):
        de_part = de_part + _mm(dm_mine[:, i], ada_w[i], 'nt', "ada_de%d" % i)
    de_all = jnp.sum(_exchange(de_part, False, "gather_de"), axis=0)
    sig = jax.nn.sigmoid(pre)
    dpre = de_all * (sig * (1.0 + pre * (1.0 - sig)))
    grads['cond_b'] = jnp.sum(dpre, axis=0)
    grads['cond_w'] = _mm_outer(c_mine, dpre, "cond_dw")

    small_names = sm_names + SMALL_REPL
    small_packed = _pack_rows([loss_local.reshape(1)] + [dp[n] for n in small_names], F32)
    small_sum = _sum_slots(_exchange(small_packed, False, "gather_small_grads"), "sum_small_grads")
    small = _unpack_rows(small_sum, [(1,)] + [dp[n].shape for n in small_names])
    loss = small[0][0]
    for n, g in zip(small_names, small[1:]):
        if n in SMALL_SHARDED:
            ax = SMALL_SHARDED[n]
            g = lax.dynamic_slice_in_dim(g, me * w[n].shape[ax], w[n].shape[ax], axis=ax)
        grads[n] = g

    send = _pack_big_grads([dp[n] for n in big_names], [BIG[n] for n in big_names])
    shapes = [w[n].shape for n in big_names]
    outs = _adamw(_reduce_to_owner(send), *[_pack_rows([t[n] for n in big_names], F32, BIG_ROW_ALIGN) for t in (w, mom, var)],
                  "adamw_big")
    g_b, d_b, m_b, v_b = (_unpack_rows(o, shapes) for o in outs)
    delta, new_m, new_v = {}, {}, {}
    for j, n in enumerate(big_names):
        grads[n], delta[n], new_m[n], new_v[n] = g_b[j], d_b[j], m_b[j], v_b[j]

    rest = [n for n in WEIGHTS if n not in BIG]
    shapes = [w[n].shape for n in rest]
    outs = _adamw([(_pack_rows([grads[n] for n in rest], F32, BIG_ROW_ALIGN)[None], 0)],
                  *[_pack_rows([t[n] for n in rest], F32, BIG_ROW_ALIGN) for t in (w, mom, var)], "adamw_rest")
    _, d_r, m_r, v_r = (_unpack_rows(o, shapes) for o in outs)
    for j, n in enumerate(rest):
        delta[n], new_m[n], new_v[n] = d_r[j], m_r[j], v_r[j]

    return (loss, dx[None], *[grads[n] for n in WEIGHTS], *[delta[n] for n in WEIGHTS],
            *[new_m[n] for n in WEIGHTS], *[new_v[n] for n in WEIGHTS])
```

```python
import functools
import math

import jax
import jax.numpy as jnp
from jax import lax
from jax.experimental import pallas as pl
from jax.experimental.pallas import tpu as pltpu

F32 = jnp.float32
BF16 = jnp.bfloat16
MXU_DTYPE = jnp.bfloat16
MESH = pl.DeviceIdType.MESH
N_DEV = 8
LANES = 128
SUBLANES = 8
VMEM_LIMIT = 48 * 1024 * 1024

DEPTH = 4
LN_EPS = 1e-5
DN_ALPHA = (2.0 * DEPTH) ** 0.25
GDN_HEADS, GDN_CHUNK = 8, 64
RET_HEADS, RET_CHUNK, RET_ROPE_BASE = 4, 128, 10000.0
GMLP_CHUNK, GMLP_GROUPS = 128, 8
SB_HEADS = 16
ADAM_LR, ADAM_B1, ADAM_B2, ADAM_EPS, ADAM_WD, ADAM_STEP = 0.001, 0.9, 0.999, 1e-08, 0.01, 10

WEIGHTS = ['cond_w', 'cond_b', 'ada_w', 'ada_b', 'ln_g', 'ln_b', 'ffn_up', 'ffn_conv_w', 'ffn_conv_b', 'ffn_down',
           'gdn_w_in', 'gdn_conv_w', 'gdn_a_log', 'gdn_dt_bias', 'gdn_norm_w', 'gdn_w_out', 'ret_w_in', 'ret_w_out',
           'gmlp_w_in', 'gmlp_ln_g', 'gmlp_ln_b', 'gmlp_w_s', 'gmlp_b_s', 'gmlp_w_out', 'sb_w_in', 'sb_w_out']
BIG = {'ffn_up': 2, 'ffn_down': 1, 'gdn_w_in': 1, 'gdn_w_out': 0, 'ret_w_in': 1, 'ret_w_out': 0,
       'gmlp_w_in': 1, 'gmlp_w_out': 0, 'sb_w_in': 1, 'sb_w_out': 0}
SMALL_SHARDED = {'ln_g': 2, 'ln_b': 2, 'ffn_conv_w': 2, 'gdn_conv_w': 1}
SMALL_REPL = ['ffn_conv_b', 'gdn_a_log', 'gdn_dt_bias', 'gdn_norm_w', 'gmlp_ln_g', 'gmlp_ln_b', 'gmlp_w_s', 'gmlp_b_s']
MIXER_PARAMS = (('gdn_w_qkvz', 'gdn_w_ab', 'gdn_conv_w', 'gdn_a_log', 'gdn_dt_bias', 'gdn_norm_w', 'gdn_w_out'),
                ('ret_w_in', 'ret_w_out'),
                ('gmlp_w_in', 'gmlp_ln_g', 'gmlp_ln_b', 'gmlp_w_s', 'gmlp_b_s', 'gmlp_w_out'),
                ('sb_w_in', 'sb_w_out'))


def _pcall(body, **kw):
    return pl.pallas_call(body, **kw)


def _params(*semantics):
    return pltpu.CompilerParams(dimension_semantics=semantics, vmem_limit_bytes=VMEM_LIMIT)


def _my_id():
    return 4 * lax.axis_index("x") + 2 * lax.axis_index("y") + lax.axis_index("c")


def _pick(dim, prefs):
    for p in prefs:
        if dim % p == 0:
            return p
    return dim


def _exchange(src, scatter, name):
    blk = src.shape[1:] if scatter else src.shape
    out_shape = jax.ShapeDtypeStruct((N_DEV,) + tuple(blk), src.dtype)

    def body(src_ref, out_ref, send_sems, recv_sems, local_sem):
        x, y, c = lax.axis_index("x"), lax.axis_index("y"), lax.axis_index("c")
        me = 4 * x + 2 * y + c
        mine = pltpu.make_async_copy(src_ref.at[me] if scatter else src_ref, out_ref.at[me], local_sem)
        mine.start()
        copies = []
        for k in range(1, N_DEV):
            px = 1 - x if (k >> 2) & 1 else x
            py = 1 - y if (k >> 1) & 1 else y
            pc = 1 - c if k & 1 else c
            peer = 4 * px + 2 * py + pc
            cp = pltpu.make_async_remote_copy(
                src_ref=src_ref.at[peer] if scatter else src_ref,
                dst_ref=out_ref.at[me],
                send_sem=send_sems.at[k - 1], recv_sem=recv_sems.at[k - 1],
                device_id=(px, py, pc), device_id_type=MESH)
            cp.start()
            copies.append(cp)
        for cp in copies:
            cp.wait()
        mine.wait()

    return _pcall(
        body, name=name, out_shape=out_shape,
        in_specs=[pl.BlockSpec(memory_space=pl.ANY)],
        out_specs=pl.BlockSpec(memory_space=pl.ANY),
        scratch_shapes=[pltpu.SemaphoreType.DMA((N_DEV - 1,)), pltpu.SemaphoreType.DMA((N_DEV - 1,)),
                        pltpu.SemaphoreType.DMA(())],
    )(src)


def _flip(x, y, c, k):
    return (1 - x if (k >> 2) & 1 else x, 1 - y if (k >> 1) & 1 else y, 1 - c if k & 1 else c)


def _dev_id(p):
    return 4 * p[0] + 2 * p[1] + p[2]


OTHER_CHIPS = (4, 2, 6)


def _gather_two_level(src, name):
    out_shape = jax.ShapeDtypeStruct((N_DEV,) + tuple(src.shape), src.dtype)

    def body(x_ref, out_ref, send_sems, recv_sems, local_sem):
        x, y, c = lax.axis_index("x"), lax.axis_index("y"), lax.axis_index("c")
        me, sibling = (x, y, c), (x, y, 1 - c)
        chips = [_flip(x, y, c, k) for k in OTHER_CHIPS]

        def copy(k, block, to, from_src=False):
            slot = out_ref.at[_dev_id(block)]
            return pltpu.make_async_remote_copy(
                src_ref=x_ref if from_src else slot, dst_ref=slot,
                send_sem=send_sems.at[k], recv_sem=recv_sems.at[k], device_id=to, device_id_type=MESH)

        mine = pltpu.make_async_copy(x_ref, out_ref.at[_dev_id(me)], local_sem)
        mine.start()
        first = [copy(0, me, sibling, True)] + [copy(1 + j, me, chip, True) for j, chip in enumerate(chips)]
        for cp in first:
            cp.start()
        passed = [copy(4 + j, chip, sibling) for j, chip in enumerate(chips)]
        for j, chip in enumerate(chips):
            copy(1 + j, chip, me).wait_recv()
            passed[j].start()
        copy(0, sibling, me).wait_recv()
        for j, chip in enumerate(chips):
            copy(4 + j, (chip[0], chip[1], 1 - c), me).wait_recv()
        for cp in first + passed:
            cp.wait_send()
        mine.wait()

    return _pcall(
        body, name=name, out_shape=out_shape,
        in_specs=[pl.BlockSpec(memory_space=pl.ANY)],
        out_specs=pl.BlockSpec(memory_space=pl.ANY),
        scratch_shapes=[pltpu.SemaphoreType.DMA((N_DEV - 1,)), pltpu.SemaphoreType.DMA((N_DEV - 1,)),
                        pltpu.SemaphoreType.DMA(())],
    )(src)


def _send_slots(src, plan, n_out, name):
    out_shape = jax.ShapeDtypeStruct((n_out,) + tuple(src.shape[1:]), src.dtype)

    def body(src_ref, out_ref, send_sems, recv_sems):
        x, y, c = lax.axis_index("x"), lax.axis_index("y"), lax.axis_index("c")
        copies = []
        for e, (k, src_slot, dst_slot) in enumerate(plan):
            cp = pltpu.make_async_remote_copy(
                src_ref=src_ref.at[src_slot(x, y, c)], dst_ref=out_ref.at[dst_slot],
                send_sem=send_sems.at[e], recv_sem=recv_sems.at[e],
                device_id=_flip(x, y, c, k), device_id_type=MESH)
            cp.start()
            copies.append(cp)
        for cp in copies:
            cp.wait()

    return _pcall(
        body, name=name, out_shape=out_shape,
        in_specs=[pl.BlockSpec(memory_space=pl.ANY)],
        out_specs=pl.BlockSpec(memory_space=pl.ANY),
        scratch_shapes=[pltpu.SemaphoreType.DMA((len(plan),)), pltpu.SemaphoreType.DMA((len(plan),))],
    )(src)


def _reduce_to_owner(send):
    x, y, c = lax.axis_index("x"), lax.axis_index("y"), lax.axis_index("c")
    plan_a = [(1, lambda x, y, c: _dev_id((x, y, 1 - c)), 0)]
    plan_a += [(1, functools.partial(lambda k, x, y, c: _dev_id(_flip(x, y, c, k | 1)), k), 1 + j)
               for j, k in enumerate(OTHER_CHIPS)]
    from_sibling = _send_slots(send, plan_a, 1 + len(OTHER_CHIPS), "reduce_d2d")
    mine = jnp.stack([lax.dynamic_index_in_dim(send, _dev_id(_flip(x, y, c, k)), 0, keepdims=False)
                      for k in OTHER_CHIPS])
    rows = mine.shape[1]
    pair = _add_rows(mine.reshape(-1, LANES), from_sibling[1:].reshape(-1, LANES), BF16, "reduce_pair_sum")
    plan_c = [(k, functools.partial(lambda j, x, y, c: j, j), j) for j, k in enumerate(OTHER_CHIPS)]
    from_chips = _send_slots(pair.reshape(len(OTHER_CHIPS), rows, LANES), plan_c, len(OTHER_CHIPS), "reduce_ici")
    own = lax.dynamic_index_in_dim(send, _dev_id((x, y, c)), 0, keepdims=True)
    return [(own, 0), (from_sibling, 0)] + [(from_chips, j) for j in range(len(OTHER_CHIPS))]


ROW_ALIGN = 16
BIG_ROW_ALIGN = 512


def _pack_rows(parts, dtype, row_align=ROW_ALIGN):
    flat = jnp.concatenate([p.reshape(-1).astype(dtype) for p in parts])
    n = flat.shape[0]
    pad = (-n) % (row_align * LANES)
    if pad:
        flat = jnp.concatenate([flat, jnp.zeros((pad,), dtype)])
    return flat.reshape(-1, LANES)


def _unpack_rows(packed, shapes):
    lead = packed.shape[:-2]
    flat = packed.reshape(lead + (-1,))
    out, off = [], 0
    for s in shapes:
        n = math.prod(s)
        out.append(flat[..., off:off + n].reshape(lead + tuple(s)))
        off += n
    return out


def _mm(a, b, dims, name, exact=False):
    if dims == 'nn':
        (m, k), n = a.shape, b.shape[1]
    elif dims == 'nt':
        (m, k), n = a.shape, b.shape[0]
    else:
        (k, m), n = a.shape, b.shape[1]
    tm = _pick(m, (1408, 1024, 512, 256, 128))
    tn = _pick(n, (512, 256, 128))
    tk = k if k <= 2816 else _pick(k, (2816, 2048, 1536, 1024, 512, 256, 128))
    nk = k // tk
    if dims == 'nn':
        a_spec = pl.BlockSpec((tm, tk), lambda i, j, kk: (i, kk))
        b_spec = pl.BlockSpec((tk, tn), lambda i, j, kk: (kk, j))
        dn = (((1,), (0,)), ((), ()))
    elif dims == 'nt':
        a_spec = pl.BlockSpec((tm, tk), lambda i, j, kk: (i, kk))
        b_spec = pl.BlockSpec((tn, tk), lambda i, j, kk: (j, kk))
        dn = (((1,), (1,)), ((), ()))
    else:
        a_spec = pl.BlockSpec((tk, tm), lambda i, j, kk: (kk, i))
        b_spec = pl.BlockSpec((tk, tn), lambda i, j, kk: (kk, j))
        dn = (((0,), (0,)), ((), ()))

    def product(a_ref, b_ref):
        if exact:
            return lax.dot_general(a_ref[...], b_ref[...], dn, precision=lax.Precision.HIGHEST,
                                   preferred_element_type=F32)
        return lax.dot_general(a_ref[...].astype(MXU_DTYPE), b_ref[...].astype(MXU_DTYPE), dn,
                               preferred_element_type=F32)

    def body(a_ref, b_ref, o_ref, *acc):
        if nk == 1:
            o_ref[...] = product(a_ref, b_ref)
            return
        acc_ref, = acc
        kk = pl.program_id(2)

        @pl.when(kk == 0)
        def _():
            acc_ref[...] = jnp.zeros_like(acc_ref)

        acc_ref[...] += product(a_ref, b_ref)

        @pl.when(kk == nk - 1)
        def _():
            o_ref[...] = acc_ref[...]

    return _pcall(
        body, name=name, out_shape=jax.ShapeDtypeStruct((m, n), F32),
        grid=(m // tm, n // tn, nk),
        in_specs=[a_spec, b_spec],
        out_specs=pl.BlockSpec((tm, tn), lambda i, j, kk: (i, j)),
        scratch_shapes=[pltpu.VMEM((tm, tn), F32)] if nk > 1 else [],
        compiler_params=_params("parallel", "parallel", "arbitrary"),
    )(a, b)


def _mm_outer(a, b, name):
    pad = LANES - a.shape[0]
    return _mm(jnp.pad(a.T, ((0, 0), (0, pad))), jnp.pad(b, ((0, pad), (0, 0))), 'nn', name, exact=True)


@functools.partial(jax.custom_vjp, nondiff_argnums=(2,))
def _linear(a, w, name):
    return _mm(a, w, 'nn', name + "_fwd")


def _linear_fwd(a, w, name):
    return _mm(a, w, 'nn', name + "_fwd"), (a, w)


def _linear_bwd(name, res, dy):
    a, w = res
    return _mm(dy, w, 'nt', name + "_dx"), _mm(a, dy, 'tn', name + "_dw")


_linear.defvjp(_linear_fwd, _linear_bwd)


SB_BK = 256
SB_STRIP = 16


def _sb_tiles(s):
    tq = _pick(s, (512, 256, 128))
    bk = min(SB_BK, tq)
    return tq, bk, tq // bk


def _sb_valid(t, sr, bk, q0, k0):
    row = lax.broadcasted_iota(jnp.int32, (sr, bk), 0) + (q0 + t * sr)
    col = lax.broadcasted_iota(jnp.int32, (sr, bk), 1) + k0
    return col < row


def _sb_tri(bk, inclusive):
    r = jnp.bitwise_and(lax.broadcasted_iota(jnp.int32, (2 * bk, bk), 0), bk - 1)
    c = lax.broadcasted_iota(jnp.int32, (2 * bk, bk), 1)
    return (r >= c).astype(BF16) if inclusive else (r > c).astype(BF16)


def _sb_split(ref, n, rows, bk, val):
    hi = val.astype(BF16)
    ref[n, rows, 0:bk] = hi
    ref[n, rows, bk:2 * bk] = (val - hi.astype(F32)).astype(BF16)


LOG2E = 1.0 / math.log(2.0)


def _sb_logits_phase(z_ref, ls_ref, hl_ref, l0_ref, n, tq, bk, sr, q0, k0, masked):
    for t in range(tq // sr):
        rows = slice(t * sr, (t + 1) * sr)
        z = z_ref[n, rows, :] * LOG2E
        ls = jnp.minimum(z, 0.0) - jnp.log(1.0 + jnp.exp2(-jnp.abs(z))) * LOG2E
        lm = ls - z
        if masked:
            lm = jnp.where(_sb_valid(t, sr, bk, q0, k0), lm, 0.0)
        ls_ref[n, rows, :] = ls
        _sb_split(hl_ref, n, rows, bk, lm)
        l0_ref[n, rows, :] = lm[:, 0:1]


def _sb_fwd_call(q, kt, v):
    h, s, dh = q.shape
    tq, bk, nt = _sb_tiles(s)
    sr = SB_STRIP

    def body(q_ref, kt_ref, v_ref, o_ref, z_ref, ls_ref, hl_ref, f_ref, a_ref, l0_ref, cl_ref, acc_ref):
        i = pl.program_id(1)
        q0 = i * tq
        cl_ref[...] = jnp.zeros_like(cl_ref)
        acc_ref[...] = jnp.zeros_like(acc_ref)
        u_excl = _sb_tri(bk, False)

        def iteration(kb0, masked):
            k0s = [pl.multiple_of(kb0 + (nt - 1 - n) * bk, bk) for n in range(nt)]
            for n in range(nt):
                z_ref[n] = jnp.dot(q_ref[...], kt_ref[:, pl.ds(k0s[n], bk)], preferred_element_type=F32)
            for n in range(nt):
                _sb_logits_phase(z_ref, ls_ref, hl_ref, l0_ref, n, tq, bk, sr, q0, k0s[n], masked)
            for n in range(nt):
                f_ref[n] = jnp.dot(hl_ref[n], u_excl, preferred_element_type=F32)
            for t in range(tq // sr):
                rows = slice(t * sr, (t + 1) * sr)
                c = cl_ref[rows, :]
                for n in range(nt):
                    f = f_ref[n, rows, :]
                    a = jnp.exp2(ls_ref[n, rows, :] + f + c)
                    if masked:
                        a = jnp.where(_sb_valid(t, sr, bk, q0, k0s[n]), a, 0.0)
                    a_ref[n, rows, :] = a.astype(a_ref.dtype)
                    c = c + f[:, 0:1] + l0_ref[n, rows, :]
                cl_ref[rows, :] = c
            for n in range(nt):
                acc_ref[...] += jnp.dot(a_ref[n], v_ref[pl.ds(k0s[n], bk), :], preferred_element_type=F32)

        def below(jj, c):
            iteration((i - 1 - jj) * tq, False)
            return c

        iteration(q0, True)
        lax.fori_loop(0, i, below, 0)
        o_ref[...] = acc_ref[...]

    return _pcall(
        body, name="sb_fwd", out_shape=jax.ShapeDtypeStruct((h, s, dh), F32),
        grid=(h, s // tq),
        in_specs=[pl.BlockSpec((None, tq, dh), lambda hh, i: (hh, i, 0)),
                  pl.BlockSpec((None, dh, s), lambda hh, i: (hh, 0, 0)),
                  pl.BlockSpec((None, s, dh), lambda hh, i: (hh, 0, 0))],
        out_specs=pl.BlockSpec((None, tq, dh), lambda hh, i: (hh, i, 0)),
        scratch_shapes=[pltpu.VMEM((nt, tq, bk), F32), pltpu.VMEM((nt, tq, bk), F32),
                        pltpu.VMEM((nt, tq, 2 * bk), BF16), pltpu.VMEM((nt, tq, bk), F32),
                        pltpu.VMEM((nt, tq, bk), q.dtype), pltpu.VMEM((nt, tq, 1), F32),
                        pltpu.VMEM((tq, 1), F32), pltpu.VMEM((tq, dh), F32)],
        compiler_params=_params("parallel", "arbitrary"),
    )(q, kt, v)


def _sb_bwd_call(q, qt, k, kt, vt, o, do, dot):
    h, s, dh = q.shape
    tq, bk, nt = _sb_tiles(s)
    sr = SB_STRIP

    def body(q_ref, qt_ref, k_ref, kt_ref, vt_ref, o_ref, do_ref, dot_ref, dq_ref, dkt_ref, dvt_ref,
             z_ref, ls_ref, hl_ref, f_ref, a_ref, g_ref, dz_ref, da_ref, l0_ref, dob_ref,
             cl_ref, cg_ref, dl_ref, dqa_ref):
        i = pl.program_id(1)
        q0 = i * tq

        @pl.when(i == 0)
        def _():
            dkt_ref[...] = jnp.zeros_like(dkt_ref)
            dvt_ref[...] = jnp.zeros_like(dvt_ref)

        cl_ref[...] = jnp.zeros_like(cl_ref)
        cg_ref[...] = jnp.zeros_like(cg_ref)
        dqa_ref[...] = jnp.zeros_like(dqa_ref)
        dob = do_ref[...].astype(dob_ref.dtype)
        dob_ref[...] = dob
        dl_ref[...] = jnp.sum(dob.astype(F32) * o_ref[...], axis=1, keepdims=True)
        u_excl = _sb_tri(bk, False)
        u_incl = _sb_tri(bk, True)

        def iteration(kb0, masked):
            k0s = [pl.multiple_of(kb0 + (nt - 1 - n) * bk, bk) for n in range(nt)]
            for n in range(nt):
                z_ref[n] = jnp.dot(q_ref[...], kt_ref[:, pl.ds(k0s[n], bk)], preferred_element_type=F32)
                da_ref[n] = jnp.dot(dob_ref[...], vt_ref[:, pl.ds(k0s[n], bk)], preferred_element_type=F32)
            for n in range(nt):
                _sb_logits_phase(z_ref, ls_ref, hl_ref, l0_ref, n, tq, bk, sr, q0, k0s[n], masked)
            for n in range(nt):
                f_ref[n] = jnp.dot(hl_ref[n], u_excl, preferred_element_type=F32)
            for t in range(tq // sr):
                rows = slice(t * sr, (t + 1) * sr)
                c = cl_ref[rows, :]
                for n in range(nt):
                    f = f_ref[n, rows, :]
                    a = jnp.exp2(ls_ref[n, rows, :] + f + c)
                    if masked:
                        a = jnp.where(_sb_valid(t, sr, bk, q0, k0s[n]), a, 0.0)
                    ab = a.astype(a_ref.dtype)
                    a_ref[n, rows, :] = ab
                    g = da_ref[n, rows, :] * ab.astype(F32)
                    g_ref[n, rows, :] = g
                    _sb_split(hl_ref, n, rows, bk, g)
                    c = c + f[:, 0:1] + l0_ref[n, rows, :]
                cl_ref[rows, :] = c
            for n in range(nt):
                f_ref[n] = jnp.dot(hl_ref[n], u_incl, preferred_element_type=F32)
            for t in range(tq // sr):
                rows = slice(t * sr, (t + 1) * sr)
                cg = cg_ref[rows, :]
                for n in range(nt):
                    sg_tile = f_ref[n, rows, :]
                    p = dl_ref[rows, :] - (sg_tile + cg)
                    g = g_ref[n, rows, :]
                    dz = g - (g + p) * jnp.exp2(ls_ref[n, rows, :])
                    if masked:
                        dz = jnp.where(_sb_valid(t, sr, bk, q0, k0s[n]), dz, 0.0)
                    dz_ref[n, rows, :] = dz.astype(dz_ref.dtype)
                    cg = cg + sg_tile[:, 0:1]
                cg_ref[rows, :] = cg
            for n in range(nt):
                cols = pl.ds(k0s[n], bk)
                dqa_ref[...] += jnp.dot(dz_ref[n], k_ref[cols, :], preferred_element_type=F32)
                dkt_ref[:, cols] += jnp.dot(qt_ref[...], dz_ref[n], preferred_element_type=F32)
                dvt_ref[:, cols] += jnp.dot(dot_ref[...], a_ref[n], preferred_element_type=F32)

        def below(jj, c):
            iteration((i - 1 - jj) * tq, False)
            return c

        iteration(q0, True)
        lax.fori_loop(0, i, below, 0)
        dq_ref[...] = dqa_ref[...]

    blk_q = pl.BlockSpec((None, tq, dh), lambda hh, i: (hh, i, 0))
    blk_qt = pl.BlockSpec((None, dh, tq), lambda hh, i: (hh, 0, i))
    blk_s = pl.BlockSpec((None, s, dh), lambda hh, i: (hh, 0, 0))
    blk_st = pl.BlockSpec((None, dh, s), lambda hh, i: (hh, 0, 0))
    mx = q.dtype
    return _pcall(
        body, name="sb_bwd",
        out_shape=(jax.ShapeDtypeStruct((h, s, dh), F32), jax.ShapeDtypeStruct((h, dh, s), F32),
                   jax.ShapeDtypeStruct((h, dh, s), F32)),
        grid=(h, s // tq),
        in_specs=[blk_q, blk_qt, blk_s, blk_st, blk_st, blk_q, blk_q, blk_qt],
        out_specs=(blk_q, blk_st, blk_st),
        scratch_shapes=[pltpu.VMEM((nt, tq, bk), F32), pltpu.VMEM((nt, tq, bk), F32),
                        pltpu.VMEM((nt, tq, 2 * bk), BF16), pltpu.VMEM((nt, tq, bk), F32),
                        pltpu.VMEM((nt, tq, bk), mx), pltpu.VMEM((nt, tq, bk), F32),
                        pltpu.VMEM((nt, tq, bk), mx), pltpu.VMEM((nt, tq, bk), F32),
                        pltpu.VMEM((nt, tq, 1), F32), pltpu.VMEM((tq, dh), mx),
                        pltpu.VMEM((tq, 1), F32), pltpu.VMEM((tq, 1), F32), pltpu.VMEM((tq, 1), F32),
                        pltpu.VMEM((tq, dh), F32)],
        compiler_params=_params("parallel", "arbitrary"),
    )(q, qt, k, kt, vt, o, do, dot)


def _swap(t):
    return t.transpose(0, 2, 1)


def _sb_scale(dh):
    assert math.log2(dh) % 2 == 0, dh
    return dh ** -0.5


@jax.custom_vjp
def _sb_core(q, k, v):
    return _sb_core_fwd(q, k, v)[0]


def _sb_core_fwd(q, k, v):
    scale = _sb_scale(q.shape[-1])
    qs, kb, vb = (q.astype(MXU_DTYPE) * scale).astype(MXU_DTYPE), k.astype(MXU_DTYPE), v.astype(MXU_DTYPE)
    o = _sb_fwd_call(qs, _swap(kb), vb)
    return o, (qs, kb, vb, o)


def _sb_core_bwd(res, do):
    qs, kb, vb, o = res
    ks = (kb * _sb_scale(kb.shape[-1])).astype(MXU_DTYPE)
    dq, dkt, dvt = _sb_bwd_call(qs, _swap(qs), ks, _swap(kb), _swap(vb), o, do, _swap(do.astype(MXU_DTYPE)))
    return dq, _swap(dkt), _swap(dvt)


_sb_core.defvjp(_sb_core_fwd, _sb_core_bwd)


def _row_block(s):
    return _pick(s, (512, 256, 128, 64, 32, 16, 8))


def _fold8(t):
    r, c = t.shape
    return jnp.sum(t.reshape(r // SUBLANES, SUBLANES, c), axis=0)


def _vec(a):
    return a.reshape(1, -1)


def _modulate(x, sc, sh, out_dtype, name):
    s, d = x.shape
    tr = _row_block(s)

    def body(x_ref, sc_ref, sh_ref, o_ref):
        o_ref[...] = (x_ref[...] * (1.0 + sc_ref[...]) + sh_ref[...]).astype(o_ref.dtype)

    row = pl.BlockSpec((tr, d), lambda i: (i, 0))
    vec = pl.BlockSpec((1, d), lambda i: (0, 0))
    return _pcall(body, name=name, out_shape=jax.ShapeDtypeStruct((s, d), out_dtype), grid=(s // tr,),
                  in_specs=[row, vec, vec], out_specs=row, compiler_params=_params("parallel"))(x, _vec(sc), _vec(sh))


def _modulate_bwd(dxa, dh, x, sc, name):
    s, d = x.shape
    tr = _row_block(s)

    def body(dxa_ref, dh_ref, x_ref, sc_ref, dx_ref, acc_ref):
        @pl.when(pl.program_id(0) == 0)
        def _():
            acc_ref[...] = jnp.zeros_like(acc_ref)

        dh = dh_ref[...]
        dx_ref[...] = dxa_ref[...] + dh * (1.0 + sc_ref[...])
        acc_ref[0] += _fold8(dh * x_ref[...])
        acc_ref[1] += _fold8(dh)

    row = pl.BlockSpec((tr, d), lambda i: (i, 0))
    vec = pl.BlockSpec((1, d), lambda i: (0, 0))
    dx, acc = _pcall(
        body, name=name,
        out_shape=(jax.ShapeDtypeStruct((s, d), F32), jax.ShapeDtypeStruct((2, SUBLANES, d), F32)),
        grid=(s // tr,), in_specs=[row, row, row, vec],
        out_specs=(row, pl.BlockSpec((2, SUBLANES, d), lambda i: (0, 0, 0))),
        compiler_params=_params("arbitrary"))(dxa, dh, x, _vec(sc))
    acc = jnp.sum(acc, axis=1)
    return dx, acc[0], acc[1]


def _resid_ln(x, y, g, gamma, beta, name):
    s, d = x.shape
    tr = _row_block(s)

    def body(x_ref, y_ref, g_ref, gam_ref, bet_ref, o_ref):
        u = DN_ALPHA * x_ref[...] + (1.0 + g_ref[...]) * y_ref[...]
        uc = u - jnp.mean(u, axis=-1, keepdims=True)
        var = jnp.mean(uc * uc, axis=-1, keepdims=True)
        o_ref[...] = uc * lax.rsqrt(var + LN_EPS) * gam_ref[...] + bet_ref[...]

    row = pl.BlockSpec((tr, d), lambda i: (i, 0))
    vec = pl.BlockSpec((1, d), lambda i: (0, 0))
    return _pcall(body, name=name, out_shape=jax.ShapeDtypeStruct((s, d), F32), grid=(s // tr,),
                  in_specs=[row, row, vec, vec, vec], out_specs=row,
                  compiler_params=_params("parallel"))(x, y, _vec(g), _vec(gamma), _vec(beta))


def _resid_ln_bwd(x, y, g, gamma, dout, name):
    s, d = x.shape
    tr = _row_block(s)

    def body(x_ref, y_ref, g_ref, gam_ref, do_ref, dxa_ref, dy_ref, acc_ref):
        @pl.when(pl.program_id(0) == 0)
        def _():
            acc_ref[...] = jnp.zeros_like(acc_ref)

        y = y_ref[...]
        gg = 1.0 + g_ref[...]
        u = DN_ALPHA * x_ref[...] + gg * y
        uc = u - jnp.mean(u, axis=-1, keepdims=True)
        rstd = lax.rsqrt(jnp.mean(uc * uc, axis=-1, keepdims=True) + LN_EPS)
        xhat = uc * rstd
        dout = do_ref[...]
        dxh = dout * gam_ref[...]
        du = rstd * (dxh - jnp.mean(dxh, axis=-1, keepdims=True)
                     - xhat * jnp.mean(dxh * xhat, axis=-1, keepdims=True))
        dxa_ref[...] = DN_ALPHA * du
        dy_ref[...] = gg * du
        acc_ref[0] += _fold8(dout * xhat)
        acc_ref[1] += _fold8(dout)
        acc_ref[2] += _fold8(du * y)

    row = pl.BlockSpec((tr, d), lambda i: (i, 0))
    vec = pl.BlockSpec((1, d), lambda i: (0, 0))
    dxa, dy, acc = _pcall(
        body, name=name,
        out_shape=(jax.ShapeDtypeStruct((s, d), F32), jax.ShapeDtypeStruct((s, d), F32),
                   jax.ShapeDtypeStruct((3, SUBLANES, d), F32)),
        grid=(s // tr,), in_specs=[row, row, vec, vec, row],
        out_specs=(row, row, pl.BlockSpec((3, SUBLANES, d), lambda i: (0, 0, 0))),
        compiler_params=_params("arbitrary"))(x, y, _vec(g), _vec(gamma), dout)
    acc = jnp.sum(acc, axis=1)
    return dxa, dy, acc[0], acc[1], acc[2]


def _loss_head(x, target, name):
    s, d = x.shape
    tr = _row_block(s)

    def body(x_ref, t_ref, dx_ref, acc_ref):
        @pl.when(pl.program_id(0) == 0)
        def _():
            acc_ref[...] = jnp.zeros_like(acc_ref)

        e = x_ref[...] - t_ref[...]
        dx_ref[...] = e * (1.0 / d)
        acc_ref[...] += _fold8(e * e)

    row = pl.BlockSpec((tr, d), lambda i: (i, 0))
    dx, acc = _pcall(
        body, name=name,
        out_shape=(jax.ShapeDtypeStruct((s, d), F32), jax.ShapeDtypeStruct((SUBLANES, d), F32)),
        grid=(s // tr,), in_specs=[row, row],
        out_specs=(row, pl.BlockSpec((SUBLANES, d), lambda i: (0, 0))),
        compiler_params=_params("arbitrary"))(x, target)
    return (0.5 / d) * jnp.sum(acc), dx


CONV_STRIPE = 128
CONV_ROWS = 256


def _shift_down(cur, halo, k):
    ext = jnp.concatenate([halo, cur], axis=0)
    return pltpu.roll(ext, k, 0)[SUBLANES:]


def _shift_up(cur, halo, k):
    ext = jnp.concatenate([cur, halo], axis=0)
    n = ext.shape[0]
    return pltpu.roll(ext, n - k, 0)[:n - SUBLANES]


def _gate_chunk(g_ref, r, rc):
    r0 = pl.multiple_of(r * rc, rc)
    cur = g_ref[pl.ds(r0, rc), :]
    hs = pl.multiple_of(jnp.maximum(r0 - SUBLANES, 0), SUBLANES)
    halo = jnp.where(r > 0, g_ref[pl.ds(hs, SUBLANES), :], 0.0)
    return r0, cur, _shift_down(cur, halo, 1), _shift_down(cur, halo, 2)


def _ffn_gate(gu, cw, cb, name):
    s, f2 = gu.shape
    f = f2 // 2
    tc = _pick(f, (CONV_STRIPE,))
    rc = _pick(s, (CONV_ROWS, 128, 64, 32, 16, 8))
    nj = f // tc

    def body(g_ref, u_ref, cw_ref, cb_ref, a_ref):
        w0, w1, w2, b = cw_ref[0:1, :], cw_ref[1:2, :], cw_ref[2:3, :], cb_ref[...]

        def chunk(r, c):
            r0, cur, x1, x2 = _gate_chunk(g_ref, r, rc)
            gc = w2 * cur + w1 * x1 + w0 * x2 + b
            a_ref[pl.ds(r0, rc), :] = (gc * jax.nn.sigmoid(gc) * u_ref[pl.ds(r0, rc), :]).astype(a_ref.dtype)
            return c

        lax.fori_loop(0, s // rc, chunk, 0)

    return _pcall(
        body, name=name, out_shape=jax.ShapeDtypeStruct((s, f), MXU_DTYPE), grid=(nj,),
        in_specs=[pl.BlockSpec((s, tc), lambda j: (0, j)), pl.BlockSpec((s, tc), lambda j: (0, j + nj)),
                  pl.BlockSpec((3, tc), lambda j: (0, j)), pl.BlockSpec((1, tc), lambda j: (0, j))],
        out_specs=pl.BlockSpec((s, tc), lambda j: (0, j)),
        compiler_params=_params("parallel"))(gu, gu, cw, _vec(cb))


def _ffn_gate_bwd(da, gu, cw, cb, name):
    s, f2 = gu.shape
    f = f2 // 2
    tc = _pick(f, (CONV_STRIPE,))
    rc = _pick(s, (CONV_ROWS, 128, 64, 32, 16, 8))
    nj = f // tc
    nr = s // rc

    def body(da_ref, g_ref, u_ref, cw_ref, cb_ref, dg_ref, du_ref, acc_ref, dgc_ref):
        w0, w1, w2, b = cw_ref[0:1, :], cw_ref[1:2, :], cw_ref[2:3, :], cb_ref[...]

        def chunk1(r, carry):
            a0, a1, a2, ab = carry
            r0, cur, x1, x2 = _gate_chunk(g_ref, r, rc)
            gc = w2 * cur + w1 * x1 + w0 * x2 + b
            sg = jax.nn.sigmoid(gc)
            da_c = da_ref[pl.ds(r0, rc), :]
            du_ref[pl.ds(r0, rc), :] = (da_c * (gc * sg)).astype(du_ref.dtype)
            dgc = da_c * u_ref[pl.ds(r0, rc), :] * (sg * (1.0 + gc * (1.0 - sg)))
            dgc_ref[pl.ds(r0, rc), :] = dgc
            return a0 + _fold8(dgc * x2), a1 + _fold8(dgc * x1), a2 + _fold8(dgc * cur), ab + _fold8(dgc)

        zero = jnp.zeros((SUBLANES, tc), F32)
        a0, a1, a2, ab = lax.fori_loop(0, nr, chunk1, (zero, zero, zero, zero))
        acc_ref[0], acc_ref[1], acc_ref[2], acc_ref[3] = a0, a1, a2, ab

        def chunk2(r, c):
            r0 = pl.multiple_of(r * rc, rc)
            cur = dgc_ref[pl.ds(r0, rc), :]
            hs = pl.multiple_of(jnp.minimum(r0 + rc, s - SUBLANES), SUBLANES)
            halo = jnp.where(r < nr - 1, dgc_ref[pl.ds(hs, SUBLANES), :], 0.0)
            dg = w2 * cur + w1 * _shift_up(cur, halo, 1) + w0 * _shift_up(cur, halo, 2)
            dg_ref[pl.ds(r0, rc), :] = dg.astype(dg_ref.dtype)
            return c

        lax.fori_loop(0, nr, chunk2, 0)

    stripe = pl.BlockSpec((s, tc), lambda j: (0, j))
    dg, du, acc = _pcall(
        body, name=name,
        out_shape=(jax.ShapeDtypeStruct((s, f), MXU_DTYPE), jax.ShapeDtypeStruct((s, f), MXU_DTYPE),
                   jax.ShapeDtypeStruct((4, SUBLANES, f), F32)),
        grid=(nj,),
        in_specs=[stripe, stripe, pl.BlockSpec((s, tc), lambda j: (0, j + nj)),
                  pl.BlockSpec((3, tc), lambda j: (0, j)), pl.BlockSpec((1, tc), lambda j: (0, j))],
        out_specs=(stripe, stripe, pl.BlockSpec((4, SUBLANES, tc), lambda j: (0, 0, j))),
        scratch_shapes=[pltpu.VMEM((s, tc), F32)],
        compiler_params=_params("parallel"))(da, gu, gu, cw, _vec(cb))
    acc = jnp.sum(acc, axis=1)
    return dg, du, acc[:3], acc[3]


def _add_rows(a, b, out_dtype, name):
    r = a.shape[0]
    tr = _pick(r, (1024, 512, 256, 128, 64, 32, 16, 8))

    def body(a_ref, b_ref, o_ref):
        o_ref[...] = (a_ref[...] + b_ref[...]).astype(o_ref.dtype)

    row = pl.BlockSpec((tr, LANES), lambda i: (i, 0))
    return _pcall(body, name=name, out_shape=jax.ShapeDtypeStruct(a.shape, out_dtype), grid=(r // tr,),
                  in_specs=[row, row], out_specs=row, compiler_params=_params("parallel"))(a, b)


def _adamw(gparts, w, m, v, name):
    r = w.shape[0]
    tr = _pick(r, (1024, 512, 256, 128, 64, 32, 16, 8))
    bc1 = 1.0 / (1.0 - ADAM_B1 ** ADAM_STEP)
    bc2 = 1.0 / (1.0 - ADAM_B2 ** ADAM_STEP)
    n = len(gparts)

    def body(*refs):
        w_ref, m_ref, v_ref, go_ref, d_ref, mo_ref, vo_ref = refs[n:]
        g = refs[0][...].astype(F32)
        for t in range(1, n):
            g = g + refs[t][...].astype(F32)
        mn = ADAM_B1 * m_ref[...] + (1.0 - ADAM_B1) * g
        vn = ADAM_B2 * v_ref[...] + (1.0 - ADAM_B2) * (g * g)
        m_hat = mn * bc1
        v_hat = vn * bc2
        go_ref[...] = g
        d_ref[...] = -ADAM_LR * (m_hat / (jnp.sqrt(v_hat) + ADAM_EPS) + ADAM_WD * w_ref[...])
        mo_ref[...] = mn
        vo_ref[...] = vn

    row = pl.BlockSpec((tr, LANES), lambda i: (i, 0))
    sds = jax.ShapeDtypeStruct((r, LANES), F32)
    return _pcall(
        body, name=name, out_shape=(sds, sds, sds, sds),
        grid=(r // tr,),
        in_specs=[pl.BlockSpec((None, tr, LANES), functools.partial(lambda slot, i: (slot, i, 0), slot))
                  for _, slot in gparts] + [row, row, row],
        out_specs=(row, row, row, row),
        compiler_params=_params("parallel"),
    )(*[a for a, _ in gparts], w, m, v)


def _sum_slots(gslots, name):
    n, r, _ = gslots.shape
    tr = _pick(r, (1024, 512, 256, 128, 64, 32, 16, 8))

    def body(g_ref, o_ref):
        g = g_ref[0]
        for t in range(1, n):
            g = g + g_ref[t]
        o_ref[...] = g

    return _pcall(
        body, name=name, out_shape=jax.ShapeDtypeStruct((r, LANES), F32),
        grid=(r // tr,),
        in_specs=[pl.BlockSpec((n, tr, LANES), lambda i: (0, i, 0))],
        out_specs=pl.BlockSpec((tr, LANES), lambda i: (i, 0)),
        compiler_params=_params("parallel"),
    )(gslots)


def _l2norm(x, eps=1e-6):
    return x * lax.rsqrt(jnp.sum(x * x, axis=-1, keepdims=True) + eps)


def _chunk_heads(t, n_heads, chunk):
    s, hd = t.shape
    return t.reshape(s // chunk, chunk, n_heads, hd // n_heads).transpose(2, 0, 1, 3)


def _unchunk_heads(t):
    h, n, c, d = t.shape
    return t.transpose(1, 2, 0, 3).reshape(n * c, h, d)


_NT = (((1,), (1,)), ((), ()))
_TN = (((0,), (0,)), ((), ()))


def _gdn_blocks(a, rev_from=None):
    h, _, r, c = a.shape
    if rev_from is None:
        return pl.BlockSpec((h, None, r, c), lambda n: (0, n, 0, 0))
    return pl.BlockSpec((h, None, r, c), lambda n: (0, rev_from - n, 0, 0))


def _gdn_scan_fwd_call(qg, w, u, qk, kd, e):
    H, n_chunks, c, dk = qg.shape
    dv = u.shape[-1]

    def body(qg_ref, w_ref, u_ref, qk_ref, kd_ref, e_ref, o_ref, sin_ref, vn_ref, state_ref):
        @pl.when(pl.program_id(0) == 0)
        def _():
            state_ref[...] = jnp.zeros_like(state_ref)

        for hd in range(H):
            st = state_ref[hd]
            stb = st.astype(MXU_DTYPE)
            sin_ref[hd] = st
            v_new = u_ref[hd] - jnp.dot(w_ref[hd].astype(MXU_DTYPE), stb, preferred_element_type=F32)
            vn_ref[hd] = v_new
            vnb = v_new.astype(MXU_DTYPE)
            o_ref[hd] = (jnp.dot(qg_ref[hd].astype(MXU_DTYPE), stb, preferred_element_type=F32)
                         + jnp.dot(qk_ref[hd].astype(MXU_DTYPE), vnb, preferred_element_type=F32))
            state_ref[hd] = st * e_ref[hd] + lax.dot_general(kd_ref[hd].astype(MXU_DTYPE), vnb, _TN,
                                                             preferred_element_type=F32)

    ins = (qg, w, u, qk, kd, e)
    outs = (jax.ShapeDtypeStruct((H, n_chunks, c, dv), F32), jax.ShapeDtypeStruct((H, n_chunks, dk, dv), F32),
            jax.ShapeDtypeStruct((H, n_chunks, c, dv), F32))
    return _pcall(
        body, name="gdn_scan_fwd", out_shape=outs, grid=(n_chunks,),
        in_specs=[_gdn_blocks(a) for a in ins], out_specs=tuple(_gdn_blocks(a) for a in outs),
        scratch_shapes=[pltpu.VMEM((H, dk, dv), F32)],
        compiler_params=_params("arbitrary"),
    )(*ins)


def _gdn_scan_bwd_call(qg, w, qk, kd, e, s_in, v_new, do):
    H, n_chunks, c, dk = qg.shape
    dv = v_new.shape[-1]

    def body(qg_ref, w_ref, qk_ref, kd_ref, e_ref, sin_ref, vn_ref, do_ref,
             dqg_ref, dw_ref, du_ref, dqk_ref, dkd_ref, de_ref, ds_ref):
        @pl.when(pl.program_id(0) == 0)
        def _():
            ds_ref[...] = jnp.zeros_like(ds_ref)

        for hd in range(H):
            st = sin_ref[hd]
            stb = st.astype(MXU_DTYPE)
            vnb = vn_ref[hd].astype(MXU_DTYPE)
            dob = do_ref[hd].astype(MXU_DTYPE)
            ds = ds_ref[hd]
            dsb = ds.astype(MXU_DTYPE)
            dvn = lax.dot_general(qk_ref[hd].astype(MXU_DTYPE), dob, _TN, preferred_element_type=F32)
            dqk_ref[hd] = lax.dot_general(dob, vnb, _NT, preferred_element_type=F32)
            dqg_ref[hd] = lax.dot_general(dob, stb, _NT, preferred_element_type=F32)
            ds_in = lax.dot_general(qg_ref[hd].astype(MXU_DTYPE), dob, _TN, preferred_element_type=F32)
            dvn = dvn + jnp.dot(kd_ref[hd].astype(MXU_DTYPE), dsb, preferred_element_type=F32)
            dkd_ref[hd] = lax.dot_general(vnb, dsb, _NT, preferred_element_type=F32)
            de_ref[hd] = _fold8(st * ds)
            ds_in = ds_in + ds * e_ref[hd]
            du_ref[hd] = dvn
            dvnb = dvn.astype(MXU_DTYPE)
            dw_ref[hd] = -lax.dot_general(dvnb, stb, _NT, preferred_element_type=F32)
            ds_ref[hd] = ds_in - lax.dot_general(w_ref[hd].astype(MXU_DTYPE), dvnb, _TN,
                                                 preferred_element_type=F32)

    last = n_chunks - 1
    ins = (qg, w, qk, kd, e, s_in, v_new, do)
    outs = (jax.ShapeDtypeStruct(qg.shape, F32), jax.ShapeDtypeStruct(w.shape, F32),
            jax.ShapeDtypeStruct(v_new.shape, F32), jax.ShapeDtypeStruct(qk.shape, F32),
            jax.ShapeDtypeStruct(kd.shape, F32), jax.ShapeDtypeStruct((H, n_chunks, SUBLANES, dv), F32))
    return _pcall(
        body, name="gdn_scan_bwd", out_shape=outs, grid=(n_chunks,),
        in_specs=[_gdn_blocks(a, last) for a in ins], out_specs=tuple(_gdn_blocks(a, last) for a in outs),
        scratch_shapes=[pltpu.VMEM((H, dk, dv), F32)],
        compiler_params=_params("arbitrary"),
    )(*ins)


@jax.custom_vjp
def _gdn_scan(qg, w, u, qk, kd, e):
    return _gdn_scan_fwd_call(qg, w, u, qk, kd, e)[0]


def _gdn_scan_fwd(qg, w, u, qk, kd, e):
    o, s_in, v_new = _gdn_scan_fwd_call(qg, w, u, qk, kd, e)
    return o, (qg, w, qk, kd, e, s_in, v_new)


def _gdn_scan_bwd(res, do):
    dqg, dw, du, dqk, dkd, de = _gdn_scan_bwd_call(*res, do)
    return dqg, dw, du, dqk, dkd, jnp.sum(de, axis=2, keepdims=True)


_gdn_scan.defvjp(_gdn_scan_fwd, _gdn_scan_bwd)


def _conv_taps(x_ref, r, rc, taps):
    r0 = pl.multiple_of(r * rc, rc)
    cur = x_ref[pl.ds(r0, rc), :]
    hs = pl.multiple_of(jnp.maximum(r0 - SUBLANES, 0), SUBLANES)
    halo = jnp.where(r > 0, x_ref[pl.ds(hs, SUBLANES), :], 0.0)
    return r0, [cur] + [_shift_down(cur, halo, k) for k in range(1, taps)]


def _conv_pre(w_ref, xs):
    taps = len(xs)
    gc = w_ref[taps - 1:taps, :] * xs[0]
    for j in range(taps - 1):
        gc = gc + w_ref[j:j + 1, :] * xs[taps - 1 - j]
    return gc


def _dwconv_silu_fwd_call(x, w):
    s, c = x.shape
    taps = w.shape[0]
    tc = _pick(c, (CONV_STRIPE,))
    rc = _pick(s, (CONV_ROWS, 128, 64, 32, 16, 8))

    def body(x_ref, w_ref, o_ref):
        def chunk(r, carry):
            r0, xs = _conv_taps(x_ref, r, rc, taps)
            gc = _conv_pre(w_ref, xs)
            o_ref[pl.ds(r0, rc), :] = gc * jax.nn.sigmoid(gc)
            return carry

        lax.fori_loop(0, s // rc, chunk, 0)

    stripe = pl.BlockSpec((s, tc), lambda j: (0, j))
    return _pcall(body, name="gdn_conv_fwd", out_shape=jax.ShapeDtypeStruct((s, c), F32), grid=(c // tc,),
                  in_specs=[stripe, pl.BlockSpec((taps, tc), lambda j: (0, j))], out_specs=stripe,
                  compiler_params=_params("parallel"))(x, w)


def _dwconv_silu_bwd_call(x, w, dy):
    s, c = x.shape
    taps = w.shape[0]
    tc = _pick(c, (CONV_STRIPE,))
    rc = _pick(s, (CONV_ROWS, 128, 64, 32, 16, 8))
    nr = s // rc

    def body(x_ref, w_ref, dy_ref, dx_ref, acc_ref, dgc_ref):
        def chunk1(r, acc):
            r0, xs = _conv_taps(x_ref, r, rc, taps)
            gc = _conv_pre(w_ref, xs)
            sg = jax.nn.sigmoid(gc)
            dgc = dy_ref[pl.ds(r0, rc), :] * (sg * (1.0 + gc * (1.0 - sg)))
            dgc_ref[pl.ds(r0, rc), :] = dgc
            return tuple(acc[j] + _fold8(dgc * xs[taps - 1 - j]) for j in range(taps))

        zero = jnp.zeros((SUBLANES, tc), F32)
        acc = lax.fori_loop(0, nr, chunk1, (zero,) * taps)
        for j in range(taps):
            acc_ref[j] = acc[j]

        def chunk2(r, carry):
            r0 = pl.multiple_of(r * rc, rc)
            cur = dgc_ref[pl.ds(r0, rc), :]
            hs = pl.multiple_of(jnp.minimum(r0 + rc, s - SUBLANES), SUBLANES)
            halo = jnp.where(r < nr - 1, dgc_ref[pl.ds(hs, SUBLANES), :], 0.0)
            dx = w_ref[taps - 1:taps, :] * cur
            for j in range(taps - 1):
                dx = dx + w_ref[j:j + 1, :] * _shift_up(cur, halo, taps - 1 - j)
            dx_ref[pl.ds(r0, rc), :] = dx
            return carry

        lax.fori_loop(0, nr, chunk2, 0)

    stripe = pl.BlockSpec((s, tc), lambda j: (0, j))
    dx, acc = _pcall(
        body, name="gdn_conv_bwd",
        out_shape=(jax.ShapeDtypeStruct((s, c), F32), jax.ShapeDtypeStruct((taps, SUBLANES, c), F32)),
        grid=(c // tc,),
        in_specs=[stripe, pl.BlockSpec((taps, tc), lambda j: (0, j)), stripe],
        out_specs=(stripe, pl.BlockSpec((taps, SUBLANES, tc), lambda j: (0, 0, j))),
        scratch_shapes=[pltpu.VMEM((s, tc), F32)],
        compiler_params=_params("parallel"))(x, w, dy)
    return dx, jnp.sum(acc, axis=1)


@jax.custom_vjp
def _dwconv_silu(x, w):
    return _dwconv_silu_fwd_call(x, w)


def _dwconv_silu_fwd(x, w):
    return _dwconv_silu_fwd_call(x, w), (x, w)


def _dwconv_silu_bwd(res, dy):
    return _dwconv_silu_bwd_call(*res, dy)


_dwconv_silu.defvjp(_dwconv_silu_fwd, _dwconv_silu_bwd)


def _gated_deltanet(hx, p):
    H, C = GDN_HEADS, GDN_CHUNK
    s, d = hx.shape
    dk = dv = d // H
    qkvz = _linear(hx, p['gdn_w_qkvz'], "gdn_in")
    ab = _linear(hx, p['gdn_w_ab'], "gdn_ab")
    qkv, z = qkvz[:, :3 * d], qkvz[:, 3 * d:]
    a, bt = ab[:, :H], ab[:, H:2 * H]
    qkv = _dwconv_silu(qkv, p['gdn_conv_w'])
    q, k, v = qkv[:, :d], qkv[:, d:2 * d], qkv[:, 2 * d:]
    q = _l2norm(_chunk_heads(q, H, C)) * (dk ** -0.5)
    k = _l2norm(_chunk_heads(k, H, C))
    v = _chunk_heads(v, H, C)
    beta = jax.nn.sigmoid(_chunk_heads(bt, H, C)[..., 0])
    g = -jnp.exp(p['gdn_a_log'])[:, None, None] * jax.nn.softplus(
        _chunk_heads(a, H, C)[..., 0] + p['gdn_dt_bias'][:, None, None])
    gc = jnp.cumsum(g, axis=-1)
    idx = jnp.arange(C)
    causal = idx[:, None] >= idx[None, :]
    strict = idx[:, None] > idx[None, :]
    diff = gc[..., :, None] - gc[..., None, :]
    decay = jnp.where(causal, jnp.exp(jnp.where(causal, diff, 0.0)), 0.0)
    kb = k * beta[..., None]
    kk = jnp.where(strict, jnp.einsum('hncd,hnmd->hncm', kb, k) * decay, 0.0)
    eye = jnp.eye(C, dtype=F32)
    rhs = jnp.concatenate([v * beta[..., None], kb * jnp.exp(gc)[..., None]], axis=-1)
    sol = lax.linalg.triangular_solve(kk + eye, rhs, left_side=True, lower=True, unit_diagonal=True)
    u, w = sol[..., :dv], sol[..., dv:]
    qk = jnp.where(causal, jnp.einsum('hncd,hnmd->hncm', q, k) * decay, 0.0)

    g_last = gc[..., -1:]
    e = jnp.broadcast_to(jnp.exp(g_last)[..., None], gc.shape[:2] + (1, dv))
    o = _gdn_scan(q * jnp.exp(gc)[..., None], w, u, qk, k * jnp.exp(g_last - gc)[..., None], e)
    o = _unchunk_heads(o)
    o = o * lax.rsqrt(jnp.mean(o * o, axis=-1, keepdims=True) + 1e-6) * p['gdn_norm_w']
    o = o * jax.nn.silu(z.reshape(s, H, dv))
    return _linear(o.reshape(s, H * dv), p['gdn_w_out'], "gdn_out")


def _ret_consts(c):
    log_gamma = jnp.log(1.0 - jnp.power(2.0, -5.0 - jnp.arange(RET_HEADS, dtype=F32)))
    idx = jnp.arange(c, dtype=F32)
    rel = idx[:, None] - idx[None, :]
    dmask = jnp.where(rel >= 0, jnp.exp(jnp.maximum(rel, 0.0) * log_gamma[:, None, None]), 0.0)
    zeta = jnp.exp((c - 1.0 - idx)[None, :] * log_gamma[:, None])[..., None]
    xi = jnp.exp((idx + 1.0)[None, :] * log_gamma[:, None])[..., None]
    gamma_c = jnp.exp(c * log_gamma)[:, None, None]
    return dmask, zeta, xi, gamma_c


def _ret_angles(s, dk):
    pos = jnp.arange(s, dtype=F32)
    inv_freq = RET_ROPE_BASE ** (-jnp.linspace(0.0, 1.0, dk // 2, dtype=F32))
    ang = pos[:, None] * inv_freq[None, :]
    return jnp.cos(ang), jnp.sin(ang)


def _rot(t, cs, sn):
    half = t.shape[1] // 2
    t1, t2 = t[:, :half], t[:, half:]
    return jnp.concatenate([t1 * cs - t2 * sn, t1 * sn + t2 * cs], axis=1)


def _rot_t(t, cs, sn):
    half = t.shape[1] // 2
    t1, t2 = t[:, :half], t[:, half:]
    return jnp.concatenate([t1 * cs + t2 * sn, t2 * cs - t1 * sn], axis=1)


def _ret_cols(d, dk, dv, hd):
    return (slice(hd * dk, (hd + 1) * dk), slice(d + hd * dk, d + (hd + 1) * dk),
            slice(2 * d + hd * dv, 2 * d + (hd + 1) * dv), slice(4 * d + hd * dv, 4 * d + (hd + 1) * dv))


def _ret_fwd_call(proj):
    s, d6 = proj.shape
    d = d6 // 6
    H, c = RET_HEADS, RET_CHUNK
    dk, dv = d // H, 2 * d // H
    n_chunks = s // c
    kscale = dk ** -0.5
    cos_a, sin_a = _ret_angles(s, dk)
    consts = _ret_consts(c)

    def body(p_ref, cos_ref, sin_ref, dm_ref, ze_ref, xi_ref, gc_ref, out_ref, oraw_ref, st_ref, state_ref):
        @pl.when(pl.program_id(0) == 0)
        def _():
            state_ref[...] = jnp.zeros_like(state_ref)

        cs, sn = cos_ref[...], sin_ref[...]
        for hd in range(H):
            qc, kc, vc, gcol = _ret_cols(d, dk, dv, hd)
            ocol = slice(hd * dv, (hd + 1) * dv)
            qb = _rot(p_ref[:, qc], cs, sn).astype(MXU_DTYPE)
            kr = _rot(p_ref[:, kc], cs, sn) * kscale
            kb = kr.astype(MXU_DTYPE)
            vb = p_ref[:, vc].astype(MXU_DTYPE)
            st = state_ref[hd]
            stb = st.astype(MXU_DTYPE)
            st_ref[hd] = stb
            sc = lax.dot_general(qb, kb, _NT, preferred_element_type=F32) * dm_ref[hd]
            o = (jnp.dot(sc.astype(MXU_DTYPE), vb, preferred_element_type=F32)
                 + jnp.dot(qb, stb, preferred_element_type=F32) * xi_ref[hd])
            state_ref[hd] = st * gc_ref[hd] + lax.dot_general((kr * ze_ref[hd]).astype(MXU_DTYPE), vb, _TN,
                                                              preferred_element_type=F32)
            oraw_ref[:, ocol] = o
            oc = o - jnp.mean(o, axis=-1, keepdims=True)
            on = oc * lax.rsqrt(jnp.mean(oc * oc, axis=-1, keepdims=True) + 1e-6)
            gate = p_ref[:, gcol]
            out_ref[:, ocol] = on * (gate * jax.nn.sigmoid(gate))

    row = lambda width: pl.BlockSpec((c, width), lambda n: (n, 0))
    whole = lambda a: pl.BlockSpec(a.shape, lambda n: (0,) * a.ndim)
    return _pcall(
        body, name="ret_fwd",
        out_shape=(jax.ShapeDtypeStruct((s, 2 * d), F32), jax.ShapeDtypeStruct((s, 2 * d), F32),
                   jax.ShapeDtypeStruct((n_chunks, H, dk, dv), MXU_DTYPE)),
        grid=(n_chunks,),
        in_specs=[row(d6), row(dk // 2), row(dk // 2)] + [whole(a) for a in consts],
        out_specs=(row(2 * d), row(2 * d), pl.BlockSpec((None, H, dk, dv), lambda n: (n, 0, 0, 0))),
        scratch_shapes=[pltpu.VMEM((H, dk, dv), F32)],
        compiler_params=_params("arbitrary"),
    )(proj, cos_a, sin_a, *consts)


def _ret_bwd_call(proj, oraw, states, dout):
    s, d6 = proj.shape
    d = d6 // 6
    H, c = RET_HEADS, RET_CHUNK
    dk, dv = d // H, 2 * d // H
    n_chunks = s // c
    kscale = dk ** -0.5
    cos_a, sin_a = _ret_angles(s, dk)
    consts = _ret_consts(c)

    def body(p_ref, cos_ref, sin_ref, dm_ref, ze_ref, xi_ref, gc_ref, oraw_ref, st_ref, do_ref, dp_ref, ds_ref):
        @pl.when(pl.program_id(0) == 0)
        def _():
            ds_ref[...] = jnp.zeros_like(ds_ref)

        cs, sn = cos_ref[...], sin_ref[...]
        for hd in range(H):
            qc, kc, vc, gcol = _ret_cols(d, dk, dv, hd)
            ocol = slice(hd * dv, (hd + 1) * dv)
            qb = _rot(p_ref[:, qc], cs, sn).astype(MXU_DTYPE)
            kr = _rot(p_ref[:, kc], cs, sn) * kscale
            kb = kr.astype(MXU_DTYPE)
            vb = p_ref[:, vc].astype(MXU_DTYPE)
            gate = p_ref[:, gcol]
            o = oraw_ref[:, ocol]
            oc = o - jnp.mean(o, axis=-1, keepdims=True)
            rstd = lax.rsqrt(jnp.mean(oc * oc, axis=-1, keepdims=True) + 1e-6)
            on = oc * rstd
            dout_h = do_ref[:, ocol]
            sg = jax.nn.sigmoid(gate)
            dp_ref[:, gcol] = dout_h * on * (sg * (1.0 + gate * (1.0 - sg)))
            don = dout_h * (gate * sg)
            do_raw = rstd * (don - jnp.mean(don, axis=-1, keepdims=True)
                             - on * jnp.mean(don * on, axis=-1, keepdims=True))
            dob = do_raw.astype(MXU_DTYPE)
            stb = st_ref[hd]
            ds = ds_ref[hd]
            dsb = ds.astype(MXU_DTYPE)
            dm = dm_ref[hd]
            scb = (lax.dot_general(qb, kb, _NT, preferred_element_type=F32) * dm).astype(MXU_DTYPE)
            dsc = (lax.dot_general(dob, vb, _NT, preferred_element_type=F32) * dm).astype(MXU_DTYPE)
            dqr = jnp.dot(dsc, kb, preferred_element_type=F32)
            dkr = lax.dot_general(dsc, qb, _TN, preferred_element_type=F32)
            dvv = lax.dot_general(scb, dob, _TN, preferred_element_type=F32)
            doi = (do_raw * xi_ref[hd]).astype(MXU_DTYPE)
            dqr = dqr + lax.dot_general(doi, stb, _NT, preferred_element_type=F32)
            ds_in = lax.dot_general(qb, doi, _TN, preferred_element_type=F32)
            ze = ze_ref[hd]
            dkr = dkr + lax.dot_general(vb, dsb, _NT, preferred_element_type=F32) * ze
            dvv = dvv + jnp.dot((kr * ze).astype(MXU_DTYPE), dsb, preferred_element_type=F32)
            ds_ref[hd] = ds * gc_ref[hd] + ds_in
            dp_ref[:, qc] = _rot_t(dqr, cs, sn)
            dp_ref[:, kc] = _rot_t(dkr * kscale, cs, sn)
            dp_ref[:, vc] = dvv

    last = n_chunks - 1
    row = lambda width: pl.BlockSpec((c, width), lambda n: (last - n, 0))
    whole = lambda a: pl.BlockSpec(a.shape, lambda n: (0,) * a.ndim)
    return _pcall(
        body, name="ret_bwd", out_shape=jax.ShapeDtypeStruct((s, d6), F32),
        grid=(n_chunks,),
        in_specs=[row(d6), row(dk // 2), row(dk // 2)] + [whole(a) for a in consts]
                 + [row(2 * d), pl.BlockSpec((None, H, dk, dv), lambda n: (last - n, 0, 0, 0)), row(2 * d)],
        out_specs=row(d6),
        scratch_shapes=[pltpu.VMEM((H, dk, dv), F32)],
        compiler_params=_params("arbitrary"),
    )(proj, cos_a, sin_a, *consts, oraw, states, dout)


@jax.custom_vjp
def _ret_core(proj):
    return _ret_fwd_call(proj)[0]


def _ret_core_fwd(proj):
    out, oraw, states = _ret_fwd_call(proj)
    return out, (proj, oraw, states)


def _ret_core_bwd(res, dout):
    return (_ret_bwd_call(*res, dout),)


_ret_core.defvjp(_ret_core_fwd, _ret_core_bwd)


def _retention(hx, p):
    return _linear(_ret_core(_linear(hx, p['ret_w_in'], "ret_in")), p['ret_w_out'], "ret_out")


SQRT_HALF = 2.0 ** -0.5
INV_SQRT_2PI = (2.0 * math.pi) ** -0.5


def _gmlp_front(p_ref, g_ref, b_ref, w):
    x = p_ref[...]
    cdf = 0.5 * (1.0 + lax.erf(x * SQRT_HALF))
    uv = x * cdf
    u, v = uv[:, :w], uv[:, w:]
    vc = v - jnp.mean(v, axis=-1, keepdims=True)
    rstd = lax.rsqrt(jnp.mean(vc * vc, axis=-1, keepdims=True) + LN_EPS)
    vhat = vc * rstd
    return x, cdf, u, vhat, rstd, vhat * g_ref[...] + b_ref[...]


def _gmlp_fwd_call(proj, ln_g, ln_b, ws, bs):
    s, w2 = proj.shape
    w = w2 // 2
    c, G = GMLP_CHUNK, GMLP_GROUPS
    gw = w // G

    def body(p_ref, g_ref, b_ref, ws_ref, bs_ref, o_ref):
        _, _, u, _, _, vn = _gmlp_front(p_ref, g_ref, b_ref, w)
        for gi in range(G):
            cols = slice(gi * gw, (gi + 1) * gw)
            vs = jnp.dot(ws_ref[gi].astype(MXU_DTYPE), vn[:, cols].astype(MXU_DTYPE),
                         preferred_element_type=F32) + bs_ref[gi]
            o_ref[:, cols] = u[:, cols] * vs

    whole = lambda a: pl.BlockSpec(a.shape, lambda n: (0,) * a.ndim)
    args = (_vec(ln_g), _vec(ln_b), ws, bs)
    return _pcall(
        body, name="gmlp_fwd", out_shape=jax.ShapeDtypeStruct((s, w), F32), grid=(s // c,),
        in_specs=[pl.BlockSpec((c, w2), lambda n: (n, 0))] + [whole(a) for a in args],
        out_specs=pl.BlockSpec((c, w), lambda n: (n, 0)),
        compiler_params=_params("parallel"),
    )(proj, *args)


def _gmlp_bwd_call(proj, ln_g, ln_b, ws, bs, dout):
    s, w2 = proj.shape
    w = w2 // 2
    c, G = GMLP_CHUNK, GMLP_GROUPS
    gw = w // G

    def body(p_ref, g_ref, b_ref, ws_ref, bs_ref, do_ref, dp_ref, dws_ref, dbs_ref, dgb_ref):
        @pl.when(pl.program_id(0) == 0)
        def _():
            dws_ref[...] = jnp.zeros_like(dws_ref)
            dbs_ref[...] = jnp.zeros_like(dbs_ref)
            dgb_ref[...] = jnp.zeros_like(dgb_ref)

        x, cdf, u, vhat, rstd, vn = _gmlp_front(p_ref, g_ref, b_ref, w)
        dout = do_ref[...]
        du_parts, dvn_parts = [], []
        for gi in range(G):
            cols = slice(gi * gw, (gi + 1) * gw)
            wsg = ws_ref[gi].astype(MXU_DTYPE)
            vng = vn[:, cols].astype(MXU_DTYPE)
            vs = jnp.dot(wsg, vng, preferred_element_type=F32) + bs_ref[gi]
            du_parts.append(dout[:, cols] * vs)
            dvs = dout[:, cols] * u[:, cols]
            dbs_ref[:, cols] += dvs
            dvsb = dvs.astype(MXU_DTYPE)
            dws_ref[gi] += lax.dot_general(dvsb, vng, _NT, preferred_element_type=F32)
            dvn_parts.append(lax.dot_general(wsg, dvsb, _TN, preferred_element_type=F32))
        dvn = jnp.concatenate(dvn_parts, axis=1)
        dgb_ref[0] += _fold8(dvn * vhat)
        dgb_ref[1] += _fold8(dvn)
        dvh = dvn * g_ref[...]
        dv = rstd * (dvh - jnp.mean(dvh, axis=-1, keepdims=True)
                     - vhat * jnp.mean(dvh * vhat, axis=-1, keepdims=True))
        duv = jnp.concatenate(du_parts + [dv], axis=1)
        dp_ref[...] = duv * (cdf + x * (jnp.exp(-0.5 * x * x) * INV_SQRT_2PI))

    whole = lambda a: pl.BlockSpec(a.shape, lambda n: (0,) * a.ndim)
    args = (_vec(ln_g), _vec(ln_b), ws, bs)
    acc = lambda *shape: pl.BlockSpec(shape, lambda n: (0,) * len(shape))
    return _pcall(
        body, name="gmlp_bwd",
        out_shape=(jax.ShapeDtypeStruct((s, w2), F32), jax.ShapeDtypeStruct((G, c, c), F32),
                   jax.ShapeDtypeStruct((c, w), F32), jax.ShapeDtypeStruct((2, SUBLANES, w), F32)),
        grid=(s // c,),
        in_specs=[pl.BlockSpec((c, w2), lambda n: (n, 0))] + [whole(a) for a in args]
                 + [pl.BlockSpec((c, w), lambda n: (n, 0))],
        out_specs=(pl.BlockSpec((c, w2), lambda n: (n, 0)), acc(G, c, c), acc(c, w), acc(2, SUBLANES, w)),
        compiler_params=_params("arbitrary"),
    )(proj, *args, dout)


def _gmlp_mask(c):
    return jnp.tril(jnp.ones((c, c), dtype=bool))


@jax.custom_vjp
def _gmlp_core(proj, ln_g, ln_b, w_s, b_s):
    ws = jnp.where(_gmlp_mask(GMLP_CHUNK), w_s, 0.0)
    return _gmlp_fwd_call(proj, ln_g, ln_b, ws, b_s[..., None])


def _gmlp_core_fwd(proj, ln_g, ln_b, w_s, b_s):
    return _gmlp_core(proj, ln_g, ln_b, w_s, b_s), (proj, ln_g, ln_b, w_s, b_s)


def _gmlp_core_bwd(res, dout):
    proj, ln_g, ln_b, w_s, b_s = res
    mask = _gmlp_mask(GMLP_CHUNK)
    dproj, dws, dbs, dgb = _gmlp_bwd_call(proj, ln_g, ln_b, jnp.where(mask, w_s, 0.0), b_s[..., None], dout)
    dgb = jnp.sum(dgb, axis=1)
    c = GMLP_CHUNK
    db_s = jnp.sum(dbs.reshape(c, GMLP_GROUPS, -1), axis=-1).T
    return dproj, dgb[0], dgb[1], jnp.where(mask, dws, 0.0), db_s


_gmlp_core.defvjp(_gmlp_core_fwd, _gmlp_core_bwd)


def _chunked_gmlp(hx, p):
    core = _gmlp_core(_linear(hx, p['gmlp_w_in'], "gmlp_in"), p['gmlp_ln_g'], p['gmlp_ln_b'],
                      p['gmlp_w_s'], p['gmlp_b_s'])
    return _linear(core, p['gmlp_w_out'], "gmlp_out")


def _stick_breaking(hx, p):
    H = SB_HEADS
    s, d = hx.shape
    dh = d // H
    qkv = _linear(hx, p['sb_w_in'], "sb_in")
    q, k, v = (qkv[:, j * d:(j + 1) * d].reshape(s, H, dh).transpose(1, 0, 2) for j in range(3))
    o = _sb_core(q, k, v)
    return _linear(o.transpose(1, 0, 2).reshape(s, d), p['sb_w_out'], "sb_out")


MIXERS = (_gated_deltanet, _retention, _chunked_gmlp, _stick_breaking)


def _trunk_grad(x, mods, p, target):
    d = x.shape[-1]
    saved = []
    for i in range(DEPTH):
        sh1, sc1, g1, sh2, sc2, g2 = (mods[i, j * d:(j + 1) * d] for j in range(6))
        h1 = _modulate(x, sc1, sh1, F32, "mod_a%d" % i)
        y1, mixer_vjp = jax.vjp(MIXERS[i], h1, {n: p[n] for n in MIXER_PARAMS[i]})
        x1 = _resid_ln(x, y1, g1, p['ln_g'][i, 0], p['ln_b'][i, 0], "ln_a%d" % i)
        h2 = _modulate(x1, sc2, sh2, MXU_DTYPE, "mod_b%d" % i)
        gu = _mm(h2, p['ffn_up'][i], 'nn', "ffn_up%d_fwd" % i)
        act = _ffn_gate(gu, p['ffn_conv_w'][i], p['ffn_conv_b'][i], "ffn_gate%d" % i)
        y2 = _mm(act, p['ffn_down'][i], 'nn', "ffn_down%d_fwd" % i)
        x2 = _resid_ln(x1, y2, g2, p['ln_g'][i, 1], p['ln_b'][i, 1], "ln_b%d" % i)
        saved.append((x, y1, mixer_vjp, x1, h2, gu, act, y2))
        x = x2
    loss, dx = _loss_head(x, target, "loss_head")

    dp = {n: None for n in p}
    d_ln_g, d_ln_b, d_up, d_down, d_cw, d_cb, dmods = [], [], [], [], [], [], []
    for i in reversed(range(DEPTH)):
        x0, y1, mixer_vjp, x1, h2, gu, act, y2 = saved[i]
        sh1, sc1, g1, sh2, sc2, g2 = (mods[i, j * d:(j + 1) * d] for j in range(6))
        dxa, dy2, dgam2, dbet2, dg2 = _resid_ln_bwd(x1, y2, g2, p['ln_g'][i, 1], dx, "ln_b%d_bwd" % i)
        dact = _mm(dy2, p['ffn_down'][i], 'nt', "ffn_down%d_dx" % i)
        d_down.append(_mm(act, dy2, 'tn', "ffn_down%d_dw" % i))
        dgate, dupp, dcw, dcb = _ffn_gate_bwd(dact, gu, p['ffn_conv_w'][i], p['ffn_conv_b'][i],
                                              "ffn_gate%d_bwd" % i)
        dgu = jnp.concatenate([dgate, dupp], axis=1)
        dh2 = _mm(dgu, p['ffn_up'][i], 'nt', "ffn_up%d_dx" % i)
        d_up.append(_mm(h2, dgu, 'tn', "ffn_up%d_dw" % i))
        dx1, dsc2, dsh2 = _modulate_bwd(dxa, dh2, x1, sc2, "mod_b%d_bwd" % i)
        dxa, dy1, dgam1, dbet1, dg1 = _resid_ln_bwd(x0, y1, g1, p['ln_g'][i, 0], dx1, "ln_a%d_bwd" % i)
        dh1, dmix = mixer_vjp(dy1)
        dp.update(dmix)
        dx, dsc1, dsh1 = _modulate_bwd(dxa, dh1, x0, sc1, "mod_a%d_bwd" % i)
        d_ln_g.append(jnp.stack([dgam1, dgam2]))
        d_ln_b.append(jnp.stack([dbet1, dbet2]))
        d_cw.append(dcw)
        d_cb.append(dcb)
        dmods.append(jnp.concatenate([dsh1, dsc1, dg1, dsh2, dsc2, dg2]))
    for n, parts in (('ln_g', d_ln_g), ('ln_b', d_ln_b), ('ffn_up', d_up), ('ffn_down', d_down),
                     ('ffn_conv_w', d_cw), ('ffn_conv_b', d_cb)):
        dp[n] = jnp.stack(parts[::-1])
    return loss, dx, jnp.stack(dmods[::-1]), dp


def _join(blocks, axis):
    return jnp.concatenate([blocks[d] for d in range(N_DEV)], axis=axis)


def _split(whole, axis):
    n = whole.shape[axis] // N_DEV
    return jnp.stack([lax.slice_in_dim(whole, d * n, (d + 1) * n, axis=axis) for d in range(N_DEV)])


def _pad8(a):
    pad = (-a.shape[0]) % 8
    return jnp.pad(a, ((0, pad), (0, 0))) if pad else a


def _pack_big_grads(full_grads, axes):
    per_dev = jnp.concatenate([_split(g, ax).reshape(N_DEV, -1) for g, ax in zip(full_grads, axes)], axis=1)
    pad = (-per_dev.shape[1]) % (BIG_ROW_ALIGN * LANES)
    if pad:
        per_dev = jnp.pad(per_dev, ((0, 0), (0, pad)))
    return per_dev.reshape(N_DEV, -1, LANES)


def kernel(x, c, cond_w, cond_b, ada_w, ada_b, ln_g, ln_b, ffn_up, ffn_conv_w, ffn_conv_b, ffn_down, gdn_w_in, gdn_conv_w, gdn_a_log, gdn_dt_bias, gdn_norm_w, gdn_w_out, ret_w_in, ret_w_out, gmlp_w_in, gmlp_ln_g, gmlp_ln_b, gmlp_w_s, gmlp_b_s, gmlp_w_out, sb_w_in, sb_w_out, loss_target, m_cond_w, m_cond_b, m_ada_w, m_ada_b, m_ln_g, m_ln_b, m_ffn_up, m_ffn_conv_w, m_ffn_conv_b, m_ffn_down, m_gdn_w_in, m_gdn_conv_w, m_gdn_a_log, m_gdn_dt_bias, m_gdn_norm_w, m_gdn_w_out, m_ret_w_in, m_ret_w_out, m_gmlp_w_in, m_gmlp_ln_g, m_gmlp_ln_b, m_gmlp_w_s, m_gmlp_b_s, m_gmlp_w_out, m_sb_w_in, m_sb_w_out, v_cond_w, v_cond_b, v_ada_w, v_ada_b, v_ln_g, v_ln_b, v_ffn_up, v_ffn_conv_w, v_ffn_conv_b, v_ffn_down, v_gdn_w_in, v_gdn_conv_w, v_gdn_a_log, v_gdn_dt_bias, v_gdn_norm_w, v_gdn_w_out, v_ret_w_in, v_ret_w_out, v_gmlp_w_in, v_gmlp_ln_g, v_gmlp_ln_b, v_gmlp_w_s, v_gmlp_b_s, v_gmlp_w_out, v_sb_w_in, v_sb_w_out):
    w = dict(cond_w=cond_w, cond_b=cond_b, ada_w=ada_w, ada_b=ada_b, ln_g=ln_g, ln_b=ln_b, ffn_up=ffn_up,
             ffn_conv_w=ffn_conv_w, ffn_conv_b=ffn_conv_b, ffn_down=ffn_down, gdn_w_in=gdn_w_in,
             gdn_conv_w=gdn_conv_w, gdn_a_log=gdn_a_log, gdn_dt_bias=gdn_dt_bias, gdn_norm_w=gdn_norm_w,
             gdn_w_out=gdn_w_out, ret_w_in=ret_w_in, ret_w_out=ret_w_out, gmlp_w_in=gmlp_w_in,
             gmlp_ln_g=gmlp_ln_g, gmlp_ln_b=gmlp_ln_b, gmlp_w_s=gmlp_w_s, gmlp_b_s=gmlp_b_s,
             gmlp_w_out=gmlp_w_out, sb_w_in=sb_w_in, sb_w_out=sb_w_out)
    mom = dict(cond_w=m_cond_w, cond_b=m_cond_b, ada_w=m_ada_w, ada_b=m_ada_b, ln_g=m_ln_g, ln_b=m_ln_b,
               ffn_up=m_ffn_up, ffn_conv_w=m_ffn_conv_w, ffn_conv_b=m_ffn_conv_b, ffn_down=m_ffn_down,
               gdn_w_in=m_gdn_w_in, gdn_conv_w=m_gdn_conv_w, gdn_a_log=m_gdn_a_log, gdn_dt_bias=m_gdn_dt_bias,
               gdn_norm_w=m_gdn_norm_w, gdn_w_out=m_gdn_w_out, ret_w_in=m_ret_w_in, ret_w_out=m_ret_w_out,
               gmlp_w_in=m_gmlp_w_in, gmlp_ln_g=m_gmlp_ln_g, gmlp_ln_b=m_gmlp_ln_b, gmlp_w_s=m_gmlp_w_s,
               gmlp_b_s=m_gmlp_b_s, gmlp_w_out=m_gmlp_w_out, sb_w_in=m_sb_w_in, sb_w_out=m_sb_w_out)
    var = dict(cond_w=v_cond_w, cond_b=v_cond_b, ada_w=v_ada_w, ada_b=v_ada_b, ln_g=v_ln_g, ln_b=v_ln_b,
               ffn_up=v_ffn_up, ffn_conv_w=v_ffn_conv_w, ffn_conv_b=v_ffn_conv_b, ffn_down=v_ffn_down,
               gdn_w_in=v_gdn_w_in, gdn_conv_w=v_gdn_conv_w, gdn_a_log=v_gdn_a_log, gdn_dt_bias=v_gdn_dt_bias,
               gdn_norm_w=v_gdn_norm_w, gdn_w_out=v_gdn_w_out, ret_w_in=v_ret_w_in, ret_w_out=v_ret_w_out,
               gmlp_w_in=v_gmlp_w_in, gmlp_ln_g=v_gmlp_ln_g, gmlp_ln_b=v_gmlp_ln_b, gmlp_w_s=v_gmlp_w_s,
               gmlp_b_s=v_gmlp_b_s, gmlp_w_out=v_gmlp_w_out, sb_w_in=v_sb_w_in, sb_w_out=v_sb_w_out)

    me = _my_id()
    x = x[0]
    target = loss_target[0]
    d = x.shape[-1]
    dsh = d // N_DEV
    msh = ada_w.shape[-1]

    c_all = _exchange(_pad8(c), False, "gather_c")[:, 0, :]
    c_mine = lax.dynamic_slice_in_dim(c_all, me * dsh, dsh, axis=1)
    pre_part = _mm(c_mine, cond_w, 'nn', "cond_fwd")
    pre = jnp.sum(_exchange(pre_part, False, "gather_pre"), axis=0) + cond_b
    e_all = jax.nn.silu(pre)
    mod_part = jnp.concatenate([_mm(e_all, ada_w[i], 'nn', "ada_fwd%d" % i) for i in range(DEPTH)], axis=0)
    mod_all = _exchange(mod_part, False, "gather_mod")
    mod_all = mod_all.reshape(N_DEV, DEPTH, N_DEV, msh)
    mods = lax.dynamic_index_in_dim(mod_all, me, axis=2, keepdims=False)
    mods = mods.transpose(1, 0, 2).reshape(DEPTH, N_DEV * msh) + ada_b

    big_names = list(BIG)
    packed = _pack_rows([w[n] for n in big_names], BF16, BIG_ROW_ALIGN)
    gathered = _gather_two_level(packed, "gather_weights")
    blocks = _unpack_rows(gathered, [w[n].shape for n in big_names])
    p = {n: _join(b, BIG[n]) for n, b in zip(big_names, blocks)}
    for n in big_names:
        if not n.startswith('ffn_'):
            p[n] = p[n].astype(F32)
    sm_names = list(SMALL_SHARDED)
    sm_packed = _pack_rows([w[n] for n in sm_names], F32)
    sm_blocks = _unpack_rows(_exchange(sm_packed, False, "gather_small"), [w[n].shape for n in sm_names])
    for n, b in zip(sm_names, sm_blocks):
        p[n] = _join(b, SMALL_SHARDED[n])
    for n in SMALL_REPL:
        p[n] = w[n]
    n_qkvz = 4 * d
    p['gdn_w_qkvz'] = p['gdn_w_in'][:, :n_qkvz]
    p['gdn_w_ab'] = jnp.pad(p['gdn_w_in'][:, n_qkvz:], ((0, 0), (0, LANES - 2 * GDN_HEADS)))
    del p['gdn_w_in']

    loss_local, dx, dmods, dp = _trunk_grad(x, mods, p, target)
    dp['gdn_w_in'] = jnp.concatenate([dp.pop('gdn_w_qkvz'), dp.pop('gdn_w_ab')[:, :2 * GDN_HEADS]], axis=1)

    dmod_all = _exchange(dmods.reshape(-1, d), False, "gather_dmod").reshape(N_DEV, DEPTH, 6 * d)
    grads = {'ada_b': jnp.sum(dmod_all, axis=0)}
    dm_mine = lax.dynamic_slice_in_dim(dmod_all, me * msh, msh, axis=2)
    grads['ada_w'] = jnp.stack([_mm_outer(e_all, dm_mine[:, i], "ada_dw%d" % i) for i in range(DEPTH)])
    de_part = _mm(dm_mine[:, 0], ada_w[0], 'nt', "ada_de0")
    for i in range(1, DEPTH):
        de_part = de_part + _mm(dm_mine[:, i], ada_w[i], 'nt', "ada_de%d" % i)
    de_all = jnp.sum(_exchange(de_part, False, "gather_de"), axis=0)
    sig = jax.nn.sigmoid(pre)
    dpre = de_all * (sig * (1.0 + pre * (1.0 - sig)))
    grads['cond_b'] = jnp.sum(dpre, axis=0)
    grads['cond_w'] = _mm_outer(c_mine, dpre, "cond_dw")

    small_names = sm_names + SMALL_REPL
    small_packed = _pack_rows([loss_local.reshape(1)] + [dp[n] for n in small_names], F32)
    small_sum = _sum_slots(_exchange(small_packed, False, "gather_small_grads"), "sum_small_grads")
    small = _unpack_rows(small_sum, [(1,)] + [dp[n].shape for n in small_names])
    loss = small[0][0]
    for n, g in zip(small_names, small[1:]):
        if n in SMALL_SHARDED:
            ax = SMALL_SHARDED[n]
            g = lax.dynamic_slice_in_dim(g, me * w[n].shape[ax], w[n].shape[ax], axis=ax)
        grads[n] = g

    send = _pack_big_grads([dp[n] for n in big_names], [BIG[n] for n in big_names])
    shapes = [w[n].shape for n in big_names]
    outs = _adamw(_reduce_to_owner(send), *[_pack_rows([t[n] for n in big_names], F32, BIG_ROW_ALIGN) for t in (w, mom, var)],
                  "adamw_big")
    g_b, d_b, m_b, v_b = (_unpack_rows(o, shapes) for o in outs)
    delta, new_m, new_v = {}, {}, {}
    for j, n in enumerate(big_names):
        grads[n], delta[n], new_m[n], new_v[n] = g_b[j], d_b[j], m_b[j], v_b[j]

    rest = [n for n in WEIGHTS if n not in BIG]
    shapes = [w[n].shape for n in rest]
    outs = _adamw([(_pack_rows([grads[n] for n in rest], F32, BIG_ROW_ALIGN)[None], 0)],
                  *[_pack_rows([t[n] for n in rest], F32, BIG_ROW_ALIGN) for t in (w, mom, var)], "adamw_rest")
    _, d_r, m_r, v_r = (_unpack_rows(o, shapes) for o in outs)
    for j, n in enumerate(rest):
        delta[n], new_m[n], new_v[n] = d_r[j], m_r[j], v_r[j]

    return (loss, dx[None], *[grads[n] for n in WEIGHTS], *[delta[n] for n in WEIGHTS],
            *[new_m[n] for n in WEIGHTS], *[new_v[n] for n in WEIGHTS])
```

```python
import functools
import math

import jax
import jax.numpy as jnp
from jax import lax
from jax.experimental import pallas as pl
from jax.experimental.pallas import tpu as pltpu

F32 = jnp.float32
BF16 = jnp.bfloat16
MXU_DTYPE = jnp.bfloat16
MESH = pl.DeviceIdType.MESH
N_DEV = 8
LANES = 128
SUBLANES = 8
VMEM_LIMIT = 48 * 1024 * 1024

DEPTH = 4
LN_EPS = 1e-5
DN_ALPHA = (2.0 * DEPTH) ** 0.25
GDN_HEADS, GDN_CHUNK = 8, 64
RET_HEADS, RET_CHUNK, RET_ROPE_BASE = 4, 128, 10000.0
GMLP_CHUNK, GMLP_GROUPS = 128, 8
SB_HEADS = 16
ADAM_LR, ADAM_B1, ADAM_B2, ADAM_EPS, ADAM_WD, ADAM_STEP = 0.001, 0.9, 0.999, 1e-08, 0.01, 10

WEIGHTS = ['cond_w', 'cond_b', 'ada_w', 'ada_b', 'ln_g', 'ln_b', 'ffn_up', 'ffn_conv_w', 'ffn_conv_b', 'ffn_down',
           'gdn_w_in', 'gdn_conv_w', 'gdn_a_log', 'gdn_dt_bias', 'gdn_norm_w', 'gdn_w_out', 'ret_w_in', 'ret_w_out',
           'gmlp_w_in', 'gmlp_ln_g', 'gmlp_ln_b', 'gmlp_w_s', 'gmlp_b_s', 'gmlp_w_out', 'sb_w_in', 'sb_w_out']
BIG = {'ffn_up': 2, 'ffn_down': 1, 'gdn_w_in': 1, 'gdn_w_out': 0, 'ret_w_in': 1, 'ret_w_out': 0,
       'gmlp_w_in': 1, 'gmlp_w_out': 0, 'sb_w_in': 1, 'sb_w_out': 0}
SMALL_SHARDED = {'ln_g': 2, 'ln_b': 2, 'ffn_conv_w': 2, 'gdn_conv_w': 1}
SMALL_REPL = ['ffn_conv_b', 'gdn_a_log', 'gdn_dt_bias', 'gdn_norm_w', 'gmlp_ln_g', 'gmlp_ln_b', 'gmlp_w_s', 'gmlp_b_s']
MIXER_PARAMS = (('gdn_w_qkvz', 'gdn_w_ab', 'gdn_conv_w', 'gdn_a_log', 'gdn_dt_bias', 'gdn_norm_w', 'gdn_w_out'),
                ('ret_w_in', 'ret_w_out'),
                ('gmlp_w_in', 'gmlp_ln_g', 'gmlp_ln_b', 'gmlp_w_s', 'gmlp_b_s', 'gmlp_w_out'),
                ('sb_w_in', 'sb_w_out'))


def _pcall(body, **kw):
    return pl.pallas_call(body, **kw)


def _params(*semantics):
    return pltpu.CompilerParams(dimension_semantics=semantics, vmem_limit_bytes=VMEM_LIMIT)


def _my_id():
    return 4 * lax.axis_index("x") + 2 * lax.axis_index("y") + lax.axis_index("c")


def _pick(dim, prefs):
    for p in prefs:
        if dim % p == 0:
            return p
    return dim


def _exchange(src, scatter, name):
    blk = src.shape[1:] if scatter else src.shape
    out_shape = jax.ShapeDtypeStruct((N_DEV,) + tuple(blk), src.dtype)

    def body(src_ref, out_ref, send_sems, recv_sems, local_sem):
        x, y, c = lax.axis_index("x"), lax.axis_index("y"), lax.axis_index("c")
        me = 4 * x + 2 * y + c
        mine = pltpu.make_async_copy(src_ref.at[me] if scatter else src_ref, out_ref.at[me], local_sem)
        mine.start()
        copies = []
        for k in range(1, N_DEV):
            px = 1 - x if (k >> 2) & 1 else x
            py = 1 - y if (k >> 1) & 1 else y
            pc = 1 - c if k & 1 else c
            peer = 4 * px + 2 * py + pc
            cp = pltpu.make_async_remote_copy(
                src_ref=src_ref.at[peer] if scatter else src_ref,
                dst_ref=out_ref.at[me],
                send_sem=send_sems.at[k - 1], recv_sem=recv_sems.at[k - 1],
                device_id=(px, py, pc), device_id_type=MESH)
            cp.start()
            copies.append(cp)
        for cp in copies:
            cp.wait()
        mine.wait()

    return _pcall(
        body, name=name, out_shape=out_shape,
        in_specs=[pl.BlockSpec(memory_space=pl.ANY)],
        out_specs=pl.BlockSpec(memory_space=pl.ANY),
        scratch_shapes=[pltpu.SemaphoreType.DMA((N_DEV - 1,)), pltpu.SemaphoreType.DMA((N_DEV - 1,)),
                        pltpu.SemaphoreType.DMA(())],
    )(src)


def _flip(x, y, c, k):
    return (1 - x if (k >> 2) & 1 else x, 1 - y if (k >> 1) & 1 else y, 1 - c if k & 1 else c)


def _dev_id(p):
    return 4 * p[0] + 2 * p[1] + p[2]


OTHER_CHIPS = (4, 2, 6)


def _gather_two_level(src, name):
    out_shape = jax.ShapeDtypeStruct((N_DEV,) + tuple(src.shape), src.dtype)

    def body(x_ref, out_ref, send_sems, recv_sems, local_sem):
        x, y, c = lax.axis_index("x"), lax.axis_index("y"), lax.axis_index("c")
        me, sibling = (x, y, c), (x, y, 1 - c)
        chips = [_flip(x, y, c, k) for k in OTHER_CHIPS]

        def copy(k, block, to, from_src=False):
            slot = out_ref.at[_dev_id(block)]
            return pltpu.make_async_remote_copy(
                src_ref=x_ref if from_src else slot, dst_ref=slot,
                send_sem=send_sems.at[k], recv_sem=recv_sems.at[k], device_id=to, device_id_type=MESH)

        mine = pltpu.make_async_copy(x_ref, out_ref.at[_dev_id(me)], local_sem)
        mine.start()
        first = [copy(0, me, sibling, True)] + [copy(1 + j, me, chip, True) for j, chip in enumerate(chips)]
        for cp in first:
            cp.start()
        passed = [copy(4 + j, chip, sibling) for j, chip in enumerate(chips)]
        for j, chip in enumerate(chips):
            copy(1 + j, chip, me).wait_recv()
            passed[j].start()
        copy(0, sibling, me).wait_recv()
        for j, chip in enumerate(chips):
            copy(4 + j, (chip[0], chip[1], 1 - c), me).wait_recv()
        for cp in first + passed:
            cp.wait_send()
        mine.wait()

    return _pcall(
        body, name=name, out_shape=out_shape,
        in_specs=[pl.BlockSpec(memory_space=pl.ANY)],
        out_specs=pl.BlockSpec(memory_space=pl.ANY),
        scratch_shapes=[pltpu.SemaphoreType.DMA((N_DEV - 1,)), pltpu.SemaphoreType.DMA((N_DEV - 1,)),
                        pltpu.SemaphoreType.DMA(())],
    )(src)


def _send_slots(src, plan, n_out, name):
    out_shape = jax.ShapeDtypeStruct((n_out,) + tuple(src.shape[1:]), src.dtype)

    def body(src_ref, out_ref, send_sems, recv_sems):
        x, y, c = lax.axis_index("x"), lax.axis_index("y"), lax.axis_index("c")
        copies = []
        for e, (k, src_slot, dst_slot) in enumerate(plan):
            cp = pltpu.make_async_remote_copy(
                src_ref=src_ref.at[src_slot(x, y, c)], dst_ref=out_ref.at[dst_slot],
                send_sem=send_sems.at[e], recv_sem=recv_sems.at[e],
                device_id=_flip(x, y, c, k), device_id_type=MESH)
            cp.start()
            copies.append(cp)
        for cp in copies:
            cp.wait()

    return _pcall(
        body, name=name, out_shape=out_shape,
        in_specs=[pl.BlockSpec(memory_space=pl.ANY)],
        out_specs=pl.BlockSpec(memory_space=pl.ANY),
        scratch_shapes=[pltpu.SemaphoreType.DMA((len(plan),)), pltpu.SemaphoreType.DMA((len(plan),))],
    )(src)


def _reduce_to_owner(send):
    x, y, c = lax.axis_index("x"), lax.axis_index("y"), lax.axis_index("c")
    plan_a = [(1, lambda x, y, c: _dev_id((x, y, 1 - c)), 0)]
    plan_a += [(1, functools.partial(lambda k, x, y, c: _dev_id(_flip(x, y, c, k | 1)), k), 1 + j)
               for j, k in enumerate(OTHER_CHIPS)]
    from_sibling = _send_slots(send, plan_a, 1 + len(OTHER_CHIPS), "reduce_d2d")
    mine = jnp.stack([lax.dynamic_index_in_dim(send, _dev_id(_flip(x, y, c, k)), 0, keepdims=False)
                      for k in OTHER_CHIPS])
    rows = mine.shape[1]
    pair = _add_rows(mine.reshape(-1, LANES), from_sibling[1:].reshape(-1, LANES), BF16, "reduce_pair_sum")
    plan_c = [(k, functools.partial(lambda j, x, y, c: j, j), j) for j, k in enumerate(OTHER_CHIPS)]
    from_chips = _send_slots(pair.reshape(len(OTHER_CHIPS), rows, LANES), plan_c, len(OTHER_CHIPS), "reduce_ici")
    own = lax.dynamic_index_in_dim(send, _dev_id((x, y, c)), 0, keepdims=True)
    return [(own, 0), (from_sibling, 0)] + [(from_chips, j) for j in range(len(OTHER_CHIPS))]


ROW_ALIGN = 16
BIG_ROW_ALIGN = 512


def _pack_rows(parts, dtype, row_align=ROW_ALIGN):
    flat = jnp.concatenate([p.reshape(-1).astype(dtype) for p in parts])
    n = flat.shape[0]
    pad = (-n) % (row_align * LANES)
    if pad:
        flat = jnp.concatenate([flat, jnp.zeros((pad,), dtype)])
    return flat.reshape(-1, LANES)


def _unpack_rows(packed, shapes):
    lead = packed.shape[:-2]
    flat = packed.reshape(lead + (-1,))
    out, off = [], 0
    for s in shapes:
        n = math.prod(s)
        out.append(flat[..., off:off + n].reshape(lead + tuple(s)))
        off += n
    return out


MM_A_BLOCK_BYTES = 8 * 1024 * 1024


def _mm(a, b, dims, name, exact=False):
    if dims == 'nn':
        (m, k), n = a.shape, b.shape[1]
    elif dims == 'nt':
        (m, k), n = a.shape, b.shape[0]
    else:
        (k, m), n = a.shape, b.shape[1]
    tn = _pick(n, (512, 256, 128))
    tk = k if k <= 2816 else _pick(k, (2816, 2048, 1536, 1024, 512, 256, 128))
    nk = k // tk
    tm = _pick(m, (1408, 1024, 512, 256, 128))
    if m % (2 * tm) == 0 and 2 * tm * tk * a.dtype.itemsize <= MM_A_BLOCK_BYTES:
        tm = 2 * tm
    if dims == 'nn':
        a_spec = pl.BlockSpec((tm, tk), lambda i, j, kk: (i, kk))
        b_spec = pl.BlockSpec((tk, tn), lambda i, j, kk: (kk, j))
        dn = (((1,), (0,)), ((), ()))
    elif dims == 'nt':
        a_spec = pl.BlockSpec((tm, tk), lambda i, j, kk: (i, kk))
        b_spec = pl.BlockSpec((tn, tk), lambda i, j, kk: (j, kk))
        dn = (((1,), (1,)), ((), ()))
    else:
        a_spec = pl.BlockSpec((tk, tm), lambda i, j, kk: (kk, i))
        b_spec = pl.BlockSpec((tk, tn), lambda i, j, kk: (kk, j))
        dn = (((0,), (0,)), ((), ()))

    def product(a_ref, b_ref):
        if exact:
            return lax.dot_general(a_ref[...], b_ref[...], dn, precision=lax.Precision.HIGHEST,
                                   preferred_element_type=F32)
        return lax.dot_general(a_ref[...].astype(MXU_DTYPE), b_ref[...].astype(MXU_DTYPE), dn,
                               preferred_element_type=F32)

    def body(a_ref, b_ref, o_ref, *acc):
        if nk == 1:
            o_ref[...] = product(a_ref, b_ref)
            return
        acc_ref, = acc
        kk = pl.program_id(2)

        @pl.when(kk == 0)
        def _():
            acc_ref[...] = jnp.zeros_like(acc_ref)

        acc_ref[...] += product(a_ref, b_ref)

        @pl.when(kk == nk - 1)
        def _():
            o_ref[...] = acc_ref[...]

    return _pcall(
        body, name=name, out_shape=jax.ShapeDtypeStruct((m, n), F32),
        grid=(m // tm, n // tn, nk),
        in_specs=[a_spec, b_spec],
        out_specs=pl.BlockSpec((tm, tn), lambda i, j, kk: (i, j)),
        scratch_shapes=[pltpu.VMEM((tm, tn), F32)] if nk > 1 else [],
        compiler_params=_params("parallel", "parallel", "arbitrary"),
    )(a, b)


def _mm_outer(a, b, name):
    pad = LANES - a.shape[0]
    return _mm(jnp.pad(a.T, ((0, 0), (0, pad))), jnp.pad(b, ((0, pad), (0, 0))), 'nn', name, exact=True)


@functools.partial(jax.custom_vjp, nondiff_argnums=(2,))
def _linear(a, w, name):
    return _mm(a, w, 'nn', name + "_fwd")


def _linear_fwd(a, w, name):
    return _mm(a, w, 'nn', name + "_fwd"), (a, w)


def _linear_bwd(name, res, dy):
    a, w = res
    return _mm(dy, w, 'nt', name + "_dx"), _mm(a, dy, 'tn', name + "_dw")


_linear.defvjp(_linear_fwd, _linear_bwd)


SB_BK = 256
SB_STRIP = 16


def _sb_tiles(s):
    tq = _pick(s, (512, 256, 128))
    bk = min(SB_BK, tq)
    return tq, bk, tq // bk


def _sb_valid(t, sr, bk, q0, k0):
    row = lax.broadcasted_iota(jnp.int32, (sr, bk), 0) + (q0 + t * sr)
    col = lax.broadcasted_iota(jnp.int32, (sr, bk), 1) + k0
    return col < row


def _sb_tri(bk, inclusive):
    r = jnp.bitwise_and(lax.broadcasted_iota(jnp.int32, (2 * bk, bk), 0), bk - 1)
    c = lax.broadcasted_iota(jnp.int32, (2 * bk, bk), 1)
    return (r >= c).astype(BF16) if inclusive else (r > c).astype(BF16)


def _sb_split(ref, n, rows, bk, val):
    hi = val.astype(BF16)
    ref[n, rows, 0:bk] = hi
    ref[n, rows, bk:2 * bk] = (val - hi.astype(F32)).astype(BF16)


LOG2E = 1.0 / math.log(2.0)


def _sb_logits_phase(z_ref, ls_ref, hl_ref, l0_ref, n, tq, bk, sr, q0, k0, masked):
    for t in range(tq // sr):
        rows = slice(t * sr, (t + 1) * sr)
        z = z_ref[n, rows, :] * LOG2E
        ls = jnp.minimum(z, 0.0) - jnp.log(1.0 + jnp.exp2(-jnp.abs(z))) * LOG2E
        lm = ls - z
        if masked:
            lm = jnp.where(_sb_valid(t, sr, bk, q0, k0), lm, 0.0)
        ls_ref[n, rows, :] = ls
        _sb_split(hl_ref, n, rows, bk, lm)
        l0_ref[n, rows, :] = lm[:, 0:1]


def _sb_fwd_call(q, kt, v):
    h, s, dh = q.shape
    tq, bk, nt = _sb_tiles(s)
    sr = SB_STRIP

    def body(q_ref, kt_ref, v_ref, o_ref, z_ref, ls_ref, hl_ref, f_ref, a_ref, l0_ref, cl_ref, acc_ref):
        i = pl.program_id(1)
        q0 = i * tq
        cl_ref[...] = jnp.zeros_like(cl_ref)
        acc_ref[...] = jnp.zeros_like(acc_ref)
        u_excl = _sb_tri(bk, False)

        def iteration(kb0, masked):
            k0s = [pl.multiple_of(kb0 + (nt - 1 - n) * bk, bk) for n in range(nt)]
            for n in range(nt):
                z_ref[n] = jnp.dot(q_ref[...], kt_ref[:, pl.ds(k0s[n], bk)], preferred_element_type=F32)
            for n in range(nt):
                _sb_logits_phase(z_ref, ls_ref, hl_ref, l0_ref, n, tq, bk, sr, q0, k0s[n], masked)
            for n in range(nt):
                f_ref[n] = jnp.dot(hl_ref[n], u_excl, preferred_element_type=F32)
            for t in range(tq // sr):
                rows = slice(t * sr, (t + 1) * sr)
                c = cl_ref[rows, :]
                for n in range(nt):
                    f = f_ref[n, rows, :]
                    a = jnp.exp2(ls_ref[n, rows, :] + f + c)
                    if masked:
                        a = jnp.where(_sb_valid(t, sr, bk, q0, k0s[n]), a, 0.0)
                    a_ref[n, rows, :] = a.astype(a_ref.dtype)
                    c = c + f[:, 0:1] + l0_ref[n, rows, :]
                cl_ref[rows, :] = c
            for n in range(nt):
                acc_ref[...] += jnp.dot(a_ref[n], v_ref[pl.ds(k0s[n], bk), :], preferred_element_type=F32)

        def below(jj, c):
            iteration((i - 1 - jj) * tq, False)
            return c

        iteration(q0, True)
        lax.fori_loop(0, i, below, 0)
        o_ref[...] = acc_ref[...]

    return _pcall(
        body, name="sb_fwd", out_shape=jax.ShapeDtypeStruct((h, s, dh), F32),
        grid=(h, s // tq),
        in_specs=[pl.BlockSpec((None, tq, dh), lambda hh, i: (hh, i, 0)),
                  pl.BlockSpec((None, dh, s), lambda hh, i: (hh, 0, 0)),
                  pl.BlockSpec((None, s, dh), lambda hh, i: (hh, 0, 0))],
        out_specs=pl.BlockSpec((None, tq, dh), lambda hh, i: (hh, i, 0)),
        scratch_shapes=[pltpu.VMEM((nt, tq, bk), F32), pltpu.VMEM((nt, tq, bk), F32),
                        pltpu.VMEM((nt, tq, 2 * bk), BF16), pltpu.VMEM((nt, tq, bk), F32),
                        pltpu.VMEM((nt, tq, bk), q.dtype), pltpu.VMEM((nt, tq, 1), F32),
                        pltpu.VMEM((tq, 1), F32), pltpu.VMEM((tq, dh), F32)],
        compiler_params=_params("parallel", "arbitrary"),
    )(q, kt, v)


def _sb_bwd_call(q, qt, k, kt, vt, o, do, dot):
    h, s, dh = q.shape
    tq, bk, nt = _sb_tiles(s)
    sr = SB_STRIP

    def body(q_ref, qt_ref, k_ref, kt_ref, vt_ref, o_ref, do_ref, dot_ref, dq_ref, dkt_ref, dvt_ref,
             z_ref, ls_ref, hl_ref, f_ref, a_ref, g_ref, dz_ref, da_ref, l0_ref, dob_ref,
             cl_ref, cg_ref, dl_ref, dqa_ref):
        i = pl.program_id(1)
        q0 = i * tq

        @pl.when(i == 0)
        def _():
            dkt_ref[...] = jnp.zeros_like(dkt_ref)
            dvt_ref[...] = jnp.zeros_like(dvt_ref)

        cl_ref[...] = jnp.zeros_like(cl_ref)
        cg_ref[...] = jnp.zeros_like(cg_ref)
        dqa_ref[...] = jnp.zeros_like(dqa_ref)
        dob = do_ref[...].astype(dob_ref.dtype)
        dob_ref[...] = dob
        dl_ref[...] = jnp.sum(dob.astype(F32) * o_ref[...], axis=1, keepdims=True)
        u_excl = _sb_tri(bk, False)
        u_incl = _sb_tri(bk, True)

        def iteration(kb0, masked):
            k0s = [pl.multiple_of(kb0 + (nt - 1 - n) * bk, bk) for n in range(nt)]
            for n in range(nt):
                z_ref[n] = jnp.dot(q_ref[...], kt_ref[:, pl.ds(k0s[n], bk)], preferred_element_type=F32)
                da_ref[n] = jnp.dot(dob_ref[...], vt_ref[:, pl.ds(k0s[n], bk)], preferred_element_type=F32)
            for n in range(nt):
                _sb_logits_phase(z_ref, ls_ref, hl_ref, l0_ref, n, tq, bk, sr, q0, k0s[n], masked)
            for n in range(nt):
                f_ref[n] = jnp.dot(hl_ref[n], u_excl, preferred_element_type=F32)
            for t in range(tq // sr):
                rows = slice(t * sr, (t + 1) * sr)
                c = cl_ref[rows, :]
                for n in range(nt):
                    f = f_ref[n, rows, :]
                    a = jnp.exp2(ls_ref[n, rows, :] + f + c)
                    if masked:
                        a = jnp.where(_sb_valid(t, sr, bk, q0, k0s[n]), a, 0.0)
                    ab = a.astype(a_ref.dtype)
                    a_ref[n, rows, :] = ab
                    g = da_ref[n, rows, :] * ab.astype(F32)
                    g_ref[n, rows, :] = g
                    _sb_split(hl_ref, n, rows, bk, g)
                    c = c + f[:, 0:1] + l0_ref[n, rows, :]
                cl_ref[rows, :] = c
            for n in range(nt):
                f_ref[n] = jnp.dot(hl_ref[n], u_incl, preferred_element_type=F32)
            for t in range(tq // sr):
                rows = slice(t * sr, (t + 1) * sr)
                cg = cg_ref[rows, :]
                for n in range(nt):
                    sg_tile = f_ref[n, rows, :]
                    p = dl_ref[rows, :] - (sg_tile + cg)
                    g = g_ref[n, rows, :]
                    dz = g - (g + p) * jnp.exp2(ls_ref[n, rows, :])
                    if masked:
                        dz = jnp.where(_sb_valid(t, sr, bk, q0, k0s[n]), dz, 0.0)
                    dz_ref[n, rows, :] = dz.astype(dz_ref.dtype)
                    cg = cg + sg_tile[:, 0:1]
                cg_ref[rows, :] = cg
            for n in range(nt):
                cols = pl.ds(k0s[n], bk)
                dqa_ref[...] += jnp.dot(dz_ref[n], k_ref[cols, :], preferred_element_type=F32)
                dkt_ref[:, cols] += jnp.dot(qt_ref[...], dz_ref[n], preferred_element_type=F32)
                dvt_ref[:, cols] += jnp.dot(dot_ref[...], a_ref[n], preferred_element_type=F32)

        def below(jj, c):
            iteration((i - 1 - jj) * tq, False)
            return c

        iteration(q0, True)
        lax.fori_loop(0, i, below, 0)
        dq_ref[...] = dqa_ref[...]

    blk_q = pl.BlockSpec((None, tq, dh), lambda hh, i: (hh, i, 0))
    blk_qt = pl.BlockSpec((None, dh, tq), lambda hh, i: (hh, 0, i))
    blk_s = pl.BlockSpec((None, s, dh), lambda hh, i: (hh, 0, 0))
    blk_st = pl.BlockSpec((None, dh, s), lambda hh, i: (hh, 0, 0))
    mx = q.dtype
    return _pcall(
        body, name="sb_bwd",
        out_shape=(jax.ShapeDtypeStruct((h, s, dh), F32), jax.ShapeDtypeStruct((h, dh, s), F32),
                   jax.ShapeDtypeStruct((h, dh, s), F32)),
        grid=(h, s // tq),
        in_specs=[blk_q, blk_qt, blk_s, blk_st, blk_st, blk_q, blk_q, blk_qt],
        out_specs=(blk_q, blk_st, blk_st),
        scratch_shapes=[pltpu.VMEM((nt, tq, bk), F32), pltpu.VMEM((nt, tq, bk), F32),
                        pltpu.VMEM((nt, tq, 2 * bk), BF16), pltpu.VMEM((nt, tq, bk), F32),
                        pltpu.VMEM((nt, tq, bk), mx), pltpu.VMEM((nt, tq, bk), F32),
                        pltpu.VMEM((nt, tq, bk), mx), pltpu.VMEM((nt, tq, bk), F32),
                        pltpu.VMEM((nt, tq, 1), F32), pltpu.VMEM((tq, dh), mx),
                        pltpu.VMEM((tq, 1), F32), pltpu.VMEM((tq, 1), F32), pltpu.VMEM((tq, 1), F32),
                        pltpu.VMEM((tq, dh), F32)],
        compiler_params=_params("parallel", "arbitrary"),
    )(q, qt, k, kt, vt, o, do, dot)


def _swap(t):
    return t.transpose(0, 2, 1)


def _sb_scale(dh):
    assert math.log2(dh) % 2 == 0, dh
    return dh ** -0.5


@jax.custom_vjp
def _sb_core(q, k, v):
    return _sb_core_fwd(q, k, v)[0]


def _sb_core_fwd(q, k, v):
    scale = _sb_scale(q.shape[-1])
    qs, kb, vb = (q.astype(MXU_DTYPE) * scale).astype(MXU_DTYPE), k.astype(MXU_DTYPE), v.astype(MXU_DTYPE)
    o = _sb_fwd_call(qs, _swap(kb), vb)
    return o, (qs, kb, vb, o)


def _sb_core_bwd(res, do):
    qs, kb, vb, o = res
    ks = (kb * _sb_scale(kb.shape[-1])).astype(MXU_DTYPE)
    dq, dkt, dvt = _sb_bwd_call(qs, _swap(qs), ks, _swap(kb), _swap(vb), o, do, _swap(do.astype(MXU_DTYPE)))
    return dq, _swap(dkt), _swap(dvt)


_sb_core.defvjp(_sb_core_fwd, _sb_core_bwd)


def _row_block(s):
    return _pick(s, (512, 256, 128, 64, 32, 16, 8))


def _fold8(t):
    r, c = t.shape
    return jnp.sum(t.reshape(r // SUBLANES, SUBLANES, c), axis=0)


def _vec(a):
    return a.reshape(1, -1)


def _modulate(x, sc, sh, out_dtype, name):
    s, d = x.shape
    tr = _row_block(s)

    def body(x_ref, sc_ref, sh_ref, o_ref):
        o_ref[...] = (x_ref[...] * (1.0 + sc_ref[...]) + sh_ref[...]).astype(o_ref.dtype)

    row = pl.BlockSpec((tr, d), lambda i: (i, 0))
    vec = pl.BlockSpec((1, d), lambda i: (0, 0))
    return _pcall(body, name=name, out_shape=jax.ShapeDtypeStruct((s, d), out_dtype), grid=(s // tr,),
                  in_specs=[row, vec, vec], out_specs=row, compiler_params=_params("parallel"))(x, _vec(sc), _vec(sh))


def _modulate_bwd(dxa, dh, x, sc, name):
    s, d = x.shape
    tr = _row_block(s)

    def body(dxa_ref, dh_ref, x_ref, sc_ref, dx_ref, acc_ref):
        @pl.when(pl.program_id(0) == 0)
        def _():
            acc_ref[...] = jnp.zeros_like(acc_ref)

        dh = dh_ref[...]
        dx_ref[...] = dxa_ref[...] + dh * (1.0 + sc_ref[...])
        acc_ref[0] += _fold8(dh * x_ref[...])
        acc_ref[1] += _fold8(dh)

    row = pl.BlockSpec((tr, d), lambda i: (i, 0))
    vec = pl.BlockSpec((1, d), lambda i: (0, 0))
    dx, acc = _pcall(
        body, name=name,
        out_shape=(jax.ShapeDtypeStruct((s, d), F32), jax.ShapeDtypeStruct((2, SUBLANES, d), F32)),
        grid=(s // tr,), in_specs=[row, row, row, vec],
        out_specs=(row, pl.BlockSpec((2, SUBLANES, d), lambda i: (0, 0, 0))),
        compiler_params=_params("arbitrary"))(dxa, dh, x, _vec(sc))
    acc = jnp.sum(acc, axis=1)
    return dx, acc[0], acc[1]


def _resid_ln(x, y, g, gamma, beta, name):
    s, d = x.shape
    tr = _row_block(s)

    def body(x_ref, y_ref, g_ref, gam_ref, bet_ref, o_ref):
        u = DN_ALPHA * x_ref[...] + (1.0 + g_ref[...]) * y_ref[...]
        uc = u - jnp.mean(u, axis=-1, keepdims=True)
        var = jnp.mean(uc * uc, axis=-1, keepdims=True)
        o_ref[...] = uc * lax.rsqrt(var + LN_EPS) * gam_ref[...] + bet_ref[...]

    row = pl.BlockSpec((tr, d), lambda i: (i, 0))
    vec = pl.BlockSpec((1, d), lambda i: (0, 0))
    return _pcall(body, name=name, out_shape=jax.ShapeDtypeStruct((s, d), F32), grid=(s // tr,),
                  in_specs=[row, row, vec, vec, vec], out_specs=row,
                  compiler_params=_params("parallel"))(x, y, _vec(g), _vec(gamma), _vec(beta))


def _resid_ln_bwd(x, y, g, gamma, dout, name):
    s, d = x.shape
    tr = _row_block(s)

    def body(x_ref, y_ref, g_ref, gam_ref, do_ref, dxa_ref, dy_ref, acc_ref):
        @pl.when(pl.program_id(0) == 0)
        def _():
            acc_ref[...] = jnp.zeros_like(acc_ref)

        y = y_ref[...]
        gg = 1.0 + g_ref[...]
        u = DN_ALPHA * x_ref[...] + gg * y
        uc = u - jnp.mean(u, axis=-1, keepdims=True)
        rstd = lax.rsqrt(jnp.mean(uc * uc, axis=-1, keepdims=True) + LN_EPS)
        xhat = uc * rstd
        dout = do_ref[...]
        dxh = dout * gam_ref[...]
        du = rstd * (dxh - jnp.mean(dxh, axis=-1, keepdims=True)
                     - xhat * jnp.mean(dxh * xhat, axis=-1, keepdims=True))
        dxa_ref[...] = DN_ALPHA * du
        dy_ref[...] = gg * du
        acc_ref[0] += _fold8(dout * xhat)
        acc_ref[1] += _fold8(dout)
        acc_ref[2] += _fold8(du * y)

    row = pl.BlockSpec((tr, d), lambda i: (i, 0))
    vec = pl.BlockSpec((1, d), lambda i: (0, 0))
    dxa, dy, acc = _pcall(
        body, name=name,
        out_shape=(jax.ShapeDtypeStruct((s, d), F32), jax.ShapeDtypeStruct((s, d), F32),
                   jax.ShapeDtypeStruct((3, SUBLANES, d), F32)),
        grid=(s // tr,), in_specs=[row, row, vec, vec, row],
        out_specs=(row, row, pl.BlockSpec((3, SUBLANES, d), lambda i: (0, 0, 0))),
        compiler_params=_params("arbitrary"))(x, y, _vec(g), _vec(gamma), dout)
    acc = jnp.sum(acc, axis=1)
    return dxa, dy, acc[0], acc[1], acc[2]


def _loss_head(x, target, name):
    s, d = x.shape
    tr = _row_block(s)

    def body(x_ref, t_ref, dx_ref, acc_ref):
        @pl.when(pl.program_id(0) == 0)
        def _():
            acc_ref[...] = jnp.zeros_like(acc_ref)

        e = x_ref[...] - t_ref[...]
        dx_ref[...] = e * (1.0 / d)
        acc_ref[...] += _fold8(e * e)

    row = pl.BlockSpec((tr, d), lambda i: (i, 0))
    dx, acc = _pcall(
        body, name=name,
        out_shape=(jax.ShapeDtypeStruct((s, d), F32), jax.ShapeDtypeStruct((SUBLANES, d), F32)),
        grid=(s // tr,), in_specs=[row, row],
        out_specs=(row, pl.BlockSpec((SUBLANES, d), lambda i: (0, 0))),
        compiler_params=_params("arbitrary"))(x, target)
    return (0.5 / d) * jnp.sum(acc), dx


CONV_STRIPE = 128
CONV_ROWS = 256


def _shift_down(cur, halo, k):
    ext = jnp.concatenate([halo, cur], axis=0)
    return pltpu.roll(ext, k, 0)[SUBLANES:]


def _shift_up(cur, halo, k):
    ext = jnp.concatenate([cur, halo], axis=0)
    n = ext.shape[0]
    return pltpu.roll(ext, n - k, 0)[:n - SUBLANES]


def _gate_chunk(g_ref, r, rc):
    r0 = pl.multiple_of(r * rc, rc)
    cur = g_ref[pl.ds(r0, rc), :]
    hs = pl.multiple_of(jnp.maximum(r0 - SUBLANES, 0), SUBLANES)
    halo = jnp.where(r > 0, g_ref[pl.ds(hs, SUBLANES), :], 0.0)
    return r0, cur, _shift_down(cur, halo, 1), _shift_down(cur, halo, 2)


def _ffn_gate(gu, cw, cb, name):
    s, f2 = gu.shape
    f = f2 // 2
    tc = _pick(f, (CONV_STRIPE,))
    rc = _pick(s, (CONV_ROWS, 128, 64, 32, 16, 8))
    nj = f // tc

    def body(g_ref, u_ref, cw_ref, cb_ref, a_ref):
        w0, w1, w2, b = cw_ref[0:1, :], cw_ref[1:2, :], cw_ref[2:3, :], cb_ref[...]

        def chunk(r, c):
            r0, cur, x1, x2 = _gate_chunk(g_ref, r, rc)
            gc = w2 * cur + w1 * x1 + w0 * x2 + b
            a_ref[pl.ds(r0, rc), :] = (gc * jax.nn.sigmoid(gc) * u_ref[pl.ds(r0, rc), :]).astype(a_ref.dtype)
            return c

        lax.fori_loop(0, s // rc, chunk, 0)

    return _pcall(
        body, name=name, out_shape=jax.ShapeDtypeStruct((s, f), MXU_DTYPE), grid=(nj,),
        in_specs=[pl.BlockSpec((s, tc), lambda j: (0, j)), pl.BlockSpec((s, tc), lambda j: (0, j + nj)),
                  pl.BlockSpec((3, tc), lambda j: (0, j)), pl.BlockSpec((1, tc), lambda j: (0, j))],
        out_specs=pl.BlockSpec((s, tc), lambda j: (0, j)),
        compiler_params=_params("parallel"))(gu, gu, cw, _vec(cb))


def _ffn_gate_bwd(da, gu, cw, cb, name):
    s, f2 = gu.shape
    f = f2 // 2
    tc = _pick(f, (CONV_STRIPE,))
    rc = _pick(s, (CONV_ROWS, 128, 64, 32, 16, 8))
    nj = f // tc
    nr = s // rc

    def body(da_ref, g_ref, u_ref, cw_ref, cb_ref, dg_ref, du_ref, acc_ref, dgc_ref):
        w0, w1, w2, b = cw_ref[0:1, :], cw_ref[1:2, :], cw_ref[2:3, :], cb_ref[...]

        def chunk1(r, carry):
            a0, a1, a2, ab = carry
            r0, cur, x1, x2 = _gate_chunk(g_ref, r, rc)
            gc = w2 * cur + w1 * x1 + w0 * x2 + b
            sg = jax.nn.sigmoid(gc)
            da_c = da_ref[pl.ds(r0, rc), :]
            du_ref[pl.ds(r0, rc), :] = (da_c * (gc * sg)).astype(du_ref.dtype)
            dgc = da_c * u_ref[pl.ds(r0, rc), :] * (sg * (1.0 + gc * (1.0 - sg)))
            dgc_ref[pl.ds(r0, rc), :] = dgc
            return a0 + _fold8(dgc * x2), a1 + _fold8(dgc * x1), a2 + _fold8(dgc * cur), ab + _fold8(dgc)

        zero = jnp.zeros((SUBLANES, tc), F32)
        a0, a1, a2, ab = lax.fori_loop(0, nr, chunk1, (zero, zero, zero, zero))
        acc_ref[0], acc_ref[1], acc_ref[2], acc_ref[3] = a0, a1, a2, ab

        def chunk2(r, c):
            r0 = pl.multiple_of(r * rc, rc)
            cur = dgc_ref[pl.ds(r0, rc), :]
            hs = pl.multiple_of(jnp.minimum(r0 + rc, s - SUBLANES), SUBLANES)
            halo = jnp.where(r < nr - 1, dgc_ref[pl.ds(hs, SUBLANES), :], 0.0)
            dg = w2 * cur + w1 * _shift_up(cur, halo, 1) + w0 * _shift_up(cur, halo, 2)
            dg_ref[pl.ds(r0, rc), :] = dg.astype(dg_ref.dtype)
            return c

        lax.fori_loop(0, nr, chunk2, 0)

    stripe = pl.BlockSpec((s, tc), lambda j: (0, j))
    dg, du, acc = _pcall(
        body, name=name,
        out_shape=(jax.ShapeDtypeStruct((s, f), MXU_DTYPE), jax.ShapeDtypeStruct((s, f), MXU_DTYPE),
                   jax.ShapeDtypeStruct((4, SUBLANES, f), F32)),
        grid=(nj,),
        in_specs=[stripe, stripe, pl.BlockSpec((s, tc), lambda j: (0, j + nj)),
                  pl.BlockSpec((3, tc), lambda j: (0, j)), pl.BlockSpec((1, tc), lambda j: (0, j))],
        out_specs=(stripe, stripe, pl.BlockSpec((4, SUBLANES, tc), lambda j: (0, 0, j))),
        scratch_shapes=[pltpu.VMEM((s, tc), F32)],
        compiler_params=_params("parallel"))(da, gu, gu, cw, _vec(cb))
    acc = jnp.sum(acc, axis=1)
    return dg, du, acc[:3], acc[3]


def _add_rows(a, b, out_dtype, name):
    r = a.shape[0]
    tr = _pick(r, (1024, 512, 256, 128, 64, 32, 16, 8))

    def body(a_ref, b_ref, o_ref):
        o_ref[...] = (a_ref[...] + b_ref[...]).astype(o_ref.dtype)

    row = pl.BlockSpec((tr, LANES), lambda i: (i, 0))
    return _pcall(body, name=name, out_shape=jax.ShapeDtypeStruct(a.shape, out_dtype), grid=(r // tr,),
                  in_specs=[row, row], out_specs=row, compiler_params=_params("parallel"))(a, b)


def _adamw(gparts, w, m, v, name):
    r = w.shape[0]
    tr = _pick(r, (1024, 512, 256, 128, 64, 32, 16, 8))
    bc1 = 1.0 / (1.0 - ADAM_B1 ** ADAM_STEP)
    bc2 = 1.0 / (1.0 - ADAM_B2 ** ADAM_STEP)
    n = len(gparts)

    def body(*refs):
        w_ref, m_ref, v_ref, go_ref, d_ref, mo_ref, vo_ref = refs[n:]
        g = refs[0][...].astype(F32)
        for t in range(1, n):
            g = g + refs[t][...].astype(F32)
        mn = ADAM_B1 * m_ref[...] + (1.0 - ADAM_B1) * g
        vn = ADAM_B2 * v_ref[...] + (1.0 - ADAM_B2) * (g * g)
        m_hat = mn * bc1
        v_hat = vn * bc2
        go_ref[...] = g
        d_ref[...] = -ADAM_LR * (m_hat / (jnp.sqrt(v_hat) + ADAM_EPS) + ADAM_WD * w_ref[...])
        mo_ref[...] = mn
        vo_ref[...] = vn

    row = pl.BlockSpec((tr, LANES), lambda i: (i, 0))
    sds = jax.ShapeDtypeStruct((r, LANES), F32)
    return _pcall(
        body, name=name, out_shape=(sds, sds, sds, sds),
        grid=(r // tr,),
        in_specs=[pl.BlockSpec((None, tr, LANES), functools.partial(lambda slot, i: (slot, i, 0), slot))
                  for _, slot in gparts] + [row, row, row],
        out_specs=(row, row, row, row),
        compiler_params=_params("parallel"),
    )(*[a for a, _ in gparts], w, m, v)


def _sum_slots(gslots, name):
    n, r, _ = gslots.shape
    tr = _pick(r, (1024, 512, 256, 128, 64, 32, 16, 8))

    def body(g_ref, o_ref):
        g = g_ref[0]
        for t in range(1, n):
            g = g + g_ref[t]
        o_ref[...] = g

    return _pcall(
        body, name=name, out_shape=jax.ShapeDtypeStruct((r, LANES), F32),
        grid=(r // tr,),
        in_specs=[pl.BlockSpec((n, tr, LANES), lambda i: (0, i, 0))],
        out_specs=pl.BlockSpec((tr, LANES), lambda i: (i, 0)),
        compiler_params=_params("parallel"),
    )(gslots)


def _l2norm(x, eps=1e-6):
    return x * lax.rsqrt(jnp.sum(x * x, axis=-1, keepdims=True) + eps)


def _chunk_heads(t, n_heads, chunk):
    s, hd = t.shape
    return t.reshape(s // chunk, chunk, n_heads, hd // n_heads).transpose(2, 0, 1, 3)


def _unchunk_heads(t):
    h, n, c, d = t.shape
    return t.transpose(1, 2, 0, 3).reshape(n * c, h, d)


_NT = (((1,), (1,)), ((), ()))
_TN = (((0,), (0,)), ((), ()))


def _gdn_blocks(a, rev_from=None):
    h, _, r, c = a.shape
    if rev_from is None:
        return pl.BlockSpec((h, None, r, c), lambda n: (0, n, 0, 0))
    return pl.BlockSpec((h, None, r, c), lambda n: (0, rev_from - n, 0, 0))


def _gdn_scan_fwd_call(qg, w, u, qk, kd, e):
    H, n_chunks, c, dk = qg.shape
    dv = u.shape[-1]

    def body(qg_ref, w_ref, u_ref, qk_ref, kd_ref, e_ref, o_ref, sin_ref, vn_ref, state_ref):
        @pl.when(pl.program_id(0) == 0)
        def _():
            state_ref[...] = jnp.zeros_like(state_ref)

        for hd in range(H):
            st = state_ref[hd]
            stb = st.astype(MXU_DTYPE)
            sin_ref[hd] = st
            v_new = u_ref[hd] - jnp.dot(w_ref[hd].astype(MXU_DTYPE), stb, preferred_element_type=F32)
            vn_ref[hd] = v_new
            vnb = v_new.astype(MXU_DTYPE)
            o_ref[hd] = (jnp.dot(qg_ref[hd].astype(MXU_DTYPE), stb, preferred_element_type=F32)
                         + jnp.dot(qk_ref[hd].astype(MXU_DTYPE), vnb, preferred_element_type=F32))
            state_ref[hd] = st * e_ref[hd] + lax.dot_general(kd_ref[hd].astype(MXU_DTYPE), vnb, _TN,
                                                             preferred_element_type=F32)

    ins = (qg, w, u, qk, kd, e)
    outs = (jax.ShapeDtypeStruct((H, n_chunks, c, dv), F32), jax.ShapeDtypeStruct((H, n_chunks, dk, dv), F32),
            jax.ShapeDtypeStruct((H, n_chunks, c, dv), F32))
    return _pcall(
        body, name="gdn_scan_fwd", out_shape=outs, grid=(n_chunks,),
        in_specs=[_gdn_blocks(a) for a in ins], out_specs=tuple(_gdn_blocks(a) for a in outs),
        scratch_shapes=[pltpu.VMEM((H, dk, dv), F32)],
        compiler_params=_params("arbitrary"),
    )(*ins)


def _gdn_scan_bwd_call(qg, w, qk, kd, e, s_in, v_new, do):
    H, n_chunks, c, dk = qg.shape
    dv = v_new.shape[-1]

    def body(qg_ref, w_ref, qk_ref, kd_ref, e_ref, sin_ref, vn_ref, do_ref,
             dqg_ref, dw_ref, du_ref, dqk_ref, dkd_ref, de_ref, ds_ref):
        @pl.when(pl.program_id(0) == 0)
        def _():
            ds_ref[...] = jnp.zeros_like(ds_ref)

        for hd in range(H):
            st = sin_ref[hd]
            stb = st.astype(MXU_DTYPE)
            vnb = vn_ref[hd].astype(MXU_DTYPE)
            dob = do_ref[hd].astype(MXU_DTYPE)
            ds = ds_ref[hd]
            dsb = ds.astype(MXU_DTYPE)
            dvn = lax.dot_general(qk_ref[hd].astype(MXU_DTYPE), dob, _TN, preferred_element_type=F32)
            dqk_ref[hd] = lax.dot_general(dob, vnb, _NT, preferred_element_type=F32)
            dqg_ref[hd] = lax.dot_general(dob, stb, _NT, preferred_element_type=F32)
            ds_in = lax.dot_general(qg_ref[hd].astype(MXU_DTYPE), dob, _TN, preferred_element_type=F32)
            dvn = dvn + jnp.dot(kd_ref[hd].astype(MXU_DTYPE), dsb, preferred_element_type=F32)
            dkd_ref[hd] = lax.dot_general(vnb, dsb, _NT, preferred_element_type=F32)
            de_ref[hd] = _fold8(st * ds)
            ds_in = ds_in + ds * e_ref[hd]
            du_ref[hd] = dvn
            dvnb = dvn.astype(MXU_DTYPE)
            dw_ref[hd] = -lax.dot_general(dvnb, stb, _NT, preferred_element_type=F32)
            ds_ref[hd] = ds_in - lax.dot_general(w_ref[hd].astype(MXU_DTYPE), dvnb, _TN,
                                                 preferred_element_type=F32)

    last = n_chunks - 1
    ins = (qg, w, qk, kd, e, s_in, v_new, do)
    outs = (jax.ShapeDtypeStruct(qg.shape, F32), jax.ShapeDtypeStruct(w.shape, F32),
            jax.ShapeDtypeStruct(v_new.shape, F32), jax.ShapeDtypeStruct(qk.shape, F32),
            jax.ShapeDtypeStruct(kd.shape, F32), jax.ShapeDtypeStruct((H, n_chunks, SUBLANES, dv), F32))
    return _pcall(
        body, name="gdn_scan_bwd", out_shape=outs, grid=(n_chunks,),
        in_specs=[_gdn_blocks(a, last) for a in ins], out_specs=tuple(_gdn_blocks(a, last) for a in outs),
        scratch_shapes=[pltpu.VMEM((H, dk, dv), F32)],
        compiler_params=_params("arbitrary"),
    )(*ins)


@jax.custom_vjp
def _gdn_scan(qg, w, u, qk, kd, e):
    return _gdn_scan_fwd_call(qg, w, u, qk, kd, e)[0]


def _gdn_scan_fwd(qg, w, u, qk, kd, e):
    o, s_in, v_new = _gdn_scan_fwd_call(qg, w, u, qk, kd, e)
    return o, (qg, w, qk, kd, e, s_in, v_new)


def _gdn_scan_bwd(res, do):
    dqg, dw, du, dqk, dkd, de = _gdn_scan_bwd_call(*res, do)
    return dqg, dw, du, dqk, dkd, jnp.sum(de, axis=2, keepdims=True)


_gdn_scan.defvjp(_gdn_scan_fwd, _gdn_scan_bwd)


def _conv_taps(x_ref, r, rc, taps):
    r0 = pl.multiple_of(r * rc, rc)
    cur = x_ref[pl.ds(r0, rc), :]
    hs = pl.multiple_of(jnp.maximum(r0 - SUBLANES, 0), SUBLANES)
    halo = jnp.where(r > 0, x_ref[pl.ds(hs, SUBLANES), :], 0.0)
    return r0, [cur] + [_shift_down(cur, halo, k) for k in range(1, taps)]


def _conv_pre(w_ref, xs):
    taps = len(xs)
    gc = w_ref[taps - 1:taps, :] * xs[0]
    for j in range(taps - 1):
        gc = gc + w_ref[j:j + 1, :] * xs[taps - 1 - j]
    return gc


def _dwconv_silu_fwd_call(x, w):
    s, c = x.shape
    taps = w.shape[0]
    tc = _pick(c, (CONV_STRIPE,))
    rc = _pick(s, (CONV_ROWS, 128, 64, 32, 16, 8))

    def body(x_ref, w_ref, o_ref):
        def chunk(r, carry):
            r0, xs = _conv_taps(x_ref, r, rc, taps)
            gc = _conv_pre(w_ref, xs)
            o_ref[pl.ds(r0, rc), :] = gc * jax.nn.sigmoid(gc)
            return carry

        lax.fori_loop(0, s // rc, chunk, 0)

    stripe = pl.BlockSpec((s, tc), lambda j: (0, j))
    return _pcall(body, name="gdn_conv_fwd", out_shape=jax.ShapeDtypeStruct((s, c), F32), grid=(c // tc,),
                  in_specs=[stripe, pl.BlockSpec((taps, tc), lambda j: (0, j))], out_specs=stripe,
                  compiler_params=_params("parallel"))(x, w)


def _dwconv_silu_bwd_call(x, w, dy):
    s, c = x.shape
    taps = w.shape[0]
    tc = _pick(c, (CONV_STRIPE,))
    rc = _pick(s, (CONV_ROWS, 128, 64, 32, 16, 8))
    nr = s // rc

    def body(x_ref, w_ref, dy_ref, dx_ref, acc_ref, dgc_ref):
        def chunk1(r, acc):
            r0, xs = _conv_taps(x_ref, r, rc, taps)
            gc = _conv_pre(w_ref, xs)
            sg = jax.nn.sigmoid(gc)
            dgc = dy_ref[pl.ds(r0, rc), :] * (sg * (1.0 + gc * (1.0 - sg)))
            dgc_ref[pl.ds(r0, rc), :] = dgc
            return tuple(acc[j] + _fold8(dgc * xs[taps - 1 - j]) for j in range(taps))

        zero = jnp.zeros((SUBLANES, tc), F32)
        acc = lax.fori_loop(0, nr, chunk1, (zero,) * taps)
        for j in range(taps):
            acc_ref[j] = acc[j]

        def chunk2(r, carry):
            r0 = pl.multiple_of(r * rc, rc)
            cur = dgc_ref[pl.ds(r0, rc), :]
            hs = pl.multiple_of(jnp.minimum(r0 + rc, s - SUBLANES), SUBLANES)
            halo = jnp.where(r < nr - 1, dgc_ref[pl.ds(hs, SUBLANES), :], 0.0)
            dx = w_ref[taps - 1:taps, :] * cur
            for j in range(taps - 1):
                dx = dx + w_ref[j:j + 1, :] * _shift_up(cur, halo, taps - 1 - j)
            dx_ref[pl.ds(r0, rc), :] = dx
            return carry

        lax.fori_loop(0, nr, chunk2, 0)

    stripe = pl.BlockSpec((s, tc), lambda j: (0, j))
    dx, acc = _pcall(
        body, name="gdn_conv_bwd",
        out_shape=(jax.ShapeDtypeStruct((s, c), F32), jax.ShapeDtypeStruct((taps, SUBLANES, c), F32)),
        grid=(c // tc,),
        in_specs=[stripe, pl.BlockSpec((taps, tc), lambda j: (0, j)), stripe],
        out_specs=(stripe, pl.BlockSpec((taps, SUBLANES, tc), lambda j: (0, 0, j))),
        scratch_shapes=[pltpu.VMEM((s, tc), F32)],
        compiler_params=_params("parallel"))(x, w, dy)
    return dx, jnp.sum(acc, axis=1)


@jax.custom_vjp
def _dwconv_silu(x, w):
    return _dwconv_silu_fwd_call(x, w)


def _dwconv_silu_fwd(x, w):
    return _dwconv_silu_fwd_call(x, w), (x, w)


def _dwconv_silu_bwd(res, dy):
    return _dwconv_silu_bwd_call(*res, dy)


_dwconv_silu.defvjp(_dwconv_silu_fwd, _dwconv_silu_bwd)


def _gated_deltanet(hx, p):
    H, C = GDN_HEADS, GDN_CHUNK
    s, d = hx.shape
    dk = dv = d // H
    qkvz = _linear(hx, p['gdn_w_qkvz'], "gdn_in")
    ab = _linear(hx, p['gdn_w_ab'], "gdn_ab")
    qkv, z = qkvz[:, :3 * d], qkvz[:, 3 * d:]
    a, bt = ab[:, :H], ab[:, H:2 * H]
    qkv = _dwconv_silu(qkv, p['gdn_conv_w'])
    q, k, v = qkv[:, :d], qkv[:, d:2 * d], qkv[:, 2 * d:]
    q = _l2norm(_chunk_heads(q, H, C)) * (dk ** -0.5)
    k = _l2norm(_chunk_heads(k, H, C))
    v = _chunk_heads(v, H, C)
    beta = jax.nn.sigmoid(_chunk_heads(bt, H, C)[..., 0])
    g = -jnp.exp(p['gdn_a_log'])[:, None, None] * jax.nn.softplus(
        _chunk_heads(a, H, C)[..., 0] + p['gdn_dt_bias'][:, None, None])
    gc = jnp.cumsum(g, axis=-1)
    idx = jnp.arange(C)
    causal = idx[:, None] >= idx[None, :]
    strict = idx[:, None] > idx[None, :]
    diff = gc[..., :, None] - gc[..., None, :]
    decay = jnp.where(causal, jnp.exp(jnp.where(causal, diff, 0.0)), 0.0)
    kb = k * beta[..., None]
    kk = jnp.where(strict, jnp.einsum('hncd,hnmd->hncm', kb, k) * decay, 0.0)
    eye = jnp.eye(C, dtype=F32)
    rhs = jnp.concatenate([v * beta[..., None], kb * jnp.exp(gc)[..., None]], axis=-1)
    sol = lax.linalg.triangular_solve(kk + eye, rhs, left_side=True, lower=True, unit_diagonal=True)
    u, w = sol[..., :dv], sol[..., dv:]
    qk = jnp.where(causal, jnp.einsum('hncd,hnmd->hncm', q, k) * decay, 0.0)

    g_last = gc[..., -1:]
    e = jnp.broadcast_to(jnp.exp(g_last)[..., None], gc.shape[:2] + (1, dv))
    o = _gdn_scan(q * jnp.exp(gc)[..., None], w, u, qk, k * jnp.exp(g_last - gc)[..., None], e)
    o = _unchunk_heads(o)
    o = o * lax.rsqrt(jnp.mean(o * o, axis=-1, keepdims=True) + 1e-6) * p['gdn_norm_w']
    o = o * jax.nn.silu(z.reshape(s, H, dv))
    return _linear(o.reshape(s, H * dv), p['gdn_w_out'], "gdn_out")


def _ret_consts(c):
    log_gamma = jnp.log(1.0 - jnp.power(2.0, -5.0 - jnp.arange(RET_HEADS, dtype=F32)))
    idx = jnp.arange(c, dtype=F32)
    rel = idx[:, None] - idx[None, :]
    dmask = jnp.where(rel >= 0, jnp.exp(jnp.maximum(rel, 0.0) * log_gamma[:, None, None]), 0.0)
    zeta = jnp.exp((c - 1.0 - idx)[None, :] * log_gamma[:, None])[..., None]
    xi = jnp.exp((idx + 1.0)[None, :] * log_gamma[:, None])[..., None]
    gamma_c = jnp.exp(c * log_gamma)[:, None, None]
    return dmask, zeta, xi, gamma_c


def _ret_angles(s, dk):
    pos = jnp.arange(s, dtype=F32)
    inv_freq = RET_ROPE_BASE ** (-jnp.linspace(0.0, 1.0, dk // 2, dtype=F32))
    ang = pos[:, None] * inv_freq[None, :]
    return jnp.cos(ang), jnp.sin(ang)


def _rot(t, cs, sn):
    half = t.shape[1] // 2
    t1, t2 = t[:, :half], t[:, half:]
    return jnp.concatenate([t1 * cs - t2 * sn, t1 * sn + t2 * cs], axis=1)


def _rot_t(t, cs, sn):
    half = t.shape[1] // 2
    t1, t2 = t[:, :half], t[:, half:]
    return jnp.concatenate([t1 * cs + t2 * sn, t2 * cs - t1 * sn], axis=1)


def _ret_cols(d, dk, dv, hd):
    return (slice(hd * dk, (hd + 1) * dk), slice(d + hd * dk, d + (hd + 1) * dk),
            slice(2 * d + hd * dv, 2 * d + (hd + 1) * dv), slice(4 * d + hd * dv, 4 * d + (hd + 1) * dv))


def _ret_fwd_call(proj):
    s, d6 = proj.shape
    d = d6 // 6
    H, c = RET_HEADS, RET_CHUNK
    dk, dv = d // H, 2 * d // H
    n_chunks = s // c
    kscale = dk ** -0.5
    cos_a, sin_a = _ret_angles(s, dk)
    consts = _ret_consts(c)

    def body(p_ref, cos_ref, sin_ref, dm_ref, ze_ref, xi_ref, gc_ref, out_ref, oraw_ref, st_ref, state_ref):
        @pl.when(pl.program_id(0) == 0)
        def _():
            state_ref[...] = jnp.zeros_like(state_ref)

        cs, sn = cos_ref[...], sin_ref[...]
        for hd in range(H):
            qc, kc, vc, gcol = _ret_cols(d, dk, dv, hd)
            ocol = slice(hd * dv, (hd + 1) * dv)
            qb = _rot(p_ref[:, qc], cs, sn).astype(MXU_DTYPE)
            kr = _rot(p_ref[:, kc], cs, sn) * kscale
            kb = kr.astype(MXU_DTYPE)
            vb = p_ref[:, vc].astype(MXU_DTYPE)
            st = state_ref[hd]
            stb = st.astype(MXU_DTYPE)
            st_ref[hd] = stb
            sc = lax.dot_general(qb, kb, _NT, preferred_element_type=F32) * dm_ref[hd]
            o = (jnp.dot(sc.astype(MXU_DTYPE), vb, preferred_element_type=F32)
                 + jnp.dot(qb, stb, preferred_element_type=F32) * xi_ref[hd])
            state_ref[hd] = st * gc_ref[hd] + lax.dot_general((kr * ze_ref[hd]).astype(MXU_DTYPE), vb, _TN,
                                                              preferred_element_type=F32)
            oraw_ref[:, ocol] = o
            oc = o - jnp.mean(o, axis=-1, keepdims=True)
            on = oc * lax.rsqrt(jnp.mean(oc * oc, axis=-1, keepdims=True) + 1e-6)
            gate = p_ref[:, gcol]
            out_ref[:, ocol] = on * (gate * jax.nn.sigmoid(gate))

    row = lambda width: pl.BlockSpec((c, width), lambda n: (n, 0))
    whole = lambda a: pl.BlockSpec(a.shape, lambda n: (0,) * a.ndim)
    return _pcall(
        body, name="ret_fwd",
        out_shape=(jax.ShapeDtypeStruct((s, 2 * d), F32), jax.ShapeDtypeStruct((s, 2 * d), F32),
                   jax.ShapeDtypeStruct((n_chunks, H, dk, dv), MXU_DTYPE)),
        grid=(n_chunks,),
        in_specs=[row(d6), row(dk // 2), row(dk // 2)] + [whole(a) for a in consts],
        out_specs=(row(2 * d), row(2 * d), pl.BlockSpec((None, H, dk, dv), lambda n: (n, 0, 0, 0))),
        scratch_shapes=[pltpu.VMEM((H, dk, dv), F32)],
        compiler_params=_params("arbitrary"),
    )(proj, cos_a, sin_a, *consts)


def _ret_bwd_call(proj, oraw, states, dout):
    s, d6 = proj.shape
    d = d6 // 6
    H, c = RET_HEADS, RET_CHUNK
    dk, dv = d // H, 2 * d // H
    n_chunks = s // c
    kscale = dk ** -0.5
    cos_a, sin_a = _ret_angles(s, dk)
    consts = _ret_consts(c)

    def body(p_ref, cos_ref, sin_ref, dm_ref, ze_ref, xi_ref, gc_ref, oraw_ref, st_ref, do_ref, dp_ref, ds_ref):
        @pl.when(pl.program_id(0) == 0)
        def _():
            ds_ref[...] = jnp.zeros_like(ds_ref)

        cs, sn = cos_ref[...], sin_ref[...]
        for hd in range(H):
            qc, kc, vc, gcol = _ret_cols(d, dk, dv, hd)
            ocol = slice(hd * dv, (hd + 1) * dv)
            qb = _rot(p_ref[:, qc], cs, sn).astype(MXU_DTYPE)
            kr = _rot(p_ref[:, kc], cs, sn) * kscale
            kb = kr.astype(MXU_DTYPE)
            vb = p_ref[:, vc].astype(MXU_DTYPE)
            gate = p_ref[:, gcol]
            o = oraw_ref[:, ocol]
            oc = o - jnp.mean(o, axis=-1, keepdims=True)
            rstd = lax.rsqrt(jnp.mean(oc * oc, axis=-1, keepdims=True) + 1e-6)
            on = oc * rstd
            dout_h = do_ref[:, ocol]
            sg = jax.nn.sigmoid(gate)
            dp_ref[:, gcol] = dout_h * on * (sg * (1.0 + gate * (1.0 - sg)))
            don = dout_h * (gate * sg)
            do_raw = rstd * (don - jnp.mean(don, axis=-1, keepdims=True)
                             - on * jnp.mean(don * on, axis=-1, keepdims=True))
            dob = do_raw.astype(MXU_DTYPE)
            stb = st_ref[hd]
            ds = ds_ref[hd]
            dsb = ds.astype(MXU_DTYPE)
            dm = dm_ref[hd]
            scb = (lax.dot_general(qb, kb, _NT, preferred_element_type=F32) * dm).astype(MXU_DTYPE)
            dsc = (lax.dot_general(dob, vb, _NT, preferred_element_type=F32) * dm).astype(MXU_DTYPE)
            dqr = jnp.dot(dsc, kb, preferred_element_type=F32)
            dkr = lax.dot_general(dsc, qb, _TN, preferred_element_type=F32)
            dvv = lax.dot_general(scb, dob, _TN, preferred_element_type=F32)
            doi = (do_raw * xi_ref[hd]).astype(MXU_DTYPE)
            dqr = dqr + lax.dot_general(doi, stb, _NT, preferred_element_type=F32)
            ds_in = lax.dot_general(qb, doi, _TN, preferred_element_type=F32)
            ze = ze_ref[hd]
            dkr = dkr + lax.dot_general(vb, dsb, _NT, preferred_element_type=F32) * ze
            dvv = dvv + jnp.dot((kr * ze).astype(MXU_DTYPE), dsb, preferred_element_type=F32)
            ds_ref[hd] = ds * gc_ref[hd] + ds_in
            dp_ref[:, qc] = _rot_t(dqr, cs, sn)
            dp_ref[:, kc] = _rot_t(dkr * kscale, cs, sn)
            dp_ref[:, vc] = dvv

    last = n_chunks - 1
    row = lambda width: pl.BlockSpec((c, width), lambda n: (last - n, 0))
    whole = lambda a: pl.BlockSpec(a.shape, lambda n: (0,) * a.ndim)
    return _pcall(
        body, name="ret_bwd", out_shape=jax.ShapeDtypeStruct((s, d6), F32),
        grid=(n_chunks,),
        in_specs=[row(d6), row(dk // 2), row(dk // 2)] + [whole(a) for a in consts]
                 + [row(2 * d), pl.BlockSpec((None, H, dk, dv), lambda n: (last - n, 0, 0, 0)), row(2 * d)],
        out_specs=row(d6),
        scratch_shapes=[pltpu.VMEM((H, dk, dv), F32)],
        compiler_params=_params("arbitrary"),
    )(proj, cos_a, sin_a, *consts, oraw, states, dout)


@jax.custom_vjp
def _ret_core(proj):
    return _ret_fwd_call(proj)[0]


def _ret_core_fwd(proj):
    out, oraw, states = _ret_fwd_call(proj)
    return out, (proj, oraw, states)


def _ret_core_bwd(res, dout):
    return (_ret_bwd_call(*res, dout),)


_ret_core.defvjp(_ret_core_fwd, _ret_core_bwd)


def _retention(hx, p):
    return _linear(_ret_core(_linear(hx, p['ret_w_in'], "ret_in")), p['ret_w_out'], "ret_out")


SQRT_HALF = 2.0 ** -0.5
INV_SQRT_2PI = (2.0 * math.pi) ** -0.5


def _gmlp_front(p_ref, g_ref, b_ref, w):
    x = p_ref[...]
    cdf = 0.5 * (1.0 + lax.erf(x * SQRT_HALF))
    uv = x * cdf
    u, v = uv[:, :w], uv[:, w:]
    vc = v - jnp.mean(v, axis=-1, keepdims=True)
    rstd = lax.rsqrt(jnp.mean(vc * vc, axis=-1, keepdims=True) + LN_EPS)
    vhat = vc * rstd
    return x, cdf, u, vhat, rstd, vhat * g_ref[...] + b_ref[...]


def _gmlp_fwd_call(proj, ln_g, ln_b, ws, bs):
    s, w2 = proj.shape
    w = w2 // 2
    c, G = GMLP_CHUNK, GMLP_GROUPS
    gw = w // G

    def body(p_ref, g_ref, b_ref, ws_ref, bs_ref, o_ref):
        _, _, u, _, _, vn = _gmlp_front(p_ref, g_ref, b_ref, w)
        for gi in range(G):
            cols = slice(gi * gw, (gi + 1) * gw)
            vs = jnp.dot(ws_ref[gi].astype(MXU_DTYPE), vn[:, cols].astype(MXU_DTYPE),
                         preferred_element_type=F32) + bs_ref[gi]
            o_ref[:, cols] = u[:, cols] * vs

    whole = lambda a: pl.BlockSpec(a.shape, lambda n: (0,) * a.ndim)
    args = (_vec(ln_g), _vec(ln_b), ws, bs)
    return _pcall(
        body, name="gmlp_fwd", out_shape=jax.ShapeDtypeStruct((s, w), F32), grid=(s // c,),
        in_specs=[pl.BlockSpec((c, w2), lambda n: (n, 0))] + [whole(a) for a in args],
        out_specs=pl.BlockSpec((c, w), lambda n: (n, 0)),
        compiler_params=_params("parallel"),
    )(proj, *args)


def _gmlp_bwd_call(proj, ln_g, ln_b, ws, bs, dout):
    s, w2 = proj.shape
    w = w2 // 2
    c, G = GMLP_CHUNK, GMLP_GROUPS
    gw = w // G

    def body(p_ref, g_ref, b_ref, ws_ref, bs_ref, do_ref, dp_ref, dws_ref, dbs_ref, dgb_ref):
        @pl.when(pl.program_id(0) == 0)
        def _():
            dws_ref[...] = jnp.zeros_like(dws_ref)
            dbs_ref[...] = jnp.zeros_like(dbs_ref)
            dgb_ref[...] = jnp.zeros_like(dgb_ref)

        x, cdf, u, vhat, rstd, vn = _gmlp_front(p_ref, g_ref, b_ref, w)
        dout = do_ref[...]
        du_parts, dvn_parts = [], []
        for gi in range(G):
            cols = slice(gi * gw, (gi + 1) * gw)
            wsg = ws_ref[gi].astype(MXU_DTYPE)
            vng = vn[:, cols].astype(MXU_DTYPE)
            vs = jnp.dot(wsg, vng, preferred_element_type=F32) + bs_ref[gi]
            du_parts.append(dout[:, cols] * vs)
            dvs = dout[:, cols] * u[:, cols]
            dbs_ref[:, cols] += dvs
            dvsb = dvs.astype(MXU_DTYPE)
            dws_ref[gi] += lax.dot_general(dvsb, vng, _NT, preferred_element_type=F32)
            dvn_parts.append(lax.dot_general(wsg, dvsb, _TN, preferred_element_type=F32))
        dvn = jnp.concatenate(dvn_parts, axis=1)
        dgb_ref[0] += _fold8(dvn * vhat)
        dgb_ref[1] += _fold8(dvn)
        dvh = dvn * g_ref[...]
        dv = rstd * (dvh - jnp.mean(dvh, axis=-1, keepdims=True)
                     - vhat * jnp.mean(dvh * vhat, axis=-1, keepdims=True))
        duv = jnp.concatenate(du_parts + [dv], axis=1)
        dp_ref[...] = duv * (cdf + x * (jnp.exp(-0.5 * x * x) * INV_SQRT_2PI))

    whole = lambda a: pl.BlockSpec(a.shape, lambda n: (0,) * a.ndim)
    args = (_vec(ln_g), _vec(ln_b), ws, bs)
    acc = lambda *shape: pl.BlockSpec(shape, lambda n: (0,) * len(shape))
    return _pcall(
        body, name="gmlp_bwd",
        out_shape=(jax.ShapeDtypeStruct((s, w2), F32), jax.ShapeDtypeStruct((G, c, c), F32),
                   jax.ShapeDtypeStruct((c, w), F32), jax.ShapeDtypeStruct((2, SUBLANES, w), F32)),
        grid=(s // c,),
        in_specs=[pl.BlockSpec((c, w2), lambda n: (n, 0))] + [whole(a) for a in args]
                 + [pl.BlockSpec((c, w), lambda n: (n, 0))],
        out_specs=(pl.BlockSpec((c, w2), lambda n: (n, 0)), acc(G, c, c), acc(c, w), acc(2, SUBLANES, w)),
        compiler_params=_params("arbitrary"),
    )(proj, *args, dout)


def _gmlp_mask(c):
    return jnp.tril(jnp.ones((c, c), dtype=bool))


@jax.custom_vjp
def _gmlp_core(proj, ln_g, ln_b, w_s, b_s):
    ws = jnp.where(_gmlp_mask(GMLP_CHUNK), w_s, 0.0)
    return _gmlp_fwd_call(proj, ln_g, ln_b, ws, b_s[..., None])


def _gmlp_core_fwd(proj, ln_g, ln_b, w_s, b_s):
    return _gmlp_core(proj, ln_g, ln_b, w_s, b_s), (proj, ln_g, ln_b, w_s, b_s)


def _gmlp_core_bwd(res, dout):
    proj, ln_g, ln_b, w_s, b_s = res
    mask = _gmlp_mask(GMLP_CHUNK)
    dproj, dws, dbs, dgb = _gmlp_bwd_call(proj, ln_g, ln_b, jnp.where(mask, w_s, 0.0), b_s[..., None], dout)
    dgb = jnp.sum(dgb, axis=1)
    c = GMLP_CHUNK
    db_s = jnp.sum(dbs.reshape(c, GMLP_GROUPS, -1), axis=-1).T
    return dproj, dgb[0], dgb[1], jnp.where(mask, dws, 0.0), db_s


_gmlp_core.defvjp(_gmlp_core_fwd, _gmlp_core_bwd)


def _chunked_gmlp(hx, p):
    core = _gmlp_core(_linear(hx, p['gmlp_w_in'], "gmlp_in"), p['gmlp_ln_g'], p['gmlp_ln_b'],
                      p['gmlp_w_s'], p['gmlp_b_s'])
    return _linear(core, p['gmlp_w_out'], "gmlp_out")


def _stick_breaking(hx, p):
    H = SB_HEADS
    s, d = hx.shape
    dh = d // H
    qkv = _linear(hx, p['sb_w_in'], "sb_in")
    q, k, v = (qkv[:, j * d:(j + 1) * d].reshape(s, H, dh).transpose(1, 0, 2) for j in range(3))
    o = _sb_core(q, k, v)
    return _linear(o.transpose(1, 0, 2).reshape(s, d), p['sb_w_out'], "sb_out")


MIXERS = (_gated_deltanet, _retention, _chunked_gmlp, _stick_breaking)


def _trunk_grad(x, mods, p, target):
    d = x.shape[-1]
    saved = []
    for i in range(DEPTH):
        sh1, sc1, g1, sh2, sc2, g2 = (mods[i, j * d:(j + 1) * d] for j in range(6))
        h1 = _modulate(x, sc1, sh1, F32, "mod_a%d" % i)
        y1, mixer_vjp = jax.vjp(MIXERS[i], h1, {n: p[n] for n in MIXER_PARAMS[i]})
        x1 = _resid_ln(x, y1, g1, p['ln_g'][i, 0], p['ln_b'][i, 0], "ln_a%d" % i)
        h2 = _modulate(x1, sc2, sh2, MXU_DTYPE, "mod_b%d" % i)
        gu = _mm(h2, p['ffn_up'][i], 'nn', "ffn_up%d_fwd" % i)
        act = _ffn_gate(gu, p['ffn_conv_w'][i], p['ffn_conv_b'][i], "ffn_gate%d" % i)
        y2 = _mm(act, p['ffn_down'][i], 'nn', "ffn_down%d_fwd" % i)
        x2 = _resid_ln(x1, y2, g2, p['ln_g'][i, 1], p['ln_b'][i, 1], "ln_b%d" % i)
        saved.append((x, y1, mixer_vjp, x1, h2, gu, act, y2))
        x = x2
    loss, dx = _loss_head(x, target, "loss_head")

    dp = {n: None for n in p}
    d_ln_g, d_ln_b, d_up, d_down, d_cw, d_cb, dmods = [], [], [], [], [], [], []
    for i in reversed(range(DEPTH)):
        x0, y1, mixer_vjp, x1, h2, gu, act, y2 = saved[i]
        sh1, sc1, g1, sh2, sc2, g2 = (mods[i, j * d:(j + 1) * d] for j in range(6))
        dxa, dy2, dgam2, dbet2, dg2 = _resid_ln_bwd(x1, y2, g2, p['ln_g'][i, 1], dx, "ln_b%d_bwd" % i)
        dact = _mm(dy2, p['ffn_down'][i], 'nt', "ffn_down%d_dx" % i)
        d_down.append(_mm(act, dy2, 'tn', "ffn_down%d_dw" % i))
        dgate, dupp, dcw, dcb = _ffn_gate_bwd(dact, gu, p['ffn_conv_w'][i], p['ffn_conv_b'][i],
                                              "ffn_gate%d_bwd" % i)
        dgu = jnp.concatenate([dgate, dupp], axis=1)
        dh2 = _mm(dgu, p['ffn_up'][i], 'nt', "ffn_up%d_dx" % i)
        d_up.append(_mm(h2, dgu, 'tn', "ffn_up%d_dw" % i))
        dx1, dsc2, dsh2 = _modulate_bwd(dxa, dh2, x1, sc2, "mod_b%d_bwd" % i)
        dxa, dy1, dgam1, dbet1, dg1 = _resid_ln_bwd(x0, y1, g1, p['ln_g'][i, 0], dx1, "ln_a%d_bwd" % i)
        dh1, dmix = mixer_vjp(dy1)
        dp.update(dmix)
        dx, dsc1, dsh1 = _modulate_bwd(dxa, dh1, x0, sc1, "mod_a%d_bwd" % i)
        d_ln_g.append(jnp.stack([dgam1, dgam2]))
        d_ln_b.append(jnp.stack([dbet1, dbet2]))
        d_cw.append(dcw)
        d_cb.append(dcb)
        dmods.append(jnp.concatenate([dsh1, dsc1, dg1, dsh2, dsc2, dg2]))
    for n, parts in (('ln_g', d_ln_g), ('ln_b', d_ln_b), ('ffn_up', d_up), ('ffn_down', d_down),
                     ('ffn_conv_w', d_cw), ('ffn_conv_b', d_cb)):
        dp[n] = jnp.stack(parts[::-1])
    return loss, dx, jnp.stack(dmods[::-1]), dp


def _join(blocks, axis):
    return jnp.concatenate([blocks[d] for d in range(N_DEV)], axis=axis)


def _split(whole, axis):
    n = whole.shape[axis] // N_DEV
    return jnp.stack([lax.slice_in_dim(whole, d * n, (d + 1) * n, axis=axis) for d in range(N_DEV)])


def _pad8(a):
    pad = (-a.shape[0]) % 8
    return jnp.pad(a, ((0, pad), (0, 0))) if pad else a


def _pack_big_grads(full_grads, axes):
    per_dev = jnp.concatenate([_split(g, ax).reshape(N_DEV, -1) for g, ax in zip(full_grads, axes)], axis=1)
    pad = (-per_dev.shape[1]) % (BIG_ROW_ALIGN * LANES)
    if pad:
        per_dev = jnp.pad(per_dev, ((0, 0), (0, pad)))
    return per_dev.reshape(N_DEV, -1, LANES)


def kernel(x, c, cond_w, cond_b, ada_w, ada_b, ln_g, ln_b, ffn_up, ffn_conv_w, ffn_conv_b, ffn_down, gdn_w_in, gdn_conv_w, gdn_a_log, gdn_dt_bias, gdn_norm_w, gdn_w_out, ret_w_in, ret_w_out, gmlp_w_in, gmlp_ln_g, gmlp_ln_b, gmlp_w_s, gmlp_b_s, gmlp_w_out, sb_w_in, sb_w_out, loss_target, m_cond_w, m_cond_b, m_ada_w, m_ada_b, m_ln_g, m_ln_b, m_ffn_up, m_ffn_conv_w, m_ffn_conv_b, m_ffn_down, m_gdn_w_in, m_gdn_conv_w, m_gdn_a_log, m_gdn_dt_bias, m_gdn_norm_w, m_gdn_w_out, m_ret_w_in, m_ret_w_out, m_gmlp_w_in, m_gmlp_ln_g, m_gmlp_ln_b, m_gmlp_w_s, m_gmlp_b_s, m_gmlp_w_out, m_sb_w_in, m_sb_w_out, v_cond_w, v_cond_b, v_ada_w, v_ada_b, v_ln_g, v_ln_b, v_ffn_up, v_ffn_conv_w, v_ffn_conv_b, v_ffn_down, v_gdn_w_in, v_gdn_conv_w, v_gdn_a_log, v_gdn_dt_bias, v_gdn_norm_w, v_gdn_w_out, v_ret_w_in, v_ret_w_out, v_gmlp_w_in, v_gmlp_ln_g, v_gmlp_ln_b, v_gmlp_w_s, v_gmlp_b_s, v_gmlp_w_out, v_sb_w_in, v_sb_w_out):
    w = dict(cond_w=cond_w, cond_b=cond_b, ada_w=ada_w, ada_b=ada_b, ln_g=ln_g, ln_b=ln_b, ffn_up=ffn_up,
             ffn_conv_w=ffn_conv_w, ffn_conv_b=ffn_conv_b, ffn_down=ffn_down, gdn_w_in=gdn_w_in,
             gdn_conv_w=gdn_conv_w, gdn_a_log=gdn_a_log, gdn_dt_bias=gdn_dt_bias, gdn_norm_w=gdn_norm_w,
             gdn_w_out=gdn_w_out, ret_w_in=ret_w_in, ret_w_out=ret_w_out, gmlp_w_in=gmlp_w_in,
             gmlp_ln_g=gmlp_ln_g, gmlp_ln_b=gmlp_ln_b, gmlp_w_s=gmlp_w_s, gmlp_b_s=gmlp_b_s,
             gmlp_w_out=gmlp_w_out, sb_w_in=sb_w_in, sb_w_out=sb_w_out)
    mom = dict(cond_w=m_cond_w, cond_b=m_cond_b, ada_w=m_ada_w, ada_b=m_ada_b, ln_g=m_ln_g, ln_b=m_ln_b,
               ffn_up=m_ffn_up, ffn_conv_w=m_ffn_conv_w, ffn_conv_b=m_ffn_conv_b, ffn_down=m_ffn_down,
               gdn_w_in=m_gdn_w_in, gdn_conv_w=m_gdn_conv_w, gdn_a_log=m_gdn_a_log, gdn_dt_bias=m_gdn_dt_bias,
               gdn_norm_w=m_gdn_norm_w, gdn_w_out=m_gdn_w_out, ret_w_in=m_ret_w_in, ret_w_out=m_ret_w_out,
               gmlp_w_in=m_gmlp_w_in, gmlp_ln_g=m_gmlp_ln_g, gmlp_ln_b=m_gmlp_ln_b, gmlp_w_s=m_gmlp_w_s,
               gmlp_b_s=m_gmlp_b_s, gmlp_w_out=m_gmlp_w_out, sb_w_in=m_sb_w_in, sb_w_out=m_sb_w_out)
    var = dict(cond_w=v_cond_w, cond_b=v_cond_b, ada_w=v_ada_w, ada_b=v_ada_b, ln_g=v_ln_g, ln_b=v_ln_b,
               ffn_up=v_ffn_up, ffn_conv_w=v_ffn_conv_w, ffn_conv_b=v_ffn_conv_b, ffn_down=v_ffn_down,
               gdn_w_in=v_gdn_w_in, gdn_conv_w=v_gdn_conv_w, gdn_a_log=v_gdn_a_log, gdn_dt_bias=v_gdn_dt_bias,
               gdn_norm_w=v_gdn_norm_w, gdn_w_out=v_gdn_w_out, ret_w_in=v_ret_w_in, ret_w_out=v_ret_w_out,
               gmlp_w_in=v_gmlp_w_in, gmlp_ln_g=v_gmlp_ln_g, gmlp_ln_b=v_gmlp_ln_b, gmlp_w_s=v_gmlp_w_s,
               gmlp_b_s=v_gmlp_b_s, gmlp_w_out=v_gmlp_w_out, sb_w_in=v_sb_w_in, sb_w_out=v_sb_w_out)

    me = _my_id()
    x = x[0]
    target = loss_target[0]
    d = x.shape[-1]
    dsh = d // N_DEV
    msh = ada_w.shape[-1]

    c_all = _exchange(_pad8(c), False, "gather_c")[:, 0, :]
    c_mine = lax.dynamic_slice_in_dim(c_all, me * dsh, dsh, axis=1)
    pre_part = _mm(c_mine, cond_w, 'nn', "cond_fwd")
    pre = jnp.sum(_exchange(pre_part, False, "gather_pre"), axis=0) + cond_b
    e_all = jax.nn.silu(pre)
    mod_part = jnp.concatenate([_mm(e_all, ada_w[i], 'nn', "ada_fwd%d" % i) for i in range(DEPTH)], axis=0)
    mod_all = _exchange(mod_part, False, "gather_mod")
    mod_all = mod_all.reshape(N_DEV, DEPTH, N_DEV, msh)
    mods = lax.dynamic_index_in_dim(mod_all, me, axis=2, keepdims=False)
    mods = mods.transpose(1, 0, 2).reshape(DEPTH, N_DEV * msh) + ada_b

    big_names = list(BIG)
    packed = _pack_rows([w[n] for n in big_names], BF16, BIG_ROW_ALIGN)
    gathered = _gather_two_level(packed, "gather_weights")
    blocks = _unpack_rows(gathered, [w[n].shape for n in big_names])
    p = {n: _join(b, BIG[n]) for n, b in zip(big_names, blocks)}
    for n in big_names:
        if not n.startswith('ffn_'):
            p[n] = p[n].astype(F32)
    sm_names = list(SMALL_SHARDED)
    sm_packed = _pack_rows([w[n] for n in sm_names], F32)
    sm_blocks = _unpack_rows(_exchange(sm_packed, False, "gather_small"), [w[n].shape for n in sm_names])
    for n, b in zip(sm_names, sm_blocks):
        p[n] = _join(b, SMALL_SHARDED[n])
    for n in SMALL_REPL:
        p[n] = w[n]
    n_qkvz = 4 * d
    p['gdn_w_qkvz'] = p['gdn_w_in'][:, :n_qkvz]
    p['gdn_w_ab'] = jnp.pad(p['gdn_w_in'][:, n_qkvz:], ((0, 0), (0, LANES - 2 * GDN_HEADS)))
    del p['gdn_w_in']

    loss_local, dx, dmods, dp = _trunk_grad(x, mods, p, target)
    dp['gdn_w_in'] = jnp.concatenate([dp.pop('gdn_w_qkvz'), dp.pop('gdn_w_ab')[:, :2 * GDN_HEADS]], axis=1)

    dmod_all = _exchange(dmods.reshape(-1, d), False, "gather_dmod").reshape(N_DEV, DEPTH, 6 * d)
    grads = {'ada_b': jnp.sum(dmod_all, axis=0)}
    dm_mine = lax.dynamic_slice_in_dim(dmod_all, me * msh, msh, axis=2)
    grads['ada_w'] = jnp.stack([_mm_outer(e_all, dm_mine[:, i], "ada_dw%d" % i) for i in range(DEPTH)])
    de_part = _mm(dm_mine[:, 0], ada_w[0], 'nt', "ada_de0")
    for i in range(1, DEPTH):
        de_part = de_part + _mm(dm_mine[:, i], ada_w[i], 'nt', "ada_de%d" % i)
    de_all = jnp.sum(_exchange(de_part, False, "gather_de"), axis=0)
    sig = jax.nn.sigmoid(pre)
    dpre = de_all * (sig * (1.0 + pre * (1.0 - sig)))
    grads['cond_b'] = jnp.sum(dpre, axis=0)
    grads['cond_w'] = _mm_outer(c_mine, dpre, "cond_dw")

    small_names = sm_names + SMALL_REPL
    small_packed = _pack_rows([loss_local.reshape(1)] + [dp[n] for n in small_names], F32)
    small_sum = _sum_slots(_exchange(small_packed, False, "gather_small_grads"), "sum_small_grads")
    small = _unpack_rows(small_sum, [(1,)] + [dp[n].shape for n in small_names])
    loss = small[0][0]
    for n, g in zip(small_names, small[1:]):
        if n in SMALL_SHARDED:
            ax = SMALL_SHARDED[n]
            g = lax.dynamic_slice_in_dim(g, me * w[n].shape[ax], w[n].shape[ax], axis=ax)
        grads[n] = g

    send = _pack_big_grads([dp[n] for n in big_names], [BIG[n] for n in big_names])
    shapes = [w[n].shape for n in big_names]
    outs = _adamw(_reduce_to_owner(send), *[_pack_rows([t[n] for n in big_names], F32, BIG_ROW_ALIGN) for t in (w, mom, var)],
                  "adamw_big")
    g_b, d_b, m_b, v_b = (_unpack_rows(o, shapes) for o in outs)
    delta, new_m, new_v = {}, {}, {}
    for j, n in enumerate(big_names):
        grads[n], delta[n], new_m[n], new_v[n] = g_b[j], d_b[j], m_b[j], v_b[j]

    rest = [n for n in WEIGHTS if n not in BIG]
    shapes = [w[n].shape for n in rest]
    outs = _adamw([(_pack_rows([grads[n] for n in rest], F32, BIG_ROW_ALIGN)[None], 0)],
                  *[_pack_rows([t[n] for n in rest], F32, BIG_ROW_ALIGN) for t in (w, mom, var)], "adamw_rest")
    _, d_r, m_r, v_r = (_unpack_rows(o, shapes) for o in outs)
    for j, n in enumerate(rest):
        delta[n], new_m[n], new_v[n] = d_r[j], m_r[j], v_r[j]

    return (loss, dx[None], *[grads[n] for n in WEIGHTS], *[delta[n] for n in WEIGHTS],
            *[new_m[n] for n in WEIGHTS], *[new_v[n] for n in WEIGHTS])
```

```python
import functools
import math

import jax
import jax.numpy as jnp
from jax import lax
from jax.experimental import pallas as pl
from jax.experimental.pallas import tpu as pltpu

F32 = jnp.float32
BF16 = jnp.bfloat16
MXU_DTYPE = jnp.bfloat16
MESH = pl.DeviceIdType.MESH
N_DEV = 8
LANES = 128
SUBLANES = 8
VMEM_LIMIT = 48 * 1024 * 1024

DEPTH = 4
LN_EPS = 1e-5
DN_ALPHA = (2.0 * DEPTH) ** 0.25
GDN_HEADS, GDN_CHUNK = 8, 64
RET_HEADS, RET_CHUNK, RET_ROPE_BASE = 4, 128, 10000.0
GMLP_CHUNK, GMLP_GROUPS = 128, 8
SB_HEADS = 16
ADAM_LR, ADAM_B1, ADAM_B2, ADAM_EPS, ADAM_WD, ADAM_STEP = 0.001, 0.9, 0.999, 1e-08, 0.01, 10

WEIGHTS = ['cond_w', 'cond_b', 'ada_w', 'ada_b', 'ln_g', 'ln_b', 'ffn_up', 'ffn_conv_w', 'ffn_conv_b', 'ffn_down',
           'gdn_w_in', 'gdn_conv_w', 'gdn_a_log', 'gdn_dt_bias', 'gdn_norm_w', 'gdn_w_out', 'ret_w_in', 'ret_w_out',
           'gmlp_w_in', 'gmlp_ln_g', 'gmlp_ln_b', 'gmlp_w_s', 'gmlp_b_s', 'gmlp_w_out', 'sb_w_in', 'sb_w_out']
BIG = {'ffn_up': 2, 'ffn_down': 1, 'gdn_w_in': 1, 'gdn_w_out': 0, 'ret_w_in': 1, 'ret_w_out': 0,
       'gmlp_w_in': 1, 'gmlp_w_out': 0, 'sb_w_in': 1, 'sb_w_out': 0}
SMALL_SHARDED = {'ln_g': 2, 'ln_b': 2, 'ffn_conv_w': 2, 'gdn_conv_w': 1}
SMALL_REPL = ['ffn_conv_b', 'gdn_a_log', 'gdn_dt_bias', 'gdn_norm_w', 'gmlp_ln_g', 'gmlp_ln_b', 'gmlp_w_s', 'gmlp_b_s']
MIXER_PARAMS = (('gdn_w_qkvz', 'gdn_w_ab', 'gdn_conv_w', 'gdn_a_log', 'gdn_dt_bias', 'gdn_norm_w', 'gdn_w_out'),
                ('ret_w_in', 'ret_w_out'),
                ('gmlp_w_in', 'gmlp_ln_g', 'gmlp_ln_b', 'gmlp_w_s', 'gmlp_b_s', 'gmlp_w_out'),
                ('sb_w_in', 'sb_w_out'))


def _pcall(body, **kw):
    return pl.pallas_call(body, **kw)


def _params(*semantics):
    return pltpu.CompilerParams(dimension_semantics=semantics, vmem_limit_bytes=VMEM_LIMIT)


def _my_id():
    return 4 * lax.axis_index("x") + 2 * lax.axis_index("y") + lax.axis_index("c")


def _pick(dim, prefs):
    for p in prefs:
        if dim % p == 0:
            return p
    return dim


def _exchange(src, scatter, name):
    blk = src.shape[1:] if scatter else src.shape
    out_shape = jax.ShapeDtypeStruct((N_DEV,) + tuple(blk), src.dtype)

    def body(src_ref, out_ref, send_sems, recv_sems, local_sem):
        x, y, c = lax.axis_index("x"), lax.axis_index("y"), lax.axis_index("c")
        me = 4 * x + 2 * y + c
        mine = pltpu.make_async_copy(src_ref.at[me] if scatter else src_ref, out_ref.at[me], local_sem)
        mine.start()
        copies = []
        for k in range(1, N_DEV):
            px = 1 - x if (k >> 2) & 1 else x
            py = 1 - y if (k >> 1) & 1 else y
            pc = 1 - c if k & 1 else c
            peer = 4 * px + 2 * py + pc
            cp = pltpu.make_async_remote_copy(
                src_ref=src_ref.at[peer] if scatter else src_ref,
                dst_ref=out_ref.at[me],
                send_sem=send_sems.at[k - 1], recv_sem=recv_sems.at[k - 1],
                device_id=(px, py, pc), device_id_type=MESH)
            cp.start()
            copies.append(cp)
        for cp in copies:
            cp.wait()
        mine.wait()

    return _pcall(
        body, name=name, out_shape=out_shape,
        in_specs=[pl.BlockSpec(memory_space=pl.ANY)],
        out_specs=pl.BlockSpec(memory_space=pl.ANY),
        scratch_shapes=[pltpu.SemaphoreType.DMA((N_DEV - 1,)), pltpu.SemaphoreType.DMA((N_DEV - 1,)),
                        pltpu.SemaphoreType.DMA(())],
    )(src)


def _flip(x, y, c, k):
    return (1 - x if (k >> 2) & 1 else x, 1 - y if (k >> 1) & 1 else y, 1 - c if k & 1 else c)


def _dev_id(p):
    return 4 * p[0] + 2 * p[1] + p[2]


OTHER_CHIPS = (4, 2, 6)


def _gather_two_level(src, name):
    out_shape = jax.ShapeDtypeStruct((N_DEV,) + tuple(src.shape), src.dtype)

    def body(x_ref, out_ref, send_sems, recv_sems, local_sem):
        x, y, c = lax.axis_index("x"), lax.axis_index("y"), lax.axis_index("c")
        me, sibling = (x, y, c), (x, y, 1 - c)
        chips = [_flip(x, y, c, k) for k in OTHER_CHIPS]

        def copy(k, block, to, from_src=False):
            slot = out_ref.at[_dev_id(block)]
            return pltpu.make_async_remote_copy(
                src_ref=x_ref if from_src else slot, dst_ref=slot,
                send_sem=send_sems.at[k], recv_sem=recv_sems.at[k], device_id=to, device_id_type=MESH)

        mine = pltpu.make_async_copy(x_ref, out_ref.at[_dev_id(me)], local_sem)
        mine.start()
        first = [copy(0, me, sibling, True)] + [copy(1 + j, me, chip, True) for j, chip in enumerate(chips)]
        for cp in first:
            cp.start()
        passed = [copy(4 + j, chip, sibling) for j, chip in enumerate(chips)]
        for j, chip in enumerate(chips):
            copy(1 + j, chip, me).wait_recv()
            passed[j].start()
        copy(0, sibling, me).wait_recv()
        for j, chip in enumerate(chips):
            copy(4 + j, (chip[0], chip[1], 1 - c), me).wait_recv()
        for cp in first + passed:
            cp.wait_send()
        mine.wait()

    return _pcall(
        body, name=name, out_shape=out_shape,
        in_specs=[pl.BlockSpec(memory_space=pl.ANY)],
        out_specs=pl.BlockSpec(memory_space=pl.ANY),
        scratch_shapes=[pltpu.SemaphoreType.DMA((N_DEV - 1,)), pltpu.SemaphoreType.DMA((N_DEV - 1,)),
                        pltpu.SemaphoreType.DMA(())],
    )(src)


def _send_slots(src, plan, n_out, name):
    out_shape = jax.ShapeDtypeStruct((n_out,) + tuple(src.shape[1:]), src.dtype)

    def body(src_ref, out_ref, send_sems, recv_sems):
        x, y, c = lax.axis_index("x"), lax.axis_index("y"), lax.axis_index("c")
        copies = []
        for e, (k, src_slot, dst_slot) in enumerate(plan):
            cp = pltpu.make_async_remote_copy(
                src_ref=src_ref.at[src_slot(x, y, c)], dst_ref=out_ref.at[dst_slot],
                send_sem=send_sems.at[e], recv_sem=recv_sems.at[e],
                device_id=_flip(x, y, c, k), device_id_type=MESH)
            cp.start()
            copies.append(cp)
        for cp in copies:
            cp.wait()

    return _pcall(
        body, name=name, out_shape=out_shape,
        in_specs=[pl.BlockSpec(memory_space=pl.ANY)],
        out_specs=pl.BlockSpec(memory_space=pl.ANY),
        scratch_shapes=[pltpu.SemaphoreType.DMA((len(plan),)), pltpu.SemaphoreType.DMA((len(plan),))],
    )(src)


def _reduce_to_owner(send):
    x, y, c = lax.axis_index("x"), lax.axis_index("y"), lax.axis_index("c")
    plan_a = [(1, lambda x, y, c: _dev_id((x, y, 1 - c)), 0)]
    plan_a += [(1, functools.partial(lambda k, x, y, c: _dev_id(_flip(x, y, c, k | 1)), k), 1 + j)
               for j, k in enumerate(OTHER_CHIPS)]
    from_sibling = _send_slots(send, plan_a, 1 + len(OTHER_CHIPS), "reduce_d2d")
    mine = jnp.stack([lax.dynamic_index_in_dim(send, _dev_id(_flip(x, y, c, k)), 0, keepdims=False)
                      for k in OTHER_CHIPS])
    rows = mine.shape[1]
    pair = _add_rows(mine.reshape(-1, LANES), from_sibling[1:].reshape(-1, LANES), BF16, "reduce_pair_sum")
    plan_c = [(k, functools.partial(lambda j, x, y, c: j, j), j) for j, k in enumerate(OTHER_CHIPS)]
    from_chips = _send_slots(pair.reshape(len(OTHER_CHIPS), rows, LANES), plan_c, len(OTHER_CHIPS), "reduce_ici")
    own = lax.dynamic_index_in_dim(send, _dev_id((x, y, c)), 0, keepdims=True)
    return [(own, 0), (from_sibling, 0)] + [(from_chips, j) for j in range(len(OTHER_CHIPS))]


ROW_ALIGN = 16
BIG_ROW_ALIGN = 512


def _pack_rows(parts, dtype, row_align=ROW_ALIGN):
    flat = jnp.concatenate([p.reshape(-1).astype(dtype) for p in parts])
    n = flat.shape[0]
    pad = (-n) % (row_align * LANES)
    if pad:
        flat = jnp.concatenate([flat, jnp.zeros((pad,), dtype)])
    return flat.reshape(-1, LANES)


def _unpack_rows(packed, shapes):
    lead = packed.shape[:-2]
    flat = packed.reshape(lead + (-1,))
    out, off = [], 0
    for s in shapes:
        n = math.prod(s)
        out.append(flat[..., off:off + n].reshape(lead + tuple(s)))
        off += n
    return out


MM_A_BLOCK_BYTES = 8 * 1024 * 1024


def _mm(a, b, dims, name, exact=False):
    if dims == 'nn':
        (m, k), n = a.shape, b.shape[1]
    elif dims == 'nt':
        (m, k), n = a.shape, b.shape[0]
    else:
        (k, m), n = a.shape, b.shape[1]
    tn = _pick(n, (512, 256, 128))
    tk = k if k <= 2816 else _pick(k, (2816, 2048, 1536, 1024, 512, 256, 128))
    nk = k // tk
    tm = _pick(m, (1408, 1024, 512, 256, 128))
    if m % (2 * tm) == 0:
        if 2 * tm * tk * a.dtype.itemsize <= MM_A_BLOCK_BYTES:
            tm = 2 * tm
        elif dims != 'tn' and nk > 1 and tk % (2 * LANES) == 0 and tm * tk * a.dtype.itemsize <= MM_A_BLOCK_BYTES:
            tm, tk, nk = 2 * tm, tk // 2, 2 * nk
    if dims == 'nn':
        a_spec = pl.BlockSpec((tm, tk), lambda i, j, kk: (i, kk))
        b_spec = pl.BlockSpec((tk, tn), lambda i, j, kk: (kk, j))
        dn = (((1,), (0,)), ((), ()))
    elif dims == 'nt':
        a_spec = pl.BlockSpec((tm, tk), lambda i, j, kk: (i, kk))
        b_spec = pl.BlockSpec((tn, tk), lambda i, j, kk: (j, kk))
        dn = (((1,), (1,)), ((), ()))
    else:
        a_spec = pl.BlockSpec((tk, tm), lambda i, j, kk: (kk, i))
        b_spec = pl.BlockSpec((tk, tn), lambda i, j, kk: (kk, j))
        dn = (((0,), (0,)), ((), ()))

    def product(a_ref, b_ref):
        if exact:
            return lax.dot_general(a_ref[...], b_ref[...], dn, precision=lax.Precision.HIGHEST,
                                   preferred_element_type=F32)
        return lax.dot_general(a_ref[...].astype(MXU_DTYPE), b_ref[...].astype(MXU_DTYPE), dn,
                               preferred_element_type=F32)

    def body(a_ref, b_ref, o_ref, *acc):
        if nk == 1:
            o_ref[...] = product(a_ref, b_ref)
            return
        acc_ref, = acc
        kk = pl.program_id(2)

        @pl.when(kk == 0)
        def _():
            acc_ref[...] = jnp.zeros_like(acc_ref)

        acc_ref[...] += product(a_ref, b_ref)

        @pl.when(kk == nk - 1)
        def _():
            o_ref[...] = acc_ref[...]

    return _pcall(
        body, name=name, out_shape=jax.ShapeDtypeStruct((m, n), F32),
        grid=(m // tm, n // tn, nk),
        in_specs=[a_spec, b_spec],
        out_specs=pl.BlockSpec((tm, tn), lambda i, j, kk: (i, j)),
        scratch_shapes=[pltpu.VMEM((tm, tn), F32)] if nk > 1 else [],
        compiler_params=_params("parallel", "parallel", "arbitrary"),
    )(a, b)


def _mm_outer(a, b, name):
    pad = LANES - a.shape[0]
    return _mm(jnp.pad(a.T, ((0, 0), (0, pad))), jnp.pad(b, ((0, pad), (0, 0))), 'nn', name, exact=True)


@functools.partial(jax.custom_vjp, nondiff_argnums=(2,))
def _linear(a, w, name):
    return _mm(a, w, 'nn', name + "_fwd")


def _linear_fwd(a, w, name):
    return _mm(a, w, 'nn', name + "_fwd"), (a, w)


def _linear_bwd(name, res, dy):
    a, w = res
    return _mm(dy, w, 'nt', name + "_dx"), _mm(a, dy, 'tn', name + "_dw")


_linear.defvjp(_linear_fwd, _linear_bwd)


SB_BK = 256
SB_STRIP = 16


def _sb_tiles(s):
    tq = _pick(s, (512, 256, 128))
    bk = min(SB_BK, tq)
    return tq, bk, tq // bk


def _sb_valid(t, sr, bk, q0, k0):
    row = lax.broadcasted_iota(jnp.int32, (sr, bk), 0) + (q0 + t * sr)
    col = lax.broadcasted_iota(jnp.int32, (sr, bk), 1) + k0
    return col < row


def _sb_tri(bk, inclusive):
    r = jnp.bitwise_and(lax.broadcasted_iota(jnp.int32, (2 * bk, bk), 0), bk - 1)
    c = lax.broadcasted_iota(jnp.int32, (2 * bk, bk), 1)
    return (r >= c).astype(BF16) if inclusive else (r > c).astype(BF16)


def _sb_split(ref, n, rows, bk, val):
    hi = val.astype(BF16)
    ref[n, rows, 0:bk] = hi
    ref[n, rows, bk:2 * bk] = (val - hi.astype(F32)).astype(BF16)


LOG2E = 1.0 / math.log(2.0)


def _sb_logits_phase(z_ref, ls_ref, hl_ref, l0_ref, n, tq, bk, sr, q0, k0, masked):
    for t in range(tq // sr):
        rows = slice(t * sr, (t + 1) * sr)
        z = z_ref[n, rows, :] * LOG2E
        ls = jnp.minimum(z, 0.0) - jnp.log(1.0 + jnp.exp2(-jnp.abs(z))) * LOG2E
        lm = ls - z
        if masked:
            lm = jnp.where(_sb_valid(t, sr, bk, q0, k0), lm, 0.0)
        ls_ref[n, rows, :] = ls
        _sb_split(hl_ref, n, rows, bk, lm)
        l0_ref[n, rows, :] = lm[:, 0:1]


def _sb_fwd_call(q, kt, v):
    h, s, dh = q.shape
    tq, bk, nt = _sb_tiles(s)
    sr = SB_STRIP

    def body(q_ref, kt_ref, v_ref, o_ref, z_ref, ls_ref, hl_ref, f_ref, a_ref, l0_ref, cl_ref, acc_ref):
        i = pl.program_id(1)
        q0 = i * tq
        cl_ref[...] = jnp.zeros_like(cl_ref)
        acc_ref[...] = jnp.zeros_like(acc_ref)
        u_excl = _sb_tri(bk, False)

        def iteration(kb0, masked):
            k0s = [pl.multiple_of(kb0 + (nt - 1 - n) * bk, bk) for n in range(nt)]
            for n in range(nt):
                z_ref[n] = jnp.dot(q_ref[...], kt_ref[:, pl.ds(k0s[n], bk)], preferred_element_type=F32)
            for n in range(nt):
                _sb_logits_phase(z_ref, ls_ref, hl_ref, l0_ref, n, tq, bk, sr, q0, k0s[n], masked)
            for n in range(nt):
                f_ref[n] = jnp.dot(hl_ref[n], u_excl, preferred_element_type=F32)
            for t in range(tq // sr):
                rows = slice(t * sr, (t + 1) * sr)
                c = cl_ref[rows, :]
                for n in range(nt):
                    f = f_ref[n, rows, :]
                    a = jnp.exp2(ls_ref[n, rows, :] + f + c)
                    if masked:
                        a = jnp.where(_sb_valid(t, sr, bk, q0, k0s[n]), a, 0.0)
                    a_ref[n, rows, :] = a.astype(a_ref.dtype)
                    c = c + f[:, 0:1] + l0_ref[n, rows, :]
                cl_ref[rows, :] = c
            for n in range(nt):
                acc_ref[...] += jnp.dot(a_ref[n], v_ref[pl.ds(k0s[n], bk), :], preferred_element_type=F32)

        def below(jj, c):
            iteration((i - 1 - jj) * tq, False)
            return c

        iteration(q0, True)
        lax.fori_loop(0, i, below, 0)
        o_ref[...] = acc_ref[...]

    return _pcall(
        body, name="sb_fwd", out_shape=jax.ShapeDtypeStruct((h, s, dh), F32),
        grid=(h, s // tq),
        in_specs=[pl.BlockSpec((None, tq, dh), lambda hh, i: (hh, i, 0)),
                  pl.BlockSpec((None, dh, s), lambda hh, i: (hh, 0, 0)),
                  pl.BlockSpec((None, s, dh), lambda hh, i: (hh, 0, 0))],
        out_specs=pl.BlockSpec((None, tq, dh), lambda hh, i: (hh, i, 0)),
        scratch_shapes=[pltpu.VMEM((nt, tq, bk), F32), pltpu.VMEM((nt, tq, bk), F32),
                        pltpu.VMEM((nt, tq, 2 * bk), BF16), pltpu.VMEM((nt, tq, bk), F32),
                        pltpu.VMEM((nt, tq, bk), q.dtype), pltpu.VMEM((nt, tq, 1), F32),
                        pltpu.VMEM((tq, 1), F32), pltpu.VMEM((tq, dh), F32)],
        compiler_params=_params("parallel", "arbitrary"),
    )(q, kt, v)


def _sb_bwd_call(q, qt, k, kt, vt, o, do, dot):
    h, s, dh = q.shape
    tq, bk, nt = _sb_tiles(s)
    sr = SB_STRIP

    def body(q_ref, qt_ref, k_ref, kt_ref, vt_ref, o_ref, do_ref, dot_ref, dq_ref, dkt_ref, dvt_ref,
             z_ref, ls_ref, hl_ref, f_ref, a_ref, g_ref, dz_ref, da_ref, l0_ref, dob_ref,
             cl_ref, cg_ref, dl_ref, dqa_ref):
        i = pl.program_id(1)
        q0 = i * tq

        @pl.when(i == 0)
        def _():
            dkt_ref[...] = jnp.zeros_like(dkt_ref)
            dvt_ref[...] = jnp.zeros_like(dvt_ref)

        cl_ref[...] = jnp.zeros_like(cl_ref)
        cg_ref[...] = jnp.zeros_like(cg_ref)
        dqa_ref[...] = jnp.zeros_like(dqa_ref)
        dob = do_ref[...].astype(dob_ref.dtype)
        dob_ref[...] = dob
        dl_ref[...] = jnp.sum(dob.astype(F32) * o_ref[...], axis=1, keepdims=True)
        u_excl = _sb_tri(bk, False)
        u_incl = _sb_tri(bk, True)

        def iteration(kb0, masked):
            k0s = [pl.multiple_of(kb0 + (nt - 1 - n) * bk, bk) for n in range(nt)]
            for n in range(nt):
                z_ref[n] = jnp.dot(q_ref[...], kt_ref[:, pl.ds(k0s[n], bk)], preferred_element_type=F32)
                da_ref[n] = jnp.dot(dob_ref[...], vt_ref[:, pl.ds(k0s[n], bk)], preferred_element_type=F32)
            for n in range(nt):
                _sb_logits_phase(z_ref, ls_ref, hl_ref, l0_ref, n, tq, bk, sr, q0, k0s[n], masked)
            for n in range(nt):
                f_ref[n] = jnp.dot(hl_ref[n], u_excl, preferred_element_type=F32)
            for t in range(tq // sr):
                rows = slice(t * sr, (t + 1) * sr)
                c = cl_ref[rows, :]
                for n in range(nt):
                    f = f_ref[n, rows, :]
                    a = jnp.exp2(ls_ref[n, rows, :] + f + c)
                    if masked:
                        a = jnp.where(_sb_valid(t, sr, bk, q0, k0s[n]), a, 0.0)
                    ab = a.astype(a_ref.dtype)
                    a_ref[n, rows, :] = ab
                    g = da_ref[n, rows, :] * ab.astype(F32)
                    g_ref[n, rows, :] = g
                    _sb_split(hl_ref, n, rows, bk, g)
                    c = c + f[:, 0:1] + l0_ref[n, rows, :]
                cl_ref[rows, :] = c
            for n in range(nt):
                f_ref[n] = jnp.dot(hl_ref[n], u_incl, preferred_element_type=F32)
            for t in range(tq // sr):
                rows = slice(t * sr, (t + 1) * sr)
                cg = cg_ref[rows, :]
                for n in range(nt):
                    sg_tile = f_ref[n, rows, :]
                    p = dl_ref[rows, :] - (sg_tile + cg)
                    g = g_ref[n, rows, :]
                    dz = g - (g + p) * jnp.exp2(ls_ref[n, rows, :])
                    if masked:
                        dz = jnp.where(_sb_valid(t, sr, bk, q0, k0s[n]), dz, 0.0)
                    dz_ref[n, rows, :] = dz.astype(dz_ref.dtype)
                    cg = cg + sg_tile[:, 0:1]
                cg_ref[rows, :] = cg
            for n in range(nt):
                cols = pl.ds(k0s[n], bk)
                dqa_ref[...] += jnp.dot(dz_ref[n], k_ref[cols, :], preferred_element_type=F32)
                dkt_ref[:, cols] += jnp.dot(qt_ref[...], dz_ref[n], preferred_element_type=F32)
                dvt_ref[:, cols] += jnp.dot(dot_ref[...], a_ref[n], preferred_element_type=F32)

        def below(jj, c):
            iteration((i - 1 - jj) * tq, False)
            return c

        iteration(q0, True)
        lax.fori_loop(0, i, below, 0)
        dq_ref[...] = dqa_ref[...]

    blk_q = pl.BlockSpec((None, tq, dh), lambda hh, i: (hh, i, 0))
    blk_qt = pl.BlockSpec((None, dh, tq), lambda hh, i: (hh, 0, i))
    blk_s = pl.BlockSpec((None, s, dh), lambda hh, i: (hh, 0, 0))
    blk_st = pl.BlockSpec((None, dh, s), lambda hh, i: (hh, 0, 0))
    mx = q.dtype
    return _pcall(
        body, name="sb_bwd",
        out_shape=(jax.ShapeDtypeStruct((h, s, dh), F32), jax.ShapeDtypeStruct((h, dh, s), F32),
                   jax.ShapeDtypeStruct((h, dh, s), F32)),
        grid=(h, s // tq),
        in_specs=[blk_q, blk_qt, blk_s, blk_st, blk_st, blk_q, blk_q, blk_qt],
        out_specs=(blk_q, blk_st, blk_st),
        scratch_shapes=[pltpu.VMEM((nt, tq, bk), F32), pltpu.VMEM((nt, tq, bk), F32),
                        pltpu.VMEM((nt, tq, 2 * bk), BF16), pltpu.VMEM((nt, tq, bk), F32),
                        pltpu.VMEM((nt, tq, bk), mx), pltpu.VMEM((nt, tq, bk), F32),
                        pltpu.VMEM((nt, tq, bk), mx), pltpu.VMEM((nt, tq, bk), F32),
                        pltpu.VMEM((nt, tq, 1), F32), pltpu.VMEM((tq, dh), mx),
                        pltpu.VMEM((tq, 1), F32), pltpu.VMEM((tq, 1), F32), pltpu.VMEM((tq, 1), F32),
                        pltpu.VMEM((tq, dh), F32)],
        compiler_params=_params("parallel", "arbitrary"),
    )(q, qt, k, kt, vt, o, do, dot)


def _swap(t):
    return t.transpose(0, 2, 1)


def _sb_scale(dh):
    assert math.log2(dh) % 2 == 0, dh
    return dh ** -0.5


@jax.custom_vjp
def _sb_core(q, k, v):
    return _sb_core_fwd(q, k, v)[0]


def _sb_core_fwd(q, k, v):
    scale = _sb_scale(q.shape[-1])
    qs, kb, vb = (q.astype(MXU_DTYPE) * scale).astype(MXU_DTYPE), k.astype(MXU_DTYPE), v.astype(MXU_DTYPE)
    o = _sb_fwd_call(qs, _swap(kb), vb)
    return o, (qs, kb, vb, o)


def _sb_core_bwd(res, do):
    qs, kb, vb, o = res
    ks = (kb * _sb_scale(kb.shape[-1])).astype(MXU_DTYPE)
    dq, dkt, dvt = _sb_bwd_call(qs, _swap(qs), ks, _swap(kb), _swap(vb), o, do, _swap(do.astype(MXU_DTYPE)))
    return dq, _swap(dkt), _swap(dvt)


_sb_core.defvjp(_sb_core_fwd, _sb_core_bwd)


def _row_block(s):
    return _pick(s, (512, 256, 128, 64, 32, 16, 8))


def _fold8(t):
    r, c = t.shape
    return jnp.sum(t.reshape(r // SUBLANES, SUBLANES, c), axis=0)


def _vec(a):
    return a.reshape(1, -1)


def _modulate(x, sc, sh, out_dtype, name):
    s, d = x.shape
    tr = _row_block(s)

    def body(x_ref, sc_ref, sh_ref, o_ref):
        o_ref[...] = (x_ref[...] * (1.0 + sc_ref[...]) + sh_ref[...]).astype(o_ref.dtype)

    row = pl.BlockSpec((tr, d), lambda i: (i, 0))
    vec = pl.BlockSpec((1, d), lambda i: (0, 0))
    return _pcall(body, name=name, out_shape=jax.ShapeDtypeStruct((s, d), out_dtype), grid=(s // tr,),
                  in_specs=[row, vec, vec], out_specs=row, compiler_params=_params("parallel"))(x, _vec(sc), _vec(sh))


def _modulate_bwd(dxa, dh, x, sc, name):
    s, d = x.shape
    tr = _row_block(s)

    def body(dxa_ref, dh_ref, x_ref, sc_ref, dx_ref, acc_ref):
        @pl.when(pl.program_id(0) == 0)
        def _():
            acc_ref[...] = jnp.zeros_like(acc_ref)

        dh = dh_ref[...]
        dx_ref[...] = dxa_ref[...] + dh * (1.0 + sc_ref[...])
        acc_ref[0] += _fold8(dh * x_ref[...])
        acc_ref[1] += _fold8(dh)

    row = pl.BlockSpec((tr, d), lambda i: (i, 0))
    vec = pl.BlockSpec((1, d), lambda i: (0, 0))
    dx, acc = _pcall(
        body, name=name,
        out_shape=(jax.ShapeDtypeStruct((s, d), F32), jax.ShapeDtypeStruct((2, SUBLANES, d), F32)),
        grid=(s // tr,), in_specs=[row, row, row, vec],
        out_specs=(row, pl.BlockSpec((2, SUBLANES, d), lambda i: (0, 0, 0))),
        compiler_params=_params("arbitrary"))(dxa, dh, x, _vec(sc))
    acc = jnp.sum(acc, axis=1)
    return dx, acc[0], acc[1]


def _resid_ln(x, y, g, gamma, beta, name):
    s, d = x.shape
    tr = _row_block(s)

    def body(x_ref, y_ref, g_ref, gam_ref, bet_ref, o_ref):
        u = DN_ALPHA * x_ref[...] + (1.0 + g_ref[...]) * y_ref[...]
        uc = u - jnp.mean(u, axis=-1, keepdims=True)
        var = jnp.mean(uc * uc, axis=-1, keepdims=True)
        o_ref[...] = uc * lax.rsqrt(var + LN_EPS) * gam_ref[...] + bet_ref[...]

    row = pl.BlockSpec((tr, d), lambda i: (i, 0))
    vec = pl.BlockSpec((1, d), lambda i: (0, 0))
    return _pcall(body, name=name, out_shape=jax.ShapeDtypeStruct((s, d), F32), grid=(s // tr,),
                  in_specs=[row, row, vec, vec, vec], out_specs=row,
                  compiler_params=_params("parallel"))(x, y, _vec(g), _vec(gamma), _vec(beta))


def _resid_ln_bwd(x, y, g, gamma, dout, name):
    s, d = x.shape
    tr = _row_block(s)

    def body(x_ref, y_ref, g_ref, gam_ref, do_ref, dxa_ref, dy_ref, acc_ref):
        @pl.when(pl.program_id(0) == 0)
        def _():
            acc_ref[...] = jnp.zeros_like(acc_ref)

        y = y_ref[...]
        gg = 1.0 + g_ref[...]
        u = DN_ALPHA * x_ref[...] + gg * y
        uc = u - jnp.mean(u, axis=-1, keepdims=True)
        rstd = lax.rsqrt(jnp.mean(uc * uc, axis=-1, keepdims=True) + LN_EPS)
        xhat = uc * rstd
        dout = do_ref[...]
        dxh = dout * gam_ref[...]
        du = rstd * (dxh - jnp.mean(dxh, axis=-1, keepdims=True)
                     - xhat * jnp.mean(dxh * xhat, axis=-1, keepdims=True))
        dxa_ref[...] = DN_ALPHA * du
        dy_ref[...] = gg * du
        acc_ref[0] += _fold8(dout * xhat)
        acc_ref[1] += _fold8(dout)
        acc_ref[2] += _fold8(du * y)

    row = pl.BlockSpec((tr, d), lambda i: (i, 0))
    vec = pl.BlockSpec((1, d), lambda i: (0, 0))
    dxa, dy, acc = _pcall(
        body, name=name,
        out_shape=(jax.ShapeDtypeStruct((s, d), F32), jax.ShapeDtypeStruct((s, d), F32),
                   jax.ShapeDtypeStruct((3, SUBLANES, d), F32)),
        grid=(s // tr,), in_specs=[row, row, vec, vec, row],
        out_specs=(row, row, pl.BlockSpec((3, SUBLANES, d), lambda i: (0, 0, 0))),
        compiler_params=_params("arbitrary"))(x, y, _vec(g), _vec(gamma), dout)
    acc = jnp.sum(acc, axis=1)
    return dxa, dy, acc[0], acc[1], acc[2]


def _loss_head(x, target, name):
    s, d = x.shape
    tr = _row_block(s)

    def body(x_ref, t_ref, dx_ref, acc_ref):
        @pl.when(pl.program_id(0) == 0)
        def _():
            acc_ref[...] = jnp.zeros_like(acc_ref)

        e = x_ref[...] - t_ref[...]
        dx_ref[...] = e * (1.0 / d)
        acc_ref[...] += _fold8(e * e)

    row = pl.BlockSpec((tr, d), lambda i: (i, 0))
    dx, acc = _pcall(
        body, name=name,
        out_shape=(jax.ShapeDtypeStruct((s, d), F32), jax.ShapeDtypeStruct((SUBLANES, d), F32)),
        grid=(s // tr,), in_specs=[row, row],
        out_specs=(row, pl.BlockSpec((SUBLANES, d), lambda i: (0, 0))),
        compiler_params=_params("arbitrary"))(x, target)
    return (0.5 / d) * jnp.sum(acc), dx


CONV_STRIPE = 128
CONV_ROWS = 256


def _shift_down(cur, halo, k):
    ext = jnp.concatenate([halo, cur], axis=0)
    return pltpu.roll(ext, k, 0)[SUBLANES:]


def _shift_up(cur, halo, k):
    ext = jnp.concatenate([cur, halo], axis=0)
    n = ext.shape[0]
    return pltpu.roll(ext, n - k, 0)[:n - SUBLANES]


def _gate_chunk(g_ref, r, rc):
    r0 = pl.multiple_of(r * rc, rc)
    cur = g_ref[pl.ds(r0, rc), :]
    hs = pl.multiple_of(jnp.maximum(r0 - SUBLANES, 0), SUBLANES)
    halo = jnp.where(r > 0, g_ref[pl.ds(hs, SUBLANES), :], 0.0)
    return r0, cur, _shift_down(cur, halo, 1), _shift_down(cur, halo, 2)


def _ffn_gate(gu, cw, cb, name):
    s, f2 = gu.shape
    f = f2 // 2
    tc = _pick(f, (CONV_STRIPE,))
    rc = _pick(s, (CONV_ROWS, 128, 64, 32, 16, 8))
    nj = f // tc

    def body(g_ref, u_ref, cw_ref, cb_ref, a_ref):
        w0, w1, w2, b = cw_ref[0:1, :], cw_ref[1:2, :], cw_ref[2:3, :], cb_ref[...]

        def chunk(r, c):
            r0, cur, x1, x2 = _gate_chunk(g_ref, r, rc)
            gc = w2 * cur + w1 * x1 + w0 * x2 + b
            a_ref[pl.ds(r0, rc), :] = (gc * jax.nn.sigmoid(gc) * u_ref[pl.ds(r0, rc), :]).astype(a_ref.dtype)
            return c

        lax.fori_loop(0, s // rc, chunk, 0)

    return _pcall(
        body, name=name, out_shape=jax.ShapeDtypeStruct((s, f), MXU_DTYPE), grid=(nj,),
        in_specs=[pl.BlockSpec((s, tc), lambda j: (0, j)), pl.BlockSpec((s, tc), lambda j: (0, j + nj)),
                  pl.BlockSpec((3, tc), lambda j: (0, j)), pl.BlockSpec((1, tc), lambda j: (0, j))],
        out_specs=pl.BlockSpec((s, tc), lambda j: (0, j)),
        compiler_params=_params("parallel"))(gu, gu, cw, _vec(cb))


def _ffn_gate_bwd(da, gu, cw, cb, name):
    s, f2 = gu.shape
    f = f2 // 2
    tc = _pick(f, (CONV_STRIPE,))
    rc = _pick(s, (CONV_ROWS, 128, 64, 32, 16, 8))
    nj = f // tc
    nr = s // rc

    def body(da_ref, g_ref, u_ref, cw_ref, cb_ref, dg_ref, du_ref, acc_ref, dgc_ref):
        w0, w1, w2, b = cw_ref[0:1, :], cw_ref[1:2, :], cw_ref[2:3, :], cb_ref[...]

        def chunk1(r, carry):
            a0, a1, a2, ab = carry
            r0, cur, x1, x2 = _gate_chunk(g_ref, r, rc)
            gc = w2 * cur + w1 * x1 + w0 * x2 + b
            sg = jax.nn.sigmoid(gc)
            da_c = da_ref[pl.ds(r0, rc), :]
            du_ref[pl.ds(r0, rc), :] = (da_c * (gc * sg)).astype(du_ref.dtype)
            dgc = da_c * u_ref[pl.ds(r0, rc), :] * (sg * (1.0 + gc * (1.0 - sg)))
            dgc_ref[pl.ds(r0, rc), :] = dgc
            return a0 + _fold8(dgc * x2), a1 + _fold8(dgc * x1), a2 + _fold8(dgc * cur), ab + _fold8(dgc)

        zero = jnp.zeros((SUBLANES, tc), F32)
        a0, a1, a2, ab = lax.fori_loop(0, nr, chunk1, (zero, zero, zero, zero))
        acc_ref[0], acc_ref[1], acc_ref[2], acc_ref[3] = a0, a1, a2, ab

        def chunk2(r, c):
            r0 = pl.multiple_of(r * rc, rc)
            cur = dgc_ref[pl.ds(r0, rc), :]
            hs = pl.multiple_of(jnp.minimum(r0 + rc, s - SUBLANES), SUBLANES)
            halo = jnp.where(r < nr - 1, dgc_ref[pl.ds(hs, SUBLANES), :], 0.0)
            dg = w2 * cur + w1 * _shift_up(cur, halo, 1) + w0 * _shift_up(cur, halo, 2)
            dg_ref[pl.ds(r0, rc), :] = dg.astype(dg_ref.dtype)
            return c

        lax.fori_loop(0, nr, chunk2, 0)

    stripe = pl.BlockSpec((s, tc), lambda j: (0, j))
    dg, du, acc = _pcall(
        body, name=name,
        out_shape=(jax.ShapeDtypeStruct((s, f), MXU_DTYPE), jax.ShapeDtypeStruct((s, f), MXU_DTYPE),
                   jax.ShapeDtypeStruct((4, SUBLANES, f), F32)),
        grid=(nj,),
        in_specs=[stripe, stripe, pl.BlockSpec((s, tc), lambda j: (0, j + nj)),
                  pl.BlockSpec((3, tc), lambda j: (0, j)), pl.BlockSpec((1, tc), lambda j: (0, j))],
        out_specs=(stripe, stripe, pl.BlockSpec((4, SUBLANES, tc), lambda j: (0, 0, j))),
        scratch_shapes=[pltpu.VMEM((s, tc), F32)],
        compiler_params=_params("parallel"))(da, gu, gu, cw, _vec(cb))
    acc = jnp.sum(acc, axis=1)
    return dg, du, acc[:3], acc[3]


def _add_rows(a, b, out_dtype, name):
    r = a.shape[0]
    tr = _pick(r, (1024, 512, 256, 128, 64, 32, 16, 8))

    def body(a_ref, b_ref, o_ref):
        o_ref[...] = (a_ref[...] + b_ref[...]).astype(o_ref.dtype)

    row = pl.BlockSpec((tr, LANES), lambda i: (i, 0))
    return _pcall(body, name=name, out_shape=jax.ShapeDtypeStruct(a.shape, out_dtype), grid=(r // tr,),
                  in_specs=[row, row], out_specs=row, compiler_params=_params("parallel"))(a, b)


def _adamw(gparts, w, m, v, name):
    r = w.shape[0]
    tr = _pick(r, (1024, 512, 256, 128, 64, 32, 16, 8))
    bc1 = 1.0 / (1.0 - ADAM_B1 ** ADAM_STEP)
    bc2 = 1.0 / (1.0 - ADAM_B2 ** ADAM_STEP)
    n = len(gparts)

    def body(*refs):
        w_ref, m_ref, v_ref, go_ref, d_ref, mo_ref, vo_ref = refs[n:]
        g = refs[0][...].astype(F32)
        for t in range(1, n):
            g = g + refs[t][...].astype(F32)
        mn = ADAM_B1 * m_ref[...] + (1.0 - ADAM_B1) * g
        vn = ADAM_B2 * v_ref[...] + (1.0 - ADAM_B2) * (g * g)
        m_hat = mn * bc1
        v_hat = vn * bc2
        go_ref[...] = g
        d_ref[...] = -ADAM_LR * (m_hat / (jnp.sqrt(v_hat) + ADAM_EPS) + ADAM_WD * w_ref[...])
        mo_ref[...] = mn
        vo_ref[...] = vn

    row = pl.BlockSpec((tr, LANES), lambda i: (i, 0))
    sds = jax.ShapeDtypeStruct((r, LANES), F32)
    return _pcall(
        body, name=name, out_shape=(sds, sds, sds, sds),
        grid=(r // tr,),
        in_specs=[pl.BlockSpec((None, tr, LANES), functools.partial(lambda slot, i: (slot, i, 0), slot))
                  for _, slot in gparts] + [row, row, row],
        out_specs=(row, row, row, row),
        compiler_params=_params("parallel"),
    )(*[a for a, _ in gparts], w, m, v)


def _sum_slots(gslots, name):
    n, r, _ = gslots.shape
    tr = _pick(r, (1024, 512, 256, 128, 64, 32, 16, 8))

    def body(g_ref, o_ref):
        g = g_ref[0]
        for t in range(1, n):
            g = g + g_ref[t]
        o_ref[...] = g

    return _pcall(
        body, name=name, out_shape=jax.ShapeDtypeStruct((r, LANES), F32),
        grid=(r // tr,),
        in_specs=[pl.BlockSpec((n, tr, LANES), lambda i: (0, i, 0))],
        out_specs=pl.BlockSpec((tr, LANES), lambda i: (i, 0)),
        compiler_params=_params("parallel"),
    )(gslots)


def _l2norm(x, eps=1e-6):
    return x * lax.rsqrt(jnp.sum(x * x, axis=-1, keepdims=True) + eps)


def _chunk_heads(t, n_heads, chunk):
    s, hd = t.shape
    return t.reshape(s // chunk, chunk, n_heads, hd // n_heads).transpose(2, 0, 1, 3)


def _unchunk_heads(t):
    h, n, c, d = t.shape
    return t.transpose(1, 2, 0, 3).reshape(n * c, h, d)


_NT = (((1,), (1,)), ((), ()))
_TN = (((0,), (0,)), ((), ()))


def _gdn_blocks(a, rev_from=None):
    h, _, r, c = a.shape
    if rev_from is None:
        return pl.BlockSpec((h, None, r, c), lambda n: (0, n, 0, 0))
    return pl.BlockSpec((h, None, r, c), lambda n: (0, rev_from - n, 0, 0))


def _gdn_scan_fwd_call(qg, w, u, qk, kd, e):
    H, n_chunks, c, dk = qg.shape
    dv = u.shape[-1]

    def body(qg_ref, w_ref, u_ref, qk_ref, kd_ref, e_ref, o_ref, sin_ref, vn_ref, state_ref):
        @pl.when(pl.program_id(0) == 0)
        def _():
            state_ref[...] = jnp.zeros_like(state_ref)

        for hd in range(H):
            st = state_ref[hd]
            stb = st.astype(MXU_DTYPE)
            sin_ref[hd] = st
            v_new = u_ref[hd] - jnp.dot(w_ref[hd].astype(MXU_DTYPE), stb, preferred_element_type=F32)
            vn_ref[hd] = v_new
            vnb = v_new.astype(MXU_DTYPE)
            o_ref[hd] = (jnp.dot(qg_ref[hd].astype(MXU_DTYPE), stb, preferred_element_type=F32)
                         + jnp.dot(qk_ref[hd].astype(MXU_DTYPE), vnb, preferred_element_type=F32))
            state_ref[hd] = st * e_ref[hd] + lax.dot_general(kd_ref[hd].astype(MXU_DTYPE), vnb, _TN,
                                                             preferred_element_type=F32)

    ins = (qg, w, u, qk, kd, e)
    outs = (jax.ShapeDtypeStruct((H, n_chunks, c, dv), F32), jax.ShapeDtypeStruct((H, n_chunks, dk, dv), F32),
            jax.ShapeDtypeStruct((H, n_chunks, c, dv), F32))
    return _pcall(
        body, name="gdn_scan_fwd", out_shape=outs, grid=(n_chunks,),
        in_specs=[_gdn_blocks(a) for a in ins], out_specs=tuple(_gdn_blocks(a) for a in outs),
        scratch_shapes=[pltpu.VMEM((H, dk, dv), F32)],
        compiler_params=_params("arbitrary"),
    )(*ins)


def _gdn_scan_bwd_call(qg, w, qk, kd, e, s_in, v_new, do):
    H, n_chunks, c, dk = qg.shape
    dv = v_new.shape[-1]

    def body(qg_ref, w_ref, qk_ref, kd_ref, e_ref, sin_ref, vn_ref, do_ref,
             dqg_ref, dw_ref, du_ref, dqk_ref, dkd_ref, de_ref, ds_ref):
        @pl.when(pl.program_id(0) == 0)
        def _():
            ds_ref[...] = jnp.zeros_like(ds_ref)

        for hd in range(H):
            st = sin_ref[hd]
            stb = st.astype(MXU_DTYPE)
            vnb = vn_ref[hd].astype(MXU_DTYPE)
            dob = do_ref[hd].astype(MXU_DTYPE)
            ds = ds_ref[hd]
            dsb = ds.astype(MXU_DTYPE)
            dvn = lax.dot_general(qk_ref[hd].astype(MXU_DTYPE), dob, _TN, preferred_element_type=F32)
            dqk_ref[hd] = lax.dot_general(dob, vnb, _NT, preferred_element_type=F32)
            dqg_ref[hd] = lax.dot_general(dob, stb, _NT, preferred_element_type=F32)
            ds_in = lax.dot_general(qg_ref[hd].astype(MXU_DTYPE), dob, _TN, preferred_element_type=F32)
            dvn = dvn + jnp.dot(kd_ref[hd].astype(MXU_DTYPE), dsb, preferred_element_type=F32)
            dkd_ref[hd] = lax.dot_general(vnb, dsb, _NT, preferred_element_type=F32)
            de_ref[hd] = _fold8(st * ds)
            ds_in = ds_in + ds * e_ref[hd]
            du_ref[hd] = dvn
            dvnb = dvn.astype(MXU_DTYPE)
            dw_ref[hd] = -lax.dot_general(dvnb, stb, _NT, preferred_element_type=F32)
            ds_ref[hd] = ds_in - lax.dot_general(w_ref[hd].astype(MXU_DTYPE), dvnb, _TN,
                                                 preferred_element_type=F32)

    last = n_chunks - 1
    ins = (qg, w, qk, kd, e, s_in, v_new, do)
    outs = (jax.ShapeDtypeStruct(qg.shape, F32), jax.ShapeDtypeStruct(w.shape, F32),
            jax.ShapeDtypeStruct(v_new.shape, F32), jax.ShapeDtypeStruct(qk.shape, F32),
            jax.ShapeDtypeStruct(kd.shape, F32), jax.ShapeDtypeStruct((H, n_chunks, SUBLANES, dv), F32))
    return _pcall(
        body, name="gdn_scan_bwd", out_shape=outs, grid=(n_chunks,),
        in_specs=[_gdn_blocks(a, last) for a in ins], out_specs=tuple(_gdn_blocks(a, last) for a in outs),
        scratch_shapes=[pltpu.VMEM((H, dk, dv), F32)],
        compiler_params=_params("arbitrary"),
    )(*ins)


@jax.custom_vjp
def _gdn_scan(qg, w, u, qk, kd, e):
    return _gdn_scan_fwd_call(qg, w, u, qk, kd, e)[0]


def _gdn_scan_fwd(qg, w, u, qk, kd, e):
    o, s_in, v_new = _gdn_scan_fwd_call(qg, w, u, qk, kd, e)
    return o, (qg, w, qk, kd, e, s_in, v_new)


def _gdn_scan_bwd(res, do):
    dqg, dw, du, dqk, dkd, de = _gdn_scan_bwd_call(*res, do)
    return dqg, dw, du, dqk, dkd, jnp.sum(de, axis=2, keepdims=True)


_gdn_scan.defvjp(_gdn_scan_fwd, _gdn_scan_bwd)


def _conv_taps(x_ref, r, rc, taps):
    r0 = pl.multiple_of(r * rc, rc)
    cur = x_ref[pl.ds(r0, rc), :]
    hs = pl.multiple_of(jnp.maximum(r0 - SUBLANES, 0), SUBLANES)
    halo = jnp.where(r > 0, x_ref[pl.ds(hs, SUBLANES), :], 0.0)
    return r0, [cur] + [_shift_down(cur, halo, k) for k in range(1, taps)]


def _conv_pre(w_ref, xs):
    taps = len(xs)
    gc = w_ref[taps - 1:taps, :] * xs[0]
    for j in range(taps - 1):
        gc = gc + w_ref[j:j + 1, :] * xs[taps - 1 - j]
    return gc


def _dwconv_silu_fwd_call(x, w):
    s, c = x.shape
    taps = w.shape[0]
    tc = _pick(c, (CONV_STRIPE,))
    rc = _pick(s, (CONV_ROWS, 128, 64, 32, 16, 8))

    def body(x_ref, w_ref, o_ref):
        def chunk(r, carry):
            r0, xs = _conv_taps(x_ref, r, rc, taps)
            gc = _conv_pre(w_ref, xs)
            o_ref[pl.ds(r0, rc), :] = gc * jax.nn.sigmoid(gc)
            return carry

        lax.fori_loop(0, s // rc, chunk, 0)

    stripe = pl.BlockSpec((s, tc), lambda j: (0, j))
    return _pcall(body, name="gdn_conv_fwd", out_shape=jax.ShapeDtypeStruct((s, c), F32), grid=(c // tc,),
                  in_specs=[stripe, pl.BlockSpec((taps, tc), lambda j: (0, j))], out_specs=stripe,
                  compiler_params=_params("parallel"))(x, w)


def _dwconv_silu_bwd_call(x, w, dy):
    s, c = x.shape
    taps = w.shape[0]
    tc = _pick(c, (CONV_STRIPE,))
    rc = _pick(s, (CONV_ROWS, 128, 64, 32, 16, 8))
    nr = s // rc

    def body(x_ref, w_ref, dy_ref, dx_ref, acc_ref, dgc_ref):
        def chunk1(r, acc):
            r0, xs = _conv_taps(x_ref, r, rc, taps)
            gc = _conv_pre(w_ref, xs)
            sg = jax.nn.sigmoid(gc)
            dgc = dy_ref[pl.ds(r0, rc), :] * (sg * (1.0 + gc * (1.0 - sg)))
            dgc_ref[pl.ds(r0, rc), :] = dgc
            return tuple(acc[j] + _fold8(dgc * xs[taps - 1 - j]) for j in range(taps))

        zero = jnp.zeros((SUBLANES, tc), F32)
        acc = lax.fori_loop(0, nr, chunk1, (zero,) * taps)
        for j in range(taps):
            acc_ref[j] = acc[j]

        def chunk2(r, carry):
            r0 = pl.multiple_of(r * rc, rc)
            cur = dgc_ref[pl.ds(r0, rc), :]
            hs = pl.multiple_of(jnp.minimum(r0 + rc, s - SUBLANES), SUBLANES)
            halo = jnp.where(r < nr - 1, dgc_ref[pl.ds(hs, SUBLANES), :], 0.0)
            dx = w_ref[taps - 1:taps, :] * cur
            for j in range(taps - 1):
                dx = dx + w_ref[j:j + 1, :] * _shift_up(cur, halo, taps - 1 - j)
            dx_ref[pl.ds(r0, rc), :] = dx
            return carry

        lax.fori_loop(0, nr, chunk2, 0)

    stripe = pl.BlockSpec((s, tc), lambda j: (0, j))
    dx, acc = _pcall(
        body, name="gdn_conv_bwd",
        out_shape=(jax.ShapeDtypeStruct((s, c), F32), jax.ShapeDtypeStruct((taps, SUBLANES, c), F32)),
        grid=(c // tc,),
        in_specs=[stripe, pl.BlockSpec((taps, tc), lambda j: (0, j)), stripe],
        out_specs=(stripe, pl.BlockSpec((taps, SUBLANES, tc), lambda j: (0, 0, j))),
        scratch_shapes=[pltpu.VMEM((s, tc), F32)],
        compiler_params=_params("parallel"))(x, w, dy)
    return dx, jnp.sum(acc, axis=1)


@jax.custom_vjp
def _dwconv_silu(x, w):
    return _dwconv_silu_fwd_call(x, w)


def _dwconv_silu_fwd(x, w):
    return _dwconv_silu_fwd_call(x, w), (x, w)


def _dwconv_silu_bwd(res, dy):
    return _dwconv_silu_bwd_call(*res, dy)


_dwconv_silu.defvjp(_dwconv_silu_fwd, _dwconv_silu_bwd)


def _gated_deltanet(hx, p):
    H, C = GDN_HEADS, GDN_CHUNK
    s, d = hx.shape
    dk = dv = d // H
    qkvz = _linear(hx, p['gdn_w_qkvz'], "gdn_in")
    ab = _linear(hx, p['gdn_w_ab'], "gdn_ab")
    qkv, z = qkvz[:, :3 * d], qkvz[:, 3 * d:]
    a, bt = ab[:, :H], ab[:, H:2 * H]
    qkv = _dwconv_silu(qkv, p['gdn_conv_w'])
    q, k, v = qkv[:, :d], qkv[:, d:2 * d], qkv[:, 2 * d:]
    q = _l2norm(_chunk_heads(q, H, C)) * (dk ** -0.5)
    k = _l2norm(_chunk_heads(k, H, C))
    v = _chunk_heads(v, H, C)
    beta = jax.nn.sigmoid(_chunk_heads(bt, H, C)[..., 0])
    g = -jnp.exp(p['gdn_a_log'])[:, None, None] * jax.nn.softplus(
        _chunk_heads(a, H, C)[..., 0] + p['gdn_dt_bias'][:, None, None])
    gc = jnp.cumsum(g, axis=-1)
    idx = jnp.arange(C)
    causal = idx[:, None] >= idx[None, :]
    strict = idx[:, None] > idx[None, :]
    diff = gc[..., :, None] - gc[..., None, :]
    decay = jnp.where(causal, jnp.exp(jnp.where(causal, diff, 0.0)), 0.0)
    kb = k * beta[..., None]
    kk = jnp.where(strict, jnp.einsum('hncd,hnmd->hncm', kb, k) * decay, 0.0)
    eye = jnp.eye(C, dtype=F32)
    rhs = jnp.concatenate([v * beta[..., None], kb * jnp.exp(gc)[..., None]], axis=-1)
    sol = lax.linalg.triangular_solve(kk + eye, rhs, left_side=True, lower=True, unit_diagonal=True)
    u, w = sol[..., :dv], sol[..., dv:]
    qk = jnp.where(causal, jnp.einsum('hncd,hnmd->hncm', q, k) * decay, 0.0)

    g_last = gc[..., -1:]
    e = jnp.broadcast_to(jnp.exp(g_last)[..., None], gc.shape[:2] + (1, dv))
    o = _gdn_scan(q * jnp.exp(gc)[..., None], w, u, qk, k * jnp.exp(g_last - gc)[..., None], e)
    o = _unchunk_heads(o)
    o = o * lax.rsqrt(jnp.mean(o * o, axis=-1, keepdims=True) + 1e-6) * p['gdn_norm_w']
    o = o * jax.nn.silu(z.reshape(s, H, dv))
    return _linear(o.reshape(s, H * dv), p['gdn_w_out'], "gdn_out")


def _ret_consts(c):
    log_gamma = jnp.log(1.0 - jnp.power(2.0, -5.0 - jnp.arange(RET_HEADS, dtype=F32)))
    idx = jnp.arange(c, dtype=F32)
    rel = idx[:, None] - idx[None, :]
    dmask = jnp.where(rel >= 0, jnp.exp(jnp.maximum(rel, 0.0) * log_gamma[:, None, None]), 0.0)
    zeta = jnp.exp((c - 1.0 - idx)[None, :] * log_gamma[:, None])[..., None]
    xi = jnp.exp((idx + 1.0)[None, :] * log_gamma[:, None])[..., None]
    gamma_c = jnp.exp(c * log_gamma)[:, None, None]
    return dmask, zeta, xi, gamma_c


def _ret_angles(s, dk):
    pos = jnp.arange(s, dtype=F32)
    inv_freq = RET_ROPE_BASE ** (-jnp.linspace(0.0, 1.0, dk // 2, dtype=F32))
    ang = pos[:, None] * inv_freq[None, :]
    return jnp.cos(ang), jnp.sin(ang)


def _rot(t, cs, sn):
    half = t.shape[1] // 2
    t1, t2 = t[:, :half], t[:, half:]
    return jnp.concatenate([t1 * cs - t2 * sn, t1 * sn + t2 * cs], axis=1)


def _rot_t(t, cs, sn):
    half = t.shape[1] // 2
    t1, t2 = t[:, :half], t[:, half:]
    return jnp.concatenate([t1 * cs + t2 * sn, t2 * cs - t1 * sn], axis=1)


def _ret_cols(d, dk, dv, hd):
    return (slice(hd * dk, (hd + 1) * dk), slice(d + hd * dk, d + (hd + 1) * dk),
            slice(2 * d + hd * dv, 2 * d + (hd + 1) * dv), slice(4 * d + hd * dv, 4 * d + (hd + 1) * dv))


def _ret_fwd_call(proj):
    s, d6 = proj.shape
    d = d6 // 6
    H, c = RET_HEADS, RET_CHUNK
    dk, dv = d // H, 2 * d // H
    n_chunks = s // c
    kscale = dk ** -0.5
    cos_a, sin_a = _ret_angles(s, dk)
    consts = _ret_consts(c)

    def body(p_ref, cos_ref, sin_ref, dm_ref, ze_ref, xi_ref, gc_ref, out_ref, oraw_ref, st_ref, state_ref):
        @pl.when(pl.program_id(0) == 0)
        def _():
            state_ref[...] = jnp.zeros_like(state_ref)

        cs, sn = cos_ref[...], sin_ref[...]
        for hd in range(H):
            qc, kc, vc, gcol = _ret_cols(d, dk, dv, hd)
            ocol = slice(hd * dv, (hd + 1) * dv)
            qb = _rot(p_ref[:, qc], cs, sn).astype(MXU_DTYPE)
            kr = _rot(p_ref[:, kc], cs, sn) * kscale
            kb = kr.astype(MXU_DTYPE)
            vb = p_ref[:, vc].astype(MXU_DTYPE)
            st = state_ref[hd]
            stb = st.astype(MXU_DTYPE)
            st_ref[hd] = stb
            sc = lax.dot_general(qb, kb, _NT, preferred_element_type=F32) * dm_ref[hd]
            o = (jnp.dot(sc.astype(MXU_DTYPE), vb, preferred_element_type=F32)
                 + jnp.dot(qb, stb, preferred_element_type=F32) * xi_ref[hd])
            state_ref[hd] = st * gc_ref[hd] + lax.dot_general((kr * ze_ref[hd]).astype(MXU_DTYPE), vb, _TN,
                                                              preferred_element_type=F32)
            oraw_ref[:, ocol] = o
            oc = o - jnp.mean(o, axis=-1, keepdims=True)
            on = oc * lax.rsqrt(jnp.mean(oc * oc, axis=-1, keepdims=True) + 1e-6)
            gate = p_ref[:, gcol]
            out_ref[:, ocol] = on * (gate * jax.nn.sigmoid(gate))

    row = lambda width: pl.BlockSpec((c, width), lambda n: (n, 0))
    whole = lambda a: pl.BlockSpec(a.shape, lambda n: (0,) * a.ndim)
    return _pcall(
        body, name="ret_fwd",
        out_shape=(jax.ShapeDtypeStruct((s, 2 * d), F32), jax.ShapeDtypeStruct((s, 2 * d), F32),
                   jax.ShapeDtypeStruct((n_chunks, H, dk, dv), MXU_DTYPE)),
        grid=(n_chunks,),
        in_specs=[row(d6), row(dk // 2), row(dk // 2)] + [whole(a) for a in consts],
        out_specs=(row(2 * d), row(2 * d), pl.BlockSpec((None, H, dk, dv), lambda n: (n, 0, 0, 0))),
        scratch_shapes=[pltpu.VMEM((H, dk, dv), F32)],
        compiler_params=_params("arbitrary"),
    )(proj, cos_a, sin_a, *consts)


def _ret_bwd_call(proj, oraw, states, dout):
    s, d6 = proj.shape
    d = d6 // 6
    H, c = RET_HEADS, RET_CHUNK
    dk, dv = d // H, 2 * d // H
    n_chunks = s // c
    kscale = dk ** -0.5
    cos_a, sin_a = _ret_angles(s, dk)
    consts = _ret_consts(c)

    def body(p_ref, cos_ref, sin_ref, dm_ref, ze_ref, xi_ref, gc_ref, oraw_ref, st_ref, do_ref, dp_ref, ds_ref):
        @pl.when(pl.program_id(0) == 0)
        def _():
            ds_ref[...] = jnp.zeros_like(ds_ref)

        cs, sn = cos_ref[...], sin_ref[...]
        for hd in range(H):
            qc, kc, vc, gcol = _ret_cols(d, dk, dv, hd)
            ocol = slice(hd * dv, (hd + 1) * dv)
            qb = _rot(p_ref[:, qc], cs, sn).astype(MXU_DTYPE)
            kr = _rot(p_ref[:, kc], cs, sn) * kscale
            kb = kr.astype(MXU_DTYPE)
            vb = p_ref[:, vc].astype(MXU_DTYPE)
            gate = p_ref[:, gcol]
            o = oraw_ref[:, ocol]
            oc = o - jnp.mean(o, axis=-1, keepdims=True)
            rstd = lax.rsqrt(jnp.mean(oc * oc, axis=-1, keepdims=True) + 1e-6)
            on = oc * rstd
            dout_h = do_ref[:, ocol]
            sg = jax.nn.sigmoid(gate)
            dp_ref[:, gcol] = dout_h * on * (sg * (1.0 + gate * (1.0 - sg)))
            don = dout_h * (gate * sg)
            do_raw = rstd * (don - jnp.mean(don, axis=-1, keepdims=True)
                             - on * jnp.mean(don * on, axis=-1, keepdims=True))
            dob = do_raw.astype(MXU_DTYPE)
            stb = st_ref[hd]
            ds = ds_ref[hd]
            dsb = ds.astype(MXU_DTYPE)
            dm = dm_ref[hd]
            scb = (lax.dot_general(qb, kb, _NT, preferred_element_type=F32) * dm).astype(MXU_DTYPE)
            dsc = (lax.dot_general(dob, vb, _NT, preferred_element_type=F32) * dm).astype(MXU_DTYPE)
            dqr = jnp.dot(dsc, kb, preferred_element_type=F32)
            dkr = lax.dot_general(dsc, qb, _TN, preferred_element_type=F32)
            dvv = lax.dot_general(scb, dob, _TN, preferred_element_type=F32)
            doi = (do_raw * xi_ref[hd]).astype(MXU_DTYPE)
            dqr = dqr + lax.dot_general(doi, stb, _NT, preferred_element_type=F32)
            ds_in = lax.dot_general(qb, doi, _TN, preferred_element_type=F32)
            ze = ze_ref[hd]
            dkr = dkr + lax.dot_general(vb, dsb, _NT, preferred_element_type=F32) * ze
            dvv = dvv + jnp.dot((kr * ze).astype(MXU_DTYPE), dsb, preferred_element_type=F32)
            ds_ref[hd] = ds * gc_ref[hd] + ds_in
            dp_ref[:, qc] = _rot_t(dqr, cs, sn)
            dp_ref[:, kc] = _rot_t(dkr * kscale, cs, sn)
            dp_ref[:, vc] = dvv

    last = n_chunks - 1
    row = lambda width: pl.BlockSpec((c, width), lambda n: (last - n, 0))
    whole = lambda a: pl.BlockSpec(a.shape, lambda n: (0,) * a.ndim)
    return _pcall(
        body, name="ret_bwd", out_shape=jax.ShapeDtypeStruct((s, d6), F32),
        grid=(n_chunks,),
        in_specs=[row(d6), row(dk // 2), row(dk // 2)] + [whole(a) for a in consts]
                 + [row(2 * d), pl.BlockSpec((None, H, dk, dv), lambda n: (last - n, 0, 0, 0)), row(2 * d)],
        out_specs=row(d6),
        scratch_shapes=[pltpu.VMEM((H, dk, dv), F32)],
        compiler_params=_params("arbitrary"),
    )(proj, cos_a, sin_a, *consts, oraw, states, dout)


@jax.custom_vjp
def _ret_core(proj):
    return _ret_fwd_call(proj)[0]


def _ret_core_fwd(proj):
    out, oraw, states = _ret_fwd_call(proj)
    return out, (proj, oraw, states)


def _ret_core_bwd(res, dout):
    return (_ret_bwd_call(*res, dout),)


_ret_core.defvjp(_ret_core_fwd, _ret_core_bwd)


def _retention(hx, p):
    return _linear(_ret_core(_linear(hx, p['ret_w_in'], "ret_in")), p['ret_w_out'], "ret_out")


SQRT_HALF = 2.0 ** -0.5
INV_SQRT_2PI = (2.0 * math.pi) ** -0.5


def _gmlp_front(p_ref, g_ref, b_ref, w):
    x = p_ref[...]
    cdf = 0.5 * (1.0 + lax.erf(x * SQRT_HALF))
    uv = x * cdf
    u, v = uv[:, :w], uv[:, w:]
    vc = v - jnp.mean(v, axis=-1, keepdims=True)
    rstd = lax.rsqrt(jnp.mean(vc * vc, axis=-1, keepdims=True) + LN_EPS)
    vhat = vc * rstd
    return x, cdf, u, vhat, rstd, vhat * g_ref[...] + b_ref[...]


def _gmlp_fwd_call(proj, ln_g, ln_b, ws, bs):
    s, w2 = proj.shape
    w = w2 // 2
    c, G = GMLP_CHUNK, GMLP_GROUPS
    gw = w // G

    def body(p_ref, g_ref, b_ref, ws_ref, bs_ref, o_ref):
        _, _, u, _, _, vn = _gmlp_front(p_ref, g_ref, b_ref, w)
        for gi in range(G):
            cols = slice(gi * gw, (gi + 1) * gw)
            vs = jnp.dot(ws_ref[gi].astype(MXU_DTYPE), vn[:, cols].astype(MXU_DTYPE),
                         preferred_element_type=F32) + bs_ref[gi]
            o_ref[:, cols] = u[:, cols] * vs

    whole = lambda a: pl.BlockSpec(a.shape, lambda n: (0,) * a.ndim)
    args = (_vec(ln_g), _vec(ln_b), ws, bs)
    return _pcall(
        body, name="gmlp_fwd", out_shape=jax.ShapeDtypeStruct((s, w), F32), grid=(s // c,),
        in_specs=[pl.BlockSpec((c, w2), lambda n: (n, 0))] + [whole(a) for a in args],
        out_specs=pl.BlockSpec((c, w), lambda n: (n, 0)),
        compiler_params=_params("parallel"),
    )(proj, *args)


def _gmlp_bwd_call(proj, ln_g, ln_b, ws, bs, dout):
    s, w2 = proj.shape
    w = w2 // 2
    c, G = GMLP_CHUNK, GMLP_GROUPS
    gw = w // G

    def body(p_ref, g_ref, b_ref, ws_ref, bs_ref, do_ref, dp_ref, dws_ref, dbs_ref, dgb_ref):
        @pl.when(pl.program_id(0) == 0)
        def _():
            dws_ref[...] = jnp.zeros_like(dws_ref)
            dbs_ref[...] = jnp.zeros_like(dbs_ref)
            dgb_ref[...] = jnp.zeros_like(dgb_ref)

        x, cdf, u, vhat, rstd, vn = _gmlp_front(p_ref, g_ref, b_ref, w)
        dout = do_ref[...]
        du_parts, dvn_parts = [], []
        for gi in range(G):
            cols = slice(gi * gw, (gi + 1) * gw)
            wsg = ws_ref[gi].astype(MXU_DTYPE)
            vng = vn[:, cols].astype(MXU_DTYPE)
            vs = jnp.dot(wsg, vng, preferred_element_type=F32) + bs_ref[gi]
            du_parts.append(dout[:, cols] * vs)
            dvs = dout[:, cols] * u[:, cols]
            dbs_ref[:, cols] += dvs
            dvsb = dvs.astype(MXU_DTYPE)
            dws_ref[gi] += lax.dot_general(dvsb, vng, _NT, preferred_element_type=F32)
            dvn_parts.append(lax.dot_general(wsg, dvsb, _TN, preferred_element_type=F32))
        dvn = jnp.concatenate(dvn_parts, axis=1)
        dgb_ref[0] += _fold8(dvn * vhat)
        dgb_ref[1] += _fold8(dvn)
        dvh = dvn * g_ref[...]
        dv = rstd * (dvh - jnp.mean(dvh, axis=-1, keepdims=True)
                     - vhat * jnp.mean(dvh * vhat, axis=-1, keepdims=True))
        duv = jnp.concatenate(du_parts + [dv], axis=1)
        dp_ref[...] = duv * (cdf + x * (jnp.exp(-0.5 * x * x) * INV_SQRT_2PI))

    whole = lambda a: pl.BlockSpec(a.shape, lambda n: (0,) * a.ndim)
    args = (_vec(ln_g), _vec(ln_b), ws, bs)
    acc = lambda *shape: pl.BlockSpec(shape, lambda n: (0,) * len(shape))
    return _pcall(
        body, name="gmlp_bwd",
        out_shape=(jax.ShapeDtypeStruct((s, w2), F32), jax.ShapeDtypeStruct((G, c, c), F32),
                   jax.ShapeDtypeStruct((c, w), F32), jax.ShapeDtypeStruct((2, SUBLANES, w), F32)),
        grid=(s // c,),
        in_specs=[pl.BlockSpec((c, w2), lambda n: (n, 0))] + [whole(a) for a in args]
                 + [pl.BlockSpec((c, w), lambda n: (n, 0))],
        out_specs=(pl.BlockSpec((c, w2), lambda n: (n, 0)), acc(G, c, c), acc(c, w), acc(2, SUBLANES, w)),
        compiler_params=_params("arbitrary"),
    )(proj, *args, dout)


def _gmlp_mask(c):
    return jnp.tril(jnp.ones((c, c), dtype=bool))


@jax.custom_vjp
def _gmlp_core(proj, ln_g, ln_b, w_s, b_s):
    ws = jnp.where(_gmlp_mask(GMLP_CHUNK), w_s, 0.0)
    return _gmlp_fwd_call(proj, ln_g, ln_b, ws, b_s[..., None])


def _gmlp_core_fwd(proj, ln_g, ln_b, w_s, b_s):
    return _gmlp_core(proj, ln_g, ln_b, w_s, b_s), (proj, ln_g, ln_b, w_s, b_s)


def _gmlp_core_bwd(res, dout):
    proj, ln_g, ln_b, w_s, b_s = res
    mask = _gmlp_mask(GMLP_CHUNK)
    dproj, dws, dbs, dgb = _gmlp_bwd_call(proj, ln_g, ln_b, jnp.where(mask, w_s, 0.0), b_s[..., None], dout)
    dgb = jnp.sum(dgb, axis=1)
    c = GMLP_CHUNK
    db_s = jnp.sum(dbs.reshape(c, GMLP_GROUPS, -1), axis=-1).T
    return dproj, dgb[0], dgb[1], jnp.where(mask, dws, 0.0), db_s


_gmlp_core.defvjp(_gmlp_core_fwd, _gmlp_core_bwd)


def _chunked_gmlp(hx, p):
    core = _gmlp_core(_linear(hx, p['gmlp_w_in'], "gmlp_in"), p['gmlp_ln_g'], p['gmlp_ln_b'],
                      p['gmlp_w_s'], p['gmlp_b_s'])
    return _linear(core, p['gmlp_w_out'], "gmlp_out")


def _stick_breaking(hx, p):
    H = SB_HEADS
    s, d = hx.shape
    dh = d // H
    qkv = _linear(hx, p['sb_w_in'], "sb_in")
    q, k, v = (qkv[:, j * d:(j + 1) * d].reshape(s, H, dh).transpose(1, 0, 2) for j in range(3))
    o = _sb_core(q, k, v)
    return _linear(o.transpose(1, 0, 2).reshape(s, d), p['sb_w_out'], "sb_out")


MIXERS = (_gated_deltanet, _retention, _chunked_gmlp, _stick_breaking)


def _trunk_grad(x, mods, p, target):
    d = x.shape[-1]
    saved = []
    for i in range(DEPTH):
        sh1, sc1, g1, sh2, sc2, g2 = (mods[i, j * d:(j + 1) * d] for j in range(6))
        h1 = _modulate(x, sc1, sh1, F32, "mod_a%d" % i)
        y1, mixer_vjp = jax.vjp(MIXERS[i], h1, {n: p[n] for n in MIXER_PARAMS[i]})
        x1 = _resid_ln(x, y1, g1, p['ln_g'][i, 0], p['ln_b'][i, 0], "ln_a%d" % i)
        h2 = _modulate(x1, sc2, sh2, MXU_DTYPE, "mod_b%d" % i)
        gu = _mm(h2, p['ffn_up'][i], 'nn', "ffn_up%d_fwd" % i)
        act = _ffn_gate(gu, p['ffn_conv_w'][i], p['ffn_conv_b'][i], "ffn_gate%d" % i)
        y2 = _mm(act, p['ffn_down'][i], 'nn', "ffn_down%d_fwd" % i)
        x2 = _resid_ln(x1, y2, g2, p['ln_g'][i, 1], p['ln_b'][i, 1], "ln_b%d" % i)
        saved.append((x, y1, mixer_vjp, x1, h2, gu, act, y2))
        x = x2
    loss, dx = _loss_head(x, target, "loss_head")

    dp = {n: None for n in p}
    d_ln_g, d_ln_b, d_up, d_down, d_cw, d_cb, dmods = [], [], [], [], [], [], []
    for i in reversed(range(DEPTH)):
        x0, y1, mixer_vjp, x1, h2, gu, act, y2 = saved[i]
        sh1, sc1, g1, sh2, sc2, g2 = (mods[i, j * d:(j + 1) * d] for j in range(6))
        dxa, dy2, dgam2, dbet2, dg2 = _resid_ln_bwd(x1, y2, g2, p['ln_g'][i, 1], dx, "ln_b%d_bwd" % i)
        dact = _mm(dy2, p['ffn_down'][i], 'nt', "ffn_down%d_dx" % i)
        d_down.append(_mm(act, dy2, 'tn', "ffn_down%d_dw" % i))
        dgate, dupp, dcw, dcb = _ffn_gate_bwd(dact, gu, p['ffn_conv_w'][i], p['ffn_conv_b'][i],
                                              "ffn_gate%d_bwd" % i)
        dgu = jnp.concatenate([dgate, dupp], axis=1)
        dh2 = _mm(dgu, p['ffn_up'][i], 'nt', "ffn_up%d_dx" % i)
        d_up.append(_mm(h2, dgu, 'tn', "ffn_up%d_dw" % i))
        dx1, dsc2, dsh2 = _modulate_bwd(dxa, dh2, x1, sc2, "mod_b%d_bwd" % i)
        dxa, dy1, dgam1, dbet1, dg1 = _resid_ln_bwd(x0, y1, g1, p['ln_g'][i, 0], dx1, "ln_a%d_bwd" % i)
        dh1, dmix = mixer_vjp(dy1)
        dp.update(dmix)
        dx, dsc1, dsh1 = _modulate_bwd(dxa, dh1, x0, sc1, "mod_a%d_bwd" % i)
        d_ln_g.append(jnp.stack([dgam1, dgam2]))
        d_ln_b.append(jnp.stack([dbet1, dbet2]))
        d_cw.append(dcw)
        d_cb.append(dcb)
        dmods.append(jnp.concatenate([dsh1, dsc1, dg1, dsh2, dsc2, dg2]))
    for n, parts in (('ln_g', d_ln_g), ('ln_b', d_ln_b), ('ffn_up', d_up), ('ffn_down', d_down),
                     ('ffn_conv_w', d_cw), ('ffn_conv_b', d_cb)):
        dp[n] = jnp.stack(parts[::-1])
    return loss, dx, jnp.stack(dmods[::-1]), dp


def _join(blocks, axis):
    return jnp.concatenate([blocks[d] for d in range(N_DEV)], axis=axis)


def _split(whole, axis):
    n = whole.shape[axis] // N_DEV
    return jnp.stack([lax.slice_in_dim(whole, d * n, (d + 1) * n, axis=axis) for d in range(N_DEV)])


def _pad8(a):
    pad = (-a.shape[0]) % 8
    return jnp.pad(a, ((0, pad), (0, 0))) if pad else a


def _pack_big_grads(full_grads, axes):
    per_dev = jnp.concatenate([_split(g, ax).reshape(N_DEV, -1) for g, ax in zip(full_grads, axes)], axis=1)
    pad = (-per_dev.shape[1]) % (BIG_ROW_ALIGN * LANES)
    if pad:
        per_dev = jnp.pad(per_dev, ((0, 0), (0, pad)))
    return per_dev.reshape(N_DEV, -1, LANES)


def kernel(x, c, cond_w, cond_b, ada_w, ada_b, ln_g, ln_b, ffn_up, ffn_conv_w, ffn_conv_b, ffn_down, gdn_w_in, gdn_conv_w, gdn_a_log, gdn_dt_bias, gdn_norm_w, gdn_w_out, ret_w_in, ret_w_out, gmlp_w_in, gmlp_ln_g, gmlp_ln_b, gmlp_w_s, gmlp_b_s, gmlp_w_out, sb_w_in, sb_w_out, loss_target, m_cond_w, m_cond_b, m_ada_w, m_ada_b, m_ln_g, m_ln_b, m_ffn_up, m_ffn_conv_w, m_ffn_conv_b, m_ffn_down, m_gdn_w_in, m_gdn_conv_w, m_gdn_a_log, m_gdn_dt_bias, m_gdn_norm_w, m_gdn_w_out, m_ret_w_in, m_ret_w_out, m_gmlp_w_in, m_gmlp_ln_g, m_gmlp_ln_b, m_gmlp_w_s, m_gmlp_b_s, m_gmlp_w_out, m_sb_w_in, m_sb_w_out, v_cond_w, v_cond_b, v_ada_w, v_ada_b, v_ln_g, v_ln_b, v_ffn_up, v_ffn_conv_w, v_ffn_conv_b, v_ffn_down, v_gdn_w_in, v_gdn_conv_w, v_gdn_a_log, v_gdn_dt_bias, v_gdn_norm_w, v_gdn_w_out, v_ret_w_in, v_ret_w_out, v_gmlp_w_in, v_gmlp_ln_g, v_gmlp_ln_b, v_gmlp_w_s, v_gmlp_b_s, v_gmlp_w_out, v_sb_w_in, v_sb_w_out):
    w = dict(cond_w=cond_w, cond_b=cond_b, ada_w=ada_w, ada_b=ada_b, ln_g=ln_g, ln_b=ln_b, ffn_up=ffn_up,
             ffn_conv_w=ffn_conv_w, ffn_conv_b=ffn_conv_b, ffn_down=ffn_down, gdn_w_in=gdn_w_in,
             gdn_conv_w=gdn_conv_w, gdn_a_log=gdn_a_log, gdn_dt_bias=gdn_dt_bias, gdn_norm_w=gdn_norm_w,
             gdn_w_out=gdn_w_out, ret_w_in=ret_w_in, ret_w_out=ret_w_out, gmlp_w_in=gmlp_w_in,
             gmlp_ln_g=gmlp_ln_g, gmlp_ln_b=gmlp_ln_b, gmlp_w_s=gmlp_w_s, gmlp_b_s=gmlp_b_s,
             gmlp_w_out=gmlp_w_out, sb_w_in=sb_w_in, sb_w_out=sb_w_out)
    mom = dict(cond_w=m_cond_w, cond_b=m_cond_b, ada_w=m_ada_w, ada_b=m_ada_b, ln_g=m_ln_g, ln_b=m_ln_b,
               ffn_up=m_ffn_up, ffn_conv_w=m_ffn_conv_w, ffn_conv_b=m_ffn_conv_b, ffn_down=m_ffn_down,
               gdn_w_in=m_gdn_w_in, gdn_conv_w=m_gdn_conv_w, gdn_a_log=m_gdn_a_log, gdn_dt_bias=m_gdn_dt_bias,
               gdn_norm_w=m_gdn_norm_w, gdn_w_out=m_gdn_w_out, ret_w_in=m_ret_w_in, ret_w_out=m_ret_w_out,
               gmlp_w_in=m_gmlp_w_in, gmlp_ln_g=m_gmlp_ln_g, gmlp_ln_b=m_gmlp_ln_b, gmlp_w_s=m_gmlp_w_s,
               gmlp_b_s=m_gmlp_b_s, gmlp_w_out=m_gmlp_w_out, sb_w_in=m_sb_w_in, sb_w_out=m_sb_w_out)
    var = dict(cond_w=v_cond_w, cond_b=v_cond_b, ada_w=v_ada_w, ada_b=v_ada_b, ln_g=v_ln_g, ln_b=v_ln_b,
               ffn_up=v_ffn_up, ffn_conv_w=v_ffn_conv_w, ffn_conv_b=v_ffn_conv_b, ffn_down=v_ffn_down,
               gdn_w_in=v_gdn_w_in, gdn_conv_w=v_gdn_conv_w, gdn_a_log=v_gdn_a_log, gdn_dt_bias=v_gdn_dt_bias,
               gdn_norm_w=v_gdn_norm_w, gdn_w_out=v_gdn_w_out, ret_w_in=v_ret_w_in, ret_w_out=v_ret_w_out,
               gmlp_w_in=v_gmlp_w_in, gmlp_ln_g=v_gmlp_ln_g, gmlp_ln_b=v_gmlp_ln_b, gmlp_w_s=v_gmlp_w_s,
               gmlp_b_s=v_gmlp_b_s, gmlp_w_out=v_gmlp_w_out, sb_w_in=v_sb_w_in, sb_w_out=v_sb_w_out)

    me = _my_id()
    x = x[0]
    target = loss_target[0]
    d = x.shape[-1]
    dsh = d // N_DEV
    msh = ada_w.shape[-1]

    c_all = _exchange(_pad8(c), False, "gather_c")[:, 0, :]
    c_mine = lax.dynamic_slice_in_dim(c_all, me * dsh, dsh, axis=1)
    pre_part = _mm(c_mine, cond_w, 'nn', "cond_fwd")
    pre = jnp.sum(_exchange(pre_part, False, "gather_pre"), axis=0) + cond_b
    e_all = jax.nn.silu(pre)
    mod_part = jnp.concatenate([_mm(e_all, ada_w[i], 'nn', "ada_fwd%d" % i) for i in range(DEPTH)], axis=0)
    mod_all = _exchange(mod_part, False, "gather_mod")
    mod_all = mod_all.reshape(N_DEV, DEPTH, N_DEV, msh)
    mods = lax.dynamic_index_in_dim(mod_all, me, axis=2, keepdims=False)
    mods = mods.transpose(1, 0, 2).reshape(DEPTH, N_DEV * msh) + ada_b

    big_names = list(BIG)
    packed = _pack_rows([w[n] for n in big_names], BF16, BIG_ROW_ALIGN)
    gathered = _gather_two_level(packed, "gather_weights")
    blocks = _unpack_rows(gathered, [w[n].shape for n in big_names])
    p = {n: _join(b, BIG[n]) for n, b in zip(big_names, blocks)}
    for n in big_names:
        if not n.startswith('ffn_'):
            p[n] = p[n].astype(F32)
    sm_names = list(SMALL_SHARDED)
    sm_packed = _pack_rows([w[n] for n in sm_names], F32)
    sm_blocks = _unpack_rows(_exchange(sm_packed, False, "gather_small"), [w[n].shape for n in sm_names])
    for n, b in zip(sm_names, sm_blocks):
        p[n] = _join(b, SMALL_SHARDED[n])
    for n in SMALL_REPL:
        p[n] = w[n]
    n_qkvz = 4 * d
    p['gdn_w_qkvz'] = p['gdn_w_in'][:, :n_qkvz]
    p['gdn_w_ab'] = jnp.pad(p['gdn_w_in'][:, n_qkvz:], ((0, 0), (0, LANES - 2 * GDN_HEADS)))
    del p['gdn_w_in']

    loss_local, dx, dmods, dp = _trunk_grad(x, mods, p, target)
    dp['gdn_w_in'] = jnp.concatenate([dp.pop('gdn_w_qkvz'), dp.pop('gdn_w_ab')[:, :2 * GDN_HEADS]], axis=1)

    dmod_all = _exchange(dmods.reshape(-1, d), False, "gather_dmod").reshape(N_DEV, DEPTH, 6 * d)
    grads = {'ada_b': jnp.sum(dmod_all, axis=0)}
    dm_mine = lax.dynamic_slice_in_dim(dmod_all, me * msh, msh, axis=2)
    grads['ada_w'] = jnp.stack([_mm_outer(e_all, dm_mine[:, i], "ada_dw%d" % i) for i in range(DEPTH)])
    de_part = _mm(dm_mine[:, 0], ada_w[0], 'nt', "ada_de0")
    for i in range(1, DEPTH):
        de_part = de_part + _mm(dm_mine[:, i], ada_w[i], 'nt', "ada_de%d" % i)
    de_all = jnp.sum(_exchange(de_part, False, "gather_de"), axis=0)
    sig = jax.nn.sigmoid(pre)
    dpre = de_all * (sig * (1.0 + pre * (1.0 - sig)))
    grads['cond_b'] = jnp.sum(dpre, axis=0)
    grads['cond_w'] = _mm_outer(c_mine, dpre, "cond_dw")

    small_names = sm_names + SMALL_REPL
    small_packed = _pack_rows([loss_local.reshape(1)] + [dp[n] for n in small_names], F32)
    small_sum = _sum_slots(_exchange(small_packed, False, "gather_small_grads"), "sum_small_grads")
    small = _unpack_rows(small_sum, [(1,)] + [dp[n].shape for n in small_names])
    loss = small[0][0]
    for n, g in zip(small_names, small[1:]):
        if n in SMALL_SHARDED:
            ax = SMALL_SHARDED[n]
            g = lax.dynamic_slice_in_dim(g, me * w[n].shape[ax], w[n].shape[ax], axis=ax)
        grads[n] = g

    send = _pack_big_grads([dp[n] for n in big_names], [BIG[n] for n in big_names])
    shapes = [w[n].shape for n in big_names]
    outs = _adamw(_reduce_to_owner(send), *[_pack_rows([t[n] for n in big_names], F32, BIG_ROW_ALIGN) for t in (w, mom, var)],
                  "adamw_big")
    g_b, d_b, m_b, v_b = (_unpack_rows(o, shapes) for o in outs)
    delta, new_m, new_v = {}, {}, {}
    for j, n in enumerate(big_names):
        grads[n], delta[n], new_m[n], new_v[n] = g_b[j], d_b[j], m_b[j], v_b[j]

    rest = [n for n in WEIGHTS if n not in BIG]
    shapes = [w[n].shape for n in rest]
    outs = _adamw([(_pack_rows([grads[n] for n in rest], F32, BIG_ROW_ALIGN)[None], 0)],
                  *[_pack_rows([t[n] for n in rest], F32, BIG_ROW_ALIGN) for t in (w, mom, var)], "adamw_rest")
    _, d_r, m_r, v_r = (_unpack_rows(o, shapes) for o in outs)
    for j, n in enumerate(rest):
        delta[n], new_m[n], new_v[n] = d_r[j], m_r[j], v_r[j]

    return (loss, dx[None], *[grads[n] for n in WEIGHTS], *[delta[n] for n in WEIGHTS],
            *[new_m[n] for n in WEIGHTS], *[new_v[n] for n in WEIGHTS])
```
